```python
import jax, jax.numpy as jnp
from jax import lax
import numpy as np

D_MODEL = 1024
BATCH = 8
SEQ = 4096
DEPTH = 2

GRID_W = 64
CTX_LEN = 256
N_MIXERS = 2
EPS = 1e-6

SSD_EXPAND = 2
D_INNER = SSD_EXPAND * D_MODEL
SSD_HEADDIM = 64
SSD_HEADS = D_INNER // SSD_HEADDIM
SSD_GROUPS = 8
SSD_HPG = SSD_HEADS // SSD_GROUPS
SSD_STATE = 128
SSD_CONV_K = 5
SSD_CHUNK = 64
SSD_GN = SSD_GROUPS * SSD_STATE
SSD_CONV_DIM = D_INNER + 2 * SSD_GN
SSD_PROJ_DIM = D_INNER + SSD_CONV_DIM + 2 * SSD_HEADS

CONF_K = 31
CONF_DIM = D_MODEL
CONF_H = CONF_DIM // 2

FFN_HIDDEN = (((8 * D_MODEL + 2) // 3 + 255) // 256) * 256

N_SSD_LAYERS = (DEPTH + 1) // 2
N_CONF_LAYERS = DEPTH // 2

kernel_name = "hybrid_ssd_conformer_dit_trunk"


def rmsnorm(x, g):
    xf = x.astype(jnp.float32)
    y = xf * lax.rsqrt(jnp.mean(xf * xf, axis=-1, keepdims=True) + EPS)
    return (y * g.astype(jnp.float32)).astype(x.dtype)


def layernorm(x, g, b):
    xf = x.astype(jnp.float32)
    mu = jnp.mean(xf, axis=-1, keepdims=True)
    var = jnp.mean(jnp.square(xf - mu), axis=-1, keepdims=True)
    y = (xf - mu) * lax.rsqrt(var + EPS) * g.astype(jnp.float32) + b.astype(jnp.float32)
    return y.astype(x.dtype)


def modulate(h, shift, scale):
    return h * (1 + scale) + shift


def ada_params(sc, w, b):
    mod = sc @ w + b
    return [t[:, None, :] for t in jnp.split(mod, 6, axis=-1)]


def dwconv1d(x, w, b):
    k = w.shape[0]
    y = lax.conv_general_dilated(
        x, w[:, None, :].astype(x.dtype), window_strides=(1,),
        padding=[(k // 2, k // 2)], dimension_numbers=("NWC", "WIO", "NWC"),
        feature_group_count=x.shape[-1])
    return y + b


def axial_dwconv(u, rows, w, b):
    n, l, ch = u.shape
    g = u.reshape(n, rows, GRID_W, ch)
    hor = dwconv1d(g[..., :CONF_H].reshape(n * rows, GRID_W, CONF_H), w[:, :CONF_H], b[:CONF_H])
    hor = hor.reshape(n, rows, GRID_W, CONF_H)
    cv = ch - CONF_H
    ver_in = jnp.swapaxes(g[..., CONF_H:], 1, 2).reshape(n * GRID_W, rows, cv)
    ver = dwconv1d(ver_in, w[:, CONF_H:], b[CONF_H:]).reshape(n, GRID_W, rows, cv)
    ver = jnp.swapaxes(ver, 1, 2)
    return jnp.concatenate([hor, ver], axis=-1).reshape(n, l, ch)


def ssd_chunked(xh, dt, A, Bm, Cm, h0):
    n, l = xh.shape[:2]
    nc, q = l // SSD_CHUNK, SSD_CHUNK
    x = (xh * dt[..., None]).reshape(n, nc, q, SSD_GROUPS, SSD_HPG, SSD_HEADDIM)
    a_cum = jnp.cumsum((dt * A).reshape(n, nc, q, SSD_GROUPS, SSD_HPG), axis=2)
    Bc = Bm.reshape(n, nc, q, SSD_GROUPS, SSD_STATE)
    Cc = Cm.reshape(n, nc, q, SSD_GROUPS, SSD_STATE)
    idx = jnp.arange(q)
    lower = (idx[:, None] >= idx[None, :])[None, None, :, :, None, None]
    seg = a_cum[:, :, :, None] - a_cum[:, :, None, :]
    decay = jnp.exp(jnp.where(lower, seg, -jnp.inf))
    scores = jnp.einsum("bcqgn,bcsgn->bcqsg", Cc, Bc)
    y_diag = jnp.einsum("bcqsg,bcqsgj,bcsgjp->bcqgjp", scores, decay, x)
    decay_to_end = jnp.exp(a_cum[:, :, -1:] - a_cum)
    states = jnp.einsum("bcsgn,bcsgj,bcsgjp->bcgjpn", Bc, decay_to_end, x)
    chunk_decay = jnp.exp(a_cum[:, :, -1])

    def step(h, inp):
        st, dec = inp
        return h * dec[..., None, None] + st, h

    h_init = h0.reshape(n, SSD_GROUPS, SSD_HPG, SSD_HEADDIM, SSD_STATE)
    h_final, h_prev = lax.scan(step, h_init, (jnp.moveaxis(states, 1, 0), jnp.moveaxis(chunk_decay, 1, 0)))
    h_prev = jnp.moveaxis(h_prev, 0, 1)
    y_off = jnp.einsum("bcqgn,bcgjpn,bcqgj->bcqgjp", Cc, h_prev, jnp.exp(a_cum))
    y = (y_diag + y_off).reshape(n, l, SSD_HEADS, SSD_HEADDIM)
    return y, h_final.reshape(n, SSD_HEADS, SSD_HEADDIM, SSD_STATE)


def ssd_mixer(h, h0_f, h0_b, w_in, conv_w, conv_b, dt_bias_f, dt_bias_b, a_log_f, a_log_b,
              d_skip, norm_w, w_out):
    n, l, _ = h.shape
    f32 = jnp.float32
    zxbcdt = h @ w_in
    z = zxbcdt[..., :D_INNER].astype(f32)
    xbc = jax.nn.silu(dwconv1d(zxbcdt[..., D_INNER:D_INNER + SSD_CONV_DIM], conv_w, conv_b)).astype(f32)
    dt = zxbcdt[..., D_INNER + SSD_CONV_DIM:].astype(f32)
    xs = xbc[..., :D_INNER].reshape(n, l, SSD_HEADS, SSD_HEADDIM)
    Bm = xbc[..., D_INNER:D_INNER + SSD_GN].reshape(n, l, SSD_GROUPS, SSD_STATE)
    Cm = xbc[..., D_INNER + SSD_GN:].reshape(n, l, SSD_GROUPS, SSD_STATE)
    dt_f = jax.nn.softplus(dt[..., :SSD_HEADS] + dt_bias_f.astype(f32))
    dt_b = jax.nn.softplus(dt[..., SSD_HEADS:] + dt_bias_b.astype(f32))
    A_f = -jnp.exp(a_log_f.astype(f32))
    A_b = -jnp.exp(a_log_b.astype(f32))
    y_f, hf = ssd_chunked(xs, dt_f, A_f, Bm, Cm, h0_f)
    flip = lambda t: jnp.flip(t, axis=1)
    y_b, hb = ssd_chunked(flip(xs), flip(dt_b), A_b, flip(Bm), flip(Cm), h0_b)
    y = y_f + flip(y_b) + d_skip.astype(f32)[:, None] * xs
    y = rmsnorm(y.reshape(n, l, D_INNER) * jax.nn.silu(z), norm_w)
    return y.astype(h.dtype) @ w_out, hf, hb


def conformer_conv_module(h, rows, w_pw1, b_pw1, dw_w, dw_b, ln_g, ln_b, w_pw2, b_pw2):
    u = h @ w_pw1 + b_pw1
    u = u[..., :CONF_DIM] * jax.nn.sigmoid(u[..., CONF_DIM:])
    v = dwconv1d(u, dw_w, dw_b) if rows is None else axial_dwconv(u, rows, dw_w, dw_b)
    v = jax.nn.silu(layernorm(v, ln_g, ln_b))
    return v @ w_pw2 + b_pw2


def swiglu(h, w_in, w_out):
    u = h @ w_in
    return (jax.nn.silu(u[..., :FFN_HIDDEN]) * u[..., FFN_HIDDEN:]) @ w_out


def _fwd_setup_inputs(seed: int = 0) -> dict:
    key = jax.random.key(seed)
    ks = jax.random.split(key, 40)
    f32 = jnp.float32
    nrm = lambda k, shape, s: jax.random.normal(k, shape, f32) * s
    NS, NC, D = N_SSD_LAYERS, N_CONF_LAYERS, D_MODEL
    dt0 = jnp.exp(jax.random.uniform(ks[0], (2, NS, SSD_HEADS), f32,
                                     float(np.log(1e-3)), float(np.log(1e-1))))
    dt_bias = dt0 + jnp.log(-jnp.expm1(-dt0))
    a_log = jnp.log(jax.random.uniform(ks[1], (2, NS, SSD_HEADS), f32, 1.0, 16.0))
    return {
        "x": nrm(ks[2], (BATCH, SEQ, D), 1.0),
        "c": nrm(ks[3], (BATCH, D), 1.0),
        "ctx": nrm(ks[4], (BATCH, CTX_LEN, D), 1.0),
        "c_ctx": nrm(ks[5], (D,), 1.0),
        "ada_w": nrm(ks[6], (DEPTH, D, 6 * D), 0.5 * D ** -0.5),
        "ada_b": nrm(ks[7], (DEPTH, 6 * D), 0.01),
        "norm_mix_g": 1.0 + nrm(ks[8], (DEPTH, D), 0.02),
        "norm_ffn_g": 1.0 + nrm(ks[9], (DEPTH, D), 0.02),
        "final_norm_g": 1.0 + nrm(ks[10], (D,), 0.02),
        "ssd_w_in": nrm(ks[11], (NS, D, SSD_PROJ_DIM), D ** -0.5),
        "ssd_conv_w": nrm(ks[12], (NS, SSD_CONV_K, SSD_CONV_DIM), SSD_CONV_K ** -0.5),
        "ssd_conv_b": nrm(ks[13], (NS, SSD_CONV_DIM), 0.01),
        "ssd_dt_bias_f": dt_bias[0],
        "ssd_dt_bias_b": dt_bias[1],
        "ssd_a_log_f": a_log[0],
        "ssd_a_log_b": a_log[1],
        "ssd_d_skip": 1.0 + nrm(ks[14], (NS, SSD_HEADS), 0.02),
        "ssd_norm_w": 1.0 + nrm(ks[15], (NS, D_INNER), 0.02),
        "ssd_w_out": nrm(ks[16], (NS, D_INNER, D), D_INNER ** -0.5),
        "conf_w_pw1": nrm(ks[17], (NC, D, 2 * CONF_DIM), D ** -0.5),
        "conf_b_pw1": nrm(ks[18], (NC, 2 * CONF_DIM), 0.01),
        "conf_dw_w": nrm(ks[19], (NC, CONF_K, CONF_DIM), CONF_K ** -0.5),
        "conf_dw_b": nrm(ks[20], (NC, CONF_DIM), 0.01),
        "conf_ln_g": 1.0 + nrm(ks[21], (NC, CONF_DIM), 0.02),
        "conf_ln_b": nrm(ks[22], (NC, CONF_DIM), 0.01),
        "conf_w_pw2": nrm(ks[23], (NC, CONF_DIM, D), CONF_DIM ** -0.5),
        "conf_b_pw2": nrm(ks[24], (NC, D), 0.01),
        "ffn_w_in": nrm(ks[25], (DEPTH, D, 2 * FFN_HIDDEN), D ** -0.5),
        "ffn_w_out": nrm(ks[26], (DEPTH, FFN_HIDDEN, D), FFN_HIDDEN ** -0.5),
    }


def _fwd_reference(x, c, ctx, c_ctx, ada_w, ada_b, norm_mix_g, norm_ffn_g, final_norm_g,
              ssd_w_in, ssd_conv_w, ssd_conv_b, ssd_dt_bias_f, ssd_dt_bias_b, ssd_a_log_f,
              ssd_a_log_b, ssd_d_skip, ssd_norm_w, ssd_w_out,
              conf_w_pw1, conf_b_pw1, conf_dw_w, conf_dw_b, conf_ln_g, conf_ln_b,
              conf_w_pw2, conf_b_pw2, ffn_w_in, ffn_w_out):
    n, l, _ = x.shape
    rows = l // GRID_W
    h_lat, h_ctx = x, ctx
    sc_lat = jax.nn.silu(c)
    sc_ctx = jax.nn.silu(c_ctx)[None, :]
    for i in range(DEPTH):
        need_ctx_out = i < DEPTH - 1
        j = i // N_MIXERS
        sh1, s1, g1, sh2, s2, g2 = ada_params(sc_lat, ada_w[i], ada_b[i])
        xn_lat = modulate(rmsnorm(h_lat, norm_mix_g[i]), sh1, s1)
        if i % N_MIXERS == 0:
            p = (ssd_w_in[j], ssd_conv_w[j], ssd_conv_b[j], ssd_dt_bias_f[j], ssd_dt_bias_b[j],
                 ssd_a_log_f[j], ssd_a_log_b[j], ssd_d_skip[j], ssd_norm_w[j], ssd_w_out[j])
            csh1, cs1, cg1, csh2, cs2, cg2 = ada_params(sc_ctx, ada_w[i], ada_b[i])
            xn_ctx = modulate(rmsnorm(h_ctx, norm_mix_g[i]), csh1, cs1)
            h_zero = jnp.zeros((n, SSD_HEADS, SSD_HEADDIM, SSD_STATE), jnp.float32)
            y_ctx, hf_ctx, hb_ctx = ssd_mixer(xn_ctx, h_zero, h_zero, *p)
            y_lat, _, _ = ssd_mixer(xn_lat, hf_ctx, hb_ctx, *p)
            if need_ctx_out:
                h_ctx = h_ctx + cg1 * y_ctx
        else:
            p = (conf_w_pw1[j], conf_b_pw1[j], conf_dw_w[j], conf_dw_b[j], conf_ln_g[j],
                 conf_ln_b[j], conf_w_pw2[j], conf_b_pw2[j])
            y_lat = conformer_conv_module(xn_lat, rows, *p)
            if need_ctx_out:
                csh1, cs1, cg1, csh2, cs2, cg2 = ada_params(sc_ctx, ada_w[i], ada_b[i])
                xn_ctx = modulate(rmsnorm(h_ctx, norm_mix_g[i]), csh1, cs1)
                h_ctx = h_ctx + cg1 * conformer_conv_module(xn_ctx, None, *p)
        h_lat = h_lat + g1 * y_lat
        h_lat = h_lat + g2 * swiglu(modulate(rmsnorm(h_lat, norm_ffn_g[i]), sh2, s2), ffn_w_in[i], ffn_w_out[i])
        if need_ctx_out:
            h_ctx = h_ctx + cg2 * swiglu(modulate(rmsnorm(h_ctx, norm_ffn_g[i]), csh2, cs2),
                                         ffn_w_in[i], ffn_w_out[i])
    return rmsnorm(h_lat, final_norm_g)


import jax as _jax
import jax.numpy as _jnp

TWIN_FORMAT = 'train_step'
FWD_PARAMS = ['x', 'c', 'ctx', 'c_ctx', 'ada_w', 'ada_b', 'norm_mix_g', 'norm_ffn_g', 'final_norm_g', 'ssd_w_in', 'ssd_conv_w', 'ssd_conv_b', 'ssd_dt_bias_f', 'ssd_dt_bias_b', 'ssd_a_log_f', 'ssd_a_log_b', 'ssd_d_skip', 'ssd_norm_w', 'ssd_w_out', 'conf_w_pw1', 'conf_b_pw1', 'conf_dw_w', 'conf_dw_b', 'conf_ln_g', 'conf_ln_b', 'conf_w_pw2', 'conf_b_pw2', 'ffn_w_in', 'ffn_w_out']
TWIN_WEIGHTS = ['c_ctx', 'ada_w', 'ada_b', 'norm_mix_g', 'norm_ffn_g', 'final_norm_g', 'ssd_w_in', 'ssd_conv_w', 'ssd_conv_b', 'ssd_dt_bias_f', 'ssd_dt_bias_b', 'ssd_a_log_f', 'ssd_a_log_b', 'ssd_d_skip', 'ssd_norm_w', 'ssd_w_out', 'conf_w_pw1', 'conf_b_pw1', 'conf_dw_w', 'conf_dw_b', 'conf_ln_g', 'conf_ln_b', 'conf_w_pw2', 'conf_b_pw2', 'ffn_w_in', 'ffn_w_out']
TWIN_DIFF_INPUT = 'x'
TWIN_INPUTS = ['x', 'c', 'ctx', 'c_ctx', 'ada_w', 'ada_b', 'norm_mix_g', 'norm_ffn_g', 'final_norm_g', 'ssd_w_in', 'ssd_conv_w', 'ssd_conv_b', 'ssd_dt_bias_f', 'ssd_dt_bias_b', 'ssd_a_log_f', 'ssd_a_log_b', 'ssd_d_skip', 'ssd_norm_w', 'ssd_w_out', 'conf_w_pw1', 'conf_b_pw1', 'conf_dw_w', 'conf_dw_b', 'conf_ln_g', 'conf_ln_b', 'conf_w_pw2', 'conf_b_pw2', 'ffn_w_in', 'ffn_w_out', 'loss_target', 'm_c_ctx', 'm_ada_w', 'm_ada_b', 'm_norm_mix_g', 'm_norm_ffn_g', 'm_final_norm_g', 'm_ssd_w_in', 'm_ssd_conv_w', 'm_ssd_conv_b', 'm_ssd_dt_bias_f', 'm_ssd_dt_bias_b', 'm_ssd_a_log_f', 'm_ssd_a_log_b', 'm_ssd_d_skip', 'm_ssd_norm_w', 'm_ssd_w_out', 'm_conf_w_pw1', 'm_conf_b_pw1', 'm_conf_dw_w', 'm_conf_dw_b', 'm_conf_ln_g', 'm_conf_ln_b', 'm_conf_w_pw2', 'm_conf_b_pw2', 'm_ffn_w_in', 'm_ffn_w_out', 'v_c_ctx', 'v_ada_w', 'v_ada_b', 'v_norm_mix_g', 'v_norm_ffn_g', 'v_final_norm_g', 'v_ssd_w_in', 'v_ssd_conv_w', 'v_ssd_conv_b', 'v_ssd_dt_bias_f', 'v_ssd_dt_bias_b', 'v_ssd_a_log_f', 'v_ssd_a_log_b', 'v_ssd_d_skip', 'v_ssd_norm_w', 'v_ssd_w_out', 'v_conf_w_pw1', 'v_conf_b_pw1', 'v_conf_dw_w', 'v_conf_dw_b', 'v_conf_ln_g', 'v_conf_ln_b', 'v_conf_w_pw2', 'v_conf_b_pw2', 'v_ffn_w_in', 'v_ffn_w_out']
TWIN_OUTPUTS = ['loss', 'grad_x', 'grad_c_ctx', 'grad_ada_w', 'grad_ada_b', 'grad_norm_mix_g', 'grad_norm_ffn_g', 'grad_final_norm_g', 'grad_ssd_w_in', 'grad_ssd_conv_w', 'grad_ssd_conv_b', 'grad_ssd_dt_bias_f', 'grad_ssd_dt_bias_b', 'grad_ssd_a_log_f', 'grad_ssd_a_log_b', 'grad_ssd_d_skip', 'grad_ssd_norm_w', 'grad_ssd_w_out', 'grad_conf_w_pw1', 'grad_conf_b_pw1', 'grad_conf_dw_w', 'grad_conf_dw_b', 'grad_conf_ln_g', 'grad_conf_ln_b', 'grad_conf_w_pw2', 'grad_conf_b_pw2', 'grad_ffn_w_in', 'grad_ffn_w_out', 'delta_c_ctx', 'delta_ada_w', 'delta_ada_b', 'delta_norm_mix_g', 'delta_norm_ffn_g', 'delta_final_norm_g', 'delta_ssd_w_in', 'delta_ssd_conv_w', 'delta_ssd_conv_b', 'delta_ssd_dt_bias_f', 'delta_ssd_dt_bias_b', 'delta_ssd_a_log_f', 'delta_ssd_a_log_b', 'delta_ssd_d_skip', 'delta_ssd_norm_w', 'delta_ssd_w_out', 'delta_conf_w_pw1', 'delta_conf_b_pw1', 'delta_conf_dw_w', 'delta_conf_dw_b', 'delta_conf_ln_g', 'delta_conf_ln_b', 'delta_conf_w_pw2', 'delta_conf_b_pw2', 'delta_ffn_w_in', 'delta_ffn_w_out', 'new_m_c_ctx', 'new_m_ada_w', 'new_m_ada_b', 'new_m_norm_mix_g', 'new_m_norm_ffn_g', 'new_m_final_norm_g', 'new_m_ssd_w_in', 'new_m_ssd_conv_w', 'new_m_ssd_conv_b', 'new_m_ssd_dt_bias_f', 'new_m_ssd_dt_bias_b', 'new_m_ssd_a_log_f', 'new_m_ssd_a_log_b', 'new_m_ssd_d_skip', 'new_m_ssd_norm_w', 'new_m_ssd_w_out', 'new_m_conf_w_pw1', 'new_m_conf_b_pw1', 'new_m_conf_dw_w', 'new_m_conf_dw_b', 'new_m_conf_ln_g', 'new_m_conf_ln_b', 'new_m_conf_w_pw2', 'new_m_conf_b_pw2', 'new_m_ffn_w_in', 'new_m_ffn_w_out', 'new_v_c_ctx', 'new_v_ada_w', 'new_v_ada_b', 'new_v_norm_mix_g', 'new_v_norm_ffn_g', 'new_v_final_norm_g', 'new_v_ssd_w_in', 'new_v_ssd_conv_w', 'new_v_ssd_conv_b', 'new_v_ssd_dt_bias_f', 'new_v_ssd_dt_bias_b', 'new_v_ssd_a_log_f', 'new_v_ssd_a_log_b', 'new_v_ssd_d_skip', 'new_v_ssd_norm_w', 'new_v_ssd_w_out', 'new_v_conf_w_pw1', 'new_v_conf_b_pw1', 'new_v_conf_dw_w', 'new_v_conf_dw_b', 'new_v_conf_ln_g', 'new_v_conf_ln_b', 'new_v_conf_w_pw2', 'new_v_conf_b_pw2', 'new_v_ffn_w_in', 'new_v_ffn_w_out']
TWIN_LEAF_KINDS = {'loss': 'loss', 'grad_x': 'grad_x', 'grad_c_ctx': 'grad_w', 'grad_ada_w': 'grad_w', 'grad_ada_b': 'grad_w', 'grad_norm_mix_g': 'grad_w', 'grad_norm_ffn_g': 'grad_w', 'grad_final_norm_g': 'grad_w', 'grad_ssd_w_in': 'grad_w', 'grad_ssd_conv_w': 'grad_w', 'grad_ssd_conv_b': 'grad_w', 'grad_ssd_dt_bias_f': 'grad_w', 'grad_ssd_dt_bias_b': 'grad_w', 'grad_ssd_a_log_f': 'grad_w', 'grad_ssd_a_log_b': 'grad_w', 'grad_ssd_d_skip': 'grad_w', 'grad_ssd_norm_w': 'grad_w', 'grad_ssd_w_out': 'grad_w', 'grad_conf_w_pw1': 'grad_w', 'grad_conf_b_pw1': 'grad_w', 'grad_conf_dw_w': 'grad_w', 'grad_conf_dw_b': 'grad_w', 'grad_conf_ln_g': 'grad_w', 'grad_conf_ln_b': 'grad_w', 'grad_conf_w_pw2': 'grad_w', 'grad_conf_b_pw2': 'grad_w', 'grad_ffn_w_in': 'grad_w', 'grad_ffn_w_out': 'grad_w', 'delta_c_ctx': 'delta_w', 'delta_ada_w': 'delta_w', 'delta_ada_b': 'delta_w', 'delta_norm_mix_g': 'delta_w', 'delta_norm_ffn_g': 'delta_w', 'delta_final_norm_g': 'delta_w', 'delta_ssd_w_in': 'delta_w', 'delta_ssd_conv_w': 'delta_w', 'delta_ssd_conv_b': 'delta_w', 'delta_ssd_dt_bias_f': 'delta_w', 'delta_ssd_dt_bias_b': 'delta_w', 'delta_ssd_a_log_f': 'delta_w', 'delta_ssd_a_log_b': 'delta_w', 'delta_ssd_d_skip': 'delta_w', 'delta_ssd_norm_w': 'delta_w', 'delta_ssd_w_out': 'delta_w', 'delta_conf_w_pw1': 'delta_w', 'delta_conf_b_pw1': 'delta_w', 'delta_conf_dw_w': 'delta_w', 'delta_conf_dw_b': 'delta_w', 'delta_conf_ln_g': 'delta_w', 'delta_conf_ln_b': 'delta_w', 'delta_conf_w_pw2': 'delta_w', 'delta_conf_b_pw2': 'delta_w', 'delta_ffn_w_in': 'delta_w', 'delta_ffn_w_out': 'delta_w', 'new_m_c_ctx': 'new_m', 'new_m_ada_w': 'new_m', 'new_m_ada_b': 'new_m', 'new_m_norm_mix_g': 'new_m', 'new_m_norm_ffn_g': 'new_m', 'new_m_final_norm_g': 'new_m', 'new_m_ssd_w_in': 'new_m', 'new_m_ssd_conv_w': 'new_m', 'new_m_ssd_conv_b': 'new_m', 'new_m_ssd_dt_bias_f': 'new_m', 'new_m_ssd_dt_bias_b': 'new_m', 'new_m_ssd_a_log_f': 'new_m', 'new_m_ssd_a_log_b': 'new_m', 'new_m_ssd_d_skip': 'new_m', 'new_m_ssd_norm_w': 'new_m', 'new_m_ssd_w_out': 'new_m', 'new_m_conf_w_pw1': 'new_m', 'new_m_conf_b_pw1': 'new_m', 'new_m_conf_dw_w': 'new_m', 'new_m_conf_dw_b': 'new_m', 'new_m_conf_ln_g': 'new_m', 'new_m_conf_ln_b': 'new_m', 'new_m_conf_w_pw2': 'new_m', 'new_m_conf_b_pw2': 'new_m', 'new_m_ffn_w_in': 'new_m', 'new_m_ffn_w_out': 'new_m', 'new_v_c_ctx': 'new_v', 'new_v_ada_w': 'new_v', 'new_v_ada_b': 'new_v', 'new_v_norm_mix_g': 'new_v', 'new_v_norm_ffn_g': 'new_v', 'new_v_final_norm_g': 'new_v', 'new_v_ssd_w_in': 'new_v', 'new_v_ssd_conv_w': 'new_v', 'new_v_ssd_conv_b': 'new_v', 'new_v_ssd_dt_bias_f': 'new_v', 'new_v_ssd_dt_bias_b': 'new_v', 'new_v_ssd_a_log_f': 'new_v', 'new_v_ssd_a_log_b': 'new_v', 'new_v_ssd_d_skip': 'new_v', 'new_v_ssd_norm_w': 'new_v', 'new_v_ssd_w_out': 'new_v', 'new_v_conf_w_pw1': 'new_v', 'new_v_conf_b_pw1': 'new_v', 'new_v_conf_dw_w': 'new_v', 'new_v_conf_dw_b': 'new_v', 'new_v_conf_ln_g': 'new_v', 'new_v_conf_ln_b': 'new_v', 'new_v_conf_w_pw2': 'new_v', 'new_v_conf_b_pw2': 'new_v', 'new_v_ffn_w_in': 'new_v', 'new_v_ffn_w_out': 'new_v'}


def _forward(args):
    return _fwd_reference(*[args[k] for k in FWD_PARAMS])


def _output_shape():
    out = _jax.eval_shape(lambda: _forward(_fwd_setup_inputs(0)))
    return out.shape, out.dtype

N_MICROBATCH = 1
ADAM_LR = 0.001
ADAM_B1 = 0.9
ADAM_B2 = 0.999
ADAM_EPS = 1e-08
ADAM_WD = 0.01
ADAM_STEP = 10
PER_EXAMPLE_BATCH_AXIS = {'x': 0, 'c': 0, 'ctx': 0, 'loss_target': 0}
SHARED_INPUTS = []
_WEIGHT_DTYPES = {'c_ctx': _jnp.float32, 'ada_w': _jnp.float32, 'ada_b': _jnp.float32, 'norm_mix_g': _jnp.float32, 'norm_ffn_g': _jnp.float32, 'final_norm_g': _jnp.float32, 'ssd_w_in': _jnp.float32, 'ssd_conv_w': _jnp.float32, 'ssd_conv_b': _jnp.float32, 'ssd_dt_bias_f': _jnp.float32, 'ssd_dt_bias_b': _jnp.float32, 'ssd_a_log_f': _jnp.float32, 'ssd_a_log_b': _jnp.float32, 'ssd_d_skip': _jnp.float32, 'ssd_norm_w': _jnp.float32, 'ssd_w_out': _jnp.float32, 'conf_w_pw1': _jnp.float32, 'conf_b_pw1': _jnp.float32, 'conf_dw_w': _jnp.float32, 'conf_dw_b': _jnp.float32, 'conf_ln_g': _jnp.float32, 'conf_ln_b': _jnp.float32, 'conf_w_pw2': _jnp.float32, 'conf_b_pw2': _jnp.float32, 'ffn_w_in': _jnp.float32, 'ffn_w_out': _jnp.float32}
MOMENT_SCALE = {'c_ctx': 4.917115e-03, 'ada_w': 5.507840e-02, 'ada_b': 9.371155e-02, 'norm_mix_g': 5.793781e-02, 'norm_ffn_g': 4.959912e-02, 'final_norm_g': 3.202127e+01, 'ssd_w_in': 3.214301e-02, 'ssd_conv_w': 2.866871e-02, 'ssd_conv_b': 4.072386e-02, 'ssd_dt_bias_f': 8.855089e-02, 'ssd_dt_bias_b': 8.836859e-02, 'ssd_a_log_f': 1.024480e-01, 'ssd_a_log_b': 1.172421e-01, 'ssd_d_skip': 1.418548e-01, 'ssd_norm_w': 4.093553e-02, 'ssd_w_out': 5.272377e-02, 'conf_w_pw1': 2.385739e-02, 'conf_b_pw1': 2.265677e-02, 'conf_dw_w': 3.141102e-02, 'conf_dw_b': 7.017817e-02, 'conf_ln_g': 3.572045e-02, 'conf_ln_b': 3.643798e-02, 'conf_w_pw2': 3.022763e-02, 'conf_b_pw2': 5.380916e-02, 'ffn_w_in': 2.202673e-02, 'ffn_w_out': 3.603207e-02}


def _to_microbatches(a, axis):
    t = _jnp.moveaxis(a, axis, 0)
    t = t.reshape((N_MICROBATCH, t.shape[0] // N_MICROBATCH) + t.shape[1:])
    return _jnp.moveaxis(t, 1, axis + 1)


def setup_inputs(seed: int = 0) -> dict:
    inp = _fwd_setup_inputs(seed)
    key = _jax.random.fold_in(_jax.random.key(seed), 7919)
    shape, _ = _output_shape()
    out = dict(inp)
    out["loss_target"] = _jax.random.normal(_jax.random.fold_in(key, 0), shape, _jnp.float32)
    for i, name in enumerate(TWIN_WEIGHTS):
        w = inp[name].astype(_jnp.float32)
        if MOMENT_SCALE is None:
            s = _jnp.sqrt(_jnp.mean(_jnp.square(w)) + 1e-30)
        else:
            s = MOMENT_SCALE[name]
        km, kv = _jax.random.split(_jax.random.fold_in(key, i + 1))
        out[name] = w
        out["m_" + name] = s * _jax.random.normal(km, w.shape, _jnp.float32)
        out["v_" + name] = (s * s) * _jax.random.uniform(kv, w.shape, _jnp.float32, 0.5, 1.5)
    if N_MICROBATCH > 1:
        for name, axis in PER_EXAMPLE_BATCH_AXIS.items():
            out[name] = _to_microbatches(out[name], axis)
    return {'x': out['x'], 'c': out['c'], 'ctx': out['ctx'], 'c_ctx': out['c_ctx'], 'ada_w': out['ada_w'], 'ada_b': out['ada_b'], 'norm_mix_g': out['norm_mix_g'], 'norm_ffn_g': out['norm_ffn_g'], 'final_norm_g': out['final_norm_g'], 'ssd_w_in': out['ssd_w_in'], 'ssd_conv_w': out['ssd_conv_w'], 'ssd_conv_b': out['ssd_conv_b'], 'ssd_dt_bias_f': out['ssd_dt_bias_f'], 'ssd_dt_bias_b': out['ssd_dt_bias_b'], 'ssd_a_log_f': out['ssd_a_log_f'], 'ssd_a_log_b': out['ssd_a_log_b'], 'ssd_d_skip': out['ssd_d_skip'], 'ssd_norm_w': out['ssd_norm_w'], 'ssd_w_out': out['ssd_w_out'], 'conf_w_pw1': out['conf_w_pw1'], 'conf_b_pw1': out['conf_b_pw1'], 'conf_dw_w': out['conf_dw_w'], 'conf_dw_b': out['conf_dw_b'], 'conf_ln_g': out['conf_ln_g'], 'conf_ln_b': out['conf_ln_b'], 'conf_w_pw2': out['conf_w_pw2'], 'conf_b_pw2': out['conf_b_pw2'], 'ffn_w_in': out['ffn_w_in'], 'ffn_w_out': out['ffn_w_out'], 'loss_target': out['loss_target'], 'm_c_ctx': out['m_c_ctx'], 'm_ada_w': out['m_ada_w'], 'm_ada_b': out['m_ada_b'], 'm_norm_mix_g': out['m_norm_mix_g'], 'm_norm_ffn_g': out['m_norm_ffn_g'], 'm_final_norm_g': out['m_final_norm_g'], 'm_ssd_w_in': out['m_ssd_w_in'], 'm_ssd_conv_w': out['m_ssd_conv_w'], 'm_ssd_conv_b': out['m_ssd_conv_b'], 'm_ssd_dt_bias_f': out['m_ssd_dt_bias_f'], 'm_ssd_dt_bias_b': out['m_ssd_dt_bias_b'], 'm_ssd_a_log_f': out['m_ssd_a_log_f'], 'm_ssd_a_log_b': out['m_ssd_a_log_b'], 'm_ssd_d_skip': out['m_ssd_d_skip'], 'm_ssd_norm_w': out['m_ssd_norm_w'], 'm_ssd_w_out': out['m_ssd_w_out'], 'm_conf_w_pw1': out['m_conf_w_pw1'], 'm_conf_b_pw1': out['m_conf_b_pw1'], 'm_conf_dw_w': out['m_conf_dw_w'], 'm_conf_dw_b': out['m_conf_dw_b'], 'm_conf_ln_g': out['m_conf_ln_g'], 'm_conf_ln_b': out['m_conf_ln_b'], 'm_conf_w_pw2': out['m_conf_w_pw2'], 'm_conf_b_pw2': out['m_conf_b_pw2'], 'm_ffn_w_in': out['m_ffn_w_in'], 'm_ffn_w_out': out['m_ffn_w_out'], 'v_c_ctx': out['v_c_ctx'], 'v_ada_w': out['v_ada_w'], 'v_ada_b': out['v_ada_b'], 'v_norm_mix_g': out['v_norm_mix_g'], 'v_norm_ffn_g': out['v_norm_ffn_g'], 'v_final_norm_g': out['v_final_norm_g'], 'v_ssd_w_in': out['v_ssd_w_in'], 'v_ssd_conv_w': out['v_ssd_conv_w'], 'v_ssd_conv_b': out['v_ssd_conv_b'], 'v_ssd_dt_bias_f': out['v_ssd_dt_bias_f'], 'v_ssd_dt_bias_b': out['v_ssd_dt_bias_b'], 'v_ssd_a_log_f': out['v_ssd_a_log_f'], 'v_ssd_a_log_b': out['v_ssd_a_log_b'], 'v_ssd_d_skip': out['v_ssd_d_skip'], 'v_ssd_norm_w': out['v_ssd_norm_w'], 'v_ssd_w_out': out['v_ssd_w_out'], 'v_conf_w_pw1': out['v_conf_w_pw1'], 'v_conf_b_pw1': out['v_conf_b_pw1'], 'v_conf_dw_w': out['v_conf_dw_w'], 'v_conf_dw_b': out['v_conf_dw_b'], 'v_conf_ln_g': out['v_conf_ln_g'], 'v_conf_ln_b': out['v_conf_ln_b'], 'v_conf_w_pw2': out['v_conf_w_pw2'], 'v_conf_b_pw2': out['v_conf_b_pw2'], 'v_ffn_w_in': out['v_ffn_w_in'], 'v_ffn_w_out': out['v_ffn_w_out']}


def _loss(weights, diff, rest, loss_target):
    with _jax.named_scope("forward"):
        args = {**rest, TWIN_DIFF_INPUT: diff, **{k: w.astype(_WEIGHT_DTYPES[k]) for k, w in weights.items()}}
        y = _forward(args)
    with _jax.named_scope("loss_head"):
        err = _jnp.square(y.astype(_jnp.float32) - loss_target)
        return 0.5 * _jnp.sum(_jnp.mean(err, axis=-1)) if err.ndim else 0.5 * err


def _adamw(w, g, m, v):
    m = ADAM_B1 * m + (1.0 - ADAM_B1) * g
    v = ADAM_B2 * v + (1.0 - ADAM_B2) * _jnp.square(g)
    m_hat = m / (1.0 - ADAM_B1 ** ADAM_STEP)
    v_hat = v / (1.0 - ADAM_B2 ** ADAM_STEP)
    delta = -ADAM_LR * (m_hat / (_jnp.sqrt(v_hat) + ADAM_EPS) + ADAM_WD * w)
    return delta, m, v


def reference(x, c, ctx, c_ctx, ada_w, ada_b, norm_mix_g, norm_ffn_g, final_norm_g, ssd_w_in, ssd_conv_w, ssd_conv_b, ssd_dt_bias_f, ssd_dt_bias_b, ssd_a_log_f, ssd_a_log_b, ssd_d_skip, ssd_norm_w, ssd_w_out, conf_w_pw1, conf_b_pw1, conf_dw_w, conf_dw_b, conf_ln_g, conf_ln_b, conf_w_pw2, conf_b_pw2, ffn_w_in, ffn_w_out, loss_target, m_c_ctx, m_ada_w, m_ada_b, m_norm_mix_g, m_norm_ffn_g, m_final_norm_g, m_ssd_w_in, m_ssd_conv_w, m_ssd_conv_b, m_ssd_dt_bias_f, m_ssd_dt_bias_b, m_ssd_a_log_f, m_ssd_a_log_b, m_ssd_d_skip, m_ssd_norm_w, m_ssd_w_out, m_conf_w_pw1, m_conf_b_pw1, m_conf_dw_w, m_conf_dw_b, m_conf_ln_g, m_conf_ln_b, m_conf_w_pw2, m_conf_b_pw2, m_ffn_w_in, m_ffn_w_out, v_c_ctx, v_ada_w, v_ada_b, v_norm_mix_g, v_norm_ffn_g, v_final_norm_g, v_ssd_w_in, v_ssd_conv_w, v_ssd_conv_b, v_ssd_dt_bias_f, v_ssd_dt_bias_b, v_ssd_a_log_f, v_ssd_a_log_b, v_ssd_d_skip, v_ssd_norm_w, v_ssd_w_out, v_conf_w_pw1, v_conf_b_pw1, v_conf_dw_w, v_conf_dw_b, v_conf_ln_g, v_conf_ln_b, v_conf_w_pw2, v_conf_b_pw2, v_ffn_w_in, v_ffn_w_out):
    given = dict(x=x, c=c, ctx=ctx, c_ctx=c_ctx, ada_w=ada_w, ada_b=ada_b, norm_mix_g=norm_mix_g, norm_ffn_g=norm_ffn_g, final_norm_g=final_norm_g, ssd_w_in=ssd_w_in, ssd_conv_w=ssd_conv_w, ssd_conv_b=ssd_conv_b, ssd_dt_bias_f=ssd_dt_bias_f, ssd_dt_bias_b=ssd_dt_bias_b, ssd_a_log_f=ssd_a_log_f, ssd_a_log_b=ssd_a_log_b, ssd_d_skip=ssd_d_skip, ssd_norm_w=ssd_norm_w, ssd_w_out=ssd_w_out, conf_w_pw1=conf_w_pw1, conf_b_pw1=conf_b_pw1, conf_dw_w=conf_dw_w, conf_dw_b=conf_dw_b, conf_ln_g=conf_ln_g, conf_ln_b=conf_ln_b, conf_w_pw2=conf_w_pw2, conf_b_pw2=conf_b_pw2, ffn_w_in=ffn_w_in, ffn_w_out=ffn_w_out, loss_target=loss_target, m_c_ctx=m_c_ctx, m_ada_w=m_ada_w, m_ada_b=m_ada_b, m_norm_mix_g=m_norm_mix_g, m_norm_ffn_g=m_norm_ffn_g, m_final_norm_g=m_final_norm_g, m_ssd_w_in=m_ssd_w_in, m_ssd_conv_w=m_ssd_conv_w, m_ssd_conv_b=m_ssd_conv_b, m_ssd_dt_bias_f=m_ssd_dt_bias_f, m_ssd_dt_bias_b=m_ssd_dt_bias_b, m_ssd_a_log_f=m_ssd_a_log_f, m_ssd_a_log_b=m_ssd_a_log_b, m_ssd_d_skip=m_ssd_d_skip, m_ssd_norm_w=m_ssd_norm_w, m_ssd_w_out=m_ssd_w_out, m_conf_w_pw1=m_conf_w_pw1, m_conf_b_pw1=m_conf_b_pw1, m_conf_dw_w=m_conf_dw_w, m_conf_dw_b=m_conf_dw_b, m_conf_ln_g=m_conf_ln_g, m_conf_ln_b=m_conf_ln_b, m_conf_w_pw2=m_conf_w_pw2, m_conf_b_pw2=m_conf_b_pw2, m_ffn_w_in=m_ffn_w_in, m_ffn_w_out=m_ffn_w_out, v_c_ctx=v_c_ctx, v_ada_w=v_ada_w, v_ada_b=v_ada_b, v_norm_mix_g=v_norm_mix_g, v_norm_ffn_g=v_norm_ffn_g, v_final_norm_g=v_final_norm_g, v_ssd_w_in=v_ssd_w_in, v_ssd_conv_w=v_ssd_conv_w, v_ssd_conv_b=v_ssd_conv_b, v_ssd_dt_bias_f=v_ssd_dt_bias_f, v_ssd_dt_bias_b=v_ssd_dt_bias_b, v_ssd_a_log_f=v_ssd_a_log_f, v_ssd_a_log_b=v_ssd_a_log_b, v_ssd_d_skip=v_ssd_d_skip, v_ssd_norm_w=v_ssd_norm_w, v_ssd_w_out=v_ssd_w_out, v_conf_w_pw1=v_conf_w_pw1, v_conf_b_pw1=v_conf_b_pw1, v_conf_dw_w=v_conf_dw_w, v_conf_dw_b=v_conf_dw_b, v_conf_ln_g=v_conf_ln_g, v_conf_ln_b=v_conf_ln_b, v_conf_w_pw2=v_conf_w_pw2, v_conf_b_pw2=v_conf_b_pw2, v_ffn_w_in=v_ffn_w_in, v_ffn_w_out=v_ffn_w_out)
    weights = {n: given[n] for n in TWIN_WEIGHTS}
    shared = {n: given[n] for n in SHARED_INPUTS}
    per_example = {n: given[n] for n in ['x', 'c', 'ctx']}
    grad_fn = _jax.value_and_grad(_loss, argnums=(0, 1))

    def one_microbatch(ex, loss_target):
        ex = dict(ex)
        diff = ex.pop(TWIN_DIFF_INPUT)
        return grad_fn(weights, diff, {**shared, **ex}, loss_target)

    if N_MICROBATCH == 1:
        loss, (grad_w, grad_x) = one_microbatch(per_example, given["loss_target"])
    else:
        def body(carry, xs):
            loss_sum, grad_sum = carry
            l_k, (gw_k, gx_k) = one_microbatch(xs[0], xs[1])
            with _jax.named_scope("update"):
                return (loss_sum + l_k, _jax.tree.map(_jnp.add, grad_sum, gw_k)), gx_k

        init = (_jnp.zeros((), _jnp.float32), _jax.tree.map(_jnp.zeros_like, weights))
        (loss, grad_w), grad_x = _jax.lax.scan(body, init, (per_example, given["loss_target"]))
    with _jax.named_scope("update"):
        delta_w, new_m, new_v = {}, {}, {}
        for n in TWIN_WEIGHTS:
            delta_w[n], new_m[n], new_v[n] = _adamw(weights[n], grad_w[n], given["m_" + n], given["v_" + n])
    return (loss, grad_x, *[grad_w[n] for n in TWIN_WEIGHTS], *[delta_w[n] for n in TWIN_WEIGHTS],
            *[new_m[n] for n in TWIN_WEIGHTS], *[new_v[n] for n in TWIN_WEIGHTS])
```

```python
import functools

import jax
import jax.numpy as jnp
from jax import lax
from jax.experimental import pallas as pl
from jax.experimental.pallas import tpu as pltpu

F32 = jnp.float32
MXU = jnp.bfloat16
WIRE = jnp.bfloat16
NDEV = 8
AXES = ("x", "y", "c")
SSD_STATE = 128
SSD_CHUNK = 64
GRID_W = 64
EPS = 1e-6
ROW_TILE = 256
LANES = 128
ADAM_LR, ADAM_B1, ADAM_B2, ADAM_EPS, ADAM_WD, ADAM_STEP = 0.001, 0.9, 0.999, 1e-08, 0.01, 10
VMEM_CAP = 56 * 2 ** 20
MESH_ID = pl.DeviceIdType.MESH


def _pick(dim, cands):
    for c in cands:
        if dim % c == 0:
            return c
    return dim


def _nbytes(shape, dtype):
    n = 1
    for s in shape:
        n *= s
    return n * jnp.dtype(dtype).itemsize


def _vmem(nbytes):
    return int(min(VMEM_CAP, max(24 * 2 ** 20, 2 * nbytes + 8 * 2 ** 20)))


def _sigmoid(x):
    return 1.0 / (1.0 + jnp.exp(-x))


def _silu(x):
    return x * _sigmoid(x)


def _dsilu(x):
    s = _sigmoid(x)
    return s * (1.0 + x * (1.0 - s))


def _softplus(x):
    return jnp.maximum(x, 0.0) + jnp.log(1.0 + jnp.exp(-jnp.abs(x)))


def _dot(a, b, dims):
    return lax.dot_general(a.astype(MXU), b.astype(MXU), (dims, ((), ())), preferred_element_type=F32)


NN, NT, TN = ((1,), (0,)), ((1,), (1,)), ((0,), (0,))


def _split(a, parts):
    out = []
    for _ in range(parts):
        p = a.astype(MXU)
        out.append(p)
        a = a - p.astype(F32)
    return out


def _dot_lx(e, a, dims, parts=2):
    return sum(lax.dot_general(e, p, (dims, ((), ())), preferred_element_type=F32) for p in _split(a, parts))


def _dot_rx(a, e, dims, parts=2):
    return sum(lax.dot_general(p, e, (dims, ((), ())), preferred_element_type=F32) for p in _split(a, parts))


def _exchange(name, gather, scatter=()):
    gather, scatter = list(gather), list(scatter)
    ng, ns = len(gather), len(scatter)
    n = ng + ns

    def body(*refs):
        ins, outs = refs[:n], refs[n:2 * n]
        send, recv, loc = refs[2 * n:]
        x, y, c = lax.axis_index("x"), lax.axis_index("y"), lax.axis_index("c")
        me, sib = (x, y, c), (x, y, 1 - c)
        chips = [(1 - x, y), (x, 1 - y), (1 - x, 1 - y)]

        def slot(p):
            return 4 * p[0] + 2 * p[1] + p[2]

        def rcopy(a, k, src, dst, to):
            return pltpu.make_async_remote_copy(src_ref=src, dst_ref=dst, send_sem=send.at[a, k], recv_sem=recv.at[a, k],
                                                device_id=to, device_id_type=MESH_ID)

        started, local = [], []
        for a in range(n):
            mine = pltpu.make_async_copy(ins[a] if a < ng else ins[a].at[slot(me)], outs[a].at[slot(me)], loc.at[a])
            mine.start()
            local.append(mine)
        rel = [(fx, fy, fc) for fx in (0, 1) for fy in (0, 1) for fc in (0, 1)][1:]
        for a in range(ng, n):
            for k, (fx, fy, fc) in enumerate(rel):
                p = (1 - x if fx else x, 1 - y if fy else y, 1 - c if fc else c)
                cp = rcopy(a, k, ins[a].at[slot(p)], outs[a].at[slot(me)], p)
                cp.start()
                started.append(cp)
        for a in range(ng):
            dst = outs[a].at[slot(me)]
            first = [rcopy(a, 0, ins[a], dst, sib)] + [rcopy(a, 1 + j, ins[a], dst, (*ch, c)) for j, ch in enumerate(chips)]
            for cp in first:
                cp.start()
            started += first
        for a in range(ng):
            for j, ch in enumerate(chips):
                blk = outs[a].at[slot((*ch, c))]
                rcopy(a, 1 + j, blk, blk, me).wait_recv()
                cp = rcopy(a, 4 + j, blk, blk, sib)
                cp.start()
                started.append(cp)
        for a in range(ng):
            blk = outs[a].at[slot(sib)]
            rcopy(a, 0, blk, blk, me).wait_recv()
            for j, ch in enumerate(chips):
                blk = outs[a].at[slot((*ch, 1 - c))]
                rcopy(a, 4 + j, blk, blk, me).wait_recv()
        for a in range(ng, n):
            for k, (fx, fy, fc) in enumerate(rel):
                p = (1 - x if fx else x, 1 - y if fy else y, 1 - c if fc else c)
                blk = outs[a].at[slot(p)]
                rcopy(a, k, blk, blk, me).wait_recv()
        for cp in started:
            cp.wait_send()
        for cp in local:
            cp.wait()

    out_shape = [jax.ShapeDtypeStruct((NDEV,) + a.shape, a.dtype) for a in gather]
    out_shape += [jax.ShapeDtypeStruct(a.shape, a.dtype) for a in scatter]
    anyspec = pl.BlockSpec(memory_space=pl.ANY)
    res = pl.pallas_call(
        body, name=name, out_shape=out_shape, in_specs=[anyspec] * n, out_specs=[anyspec] * n,
        scratch_shapes=[pltpu.SemaphoreType.DMA((n, 7)), pltpu.SemaphoreType.DMA((n, 7)), pltpu.SemaphoreType.DMA((n,))],
    )(*gather, *scatter)
    return res[:ng], res[ng:]


def _mm(name, a, b, mode, out_dtype=F32, bias=None, add=None):
    if mode == "nn":
        (m, k), (k2, n) = a.shape, b.shape
    elif mode == "nt":
        (m, k), (n, k2) = a.shape, b.shape
    else:
        (k, m), (k2, n) = a.shape, b.shape
    assert k == k2, (name, a.shape, b.shape)
    tm, tn, tk = _pick(m, (512, 256, 128)), _pick(n, (512, 256, 128)), _pick(k, (1024, 512, 256))
    nk = k // tk
    dims = {"nn": NN, "nt": NT, "tn": TN}[mode]
    a_spec = pl.BlockSpec((tk, tm), lambda i, j, kk: (kk, i)) if mode == "tn" else pl.BlockSpec((tm, tk), lambda i, j, kk: (i, kk))
    b_spec = pl.BlockSpec((tn, tk), lambda i, j, kk: (j, kk)) if mode == "nt" else pl.BlockSpec((tk, tn), lambda i, j, kk: (kk, j))
    extra, extra_specs = [], []
    if bias is not None:
        extra.append(bias)
        extra_specs.append(pl.BlockSpec((1, tn), lambda i, j, kk: (0, j)))
    if add is not None:
        extra.append(add)
        extra_specs.append(pl.BlockSpec((tm, tn), lambda i, j, kk: (i, j)))

    def body(*refs):
        a_ref, b_ref = refs[:2]
        o_ref, acc = refs[-2:]
        kk = pl.program_id(2)

        @pl.when(kk == 0)
        def _():
            acc[...] = jnp.zeros_like(acc)

        acc[...] += _dot(a_ref[...], b_ref[...], dims)

        @pl.when(kk == nk - 1)
        def _():
            r = acc[...]
            for e in refs[2:-2]:
                r = r + e[...].astype(F32)
            o_ref[...] = r.astype(o_ref.dtype)

    est = tm * tk * a.dtype.itemsize + tk * tn * b.dtype.itemsize + tm * tn * (8 + jnp.dtype(out_dtype).itemsize)
    return pl.pallas_call(
        body, name=name, grid=(m // tm, n // tn, nk), out_shape=jax.ShapeDtypeStruct((m, n), out_dtype),
        in_specs=[a_spec, b_spec] + extra_specs, out_specs=pl.BlockSpec((tm, tn), lambda i, j, kk: (i, j)),
        scratch_shapes=[pltpu.VMEM((tm, tn), F32)],
        compiler_params=pltpu.CompilerParams(dimension_semantics=("parallel", "parallel", "arbitrary"), vmem_limit_bytes=_vmem(est)),
    )(a, b, *extra)


def _ri(arr, w=None, cb=0, ro=0, lead=None):
    return (arr, arr.shape[-1] if w is None else w, cb, ro, lead)


def _rowwise(name, fn, nrows, row_ins, bc_ins, outs, accs=()):
    tr = min(ROW_TILE, nrows)
    assert nrows % tr == 0
    in_specs = []
    for (arr, w, cb, ro, lead) in row_ins:
        if lead is None:
            in_specs.append(pl.BlockSpec((tr, w), lambda i, cb=cb, ro=ro: (jnp.maximum(i + ro, 0), cb)))
        else:
            in_specs.append(pl.BlockSpec((None, tr, w), lambda i, cb=cb, ro=ro, lead=lead: (lead, jnp.maximum(i + ro, 0), cb)))
    for arr in bc_ins:
        in_specs.append(pl.BlockSpec(arr.shape, lambda i, nd=arr.ndim: (0,) * nd))
    out_shape = [jax.ShapeDtypeStruct((nrows, c), dt) for c, dt in outs] + [jax.ShapeDtypeStruct(s, F32) for s in accs]
    out_specs = [pl.BlockSpec((tr, c), lambda i: (i, 0)) for c, _ in outs] + [pl.BlockSpec(s, lambda i: (0, 0)) for s in accs]
    nr, nb, no = len(row_ins), len(bc_ins), len(outs)

    def body(*refs):
        i = pl.program_id(0)
        rows = [r[...] for r in refs[:nr]]
        bcs = [r[...] for r in refs[nr:nr + nb]]
        o, a = fn(rows, bcs, i)
        for ref, val in zip(refs[nr + nb:nr + nb + no], o):
            ref[...] = val.astype(ref.dtype)
        for ref, val in zip(refs[nr + nb + no:], a):
            @pl.when(i == 0)
            def _(ref=ref, val=val):
                ref[...] = val

            @pl.when(i > 0)
            def _(ref=ref, val=val):
                ref[...] += val

    est = sum(tr * w * arr.dtype.itemsize for (arr, w, _, _, _) in row_ins) + sum(tr * c * 4 for c, _ in outs)
    res = pl.pallas_call(
        body, name=name, grid=(nrows // tr,), out_shape=out_shape, in_specs=in_specs, out_specs=out_specs,
        compiler_params=pltpu.CompilerParams(dimension_semantics=("arbitrary",), vmem_limit_bytes=_vmem(3 * est)),
    )(*[r[0] for r in row_ins], *bc_ins)
    return res[:no], res[no:]


def _colsum(v):
    return jnp.sum(v, axis=0, keepdims=True)


def _normmod_fwd(name, h, g, s, sh, nct=0):
    d = h.shape[1]

    def fn(rows, bcs, i):
        hh, (g_, s_, sh_) = rows[0], bcs
        s1 = jnp.where(i < nct, s_[0:1], s_[1:2])
        sh1 = jnp.where(i < nct, sh_[0:1], sh_[1:2])
        r = lax.rsqrt(jnp.mean(hh * hh, axis=-1, keepdims=True) + EPS)
        return [hh * r * g_ * (1.0 + s1) + sh1], []

    return _rowwise(name, fn, h.shape[0], [_ri(h)], [g, s, sh], [(d, MXU)])[0][0]


def _normmod_bwd(name, h, dxn, dres, g, s, nct=0):
    d = h.shape[1]

    def fn(rows, bcs, i):
        hh, dx, dr = rows
        g_, s_ = bcs
        ctx = i < nct
        s1 = jnp.where(ctx, s_[0:1], s_[1:2])
        r = lax.rsqrt(jnp.mean(hh * hh, axis=-1, keepdims=True) + EPS)
        hr = hh * r
        dy = dx * (1.0 + s1)
        u = dy * g_
        dh = r * u - hr * (r * r) * jnp.mean(u * hh, axis=-1, keepdims=True)
        dh = dh + jnp.where(ctx, 0.0, dr)

        def seg(v):
            v = _colsum(v)
            return jnp.concatenate([jnp.where(ctx, v, 0.0), jnp.where(ctx, 0.0, v)], axis=0)

        return [dh], [seg(dx), seg(dx * hr * g_), seg(dy * hr)]

    (dh,), (dsh, ds, dg) = _rowwise(name, fn, h.shape[0], [_ri(h), _ri(dxn), _ri(dres, ro=-nct)], [g, s],
                                    [(d, F32)], [(2, d)] * 3)
    return dh, dsh, ds, dg


def _resnorm_fwd(name, h, y, gate, g, s, sh):
    d = h.shape[1]

    def fn(rows, bcs, i):
        hh, yy = rows
        gate_, g_, s_, sh_ = bcs
        hn = hh + gate_ * yy
        r = lax.rsqrt(jnp.mean(hn * hn, axis=-1, keepdims=True) + EPS)
        return [hn, hn * r * g_ * (1.0 + s_) + sh_], []

    return _rowwise(name, fn, h.shape[0], [_ri(h), _ri(y)], [gate, g, s, sh], [(d, F32), (d, MXU)])[0]


def _gate_bwd(name, dh, y, gate):
    d = dh.shape[1]

    def fn(rows, bcs, i):
        dd, yy = rows
        dy = dd * bcs[0]
        return [dy], [_colsum(dd * yy), _colsum(dy)]

    (dy,), (dgate, dbias) = _rowwise(name, fn, dh.shape[0], [_ri(dh), _ri(y)], [gate], [(d, MXU)], [(1, d)] * 2)
    return dy, dgate, dbias


def _swiglu_fwd(name, u):
    f = u.shape[1] // 2

    def fn(rows, bcs, i):
        return [_silu(rows[0]) * rows[1]], []

    return _rowwise(name, fn, u.shape[0], [_ri(u, f, 0), _ri(u, f, 1)], [], [(f, MXU)])[0][0]


def _swiglu_bwd(name, u, dhid):
    f = u.shape[1] // 2

    def fn(rows, bcs, i):
        a, b, dd = rows
        return [jnp.concatenate([dd * b * _dsilu(a), dd * _silu(a)], axis=1)], []

    return _rowwise(name, fn, u.shape[0], [_ri(u, f, 0), _ri(u, f, 1), _ri(dhid)], [], [(2 * f, MXU)])[0][0]


def _glu_fwd(name, u):
    d = u.shape[1] // 2

    def fn(rows, bcs, i):
        return [rows[0] * _sigmoid(rows[1])], []

    return _rowwise(name, fn, u.shape[0], [_ri(u, d, 0), _ri(u, d, 1)], [], [(d, F32)])[0][0]


def _glu_bwd(name, u, dgl):
    d = u.shape[1] // 2

    def fn(rows, bcs, i):
        a, b, dd = rows
        sg = _sigmoid(b)
        du = jnp.concatenate([dd * sg, dd * a * sg * (1.0 - sg)], axis=1)
        return [du], [_colsum(du)]

    (du,), (db,) = _rowwise(name, fn, u.shape[0], [_ri(u, d, 0), _ri(u, d, 1), _ri(dgl)], [], [(2 * d, MXU)], [(1, 2 * d)])
    return du, db


def _ln_silu_fwd(name, v, g, b):
    d = v.shape[1]

    def fn(rows, bcs, i):
        vv = rows[0]
        mu = jnp.mean(vv, axis=-1, keepdims=True)
        xc = vv - mu
        rs = lax.rsqrt(jnp.mean(xc * xc, axis=-1, keepdims=True) + EPS)
        return [_silu(xc * rs * bcs[0] + bcs[1])], []

    return _rowwise(name, fn, v.shape[0], [_ri(v)], [g, b], [(d, MXU)])[0][0]


def _ln_silu_bwd(name, v, ds, g, b):
    d = v.shape[1]

    def fn(rows, bcs, i):
        vv, dd = rows
        mu = jnp.mean(vv, axis=-1, keepdims=True)
        xc = vv - mu
        rs = lax.rsqrt(jnp.mean(xc * xc, axis=-1, keepdims=True) + EPS)
        xh = xc * rs
        dln = dd * _dsilu(xh * bcs[0] + bcs[1])
        dxh = dln * bcs[0]
        dv = rs * (dxh - jnp.mean(dxh, axis=-1, keepdims=True) - xh * jnp.mean(dxh * xh, axis=-1, keepdims=True))
        return [dv], [_colsum(dln * xh), _colsum(dln)]

    (dv,), (dg, db) = _rowwise(name, fn, v.shape[0], [_ri(v), _ri(ds)], [g, b], [(d, F32)], [(1, d)] * 2)
    return dv, dg, db


def _final_loss(name, h, f, target, gate, gf):
    d = h.shape[1]

    def fn(rows, bcs, i):
        hh, ff, tg = rows
        gate_, g_ = bcs
        hn = hh + gate_ * ff
        r = lax.rsqrt(jnp.mean(hn * hn, axis=-1, keepdims=True) + EPS)
        hr = hn * r
        err = hr * g_ - tg
        dout = err * (1.0 / d)
        u = dout * g_
        dh = r * u - hr * (r * r) * jnp.mean(u * hn, axis=-1, keepdims=True)
        sq = jnp.sum(_colsum(err * err), axis=1, keepdims=True)
        return [dh], [jnp.broadcast_to(sq, (1, LANES)), _colsum(dout * hr)]

    (dh,), (sq, dgf) = _rowwise(name, fn, h.shape[0], [_ri(h), _ri(f), _ri(target)], [gate, gf], [(d, F32)], [(1, LANES), (1, d)])
    return dh, sq, dgf


def _shift_rows(x, o, seg_lo, seg_hi, row):
    n = x.shape[0]
    if o == 0:
        return x
    sh = pltpu.roll(x, (-o) % n, 0)
    ok = (row + o >= seg_lo) & (row + o < seg_hi)
    return jnp.where(ok, sh, 0.0)


def _seg_bounds(row, tc, tt):
    ctx = row < tc
    return jnp.where(ctx, 0, tc), jnp.where(ctx, tc, tt)


def _ssd_conv_fwd(name, zx, w, b, di, tc):
    tt, kc, cd = zx.shape[0], w.shape[0], w.shape[1]
    cb = _pick(cd, (LANES,))
    off = di // cb

    def body(x_ref, w_ref, b_ref, o_ref):
        x = x_ref[...]
        row = lax.broadcasted_iota(jnp.int32, (tt, 1), 0)
        lo, hi = _seg_bounds(row, tc, tt)
        acc = jnp.broadcast_to(b_ref[...], x.shape)
        for k in range(kc):
            acc = acc + w_ref[k:k + 1, :] * _shift_rows(x, k - kc // 2, lo, hi, row)
        o_ref[...] = _silu(acc)

    return pl.pallas_call(
        body, name=name, grid=(cd // cb,), out_shape=jax.ShapeDtypeStruct((tt, cd), F32),
        in_specs=[pl.BlockSpec((tt, cb), lambda j: (0, j + off)), pl.BlockSpec((kc, cb), lambda j: (0, j)),
                  pl.BlockSpec((1, cb), lambda j: (0, j))],
        out_specs=pl.BlockSpec((tt, cb), lambda j: (0, j)),
        compiler_params=pltpu.CompilerParams(dimension_semantics=("parallel",), vmem_limit_bytes=_vmem(4 * tt * cb * 4)),
    )(zx, w, b)


def _ssd_conv_bwd(name, zx, dact2, w, b, di, tc):
    tt, kc, cd = zx.shape[0], w.shape[0], w.shape[1]
    cb = _pick(cd, (LANES,))
    off = di // cb

    def body(x_ref, d0_ref, d1_ref, w_ref, b_ref, dx_ref, dw_ref, db_ref):
        x = x_ref[...]
        row = lax.broadcasted_iota(jnp.int32, (tt, 1), 0)
        lo, hi = _seg_bounds(row, tc, tt)
        pre = jnp.broadcast_to(b_ref[...], x.shape)
        for k in range(kc):
            pre = pre + w_ref[k:k + 1, :] * _shift_rows(x, k - kc // 2, lo, hi, row)
        dpre = (d0_ref[...] + d1_ref[...]) * _dsilu(pre)
        dx = jnp.zeros_like(x)
        for k in range(kc):
            o = k - kc // 2
            dx = dx + w_ref[k:k + 1, :] * _shift_rows(dpre, -o, lo, hi, row)
            dw_ref[k:k + 1, :] = _colsum(dpre * _shift_rows(x, o, lo, hi, row))
        dx_ref[...] = dx.astype(dx_ref.dtype)
        db_ref[...] = _colsum(dpre)

    blk = pl.BlockSpec((tt, cb), lambda j: (0, j))
    return pl.pallas_call(
        body, name=name, grid=(cd // cb,),
        out_shape=[jax.ShapeDtypeStruct((tt, cd), MXU), jax.ShapeDtypeStruct((kc, cd), F32), jax.ShapeDtypeStruct((1, cd), F32)],
        in_specs=[pl.BlockSpec((tt, cb), lambda j: (0, j + off)), pl.BlockSpec((None, tt, cb), lambda j: (0, 0, j)),
                  pl.BlockSpec((None, tt, cb), lambda j: (1, 0, j)), pl.BlockSpec((kc, cb), lambda j: (0, j)),
                  pl.BlockSpec((1, cb), lambda j: (0, j))],
        out_specs=[blk, pl.BlockSpec((kc, cb), lambda j: (0, j)), pl.BlockSpec((1, cb), lambda j: (0, j))],
        compiler_params=pltpu.CompilerParams(dimension_semantics=("parallel",), vmem_limit_bytes=_vmem(8 * tt * cb * 4)),
    )(zx, dact2, dact2, w, b)


def _strided_conv(name, x, w, b, stride):
    t, ch = x.shape
    kk = w.shape[0]
    pad = (kk // 2) * stride
    cb = _pick(ch, (LANES,))
    has_b = b is not None

    def body(*refs):
        x_ref, w_ref = refs[:2]
        o_ref, xp = refs[-2:]
        xp[0:pad, :] = jnp.zeros((pad, cb), F32)
        xp[pad + t:, :] = jnp.zeros((pad, cb), F32)
        xp[pad:pad + t, :] = x_ref[...]
        acc = jnp.broadcast_to(refs[2][...], (t, cb)) if has_b else jnp.zeros((t, cb), F32)
        for k in range(kk):
            acc = acc + w_ref[k:k + 1, :] * xp[k * stride:k * stride + t, :]
        o_ref[...] = acc

    ins, specs = [x, w], [pl.BlockSpec((t, cb), lambda j: (0, j)), pl.BlockSpec((kk, cb), lambda j: (0, j))]
    if has_b:
        ins.append(b)
        specs.append(pl.BlockSpec((1, cb), lambda j: (0, j)))
    return pl.pallas_call(
        body, name=name, grid=(ch // cb,), out_shape=jax.ShapeDtypeStruct((t, ch), F32), in_specs=specs,
        out_specs=pl.BlockSpec((t, cb), lambda j: (0, j)), scratch_shapes=[pltpu.VMEM((t + 2 * pad, cb), F32)],
        compiler_params=pltpu.CompilerParams(dimension_semantics=("parallel",), vmem_limit_bytes=_vmem(6 * t * cb * 4)),
    )(*ins)


def _strided_conv_dw(name, x, dv, kk, stride):
    t, ch = x.shape
    pad = (kk // 2) * stride
    cb = _pick(ch, (LANES,))

    def body(x_ref, d_ref, dw_ref, db_ref, xp):
        xp[0:pad, :] = jnp.zeros((pad, cb), F32)
        xp[pad + t:, :] = jnp.zeros((pad, cb), F32)
        xp[pad:pad + t, :] = x_ref[...]
        d = d_ref[...]
        for k in range(kk):
            dw_ref[k:k + 1, :] = _colsum(d * xp[k * stride:k * stride + t, :])
        db_ref[...] = _colsum(d)

    blk = pl.BlockSpec((t, cb), lambda j: (0, j))
    return pl.pallas_call(
        body, name=name, grid=(ch // cb,), out_shape=[jax.ShapeDtypeStruct((kk, ch), F32), jax.ShapeDtypeStruct((1, ch), F32)],
        in_specs=[blk, blk], out_specs=[pl.BlockSpec((kk, cb), lambda j: (0, j)), pl.BlockSpec((1, cb), lambda j: (0, j))],
        scratch_shapes=[pltpu.VMEM((t + 2 * pad, cb), F32)],
        compiler_params=pltpu.CompilerParams(dimension_semantics=("parallel",), vmem_limit_bytes=_vmem(6 * t * cb * 4)),
    )(x, dv)


def _grid_t(a, n1, n2):
    return a.reshape(n1, n2, a.shape[-1]).swapaxes(0, 1).reshape(n1 * n2, a.shape[-1])


def _chunk_order(d, i, ncc, nc):
    back = jnp.where(i < ncc, ncc - 1 - i, nc - 1 - (i - ncc))
    return jnp.where(d == 0, i, back)


def _ssd_chunk_setup(d, dt_raw, bias, a_log, q, h, di):
    p = di // h
    dt = _softplus(dt_raw + bias)
    a_neg = -jnp.exp(a_log)
    delta = dt * a_neg
    r = lax.broadcasted_iota(jnp.int32, (q, q), 0)
    c = lax.broadcasted_iota(jnp.int32, (q, q), 1)
    sgn = 1 - 2 * d
    mask = (r - c) * sgn >= 0
    mask_t = (c - r) * sgn >= 0
    a = _dot_lx(mask.astype(MXU), delta, NN, parts=3)
    tot = _colsum(delta)
    ea, dte, cd = jnp.exp(a), jnp.exp(tot - a), jnp.exp(tot)
    hh = lax.broadcasted_iota(jnp.int32, (h, di), 0)
    cc = lax.broadcasted_iota(jnp.int32, (h, di), 1)
    e = (cc // p == hh).astype(MXU)
    ex = _dot_rx(jnp.concatenate([dt, ea, dte, jnp.broadcast_to(cd, (8, h))], axis=0), e, NN)
    eye = (lax.broadcasted_iota(jnp.int32, (h, h), 0) == lax.broadcasted_iota(jnp.int32, (h, h), 1)).astype(MXU)
    a_t = _dot_lx(eye, a, NT, parts=3)
    return dict(dt=dt, a_neg=a_neg, a=a, a_t=a_t, mask=mask, mask_t=mask_t, e=e,
                dt_e=ex[0:q], ea_e=ex[q:2 * q], dte_e=ex[2 * q:3 * q], cd_e=ex[3 * q:3 * q + 1])


def _pick_heads(r, q, hpg, p):
    lane = lax.broadcasted_iota(jnp.int32, (q, hpg * p), 1) // p
    out = jnp.zeros((q, hpg * p), F32)
    for j in range(hpg):
        out = out + jnp.where(lane == j, r[j * q:(j + 1) * q], 0.0)
    return out


def _ssd_fwd(name, xbc, dt2, bias2, alog2, di, tc):
    tt, cd = xbc.shape
    h = dt2.shape[-1]
    q, n = SSD_CHUNK, SSD_STATE
    gn = (cd - di) // 2
    g = gn // n
    hpg, p = h // g, di // h
    gp = hpg * p
    nc, ncc = tt // q, tc // q
    assert di % gn == 0

    def body(x_ref, b_ref, c_ref, dt_ref, bias_ref, alog_ref, y_ref, hp_ref, ht):
        d, i = pl.program_id(0), pl.program_id(1)

        @pl.when(i == 0)
        def _():
            ht[...] = jnp.zeros_like(ht)

        s = _ssd_chunk_setup(d, dt_ref[...], bias_ref[...], alog_ref[...], q, h, di)
        xd = x_ref[...] * s["dt_e"]
        hp_ref[...] = ht[...]
        for gi in range(g):
            bg, cg = b_ref[:, gi * n:(gi + 1) * n].astype(MXU), c_ref[:, gi * n:(gi + 1) * n].astype(MXU)
            sl = slice(gi * gp, (gi + 1) * gp)
            sc = _dot(cg, bg, NT)
            ms = []
            for j in range(hpg):
                hd = gi * hpg + j
                seg = s["a"][:, hd:hd + 1] - s["a_t"][hd:hd + 1, :]
                ms.append(sc * jnp.exp(jnp.where(s["mask"], seg, -jnp.inf)))
            xdg = xd[:, sl]
            ydiag = _pick_heads(_dot(jnp.concatenate(ms, axis=0), xdg, NN), q, hpg, p)
            htg = ht[:, sl]
            y_ref[:, sl] = ydiag + _dot(cg, htg, NN) * s["ea_e"][:, sl]
            ht[:, sl] = s["cd_e"][:, sl] * htg + _dot(bg, xdg * s["dte_e"][:, sl], TN)

    def cidx(d, i):
        return _chunk_order(d, i, ncc, nc)

    return pl.pallas_call(
        body, name=name, grid=(2, nc),
        out_shape=[jax.ShapeDtypeStruct((2, tt, di), F32), jax.ShapeDtypeStruct((2, nc, n, di), F32)],
        in_specs=[pl.BlockSpec((q, di), lambda d, i: (cidx(d, i), 0)),
                  pl.BlockSpec((q, gn), lambda d, i: (cidx(d, i), di // gn)),
                  pl.BlockSpec((q, gn), lambda d, i: (cidx(d, i), di // gn + 1)),
                  pl.BlockSpec((None, q, h), lambda d, i: (d, cidx(d, i), 0)),
                  pl.BlockSpec((None, 1, h), lambda d, i: (d, 0, 0)),
                  pl.BlockSpec((None, 1, h), lambda d, i: (d, 0, 0))],
        out_specs=[pl.BlockSpec((None, q, di), lambda d, i: (d, cidx(d, i), 0)),
                   pl.BlockSpec((None, None, n, di), lambda d, i: (d, cidx(d, i), 0, 0))],
        scratch_shapes=[pltpu.VMEM((n, di), F32)],
        compiler_params=pltpu.CompilerParams(dimension_semantics=("arbitrary", "arbitrary"), vmem_limit_bytes=_vmem(16 * q * di * 4)),
    )(xbc, xbc, xbc, dt2, bias2, alog2)


def _ssd_bwd(name, xbc, dt2, bias2, alog2, dy, hp2, dskip_e, di, tc):
    tt, cd = xbc.shape
    h = dt2.shape[-1]
    q, n = SSD_CHUNK, SSD_STATE
    gn = (cd - di) // 2
    g = gn // n
    hpg, p = h // g, di // h
    gp = hpg * p
    nc, ncc = tt // q, tc // q

    def body(x_ref, b_ref, c_ref, dt_ref, bias_ref, alog_ref, dy_ref, hp_ref, dsk_ref,
             dxbc_ref, ddt_ref, dalog_ref, dbias_ref, dht, dxd, off):
        d, i = pl.program_id(0), pl.program_id(1)

        @pl.when(i == 0)
        def _():
            dht[...] = jnp.zeros_like(dht)
            dalog_ref[...] = jnp.zeros_like(dalog_ref)
            dbias_ref[...] = jnp.zeros_like(dbias_ref)

        s = _ssd_chunk_setup(d, dt_ref[...], bias_ref[...], alog_ref[...], q, h, di)
        x, dyc = x_ref[...], dy_ref[...]
        xd = x * s["dt_e"]
        dyea = dyc * s["ea_e"]
        xdte = xd * s["dte_e"]
        lane = lax.broadcasted_iota(jnp.int32, (q, gp), 1) // p
        lane_h = lax.broadcasted_iota(jnp.int32, (q, h), 1)
        da_d = jnp.zeros((q, h), F32)
        last_e = []
        for gi in range(g):
            bg, cg = b_ref[:, gi * n:(gi + 1) * n].astype(MXU), c_ref[:, gi * n:(gi + 1) * n].astype(MXU)
            sl = slice(gi * gp, (gi + 1) * gp)
            sc, sct = _dot(cg, bg, NT), _dot(bg, cg, NT)
            dyg, xdg = dyc[:, sl], xd[:, sl]
            htg, dhtg = hp_ref[:, sl], dht[:, sl]
            dystack = jnp.concatenate([jnp.where(lane == j, dyg, 0.0) for j in range(hpg)], axis=0)
            xdstack = jnp.concatenate([jnp.where(lane == j, xdg, 0.0) for j in range(hpg)], axis=0)
            gs = _dot(dystack, xdg, NT)
            gst = _dot(xdstack, dyg, NT)
            ds = jnp.zeros((q, q), F32)
            mts = []
            for j in range(hpg):
                hd = gi * hpg + j
                col, rw = s["a"][:, hd:hd + 1], s["a_t"][hd:hd + 1, :]
                gl = gs[j * q:(j + 1) * q] * jnp.exp(jnp.where(s["mask"], col - rw, -jnp.inf))
                ds = ds + gl
                mt = sct * jnp.exp(jnp.where(s["mask_t"], rw - col, -jnp.inf))
                mts.append(mt)
                da_j = jnp.sum(gl * sc, axis=1, keepdims=True) - jnp.sum(gst[j * q:(j + 1) * q] * mt, axis=1, keepdims=True)
                da_d = da_d + jnp.where(lane_h == hd, da_j, 0.0)
            dxd_diag = _pick_heads(_dot(jnp.concatenate(mts, axis=0), dyg, NN), q, hpg, p)
            z = _dot(bg, dhtg, NN) * s["dte_e"][:, sl]
            yoff = _dot(cg, htg, NN) * s["ea_e"][:, sl]
            off[:, sl] = dyg * yoff - xdg * z
            dxd[:, sl] = dxd_diag + z
            dxbc_ref[:, di + gi * n:di + (gi + 1) * n] = _dot(ds, cg, TN) + _dot(xdte[:, sl], dhtg, NT)
            dxbc_ref[:, di + gn + gi * n:di + gn + (gi + 1) * n] = _dot(ds, bg, NN) + _dot(dyea[:, sl], htg, NT)
            last_e.append(s["cd_e"][:, sl] * _colsum(dhtg * htg) + _colsum(xdg * z))
            dht[:, sl] = s["cd_e"][:, sl] * dhtg + _dot(cg, dyea[:, sl], TN)
        dxd_all = dxd[...]
        last = jnp.concatenate(last_e, axis=1)
        da = da_d + _dot_rx(off[...], s["e"], NT)
        last_h = _dot_rx(jnp.broadcast_to(last, (8, di)), s["e"], NT)[0:1]
        ddelta = _dot_lx(s["mask_t"].astype(MXU), da, NN, parts=3) + last_h
        ddt = ddelta * s["a_neg"] + _dot_rx(dxd_all * x, s["e"], NT)
        ddt_raw = ddt * _sigmoid(dt_ref[...] + bias_ref[...])
        ddt_ref[...] = ddt_raw
        dalog_ref[...] += _colsum(ddelta * s["dt"]) * s["a_neg"]
        dbias_ref[...] += _colsum(ddt_raw)
        dxbc_ref[:, 0:di] = dxd_all * s["dt_e"] + jnp.where(d == 0, dyc * dsk_ref[...], 0.0)

    def cidx(d, i):
        return _chunk_order(d, nc - 1 - i, ncc, nc)

    return pl.pallas_call(
        body, name=name, grid=(2, nc),
        out_shape=[jax.ShapeDtypeStruct((2, tt, cd), F32), jax.ShapeDtypeStruct((2, tt, h), F32),
                   jax.ShapeDtypeStruct((2, 1, h), F32), jax.ShapeDtypeStruct((2, 1, h), F32)],
        in_specs=[pl.BlockSpec((q, di), lambda d, i: (cidx(d, i), 0)),
                  pl.BlockSpec((q, gn), lambda d, i: (cidx(d, i), di // gn)),
                  pl.BlockSpec((q, gn), lambda d, i: (cidx(d, i), di // gn + 1)),
                  pl.BlockSpec((None, q, h), lambda d, i: (d, cidx(d, i), 0)),
                  pl.BlockSpec((None, 1, h), lambda d, i: (d, 0, 0)),
                  pl.BlockSpec((None, 1, h), lambda d, i: (d, 0, 0)),
                  pl.BlockSpec((q, di), lambda d, i: (cidx(d, i), 0)),
                  pl.BlockSpec((None, None, n, di), lambda d, i: (d, cidx(d, i), 0, 0)),
                  pl.BlockSpec((1, di), lambda d, i: (0, 0))],
        out_specs=[pl.BlockSpec((None, q, cd), lambda d, i: (d, cidx(d, i), 0)),
                   pl.BlockSpec((None, q, h), lambda d, i: (d, cidx(d, i), 0)),
                   pl.BlockSpec((None, 1, h), lambda d, i: (d, 0, 0)),
                   pl.BlockSpec((None, 1, h), lambda d, i: (d, 0, 0))],
        scratch_shapes=[pltpu.VMEM((n, di), F32), pltpu.VMEM((q, di), F32), pltpu.VMEM((q, di), F32)],
        compiler_params=pltpu.CompilerParams(dimension_semantics=("arbitrary", "arbitrary"), vmem_limit_bytes=_vmem(24 * q * di * 4)),
    )(xbc, xbc, xbc, dt2, bias2, alog2, dy, hp2, dskip_e)


def _ssd_gate_fwd(name, y2, xbc, zx, dskip_e, norm_w, di, nct, t):
    def fn(rows, bcs, i):
        yf, yb, xs, z = rows
        zg = (yf + yb + bcs[0] * xs) * _silu(z)
        rn = lax.rsqrt(jnp.mean(zg * zg, axis=-1, keepdims=True) + EPS)
        return [zg * rn * bcs[1]], []

    ins = [_ri(y2, lead=0, ro=nct), _ri(y2, lead=1, ro=nct), _ri(xbc, di, 0, ro=nct), _ri(zx, di, 0, ro=nct)]
    return _rowwise(name, fn, t, ins, [dskip_e, norm_w], [(di, MXU)])[0][0]


def _ssd_gate_bwd(name, dyn, y2, xbc, zx, dskip_e, norm_w, di, nct, tt):
    def fn(rows, bcs, i):
        dn, yf, yb, xs, z = rows
        lat = i >= nct
        ytot = yf + yb + bcs[0] * xs
        sz = _silu(z)
        zg = ytot * sz
        rn = lax.rsqrt(jnp.mean(zg * zg, axis=-1, keepdims=True) + EPS)
        u = dn * bcs[1]
        dzg = rn * u - zg * (rn * rn * rn) * jnp.mean(u * zg, axis=-1, keepdims=True)
        dy = jnp.where(lat, dzg * sz, 0.0)
        dz = jnp.where(lat, dzg * ytot * _dsilu(z), 0.0)
        return [dy, dz], [jnp.where(lat, _colsum(dn * zg * rn), 0.0), jnp.where(lat, _colsum(dy * xs), 0.0)]

    ins = [_ri(dyn, ro=-nct), _ri(y2, lead=0), _ri(y2, lead=1), _ri(xbc, di, 0), _ri(zx, di, 0)]
    (dy, dz), (dnw, ddsk) = _rowwise(name, fn, tt, ins, [dskip_e, norm_w], [(di, F32), (di, MXU)], [(1, di)] * 2)
    return dy, dz, dnw, ddsk


def _ada_fwd(name, cs, w, b):
    nl, d, c = w.shape
    r = cs.shape[0]

    def body(cs_ref, w_ref, b_ref, o_ref):
        o_ref[...] = _dot(_silu(cs_ref[...]), w_ref[...], NN) + b_ref[...]

    return pl.pallas_call(
        body, name=name, grid=(nl,), out_shape=jax.ShapeDtypeStruct((nl, r, c), F32),
        in_specs=[pl.BlockSpec((r, d), lambda l: (0, 0)), pl.BlockSpec((None, d, c), lambda l: (l, 0, 0)),
                  pl.BlockSpec((None, 1, c), lambda l: (l, 0, 0))],
        out_specs=pl.BlockSpec((None, r, c), lambda l: (l, 0, 0)),
        compiler_params=pltpu.CompilerParams(dimension_semantics=("parallel",), vmem_limit_bytes=_vmem(2 * d * c * 4)),
    )(cs, w, b)


def _ada_bwd(name, cs, w, dmod):
    nl, d, c = w.shape
    r = cs.shape[0]

    def body(cs_ref, w_ref, dm_ref, dw_ref, dsc_ref):
        dm = dm_ref[...]
        dw_ref[...] = _dot(_silu(cs_ref[...]), dm, TN)

        @pl.when(pl.program_id(0) == 0)
        def _():
            dctx = jnp.broadcast_to(_colsum(dm[r // 2:]), (8, c))
            dsc_ref[...] = _dot(dctx, w_ref[...], NT)[0:1]

    return pl.pallas_call(
        body, name=name, grid=(nl,), out_shape=[jax.ShapeDtypeStruct((nl, d, c), F32), jax.ShapeDtypeStruct((1, d), F32)],
        in_specs=[pl.BlockSpec((r, d), lambda l: (0, 0)), pl.BlockSpec((None, d, c), lambda l: (l, 0, 0)),
                  pl.BlockSpec((None, r, c), lambda l: (l, 0, 0))],
        out_specs=[pl.BlockSpec((None, d, c), lambda l: (l, 0, 0)), pl.BlockSpec((1, d), lambda l: (0, 0))],
        compiler_params=pltpu.CompilerParams(dimension_semantics=("arbitrary",), vmem_limit_bytes=_vmem(4 * d * c * 4)),
    )(cs, w, dmod)


def _adam_math(w, g, m, v):
    m = ADAM_B1 * m + (1.0 - ADAM_B1) * g
    v = ADAM_B2 * v + (1.0 - ADAM_B2) * (g * g)
    m_hat = m / (1.0 - ADAM_B1 ** ADAM_STEP)
    v_hat = v / (1.0 - ADAM_B2 ** ADAM_STEP)
    delta = -ADAM_LR * (m_hat / (jnp.sqrt(v_hat) + ADAM_EPS) + ADAM_WD * w)
    return delta, m, v


def _adam(name, slots, w, m, v):
    ns, r, c = slots.shape
    tr = _pick(r, (256, 128, 64, 32, 16, 8))

    def body(s_ref, w_ref, m_ref, v_ref, g_ref, d_ref, mo_ref, vo_ref):
        g = s_ref[0].astype(F32)
        for k in range(1, ns):
            g = g + s_ref[k].astype(F32)
        d, mn, vn = _adam_math(w_ref[...], g, m_ref[...], v_ref[...])
        g_ref[...], d_ref[...], mo_ref[...], vo_ref[...] = g, d, mn, vn

    blk = pl.BlockSpec((tr, c), lambda i: (i, 0))
    return pl.pallas_call(
        body, name=name, grid=(r // tr,), out_shape=[jax.ShapeDtypeStruct((r, c), F32)] * 4,
        in_specs=[pl.BlockSpec((ns, tr, c), lambda i: (0, i, 0)), blk, blk, blk], out_specs=[blk] * 4,
        compiler_params=pltpu.CompilerParams(dimension_semantics=("parallel",), vmem_limit_bytes=_vmem(16 * tr * c * 4)),
    )(slots, w, m, v)


def _adam_small(name, slots, ws, ms, vs, scale=None):
    k = len(slots)

    def body(*refs):
        s_refs, w_refs, m_refs, v_refs = refs[:k], refs[k:2 * k], refs[2 * k:3 * k], refs[3 * k:4 * k]
        sc_ref = refs[4 * k] if scale is not None else None
        outs = refs[4 * k + (scale is not None):]
        for a in range(k):
            g = s_refs[a][0]
            for j in range(1, NDEV):
                g = g + s_refs[a][j]
            if scale is not None and a == scale[0]:
                g = g * _dsilu(sc_ref[...])
            d, mn, vn = _adam_math(w_refs[a][...], g, m_refs[a][...], v_refs[a][...])
            outs[a][...], outs[k + a][...], outs[2 * k + a][...], outs[3 * k + a][...] = g, d, mn, vn

    shapes = [jax.ShapeDtypeStruct(w.shape, F32) for w in ws]
    extra = [scale[1]] if scale is not None else []
    res = pl.pallas_call(body, name=name, out_shape=shapes * 4)(*slots, *ws, *ms, *vs, *extra)
    return res[:k], res[k:2 * k], res[2 * k:3 * k], res[3 * k:]


def _unshard_cols(g):
    g = jnp.moveaxis(g, 0, -2)
    return g.reshape(g.shape[:-2] + (g.shape[-2] * g.shape[-1],))


def _shard_cols(a):
    a = a.reshape(a.shape[:-1] + (NDEV, a.shape[-1] // NDEV))
    return jnp.moveaxis(a, -2, 0)


def _unshard_rows(g):
    g = jnp.moveaxis(g, 0, -3)
    return g.reshape(g.shape[:-3] + (g.shape[-3] * g.shape[-2], g.shape[-1]))


def _shard_rows(a):
    a = a.reshape(a.shape[:-2] + (NDEV, a.shape[-2] // NDEV, a.shape[-1]))
    return jnp.moveaxis(a, -3, 0)


def _flat2(a):
    return a.reshape((-1, a.shape[-1]))


def kernel(x, c, ctx, c_ctx, ada_w, ada_b, norm_mix_g, norm_ffn_g, final_norm_g, ssd_w_in, ssd_conv_w, ssd_conv_b, ssd_dt_bias_f, ssd_dt_bias_b, ssd_a_log_f, ssd_a_log_b, ssd_d_skip, ssd_norm_w, ssd_w_out, conf_w_pw1, conf_b_pw1, conf_dw_w, conf_dw_b, conf_ln_g, conf_ln_b, conf_w_pw2, conf_b_pw2, ffn_w_in, ffn_w_out, loss_target, m_c_ctx, m_ada_w, m_ada_b, m_norm_mix_g, m_norm_ffn_g, m_final_norm_g, m_ssd_w_in, m_ssd_conv_w, m_ssd_conv_b, m_ssd_dt_bias_f, m_ssd_dt_bias_b, m_ssd_a_log_f, m_ssd_a_log_b, m_ssd_d_skip, m_ssd_norm_w, m_ssd_w_out, m_conf_w_pw1, m_conf_b_pw1, m_conf_dw_w, m_conf_dw_b, m_conf_ln_g, m_conf_ln_b, m_conf_w_pw2, m_conf_b_pw2, m_ffn_w_in, m_ffn_w_out, v_c_ctx, v_ada_w, v_ada_b, v_norm_mix_g, v_norm_ffn_g, v_final_norm_g, v_ssd_w_in, v_ssd_conv_w, v_ssd_conv_b, v_ssd_dt_bias_f, v_ssd_dt_bias_b, v_ssd_a_log_f, v_ssd_a_log_b, v_ssd_d_skip, v_ssd_norm_w, v_ssd_w_out, v_conf_w_pw1, v_conf_b_pw1, v_conf_dw_w, v_conf_dw_b, v_conf_ln_g, v_conf_ln_b, v_conf_w_pw2, v_conf_b_pw2, v_ffn_w_in, v_ffn_w_out):
    args = dict(locals())
    names = ['c_ctx', 'ada_w', 'ada_b', 'norm_mix_g', 'norm_ffn_g', 'final_norm_g', 'ssd_w_in', 'ssd_conv_w', 'ssd_conv_b',
             'ssd_dt_bias_f', 'ssd_dt_bias_b', 'ssd_a_log_f', 'ssd_a_log_b', 'ssd_d_skip', 'ssd_norm_w', 'ssd_w_out',
             'conf_w_pw1', 'conf_b_pw1', 'conf_dw_w', 'conf_dw_b', 'conf_ln_g', 'conf_ln_b', 'conf_w_pw2', 'conf_b_pw2',
             'ffn_w_in', 'ffn_w_out']
    me = 4 * lax.axis_index("x") + 2 * lax.axis_index("y") + lax.axis_index("c")
    t, d = x.shape[1], x.shape[2]
    tc = ctx.shape[1]
    tt = tc + t
    nct = tc // ROW_TILE
    assert tc % ROW_TILE == 0 and t % ROW_TILE == 0
    h = ssd_dt_bias_f.shape[-1]
    di = ssd_norm_w.shape[-1]
    cdim = ssd_conv_b.shape[-1]
    kc = ssd_conv_w.shape[1]
    ck = conf_dw_w.shape[1]
    ch = d // 2
    rows_g = t // GRID_W
    nl = ada_w.shape[0]
    cw = ada_w.shape[2]
    x2, ctx2, tgt = x[0], ctx[0], loss_target[0]

    gat, _ = _exchange("gather_weights", [
        c, ssd_w_in[0].astype(WIRE), ssd_w_out[0].astype(WIRE), conf_w_pw1[0].astype(WIRE), conf_w_pw2[0].astype(WIRE),
        ffn_w_in.astype(WIRE), ffn_w_out.astype(WIRE), ssd_conv_w[0], conf_b_pw1, conf_dw_w[0], conf_dw_b, conf_ln_g,
        conf_ln_b, conf_b_pw2])
    (c_all, w_in_g, w_out_g, pw1_g, pw2_g, fin_g, fout_g, convw_g, bpw1_g, dww_g, dwb_g, lng_g, lnb_g, bpw2_g) = gat
    w_ssd_in = _unshard_cols(w_in_g)
    w_zx = w_ssd_in[:, :di + cdim]
    w_z, w_xbc = w_zx[:, :di], w_zx[:, di:]
    w_dt = jnp.pad(w_ssd_in[:, di + cdim:], ((0, 0), (0, LANES - 2 * h)))
    w_ssd_out = _unshard_rows(w_out_g)
    w_pw1, w_pw2 = _unshard_cols(pw1_g), _unshard_rows(pw2_g)
    w_fin, w_fout = _unshard_cols(fin_g), _unshard_rows(fout_g)
    conv_w_full, dw_w_full = _unshard_cols(convw_g), _unshard_cols(dww_g)
    b_pw1, dw_b, ln_g, ln_b, b_pw2 = (_unshard_cols(a) for a in (bpw1_g, dwb_g, lng_g, lnb_g, bpw2_g))

    cs_all = jnp.concatenate([c_all[:, 0, :], jnp.broadcast_to(c_ctx[None, :], (NDEV, d))], axis=0)
    ada_b_mine = lax.dynamic_slice_in_dim(ada_b, me * cw, cw, axis=1)[:, None, :]
    mod_part = _ada_fwd("ada_fwd", cs_all, ada_w, ada_b_mine)
    (mod_g,), _ = _exchange("gather_mod", [mod_part])
    mod_all = jnp.moveaxis(mod_g, 0, 2).reshape(nl, 2 * NDEV, NDEV * cw)
    mod_lat = lax.dynamic_slice_in_dim(mod_all, me, 1, axis=1)[:, 0, :]
    mod_ctx = mod_all[0, NDEV, :]

    def six(v):
        return [v[k * d:(k + 1) * d][None, :] for k in range(6)]

    sh1, s1, g1, sh2, s2, g2 = six(mod_lat[0])
    csh1, cs1 = six(mod_ctx)[:2]
    sh1b, s1b, g1b, sh2b, s2b, g2b = six(mod_lat[1])
    nmg, nfg = norm_mix_g, norm_ffn_g

    h_all = jnp.concatenate([ctx2, x2], axis=0)
    s01, sh01 = jnp.concatenate([cs1, s1], axis=0), jnp.concatenate([csh1, sh1], axis=0)
    xn_all = _normmod_fwd("l0_norm", h_all, nmg[0:1], s01, sh01, nct)
    zx = _mm("ssd_in_proj", xn_all, w_zx, "nn")
    dtr = _mm("ssd_dt_proj", xn_all, w_dt, "nn")
    dt2 = jnp.moveaxis(dtr[:, :2 * h].reshape(tt, 2, h), 1, 0)
    bias2 = jnp.stack([ssd_dt_bias_f, ssd_dt_bias_b])
    alog2 = jnp.stack([ssd_a_log_f, ssd_a_log_b])
    xbc = _ssd_conv_fwd("ssd_conv", zx, conv_w_full, ssd_conv_b, di, tc)
    y2, hp2 = _ssd_fwd("ssd_scan", xbc, dt2, bias2, alog2, di, tc)
    dskip_e = jnp.repeat(ssd_d_skip, di // h, axis=1)
    yn = _ssd_gate_fwd("ssd_gate", y2, xbc, zx, dskip_e, ssd_norm_w, di, nct, t)
    mix0 = _mm("ssd_out_proj", yn, w_ssd_out, "nn")
    h1, xf0 = _resnorm_fwd("l0_res_norm", x2, mix0, g1, nfg[0:1], s2, sh2)
    u0 = _mm("ffn0_in", xf0, w_fin[0], "nn")
    hid0 = _swiglu_fwd("ffn0_act", u0)
    f0 = _mm("ffn0_out", hid0, w_fout[0], "nn")
    h2, xn1 = _resnorm_fwd("l1_norm", h1, f0, g2, nmg[1:2], s1b, sh1b)
    u1 = _mm("conf_pw1", xn1, w_pw1, "nn", bias=b_pw1)
    gl = _glu_fwd("conf_glu", u1)
    gl_h = _grid_t(gl[:, :ch], rows_g, GRID_W)
    gl_v = gl[:, ch:]
    v_h = _strided_conv("conf_conv_h", gl_h, dw_w_full[:, :ch], dw_b[:, :ch], rows_g)
    v_v = _strided_conv("conf_conv_v", gl_v, dw_w_full[:, ch:], dw_b[:, ch:], GRID_W)
    v = jnp.concatenate([_grid_t(v_h, GRID_W, rows_g), v_v], axis=1)
    sl = _ln_silu_fwd("conf_ln", v, ln_g, ln_b)
    mix1 = _mm("conf_pw2", sl, w_pw2, "nn", bias=b_pw2)
    h3, xf1 = _resnorm_fwd("l1_res_norm", h2, mix1, g1b, nfg[1:2], s2b, sh2b)
    u2 = _mm("ffn1_in", xf1, w_fin[1], "nn")
    hid1 = _swiglu_fwd("ffn1_act", u2)
    f1 = _mm("ffn1_out", hid1, w_fout[1], "nn")
    dh, sq, d_final_g = _final_loss("final_loss", h3, f1, tgt, g2b, final_norm_g[None, :])
    loss = lax.psum(0.5 * sq[0, 0] / d, AXES)

    zero2 = jnp.zeros((2, d), F32)

    def ffn_bwd(tag, dh, hin, xf, u, hid, f, gate, w_in, w_out, g_norm, s_mod):
        df, dgate, _ = _gate_bwd(tag + "_gate_bwd", dh, f, gate)
        dhid = _mm(tag + "_dhid", df, w_out, "nt")
        dw_out = _mm(tag + "_dwout", hid, df, "tn")
        du = _swiglu_bwd(tag + "_act_bwd", u, dhid)
        dw_in = _mm(tag + "_dwin", xf, du, "tn")
        dxf = _mm(tag + "_dx", du, w_in, "nt")
        s_2 = jnp.concatenate([s_mod, s_mod], axis=0)
        dh, dsh, ds, dg = _normmod_bwd(tag + "_norm_bwd", hin, dxf, dh, g_norm, s_2)
        return dh, dgate, dsh[1:2], ds[1:2], dg[1:2], dw_in, dw_out

    dh, d_g2b, d_sh2b, d_s2b, d_nfg1, g_fin1, g_fout1 = ffn_bwd("ffn1", dh, h3, xf1, u2, hid1, f1, g2b, w_fin[1], w_fout[1], nfg[1:2], s2b)
    dmix1, d_g1b, g_bpw2 = _gate_bwd("conf_gate_bwd", dh, mix1, g1b)
    dsl = _mm("conf_dsl", dmix1, w_pw2, "nt")
    g_pw2 = _mm("conf_dwpw2", sl, dmix1, "tn")
    dv, g_lng, g_lnb = _ln_silu_bwd("conf_ln_bwd", v, dsl, ln_g, ln_b)
    dv_h, dv_v = _grid_t(dv[:, :ch], rows_g, GRID_W), dv[:, ch:]
    w_flip = dw_w_full[::-1]
    dgl_h = _strided_conv("conf_conv_h_bwd", dv_h, w_flip[:, :ch], None, rows_g)
    dgl_v = _strided_conv("conf_conv_v_bwd", dv_v, w_flip[:, ch:], None, GRID_W)
    g_dww_h, g_dwb_h = _strided_conv_dw("conf_conv_h_dw", gl_h, dv_h, ck, rows_g)
    g_dww_v, g_dwb_v = _strided_conv_dw("conf_conv_v_dw", gl_v, dv_v, ck, GRID_W)
    g_dww, g_dwb = jnp.concatenate([g_dww_h, g_dww_v], axis=1), jnp.concatenate([g_dwb_h, g_dwb_v], axis=1)
    dgl = jnp.concatenate([_grid_t(dgl_h, GRID_W, rows_g), dgl_v], axis=1)
    du1, g_bpw1 = _glu_bwd("conf_glu_bwd", u1, dgl)
    g_pw1 = _mm("conf_dwpw1", xn1, du1, "tn")
    dxn1 = _mm("conf_dx", du1, w_pw1, "nt")
    dh, dsh_, ds_, dg_ = _normmod_bwd("l1_norm_bwd", h2, dxn1, dh, nmg[1:2], jnp.concatenate([s1b, s1b], axis=0))
    d_sh1b, d_s1b, d_nmg1 = dsh_[1:2], ds_[1:2], dg_[1:2]
    dh, d_g2, d_sh2, d_s2, d_nfg0, g_fin0, g_fout0 = ffn_bwd("ffn0", dh, h1, xf0, u0, hid0, f0, g2, w_fin[0], w_fout[0], nfg[0:1], s2)
    dmix0, d_g1, _ = _gate_bwd("ssd_gate_res_bwd", dh, mix0, g1)
    dyn = _mm("ssd_dyn", dmix0, w_ssd_out, "nt")
    g_ssd_out = _mm("ssd_dwout", yn, dmix0, "tn")
    dy, dz, g_normw, ddsk_e = _ssd_gate_bwd("ssd_gate_bwd", dyn, y2, xbc, zx, dskip_e, ssd_norm_w, di, nct, tt)
    dxbc2, ddt2, g_alog2, g_bias2 = _ssd_bwd("ssd_scan_bwd", xbc, dt2, bias2, alog2, dy, hp2, dskip_e, di, tc)
    dxr, g_convw, g_convb = _ssd_conv_bwd("ssd_conv_bwd", zx, dxbc2, conv_w_full, ssd_conv_b, di, tc)
    ddt_p = jnp.pad(jnp.moveaxis(ddt2, 0, 1).reshape(tt, 2 * h), ((0, 0), (0, LANES - 2 * h))).astype(MXU)
    dxn = _mm("ssd_dx_z", dz, w_z, "nt")
    dxn = _mm("ssd_dx_xbc", dxr, w_xbc, "nt", add=dxn)
    dxn = _mm("ssd_dx_dt", ddt_p, w_dt, "nt", add=dxn)
    g_ssd_in = jnp.concatenate([_mm("ssd_dw_z", xn_all, dz, "tn"), _mm("ssd_dw_xbc", xn_all, dxr, "tn"),
                                _mm("ssd_dw_dt", xn_all, ddt_p, "tn")[:, :2 * h]], axis=1)
    dh_all, dsh_, ds_, dg_ = _normmod_bwd("l0_norm_bwd", h_all, dxn, dh, nmg[0:1], s01, nct)
    grad_x = dh_all[tc:][None]
    d_csh1, d_sh1, d_cs1, d_s1 = dsh_[0:1], dsh_[1:2], ds_[0:1], ds_[1:2]
    d_nmg0 = dg_[0:1] + dg_[1:2]

    z1 = jnp.zeros((1, d), F32)
    dmod = jnp.concatenate([jnp.concatenate([d_sh1, d_s1, d_g1, d_sh2, d_s2, d_g2], axis=1),
                            jnp.concatenate([d_sh1b, d_s1b, d_g1b, d_sh2b, d_s2b, d_g2b], axis=1),
                            jnp.concatenate([d_csh1, d_cs1, z1, z1, z1, z1], axis=1)], axis=0)
    (dmod_g,), _ = _exchange("gather_dmod", [dmod])
    dmod_mine = lax.dynamic_slice_in_dim(dmod_g, me * cw, cw, axis=2)
    dmod16 = jnp.stack([jnp.concatenate([dmod_mine[:, 0], dmod_mine[:, 2]], axis=0),
                        jnp.concatenate([dmod_mine[:, 1], jnp.zeros((NDEV, cw), F32)], axis=0)])
    g_ada_w, dsc_part = _ada_bwd("ada_bwd", cs_all, ada_w, dmod16)
    g_ada_b = dmod[0:2] + jnp.concatenate([dmod[2:3], jnp.zeros((1, 6 * d), F32)], axis=0)

    big = [_shard_cols(g_ssd_in), _shard_rows(g_ssd_out), _shard_cols(g_pw1), _shard_rows(g_pw2),
           _flat3(_shard_cols(jnp.stack([g_fin0, g_fin1]))), _flat3(_shard_rows(jnp.stack([g_fout0, g_fout1])))]
    big = [b.astype(WIRE) for b in big]
    small_sh = [_shard_cols(g_convw), _shard_cols(g_bpw1), _shard_cols(g_dww), _shard_cols(g_dwb), _shard_cols(g_lng),
                _shard_cols(g_lnb), _shard_cols(g_bpw2)]
    d_dskip = jnp.sum(ddsk_e.reshape(h, di // h), axis=1)[None, :]
    rep = [dsc_part, g_ada_b, jnp.concatenate([d_nmg0, d_nmg1], axis=0), jnp.concatenate([d_nfg0, d_nfg1], axis=0),
           d_final_g, g_convb, g_bias2[0], g_bias2[1], g_alog2[0], g_alog2[1], d_dskip, g_normw]
    rep_g, sc_g = _exchange("exchange_grads", rep, big + small_sh)
    big_r, small_r = sc_g[:len(big)], sc_g[len(big):]

    out = {}

    def put(name, res):
        w = args[name]
        out["grad_" + name], out["delta_" + name], out["new_m_" + name], out["new_v_" + name] = (r.reshape(w.shape) for r in res)

    for name, slots in zip(["ssd_w_in", "ssd_w_out", "conf_w_pw1", "conf_w_pw2", "ffn_w_in", "ffn_w_out"], big_r):
        put(name, _adam("adam_" + name, slots, _flat2(args[name]), _flat2(args["m_" + name]), _flat2(args["v_" + name])))
    put("ada_w", _adam("adam_ada_w", _flat2(g_ada_w)[None], _flat2(ada_w), _flat2(m_ada_w), _flat2(v_ada_w)))
    small_names = ["ssd_conv_w", "conf_b_pw1", "conf_dw_w", "conf_dw_b", "conf_ln_g", "conf_ln_b", "conf_b_pw2",
                   "c_ctx", "ada_b", "norm_mix_g", "norm_ffn_g", "final_norm_g", "ssd_conv_b", "ssd_dt_bias_f", "ssd_dt_bias_b",
                   "ssd_a_log_f", "ssd_a_log_b", "ssd_d_skip", "ssd_norm_w"]
    slots = list(small_r) + list(rep_g)

    def as2(a):
        return a.reshape((1, -1)) if a.ndim == 1 else _flat2(a)

    res = _adam_small("adam_small", slots, [as2(args[n]) for n in small_names], [as2(args["m_" + n]) for n in small_names],
                      [as2(args["v_" + n]) for n in small_names], scale=(small_names.index("c_ctx"), c_ctx[None, :]))
    for k, name in enumerate(small_names):
        put(name, [r[k] for r in res])
    return (loss, grad_x, *[out["grad_" + n] for n in names], *[out["delta_" + n] for n in names],
            *[out["new_m_" + n] for n in names], *[out["new_v_" + n] for n in names])


def _flat3(a):
    return a.reshape((a.shape[0], -1, a.shape[-1]))
```

```python
import functools

import jax
import jax.numpy as jnp
from jax import lax
from jax.experimental import pallas as pl
from jax.experimental.pallas import tpu as pltpu

F32 = jnp.float32
MXU = jnp.bfloat16
WIRE = jnp.bfloat16
NDEV = 8
AXES = ("x", "y", "c")
SSD_STATE = 128
SSD_CHUNK = 64
GRID_W = 64
EPS = 1e-6
ROW_TILE = 256
LANES = 128
ADAM_LR, ADAM_B1, ADAM_B2, ADAM_EPS, ADAM_WD, ADAM_STEP = 0.001, 0.9, 0.999, 1e-08, 0.01, 10
VMEM_CAP = 56 * 2 ** 20
MESH_ID = pl.DeviceIdType.MESH


def _pick(dim, cands):
    for c in cands:
        if dim % c == 0:
            return c
    return dim


def _nbytes(shape, dtype):
    n = 1
    for s in shape:
        n *= s
    return n * jnp.dtype(dtype).itemsize


def _vmem(nbytes):
    return int(min(VMEM_CAP, max(24 * 2 ** 20, 2 * nbytes + 8 * 2 ** 20)))


def _sigmoid(x):
    return 1.0 / (1.0 + jnp.exp(-x))


def _silu(x):
    return x * _sigmoid(x)


def _dsilu(x):
    s = _sigmoid(x)
    return s * (1.0 + x * (1.0 - s))


def _softplus(x):
    return jnp.maximum(x, 0.0) + jnp.log(1.0 + jnp.exp(-jnp.abs(x)))


def _dot(a, b, dims):
    return lax.dot_general(a.astype(MXU), b.astype(MXU), (dims, ((), ())), preferred_element_type=F32)


NN, NT, TN = ((1,), (0,)), ((1,), (1,)), ((0,), (0,))


def _split(a, parts):
    out = []
    for _ in range(parts):
        p = a.astype(MXU)
        out.append(p)
        a = a - p.astype(F32)
    return out


def _dot_lx(e, a, dims, parts=2):
    return sum(lax.dot_general(e, p, (dims, ((), ())), preferred_element_type=F32) for p in _split(a, parts))


def _dot_rx(a, e, dims, parts=2):
    return sum(lax.dot_general(p, e, (dims, ((), ())), preferred_element_type=F32) for p in _split(a, parts))


def _exchange(name, gather, scatter=()):
    gather, scatter = list(gather), list(scatter)
    ng, ns = len(gather), len(scatter)
    n = ng + ns

    def body(*refs):
        ins, outs = refs[:n], refs[n:2 * n]
        send, recv, loc = refs[2 * n:]
        x, y, c = lax.axis_index("x"), lax.axis_index("y"), lax.axis_index("c")
        me, sib = (x, y, c), (x, y, 1 - c)
        chips = [(1 - x, y), (x, 1 - y), (1 - x, 1 - y)]

        def slot(p):
            return 4 * p[0] + 2 * p[1] + p[2]

        def rcopy(a, k, src, dst, to):
            return pltpu.make_async_remote_copy(src_ref=src, dst_ref=dst, send_sem=send.at[a, k], recv_sem=recv.at[a, k],
                                                device_id=to, device_id_type=MESH_ID)

        started, local = [], []
        for a in range(n):
            mine = pltpu.make_async_copy(ins[a] if a < ng else ins[a].at[slot(me)], outs[a].at[slot(me)], loc.at[a])
            mine.start()
            local.append(mine)
        rel = [(fx, fy, fc) for fx in (0, 1) for fy in (0, 1) for fc in (0, 1)][1:]
        for a in range(ng, n):
            for k, (fx, fy, fc) in enumerate(rel):
                p = (1 - x if fx else x, 1 - y if fy else y, 1 - c if fc else c)
                cp = rcopy(a, k, ins[a].at[slot(p)], outs[a].at[slot(me)], p)
                cp.start()
                started.append(cp)
        for a in range(ng):
            dst = outs[a].at[slot(me)]
            first = [rcopy(a, 0, ins[a], dst, sib)] + [rcopy(a, 1 + j, ins[a], dst, (*ch, c)) for j, ch in enumerate(chips)]
            for cp in first:
                cp.start()
            started += first
        for a in range(ng):
            for j, ch in enumerate(chips):
                blk = outs[a].at[slot((*ch, c))]
                rcopy(a, 1 + j, blk, blk, me).wait_recv()
                cp = rcopy(a, 4 + j, blk, blk, sib)
                cp.start()
                started.append(cp)
        for a in range(ng):
            blk = outs[a].at[slot(sib)]
            rcopy(a, 0, blk, blk, me).wait_recv()
            for j, ch in enumerate(chips):
                blk = outs[a].at[slot((*ch, 1 - c))]
                rcopy(a, 4 + j, blk, blk, me).wait_recv()
        for a in range(ng, n):
            for k, (fx, fy, fc) in enumerate(rel):
                p = (1 - x if fx else x, 1 - y if fy else y, 1 - c if fc else c)
                blk = outs[a].at[slot(p)]
                rcopy(a, k, blk, blk, me).wait_recv()
        for cp in started:
            cp.wait_send()
        for cp in local:
            cp.wait()

    out_shape = [jax.ShapeDtypeStruct((NDEV,) + a.shape, a.dtype) for a in gather]
    out_shape += [jax.ShapeDtypeStruct(a.shape, a.dtype) for a in scatter]
    anyspec = pl.BlockSpec(memory_space=pl.ANY)
    res = pl.pallas_call(
        body, name=name, out_shape=out_shape, in_specs=[anyspec] * n, out_specs=[anyspec] * n,
        scratch_shapes=[pltpu.SemaphoreType.DMA((n, 7)), pltpu.SemaphoreType.DMA((n, 7)), pltpu.SemaphoreType.DMA((n,))],
    )(*gather, *scatter)
    return res[:ng], res[ng:]


def _hbm(a):
    return pltpu.with_memory_space_constraint(a, pltpu.HBM)


def _divs(dim, mult):
    return [dim] + [dim // parts for parts in range(2, dim // mult + 1) if dim % parts == 0 and (dim // parts) % mult == 0]


MM_VMEM_BUDGET = 40 * 2 ** 20
GRID_STEP_US = 0.35
HBM_BYTES_PER_US = 3.0e6


def _mm_tiles(m, n, k, sizes, mode, has_add):
    sa, sb, so = sizes
    sub = 16
    best = None
    for tk in _divs(k, LANES):
        for tn in _divs(n, LANES):
            for tm in _divs(m, LANES if mode == "tn" else sub):
                nk = k // tk
                out_t = tm * tn
                est = (2 * (tm * tk * sa + tk * tn * sb) + 2 * out_t * so + 2 * (tm * tk + tk * tn) + 4 * out_t
                       + (4 * out_t if nk > 1 else 0) + (8 * out_t if has_add else 0))
                if est > MM_VMEM_BUDGET:
                    continue
                steps = (m // tm) * (n // tn) * nk
                cost = steps * GRID_STEP_US + (tm * tk * sa + tk * tn * sb + out_t * so) / HBM_BYTES_PER_US
                if best is None or cost < best[0]:
                    best = (cost, tm, tn, tk, est)
    assert best is not None, (m, n, k)
    return best[1:]


def _mm(name, a, b, mode, out_dtype=F32, bias=None, add=None):
    if mode == "nn":
        (m, k), (k2, n) = a.shape, b.shape
    elif mode == "nt":
        (m, k), (n, k2) = a.shape, b.shape
    else:
        (k, m), (k2, n) = a.shape, b.shape
    assert k == k2, (name, a.shape, b.shape)
    sizes = (a.dtype.itemsize, b.dtype.itemsize, jnp.dtype(out_dtype).itemsize)
    tm, tn, tk, est = _mm_tiles(m, n, k, sizes, mode, add is not None)
    nk = k // tk
    dims = {"nn": NN, "nt": NT, "tn": TN}[mode]
    a_spec = pl.BlockSpec((tk, tm), lambda i, j, kk: (kk, i)) if mode == "tn" else pl.BlockSpec((tm, tk), lambda i, j, kk: (i, kk))
    b_spec = pl.BlockSpec((tn, tk), lambda i, j, kk: (j, kk)) if mode == "nt" else pl.BlockSpec((tk, tn), lambda i, j, kk: (kk, j))
    extra, extra_specs = [], []
    if bias is not None:
        extra.append(bias)
        extra_specs.append(pl.BlockSpec((1, tn), lambda i, j, kk: (0, j)))
    if add is not None:
        extra.append(add)
        extra_specs.append(pl.BlockSpec((tm, tn), lambda i, j, kk: (i, j)))

    def finish(r, extras, o_ref):
        for e in extras:
            r = r + e[...].astype(F32)
        o_ref[...] = r.astype(o_ref.dtype)

    def body_acc(*refs):
        a_ref, b_ref = refs[:2]
        o_ref, acc = refs[-2:]
        kk = pl.program_id(2)

        @pl.when(kk == 0)
        def _():
            acc[...] = jnp.zeros_like(acc)

        acc[...] += _dot(a_ref[...], b_ref[...], dims)

        @pl.when(kk == nk - 1)
        def _():
            finish(acc[...], refs[2:-2], o_ref)

    def body_one(*refs):
        finish(_dot(refs[0][...], refs[1][...], dims), refs[2:-1], refs[-1])

    return pl.pallas_call(
        body_acc if nk > 1 else body_one, name=name, grid=(m // tm, n // tn, nk),
        out_shape=pltpu.HBM((m, n), out_dtype),
        in_specs=[a_spec, b_spec] + extra_specs, out_specs=pl.BlockSpec((tm, tn), lambda i, j, kk: (i, j)),
        scratch_shapes=[pltpu.VMEM((tm, tn), F32)] if nk > 1 else [],
        compiler_params=pltpu.CompilerParams(dimension_semantics=("parallel", "parallel", "arbitrary"),
                                             vmem_limit_bytes=int(min(VMEM_CAP, est + 12 * 2 ** 20))),
    )(*[_hbm(v) for v in (a, b, *extra)])


def _ri(arr, w=None, cb=0, ro=0, lead=None):
    return (arr, arr.shape[-1] if w is None else w, cb, ro, lead)


def _rowwise(name, fn, nrows, row_ins, bc_ins, outs, accs=()):
    tr = min(ROW_TILE, nrows)
    assert nrows % tr == 0
    in_specs = []
    for (arr, w, cb, ro, lead) in row_ins:
        if lead is None:
            in_specs.append(pl.BlockSpec((tr, w), lambda i, cb=cb, ro=ro: (jnp.maximum(i + ro, 0), cb)))
        else:
            in_specs.append(pl.BlockSpec((None, tr, w), lambda i, cb=cb, ro=ro, lead=lead: (lead, jnp.maximum(i + ro, 0), cb)))
    for arr in bc_ins:
        in_specs.append(pl.BlockSpec(arr.shape, lambda i, nd=arr.ndim: (0,) * nd))
    out_shape = [pltpu.HBM((nrows, c), dt) for c, dt in outs] + [pltpu.HBM(s, F32) for s in accs]
    out_specs = [pl.BlockSpec((tr, c), lambda i: (i, 0)) for c, _ in outs] + [pl.BlockSpec(s, lambda i: (0, 0)) for s in accs]
    nr, nb, no = len(row_ins), len(bc_ins), len(outs)

    def body(*refs):
        i = pl.program_id(0)
        rows = [r[...] for r in refs[:nr]]
        bcs = [r[...] for r in refs[nr:nr + nb]]
        o, a = fn(rows, bcs, i)
        for ref, val in zip(refs[nr + nb:nr + nb + no], o):
            ref[...] = val.astype(ref.dtype)
        for ref, val in zip(refs[nr + nb + no:], a):
            @pl.when(i == 0)
            def _(ref=ref, val=val):
                ref[...] = val

            @pl.when(i > 0)
            def _(ref=ref, val=val):
                ref[...] += val

    est = sum(tr * w * arr.dtype.itemsize for (arr, w, _, _, _) in row_ins) + sum(tr * c * 4 for c, _ in outs)
    res = pl.pallas_call(
        body, name=name, grid=(nrows // tr,), out_shape=out_shape, in_specs=in_specs, out_specs=out_specs,
        compiler_params=pltpu.CompilerParams(dimension_semantics=("arbitrary",), vmem_limit_bytes=_vmem(3 * est)),
    )(*[_hbm(r[0]) for r in row_ins], *[_hbm(v) for v in bc_ins])
    return res[:no], res[no:]


def _colsum(v):
    return jnp.sum(v, axis=0, keepdims=True)


def _normmod_fwd(name, h, g, s, sh, nct=0):
    d = h.shape[1]

    def fn(rows, bcs, i):
        hh, (g_, s_, sh_) = rows[0], bcs
        s1 = jnp.where(i < nct, s_[0:1], s_[1:2])
        sh1 = jnp.where(i < nct, sh_[0:1], sh_[1:2])
        r = lax.rsqrt(jnp.mean(hh * hh, axis=-1, keepdims=True) + EPS)
        return [hh * r * g_ * (1.0 + s1) + sh1], []

    return _rowwise(name, fn, h.shape[0], [_ri(h)], [g, s, sh], [(d, MXU)])[0][0]


def _normmod_bwd(name, h, dxn, dres, g, s, nct=0):
    d = h.shape[1]

    def fn(rows, bcs, i):
        hh, dx, dr = rows
        g_, s_ = bcs
        ctx = i < nct
        s1 = jnp.where(ctx, s_[0:1], s_[1:2])
        r = lax.rsqrt(jnp.mean(hh * hh, axis=-1, keepdims=True) + EPS)
        hr = hh * r
        dy = dx * (1.0 + s1)
        u = dy * g_
        dh = r * u - hr * (r * r) * jnp.mean(u * hh, axis=-1, keepdims=True)
        dh = dh + jnp.where(ctx, 0.0, dr)

        def seg(v):
            v = _colsum(v)
            return jnp.concatenate([jnp.where(ctx, v, 0.0), jnp.where(ctx, 0.0, v)], axis=0)

        return [dh], [seg(dx), seg(dx * hr * g_), seg(dy * hr)]

    (dh,), (dsh, ds, dg) = _rowwise(name, fn, h.shape[0], [_ri(h), _ri(dxn), _ri(dres, ro=-nct)], [g, s],
                                    [(d, F32)], [(2, d)] * 3)
    return dh, dsh, ds, dg


def _resnorm_fwd(name, h, y, gate, g, s, sh):
    d = h.shape[1]

    def fn(rows, bcs, i):
        hh, yy = rows
        gate_, g_, s_, sh_ = bcs
        hn = hh + gate_ * yy
        r = lax.rsqrt(jnp.mean(hn * hn, axis=-1, keepdims=True) + EPS)
        return [hn, hn * r * g_ * (1.0 + s_) + sh_], []

    return _rowwise(name, fn, h.shape[0], [_ri(h), _ri(y)], [gate, g, s, sh], [(d, F32), (d, MXU)])[0]


def _gate_bwd(name, dh, y, gate):
    d = dh.shape[1]

    def fn(rows, bcs, i):
        dd, yy = rows
        dy = dd * bcs[0]
        return [dy], [_colsum(dd * yy), _colsum(dy)]

    (dy,), (dgate, dbias) = _rowwise(name, fn, dh.shape[0], [_ri(dh), _ri(y)], [gate], [(d, MXU)], [(1, d)] * 2)
    return dy, dgate, dbias


def _swiglu_fwd(name, u):
    f = u.shape[1] // 2

    def fn(rows, bcs, i):
        return [_silu(rows[0]) * rows[1]], []

    return _rowwise(name, fn, u.shape[0], [_ri(u, f, 0), _ri(u, f, 1)], [], [(f, MXU)])[0][0]


def _swiglu_bwd(name, u, dhid):
    f = u.shape[1] // 2

    def fn(rows, bcs, i):
        a, b, dd = rows
        return [jnp.concatenate([dd * b * _dsilu(a), dd * _silu(a)], axis=1)], []

    return _rowwise(name, fn, u.shape[0], [_ri(u, f, 0), _ri(u, f, 1), _ri(dhid)], [], [(2 * f, MXU)])[0][0]


def _glu_fwd(name, u):
    d = u.shape[1] // 2

    def fn(rows, bcs, i):
        return [rows[0] * _sigmoid(rows[1])], []

    return _rowwise(name, fn, u.shape[0], [_ri(u, d, 0), _ri(u, d, 1)], [], [(d, F32)])[0][0]


def _glu_bwd(name, u, dgl):
    d = u.shape[1] // 2

    def fn(rows, bcs, i):
        a, b, dd = rows
        sg = _sigmoid(b)
        du = jnp.concatenate([dd * sg, dd * a * sg * (1.0 - sg)], axis=1)
        return [du], [_colsum(du)]

    (du,), (db,) = _rowwise(name, fn, u.shape[0], [_ri(u, d, 0), _ri(u, d, 1), _ri(dgl)], [], [(2 * d, MXU)], [(1, 2 * d)])
    return du, db


def _ln_silu_fwd(name, v, g, b):
    d = v.shape[1]

    def fn(rows, bcs, i):
        vv = rows[0]
        mu = jnp.mean(vv, axis=-1, keepdims=True)
        xc = vv - mu
        rs = lax.rsqrt(jnp.mean(xc * xc, axis=-1, keepdims=True) + EPS)
        return [_silu(xc * rs * bcs[0] + bcs[1])], []

    return _rowwise(name, fn, v.shape[0], [_ri(v)], [g, b], [(d, MXU)])[0][0]


def _ln_silu_bwd(name, v, ds, g, b):
    d = v.shape[1]

    def fn(rows, bcs, i):
        vv, dd = rows
        mu = jnp.mean(vv, axis=-1, keepdims=True)
        xc = vv - mu
        rs = lax.rsqrt(jnp.mean(xc * xc, axis=-1, keepdims=True) + EPS)
        xh = xc * rs
        dln = dd * _dsilu(xh * bcs[0] + bcs[1])
        dxh = dln * bcs[0]
        dv = rs * (dxh - jnp.mean(dxh, axis=-1, keepdims=True) - xh * jnp.mean(dxh * xh, axis=-1, keepdims=True))
        return [dv], [_colsum(dln * xh), _colsum(dln)]

    (dv,), (dg, db) = _rowwise(name, fn, v.shape[0], [_ri(v), _ri(ds)], [g, b], [(d, F32)], [(1, d)] * 2)
    return dv, dg, db


def _final_loss(name, h, f, target, gate, gf):
    d = h.shape[1]

    def fn(rows, bcs, i):
        hh, ff, tg = rows
        gate_, g_ = bcs
        hn = hh + gate_ * ff
        r = lax.rsqrt(jnp.mean(hn * hn, axis=-1, keepdims=True) + EPS)
        hr = hn * r
        err = hr * g_ - tg
        dout = err * (1.0 / d)
        u = dout * g_
        dh = r * u - hr * (r * r) * jnp.mean(u * hn, axis=-1, keepdims=True)
        sq = jnp.sum(_colsum(err * err), axis=1, keepdims=True)
        return [dh], [jnp.broadcast_to(sq, (1, LANES)), _colsum(dout * hr)]

    (dh,), (sq, dgf) = _rowwise(name, fn, h.shape[0], [_ri(h), _ri(f), _ri(target)], [gate, gf], [(d, F32)], [(1, LANES), (1, d)])
    return dh, sq, dgf


def _shift_rows(x, o, seg_lo, seg_hi, row):
    n = x.shape[0]
    if o == 0:
        return x
    sh = pltpu.roll(x, (-o) % n, 0)
    ok = (row + o >= seg_lo) & (row + o < seg_hi)
    return jnp.where(ok, sh, 0.0)


def _seg_bounds(row, tc, tt):
    ctx = row < tc
    return jnp.where(ctx, 0, tc), jnp.where(ctx, tc, tt)


def _ssd_conv_fwd(name, zx, w, b, di, tc):
    tt, kc, cd = zx.shape[0], w.shape[0], w.shape[1]
    cb = _pick(cd, (LANES,))
    off = di // cb

    def body(x_ref, w_ref, b_ref, o_ref):
        x = x_ref[...]
        row = lax.broadcasted_iota(jnp.int32, (tt, 1), 0)
        lo, hi = _seg_bounds(row, tc, tt)
        acc = jnp.broadcast_to(b_ref[...], x.shape)
        for k in range(kc):
            acc = acc + w_ref[k:k + 1, :] * _shift_rows(x, k - kc // 2, lo, hi, row)
        o_ref[...] = _silu(acc)

    return pl.pallas_call(
        body, name=name, grid=(cd // cb,), out_shape=pltpu.HBM((tt, cd), F32),
        in_specs=[pl.BlockSpec((tt, cb), lambda j: (0, j + off)), pl.BlockSpec((kc, cb), lambda j: (0, j)),
                  pl.BlockSpec((1, cb), lambda j: (0, j))],
        out_specs=pl.BlockSpec((tt, cb), lambda j: (0, j)),
        compiler_params=pltpu.CompilerParams(dimension_semantics=("parallel",), vmem_limit_bytes=_vmem(4 * tt * cb * 4)),
    )(_hbm(zx), _hbm(w), _hbm(b))


def _ssd_conv_bwd(name, zx, dact2, w, b, di, tc):
    tt, kc, cd = zx.shape[0], w.shape[0], w.shape[1]
    cb = _pick(cd, (LANES,))
    off = di // cb

    def body(x_ref, d0_ref, d1_ref, w_ref, b_ref, dx_ref, dw_ref, db_ref):
        x = x_ref[...]
        row = lax.broadcasted_iota(jnp.int32, (tt, 1), 0)
        lo, hi = _seg_bounds(row, tc, tt)
        pre = jnp.broadcast_to(b_ref[...], x.shape)
        for k in range(kc):
            pre = pre + w_ref[k:k + 1, :] * _shift_rows(x, k - kc // 2, lo, hi, row)
        dpre = (d0_ref[...] + d1_ref[...]) * _dsilu(pre)
        dx = jnp.zeros_like(x)
        for k in range(kc):
            o = k - kc // 2
            dx = dx + w_ref[k:k + 1, :] * _shift_rows(dpre, -o, lo, hi, row)
            dw_ref[k:k + 1, :] = _colsum(dpre * _shift_rows(x, o, lo, hi, row))
        dx_ref[...] = dx.astype(dx_ref.dtype)
        db_ref[...] = _colsum(dpre)

    blk = pl.BlockSpec((tt, cb), lambda j: (0, j))
    return pl.pallas_call(
        body, name=name, grid=(cd // cb,),
        out_shape=[pltpu.HBM((tt, cd), MXU), pltpu.HBM((kc, cd), F32), pltpu.HBM((1, cd), F32)],
        in_specs=[pl.BlockSpec((tt, cb), lambda j: (0, j + off)), pl.BlockSpec((None, tt, cb), lambda j: (0, 0, j)),
                  pl.BlockSpec((None, tt, cb), lambda j: (1, 0, j)), pl.BlockSpec((kc, cb), lambda j: (0, j)),
                  pl.BlockSpec((1, cb), lambda j: (0, j))],
        out_specs=[blk, pl.BlockSpec((kc, cb), lambda j: (0, j)), pl.BlockSpec((1, cb), lambda j: (0, j))],
        compiler_params=pltpu.CompilerParams(dimension_semantics=("parallel",), vmem_limit_bytes=_vmem(8 * tt * cb * 4)),
    )(_hbm(zx), _hbm(dact2), _hbm(dact2), _hbm(w), _hbm(b))


def _strided_conv(name, x, w, b, stride):
    t, ch = x.shape
    kk = w.shape[0]
    pad = (kk // 2) * stride
    cb = _pick(ch, (LANES,))
    has_b = b is not None

    def body(*refs):
        x_ref, w_ref = refs[:2]
        o_ref, xp = refs[-2:]
        xp[0:pad, :] = jnp.zeros((pad, cb), F32)
        xp[pad + t:, :] = jnp.zeros((pad, cb), F32)
        xp[pad:pad + t, :] = x_ref[...]
        acc = jnp.broadcast_to(refs[2][...], (t, cb)) if has_b else jnp.zeros((t, cb), F32)
        for k in range(kk):
            acc = acc + w_ref[k:k + 1, :] * xp[k * stride:k * stride + t, :]
        o_ref[...] = acc

    ins, specs = [x, w], [pl.BlockSpec((t, cb), lambda j: (0, j)), pl.BlockSpec((kk, cb), lambda j: (0, j))]
    if has_b:
        ins.append(b)
        specs.append(pl.BlockSpec((1, cb), lambda j: (0, j)))
    return pl.pallas_call(
        body, name=name, grid=(ch // cb,), out_shape=pltpu.HBM((t, ch), F32), in_specs=specs,
        out_specs=pl.BlockSpec((t, cb), lambda j: (0, j)), scratch_shapes=[pltpu.VMEM((t + 2 * pad, cb), F32)],
        compiler_params=pltpu.CompilerParams(dimension_semantics=("parallel",), vmem_limit_bytes=_vmem(6 * t * cb * 4)),
    )(*[_hbm(v) for v in ins])


def _strided_conv_dw(name, x, dv, kk, stride):
    t, ch = x.shape
    pad = (kk // 2) * stride
    cb = _pick(ch, (LANES,))

    def body(x_ref, d_ref, dw_ref, db_ref, xp):
        xp[0:pad, :] = jnp.zeros((pad, cb), F32)
        xp[pad + t:, :] = jnp.zeros((pad, cb), F32)
        xp[pad:pad + t, :] = x_ref[...]
        d = d_ref[...]
        for k in range(kk):
            dw_ref[k:k + 1, :] = _colsum(d * xp[k * stride:k * stride + t, :])
        db_ref[...] = _colsum(d)

    blk = pl.BlockSpec((t, cb), lambda j: (0, j))
    return pl.pallas_call(
        body, name=name, grid=(ch // cb,), out_shape=[pltpu.HBM((kk, ch), F32), pltpu.HBM((1, ch), F32)],
        in_specs=[blk, blk], out_specs=[pl.BlockSpec((kk, cb), lambda j: (0, j)), pl.BlockSpec((1, cb), lambda j: (0, j))],
        scratch_shapes=[pltpu.VMEM((t + 2 * pad, cb), F32)],
        compiler_params=pltpu.CompilerParams(dimension_semantics=("parallel",), vmem_limit_bytes=_vmem(6 * t * cb * 4)),
    )(_hbm(x), _hbm(dv))


def _grid_t(a, n1, n2):
    return a.reshape(n1, n2, a.shape[-1]).swapaxes(0, 1).reshape(n1 * n2, a.shape[-1])


def _chunk_order(d, i, ncc, nc):
    back = jnp.where(i < ncc, ncc - 1 - i, nc - 1 - (i - ncc))
    return jnp.where(d == 0, i, back)


def _ssd_chunk_setup(d, dt_raw, bias, a_log, q, h, di):
    p = di // h
    dt = _softplus(dt_raw + bias)
    a_neg = -jnp.exp(a_log)
    delta = dt * a_neg
    r = lax.broadcasted_iota(jnp.int32, (q, q), 0)
    c = lax.broadcasted_iota(jnp.int32, (q, q), 1)
    sgn = 1 - 2 * d
    mask = (r - c) * sgn >= 0
    mask_t = (c - r) * sgn >= 0
    a = _dot_lx(mask.astype(MXU), delta, NN, parts=3)
    tot = _colsum(delta)
    ea, dte, cd = jnp.exp(a), jnp.exp(tot - a), jnp.exp(tot)
    hh = lax.broadcasted_iota(jnp.int32, (h, di), 0)
    cc = lax.broadcasted_iota(jnp.int32, (h, di), 1)
    e = (cc // p == hh).astype(MXU)
    ex = _dot_rx(jnp.concatenate([dt, ea, dte, jnp.broadcast_to(cd, (8, h))], axis=0), e, NN)
    eye = (lax.broadcasted_iota(jnp.int32, (h, h), 0) == lax.broadcasted_iota(jnp.int32, (h, h), 1)).astype(MXU)
    a_t = _dot_lx(eye, a, NT, parts=3)
    return dict(dt=dt, a_neg=a_neg, a=a, a_t=a_t, mask=mask, mask_t=mask_t, e=e,
                dt_e=ex[0:q], ea_e=ex[q:2 * q], dte_e=ex[2 * q:3 * q], cd_e=ex[3 * q:3 * q + 1])


def _pick_heads(r, q, hpg, p):
    lane = lax.broadcasted_iota(jnp.int32, (q, hpg * p), 1) // p
    out = jnp.zeros((q, hpg * p), F32)
    for j in range(hpg):
        out = out + jnp.where(lane == j, r[j * q:(j + 1) * q], 0.0)
    return out


def _ssd_fwd(name, xbc, dt2, bias2, alog2, di, tc):
    tt, cd = xbc.shape
    h = dt2.shape[-1]
    q, n = SSD_CHUNK, SSD_STATE
    gn = (cd - di) // 2
    g = gn // n
    hpg, p = h // g, di // h
    gp = hpg * p
    nc, ncc = tt // q, tc // q
    assert di % gn == 0

    def body(x_ref, b_ref, c_ref, dt_ref, bias_ref, alog_ref, y_ref, hp_ref, ht):
        d, i = pl.program_id(0), pl.program_id(1)

        @pl.when(i == 0)
        def _():
            ht[...] = jnp.zeros_like(ht)

        s = _ssd_chunk_setup(d, dt_ref[...], bias_ref[...], alog_ref[...], q, h, di)
        xd = x_ref[...] * s["dt_e"]
        hp_ref[...] = ht[...]
        for gi in range(g):
            bg, cg = b_ref[:, gi * n:(gi + 1) * n].astype(MXU), c_ref[:, gi * n:(gi + 1) * n].astype(MXU)
            sl = slice(gi * gp, (gi + 1) * gp)
            sc = _dot(cg, bg, NT)
            ms = []
            for j in range(hpg):
                hd = gi * hpg + j
                seg = s["a"][:, hd:hd + 1] - s["a_t"][hd:hd + 1, :]
                ms.append(sc * jnp.exp(jnp.where(s["mask"], seg, -jnp.inf)))
            xdg = xd[:, sl]
            ydiag = _pick_heads(_dot(jnp.concatenate(ms, axis=0), xdg, NN), q, hpg, p)
            htg = ht[:, sl]
            y_ref[:, sl] = ydiag + _dot(cg, htg, NN) * s["ea_e"][:, sl]
            ht[:, sl] = s["cd_e"][:, sl] * htg + _dot(bg, xdg * s["dte_e"][:, sl], TN)

    def cidx(d, i):
        return _chunk_order(d, i, ncc, nc)

    return pl.pallas_call(
        body, name=name, grid=(2, nc),
        out_shape=[pltpu.HBM((2, tt, di), F32), pltpu.HBM((2, nc, n, di), F32)],
        in_specs=[pl.BlockSpec((q, di), lambda d, i: (cidx(d, i), 0)),
                  pl.BlockSpec((q, gn), lambda d, i: (cidx(d, i), di // gn)),
                  pl.BlockSpec((q, gn), lambda d, i: (cidx(d, i), di // gn + 1)),
                  pl.BlockSpec((None, q, h), lambda d, i: (d, cidx(d, i), 0)),
                  pl.BlockSpec((None, 1, h), lambda d, i: (d, 0, 0)),
                  pl.BlockSpec((None, 1, h), lambda d, i: (d, 0, 0))],
        out_specs=[pl.BlockSpec((None, q, di), lambda d, i: (d, cidx(d, i), 0)),
                   pl.BlockSpec((None, None, n, di), lambda d, i: (d, cidx(d, i), 0, 0))],
        scratch_shapes=[pltpu.VMEM((n, di), F32)],
        compiler_params=pltpu.CompilerParams(dimension_semantics=("arbitrary", "arbitrary"), vmem_limit_bytes=_vmem(16 * q * di * 4)),
    )(*[_hbm(v) for v in (xbc, xbc, xbc, dt2, bias2, alog2)])


def _ssd_bwd(name, xbc, dt2, bias2, alog2, dy, hp2, dskip_e, di, tc):
    tt, cd = xbc.shape
    h = dt2.shape[-1]
    q, n = SSD_CHUNK, SSD_STATE
    gn = (cd - di) // 2
    g = gn // n
    hpg, p = h // g, di // h
    gp = hpg * p
    nc, ncc = tt // q, tc // q

    def body(x_ref, b_ref, c_ref, dt_ref, bias_ref, alog_ref, dy_ref, hp_ref, dsk_ref,
             dxbc_ref, ddt_ref, dalog_ref, dbias_ref, dht, dxd, off):
        d, i = pl.program_id(0), pl.program_id(1)

        @pl.when(i == 0)
        def _():
            dht[...] = jnp.zeros_like(dht)
            dalog_ref[...] = jnp.zeros_like(dalog_ref)
            dbias_ref[...] = jnp.zeros_like(dbias_ref)

        s = _ssd_chunk_setup(d, dt_ref[...], bias_ref[...], alog_ref[...], q, h, di)
        x, dyc = x_ref[...], dy_ref[...]
        xd = x * s["dt_e"]
        dyea = dyc * s["ea_e"]
        xdte = xd * s["dte_e"]
        lane = lax.broadcasted_iota(jnp.int32, (q, gp), 1) // p
        lane_h = lax.broadcasted_iota(jnp.int32, (q, h), 1)
        da_d = jnp.zeros((q, h), F32)
        last_e = []
        for gi in range(g):
            bg, cg = b_ref[:, gi * n:(gi + 1) * n].astype(MXU), c_ref[:, gi * n:(gi + 1) * n].astype(MXU)
            sl = slice(gi * gp, (gi + 1) * gp)
            sc, sct = _dot(cg, bg, NT), _dot(bg, cg, NT)
            dyg, xdg = dyc[:, sl], xd[:, sl]
            htg, dhtg = hp_ref[:, sl], dht[:, sl]
            dystack = jnp.concatenate([jnp.where(lane == j, dyg, 0.0) for j in range(hpg)], axis=0)
            xdstack = jnp.concatenate([jnp.where(lane == j, xdg, 0.0) for j in range(hpg)], axis=0)
            gs = _dot(dystack, xdg, NT)
            gst = _dot(xdstack, dyg, NT)
            ds = jnp.zeros((q, q), F32)
            mts = []
            for j in range(hpg):
                hd = gi * hpg + j
                col, rw = s["a"][:, hd:hd + 1], s["a_t"][hd:hd + 1, :]
                gl = gs[j * q:(j + 1) * q] * jnp.exp(jnp.where(s["mask"], col - rw, -jnp.inf))
                ds = ds + gl
                mt = sct * jnp.exp(jnp.where(s["mask_t"], rw - col, -jnp.inf))
                mts.append(mt)
                da_j = jnp.sum(gl * sc, axis=1, keepdims=True) - jnp.sum(gst[j * q:(j + 1) * q] * mt, axis=1, keepdims=True)
                da_d = da_d + jnp.where(lane_h == hd, da_j, 0.0)
            dxd_diag = _pick_heads(_dot(jnp.concatenate(mts, axis=0), dyg, NN), q, hpg, p)
            z = _dot(bg, dhtg, NN) * s["dte_e"][:, sl]
            yoff = _dot(cg, htg, NN) * s["ea_e"][:, sl]
            off[:, sl] = dyg * yoff - xdg * z
            dxd[:, sl] = dxd_diag + z
            dxbc_ref[:, di + gi * n:di + (gi + 1) * n] = _dot(ds, cg, TN) + _dot(xdte[:, sl], dhtg, NT)
            dxbc_ref[:, di + gn + gi * n:di + gn + (gi + 1) * n] = _dot(ds, bg, NN) + _dot(dyea[:, sl], htg, NT)
            last_e.append(s["cd_e"][:, sl] * _colsum(dhtg * htg) + _colsum(xdg * z))
            dht[:, sl] = s["cd_e"][:, sl] * dhtg + _dot(cg, dyea[:, sl], TN)
        dxd_all = dxd[...]
        last = jnp.concatenate(last_e, axis=1)
        da = da_d + _dot_rx(off[...], s["e"], NT)
        last_h = _dot_rx(jnp.broadcast_to(last, (8, di)), s["e"], NT)[0:1]
        ddelta = _dot_lx(s["mask_t"].astype(MXU), da, NN, parts=3) + last_h
        ddt = ddelta * s["a_neg"] + _dot_rx(dxd_all * x, s["e"], NT)
        ddt_raw = ddt * _sigmoid(dt_ref[...] + bias_ref[...])
        ddt_ref[...] = ddt_raw
        dalog_ref[...] += _colsum(ddelta * s["dt"]) * s["a_neg"]
        dbias_ref[...] += _colsum(ddt_raw)
        dxbc_ref[:, 0:di] = dxd_all * s["dt_e"] + jnp.where(d == 0, dyc * dsk_ref[...], 0.0)

    def cidx(d, i):
        return _chunk_order(d, nc - 1 - i, ncc, nc)

    return pl.pallas_call(
        body, name=name, grid=(2, nc),
        out_shape=[pltpu.HBM((2, tt, cd), F32), pltpu.HBM((2, tt, h), F32),
                   pltpu.HBM((2, 1, h), F32), pltpu.HBM((2, 1, h), F32)],
        in_specs=[pl.BlockSpec((q, di), lambda d, i: (cidx(d, i), 0)),
                  pl.BlockSpec((q, gn), lambda d, i: (cidx(d, i), di // gn)),
                  pl.BlockSpec((q, gn), lambda d, i: (cidx(d, i), di // gn + 1)),
                  pl.BlockSpec((None, q, h), lambda d, i: (d, cidx(d, i), 0)),
                  pl.BlockSpec((None, 1, h), lambda d, i: (d, 0, 0)),
                  pl.BlockSpec((None, 1, h), lambda d, i: (d, 0, 0)),
                  pl.BlockSpec((q, di), lambda d, i: (cidx(d, i), 0)),
                  pl.BlockSpec((None, None, n, di), lambda d, i: (d, cidx(d, i), 0, 0)),
                  pl.BlockSpec((1, di), lambda d, i: (0, 0))],
        out_specs=[pl.BlockSpec((None, q, cd), lambda d, i: (d, cidx(d, i), 0)),
                   pl.BlockSpec((None, q, h), lambda d, i: (d, cidx(d, i), 0)),
                   pl.BlockSpec((None, 1, h), lambda d, i: (d, 0, 0)),
                   pl.BlockSpec((None, 1, h), lambda d, i: (d, 0, 0))],
        scratch_shapes=[pltpu.VMEM((n, di), F32), pltpu.VMEM((q, di), F32), pltpu.VMEM((q, di), F32)],
        compiler_params=pltpu.CompilerParams(dimension_semantics=("arbitrary", "arbitrary"), vmem_limit_bytes=_vmem(24 * q * di * 4)),
    )(*[_hbm(v) for v in (xbc, xbc, xbc, dt2, bias2, alog2, dy, hp2, dskip_e)])


def _ssd_gate_fwd(name, y2, xbc, zx, dskip_e, norm_w, di, nct, t):
    def fn(rows, bcs, i):
        yf, yb, xs, z = rows
        zg = (yf + yb + bcs[0] * xs) * _silu(z)
        rn = lax.rsqrt(jnp.mean(zg * zg, axis=-1, keepdims=True) + EPS)
        return [zg * rn * bcs[1]], []

    ins = [_ri(y2, lead=0, ro=nct), _ri(y2, lead=1, ro=nct), _ri(xbc, di, 0, ro=nct), _ri(zx, di, 0, ro=nct)]
    return _rowwise(name, fn, t, ins, [dskip_e, norm_w], [(di, MXU)])[0][0]


def _ssd_gate_bwd(name, dyn, y2, xbc, zx, dskip_e, norm_w, di, nct, tt):
    def fn(rows, bcs, i):
        dn, yf, yb, xs, z = rows
        lat = i >= nct
        ytot = yf + yb + bcs[0] * xs
        sz = _silu(z)
        zg = ytot * sz
        rn = lax.rsqrt(jnp.mean(zg * zg, axis=-1, keepdims=True) + EPS)
        u = dn * bcs[1]
        dzg = rn * u - zg * (rn * rn * rn) * jnp.mean(u * zg, axis=-1, keepdims=True)
        dy = jnp.where(lat, dzg * sz, 0.0)
        dz = jnp.where(lat, dzg * ytot * _dsilu(z), 0.0)
        return [dy, dz], [jnp.where(lat, _colsum(dn * zg * rn), 0.0), jnp.where(lat, _colsum(dy * xs), 0.0)]

    ins = [_ri(dyn, ro=-nct), _ri(y2, lead=0), _ri(y2, lead=1), _ri(xbc, di, 0), _ri(zx, di, 0)]
    (dy, dz), (dnw, ddsk) = _rowwise(name, fn, tt, ins, [dskip_e, norm_w], [(di, F32), (di, MXU)], [(1, di)] * 2)
    return dy, dz, dnw, ddsk


def _ada_fwd(name, cs, w, b):
    nl, d, c = w.shape
    r = cs.shape[0]

    def body(cs_ref, w_ref, b_ref, o_ref):
        o_ref[...] = _dot(_silu(cs_ref[...]), w_ref[...], NN) + b_ref[...]

    return pl.pallas_call(
        body, name=name, grid=(nl,), out_shape=pltpu.HBM((nl, r, c), F32),
        in_specs=[pl.BlockSpec((r, d), lambda l: (0, 0)), pl.BlockSpec((None, d, c), lambda l: (l, 0, 0)),
                  pl.BlockSpec((None, 1, c), lambda l: (l, 0, 0))],
        out_specs=pl.BlockSpec((None, r, c), lambda l: (l, 0, 0)),
        compiler_params=pltpu.CompilerParams(dimension_semantics=("parallel",), vmem_limit_bytes=_vmem(2 * d * c * 4)),
    )(_hbm(cs), _hbm(w), _hbm(b))


def _ada_bwd(name, cs, w, dmod):
    nl, d, c = w.shape
    r = cs.shape[0]

    def body(cs_ref, w_ref, dm_ref, dw_ref, dsc_ref):
        dm = dm_ref[...]
        dw_ref[...] = _dot(_silu(cs_ref[...]), dm, TN)

        @pl.when(pl.program_id(0) == 0)
        def _():
            dctx = jnp.broadcast_to(_colsum(dm[r // 2:]), (8, c))
            dsc_ref[...] = _dot(dctx, w_ref[...], NT)[0:1]

    return pl.pallas_call(
        body, name=name, grid=(nl,), out_shape=[pltpu.HBM((nl, d, c), F32), pltpu.HBM((1, d), F32)],
        in_specs=[pl.BlockSpec((r, d), lambda l: (0, 0)), pl.BlockSpec((None, d, c), lambda l: (l, 0, 0)),
                  pl.BlockSpec((None, r, c), lambda l: (l, 0, 0))],
        out_specs=[pl.BlockSpec((None, d, c), lambda l: (l, 0, 0)), pl.BlockSpec((1, d), lambda l: (0, 0))],
        compiler_params=pltpu.CompilerParams(dimension_semantics=("arbitrary",), vmem_limit_bytes=_vmem(4 * d * c * 4)),
    )(_hbm(cs), _hbm(w), _hbm(dmod))


def _adam_math(w, g, m, v):
    m = ADAM_B1 * m + (1.0 - ADAM_B1) * g
    v = ADAM_B2 * v + (1.0 - ADAM_B2) * (g * g)
    m_hat = m / (1.0 - ADAM_B1 ** ADAM_STEP)
    v_hat = v / (1.0 - ADAM_B2 ** ADAM_STEP)
    delta = -ADAM_LR * (m_hat / (jnp.sqrt(v_hat) + ADAM_EPS) + ADAM_WD * w)
    return delta, m, v


def _adam(name, slots, w, m, v):
    ns, r, c = slots.shape
    tr = _pick(r, (256, 128, 64, 32, 16, 8))

    def body(s_ref, w_ref, m_ref, v_ref, g_ref, d_ref, mo_ref, vo_ref):
        g = s_ref[0].astype(F32)
        for k in range(1, ns):
            g = g + s_ref[k].astype(F32)
        d, mn, vn = _adam_math(w_ref[...], g, m_ref[...], v_ref[...])
        g_ref[...], d_ref[...], mo_ref[...], vo_ref[...] = g, d, mn, vn

    blk = pl.BlockSpec((tr, c), lambda i: (i, 0))
    return pl.pallas_call(
        body, name=name, grid=(r // tr,), out_shape=[pltpu.HBM((r, c), F32)] * 4,
        in_specs=[pl.BlockSpec((ns, tr, c), lambda i: (0, i, 0)), blk, blk, blk], out_specs=[blk] * 4,
        compiler_params=pltpu.CompilerParams(dimension_semantics=("parallel",), vmem_limit_bytes=_vmem(16 * tr * c * 4)),
    )(_hbm(slots), _hbm(w), _hbm(m), _hbm(v))


def _adam_small(name, slots, ws, ms, vs, scale=None):
    k = len(slots)

    def body(*refs):
        s_refs, w_refs, m_refs, v_refs = refs[:k], refs[k:2 * k], refs[2 * k:3 * k], refs[3 * k:4 * k]
        sc_ref = refs[4 * k] if scale is not None else None
        outs = refs[4 * k + (scale is not None):]
        for a in range(k):
            g = s_refs[a][0]
            for j in range(1, NDEV):
                g = g + s_refs[a][j]
            if scale is not None and a == scale[0]:
                g = g * _dsilu(sc_ref[...])
            d, mn, vn = _adam_math(w_refs[a][...], g, m_refs[a][...], v_refs[a][...])
            outs[a][...], outs[k + a][...], outs[2 * k + a][...], outs[3 * k + a][...] = g, d, mn, vn

    shapes = [pltpu.HBM(w.shape, F32) for w in ws]
    extra = [scale[1]] if scale is not None else []
    ins = [*slots, *ws, *ms, *vs, *extra]

    def whole(shape):
        return pl.BlockSpec(shape, lambda i, nd=len(shape): (0,) * nd)

    res = pl.pallas_call(body, name=name, grid=(1,), out_shape=shapes * 4, in_specs=[whole(v.shape) for v in ins],
                         out_specs=[whole(s.shape) for s in shapes * 4])(*[_hbm(v) for v in ins])
    return res[:k], res[k:2 * k], res[2 * k:3 * k], res[3 * k:]


def _unshard_cols(g):
    g = jnp.moveaxis(g, 0, -2)
    return g.reshape(g.shape[:-2] + (g.shape[-2] * g.shape[-1],))


def _shard_cols(a):
    a = a.reshape(a.shape[:-1] + (NDEV, a.shape[-1] // NDEV))
    return jnp.moveaxis(a, -2, 0)


def _unshard_rows(g):
    g = jnp.moveaxis(g, 0, -3)
    return g.reshape(g.shape[:-3] + (g.shape[-3] * g.shape[-2], g.shape[-1]))


def _shard_rows(a):
    a = a.reshape(a.shape[:-2] + (NDEV, a.shape[-2] // NDEV, a.shape[-1]))
    return jnp.moveaxis(a, -3, 0)


def _flat2(a):
    return a.reshape((-1, a.shape[-1]))


def kernel(x, c, ctx, c_ctx, ada_w, ada_b, norm_mix_g, norm_ffn_g, final_norm_g, ssd_w_in, ssd_conv_w, ssd_conv_b, ssd_dt_bias_f, ssd_dt_bias_b, ssd_a_log_f, ssd_a_log_b, ssd_d_skip, ssd_norm_w, ssd_w_out, conf_w_pw1, conf_b_pw1, conf_dw_w, conf_dw_b, conf_ln_g, conf_ln_b, conf_w_pw2, conf_b_pw2, ffn_w_in, ffn_w_out, loss_target, m_c_ctx, m_ada_w, m_ada_b, m_norm_mix_g, m_norm_ffn_g, m_final_norm_g, m_ssd_w_in, m_ssd_conv_w, m_ssd_conv_b, m_ssd_dt_bias_f, m_ssd_dt_bias_b, m_ssd_a_log_f, m_ssd_a_log_b, m_ssd_d_skip, m_ssd_norm_w, m_ssd_w_out, m_conf_w_pw1, m_conf_b_pw1, m_conf_dw_w, m_conf_dw_b, m_conf_ln_g, m_conf_ln_b, m_conf_w_pw2, m_conf_b_pw2, m_ffn_w_in, m_ffn_w_out, v_c_ctx, v_ada_w, v_ada_b, v_norm_mix_g, v_norm_ffn_g, v_final_norm_g, v_ssd_w_in, v_ssd_conv_w, v_ssd_conv_b, v_ssd_dt_bias_f, v_ssd_dt_bias_b, v_ssd_a_log_f, v_ssd_a_log_b, v_ssd_d_skip, v_ssd_norm_w, v_ssd_w_out, v_conf_w_pw1, v_conf_b_pw1, v_conf_dw_w, v_conf_dw_b, v_conf_ln_g, v_conf_ln_b, v_conf_w_pw2, v_conf_b_pw2, v_ffn_w_in, v_ffn_w_out):
    args = dict(locals())
    names = ['c_ctx', 'ada_w', 'ada_b', 'norm_mix_g', 'norm_ffn_g', 'final_norm_g', 'ssd_w_in', 'ssd_conv_w', 'ssd_conv_b',
             'ssd_dt_bias_f', 'ssd_dt_bias_b', 'ssd_a_log_f', 'ssd_a_log_b', 'ssd_d_skip', 'ssd_norm_w', 'ssd_w_out',
             'conf_w_pw1', 'conf_b_pw1', 'conf_dw_w', 'conf_dw_b', 'conf_ln_g', 'conf_ln_b', 'conf_w_pw2', 'conf_b_pw2',
             'ffn_w_in', 'ffn_w_out']
    me = 4 * lax.axis_index("x") + 2 * lax.axis_index("y") + lax.axis_index("c")
    t, d = x.shape[1], x.shape[2]
    tc = ctx.shape[1]
    tt = tc + t
    nct = tc // ROW_TILE
    assert tc % ROW_TILE == 0 and t % ROW_TILE == 0
    h = ssd_dt_bias_f.shape[-1]
    di = ssd_norm_w.shape[-1]
    cdim = ssd_conv_b.shape[-1]
    kc = ssd_conv_w.shape[1]
    ck = conf_dw_w.shape[1]
    ch = d // 2
    rows_g = t // GRID_W
    nl = ada_w.shape[0]
    cw = ada_w.shape[2]
    x2, ctx2, tgt = x[0], ctx[0], loss_target[0]

    gat, _ = _exchange("gather_weights", [
        c, ssd_w_in[0].astype(WIRE), ssd_w_out[0].astype(WIRE), conf_w_pw1[0].astype(WIRE), conf_w_pw2[0].astype(WIRE),
        ffn_w_in.astype(WIRE), ffn_w_out.astype(WIRE), ssd_conv_w[0], conf_b_pw1, conf_dw_w[0], conf_dw_b, conf_ln_g,
        conf_ln_b, conf_b_pw2])
    (c_all, w_in_g, w_out_g, pw1_g, pw2_g, fin_g, fout_g, convw_g, bpw1_g, dww_g, dwb_g, lng_g, lnb_g, bpw2_g) = gat
    w_ssd_in = _unshard_cols(w_in_g)
    w_zx = w_ssd_in[:, :di + cdim]
    w_z, w_xbc = w_zx[:, :di], w_zx[:, di:]
    w_dt = jnp.pad(w_ssd_in[:, di + cdim:], ((0, 0), (0, LANES - 2 * h)))
    w_ssd_out = _unshard_rows(w_out_g)
    w_pw1, w_pw2 = _unshard_cols(pw1_g), _unshard_rows(pw2_g)
    w_fin, w_fout = _unshard_cols(fin_g), _unshard_rows(fout_g)
    conv_w_full, dw_w_full = _unshard_cols(convw_g), _unshard_cols(dww_g)
    b_pw1, dw_b, ln_g, ln_b, b_pw2 = (_unshard_cols(a) for a in (bpw1_g, dwb_g, lng_g, lnb_g, bpw2_g))

    cs_all = jnp.concatenate([c_all[:, 0, :], jnp.broadcast_to(c_ctx[None, :], (NDEV, d))], axis=0)
    ada_b_mine = lax.dynamic_slice_in_dim(ada_b, me * cw, cw, axis=1)[:, None, :]
    mod_part = _ada_fwd("ada_fwd", cs_all, ada_w, ada_b_mine)
    (mod_g,), _ = _exchange("gather_mod", [mod_part])
    mod_all = jnp.moveaxis(mod_g, 0, 2).reshape(nl, 2 * NDEV, NDEV * cw)
    mod_lat = lax.dynamic_slice_in_dim(mod_all, me, 1, axis=1)[:, 0, :]
    mod_ctx = mod_all[0, NDEV, :]

    def six(v):
        return [v[k * d:(k + 1) * d][None, :] for k in range(6)]

    sh1, s1, g1, sh2, s2, g2 = six(mod_lat[0])
    csh1, cs1 = six(mod_ctx)[:2]
    sh1b, s1b, g1b, sh2b, s2b, g2b = six(mod_lat[1])
    nmg, nfg = norm_mix_g, norm_ffn_g

    h_all = jnp.concatenate([ctx2, x2], axis=0)
    s01, sh01 = jnp.concatenate([cs1, s1], axis=0), jnp.concatenate([csh1, sh1], axis=0)
    xn_all = _normmod_fwd("l0_norm", h_all, nmg[0:1], s01, sh01, nct)
    zx = _mm("ssd_in_proj", xn_all, w_zx, "nn")
    dtr = _mm("ssd_dt_proj", xn_all, w_dt, "nn")
    dt2 = jnp.moveaxis(dtr[:, :2 * h].reshape(tt, 2, h), 1, 0)
    bias2 = jnp.stack([ssd_dt_bias_f, ssd_dt_bias_b])
    alog2 = jnp.stack([ssd_a_log_f, ssd_a_log_b])
    xbc = _ssd_conv_fwd("ssd_conv", zx, conv_w_full, ssd_conv_b, di, tc)
    y2, hp2 = _ssd_fwd("ssd_scan", xbc, dt2, bias2, alog2, di, tc)
    dskip_e = jnp.repeat(ssd_d_skip, di // h, axis=1)
    yn = _ssd_gate_fwd("ssd_gate", y2, xbc, zx, dskip_e, ssd_norm_w, di, nct, t)
    mix0 = _mm("ssd_out_proj", yn, w_ssd_out, "nn")
    h1, xf0 = _resnorm_fwd("l0_res_norm", x2, mix0, g1, nfg[0:1], s2, sh2)
    u0 = _mm("ffn0_in", xf0, w_fin[0], "nn")
    hid0 = _swiglu_fwd("ffn0_act", u0)
    f0 = _mm("ffn0_out", hid0, w_fout[0], "nn")
    h2, xn1 = _resnorm_fwd("l1_norm", h1, f0, g2, nmg[1:2], s1b, sh1b)
    u1 = _mm("conf_pw1", xn1, w_pw1, "nn", bias=b_pw1)
    gl = _glu_fwd("conf_glu", u1)
    gl_h = _grid_t(gl[:, :ch], rows_g, GRID_W)
    gl_v = gl[:, ch:]
    v_h = _strided_conv("conf_conv_h", gl_h, dw_w_full[:, :ch], dw_b[:, :ch], rows_g)
    v_v = _strided_conv("conf_conv_v", gl_v, dw_w_full[:, ch:], dw_b[:, ch:], GRID_W)
    v = jnp.concatenate([_grid_t(v_h, GRID_W, rows_g), v_v], axis=1)
    sl = _ln_silu_fwd("conf_ln", v, ln_g, ln_b)
    mix1 = _mm("conf_pw2", sl, w_pw2, "nn", bias=b_pw2)
    h3, xf1 = _resnorm_fwd("l1_res_norm", h2, mix1, g1b, nfg[1:2], s2b, sh2b)
    u2 = _mm("ffn1_in", xf1, w_fin[1], "nn")
    hid1 = _swiglu_fwd("ffn1_act", u2)
    f1 = _mm("ffn1_out", hid1, w_fout[1], "nn")
    dh, sq, d_final_g = _final_loss("final_loss", h3, f1, tgt, g2b, final_norm_g[None, :])
    loss = lax.psum(0.5 * sq[0, 0] / d, AXES)

    zero2 = jnp.zeros((2, d), F32)

    def ffn_bwd(tag, dh, hin, xf, u, hid, f, gate, w_in, w_out, g_norm, s_mod):
        df, dgate, _ = _gate_bwd(tag + "_gate_bwd", dh, f, gate)
        dhid = _mm(tag + "_dhid", df, w_out, "nt")
        dw_out = _mm(tag + "_dwout", hid, df, "tn", WIRE)
        du = _swiglu_bwd(tag + "_act_bwd", u, dhid)
        dw_in = _mm(tag + "_dwin", xf, du, "tn", WIRE)
        dxf = _mm(tag + "_dx", du, w_in, "nt")
        s_2 = jnp.concatenate([s_mod, s_mod], axis=0)
        dh, dsh, ds, dg = _normmod_bwd(tag + "_norm_bwd", hin, dxf, dh, g_norm, s_2)
        return dh, dgate, dsh[1:2], ds[1:2], dg[1:2], dw_in, dw_out

    dh, d_g2b, d_sh2b, d_s2b, d_nfg1, g_fin1, g_fout1 = ffn_bwd("ffn1", dh, h3, xf1, u2, hid1, f1, g2b, w_fin[1], w_fout[1], nfg[1:2], s2b)
    dmix1, d_g1b, g_bpw2 = _gate_bwd("conf_gate_bwd", dh, mix1, g1b)
    dsl = _mm("conf_dsl", dmix1, w_pw2, "nt")
    g_pw2 = _mm("conf_dwpw2", sl, dmix1, "tn", WIRE)
    dv, g_lng, g_lnb = _ln_silu_bwd("conf_ln_bwd", v, dsl, ln_g, ln_b)
    dv_h, dv_v = _grid_t(dv[:, :ch], rows_g, GRID_W), dv[:, ch:]
    w_flip = dw_w_full[::-1]
    dgl_h = _strided_conv("conf_conv_h_bwd", dv_h, w_flip[:, :ch], None, rows_g)
    dgl_v = _strided_conv("conf_conv_v_bwd", dv_v, w_flip[:, ch:], None, GRID_W)
    g_dww_h, g_dwb_h = _strided_conv_dw("conf_conv_h_dw", gl_h, dv_h, ck, rows_g)
    g_dww_v, g_dwb_v = _strided_conv_dw("conf_conv_v_dw", gl_v, dv_v, ck, GRID_W)
    g_dww, g_dwb = jnp.concatenate([g_dww_h, g_dww_v], axis=1), jnp.concatenate([g_dwb_h, g_dwb_v], axis=1)
    dgl = jnp.concatenate([_grid_t(dgl_h, GRID_W, rows_g), dgl_v], axis=1)
    du1, g_bpw1 = _glu_bwd("conf_glu_bwd", u1, dgl)
    g_pw1 = _mm("conf_dwpw1", xn1, du1, "tn", WIRE)
    dxn1 = _mm("conf_dx", du1, w_pw1, "nt")
    dh, dsh_, ds_, dg_ = _normmod_bwd("l1_norm_bwd", h2, dxn1, dh, nmg[1:2], jnp.concatenate([s1b, s1b], axis=0))
    d_sh1b, d_s1b, d_nmg1 = dsh_[1:2], ds_[1:2], dg_[1:2]
    dh, d_g2, d_sh2, d_s2, d_nfg0, g_fin0, g_fout0 = ffn_bwd("ffn0", dh, h1, xf0, u0, hid0, f0, g2, w_fin[0], w_fout[0], nfg[0:1], s2)
    dmix0, d_g1, _ = _gate_bwd("ssd_gate_res_bwd", dh, mix0, g1)
    dyn = _mm("ssd_dyn", dmix0, w_ssd_out, "nt")
    g_ssd_out = _mm("ssd_dwout", yn, dmix0, "tn", WIRE)
    dy, dz, g_normw, ddsk_e = _ssd_gate_bwd("ssd_gate_bwd", dyn, y2, xbc, zx, dskip_e, ssd_norm_w, di, nct, tt)
    dxbc2, ddt2, g_alog2, g_bias2 = _ssd_bwd("ssd_scan_bwd", xbc, dt2, bias2, alog2, dy, hp2, dskip_e, di, tc)
    dxr, g_convw, g_convb = _ssd_conv_bwd("ssd_conv_bwd", zx, dxbc2, conv_w_full, ssd_conv_b, di, tc)
    ddt_p = jnp.pad(jnp.moveaxis(ddt2, 0, 1).reshape(tt, 2 * h), ((0, 0), (0, LANES - 2 * h))).astype(MXU)
    dxn = _mm("ssd_dx_z", dz, w_z, "nt")
    dxn = _mm("ssd_dx_xbc", dxr, w_xbc, "nt", add=dxn)
    dxn = _mm("ssd_dx_dt", ddt_p, w_dt, "nt", add=dxn)
    g_ssd_in = jnp.concatenate([_mm("ssd_dw_z", xn_all, dz, "tn", WIRE), _mm("ssd_dw_xbc", xn_all, dxr, "tn", WIRE),
                                _mm("ssd_dw_dt", xn_all, ddt_p, "tn", WIRE)[:, :2 * h]], axis=1)
    dh_all, dsh_, ds_, dg_ = _normmod_bwd("l0_norm_bwd", h_all, dxn, dh, nmg[0:1], s01, nct)
    grad_x = dh_all[tc:][None]
    d_csh1, d_sh1, d_cs1, d_s1 = dsh_[0:1], dsh_[1:2], ds_[0:1], ds_[1:2]
    d_nmg0 = dg_[0:1] + dg_[1:2]

    z1 = jnp.zeros((1, d), F32)
    dmod = jnp.concatenate([jnp.concatenate([d_sh1, d_s1, d_g1, d_sh2, d_s2, d_g2], axis=1),
                            jnp.concatenate([d_sh1b, d_s1b, d_g1b, d_sh2b, d_s2b, d_g2b], axis=1),
                            jnp.concatenate([d_csh1, d_cs1, z1, z1, z1, z1], axis=1)], axis=0)
    (dmod_g,), _ = _exchange("gather_dmod", [dmod])
    dmod_mine = lax.dynamic_slice_in_dim(dmod_g, me * cw, cw, axis=2)
    dmod16 = jnp.stack([jnp.concatenate([dmod_mine[:, 0], dmod_mine[:, 2]], axis=0),
                        jnp.concatenate([dmod_mine[:, 1], jnp.zeros((NDEV, cw), F32)], axis=0)])
    g_ada_w, dsc_part = _ada_bwd("ada_bwd", cs_all, ada_w, dmod16)
    g_ada_b = dmod[0:2] + jnp.concatenate([dmod[2:3], jnp.zeros((1, 6 * d), F32)], axis=0)

    big = [_shard_cols(g_ssd_in), _shard_rows(g_ssd_out), _shard_cols(g_pw1), _shard_rows(g_pw2),
           _flat3(_shard_cols(jnp.stack([g_fin0, g_fin1]))), _flat3(_shard_rows(jnp.stack([g_fout0, g_fout1])))]
    big = [b.astype(WIRE) for b in big]
    small_sh = [_shard_cols(g_convw), _shard_cols(g_bpw1), _shard_cols(g_dww), _shard_cols(g_dwb), _shard_cols(g_lng),
                _shard_cols(g_lnb), _shard_cols(g_bpw2)]
    d_dskip = jnp.sum(ddsk_e.reshape(h, di // h), axis=1)[None, :]
    rep = [dsc_part, g_ada_b, jnp.concatenate([d_nmg0, d_nmg1], axis=0), jnp.concatenate([d_nfg0, d_nfg1], axis=0),
           d_final_g, g_convb, g_bias2[0], g_bias2[1], g_alog2[0], g_alog2[1], d_dskip, g_normw]
    rep_g, sc_g = _exchange("exchange_grads", rep, big + small_sh)
    big_r, small_r = sc_g[:len(big)], sc_g[len(big):]

    out = {}

    def put(name, res):
        w = args[name]
        out["grad_" + name], out["delta_" + name], out["new_m_" + name], out["new_v_" + name] = (r.reshape(w.shape) for r in res)

    for name, slots in zip(["ssd_w_in", "ssd_w_out", "conf_w_pw1", "conf_w_pw2", "ffn_w_in", "ffn_w_out"], big_r):
        put(name, _adam("adam_" + name, slots, _flat2(args[name]), _flat2(args["m_" + name]), _flat2(args["v_" + name])))
    put("ada_w", _adam("adam_ada_w", _flat2(g_ada_w)[None], _flat2(ada_w), _flat2(m_ada_w), _flat2(v_ada_w)))
    small_names = ["ssd_conv_w", "conf_b_pw1", "conf_dw_w", "conf_dw_b", "conf_ln_g", "conf_ln_b", "conf_b_pw2",
                   "c_ctx", "ada_b", "norm_mix_g", "norm_ffn_g", "final_norm_g", "ssd_conv_b", "ssd_dt_bias_f", "ssd_dt_bias_b",
                   "ssd_a_log_f", "ssd_a_log_b", "ssd_d_skip", "ssd_norm_w"]
    slots = list(small_r) + list(rep_g)

    def as2(a):
        return a.reshape((1, -1)) if a.ndim == 1 else _flat2(a)

    res = _adam_small("adam_small", slots, [as2(args[n]) for n in small_names], [as2(args["m_" + n]) for n in small_names],
                      [as2(args["v_" + n]) for n in small_names], scale=(small_names.index("c_ctx"), c_ctx[None, :]))
    for k, name in enumerate(small_names):
        put(name, [r[k] for r in res])
    return (loss, grad_x, *[out["grad_" + n] for n in names], *[out["delta_" + n] for n in names],
            *[out["new_m_" + n] for n in names], *[out["new_v_" + n] for n in names])


def _flat3(a):
    return a.reshape((a.shape[0], -1, a.shape[-1]))
```

```python
import functools

import jax
import jax.numpy as jnp
from jax import lax
from jax.experimental import pallas as pl
from jax.experimental.pallas import tpu as pltpu

F32 = jnp.float32
MXU = jnp.bfloat16
WIRE = jnp.bfloat16
NDEV = 8
AXES = ("x", "y", "c")
SSD_STATE = 128
SSD_CHUNK = 64
GRID_W = 64
EPS = 1e-6
ROW_TILE = 256
LANES = 128
ADAM_LR, ADAM_B1, ADAM_B2, ADAM_EPS, ADAM_WD, ADAM_STEP = 0.001, 0.9, 0.999, 1e-08, 0.01, 10
VMEM_CAP = 56 * 2 ** 20
MESH_ID = pl.DeviceIdType.MESH


def _pick(dim, cands):
    for c in cands:
        if dim % c == 0:
            return c
    return dim


def _nbytes(shape, dtype):
    n = 1
    for s in shape:
        n *= s
    return n * jnp.dtype(dtype).itemsize


def _vmem(nbytes):
    return int(min(VMEM_CAP, max(24 * 2 ** 20, 2 * nbytes + 8 * 2 ** 20)))


def _sigmoid(x):
    return 1.0 / (1.0 + jnp.exp(-x))


def _silu(x):
    return x * _sigmoid(x)


def _dsilu(x):
    s = _sigmoid(x)
    return s * (1.0 + x * (1.0 - s))


def _softplus(x):
    return jnp.maximum(x, 0.0) + jnp.log(1.0 + jnp.exp(-jnp.abs(x)))


def _dot(a, b, dims):
    return lax.dot_general(a.astype(MXU), b.astype(MXU), (dims, ((), ())), preferred_element_type=F32)


NN, NT, TN = ((1,), (0,)), ((1,), (1,)), ((0,), (0,))


def _split(a, parts):
    out = []
    for _ in range(parts):
        p = a.astype(MXU)
        out.append(p)
        a = a - p.astype(F32)
    return out


def _dot_lx(e, a, dims, parts=2):
    return sum(lax.dot_general(e, p, (dims, ((), ())), preferred_element_type=F32) for p in _split(a, parts))


def _dot_rx(a, e, dims, parts=2):
    return sum(lax.dot_general(p, e, (dims, ((), ())), preferred_element_type=F32) for p in _split(a, parts))


class _Comm:
    def __init__(self, gather=(), scatter=()):
        self.gather, self.scatter = list(gather), list(scatter)
        self.ng, self.n = len(self.gather), len(self.gather) + len(self.scatter)
        self.operands = self.gather + self.scatter
        self.specs = [pl.BlockSpec(memory_space=pl.ANY)] * self.n
        self.out_shape = ([jax.ShapeDtypeStruct((NDEV,) + a.shape, a.dtype) for a in self.gather]
                          + [jax.ShapeDtypeStruct(a.shape, a.dtype) for a in self.scatter])
        self.scratch = [pltpu.SemaphoreType.DMA((self.n, 7)), pltpu.SemaphoreType.DMA((self.n, 7)),
                        pltpu.SemaphoreType.DMA((self.n,))]

    def split(self, res):
        return res[:self.ng], res[self.ng:]

    def _copies(self, ins, outs, sems):
        send, recv, loc = sems
        ng, n = self.ng, self.n
        x, y, c = lax.axis_index("x"), lax.axis_index("y"), lax.axis_index("c")
        me, sib = (x, y, c), (x, y, 1 - c)
        chips = [(1 - x, y), (x, 1 - y), (1 - x, 1 - y)]

        def slot(p):
            return 4 * p[0] + 2 * p[1] + p[2]

        def rcopy(a, k, src, dst, to):
            return functools.partial(pltpu.make_async_remote_copy, src_ref=src, dst_ref=dst, send_sem=send.at[a, k],
                                     recv_sem=recv.at[a, k], device_id=to, device_id_type=MESH_ID)

        local = [functools.partial(pltpu.make_async_copy, ins[a] if a < ng else ins[a].at[slot(me)], outs[a].at[slot(me)],
                                   loc.at[a]) for a in range(n)]
        rel = [(fx, fy, fc) for fx in (0, 1) for fy in (0, 1) for fc in (0, 1)][1:]
        first, landed, passed = [], [], []
        for a in range(ng, n):
            for k, (fx, fy, fc) in enumerate(rel):
                p = (1 - x if fx else x, 1 - y if fy else y, 1 - c if fc else c)
                first.append(rcopy(a, k, ins[a].at[slot(p)], outs[a].at[slot(me)], p))
                blk = outs[a].at[slot(p)]
                landed.append(rcopy(a, k, blk, blk, me))
        for a in range(ng):
            dst = outs[a].at[slot(me)]
            first.append(rcopy(a, 0, ins[a], dst, sib))
            first += [rcopy(a, 1 + j, ins[a], dst, (*ch, c)) for j, ch in enumerate(chips)]
            blk = outs[a].at[slot(sib)]
            landed.append(rcopy(a, 0, blk, blk, me))
            for j, ch in enumerate(chips):
                blk = outs[a].at[slot((*ch, c))]
                passed.append((rcopy(a, 1 + j, blk, blk, me), rcopy(a, 4 + j, blk, blk, sib)))
                blk = outs[a].at[slot((*ch, 1 - c))]
                landed.append(rcopy(a, 4 + j, blk, blk, me))
        return local, first, passed, landed

    def start(self, ins, outs, sems):
        local, first, _, _ = self._copies(ins, outs, sems)
        for make in local + first:
            make().start()

    def finish(self, ins, outs, sems):
        local, first, passed, landed = self._copies(ins, outs, sems)
        onward = []
        for arrived, forward in passed:
            arrived().wait_recv()
            onward.append(forward())
            onward[-1].start()
        for make in landed:
            make().wait_recv()
        for make in first:
            make().wait_send()
        for cp in onward:
            cp.wait_send()
        for make in local:
            make().wait()


def _carry(body, comm, n_in, n_out, grid):
    if comm is None:
        return body
    n = comm.n

    def wrapped(*refs):
        own_in, c_in = refs[:n_in], refs[n_in:n_in + n]
        own_out, c_out = refs[n_in + n:n_in + n + n_out], refs[n_in + n + n_out:n_in + 2 * n + n_out]
        own_scr, sems = refs[n_in + 2 * n + n_out:-3], refs[-3:]
        ids = [pl.program_id(ax) for ax in range(len(grid))]
        first, last = ids[0] == 0, ids[0] == grid[0] - 1
        for ax in range(1, len(grid)):
            first, last = first & (ids[ax] == 0), last & (ids[ax] == grid[ax] - 1)

        @pl.when(first)
        def _():
            comm.start(c_in, c_out, sems)

        body(*own_in, *own_out, *own_scr)

        @pl.when(last)
        def _():
            comm.finish(c_in, c_out, sems)

    return wrapped


def _exchange(name, gather, scatter=()):
    comm = _Comm(gather, scatter)
    n = comm.n

    def body(*refs):
        ins, outs, sems = refs[:n], refs[n:2 * n], refs[2 * n:]
        comm.start(ins, outs, sems)
        comm.finish(ins, outs, sems)

    res = pl.pallas_call(body, name=name, out_shape=comm.out_shape, in_specs=comm.specs, out_specs=comm.specs,
                         scratch_shapes=comm.scratch)(*comm.operands)
    return comm.split(res)


def _hbm(a):
    return pltpu.with_memory_space_constraint(a, pltpu.HBM)


def _divs(dim, mult):
    return [dim] + [dim // parts for parts in range(2, dim // mult + 1) if dim % parts == 0 and (dim // parts) % mult == 0]


MM_VMEM_BUDGET = 40 * 2 ** 20
GRID_STEP_US = 0.35
HBM_BYTES_PER_US = 3.0e6


def _mm_tiles(m, n, k, sizes, mode, has_add):
    sa, sb, so = sizes
    sub = 16
    best = None
    for tk in _divs(k, LANES):
        for tn in _divs(n, LANES):
            for tm in _divs(m, LANES if mode == "tn" else sub):
                nk = k // tk
                out_t = tm * tn
                est = (2 * (tm * tk * sa + tk * tn * sb) + 2 * out_t * so + 2 * (tm * tk + tk * tn) + 4 * out_t
                       + (4 * out_t if nk > 1 else 0) + (8 * out_t if has_add else 0))
                if est > MM_VMEM_BUDGET:
                    continue
                steps = (m // tm) * (n // tn) * nk
                cost = steps * GRID_STEP_US + (tm * tk * sa + tk * tn * sb + out_t * so) / HBM_BYTES_PER_US
                if best is None or cost < best[0]:
                    best = (cost, tm, tn, tk, est)
    assert best is not None, (m, n, k)
    return best[1:]


def _mm(name, a, b, mode, out_dtype=F32, bias=None, add=None):
    if mode == "nn":
        (m, k), (k2, n) = a.shape, b.shape
    elif mode == "nt":
        (m, k), (n, k2) = a.shape, b.shape
    else:
        (k, m), (k2, n) = a.shape, b.shape
    assert k == k2, (name, a.shape, b.shape)
    sizes = (a.dtype.itemsize, b.dtype.itemsize, jnp.dtype(out_dtype).itemsize)
    tm, tn, tk, est = _mm_tiles(m, n, k, sizes, mode, add is not None)
    nk = k // tk
    dims = {"nn": NN, "nt": NT, "tn": TN}[mode]
    a_spec = pl.BlockSpec((tk, tm), lambda i, j, kk: (kk, i)) if mode == "tn" else pl.BlockSpec((tm, tk), lambda i, j, kk: (i, kk))
    b_spec = pl.BlockSpec((tn, tk), lambda i, j, kk: (j, kk)) if mode == "nt" else pl.BlockSpec((tk, tn), lambda i, j, kk: (kk, j))
    extra, extra_specs = [], []
    if bias is not None:
        extra.append(bias)
        extra_specs.append(pl.BlockSpec((1, tn), lambda i, j, kk: (0, j)))
    if add is not None:
        extra.append(add)
        extra_specs.append(pl.BlockSpec((tm, tn), lambda i, j, kk: (i, j)))

    def finish(r, extras, o_ref):
        for e in extras:
            r = r + e[...].astype(F32)
        o_ref[...] = r.astype(o_ref.dtype)

    def body_acc(*refs):
        a_ref, b_ref = refs[:2]
        o_ref, acc = refs[-2:]
        kk = pl.program_id(2)

        @pl.when(kk == 0)
        def _():
            acc[...] = jnp.zeros_like(acc)

        acc[...] += _dot(a_ref[...], b_ref[...], dims)

        @pl.when(kk == nk - 1)
        def _():
            finish(acc[...], refs[2:-2], o_ref)

    def body_one(*refs):
        finish(_dot(refs[0][...], refs[1][...], dims), refs[2:-1], refs[-1])

    return pl.pallas_call(
        body_acc if nk > 1 else body_one, name=name, grid=(m // tm, n // tn, nk),
        out_shape=pltpu.HBM((m, n), out_dtype),
        in_specs=[a_spec, b_spec] + extra_specs, out_specs=pl.BlockSpec((tm, tn), lambda i, j, kk: (i, j)),
        scratch_shapes=[pltpu.VMEM((tm, tn), F32)] if nk > 1 else [],
        compiler_params=pltpu.CompilerParams(dimension_semantics=("parallel", "parallel", "arbitrary"),
                                             vmem_limit_bytes=int(min(VMEM_CAP, est + 12 * 2 ** 20))),
    )(*[_hbm(v) for v in (a, b, *extra)])


def _ri(arr, w=None, cb=0, ro=0, lead=None):
    return (arr, arr.shape[-1] if w is None else w, cb, ro, lead)


def _rowwise(name, fn, nrows, row_ins, bc_ins, outs, accs=()):
    tr = min(ROW_TILE, nrows)
    assert nrows % tr == 0
    in_specs = []
    for (arr, w, cb, ro, lead) in row_ins:
        if lead is None:
            in_specs.append(pl.BlockSpec((tr, w), lambda i, cb=cb, ro=ro: (jnp.maximum(i + ro, 0), cb)))
        else:
            in_specs.append(pl.BlockSpec((None, tr, w), lambda i, cb=cb, ro=ro, lead=lead: (lead, jnp.maximum(i + ro, 0), cb)))
    for arr in bc_ins:
        in_specs.append(pl.BlockSpec(arr.shape, lambda i, nd=arr.ndim: (0,) * nd))
    out_shape = [pltpu.HBM((nrows, c), dt) for c, dt in outs] + [pltpu.HBM(s, F32) for s in accs]
    out_specs = [pl.BlockSpec((tr, c), lambda i: (i, 0)) for c, _ in outs] + [pl.BlockSpec(s, lambda i: (0, 0)) for s in accs]
    nr, nb, no = len(row_ins), len(bc_ins), len(outs)

    def body(*refs):
        i = pl.program_id(0)
        rows = [r[...] for r in refs[:nr]]
        bcs = [r[...] for r in refs[nr:nr + nb]]
        o, a = fn(rows, bcs, i)
        for ref, val in zip(refs[nr + nb:nr + nb + no], o):
            ref[...] = val.astype(ref.dtype)
        for ref, val in zip(refs[nr + nb + no:], a):
            @pl.when(i == 0)
            def _(ref=ref, val=val):
                ref[...] = val

            @pl.when(i > 0)
            def _(ref=ref, val=val):
                ref[...] += val

    est = sum(tr * w * arr.dtype.itemsize for (arr, w, _, _, _) in row_ins) + sum(tr * c * 4 for c, _ in outs)
    res = pl.pallas_call(
        body, name=name, grid=(nrows // tr,), out_shape=out_shape, in_specs=in_specs, out_specs=out_specs,
        compiler_params=pltpu.CompilerParams(dimension_semantics=("arbitrary",), vmem_limit_bytes=_vmem(3 * est)),
    )(*[_hbm(r[0]) for r in row_ins], *[_hbm(v) for v in bc_ins])
    return res[:no], res[no:]


def _colsum(v):
    return jnp.sum(v, axis=0, keepdims=True)


def _normmod_fwd(name, h, g, s, sh, nct=0):
    d = h.shape[1]

    def fn(rows, bcs, i):
        hh, (g_, s_, sh_) = rows[0], bcs
        s1 = jnp.where(i < nct, s_[0:1], s_[1:2])
        sh1 = jnp.where(i < nct, sh_[0:1], sh_[1:2])
        r = lax.rsqrt(jnp.mean(hh * hh, axis=-1, keepdims=True) + EPS)
        return [hh * r * g_ * (1.0 + s1) + sh1], []

    return _rowwise(name, fn, h.shape[0], [_ri(h)], [g, s, sh], [(d, MXU)])[0][0]


def _normmod_bwd(name, h, dxn, dres, g, s, nct=0):
    d = h.shape[1]

    def fn(rows, bcs, i):
        hh, dx, dr = rows
        g_, s_ = bcs
        ctx = i < nct
        s1 = jnp.where(ctx, s_[0:1], s_[1:2])
        r = lax.rsqrt(jnp.mean(hh * hh, axis=-1, keepdims=True) + EPS)
        hr = hh * r
        dy = dx * (1.0 + s1)
        u = dy * g_
        dh = r * u - hr * (r * r) * jnp.mean(u * hh, axis=-1, keepdims=True)
        dh = dh + jnp.where(ctx, 0.0, dr)

        def seg(v):
            v = _colsum(v)
            return jnp.concatenate([jnp.where(ctx, v, 0.0), jnp.where(ctx, 0.0, v)], axis=0)

        return [dh], [seg(dx), seg(dx * hr * g_), seg(dy * hr)]

    (dh,), (dsh, ds, dg) = _rowwise(name, fn, h.shape[0], [_ri(h), _ri(dxn), _ri(dres, ro=-nct)], [g, s],
                                    [(d, F32)], [(2, d)] * 3)
    return dh, dsh, ds, dg


def _resnorm_fwd(name, h, y, gate, g, s, sh):
    d = h.shape[1]

    def fn(rows, bcs, i):
        hh, yy = rows
        gate_, g_, s_, sh_ = bcs
        hn = hh + gate_ * yy
        r = lax.rsqrt(jnp.mean(hn * hn, axis=-1, keepdims=True) + EPS)
        return [hn, hn * r * g_ * (1.0 + s_) + sh_], []

    return _rowwise(name, fn, h.shape[0], [_ri(h), _ri(y)], [gate, g, s, sh], [(d, F32), (d, MXU)])[0]


def _gate_bwd(name, dh, y, gate):
    d = dh.shape[1]

    def fn(rows, bcs, i):
        dd, yy = rows
        dy = dd * bcs[0]
        return [dy], [_colsum(dd * yy), _colsum(dy)]

    (dy,), (dgate, dbias) = _rowwise(name, fn, dh.shape[0], [_ri(dh), _ri(y)], [gate], [(d, MXU)], [(1, d)] * 2)
    return dy, dgate, dbias


def _swiglu_fwd(name, u):
    f = u.shape[1] // 2

    def fn(rows, bcs, i):
        return [_silu(rows[0]) * rows[1]], []

    return _rowwise(name, fn, u.shape[0], [_ri(u, f, 0), _ri(u, f, 1)], [], [(f, MXU)])[0][0]


def _swiglu_bwd(name, u, dhid):
    f = u.shape[1] // 2

    def fn(rows, bcs, i):
        a, b, dd = rows
        return [jnp.concatenate([dd * b * _dsilu(a), dd * _silu(a)], axis=1)], []

    return _rowwise(name, fn, u.shape[0], [_ri(u, f, 0), _ri(u, f, 1), _ri(dhid)], [], [(2 * f, MXU)])[0][0]


def _glu_fwd(name, u):
    d = u.shape[1] // 2

    def fn(rows, bcs, i):
        return [rows[0] * _sigmoid(rows[1])], []

    return _rowwise(name, fn, u.shape[0], [_ri(u, d, 0), _ri(u, d, 1)], [], [(d, F32)])[0][0]


def _glu_bwd(name, u, dgl):
    d = u.shape[1] // 2

    def fn(rows, bcs, i):
        a, b, dd = rows
        sg = _sigmoid(b)
        du = jnp.concatenate([dd * sg, dd * a * sg * (1.0 - sg)], axis=1)
        return [du], [_colsum(du)]

    (du,), (db,) = _rowwise(name, fn, u.shape[0], [_ri(u, d, 0), _ri(u, d, 1), _ri(dgl)], [], [(2 * d, MXU)], [(1, 2 * d)])
    return du, db


def _ln_silu_fwd(name, v, g, b):
    d = v.shape[1]

    def fn(rows, bcs, i):
        vv = rows[0]
        mu = jnp.mean(vv, axis=-1, keepdims=True)
        xc = vv - mu
        rs = lax.rsqrt(jnp.mean(xc * xc, axis=-1, keepdims=True) + EPS)
        return [_silu(xc * rs * bcs[0] + bcs[1])], []

    return _rowwise(name, fn, v.shape[0], [_ri(v)], [g, b], [(d, MXU)])[0][0]


def _ln_silu_bwd(name, v, ds, g, b):
    d = v.shape[1]

    def fn(rows, bcs, i):
        vv, dd = rows
        mu = jnp.mean(vv, axis=-1, keepdims=True)
        xc = vv - mu
        rs = lax.rsqrt(jnp.mean(xc * xc, axis=-1, keepdims=True) + EPS)
        xh = xc * rs
        dln = dd * _dsilu(xh * bcs[0] + bcs[1])
        dxh = dln * bcs[0]
        dv = rs * (dxh - jnp.mean(dxh, axis=-1, keepdims=True) - xh * jnp.mean(dxh * xh, axis=-1, keepdims=True))
        return [dv], [_colsum(dln * xh), _colsum(dln)]

    (dv,), (dg, db) = _rowwise(name, fn, v.shape[0], [_ri(v), _ri(ds)], [g, b], [(d, F32)], [(1, d)] * 2)
    return dv, dg, db


def _final_loss(name, h, f, target, gate, gf):
    d = h.shape[1]

    def fn(rows, bcs, i):
        hh, ff, tg = rows
        gate_, g_ = bcs
        hn = hh + gate_ * ff
        r = lax.rsqrt(jnp.mean(hn * hn, axis=-1, keepdims=True) + EPS)
        hr = hn * r
        err = hr * g_ - tg
        dout = err * (1.0 / d)
        u = dout * g_
        dh = r * u - hr * (r * r) * jnp.mean(u * hn, axis=-1, keepdims=True)
        sq = jnp.sum(_colsum(err * err), axis=1, keepdims=True)
        return [dh], [jnp.broadcast_to(sq, (1, LANES)), _colsum(dout * hr)]

    (dh,), (sq, dgf) = _rowwise(name, fn, h.shape[0], [_ri(h), _ri(f), _ri(target)], [gate, gf], [(d, F32)], [(1, LANES), (1, d)])
    return dh, sq, dgf


def _shift_rows(x, o, seg_lo, seg_hi, row):
    n = x.shape[0]
    if o == 0:
        return x
    sh = pltpu.roll(x, (-o) % n, 0)
    ok = (row + o >= seg_lo) & (row + o < seg_hi)
    return jnp.where(ok, sh, 0.0)


def _seg_bounds(row, tc, tt):
    ctx = row < tc
    return jnp.where(ctx, 0, tc), jnp.where(ctx, tc, tt)


def _ssd_conv_fwd(name, zx, w, b, di, tc):
    tt, kc, cd = zx.shape[0], w.shape[0], w.shape[1]
    cb = _pick(cd, (LANES,))
    off = di // cb

    def body(x_ref, w_ref, b_ref, o_ref):
        x = x_ref[...]
        row = lax.broadcasted_iota(jnp.int32, (tt, 1), 0)
        lo, hi = _seg_bounds(row, tc, tt)
        acc = jnp.broadcast_to(b_ref[...], x.shape)
        for k in range(kc):
            acc = acc + w_ref[k:k + 1, :] * _shift_rows(x, k - kc // 2, lo, hi, row)
        o_ref[...] = _silu(acc)

    return pl.pallas_call(
        body, name=name, grid=(cd // cb,), out_shape=pltpu.HBM((tt, cd), F32),
        in_specs=[pl.BlockSpec((tt, cb), lambda j: (0, j + off)), pl.BlockSpec((kc, cb), lambda j: (0, j)),
                  pl.BlockSpec((1, cb), lambda j: (0, j))],
        out_specs=pl.BlockSpec((tt, cb), lambda j: (0, j)),
        compiler_params=pltpu.CompilerParams(dimension_semantics=("parallel",), vmem_limit_bytes=_vmem(4 * tt * cb * 4)),
    )(_hbm(zx), _hbm(w), _hbm(b))


def _ssd_conv_bwd(name, zx, dact2, w, b, di, tc):
    tt, kc, cd = zx.shape[0], w.shape[0], w.shape[1]
    cb = _pick(cd, (LANES,))
    off = di // cb

    def body(x_ref, d0_ref, d1_ref, w_ref, b_ref, dx_ref, dw_ref, db_ref):
        x = x_ref[...]
        row = lax.broadcasted_iota(jnp.int32, (tt, 1), 0)
        lo, hi = _seg_bounds(row, tc, tt)
        pre = jnp.broadcast_to(b_ref[...], x.shape)
        for k in range(kc):
            pre = pre + w_ref[k:k + 1, :] * _shift_rows(x, k - kc // 2, lo, hi, row)
        dpre = (d0_ref[...] + d1_ref[...]) * _dsilu(pre)
        dx = jnp.zeros_like(x)
        for k in range(kc):
            o = k - kc // 2
            dx = dx + w_ref[k:k + 1, :] * _shift_rows(dpre, -o, lo, hi, row)
            dw_ref[k:k + 1, :] = _colsum(dpre * _shift_rows(x, o, lo, hi, row))
        dx_ref[...] = dx.astype(dx_ref.dtype)
        db_ref[...] = _colsum(dpre)

    blk = pl.BlockSpec((tt, cb), lambda j: (0, j))
    return pl.pallas_call(
        body, name=name, grid=(cd // cb,),
        out_shape=[pltpu.HBM((tt, cd), MXU), pltpu.HBM((kc, cd), F32), pltpu.HBM((1, cd), F32)],
        in_specs=[pl.BlockSpec((tt, cb), lambda j: (0, j + off)), pl.BlockSpec((None, tt, cb), lambda j: (0, 0, j)),
                  pl.BlockSpec((None, tt, cb), lambda j: (1, 0, j)), pl.BlockSpec((kc, cb), lambda j: (0, j)),
                  pl.BlockSpec((1, cb), lambda j: (0, j))],
        out_specs=[blk, pl.BlockSpec((kc, cb), lambda j: (0, j)), pl.BlockSpec((1, cb), lambda j: (0, j))],
        compiler_params=pltpu.CompilerParams(dimension_semantics=("parallel",), vmem_limit_bytes=_vmem(8 * tt * cb * 4)),
    )(_hbm(zx), _hbm(dact2), _hbm(dact2), _hbm(w), _hbm(b))


def _strided_conv(name, x, w, b, stride):
    t, ch = x.shape
    kk = w.shape[0]
    pad = (kk // 2) * stride
    cb = _pick(ch, (LANES,))
    has_b = b is not None

    def body(*refs):
        x_ref, w_ref = refs[:2]
        o_ref, xp = refs[-2:]
        xp[0:pad, :] = jnp.zeros((pad, cb), F32)
        xp[pad + t:, :] = jnp.zeros((pad, cb), F32)
        xp[pad:pad + t, :] = x_ref[...]
        acc = jnp.broadcast_to(refs[2][...], (t, cb)) if has_b else jnp.zeros((t, cb), F32)
        for k in range(kk):
            acc = acc + w_ref[k:k + 1, :] * xp[k * stride:k * stride + t, :]
        o_ref[...] = acc

    ins, specs = [x, w], [pl.BlockSpec((t, cb), lambda j: (0, j)), pl.BlockSpec((kk, cb), lambda j: (0, j))]
    if has_b:
        ins.append(b)
        specs.append(pl.BlockSpec((1, cb), lambda j: (0, j)))
    return pl.pallas_call(
        body, name=name, grid=(ch // cb,), out_shape=pltpu.HBM((t, ch), F32), in_specs=specs,
        out_specs=pl.BlockSpec((t, cb), lambda j: (0, j)), scratch_shapes=[pltpu.VMEM((t + 2 * pad, cb), F32)],
        compiler_params=pltpu.CompilerParams(dimension_semantics=("parallel",), vmem_limit_bytes=_vmem(6 * t * cb * 4)),
    )(*[_hbm(v) for v in ins])


def _strided_conv_dw(name, x, dv, kk, stride):
    t, ch = x.shape
    pad = (kk // 2) * stride
    cb = _pick(ch, (LANES,))

    def body(x_ref, d_ref, dw_ref, db_ref, xp):
        xp[0:pad, :] = jnp.zeros((pad, cb), F32)
        xp[pad + t:, :] = jnp.zeros((pad, cb), F32)
        xp[pad:pad + t, :] = x_ref[...]
        d = d_ref[...]
        for k in range(kk):
            dw_ref[k:k + 1, :] = _colsum(d * xp[k * stride:k * stride + t, :])
        db_ref[...] = _colsum(d)

    blk = pl.BlockSpec((t, cb), lambda j: (0, j))
    return pl.pallas_call(
        body, name=name, grid=(ch // cb,), out_shape=[pltpu.HBM((kk, ch), F32), pltpu.HBM((1, ch), F32)],
        in_specs=[blk, blk], out_specs=[pl.BlockSpec((kk, cb), lambda j: (0, j)), pl.BlockSpec((1, cb), lambda j: (0, j))],
        scratch_shapes=[pltpu.VMEM((t + 2 * pad, cb), F32)],
        compiler_params=pltpu.CompilerParams(dimension_semantics=("parallel",), vmem_limit_bytes=_vmem(6 * t * cb * 4)),
    )(_hbm(x), _hbm(dv))


def _grid_t(a, n1, n2):
    return a.reshape(n1, n2, a.shape[-1]).swapaxes(0, 1).reshape(n1 * n2, a.shape[-1])


def _chunk_order(d, i, ncc, nc):
    back = jnp.where(i < ncc, ncc - 1 - i, nc - 1 - (i - ncc))
    return jnp.where(d == 0, i, back)


def _ssd_chunk_setup(d, dt_raw, bias, a_log, q, h, di):
    p = di // h
    dt = _softplus(dt_raw + bias)
    a_neg = -jnp.exp(a_log)
    delta = dt * a_neg
    r = lax.broadcasted_iota(jnp.int32, (q, q), 0)
    c = lax.broadcasted_iota(jnp.int32, (q, q), 1)
    sgn = 1 - 2 * d
    mask = (r - c) * sgn >= 0
    mask_t = (c - r) * sgn >= 0
    a = _dot_lx(mask.astype(MXU), delta, NN, parts=3)
    tot = _colsum(delta)
    ea, dte, cd = jnp.exp(a), jnp.exp(tot - a), jnp.exp(tot)
    hh = lax.broadcasted_iota(jnp.int32, (h, di), 0)
    cc = lax.broadcasted_iota(jnp.int32, (h, di), 1)
    e = (cc // p == hh).astype(MXU)
    ex = _dot_rx(jnp.concatenate([dt, ea, dte, jnp.broadcast_to(cd, (8, h))], axis=0), e, NN)
    eye = (lax.broadcasted_iota(jnp.int32, (h, h), 0) == lax.broadcasted_iota(jnp.int32, (h, h), 1)).astype(MXU)
    a_t = _dot_lx(eye, a, NT, parts=3)
    return dict(dt=dt, a_neg=a_neg, a=a, a_t=a_t, mask=mask, mask_t=mask_t, e=e,
                dt_e=ex[0:q], ea_e=ex[q:2 * q], dte_e=ex[2 * q:3 * q], cd_e=ex[3 * q:3 * q + 1])


def _pick_heads(r, q, hpg, p):
    lane = lax.broadcasted_iota(jnp.int32, (q, hpg * p), 1) // p
    out = jnp.zeros((q, hpg * p), F32)
    for j in range(hpg):
        out = out + jnp.where(lane == j, r[j * q:(j + 1) * q], 0.0)
    return out


def _ssd_fwd(name, xbc, dt2, bias2, alog2, di, tc, comm=None):
    tt, cd = xbc.shape
    h = dt2.shape[-1]
    q, n = SSD_CHUNK, SSD_STATE
    gn = (cd - di) // 2
    g = gn // n
    hpg, p = h // g, di // h
    gp = hpg * p
    nc, ncc = tt // q, tc // q
    assert di % gn == 0

    def body(x_ref, b_ref, c_ref, dt_ref, bias_ref, alog_ref, y_ref, hp_ref, ht):
        d, i = pl.program_id(0), pl.program_id(1)

        @pl.when(i == 0)
        def _():
            ht[...] = jnp.zeros_like(ht)

        s = _ssd_chunk_setup(d, dt_ref[...], bias_ref[...], alog_ref[...], q, h, di)
        xd = x_ref[...] * s["dt_e"]
        hp_ref[...] = ht[...]
        for gi in range(g):
            bg, cg = b_ref[:, gi * n:(gi + 1) * n].astype(MXU), c_ref[:, gi * n:(gi + 1) * n].astype(MXU)
            sl = slice(gi * gp, (gi + 1) * gp)
            sc = _dot(cg, bg, NT)
            ms = []
            for j in range(hpg):
                hd = gi * hpg + j
                seg = s["a"][:, hd:hd + 1] - s["a_t"][hd:hd + 1, :]
                ms.append(sc * jnp.exp(jnp.where(s["mask"], seg, -jnp.inf)))
            xdg = xd[:, sl]
            ydiag = _pick_heads(_dot(jnp.concatenate(ms, axis=0), xdg, NN), q, hpg, p)
            htg = ht[:, sl]
            y_ref[:, sl] = ydiag + _dot(cg, htg, NN) * s["ea_e"][:, sl]
            ht[:, sl] = s["cd_e"][:, sl] * htg + _dot(bg, xdg * s["dte_e"][:, sl], TN)

    def cidx(d, i):
        return _chunk_order(d, i, ncc, nc)

    cm = comm if comm is not None else _Comm()
    res = pl.pallas_call(
        _carry(body, comm, 6, 2, (2, nc)), name=name, grid=(2, nc),
        out_shape=[pltpu.HBM((2, tt, di), F32), pltpu.HBM((2, nc, n, di), F32)] + cm.out_shape,
        in_specs=[pl.BlockSpec((q, di), lambda d, i: (cidx(d, i), 0)),
                  pl.BlockSpec((q, gn), lambda d, i: (cidx(d, i), di // gn)),
                  pl.BlockSpec((q, gn), lambda d, i: (cidx(d, i), di // gn + 1)),
                  pl.BlockSpec((None, q, h), lambda d, i: (d, cidx(d, i), 0)),
                  pl.BlockSpec((None, 1, h), lambda d, i: (d, 0, 0)),
                  pl.BlockSpec((None, 1, h), lambda d, i: (d, 0, 0))] + cm.specs,
        out_specs=[pl.BlockSpec((None, q, di), lambda d, i: (d, cidx(d, i), 0)),
                   pl.BlockSpec((None, None, n, di), lambda d, i: (d, cidx(d, i), 0, 0))] + cm.specs,
        scratch_shapes=[pltpu.VMEM((n, di), F32)] + (cm.scratch if comm is not None else []),
        compiler_params=pltpu.CompilerParams(dimension_semantics=("arbitrary", "arbitrary"), vmem_limit_bytes=_vmem(16 * q * di * 4)),
    )(*[_hbm(v) for v in (xbc, xbc, xbc, dt2, bias2, alog2)], *cm.operands)
    return res[0], res[1], cm.split(res[2:])


def _ssd_bwd(name, xbc, dt2, bias2, alog2, dy, hp2, dskip_e, di, tc, comm=None):
    tt, cd = xbc.shape
    h = dt2.shape[-1]
    q, n = SSD_CHUNK, SSD_STATE
    gn = (cd - di) // 2
    g = gn // n
    hpg, p = h // g, di // h
    gp = hpg * p
    nc, ncc = tt // q, tc // q

    def body(x_ref, b_ref, c_ref, dt_ref, bias_ref, alog_ref, dy_ref, hp_ref, dsk_ref,
             dxbc_ref, ddt_ref, dalog_ref, dbias_ref, dht, dxd, off):
        d, i = pl.program_id(0), pl.program_id(1)

        @pl.when(i == 0)
        def _():
            dht[...] = jnp.zeros_like(dht)
            dalog_ref[...] = jnp.zeros_like(dalog_ref)
            dbias_ref[...] = jnp.zeros_like(dbias_ref)

        s = _ssd_chunk_setup(d, dt_ref[...], bias_ref[...], alog_ref[...], q, h, di)
        x, dyc = x_ref[...], dy_ref[...]
        xd = x * s["dt_e"]
        dyea = dyc * s["ea_e"]
        xdte = xd * s["dte_e"]
        lane = lax.broadcasted_iota(jnp.int32, (q, gp), 1) // p
        lane_h = lax.broadcasted_iota(jnp.int32, (q, h), 1)
        da_d = jnp.zeros((q, h), F32)
        last_e = []
        for gi in range(g):
            bg, cg = b_ref[:, gi * n:(gi + 1) * n].astype(MXU), c_ref[:, gi * n:(gi + 1) * n].astype(MXU)
            sl = slice(gi * gp, (gi + 1) * gp)
            sc, sct = _dot(cg, bg, NT), _dot(bg, cg, NT)
            dyg, xdg = dyc[:, sl], xd[:, sl]
            htg, dhtg = hp_ref[:, sl], dht[:, sl]
            dystack = jnp.concatenate([jnp.where(lane == j, dyg, 0.0) for j in range(hpg)], axis=0)
            xdstack = jnp.concatenate([jnp.where(lane == j, xdg, 0.0) for j in range(hpg)], axis=0)
            gs = _dot(dystack, xdg, NT)
            gst = _dot(xdstack, dyg, NT)
            ds = jnp.zeros((q, q), F32)
            mts = []
            for j in range(hpg):
                hd = gi * hpg + j
                col, rw = s["a"][:, hd:hd + 1], s["a_t"][hd:hd + 1, :]
                gl = gs[j * q:(j + 1) * q] * jnp.exp(jnp.where(s["mask"], col - rw, -jnp.inf))
                ds = ds + gl
                mt = sct * jnp.exp(jnp.where(s["mask_t"], rw - col, -jnp.inf))
                mts.append(mt)
                da_j = jnp.sum(gl * sc, axis=1, keepdims=True) - jnp.sum(gst[j * q:(j + 1) * q] * mt, axis=1, keepdims=True)
                da_d = da_d + jnp.where(lane_h == hd, da_j, 0.0)
            dxd_diag = _pick_heads(_dot(jnp.concatenate(mts, axis=0), dyg, NN), q, hpg, p)
            z = _dot(bg, dhtg, NN) * s["dte_e"][:, sl]
            yoff = _dot(cg, htg, NN) * s["ea_e"][:, sl]
            off[:, sl] = dyg * yoff - xdg * z
            dxd[:, sl] = dxd_diag + z
            dxbc_ref[:, di + gi * n:di + (gi + 1) * n] = _dot(ds, cg, TN) + _dot(xdte[:, sl], dhtg, NT)
            dxbc_ref[:, di + gn + gi * n:di + gn + (gi + 1) * n] = _dot(ds, bg, NN) + _dot(dyea[:, sl], htg, NT)
            last_e.append(s["cd_e"][:, sl] * _colsum(dhtg * htg) + _colsum(xdg * z))
            dht[:, sl] = s["cd_e"][:, sl] * dhtg + _dot(cg, dyea[:, sl], TN)
        dxd_all = dxd[...]
        last = jnp.concatenate(last_e, axis=1)
        da = da_d + _dot_rx(off[...], s["e"], NT)
        last_h = _dot_rx(jnp.broadcast_to(last, (8, di)), s["e"], NT)[0:1]
        ddelta = _dot_lx(s["mask_t"].astype(MXU), da, NN, parts=3) + last_h
        ddt = ddelta * s["a_neg"] + _dot_rx(dxd_all * x, s["e"], NT)
        ddt_raw = ddt * _sigmoid(dt_ref[...] + bias_ref[...])
        ddt_ref[...] = ddt_raw
        dalog_ref[...] += _colsum(ddelta * s["dt"]) * s["a_neg"]
        dbias_ref[...] += _colsum(ddt_raw)
        dxbc_ref[:, 0:di] = dxd_all * s["dt_e"] + jnp.where(d == 0, dyc * dsk_ref[...], 0.0)

    def cidx(d, i):
        return _chunk_order(d, nc - 1 - i, ncc, nc)

    cm = comm if comm is not None else _Comm()
    res = pl.pallas_call(
        _carry(body, comm, 9, 4, (2, nc)), name=name, grid=(2, nc),
        out_shape=[pltpu.HBM((2, tt, cd), F32), pltpu.HBM((2, tt, h), F32),
                   pltpu.HBM((2, 1, h), F32), pltpu.HBM((2, 1, h), F32)] + cm.out_shape,
        in_specs=[pl.BlockSpec((q, di), lambda d, i: (cidx(d, i), 0)),
                  pl.BlockSpec((q, gn), lambda d, i: (cidx(d, i), di // gn)),
                  pl.BlockSpec((q, gn), lambda d, i: (cidx(d, i), di // gn + 1)),
                  pl.BlockSpec((None, q, h), lambda d, i: (d, cidx(d, i), 0)),
                  pl.BlockSpec((None, 1, h), lambda d, i: (d, 0, 0)),
                  pl.BlockSpec((None, 1, h), lambda d, i: (d, 0, 0)),
                  pl.BlockSpec((q, di), lambda d, i: (cidx(d, i), 0)),
                  pl.BlockSpec((None, None, n, di), lambda d, i: (d, cidx(d, i), 0, 0)),
                  pl.BlockSpec((1, di), lambda d, i: (0, 0))] + cm.specs,
        out_specs=[pl.BlockSpec((None, q, cd), lambda d, i: (d, cidx(d, i), 0)),
                   pl.BlockSpec((None, q, h), lambda d, i: (d, cidx(d, i), 0)),
                   pl.BlockSpec((None, 1, h), lambda d, i: (d, 0, 0)),
                   pl.BlockSpec((None, 1, h), lambda d, i: (d, 0, 0))] + cm.specs,
        scratch_shapes=[pltpu.VMEM((n, di), F32), pltpu.VMEM((q, di), F32), pltpu.VMEM((q, di), F32)]
        + (cm.scratch if comm is not None else []),
        compiler_params=pltpu.CompilerParams(dimension_semantics=("arbitrary", "arbitrary"), vmem_limit_bytes=_vmem(24 * q * di * 4)),
    )(*[_hbm(v) for v in (xbc, xbc, xbc, dt2, bias2, alog2, dy, hp2, dskip_e)], *cm.operands)
    return res[0], res[1], res[2], res[3], cm.split(res[4:])


def _ssd_gate_fwd(name, y2, xbc, zx, dskip_e, norm_w, di, nct, t):
    def fn(rows, bcs, i):
        yf, yb, xs, z = rows
        zg = (yf + yb + bcs[0] * xs) * _silu(z)
        rn = lax.rsqrt(jnp.mean(zg * zg, axis=-1, keepdims=True) + EPS)
        return [zg * rn * bcs[1]], []

    ins = [_ri(y2, lead=0, ro=nct), _ri(y2, lead=1, ro=nct), _ri(xbc, di, 0, ro=nct), _ri(zx, di, 0, ro=nct)]
    return _rowwise(name, fn, t, ins, [dskip_e, norm_w], [(di, MXU)])[0][0]


def _ssd_gate_bwd(name, dyn, y2, xbc, zx, dskip_e, norm_w, di, nct, tt):
    def fn(rows, bcs, i):
        dn, yf, yb, xs, z = rows
        lat = i >= nct
        ytot = yf + yb + bcs[0] * xs
        sz = _silu(z)
        zg = ytot * sz
        rn = lax.rsqrt(jnp.mean(zg * zg, axis=-1, keepdims=True) + EPS)
        u = dn * bcs[1]
        dzg = rn * u - zg * (rn * rn * rn) * jnp.mean(u * zg, axis=-1, keepdims=True)
        dy = jnp.where(lat, dzg * sz, 0.0)
        dz = jnp.where(lat, dzg * ytot * _dsilu(z), 0.0)
        return [dy, dz], [jnp.where(lat, _colsum(dn * zg * rn), 0.0), jnp.where(lat, _colsum(dy * xs), 0.0)]

    ins = [_ri(dyn, ro=-nct), _ri(y2, lead=0), _ri(y2, lead=1), _ri(xbc, di, 0), _ri(zx, di, 0)]
    (dy, dz), (dnw, ddsk) = _rowwise(name, fn, tt, ins, [dskip_e, norm_w], [(di, F32), (di, MXU)], [(1, di)] * 2)
    return dy, dz, dnw, ddsk


def _ada_fwd(name, cs, w, b):
    nl, d, c = w.shape
    r = cs.shape[0]

    def body(cs_ref, w_ref, b_ref, o_ref):
        o_ref[...] = _dot(_silu(cs_ref[...]), w_ref[...], NN) + b_ref[...]

    return pl.pallas_call(
        body, name=name, grid=(nl,), out_shape=pltpu.HBM((nl, r, c), F32),
        in_specs=[pl.BlockSpec((r, d), lambda l: (0, 0)), pl.BlockSpec((None, d, c), lambda l: (l, 0, 0)),
                  pl.BlockSpec((None, 1, c), lambda l: (l, 0, 0))],
        out_specs=pl.BlockSpec((None, r, c), lambda l: (l, 0, 0)),
        compiler_params=pltpu.CompilerParams(dimension_semantics=("parallel",), vmem_limit_bytes=_vmem(2 * d * c * 4)),
    )(_hbm(cs), _hbm(w), _hbm(b))


def _ada_bwd(name, cs, w, dmod):
    nl, d, c = w.shape
    r = cs.shape[0]

    def body(cs_ref, w_ref, dm_ref, dw_ref, dsc_ref):
        dm = dm_ref[...]
        dw_ref[...] = _dot(_silu(cs_ref[...]), dm, TN)

        @pl.when(pl.program_id(0) == 0)
        def _():
            dctx = jnp.broadcast_to(_colsum(dm[r // 2:]), (8, c))
            dsc_ref[...] = _dot(dctx, w_ref[...], NT)[0:1]

    return pl.pallas_call(
        body, name=name, grid=(nl,), out_shape=[pltpu.HBM((nl, d, c), F32), pltpu.HBM((1, d), F32)],
        in_specs=[pl.BlockSpec((r, d), lambda l: (0, 0)), pl.BlockSpec((None, d, c), lambda l: (l, 0, 0)),
                  pl.BlockSpec((None, r, c), lambda l: (l, 0, 0))],
        out_specs=[pl.BlockSpec((None, d, c), lambda l: (l, 0, 0)), pl.BlockSpec((1, d), lambda l: (0, 0))],
        compiler_params=pltpu.CompilerParams(dimension_semantics=("arbitrary",), vmem_limit_bytes=_vmem(4 * d * c * 4)),
    )(_hbm(cs), _hbm(w), _hbm(dmod))


def _adam_math(w, g, m, v):
    m = ADAM_B1 * m + (1.0 - ADAM_B1) * g
    v = ADAM_B2 * v + (1.0 - ADAM_B2) * (g * g)
    m_hat = m / (1.0 - ADAM_B1 ** ADAM_STEP)
    v_hat = v / (1.0 - ADAM_B2 ** ADAM_STEP)
    delta = -ADAM_LR * (m_hat / (jnp.sqrt(v_hat) + ADAM_EPS) + ADAM_WD * w)
    return delta, m, v


def _adam(name, slots, w, m, v):
    ns, r, c = slots.shape
    tr = _pick(r, (256, 128, 64, 32, 16, 8))

    def body(s_ref, w_ref, m_ref, v_ref, g_ref, d_ref, mo_ref, vo_ref):
        g = s_ref[0].astype(F32)
        for k in range(1, ns):
            g = g + s_ref[k].astype(F32)
        d, mn, vn = _adam_math(w_ref[...], g, m_ref[...], v_ref[...])
        g_ref[...], d_ref[...], mo_ref[...], vo_ref[...] = g, d, mn, vn

    blk = pl.BlockSpec((tr, c), lambda i: (i, 0))
    return pl.pallas_call(
        body, name=name, grid=(r // tr,), out_shape=[pltpu.HBM((r, c), F32)] * 4,
        in_specs=[pl.BlockSpec((ns, tr, c), lambda i: (0, i, 0)), blk, blk, blk], out_specs=[blk] * 4,
        compiler_params=pltpu.CompilerParams(dimension_semantics=("parallel",), vmem_limit_bytes=_vmem(16 * tr * c * 4)),
    )(_hbm(slots), _hbm(w), _hbm(m), _hbm(v))


def _adam_small(name, slots, ws, ms, vs, scale=None):
    k = len(slots)

    def body(*refs):
        s_refs, w_refs, m_refs, v_refs = refs[:k], refs[k:2 * k], refs[2 * k:3 * k], refs[3 * k:4 * k]
        sc_ref = refs[4 * k] if scale is not None else None
        outs = refs[4 * k + (scale is not None):]
        for a in range(k):
            g = s_refs[a][0]
            for j in range(1, NDEV):
                g = g + s_refs[a][j]
            if scale is not None and a == scale[0]:
                g = g * _dsilu(sc_ref[...])
            d, mn, vn = _adam_math(w_refs[a][...], g, m_refs[a][...], v_refs[a][...])
            outs[a][...], outs[k + a][...], outs[2 * k + a][...], outs[3 * k + a][...] = g, d, mn, vn

    shapes = [pltpu.HBM(w.shape, F32) for w in ws]
    extra = [scale[1]] if scale is not None else []
    ins = [*slots, *ws, *ms, *vs, *extra]

    def whole(shape):
        return pl.BlockSpec(shape, lambda i, nd=len(shape): (0,) * nd)

    res = pl.pallas_call(body, name=name, grid=(1,), out_shape=shapes * 4, in_specs=[whole(v.shape) for v in ins],
                         out_specs=[whole(s.shape) for s in shapes * 4])(*[_hbm(v) for v in ins])
    return res[:k], res[k:2 * k], res[2 * k:3 * k], res[3 * k:]


def _unshard_cols(g):
    g = jnp.moveaxis(g, 0, -2)
    return g.reshape(g.shape[:-2] + (g.shape[-2] * g.shape[-1],))


def _shard_cols(a):
    a = a.reshape(a.shape[:-1] + (NDEV, a.shape[-1] // NDEV))
    return jnp.moveaxis(a, -2, 0)


def _unshard_rows(g):
    g = jnp.moveaxis(g, 0, -3)
    return g.reshape(g.shape[:-3] + (g.shape[-3] * g.shape[-2], g.shape[-1]))


def _shard_rows(a):
    a = a.reshape(a.shape[:-2] + (NDEV, a.shape[-2] // NDEV, a.shape[-1]))
    return jnp.moveaxis(a, -3, 0)


def _flat2(a):
    return a.reshape((-1, a.shape[-1]))


def kernel(x, c, ctx, c_ctx, ada_w, ada_b, norm_mix_g, norm_ffn_g, final_norm_g, ssd_w_in, ssd_conv_w, ssd_conv_b, ssd_dt_bias_f, ssd_dt_bias_b, ssd_a_log_f, ssd_a_log_b, ssd_d_skip, ssd_norm_w, ssd_w_out, conf_w_pw1, conf_b_pw1, conf_dw_w, conf_dw_b, conf_ln_g, conf_ln_b, conf_w_pw2, conf_b_pw2, ffn_w_in, ffn_w_out, loss_target, m_c_ctx, m_ada_w, m_ada_b, m_norm_mix_g, m_norm_ffn_g, m_final_norm_g, m_ssd_w_in, m_ssd_conv_w, m_ssd_conv_b, m_ssd_dt_bias_f, m_ssd_dt_bias_b, m_ssd_a_log_f, m_ssd_a_log_b, m_ssd_d_skip, m_ssd_norm_w, m_ssd_w_out, m_conf_w_pw1, m_conf_b_pw1, m_conf_dw_w, m_conf_dw_b, m_conf_ln_g, m_conf_ln_b, m_conf_w_pw2, m_conf_b_pw2, m_ffn_w_in, m_ffn_w_out, v_c_ctx, v_ada_w, v_ada_b, v_norm_mix_g, v_norm_ffn_g, v_final_norm_g, v_ssd_w_in, v_ssd_conv_w, v_ssd_conv_b, v_ssd_dt_bias_f, v_ssd_dt_bias_b, v_ssd_a_log_f, v_ssd_a_log_b, v_ssd_d_skip, v_ssd_norm_w, v_ssd_w_out, v_conf_w_pw1, v_conf_b_pw1, v_conf_dw_w, v_conf_dw_b, v_conf_ln_g, v_conf_ln_b, v_conf_w_pw2, v_conf_b_pw2, v_ffn_w_in, v_ffn_w_out):
    args = dict(locals())
    names = ['c_ctx', 'ada_w', 'ada_b', 'norm_mix_g', 'norm_ffn_g', 'final_norm_g', 'ssd_w_in', 'ssd_conv_w', 'ssd_conv_b',
             'ssd_dt_bias_f', 'ssd_dt_bias_b', 'ssd_a_log_f', 'ssd_a_log_b', 'ssd_d_skip', 'ssd_norm_w', 'ssd_w_out',
             'conf_w_pw1', 'conf_b_pw1', 'conf_dw_w', 'conf_dw_b', 'conf_ln_g', 'conf_ln_b', 'conf_w_pw2', 'conf_b_pw2',
             'ffn_w_in', 'ffn_w_out']
    me = 4 * lax.axis_index("x") + 2 * lax.axis_index("y") + lax.axis_index("c")
    t, d = x.shape[1], x.shape[2]
    tc = ctx.shape[1]
    tt = tc + t
    nct = tc // ROW_TILE
    assert tc % ROW_TILE == 0 and t % ROW_TILE == 0
    h = ssd_dt_bias_f.shape[-1]
    di = ssd_norm_w.shape[-1]
    cdim = ssd_conv_b.shape[-1]
    kc = ssd_conv_w.shape[1]
    ck = conf_dw_w.shape[1]
    ch = d // 2
    rows_g = t // GRID_W
    nl = ada_w.shape[0]
    cw = ada_w.shape[2]
    x2, ctx2, tgt = x[0], ctx[0], loss_target[0]

    (c_all, w_in_g, convw_g), _ = _exchange("gather_first", [c, ssd_w_in[0].astype(WIRE), ssd_conv_w[0]])
    later = _Comm(gather=[ssd_w_out[0].astype(WIRE), ffn_w_in.astype(WIRE), ffn_w_out.astype(WIRE), conf_w_pw1[0].astype(WIRE),
                          conf_w_pw2[0].astype(WIRE), conf_b_pw1, conf_dw_w[0], conf_dw_b, conf_ln_g, conf_ln_b, conf_b_pw2])
    w_ssd_in = _unshard_cols(w_in_g)
    w_zx = w_ssd_in[:, :di + cdim]
    w_z, w_xbc = w_zx[:, :di], w_zx[:, di:]
    w_dt = jnp.pad(w_ssd_in[:, di + cdim:], ((0, 0), (0, LANES - 2 * h)))
    conv_w_full = _unshard_cols(convw_g)

    cs_all = jnp.concatenate([c_all[:, 0, :], jnp.broadcast_to(c_ctx[None, :], (NDEV, d))], axis=0)
    ada_b_mine = lax.dynamic_slice_in_dim(ada_b, me * cw, cw, axis=1)[:, None, :]
    mod_part = _ada_fwd("ada_fwd", cs_all, ada_w, ada_b_mine)
    (mod_g,), _ = _exchange("gather_mod", [mod_part])
    mod_all = jnp.moveaxis(mod_g, 0, 2).reshape(nl, 2 * NDEV, NDEV * cw)
    mod_lat = lax.dynamic_slice_in_dim(mod_all, me, 1, axis=1)[:, 0, :]
    mod_ctx = mod_all[0, NDEV, :]

    def six(v):
        return [v[k * d:(k + 1) * d][None, :] for k in range(6)]

    sh1, s1, g1, sh2, s2, g2 = six(mod_lat[0])
    csh1, cs1 = six(mod_ctx)[:2]
    sh1b, s1b, g1b, sh2b, s2b, g2b = six(mod_lat[1])
    nmg, nfg = norm_mix_g, norm_ffn_g

    h_all = jnp.concatenate([ctx2, x2], axis=0)
    s01, sh01 = jnp.concatenate([cs1, s1], axis=0), jnp.concatenate([csh1, sh1], axis=0)
    xn_all = _normmod_fwd("l0_norm", h_all, nmg[0:1], s01, sh01, nct)
    zx = _mm("ssd_in_proj", xn_all, w_zx, "nn")
    dtr = _mm("ssd_dt_proj", xn_all, w_dt, "nn")
    dt2 = jnp.moveaxis(dtr[:, :2 * h].reshape(tt, 2, h), 1, 0)
    bias2 = jnp.stack([ssd_dt_bias_f, ssd_dt_bias_b])
    alog2 = jnp.stack([ssd_a_log_f, ssd_a_log_b])
    xbc = _ssd_conv_fwd("ssd_conv", zx, conv_w_full, ssd_conv_b, di, tc)
    y2, hp2, (gat, _) = _ssd_fwd("ssd_scan", xbc, dt2, bias2, alog2, di, tc, comm=later)
    (w_out_g, fin_g, fout_g, pw1_g, pw2_g, bpw1_g, dww_g, dwb_g, lng_g, lnb_g, bpw2_g) = gat
    w_ssd_out = _unshard_rows(w_out_g)
    w_pw1, w_pw2 = _unshard_cols(pw1_g), _unshard_rows(pw2_g)
    w_fin, w_fout = _unshard_cols(fin_g), _unshard_rows(fout_g)
    dw_w_full = _unshard_cols(dww_g)
    b_pw1, dw_b, ln_g, ln_b, b_pw2 = (_unshard_cols(a) for a in (bpw1_g, dwb_g, lng_g, lnb_g, bpw2_g))
    dskip_e = jnp.repeat(ssd_d_skip, di // h, axis=1)
    yn = _ssd_gate_fwd("ssd_gate", y2, xbc, zx, dskip_e, ssd_norm_w, di, nct, t)
    mix0 = _mm("ssd_out_proj", yn, w_ssd_out, "nn")
    h1, xf0 = _resnorm_fwd("l0_res_norm", x2, mix0, g1, nfg[0:1], s2, sh2)
    u0 = _mm("ffn0_in", xf0, w_fin[0], "nn")
    hid0 = _swiglu_fwd("ffn0_act", u0)
    f0 = _mm("ffn0_out", hid0, w_fout[0], "nn")
    h2, xn1 = _resnorm_fwd("l1_norm", h1, f0, g2, nmg[1:2], s1b, sh1b)
    u1 = _mm("conf_pw1", xn1, w_pw1, "nn", bias=b_pw1)
    gl = _glu_fwd("conf_glu", u1)
    gl_h = _grid_t(gl[:, :ch], rows_g, GRID_W)
    gl_v = gl[:, ch:]
    v_h = _strided_conv("conf_conv_h", gl_h, dw_w_full[:, :ch], dw_b[:, :ch], rows_g)
    v_v = _strided_conv("conf_conv_v", gl_v, dw_w_full[:, ch:], dw_b[:, ch:], GRID_W)
    v = jnp.concatenate([_grid_t(v_h, GRID_W, rows_g), v_v], axis=1)
    sl = _ln_silu_fwd("conf_ln", v, ln_g, ln_b)
    mix1 = _mm("conf_pw2", sl, w_pw2, "nn", bias=b_pw2)
    h3, xf1 = _resnorm_fwd("l1_res_norm", h2, mix1, g1b, nfg[1:2], s2b, sh2b)
    u2 = _mm("ffn1_in", xf1, w_fin[1], "nn")
    hid1 = _swiglu_fwd("ffn1_act", u2)
    f1 = _mm("ffn1_out", hid1, w_fout[1], "nn")
    dh, sq, d_final_g = _final_loss("final_loss", h3, f1, tgt, g2b, final_norm_g[None, :])
    loss = lax.psum(0.5 * sq[0, 0] / d, AXES)

    zero2 = jnp.zeros((2, d), F32)

    def ffn_bwd(tag, dh, hin, xf, u, hid, f, gate, w_in, w_out, g_norm, s_mod):
        df, dgate, _ = _gate_bwd(tag + "_gate_bwd", dh, f, gate)
        dhid = _mm(tag + "_dhid", df, w_out, "nt")
        dw_out = _mm(tag + "_dwout", hid, df, "tn", WIRE)
        du = _swiglu_bwd(tag + "_act_bwd", u, dhid)
        dw_in = _mm(tag + "_dwin", xf, du, "tn", WIRE)
        dxf = _mm(tag + "_dx", du, w_in, "nt")
        s_2 = jnp.concatenate([s_mod, s_mod], axis=0)
        dh, dsh, ds, dg = _normmod_bwd(tag + "_norm_bwd", hin, dxf, dh, g_norm, s_2)
        return dh, dgate, dsh[1:2], ds[1:2], dg[1:2], dw_in, dw_out

    dh, d_g2b, d_sh2b, d_s2b, d_nfg1, g_fin1, g_fout1 = ffn_bwd("ffn1", dh, h3, xf1, u2, hid1, f1, g2b, w_fin[1], w_fout[1], nfg[1:2], s2b)
    dmix1, d_g1b, g_bpw2 = _gate_bwd("conf_gate_bwd", dh, mix1, g1b)
    dsl = _mm("conf_dsl", dmix1, w_pw2, "nt")
    g_pw2 = _mm("conf_dwpw2", sl, dmix1, "tn", WIRE)
    dv, g_lng, g_lnb = _ln_silu_bwd("conf_ln_bwd", v, dsl, ln_g, ln_b)
    dv_h, dv_v = _grid_t(dv[:, :ch], rows_g, GRID_W), dv[:, ch:]
    w_flip = dw_w_full[::-1]
    dgl_h = _strided_conv("conf_conv_h_bwd", dv_h, w_flip[:, :ch], None, rows_g)
    dgl_v = _strided_conv("conf_conv_v_bwd", dv_v, w_flip[:, ch:], None, GRID_W)
    g_dww_h, g_dwb_h = _strided_conv_dw("conf_conv_h_dw", gl_h, dv_h, ck, rows_g)
    g_dww_v, g_dwb_v = _strided_conv_dw("conf_conv_v_dw", gl_v, dv_v, ck, GRID_W)
    g_dww, g_dwb = jnp.concatenate([g_dww_h, g_dww_v], axis=1), jnp.concatenate([g_dwb_h, g_dwb_v], axis=1)
    dgl = jnp.concatenate([_grid_t(dgl_h, GRID_W, rows_g), dgl_v], axis=1)
    du1, g_bpw1 = _glu_bwd("conf_glu_bwd", u1, dgl)
    g_pw1 = _mm("conf_dwpw1", xn1, du1, "tn", WIRE)
    dxn1 = _mm("conf_dx", du1, w_pw1, "nt")
    dh, dsh_, ds_, dg_ = _normmod_bwd("l1_norm_bwd", h2, dxn1, dh, nmg[1:2], jnp.concatenate([s1b, s1b], axis=0))
    d_sh1b, d_s1b, d_nmg1 = dsh_[1:2], ds_[1:2], dg_[1:2]
    dh, d_g2, d_sh2, d_s2, d_nfg0, g_fin0, g_fout0 = ffn_bwd("ffn0", dh, h1, xf0, u0, hid0, f0, g2, w_fin[0], w_fout[0], nfg[0:1], s2)
    dmix0, d_g1, _ = _gate_bwd("ssd_gate_res_bwd", dh, mix0, g1)
    dyn = _mm("ssd_dyn", dmix0, w_ssd_out, "nt")
    g_ssd_out = _mm("ssd_dwout", yn, dmix0, "tn", WIRE)
    dy, dz, g_normw, ddsk_e = _ssd_gate_bwd("ssd_gate_bwd", dyn, y2, xbc, zx, dskip_e, ssd_norm_w, di, nct, tt)
    early = [_shard_rows(g_ssd_out), _shard_cols(g_pw1), _shard_rows(g_pw2),
             _flat3(_shard_cols(jnp.stack([g_fin0, g_fin1]))), _flat3(_shard_rows(jnp.stack([g_fout0, g_fout1]))),
             _shard_cols(g_bpw1), _shard_cols(g_dww), _shard_cols(g_dwb), _shard_cols(g_lng), _shard_cols(g_lnb),
             _shard_cols(g_bpw2)]
    dxbc2, ddt2, g_alog2, g_bias2, (_, early_r) = _ssd_bwd("ssd_scan_bwd", xbc, dt2, bias2, alog2, dy, hp2, dskip_e, di, tc,
                                                           comm=_Comm(scatter=early))
    dxr, g_convw, g_convb = _ssd_conv_bwd("ssd_conv_bwd", zx, dxbc2, conv_w_full, ssd_conv_b, di, tc)
    ddt_p = jnp.pad(jnp.moveaxis(ddt2, 0, 1).reshape(tt, 2 * h), ((0, 0), (0, LANES - 2 * h))).astype(MXU)
    dxn = _mm("ssd_dx_z", dz, w_z, "nt")
    dxn = _mm("ssd_dx_xbc", dxr, w_xbc, "nt", add=dxn)
    dxn = _mm("ssd_dx_dt", ddt_p, w_dt, "nt", add=dxn)
    g_ssd_in = jnp.concatenate([_mm("ssd_dw_z", xn_all, dz, "tn", WIRE), _mm("ssd_dw_xbc", xn_all, dxr, "tn", WIRE),
                                _mm("ssd_dw_dt", xn_all, ddt_p, "tn", WIRE)[:, :2 * h]], axis=1)
    dh_all, dsh_, ds_, dg_ = _normmod_bwd("l0_norm_bwd", h_all, dxn, dh, nmg[0:1], s01, nct)
    grad_x = dh_all[tc:][None]
    d_csh1, d_sh1, d_cs1, d_s1 = dsh_[0:1], dsh_[1:2], ds_[0:1], ds_[1:2]
    d_nmg0 = dg_[0:1] + dg_[1:2]

    z1 = jnp.zeros((1, d), F32)
    dmod = jnp.concatenate([jnp.concatenate([d_sh1, d_s1, d_g1, d_sh2, d_s2, d_g2], axis=1),
                            jnp.concatenate([d_sh1b, d_s1b, d_g1b, d_sh2b, d_s2b, d_g2b], axis=1),
                            jnp.concatenate([d_csh1, d_cs1, z1, z1, z1, z1], axis=1)], axis=0)
    (dmod_g,), _ = _exchange("gather_dmod", [dmod])
    dmod_mine = lax.dynamic_slice_in_dim(dmod_g, me * cw, cw, axis=2)
    dmod16 = jnp.stack([jnp.concatenate([dmod_mine[:, 0], dmod_mine[:, 2]], axis=0),
                        jnp.concatenate([dmod_mine[:, 1], jnp.zeros((NDEV, cw), F32)], axis=0)])
    g_ada_w, dsc_part = _ada_bwd("ada_bwd", cs_all, ada_w, dmod16)
    g_ada_b = dmod[0:2] + jnp.concatenate([dmod[2:3], jnp.zeros((1, 6 * d), F32)], axis=0)

    d_dskip = jnp.sum(ddsk_e.reshape(h, di // h), axis=1)[None, :]
    rep = [dsc_part, g_ada_b, jnp.concatenate([d_nmg0, d_nmg1], axis=0), jnp.concatenate([d_nfg0, d_nfg1], axis=0),
           d_final_g, g_convb, g_bias2[0], g_bias2[1], g_alog2[0], g_alog2[1], d_dskip, g_normw]
    rep_g, (ssd_in_r, convw_r) = _exchange("exchange_grads", rep, [_shard_cols(g_ssd_in), _shard_cols(g_convw)])
    big_r = [ssd_in_r] + list(early_r[:5])
    small_r = [convw_r] + list(early_r[5:])

    out = {}

    def put(name, res):
        w = args[name]
        out["grad_" + name], out["delta_" + name], out["new_m_" + name], out["new_v_" + name] = (r.reshape(w.shape) for r in res)

    for name, slots in zip(["ssd_w_in", "ssd_w_out", "conf_w_pw1", "conf_w_pw2", "ffn_w_in", "ffn_w_out"], big_r):
        put(name, _adam("adam_" + name, slots, _flat2(args[name]), _flat2(args["m_" + name]), _flat2(args["v_" + name])))
    put("ada_w", _adam("adam_ada_w", _flat2(g_ada_w)[None], _flat2(ada_w), _flat2(m_ada_w), _flat2(v_ada_w)))
    small_names = ["ssd_conv_w", "conf_b_pw1", "conf_dw_w", "conf_dw_b", "conf_ln_g", "conf_ln_b", "conf_b_pw2",
                   "c_ctx", "ada_b", "norm_mix_g", "norm_ffn_g", "final_norm_g", "ssd_conv_b", "ssd_dt_bias_f", "ssd_dt_bias_b",
                   "ssd_a_log_f", "ssd_a_log_b", "ssd_d_skip", "ssd_norm_w"]
    slots = list(small_r) + list(rep_g)

    def as2(a):
        return a.reshape((1, -1)) if a.ndim == 1 else _flat2(a)

    res = _adam_small("adam_small", slots, [as2(args[n]) for n in small_names], [as2(args["m_" + n]) for n in small_names],
                      [as2(args["v_" + n]) for n in small_names], scale=(small_names.index("c_ctx"), c_ctx[None, :]))
    for k, name in enumerate(small_names):
        put(name, [r[k] for r in res])
    return (loss, grad_x, *[out["grad_" + n] for n in names], *[out["delta_" + n] for n in names],
            *[out["new_m_" + n] for n in names], *[out["new_v_" + n] for n in names])


def _flat3(a):
    return a.reshape((a.shape[0], -1, a.shape[-1]))
```

```python
import functools

import jax
import jax.numpy as jnp
from jax import lax
from jax.experimental import pallas as pl
from jax.experimental.pallas import tpu as pltpu

F32 = jnp.float32
MXU = jnp.bfloat16
WIRE = jnp.bfloat16
ACT = jnp.bfloat16
NDEV = 8
AXES = ("x", "y", "c")
SSD_STATE = 128
SSD_CHUNK = 64
GRID_W = 64
EPS = 1e-6
ROW_TILE = 256
LANES = 128
ADAM_LR, ADAM_B1, ADAM_B2, ADAM_EPS, ADAM_WD, ADAM_STEP = 0.001, 0.9, 0.999, 1e-08, 0.01, 10
VMEM_CAP = 56 * 2 ** 20
MESH_ID = pl.DeviceIdType.MESH


def _pick(dim, cands):
    for c in cands:
        if dim % c == 0:
            return c
    return dim


def _nbytes(shape, dtype):
    n = 1
    for s in shape:
        n *= s
    return n * jnp.dtype(dtype).itemsize


def _vmem(nbytes):
    return int(min(VMEM_CAP, max(24 * 2 ** 20, 2 * nbytes + 8 * 2 ** 20)))


def _sigmoid(x):
    return 1.0 / (1.0 + jnp.exp(-x))


def _silu(x):
    return x * _sigmoid(x)


def _dsilu(x):
    s = _sigmoid(x)
    return s * (1.0 + x * (1.0 - s))


def _softplus(x):
    return jnp.maximum(x, 0.0) + jnp.log(1.0 + jnp.exp(-jnp.abs(x)))


def _dot(a, b, dims):
    return lax.dot_general(a.astype(MXU), b.astype(MXU), (dims, ((), ())), preferred_element_type=F32)


NN, NT, TN = ((1,), (0,)), ((1,), (1,)), ((0,), (0,))


def _split(a, parts):
    out = []
    for _ in range(parts):
        p = a.astype(MXU)
        out.append(p)
        a = a - p.astype(F32)
    return out


def _dot_lx(e, a, dims, parts=2):
    return sum(lax.dot_general(e, p, (dims, ((), ())), preferred_element_type=F32) for p in _split(a, parts))


def _dot_rx(a, e, dims, parts=2):
    return sum(lax.dot_general(p, e, (dims, ((), ())), preferred_element_type=F32) for p in _split(a, parts))


class _Comm:
    def __init__(self, gather=(), scatter=()):
        self.gather, self.scatter = list(gather), list(scatter)
        self.ng, self.n = len(self.gather), len(self.gather) + len(self.scatter)
        self.operands = self.gather + self.scatter
        self.specs = [pl.BlockSpec(memory_space=pl.ANY)] * self.n
        self.out_shape = ([jax.ShapeDtypeStruct((NDEV,) + a.shape, a.dtype) for a in self.gather]
                          + [jax.ShapeDtypeStruct(a.shape, a.dtype) for a in self.scatter])
        self.scratch = [pltpu.SemaphoreType.DMA((self.n, 7)), pltpu.SemaphoreType.DMA((self.n, 7)),
                        pltpu.SemaphoreType.DMA((self.n,))]

    def split(self, res):
        return res[:self.ng], res[self.ng:]

    def _copies(self, ins, outs, sems):
        send, recv, loc = sems
        ng, n = self.ng, self.n
        x, y, c = lax.axis_index("x"), lax.axis_index("y"), lax.axis_index("c")
        me, sib = (x, y, c), (x, y, 1 - c)
        chips = [(1 - x, y), (x, 1 - y), (1 - x, 1 - y)]

        def slot(p):
            return 4 * p[0] + 2 * p[1] + p[2]

        def rcopy(a, k, src, dst, to):
            return functools.partial(pltpu.make_async_remote_copy, src_ref=src, dst_ref=dst, send_sem=send.at[a, k],
                                     recv_sem=recv.at[a, k], device_id=to, device_id_type=MESH_ID)

        local = [functools.partial(pltpu.make_async_copy, ins[a] if a < ng else ins[a].at[slot(me)], outs[a].at[slot(me)],
                                   loc.at[a]) for a in range(n)]
        rel = [(fx, fy, fc) for fx in (0, 1) for fy in (0, 1) for fc in (0, 1)][1:]
        first, landed, passed = [], [], []
        for a in range(ng, n):
            for k, (fx, fy, fc) in enumerate(rel):
                p = (1 - x if fx else x, 1 - y if fy else y, 1 - c if fc else c)
                first.append(rcopy(a, k, ins[a].at[slot(p)], outs[a].at[slot(me)], p))
                blk = outs[a].at[slot(p)]
                landed.append(rcopy(a, k, blk, blk, me))
        for a in range(ng):
            dst = outs[a].at[slot(me)]
            first.append(rcopy(a, 0, ins[a], dst, sib))
            first += [rcopy(a, 1 + j, ins[a], dst, (*ch, c)) for j, ch in enumerate(chips)]
            blk = outs[a].at[slot(sib)]
            landed.append(rcopy(a, 0, blk, blk, me))
            for j, ch in enumerate(chips):
                blk = outs[a].at[slot((*ch, c))]
                passed.append((rcopy(a, 1 + j, blk, blk, me), rcopy(a, 4 + j, blk, blk, sib)))
                blk = outs[a].at[slot((*ch, 1 - c))]
                landed.append(rcopy(a, 4 + j, blk, blk, me))
        return local, first, passed, landed

    def start(self, ins, outs, sems):
        local, first, _, _ = self._copies(ins, outs, sems)
        for make in local + first:
            make().start()

    def finish(self, ins, outs, sems):
        local, first, passed, landed = self._copies(ins, outs, sems)
        onward = []
        for arrived, forward in passed:
            arrived().wait_recv()
            onward.append(forward())
            onward[-1].start()
        for make in landed:
            make().wait_recv()
        for make in first:
            make().wait_send()
        for cp in onward:
            cp.wait_send()
        for make in local:
            make().wait()


def _carry(body, comm, n_in, n_out, grid):
    if comm is None:
        return body
    n = comm.n

    def wrapped(*refs):
        own_in, c_in = refs[:n_in], refs[n_in:n_in + n]
        own_out, c_out = refs[n_in + n:n_in + n + n_out], refs[n_in + n + n_out:n_in + 2 * n + n_out]
        own_scr, sems = refs[n_in + 2 * n + n_out:-3], refs[-3:]
        ids = [pl.program_id(ax) for ax in range(len(grid))]
        first, last = ids[0] == 0, ids[0] == grid[0] - 1
        for ax in range(1, len(grid)):
            first, last = first & (ids[ax] == 0), last & (ids[ax] == grid[ax] - 1)

        @pl.when(first)
        def _():
            comm.start(c_in, c_out, sems)

        body(*own_in, *own_out, *own_scr)

        @pl.when(last)
        def _():
            comm.finish(c_in, c_out, sems)

    return wrapped


def _exchange(name, gather, scatter=()):
    comm = _Comm(gather, scatter)
    n = comm.n

    def body(*refs):
        ins, outs, sems = refs[:n], refs[n:2 * n], refs[2 * n:]
        comm.start(ins, outs, sems)
        comm.finish(ins, outs, sems)

    res = pl.pallas_call(body, name=name, out_shape=comm.out_shape, in_specs=comm.specs, out_specs=comm.specs,
                         scratch_shapes=comm.scratch)(*comm.operands)
    return comm.split(res)


def _hbm(a):
    return pltpu.with_memory_space_constraint(a, pltpu.HBM)


def _divs(dim, mult):
    return [dim] + [dim // parts for parts in range(2, dim // mult + 1) if dim % parts == 0 and (dim // parts) % mult == 0]


MM_VMEM_BUDGET = 40 * 2 ** 20
GRID_STEP_US = 0.35
HBM_BYTES_PER_US = 3.0e6


def _mm_tiles(m, n, k, sizes, mode, has_add):
    sa, sb, so = sizes
    sub = 16
    best = None
    for tk in _divs(k, LANES):
        for tn in _divs(n, LANES):
            for tm in _divs(m, LANES if mode == "tn" else sub):
                nk = k // tk
                out_t = tm * tn
                est = (2 * (tm * tk * sa + tk * tn * sb) + 2 * out_t * so + 2 * (tm * tk + tk * tn) + 4 * out_t
                       + (4 * out_t if nk > 1 else 0) + (8 * out_t if has_add else 0))
                if est > MM_VMEM_BUDGET:
                    continue
                steps = (m // tm) * (n // tn) * nk
                cost = steps * GRID_STEP_US + (tm * tk * sa + tk * tn * sb + out_t * so) / HBM_BYTES_PER_US
                if best is None or cost < best[0]:
                    best = (cost, tm, tn, tk, est)
    assert best is not None, (m, n, k)
    return best[1:]


def _mm(name, a, b, mode, out_dtype=F32, bias=None, add=None):
    if mode == "nn":
        (m, k), (k2, n) = a.shape, b.shape
    elif mode == "nt":
        (m, k), (n, k2) = a.shape, b.shape
    else:
        (k, m), (k2, n) = a.shape, b.shape
    assert k == k2, (name, a.shape, b.shape)
    sizes = (a.dtype.itemsize, b.dtype.itemsize, jnp.dtype(out_dtype).itemsize)
    tm, tn, tk, est = _mm_tiles(m, n, k, sizes, mode, add is not None)
    nk = k // tk
    dims = {"nn": NN, "nt": NT, "tn": TN}[mode]
    a_spec = pl.BlockSpec((tk, tm), lambda i, j, kk: (kk, i)) if mode == "tn" else pl.BlockSpec((tm, tk), lambda i, j, kk: (i, kk))
    b_spec = pl.BlockSpec((tn, tk), lambda i, j, kk: (j, kk)) if mode == "nt" else pl.BlockSpec((tk, tn), lambda i, j, kk: (kk, j))
    extra, extra_specs = [], []
    if bias is not None:
        extra.append(bias)
        extra_specs.append(pl.BlockSpec((1, tn), lambda i, j, kk: (0, j)))
    if add is not None:
        extra.append(add)
        extra_specs.append(pl.BlockSpec((tm, tn), lambda i, j, kk: (i, j)))

    def finish(r, extras, o_ref):
        for e in extras:
            r = r + e[...].astype(F32)
        o_ref[...] = r.astype(o_ref.dtype)

    def body_acc(*refs):
        a_ref, b_ref = refs[:2]
        o_ref, acc = refs[-2:]
        kk = pl.program_id(2)

        @pl.when(kk == 0)
        def _():
            acc[...] = jnp.zeros_like(acc)

        acc[...] += _dot(a_ref[...], b_ref[...], dims)

        @pl.when(kk == nk - 1)
        def _():
            finish(acc[...], refs[2:-2], o_ref)

    def body_one(*refs):
        finish(_dot(refs[0][...], refs[1][...], dims), refs[2:-1], refs[-1])

    return pl.pallas_call(
        body_acc if nk > 1 else body_one, name=name, grid=(m // tm, n // tn, nk),
        out_shape=pltpu.HBM((m, n), out_dtype),
        in_specs=[a_spec, b_spec] + extra_specs, out_specs=pl.BlockSpec((tm, tn), lambda i, j, kk: (i, j)),
        scratch_shapes=[pltpu.VMEM((tm, tn), F32)] if nk > 1 else [],
        compiler_params=pltpu.CompilerParams(dimension_semantics=("parallel", "parallel", "arbitrary"),
                                             vmem_limit_bytes=int(min(VMEM_CAP, est + 12 * 2 ** 20))),
    )(*[_hbm(v) for v in (a, b, *extra)])


def _ri(arr, w=None, cb=0, ro=0, lead=None):
    return (arr, arr.shape[-1] if w is None else w, cb, ro, lead)


def _rowwise(name, fn, nrows, row_ins, bc_ins, outs, accs=()):
    tr = min(ROW_TILE, nrows)
    assert nrows % tr == 0
    in_specs = []
    for (arr, w, cb, ro, lead) in row_ins:
        if lead is None:
            in_specs.append(pl.BlockSpec((tr, w), lambda i, cb=cb, ro=ro: (jnp.maximum(i + ro, 0), cb)))
        else:
            in_specs.append(pl.BlockSpec((None, tr, w), lambda i, cb=cb, ro=ro, lead=lead: (lead, jnp.maximum(i + ro, 0), cb)))
    for arr in bc_ins:
        in_specs.append(pl.BlockSpec(arr.shape, lambda i, nd=arr.ndim: (0,) * nd))
    out_shape = [pltpu.HBM((nrows, c), dt) for c, dt in outs] + [pltpu.HBM(s, F32) for s in accs]
    out_specs = [pl.BlockSpec((tr, c), lambda i: (i, 0)) for c, _ in outs] + [pl.BlockSpec(s, lambda i: (0, 0)) for s in accs]
    nr, nb, no = len(row_ins), len(bc_ins), len(outs)

    def body(*refs):
        i = pl.program_id(0)
        rows = [r[...].astype(F32) for r in refs[:nr]]
        bcs = [r[...] for r in refs[nr:nr + nb]]
        o, a = fn(rows, bcs, i)
        for ref, val in zip(refs[nr + nb:nr + nb + no], o):
            ref[...] = val.astype(ref.dtype)
        for ref, val in zip(refs[nr + nb + no:], a):
            @pl.when(i == 0)
            def _(ref=ref, val=val):
                ref[...] = val

            @pl.when(i > 0)
            def _(ref=ref, val=val):
                ref[...] += val

    est = sum(tr * w * arr.dtype.itemsize for (arr, w, _, _, _) in row_ins) + sum(tr * c * 4 for c, _ in outs)
    res = pl.pallas_call(
        body, name=name, grid=(nrows // tr,), out_shape=out_shape, in_specs=in_specs, out_specs=out_specs,
        compiler_params=pltpu.CompilerParams(dimension_semantics=("arbitrary",), vmem_limit_bytes=_vmem(3 * est)),
    )(*[_hbm(r[0]) for r in row_ins], *[_hbm(v) for v in bc_ins])
    return res[:no], res[no:]


def _colsum(v):
    return jnp.sum(v, axis=0, keepdims=True)


def _normmod_fwd(name, h, g, s, sh, nct=0):
    d = h.shape[1]

    def fn(rows, bcs, i):
        hh, (g_, s_, sh_) = rows[0], bcs
        s1 = jnp.where(i < nct, s_[0:1], s_[1:2])
        sh1 = jnp.where(i < nct, sh_[0:1], sh_[1:2])
        r = lax.rsqrt(jnp.mean(hh * hh, axis=-1, keepdims=True) + EPS)
        return [hh * r * g_ * (1.0 + s1) + sh1], []

    return _rowwise(name, fn, h.shape[0], [_ri(h)], [g, s, sh], [(d, MXU)])[0][0]


def _normmod_bwd(name, h, dxn, dres, g, s, nct=0):
    d = h.shape[1]

    def fn(rows, bcs, i):
        hh, dx, dr = rows
        g_, s_ = bcs
        ctx = i < nct
        s1 = jnp.where(ctx, s_[0:1], s_[1:2])
        r = lax.rsqrt(jnp.mean(hh * hh, axis=-1, keepdims=True) + EPS)
        hr = hh * r
        dy = dx * (1.0 + s1)
        u = dy * g_
        dh = r * u - hr * (r * r) * jnp.mean(u * hh, axis=-1, keepdims=True)
        dh = dh + jnp.where(ctx, 0.0, dr)

        def seg(v):
            v = _colsum(v)
            return jnp.concatenate([jnp.where(ctx, v, 0.0), jnp.where(ctx, 0.0, v)], axis=0)

        return [dh], [seg(dx), seg(dx * hr * g_), seg(dy * hr)]

    (dh,), (dsh, ds, dg) = _rowwise(name, fn, h.shape[0], [_ri(h), _ri(dxn), _ri(dres, ro=-nct)], [g, s],
                                    [(d, F32)], [(2, d)] * 3)
    return dh, dsh, ds, dg


def _resnorm_fwd(name, h, y, gate, g, s, sh):
    d = h.shape[1]

    def fn(rows, bcs, i):
        hh, yy = rows
        gate_, g_, s_, sh_ = bcs
        hn = hh + gate_ * yy
        r = lax.rsqrt(jnp.mean(hn * hn, axis=-1, keepdims=True) + EPS)
        return [hn, hn * r * g_ * (1.0 + s_) + sh_], []

    return _rowwise(name, fn, h.shape[0], [_ri(h), _ri(y)], [gate, g, s, sh], [(d, F32), (d, MXU)])[0]


def _gate_bwd(name, dh, y, gate):
    d = dh.shape[1]

    def fn(rows, bcs, i):
        dd, yy = rows
        dy = dd * bcs[0]
        return [dy], [_colsum(dd * yy), _colsum(dy)]

    (dy,), (dgate, dbias) = _rowwise(name, fn, dh.shape[0], [_ri(dh), _ri(y)], [gate], [(d, MXU)], [(1, d)] * 2)
    return dy, dgate, dbias


def _swiglu_fwd(name, u):
    f = u.shape[1] // 2

    def fn(rows, bcs, i):
        return [_silu(rows[0]) * rows[1]], []

    return _rowwise(name, fn, u.shape[0], [_ri(u, f, 0), _ri(u, f, 1)], [], [(f, MXU)])[0][0]


def _swiglu_bwd(name, u, dhid):
    f = u.shape[1] // 2

    def fn(rows, bcs, i):
        a, b, dd = rows
        return [jnp.concatenate([dd * b * _dsilu(a), dd * _silu(a)], axis=1)], []

    return _rowwise(name, fn, u.shape[0], [_ri(u, f, 0), _ri(u, f, 1), _ri(dhid)], [], [(2 * f, MXU)])[0][0]


def _glu_fwd(name, u):
    d = u.shape[1] // 2

    def fn(rows, bcs, i):
        return [rows[0] * _sigmoid(rows[1])], []

    return _rowwise(name, fn, u.shape[0], [_ri(u, d, 0), _ri(u, d, 1)], [], [(d, F32)])[0][0]


def _glu_bwd(name, u, dgl):
    d = u.shape[1] // 2

    def fn(rows, bcs, i):
        a, b, dd = rows
        sg = _sigmoid(b)
        du = jnp.concatenate([dd * sg, dd * a * sg * (1.0 - sg)], axis=1)
        return [du], [_colsum(du)]

    (du,), (db,) = _rowwise(name, fn, u.shape[0], [_ri(u, d, 0), _ri(u, d, 1), _ri(dgl)], [], [(2 * d, MXU)], [(1, 2 * d)])
    return du, db


def _ln_silu_fwd(name, v, g, b):
    d = v.shape[1]

    def fn(rows, bcs, i):
        vv = rows[0]
        mu = jnp.mean(vv, axis=-1, keepdims=True)
        xc = vv - mu
        rs = lax.rsqrt(jnp.mean(xc * xc, axis=-1, keepdims=True) + EPS)
        return [_silu(xc * rs * bcs[0] + bcs[1])], []

    return _rowwise(name, fn, v.shape[0], [_ri(v)], [g, b], [(d, MXU)])[0][0]


def _ln_silu_bwd(name, v, ds, g, b):
    d = v.shape[1]

    def fn(rows, bcs, i):
        vv, dd = rows
        mu = jnp.mean(vv, axis=-1, keepdims=True)
        xc = vv - mu
        rs = lax.rsqrt(jnp.mean(xc * xc, axis=-1, keepdims=True) + EPS)
        xh = xc * rs
        dln = dd * _dsilu(xh * bcs[0] + bcs[1])
        dxh = dln * bcs[0]
        dv = rs * (dxh - jnp.mean(dxh, axis=-1, keepdims=True) - xh * jnp.mean(dxh * xh, axis=-1, keepdims=True))
        return [dv], [_colsum(dln * xh), _colsum(dln)]

    (dv,), (dg, db) = _rowwise(name, fn, v.shape[0], [_ri(v), _ri(ds)], [g, b], [(d, F32)], [(1, d)] * 2)
    return dv, dg, db


def _final_loss(name, h, f, target, gate, gf):
    d = h.shape[1]

    def fn(rows, bcs, i):
        hh, ff, tg = rows
        gate_, g_ = bcs
        hn = hh + gate_ * ff
        r = lax.rsqrt(jnp.mean(hn * hn, axis=-1, keepdims=True) + EPS)
        hr = hn * r
        err = hr * g_ - tg
        dout = err * (1.0 / d)
        u = dout * g_
        dh = r * u - hr * (r * r) * jnp.mean(u * hn, axis=-1, keepdims=True)
        sq = jnp.sum(_colsum(err * err), axis=1, keepdims=True)
        return [dh], [jnp.broadcast_to(sq, (1, LANES)), _colsum(dout * hr)]

    (dh,), (sq, dgf) = _rowwise(name, fn, h.shape[0], [_ri(h), _ri(f), _ri(target)], [gate, gf], [(d, F32)], [(1, LANES), (1, d)])
    return dh, sq, dgf


def _shift_rows(x, o, seg_lo, seg_hi, row):
    n = x.shape[0]
    if o == 0:
        return x
    sh = pltpu.roll(x, (-o) % n, 0)
    ok = (row + o >= seg_lo) & (row + o < seg_hi)
    return jnp.where(ok, sh, 0.0)


def _seg_bounds(row, tc, tt):
    ctx = row < tc
    return jnp.where(ctx, 0, tc), jnp.where(ctx, tc, tt)


def _ssd_conv_fwd(name, zx, w, b, di, tc):
    tt, kc, cd = zx.shape[0], w.shape[0], w.shape[1]
    cb = _pick(cd, (LANES,))
    off = di // cb

    def body(x_ref, w_ref, b_ref, o_ref):
        x = x_ref[...].astype(F32)
        row = lax.broadcasted_iota(jnp.int32, (tt, 1), 0)
        lo, hi = _seg_bounds(row, tc, tt)
        acc = jnp.broadcast_to(b_ref[...], x.shape)
        for k in range(kc):
            acc = acc + w_ref[k:k + 1, :] * _shift_rows(x, k - kc // 2, lo, hi, row)
        o_ref[...] = _silu(acc).astype(o_ref.dtype)

    return pl.pallas_call(
        body, name=name, grid=(cd // cb,), out_shape=pltpu.HBM((tt, cd), ACT),
        in_specs=[pl.BlockSpec((tt, cb), lambda j: (0, j + off)), pl.BlockSpec((kc, cb), lambda j: (0, j)),
                  pl.BlockSpec((1, cb), lambda j: (0, j))],
        out_specs=pl.BlockSpec((tt, cb), lambda j: (0, j)),
        compiler_params=pltpu.CompilerParams(dimension_semantics=("parallel",), vmem_limit_bytes=_vmem(4 * tt * cb * 4)),
    )(_hbm(zx), _hbm(w), _hbm(b))


def _ssd_conv_bwd(name, zx, dact2, w, b, di, tc):
    tt, kc, cd = zx.shape[0], w.shape[0], w.shape[1]
    cb = _pick(cd, (LANES,))
    off = di // cb

    def body(x_ref, d0_ref, d1_ref, w_ref, b_ref, dx_ref, dw_ref, db_ref):
        x = x_ref[...].astype(F32)
        row = lax.broadcasted_iota(jnp.int32, (tt, 1), 0)
        lo, hi = _seg_bounds(row, tc, tt)
        pre = jnp.broadcast_to(b_ref[...], x.shape)
        for k in range(kc):
            pre = pre + w_ref[k:k + 1, :] * _shift_rows(x, k - kc // 2, lo, hi, row)
        dpre = (d0_ref[...].astype(F32) + d1_ref[...].astype(F32)) * _dsilu(pre)
        dx = jnp.zeros_like(x)
        for k in range(kc):
            o = k - kc // 2
            dx = dx + w_ref[k:k + 1, :] * _shift_rows(dpre, -o, lo, hi, row)
            dw_ref[k:k + 1, :] = _colsum(dpre * _shift_rows(x, o, lo, hi, row))
        dx_ref[...] = dx.astype(dx_ref.dtype)
        db_ref[...] = _colsum(dpre)

    blk = pl.BlockSpec((tt, cb), lambda j: (0, j))
    return pl.pallas_call(
        body, name=name, grid=(cd // cb,),
        out_shape=[pltpu.HBM((tt, cd), MXU), pltpu.HBM((kc, cd), F32), pltpu.HBM((1, cd), F32)],
        in_specs=[pl.BlockSpec((tt, cb), lambda j: (0, j + off)), pl.BlockSpec((None, tt, cb), lambda j: (0, 0, j)),
                  pl.BlockSpec((None, tt, cb), lambda j: (1, 0, j)), pl.BlockSpec((kc, cb), lambda j: (0, j)),
                  pl.BlockSpec((1, cb), lambda j: (0, j))],
        out_specs=[blk, pl.BlockSpec((kc, cb), lambda j: (0, j)), pl.BlockSpec((1, cb), lambda j: (0, j))],
        compiler_params=pltpu.CompilerParams(dimension_semantics=("parallel",), vmem_limit_bytes=_vmem(8 * tt * cb * 4)),
    )(_hbm(zx), _hbm(dact2), _hbm(dact2), _hbm(w), _hbm(b))


def _strided_conv(name, x, w, b, stride):
    t, ch = x.shape
    kk = w.shape[0]
    pad = (kk // 2) * stride
    cb = _pick(ch, (LANES,))
    has_b = b is not None

    def body(*refs):
        x_ref, w_ref = refs[:2]
        o_ref, xp = refs[-2:]
        xp[0:pad, :] = jnp.zeros((pad, cb), F32)
        xp[pad + t:, :] = jnp.zeros((pad, cb), F32)
        xp[pad:pad + t, :] = x_ref[...]
        acc = jnp.broadcast_to(refs[2][...], (t, cb)) if has_b else jnp.zeros((t, cb), F32)
        for k in range(kk):
            acc = acc + w_ref[k:k + 1, :] * xp[k * stride:k * stride + t, :]
        o_ref[...] = acc

    ins, specs = [x, w], [pl.BlockSpec((t, cb), lambda j: (0, j)), pl.BlockSpec((kk, cb), lambda j: (0, j))]
    if has_b:
        ins.append(b)
        specs.append(pl.BlockSpec((1, cb), lambda j: (0, j)))
    return pl.pallas_call(
        body, name=name, grid=(ch // cb,), out_shape=pltpu.HBM((t, ch), F32), in_specs=specs,
        out_specs=pl.BlockSpec((t, cb), lambda j: (0, j)), scratch_shapes=[pltpu.VMEM((t + 2 * pad, cb), F32)],
        compiler_params=pltpu.CompilerParams(dimension_semantics=("parallel",), vmem_limit_bytes=_vmem(6 * t * cb * 4)),
    )(*[_hbm(v) for v in ins])


def _strided_conv_dw(name, x, dv, kk, stride):
    t, ch = x.shape
    pad = (kk // 2) * stride
    cb = _pick(ch, (LANES,))

    def body(x_ref, d_ref, dw_ref, db_ref, xp):
        xp[0:pad, :] = jnp.zeros((pad, cb), F32)
        xp[pad + t:, :] = jnp.zeros((pad, cb), F32)
        xp[pad:pad + t, :] = x_ref[...]
        d = d_ref[...]
        for k in range(kk):
            dw_ref[k:k + 1, :] = _colsum(d * xp[k * stride:k * stride + t, :])
        db_ref[...] = _colsum(d)

    blk = pl.BlockSpec((t, cb), lambda j: (0, j))
    return pl.pallas_call(
        body, name=name, grid=(ch // cb,), out_shape=[pltpu.HBM((kk, ch), F32), pltpu.HBM((1, ch), F32)],
        in_specs=[blk, blk], out_specs=[pl.BlockSpec((kk, cb), lambda j: (0, j)), pl.BlockSpec((1, cb), lambda j: (0, j))],
        scratch_shapes=[pltpu.VMEM((t + 2 * pad, cb), F32)],
        compiler_params=pltpu.CompilerParams(dimension_semantics=("parallel",), vmem_limit_bytes=_vmem(6 * t * cb * 4)),
    )(_hbm(x), _hbm(dv))


def _grid_t(a, n1, n2):
    return a.reshape(n1, n2, a.shape[-1]).swapaxes(0, 1).reshape(n1 * n2, a.shape[-1])


def _chunk_order(d, i, ncc, nc):
    back = jnp.where(i < ncc, ncc - 1 - i, nc - 1 - (i - ncc))
    return jnp.where(d == 0, i, back)


def _ssd_chunk_setup(d, dt_raw, bias, a_log, q, h, di):
    p = di // h
    dt = _softplus(dt_raw + bias)
    a_neg = -jnp.exp(a_log)
    delta = dt * a_neg
    r = lax.broadcasted_iota(jnp.int32, (q, q), 0)
    c = lax.broadcasted_iota(jnp.int32, (q, q), 1)
    sgn = 1 - 2 * d
    mask = (r - c) * sgn >= 0
    mask_t = (c - r) * sgn >= 0
    a = _dot_lx(mask.astype(MXU), delta, NN, parts=3)
    tot = _colsum(delta)
    ea, dte, cd = jnp.exp(a), jnp.exp(tot - a), jnp.exp(tot)
    hh = lax.broadcasted_iota(jnp.int32, (h, di), 0)
    cc = lax.broadcasted_iota(jnp.int32, (h, di), 1)
    e = (cc // p == hh).astype(MXU)
    ex = _dot_rx(jnp.concatenate([dt, ea, dte, jnp.broadcast_to(cd, (8, h))], axis=0), e, NN)
    eye = (lax.broadcasted_iota(jnp.int32, (h, h), 0) == lax.broadcasted_iota(jnp.int32, (h, h), 1)).astype(MXU)
    a_t = _dot_lx(eye, a, NT, parts=3)
    return dict(dt=dt, a_neg=a_neg, a=a, a_t=a_t, mask=mask, mask_t=mask_t, e=e,
                dt_e=ex[0:q], ea_e=ex[q:2 * q], dte_e=ex[2 * q:3 * q], cd_e=ex[3 * q:3 * q + 1])


def _pick_heads(r, q, hpg, p):
    lane = lax.broadcasted_iota(jnp.int32, (q, hpg * p), 1) // p
    out = jnp.zeros((q, hpg * p), F32)
    for j in range(hpg):
        out = out + jnp.where(lane == j, r[j * q:(j + 1) * q], 0.0)
    return out


def _ssd_fwd(name, xbc, dt2, bias2, alog2, di, tc, comm=None):
    tt, cd = xbc.shape
    h = dt2.shape[-1]
    q, n = SSD_CHUNK, SSD_STATE
    gn = (cd - di) // 2
    g = gn // n
    hpg, p = h // g, di // h
    gp = hpg * p
    nc, ncc = tt // q, tc // q
    assert di % gn == 0

    def body(x_ref, b_ref, c_ref, dt_ref, bias_ref, alog_ref, y_ref, hp_ref, ht):
        d, i = pl.program_id(0), pl.program_id(1)

        @pl.when(i == 0)
        def _():
            ht[...] = jnp.zeros_like(ht)

        s = _ssd_chunk_setup(d, dt_ref[...], bias_ref[...], alog_ref[...], q, h, di)
        xd = x_ref[...].astype(F32) * s["dt_e"]
        hp_ref[...] = ht[...].astype(hp_ref.dtype)
        for gi in range(g):
            bg, cg = b_ref[:, gi * n:(gi + 1) * n].astype(MXU), c_ref[:, gi * n:(gi + 1) * n].astype(MXU)
            sl = slice(gi * gp, (gi + 1) * gp)
            sc = _dot(cg, bg, NT)
            ms = []
            for j in range(hpg):
                hd = gi * hpg + j
                seg = s["a"][:, hd:hd + 1] - s["a_t"][hd:hd + 1, :]
                ms.append(sc * jnp.exp(jnp.where(s["mask"], seg, -jnp.inf)))
            xdg = xd[:, sl]
            ydiag = _pick_heads(_dot(jnp.concatenate(ms, axis=0), xdg, NN), q, hpg, p)
            htg = ht[:, sl]
            y_ref[:, sl] = ydiag + _dot(cg, htg, NN) * s["ea_e"][:, sl]
            ht[:, sl] = s["cd_e"][:, sl] * htg + _dot(bg, xdg * s["dte_e"][:, sl], TN)

    def cidx(d, i):
        return _chunk_order(d, i, ncc, nc)

    cm = comm if comm is not None else _Comm()
    res = pl.pallas_call(
        _carry(body, comm, 6, 2, (2, nc)), name=name, grid=(2, nc),
        out_shape=[pltpu.HBM((2, tt, di), F32), pltpu.HBM((2, nc, n, di), ACT)] + cm.out_shape,
        in_specs=[pl.BlockSpec((q, di), lambda d, i: (cidx(d, i), 0)),
                  pl.BlockSpec((q, gn), lambda d, i: (cidx(d, i), di // gn)),
                  pl.BlockSpec((q, gn), lambda d, i: (cidx(d, i), di // gn + 1)),
                  pl.BlockSpec((None, q, h), lambda d, i: (d, cidx(d, i), 0)),
                  pl.BlockSpec((None, 1, h), lambda d, i: (d, 0, 0)),
                  pl.BlockSpec((None, 1, h), lambda d, i: (d, 0, 0))] + cm.specs,
        out_specs=[pl.BlockSpec((None, q, di), lambda d, i: (d, cidx(d, i), 0)),
                   pl.BlockSpec((None, None, n, di), lambda d, i: (d, cidx(d, i), 0, 0))] + cm.specs,
        scratch_shapes=[pltpu.VMEM((n, di), F32)] + (cm.scratch if comm is not None else []),
        compiler_params=pltpu.CompilerParams(dimension_semantics=("arbitrary", "arbitrary"), vmem_limit_bytes=_vmem(16 * q * di * 4)),
    )(*[_hbm(v) for v in (xbc, xbc, xbc, dt2, bias2, alog2)], *cm.operands)
    return res[0], res[1], cm.split(res[2:])


def _ssd_bwd(name, xbc, dt2, bias2, alog2, dy, hp2, dskip_e, di, tc, comm=None):
    tt, cd = xbc.shape
    h = dt2.shape[-1]
    q, n = SSD_CHUNK, SSD_STATE
    gn = (cd - di) // 2
    g = gn // n
    hpg, p = h // g, di // h
    gp = hpg * p
    nc, ncc = tt // q, tc // q

    def body(x_ref, b_ref, c_ref, dt_ref, bias_ref, alog_ref, dy_ref, hp_ref, dsk_ref,
             dxbc_ref, ddt_ref, dalog_ref, dbias_ref, dht, dxd, off):
        d, i = pl.program_id(0), pl.program_id(1)

        @pl.when(i == 0)
        def _():
            dht[...] = jnp.zeros_like(dht)
            dalog_ref[...] = jnp.zeros_like(dalog_ref)
            dbias_ref[...] = jnp.zeros_like(dbias_ref)

        s = _ssd_chunk_setup(d, dt_ref[...], bias_ref[...], alog_ref[...], q, h, di)
        x, dyc = x_ref[...].astype(F32), dy_ref[...]
        xd = x * s["dt_e"]
        dyea = dyc * s["ea_e"]
        xdte = xd * s["dte_e"]
        lane = lax.broadcasted_iota(jnp.int32, (q, gp), 1) // p
        lane_h = lax.broadcasted_iota(jnp.int32, (q, h), 1)
        da_d = jnp.zeros((q, h), F32)
        last_e = []
        for gi in range(g):
            bg, cg = b_ref[:, gi * n:(gi + 1) * n].astype(MXU), c_ref[:, gi * n:(gi + 1) * n].astype(MXU)
            sl = slice(gi * gp, (gi + 1) * gp)
            sc, sct = _dot(cg, bg, NT), _dot(bg, cg, NT)
            dyg, xdg = dyc[:, sl], xd[:, sl]
            htg, dhtg = hp_ref[:, sl].astype(F32), dht[:, sl]
            dystack = jnp.concatenate([jnp.where(lane == j, dyg, 0.0) for j in range(hpg)], axis=0)
            xdstack = jnp.concatenate([jnp.where(lane == j, xdg, 0.0) for j in range(hpg)], axis=0)
            gs = _dot(dystack, xdg, NT)
            gst = _dot(xdstack, dyg, NT)
            ds = jnp.zeros((q, q), F32)
            mts = []
            for j in range(hpg):
                hd = gi * hpg + j
                col, rw = s["a"][:, hd:hd + 1], s["a_t"][hd:hd + 1, :]
                gl = gs[j * q:(j + 1) * q] * jnp.exp(jnp.where(s["mask"], col - rw, -jnp.inf))
                ds = ds + gl
                mt = sct * jnp.exp(jnp.where(s["mask_t"], rw - col, -jnp.inf))
                mts.append(mt)
                da_j = jnp.sum(gl * sc, axis=1, keepdims=True) - jnp.sum(gst[j * q:(j + 1) * q] * mt, axis=1, keepdims=True)
                da_d = da_d + jnp.where(lane_h == hd, da_j, 0.0)
            dxd_diag = _pick_heads(_dot(jnp.concatenate(mts, axis=0), dyg, NN), q, hpg, p)
            z = _dot(bg, dhtg, NN) * s["dte_e"][:, sl]
            yoff = _dot(cg, htg, NN) * s["ea_e"][:, sl]
            off[:, sl] = dyg * yoff - xdg * z
            dxd[:, sl] = dxd_diag + z
            dxbc_ref[:, di + gi * n:di + (gi + 1) * n] = (_dot(ds, cg, TN) + _dot(xdte[:, sl], dhtg, NT)).astype(dxbc_ref.dtype)
            dxbc_ref[:, di + gn + gi * n:di + gn + (gi + 1) * n] = (_dot(ds, bg, NN)
                                                                    + _dot(dyea[:, sl], htg, NT)).astype(dxbc_ref.dtype)
            last_e.append(s["cd_e"][:, sl] * _colsum(dhtg * htg) + _colsum(xdg * z))
            dht[:, sl] = s["cd_e"][:, sl] * dhtg + _dot(cg, dyea[:, sl], TN)
        dxd_all = dxd[...]
        last = jnp.concatenate(last_e, axis=1)
        da = da_d + _dot_rx(off[...], s["e"], NT)
        last_h = _dot_rx(jnp.broadcast_to(last, (8, di)), s["e"], NT)[0:1]
        ddelta = _dot_lx(s["mask_t"].astype(MXU), da, NN, parts=3) + last_h
        ddt = ddelta * s["a_neg"] + _dot_rx(dxd_all * x, s["e"], NT)
        ddt_raw = ddt * _sigmoid(dt_ref[...] + bias_ref[...])
        ddt_ref[...] = ddt_raw
        dalog_ref[...] += _colsum(ddelta * s["dt"]) * s["a_neg"]
        dbias_ref[...] += _colsum(ddt_raw)
        dxbc_ref[:, 0:di] = (dxd_all * s["dt_e"] + jnp.where(d == 0, dyc * dsk_ref[...], 0.0)).astype(dxbc_ref.dtype)

    def cidx(d, i):
        return _chunk_order(d, nc - 1 - i, ncc, nc)

    cm = comm if comm is not None else _Comm()
    res = pl.pallas_call(
        _carry(body, comm, 9, 4, (2, nc)), name=name, grid=(2, nc),
        out_shape=[pltpu.HBM((2, tt, cd), ACT), pltpu.HBM((2, tt, h), F32),
                   pltpu.HBM((2, 1, h), F32), pltpu.HBM((2, 1, h), F32)] + cm.out_shape,
        in_specs=[pl.BlockSpec((q, di), lambda d, i: (cidx(d, i), 0)),
                  pl.BlockSpec((q, gn), lambda d, i: (cidx(d, i), di // gn)),
                  pl.BlockSpec((q, gn), lambda d, i: (cidx(d, i), di // gn + 1)),
                  pl.BlockSpec((None, q, h), lambda d, i: (d, cidx(d, i), 0)),
                  pl.BlockSpec((None, 1, h), lambda d, i: (d, 0, 0)),
                  pl.BlockSpec((None, 1, h), lambda d, i: (d, 0, 0)),
                  pl.BlockSpec((q, di), lambda d, i: (cidx(d, i), 0)),
                  pl.BlockSpec((None, None, n, di), lambda d, i: (d, cidx(d, i), 0, 0)),
                  pl.BlockSpec((1, di), lambda d, i: (0, 0))] + cm.specs,
        out_specs=[pl.BlockSpec((None, q, cd), lambda d, i: (d, cidx(d, i), 0)),
                   pl.BlockSpec((None, q, h), lambda d, i: (d, cidx(d, i), 0)),
                   pl.BlockSpec((None, 1, h), lambda d, i: (d, 0, 0)),
                   pl.BlockSpec((None, 1, h), lambda d, i: (d, 0, 0))] + cm.specs,
        scratch_shapes=[pltpu.VMEM((n, di), F32), pltpu.VMEM((q, di), F32), pltpu.VMEM((q, di), F32)]
        + (cm.scratch if comm is not None else []),
        compiler_params=pltpu.CompilerParams(dimension_semantics=("arbitrary", "arbitrary"), vmem_limit_bytes=_vmem(24 * q * di * 4)),
    )(*[_hbm(v) for v in (xbc, xbc, xbc, dt2, bias2, alog2, dy, hp2, dskip_e)], *cm.operands)
    return res[0], res[1], res[2], res[3], cm.split(res[4:])


def _ssd_gate_fwd(name, y2, xbc, zx, dskip_e, norm_w, di, nct, t):
    def fn(rows, bcs, i):
        yf, yb, xs, z = rows
        zg = (yf + yb + bcs[0] * xs) * _silu(z)
        rn = lax.rsqrt(jnp.mean(zg * zg, axis=-1, keepdims=True) + EPS)
        return [zg * rn * bcs[1]], []

    ins = [_ri(y2, lead=0, ro=nct), _ri(y2, lead=1, ro=nct), _ri(xbc, di, 0, ro=nct), _ri(zx, di, 0, ro=nct)]
    return _rowwise(name, fn, t, ins, [dskip_e, norm_w], [(di, MXU)])[0][0]


def _ssd_gate_bwd(name, dyn, y2, xbc, zx, dskip_e, norm_w, di, nct, tt):
    def fn(rows, bcs, i):
        dn, yf, yb, xs, z = rows
        lat = i >= nct
        ytot = yf + yb + bcs[0] * xs
        sz = _silu(z)
        zg = ytot * sz
        rn = lax.rsqrt(jnp.mean(zg * zg, axis=-1, keepdims=True) + EPS)
        u = dn * bcs[1]
        dzg = rn * u - zg * (rn * rn * rn) * jnp.mean(u * zg, axis=-1, keepdims=True)
        dy = jnp.where(lat, dzg * sz, 0.0)
        dz = jnp.where(lat, dzg * ytot * _dsilu(z), 0.0)
        return [dy, dz], [jnp.where(lat, _colsum(dn * zg * rn), 0.0), jnp.where(lat, _colsum(dy * xs), 0.0)]

    ins = [_ri(dyn, ro=-nct), _ri(y2, lead=0), _ri(y2, lead=1), _ri(xbc, di, 0), _ri(zx, di, 0)]
    (dy, dz), (dnw, ddsk) = _rowwise(name, fn, tt, ins, [dskip_e, norm_w], [(di, F32), (di, MXU)], [(1, di)] * 2)
    return dy, dz, dnw, ddsk


def _ada_fwd(name, cs, w, b):
    nl, d, c = w.shape
    r = cs.shape[0]

    def body(cs_ref, w_ref, b_ref, o_ref):
        o_ref[...] = _dot(_silu(cs_ref[...]), w_ref[...], NN) + b_ref[...]

    return pl.pallas_call(
        body, name=name, grid=(nl,), out_shape=pltpu.HBM((nl, r, c), F32),
        in_specs=[pl.BlockSpec((r, d), lambda l: (0, 0)), pl.BlockSpec((None, d, c), lambda l: (l, 0, 0)),
                  pl.BlockSpec((None, 1, c), lambda l: (l, 0, 0))],
        out_specs=pl.BlockSpec((None, r, c), lambda l: (l, 0, 0)),
        compiler_params=pltpu.CompilerParams(dimension_semantics=("parallel",), vmem_limit_bytes=_vmem(2 * d * c * 4)),
    )(_hbm(cs), _hbm(w), _hbm(b))


def _ada_bwd(name, cs, w, dmod):
    nl, d, c = w.shape
    r = cs.shape[0]

    def body(cs_ref, w_ref, dm_ref, dw_ref, dsc_ref):
        dm = dm_ref[...]
        dw_ref[...] = _dot(_silu(cs_ref[...]), dm, TN)

        @pl.when(pl.program_id(0) == 0)
        def _():
            dctx = jnp.broadcast_to(_colsum(dm[r // 2:]), (8, c))
            dsc_ref[...] = _dot(dctx, w_ref[...], NT)[0:1]

    return pl.pallas_call(
        body, name=name, grid=(nl,), out_shape=[pltpu.HBM((nl, d, c), F32), pltpu.HBM((1, d), F32)],
        in_specs=[pl.BlockSpec((r, d), lambda l: (0, 0)), pl.BlockSpec((None, d, c), lambda l: (l, 0, 0)),
                  pl.BlockSpec((None, r, c), lambda l: (l, 0, 0))],
        out_specs=[pl.BlockSpec((None, d, c), lambda l: (l, 0, 0)), pl.BlockSpec((1, d), lambda l: (0, 0))],
        compiler_params=pltpu.CompilerParams(dimension_semantics=("arbitrary",), vmem_limit_bytes=_vmem(4 * d * c * 4)),
    )(_hbm(cs), _hbm(w), _hbm(dmod))


def _adam_math(w, g, m, v):
    m = ADAM_B1 * m + (1.0 - ADAM_B1) * g
    v = ADAM_B2 * v + (1.0 - ADAM_B2) * (g * g)
    m_hat = m / (1.0 - ADAM_B1 ** ADAM_STEP)
    v_hat = v / (1.0 - ADAM_B2 ** ADAM_STEP)
    delta = -ADAM_LR * (m_hat / (jnp.sqrt(v_hat) + ADAM_EPS) + ADAM_WD * w)
    return delta, m, v


def _adam(name, slots, w, m, v):
    ns, r, c = slots.shape
    tr = _pick(r, (256, 128, 64, 32, 16, 8))

    def body(s_ref, w_ref, m_ref, v_ref, g_ref, d_ref, mo_ref, vo_ref):
        g = s_ref[0].astype(F32)
        for k in range(1, ns):
            g = g + s_ref[k].astype(F32)
        d, mn, vn = _adam_math(w_ref[...], g, m_ref[...], v_ref[...])
        g_ref[...], d_ref[...], mo_ref[...], vo_ref[...] = g, d, mn, vn

    blk = pl.BlockSpec((tr, c), lambda i: (i, 0))
    return pl.pallas_call(
        body, name=name, grid=(r // tr,), out_shape=[pltpu.HBM((r, c), F32)] * 4,
        in_specs=[pl.BlockSpec((ns, tr, c), lambda i: (0, i, 0)), blk, blk, blk], out_specs=[blk] * 4,
        compiler_params=pltpu.CompilerParams(dimension_semantics=("parallel",), vmem_limit_bytes=_vmem(16 * tr * c * 4)),
    )(_hbm(slots), _hbm(w), _hbm(m), _hbm(v))


def _adam_small(name, slots, ws, ms, vs, scale=None):
    k = len(slots)

    def body(*refs):
        s_refs, w_refs, m_refs, v_refs = refs[:k], refs[k:2 * k], refs[2 * k:3 * k], refs[3 * k:4 * k]
        sc_ref = refs[4 * k] if scale is not None else None
        outs = refs[4 * k + (scale is not None):]
        for a in range(k):
            g = s_refs[a][0]
            for j in range(1, NDEV):
                g = g + s_refs[a][j]
            if scale is not None and a == scale[0]:
                g = g * _dsilu(sc_ref[...])
            d, mn, vn = _adam_math(w_refs[a][...], g, m_refs[a][...], v_refs[a][...])
            outs[a][...], outs[k + a][...], outs[2 * k + a][...], outs[3 * k + a][...] = g, d, mn, vn

    shapes = [pltpu.HBM(w.shape, F32) for w in ws]
    extra = [scale[1]] if scale is not None else []
    ins = [*slots, *ws, *ms, *vs, *extra]

    def whole(shape):
        return pl.BlockSpec(shape, lambda i, nd=len(shape): (0,) * nd)

    res = pl.pallas_call(body, name=name, grid=(1,), out_shape=shapes * 4, in_specs=[whole(v.shape) for v in ins],
                         out_specs=[whole(s.shape) for s in shapes * 4])(*[_hbm(v) for v in ins])
    return res[:k], res[k:2 * k], res[2 * k:3 * k], res[3 * k:]


def _unshard_cols(g):
    g = jnp.moveaxis(g, 0, -2)
    return g.reshape(g.shape[:-2] + (g.shape[-2] * g.shape[-1],))


def _shard_cols(a):
    a = a.reshape(a.shape[:-1] + (NDEV, a.shape[-1] // NDEV))
    return jnp.moveaxis(a, -2, 0)


def _unshard_rows(g):
    g = jnp.moveaxis(g, 0, -3)
    return g.reshape(g.shape[:-3] + (g.shape[-3] * g.shape[-2], g.shape[-1]))


def _shard_rows(a):
    a = a.reshape(a.shape[:-2] + (NDEV, a.shape[-2] // NDEV, a.shape[-1]))
    return jnp.moveaxis(a, -3, 0)


def _flat2(a):
    return a.reshape((-1, a.shape[-1]))


def kernel(x, c, ctx, c_ctx, ada_w, ada_b, norm_mix_g, norm_ffn_g, final_norm_g, ssd_w_in, ssd_conv_w, ssd_conv_b, ssd_dt_bias_f, ssd_dt_bias_b, ssd_a_log_f, ssd_a_log_b, ssd_d_skip, ssd_norm_w, ssd_w_out, conf_w_pw1, conf_b_pw1, conf_dw_w, conf_dw_b, conf_ln_g, conf_ln_b, conf_w_pw2, conf_b_pw2, ffn_w_in, ffn_w_out, loss_target, m_c_ctx, m_ada_w, m_ada_b, m_norm_mix_g, m_norm_ffn_g, m_final_norm_g, m_ssd_w_in, m_ssd_conv_w, m_ssd_conv_b, m_ssd_dt_bias_f, m_ssd_dt_bias_b, m_ssd_a_log_f, m_ssd_a_log_b, m_ssd_d_skip, m_ssd_norm_w, m_ssd_w_out, m_conf_w_pw1, m_conf_b_pw1, m_conf_dw_w, m_conf_dw_b, m_conf_ln_g, m_conf_ln_b, m_conf_w_pw2, m_conf_b_pw2, m_ffn_w_in, m_ffn_w_out, v_c_ctx, v_ada_w, v_ada_b, v_norm_mix_g, v_norm_ffn_g, v_final_norm_g, v_ssd_w_in, v_ssd_conv_w, v_ssd_conv_b, v_ssd_dt_bias_f, v_ssd_dt_bias_b, v_ssd_a_log_f, v_ssd_a_log_b, v_ssd_d_skip, v_ssd_norm_w, v_ssd_w_out, v_conf_w_pw1, v_conf_b_pw1, v_conf_dw_w, v_conf_dw_b, v_conf_ln_g, v_conf_ln_b, v_conf_w_pw2, v_conf_b_pw2, v_ffn_w_in, v_ffn_w_out):
    args = dict(locals())
    names = ['c_ctx', 'ada_w', 'ada_b', 'norm_mix_g', 'norm_ffn_g', 'final_norm_g', 'ssd_w_in', 'ssd_conv_w', 'ssd_conv_b',
             'ssd_dt_bias_f', 'ssd_dt_bias_b', 'ssd_a_log_f', 'ssd_a_log_b', 'ssd_d_skip', 'ssd_norm_w', 'ssd_w_out',
             'conf_w_pw1', 'conf_b_pw1', 'conf_dw_w', 'conf_dw_b', 'conf_ln_g', 'conf_ln_b', 'conf_w_pw2', 'conf_b_pw2',
             'ffn_w_in', 'ffn_w_out']
    me = 4 * lax.axis_index("x") + 2 * lax.axis_index("y") + lax.axis_index("c")
    t, d = x.shape[1], x.shape[2]
    tc = ctx.shape[1]
    tt = tc + t
    nct = tc // ROW_TILE
    assert tc % ROW_TILE == 0 and t % ROW_TILE == 0
    h = ssd_dt_bias_f.shape[-1]
    di = ssd_norm_w.shape[-1]
    cdim = ssd_conv_b.shape[-1]
    kc = ssd_conv_w.shape[1]
    ck = conf_dw_w.shape[1]
    ch = d // 2
    rows_g = t // GRID_W
    nl = ada_w.shape[0]
    cw = ada_w.shape[2]
    x2, ctx2, tgt = x[0], ctx[0], loss_target[0]

    (c_all, w_in_g, convw_g), _ = _exchange("gather_first", [c, ssd_w_in[0].astype(WIRE), ssd_conv_w[0]])
    later = _Comm(gather=[ssd_w_out[0].astype(WIRE), ffn_w_in.astype(WIRE), ffn_w_out.astype(WIRE), conf_w_pw1[0].astype(WIRE),
                          conf_w_pw2[0].astype(WIRE), conf_b_pw1, conf_dw_w[0], conf_dw_b, conf_ln_g, conf_ln_b, conf_b_pw2])
    w_ssd_in = _unshard_cols(w_in_g)
    w_zx = w_ssd_in[:, :di + cdim]
    w_z, w_xbc = w_zx[:, :di], w_zx[:, di:]
    w_dt = jnp.pad(w_ssd_in[:, di + cdim:], ((0, 0), (0, LANES - 2 * h)))
    conv_w_full = _unshard_cols(convw_g)

    cs_all = jnp.concatenate([c_all[:, 0, :], jnp.broadcast_to(c_ctx[None, :], (NDEV, d))], axis=0)
    ada_b_mine = lax.dynamic_slice_in_dim(ada_b, me * cw, cw, axis=1)[:, None, :]
    mod_part = _ada_fwd("ada_fwd", cs_all, ada_w, ada_b_mine)
    (mod_g,), _ = _exchange("gather_mod", [mod_part])
    mod_all = jnp.moveaxis(mod_g, 0, 2).reshape(nl, 2 * NDEV, NDEV * cw)
    mod_lat = lax.dynamic_slice_in_dim(mod_all, me, 1, axis=1)[:, 0, :]
    mod_ctx = mod_all[0, NDEV, :]

    def six(v):
        return [v[k * d:(k + 1) * d][None, :] for k in range(6)]

    sh1, s1, g1, sh2, s2, g2 = six(mod_lat[0])
    csh1, cs1 = six(mod_ctx)[:2]
    sh1b, s1b, g1b, sh2b, s2b, g2b = six(mod_lat[1])
    nmg, nfg = norm_mix_g, norm_ffn_g

    h_all = jnp.concatenate([ctx2, x2], axis=0)
    s01, sh01 = jnp.concatenate([cs1, s1], axis=0), jnp.concatenate([csh1, sh1], axis=0)
    xn_all = _normmod_fwd("l0_norm", h_all, nmg[0:1], s01, sh01, nct)
    zx = _mm("ssd_in_proj", xn_all, w_zx, "nn", ACT)
    dtr = _mm("ssd_dt_proj", xn_all, w_dt, "nn")
    dt2 = jnp.moveaxis(dtr[:, :2 * h].reshape(tt, 2, h), 1, 0)
    bias2 = jnp.stack([ssd_dt_bias_f, ssd_dt_bias_b])
    alog2 = jnp.stack([ssd_a_log_f, ssd_a_log_b])
    xbc = _ssd_conv_fwd("ssd_conv", zx, conv_w_full, ssd_conv_b, di, tc)
    y2, hp2, (gat, _) = _ssd_fwd("ssd_scan", xbc, dt2, bias2, alog2, di, tc, comm=later)
    (w_out_g, fin_g, fout_g, pw1_g, pw2_g, bpw1_g, dww_g, dwb_g, lng_g, lnb_g, bpw2_g) = gat
    w_ssd_out = _unshard_rows(w_out_g)
    w_pw1, w_pw2 = _unshard_cols(pw1_g), _unshard_rows(pw2_g)
    w_fin, w_fout = _unshard_cols(fin_g), _unshard_rows(fout_g)
    dw_w_full = _unshard_cols(dww_g)
    b_pw1, dw_b, ln_g, ln_b, b_pw2 = (_unshard_cols(a) for a in (bpw1_g, dwb_g, lng_g, lnb_g, bpw2_g))
    dskip_e = jnp.repeat(ssd_d_skip, di // h, axis=1)
    yn = _ssd_gate_fwd("ssd_gate", y2, xbc, zx, dskip_e, ssd_norm_w, di, nct, t)
    mix0 = _mm("ssd_out_proj", yn, w_ssd_out, "nn")
    h1, xf0 = _resnorm_fwd("l0_res_norm", x2, mix0, g1, nfg[0:1], s2, sh2)
    u0 = _mm("ffn0_in", xf0, w_fin[0], "nn", ACT)
    hid0 = _swiglu_fwd("ffn0_act", u0)
    f0 = _mm("ffn0_out", hid0, w_fout[0], "nn")
    h2, xn1 = _resnorm_fwd("l1_norm", h1, f0, g2, nmg[1:2], s1b, sh1b)
    u1 = _mm("conf_pw1", xn1, w_pw1, "nn", ACT, bias=b_pw1)
    gl = _glu_fwd("conf_glu", u1)
    gl_h = _grid_t(gl[:, :ch], rows_g, GRID_W)
    gl_v = gl[:, ch:]
    v_h = _strided_conv("conf_conv_h", gl_h, dw_w_full[:, :ch], dw_b[:, :ch], rows_g)
    v_v = _strided_conv("conf_conv_v", gl_v, dw_w_full[:, ch:], dw_b[:, ch:], GRID_W)
    v = jnp.concatenate([_grid_t(v_h, GRID_W, rows_g), v_v], axis=1)
    sl = _ln_silu_fwd("conf_ln", v, ln_g, ln_b)
    mix1 = _mm("conf_pw2", sl, w_pw2, "nn", bias=b_pw2)
    h3, xf1 = _resnorm_fwd("l1_res_norm", h2, mix1, g1b, nfg[1:2], s2b, sh2b)
    u2 = _mm("ffn1_in", xf1, w_fin[1], "nn", ACT)
    hid1 = _swiglu_fwd("ffn1_act", u2)
    f1 = _mm("ffn1_out", hid1, w_fout[1], "nn")
    dh, sq, d_final_g = _final_loss("final_loss", h3, f1, tgt, g2b, final_norm_g[None, :])
    loss = lax.psum(0.5 * sq[0, 0] / d, AXES)

    zero2 = jnp.zeros((2, d), F32)

    def ffn_bwd(tag, dh, hin, xf, u, hid, f, gate, w_in, w_out, g_norm, s_mod):
        df, dgate, _ = _gate_bwd(tag + "_gate_bwd", dh, f, gate)
        dhid = _mm(tag + "_dhid", df, w_out, "nt", ACT)
        dw_out = _mm(tag + "_dwout", hid, df, "tn", WIRE)
        du = _swiglu_bwd(tag + "_act_bwd", u, dhid)
        dw_in = _mm(tag + "_dwin", xf, du, "tn", WIRE)
        dxf = _mm(tag + "_dx", du, w_in, "nt")
        s_2 = jnp.concatenate([s_mod, s_mod], axis=0)
        dh, dsh, ds, dg = _normmod_bwd(tag + "_norm_bwd", hin, dxf, dh, g_norm, s_2)
        return dh, dgate, dsh[1:2], ds[1:2], dg[1:2], dw_in, dw_out

    dh, d_g2b, d_sh2b, d_s2b, d_nfg1, g_fin1, g_fout1 = ffn_bwd("ffn1", dh, h3, xf1, u2, hid1, f1, g2b, w_fin[1], w_fout[1], nfg[1:2], s2b)
    dmix1, d_g1b, g_bpw2 = _gate_bwd("conf_gate_bwd", dh, mix1, g1b)
    dsl = _mm("conf_dsl", dmix1, w_pw2, "nt")
    g_pw2 = _mm("conf_dwpw2", sl, dmix1, "tn", WIRE)
    dv, g_lng, g_lnb = _ln_silu_bwd("conf_ln_bwd", v, dsl, ln_g, ln_b)
    dv_h, dv_v = _grid_t(dv[:, :ch], rows_g, GRID_W), dv[:, ch:]
    w_flip = dw_w_full[::-1]
    dgl_h = _strided_conv("conf_conv_h_bwd", dv_h, w_flip[:, :ch], None, rows_g)
    dgl_v = _strided_conv("conf_conv_v_bwd", dv_v, w_flip[:, ch:], None, GRID_W)
    g_dww_h, g_dwb_h = _strided_conv_dw("conf_conv_h_dw", gl_h, dv_h, ck, rows_g)
    g_dww_v, g_dwb_v = _strided_conv_dw("conf_conv_v_dw", gl_v, dv_v, ck, GRID_W)
    g_dww, g_dwb = jnp.concatenate([g_dww_h, g_dww_v], axis=1), jnp.concatenate([g_dwb_h, g_dwb_v], axis=1)
    dgl = jnp.concatenate([_grid_t(dgl_h, GRID_W, rows_g), dgl_v], axis=1)
    du1, g_bpw1 = _glu_bwd("conf_glu_bwd", u1, dgl)
    g_pw1 = _mm("conf_dwpw1", xn1, du1, "tn", WIRE)
    dxn1 = _mm("conf_dx", du1, w_pw1, "nt")
    dh, dsh_, ds_, dg_ = _normmod_bwd("l1_norm_bwd", h2, dxn1, dh, nmg[1:2], jnp.concatenate([s1b, s1b], axis=0))
    d_sh1b, d_s1b, d_nmg1 = dsh_[1:2], ds_[1:2], dg_[1:2]
    dh, d_g2, d_sh2, d_s2, d_nfg0, g_fin0, g_fout0 = ffn_bwd("ffn0", dh, h1, xf0, u0, hid0, f0, g2, w_fin[0], w_fout[0], nfg[0:1], s2)
    dmix0, d_g1, _ = _gate_bwd("ssd_gate_res_bwd", dh, mix0, g1)
    dyn = _mm("ssd_dyn", dmix0, w_ssd_out, "nt")
    g_ssd_out = _mm("ssd_dwout", yn, dmix0, "tn", WIRE)
    dy, dz, g_normw, ddsk_e = _ssd_gate_bwd("ssd_gate_bwd", dyn, y2, xbc, zx, dskip_e, ssd_norm_w, di, nct, tt)
    early = [_shard_rows(g_ssd_out), _shard_cols(g_pw1), _shard_rows(g_pw2),
             _flat3(_shard_cols(jnp.stack([g_fin0, g_fin1]))), _flat3(_shard_rows(jnp.stack([g_fout0, g_fout1]))),
             _shard_cols(g_bpw1), _shard_cols(g_dww), _shard_cols(g_dwb), _shard_cols(g_lng), _shard_cols(g_lnb),
             _shard_cols(g_bpw2)]
    dxbc2, ddt2, g_alog2, g_bias2, (_, early_r) = _ssd_bwd("ssd_scan_bwd", xbc, dt2, bias2, alog2, dy, hp2, dskip_e, di, tc,
                                                           comm=_Comm(scatter=early))
    dxr, g_convw, g_convb = _ssd_conv_bwd("ssd_conv_bwd", zx, dxbc2, conv_w_full, ssd_conv_b, di, tc)
    ddt_p = jnp.pad(jnp.moveaxis(ddt2, 0, 1).reshape(tt, 2 * h), ((0, 0), (0, LANES - 2 * h))).astype(MXU)
    dxn = _mm("ssd_dx_z", dz, w_z, "nt")
    dxn = _mm("ssd_dx_xbc", dxr, w_xbc, "nt", add=dxn)
    dxn = _mm("ssd_dx_dt", ddt_p, w_dt, "nt", add=dxn)
    g_ssd_in = jnp.concatenate([_mm("ssd_dw_z", xn_all, dz, "tn", WIRE), _mm("ssd_dw_xbc", xn_all, dxr, "tn", WIRE),
                                _mm("ssd_dw_dt", xn_all, ddt_p, "tn", WIRE)[:, :2 * h]], axis=1)
    dh_all, dsh_, ds_, dg_ = _normmod_bwd("l0_norm_bwd", h_all, dxn, dh, nmg[0:1], s01, nct)
    grad_x = dh_all[tc:][None]
    d_csh1, d_sh1, d_cs1, d_s1 = dsh_[0:1], dsh_[1:2], ds_[0:1], ds_[1:2]
    d_nmg0 = dg_[0:1] + dg_[1:2]

    z1 = jnp.zeros((1, d), F32)
    dmod = jnp.concatenate([jnp.concatenate([d_sh1, d_s1, d_g1, d_sh2, d_s2, d_g2], axis=1),
                            jnp.concatenate([d_sh1b, d_s1b, d_g1b, d_sh2b, d_s2b, d_g2b], axis=1),
                            jnp.concatenate([d_csh1, d_cs1, z1, z1, z1, z1], axis=1)], axis=0)
    (dmod_g,), _ = _exchange("gather_dmod", [dmod])
    dmod_mine = lax.dynamic_slice_in_dim(dmod_g, me * cw, cw, axis=2)
    dmod16 = jnp.stack([jnp.concatenate([dmod_mine[:, 0], dmod_mine[:, 2]], axis=0),
                        jnp.concatenate([dmod_mine[:, 1], jnp.zeros((NDEV, cw), F32)], axis=0)])
    g_ada_w, dsc_part = _ada_bwd("ada_bwd", cs_all, ada_w, dmod16)
    g_ada_b = dmod[0:2] + jnp.concatenate([dmod[2:3], jnp.zeros((1, 6 * d), F32)], axis=0)

    d_dskip = jnp.sum(ddsk_e.reshape(h, di // h), axis=1)[None, :]
    rep = [dsc_part, g_ada_b, jnp.concatenate([d_nmg0, d_nmg1], axis=0), jnp.concatenate([d_nfg0, d_nfg1], axis=0),
           d_final_g, g_convb, g_bias2[0], g_bias2[1], g_alog2[0], g_alog2[1], d_dskip, g_normw]
    rep_g, (ssd_in_r, convw_r) = _exchange("exchange_grads", rep, [_shard_cols(g_ssd_in), _shard_cols(g_convw)])
    big_r = [ssd_in_r] + list(early_r[:5])
    small_r = [convw_r] + list(early_r[5:])

    out = {}

    def put(name, res):
        w = args[name]
        out["grad_" + name], out["delta_" + name], out["new_m_" + name], out["new_v_" + name] = (r.reshape(w.shape) for r in res)

    for name, slots in zip(["ssd_w_in", "ssd_w_out", "conf_w_pw1", "conf_w_pw2", "ffn_w_in", "ffn_w_out"], big_r):
        put(name, _adam("adam_" + name, slots, _flat2(args[name]), _flat2(args["m_" + name]), _flat2(args["v_" + name])))
    put("ada_w", _adam("adam_ada_w", _flat2(g_ada_w)[None], _flat2(ada_w), _flat2(m_ada_w), _flat2(v_ada_w)))
    small_names = ["ssd_conv_w", "conf_b_pw1", "conf_dw_w", "conf_dw_b", "conf_ln_g", "conf_ln_b", "conf_b_pw2",
                   "c_ctx", "ada_b", "norm_mix_g", "norm_ffn_g", "final_norm_g", "ssd_conv_b", "ssd_dt_bias_f", "ssd_dt_bias_b",
                   "ssd_a_log_f", "ssd_a_log_b", "ssd_d_skip", "ssd_norm_w"]
    slots = list(small_r) + list(rep_g)

    def as2(a):
        return a.reshape((1, -1)) if a.ndim == 1 else _flat2(a)

    res = _adam_small("adam_small", slots, [as2(args[n]) for n in small_names], [as2(args["m_" + n]) for n in small_names],
                      [as2(args["v_" + n]) for n in small_names], scale=(small_names.index("c_ctx"), c_ctx[None, :]))
    for k, name in enumerate(small_names):
        put(name, [r[k] for r in res])
    return (loss, grad_x, *[out["grad_" + n] for n in names], *[out["delta_" + n] for n in names],
            *[out["new_m_" + n] for n in names], *[out["new_v_" + n] for n in names])


def _flat3(a):
    return a.reshape((a.shape[0], -1, a.shape[-1]))
```

```python
import functools

import jax
import jax.numpy as jnp
from jax import lax
from jax.experimental import pallas as pl
from jax.experimental.pallas import tpu as pltpu

F32 = jnp.float32
MXU = jnp.bfloat16
WIRE = jnp.bfloat16
ACT = jnp.bfloat16
NDEV = 8
AXES = ("x", "y", "c")
SSD_STATE = 128
SSD_CHUNK = 128
GRID_W = 64
EPS = 1e-6
ROW_TILE = 256
LANES = 128
ADAM_LR, ADAM_B1, ADAM_B2, ADAM_EPS, ADAM_WD, ADAM_STEP = 0.001, 0.9, 0.999, 1e-08, 0.01, 10
VMEM_CAP = 56 * 2 ** 20
MESH_ID = pl.DeviceIdType.MESH


def _pick(dim, cands):
    for c in cands:
        if dim % c == 0:
            return c
    return dim


def _nbytes(shape, dtype):
    n = 1
    for s in shape:
        n *= s
    return n * jnp.dtype(dtype).itemsize


def _vmem(nbytes):
    return int(min(VMEM_CAP, max(24 * 2 ** 20, 2 * nbytes + 8 * 2 ** 20)))


def _sigmoid(x):
    return 1.0 / (1.0 + jnp.exp(-x))


def _silu(x):
    return x * _sigmoid(x)


def _dsilu(x):
    s = _sigmoid(x)
    return s * (1.0 + x * (1.0 - s))


def _softplus(x):
    return jnp.maximum(x, 0.0) + jnp.log(1.0 + jnp.exp(-jnp.abs(x)))


def _dot(a, b, dims):
    return lax.dot_general(a.astype(MXU), b.astype(MXU), (dims, ((), ())), preferred_element_type=F32)


NN, NT, TN = ((1,), (0,)), ((1,), (1,)), ((0,), (0,))


def _split(a, parts):
    out = []
    for _ in range(parts):
        p = a.astype(MXU)
        out.append(p)
        a = a - p.astype(F32)
    return out


def _dot_lx(e, a, dims, parts=2):
    return sum(lax.dot_general(e, p, (dims, ((), ())), preferred_element_type=F32) for p in _split(a, parts))


def _dot_rx(a, e, dims, parts=2):
    return sum(lax.dot_general(p, e, (dims, ((), ())), preferred_element_type=F32) for p in _split(a, parts))


class _Comm:
    def __init__(self, gather=(), scatter=()):
        self.gather, self.scatter = list(gather), list(scatter)
        self.ng, self.n = len(self.gather), len(self.gather) + len(self.scatter)
        self.operands = self.gather + self.scatter
        self.specs = [pl.BlockSpec(memory_space=pl.ANY)] * self.n
        self.out_shape = ([jax.ShapeDtypeStruct((NDEV,) + a.shape, a.dtype) for a in self.gather]
                          + [jax.ShapeDtypeStruct(a.shape, a.dtype) for a in self.scatter])
        self.scratch = [pltpu.SemaphoreType.DMA((self.n, 7)), pltpu.SemaphoreType.DMA((self.n, 7)),
                        pltpu.SemaphoreType.DMA((self.n,))]

    def split(self, res):
        return res[:self.ng], res[self.ng:]

    def _copies(self, ins, outs, sems):
        send, recv, loc = sems
        ng, n = self.ng, self.n
        x, y, c = lax.axis_index("x"), lax.axis_index("y"), lax.axis_index("c")
        me, sib = (x, y, c), (x, y, 1 - c)
        chips = [(1 - x, y), (x, 1 - y), (1 - x, 1 - y)]

        def slot(p):
            return 4 * p[0] + 2 * p[1] + p[2]

        def rcopy(a, k, src, dst, to):
            return functools.partial(pltpu.make_async_remote_copy, src_ref=src, dst_ref=dst, send_sem=send.at[a, k],
                                     recv_sem=recv.at[a, k], device_id=to, device_id_type=MESH_ID)

        local = [functools.partial(pltpu.make_async_copy, ins[a] if a < ng else ins[a].at[slot(me)], outs[a].at[slot(me)],
                                   loc.at[a]) for a in range(n)]
        rel = [(fx, fy, fc) for fx in (0, 1) for fy in (0, 1) for fc in (0, 1)][1:]
        first, landed, passed = [], [], []
        for a in range(ng, n):
            for k, (fx, fy, fc) in enumerate(rel):
                p = (1 - x if fx else x, 1 - y if fy else y, 1 - c if fc else c)
                first.append(rcopy(a, k, ins[a].at[slot(p)], outs[a].at[slot(me)], p))
                blk = outs[a].at[slot(p)]
                landed.append(rcopy(a, k, blk, blk, me))
        for a in range(ng):
            dst = outs[a].at[slot(me)]
            first.append(rcopy(a, 0, ins[a], dst, sib))
            first += [rcopy(a, 1 + j, ins[a], dst, (*ch, c)) for j, ch in enumerate(chips)]
            blk = outs[a].at[slot(sib)]
            landed.append(rcopy(a, 0, blk, blk, me))
            for j, ch in enumerate(chips):
                blk = outs[a].at[slot((*ch, c))]
                passed.append((rcopy(a, 1 + j, blk, blk, me), rcopy(a, 4 + j, blk, blk, sib)))
                blk = outs[a].at[slot((*ch, 1 - c))]
                landed.append(rcopy(a, 4 + j, blk, blk, me))
        return local, first, passed, landed

    def start(self, ins, outs, sems):
        local, first, _, _ = self._copies(ins, outs, sems)
        for make in local + first:
            make().start()

    def finish(self, ins, outs, sems):
        local, first, passed, landed = self._copies(ins, outs, sems)
        onward = []
        for arrived, forward in passed:
            arrived().wait_recv()
            onward.append(forward())
            onward[-1].start()
        for make in landed:
            make().wait_recv()
        for make in first:
            make().wait_send()
        for cp in onward:
            cp.wait_send()
        for make in local:
            make().wait()


def _carry(body, comm, n_in, n_out, grid):
    if comm is None:
        return body
    n = comm.n

    def wrapped(*refs):
        own_in, c_in = refs[:n_in], refs[n_in:n_in + n]
        own_out, c_out = refs[n_in + n:n_in + n + n_out], refs[n_in + n + n_out:n_in + 2 * n + n_out]
        own_scr, sems = refs[n_in + 2 * n + n_out:-3], refs[-3:]
        ids = [pl.program_id(ax) for ax in range(len(grid))]
        first, last = ids[0] == 0, ids[0] == grid[0] - 1
        for ax in range(1, len(grid)):
            first, last = first & (ids[ax] == 0), last & (ids[ax] == grid[ax] - 1)

        @pl.when(first)
        def _():
            comm.start(c_in, c_out, sems)

        body(*own_in, *own_out, *own_scr)

        @pl.when(last)
        def _():
            comm.finish(c_in, c_out, sems)

    return wrapped


def _exchange(name, gather, scatter=()):
    comm = _Comm(gather, scatter)
    n = comm.n

    def body(*refs):
        ins, outs, sems = refs[:n], refs[n:2 * n], refs[2 * n:]
        comm.start(ins, outs, sems)
        comm.finish(ins, outs, sems)

    res = pl.pallas_call(body, name=name, out_shape=comm.out_shape, in_specs=comm.specs, out_specs=comm.specs,
                         scratch_shapes=comm.scratch)(*comm.operands)
    return comm.split(res)


def _hbm(a):
    return pltpu.with_memory_space_constraint(a, pltpu.HBM)


def _divs(dim, mult):
    return [dim] + [dim // parts for parts in range(2, dim // mult + 1) if dim % parts == 0 and (dim // parts) % mult == 0]


MM_VMEM_BUDGET = 40 * 2 ** 20
GRID_STEP_US = 0.35
HBM_BYTES_PER_US = 3.0e6


def _mm_tiles(m, n, k, sizes, mode, has_add):
    sa, sb, so = sizes
    sub = 16
    best = None
    for tk in _divs(k, LANES):
        for tn in _divs(n, LANES):
            for tm in _divs(m, LANES if mode == "tn" else sub):
                nk = k // tk
                out_t = tm * tn
                est = (2 * (tm * tk * sa + tk * tn * sb) + 2 * out_t * so + 2 * (tm * tk + tk * tn) + 4 * out_t
                       + (4 * out_t if nk > 1 else 0) + (8 * out_t if has_add else 0))
                if est > MM_VMEM_BUDGET:
                    continue
                steps = (m // tm) * (n // tn) * nk
                cost = steps * GRID_STEP_US + (tm * tk * sa + tk * tn * sb + out_t * so) / HBM_BYTES_PER_US
                if best is None or cost < best[0]:
                    best = (cost, tm, tn, tk, est)
    assert best is not None, (m, n, k)
    return best[1:]


def _mm(name, a, b, mode, out_dtype=F32, bias=None, add=None, comm=None):
    if mode == "nn":
        (m, k), (k2, n) = a.shape, b.shape
    elif mode == "nt":
        (m, k), (n, k2) = a.shape, b.shape
    else:
        (k, m), (k2, n) = a.shape, b.shape
    assert k == k2, (name, a.shape, b.shape)
    sizes = (a.dtype.itemsize, b.dtype.itemsize, jnp.dtype(out_dtype).itemsize)
    tm, tn, tk, est = _mm_tiles(m, n, k, sizes, mode, add is not None)
    nk = k // tk
    dims = {"nn": NN, "nt": NT, "tn": TN}[mode]
    a_spec = pl.BlockSpec((tk, tm), lambda i, j, kk: (kk, i)) if mode == "tn" else pl.BlockSpec((tm, tk), lambda i, j, kk: (i, kk))
    b_spec = pl.BlockSpec((tn, tk), lambda i, j, kk: (j, kk)) if mode == "nt" else pl.BlockSpec((tk, tn), lambda i, j, kk: (kk, j))
    extra, extra_specs = [], []
    if bias is not None:
        extra.append(bias)
        extra_specs.append(pl.BlockSpec((1, tn), lambda i, j, kk: (0, j)))
    if add is not None:
        extra.append(add)
        extra_specs.append(pl.BlockSpec((tm, tn), lambda i, j, kk: (i, j)))

    def finish(r, extras, o_ref):
        for e in extras:
            r = r + e[...].astype(F32)
        o_ref[...] = r.astype(o_ref.dtype)

    def body_acc(*refs):
        a_ref, b_ref = refs[:2]
        o_ref, acc = refs[-2:]
        kk = pl.program_id(2)

        @pl.when(kk == 0)
        def _():
            acc[...] = jnp.zeros_like(acc)

        acc[...] += _dot(a_ref[...], b_ref[...], dims)

        @pl.when(kk == nk - 1)
        def _():
            finish(acc[...], refs[2:-2], o_ref)

    def body_one(*refs):
        finish(_dot(refs[0][...], refs[1][...], dims), refs[2:-1], refs[-1])

    cm = comm if comm is not None else _Comm()
    grid = (m // tm, n // tn, nk)
    res = pl.pallas_call(
        _carry(body_acc if nk > 1 else body_one, comm, 2 + len(extra), 1, grid), name=name, grid=grid,
        out_shape=[pltpu.HBM((m, n), out_dtype)] + cm.out_shape,
        in_specs=[a_spec, b_spec] + extra_specs + cm.specs,
        out_specs=[pl.BlockSpec((tm, tn), lambda i, j, kk: (i, j))] + cm.specs,
        scratch_shapes=([pltpu.VMEM((tm, tn), F32)] if nk > 1 else []) + (cm.scratch if comm is not None else []),
        compiler_params=pltpu.CompilerParams(
            dimension_semantics=("parallel", "parallel", "arbitrary") if comm is None else ("arbitrary",) * 3,
            vmem_limit_bytes=int(min(VMEM_CAP, est + 12 * 2 ** 20))),
    )(*[_hbm(v) for v in (a, b, *extra)], *cm.operands)
    return res[0] if comm is None else (res[0], cm.split(res[1:]))


def _ri(arr, w=None, cb=0, ro=0, lead=None):
    return (arr, arr.shape[-1] if w is None else w, cb, ro, lead)


def _rowwise(name, fn, nrows, row_ins, bc_ins, outs, accs=()):
    tr = min(ROW_TILE, nrows)
    assert nrows % tr == 0
    in_specs = []
    for (arr, w, cb, ro, lead) in row_ins:
        if lead is None:
            in_specs.append(pl.BlockSpec((tr, w), lambda i, cb=cb, ro=ro: (jnp.maximum(i + ro, 0), cb)))
        else:
            in_specs.append(pl.BlockSpec((None, tr, w), lambda i, cb=cb, ro=ro, lead=lead: (lead, jnp.maximum(i + ro, 0), cb)))
    for arr in bc_ins:
        in_specs.append(pl.BlockSpec(arr.shape, lambda i, nd=arr.ndim: (0,) * nd))
    outs = [o if len(o) == 4 else (o[0], o[1], o[0], 0) for o in outs]
    out_shape = [pltpu.HBM((nrows, total), dt) for _, dt, total, _ in outs] + [pltpu.HBM(s, F32) for s in accs]
    out_specs = ([pl.BlockSpec((tr, c), lambda i, cb=cb: (i, cb)) for c, _, _, cb in outs]
                 + [pl.BlockSpec(s, lambda i: (0, 0)) for s in accs])
    nr, nb, no = len(row_ins), len(bc_ins), len(outs)

    def body(*refs):
        i = pl.program_id(0)
        rows = [r[...].astype(F32) for r in refs[:nr]]
        bcs = [r[...] for r in refs[nr:nr + nb]]
        o, a = fn(rows, bcs, i)
        for ref, val in zip(refs[nr + nb:nr + nb + no], o):
            ref[...] = val.astype(ref.dtype)
        for ref, val in zip(refs[nr + nb + no:], a):
            @pl.when(i == 0)
            def _(ref=ref, val=val):
                ref[...] = val

            @pl.when(i > 0)
            def _(ref=ref, val=val):
                ref[...] += val

    est = sum(tr * w * arr.dtype.itemsize for (arr, w, _, _, _) in row_ins) + sum(tr * o[0] * 4 for o in outs)
    res = pl.pallas_call(
        body, name=name, grid=(nrows // tr,), out_shape=out_shape, in_specs=in_specs, out_specs=out_specs,
        compiler_params=pltpu.CompilerParams(dimension_semantics=("arbitrary",), vmem_limit_bytes=_vmem(3 * est)),
    )(*[_hbm(r[0]) for r in row_ins], *[_hbm(v) for v in bc_ins])
    return res[:no], res[no:]


def _colsum(v):
    return jnp.sum(v, axis=0, keepdims=True)


def _normmod_fwd(name, h, g, s, sh, nct=0):
    d = h.shape[1]

    def fn(rows, bcs, i):
        hh, (g_, s_, sh_) = rows[0], bcs
        s1 = jnp.where(i < nct, s_[0:1], s_[1:2])
        sh1 = jnp.where(i < nct, sh_[0:1], sh_[1:2])
        r = lax.rsqrt(jnp.mean(hh * hh, axis=-1, keepdims=True) + EPS)
        return [hh * r * g_ * (1.0 + s1) + sh1], []

    return _rowwise(name, fn, h.shape[0], [_ri(h)], [g, s, sh], [(d, MXU)])[0][0]


def _normmod_bwd(name, h, dxn, dres, g, s, nct=0):
    d = h.shape[1]

    def fn(rows, bcs, i):
        hh, dx, dr = rows
        g_, s_ = bcs
        ctx = i < nct
        s1 = jnp.where(ctx, s_[0:1], s_[1:2])
        r = lax.rsqrt(jnp.mean(hh * hh, axis=-1, keepdims=True) + EPS)
        hr = hh * r
        dy = dx * (1.0 + s1)
        u = dy * g_
        dh = r * u - hr * (r * r) * jnp.mean(u * hh, axis=-1, keepdims=True)
        dh = dh + jnp.where(ctx, 0.0, dr)

        def seg(v):
            v = _colsum(v)
            return jnp.concatenate([jnp.where(ctx, v, 0.0), jnp.where(ctx, 0.0, v)], axis=0)

        return [dh], [seg(dx), seg(dx * hr * g_), seg(dy * hr)]

    (dh,), (dsh, ds, dg) = _rowwise(name, fn, h.shape[0], [_ri(h), _ri(dxn), _ri(dres, ro=-nct)], [g, s],
                                    [(d, F32)], [(2, d)] * 3)
    return dh, dsh, ds, dg


def _resnorm_fwd(name, h, y, gate, g, s, sh):
    d = h.shape[1]

    def fn(rows, bcs, i):
        hh, yy = rows
        gate_, g_, s_, sh_ = bcs
        hn = hh + gate_ * yy
        r = lax.rsqrt(jnp.mean(hn * hn, axis=-1, keepdims=True) + EPS)
        return [hn, hn * r * g_ * (1.0 + s_) + sh_], []

    return _rowwise(name, fn, h.shape[0], [_ri(h), _ri(y)], [gate, g, s, sh], [(d, F32), (d, MXU)])[0]


def _gate_bwd(name, dh, y, gate):
    d = dh.shape[1]

    def fn(rows, bcs, i):
        dd, yy = rows
        dy = dd * bcs[0]
        return [dy], [_colsum(dd * yy), _colsum(dy)]

    (dy,), (dgate, dbias) = _rowwise(name, fn, dh.shape[0], [_ri(dh), _ri(y)], [gate], [(d, MXU)], [(1, d)] * 2)
    return dy, dgate, dbias


def _swiglu_fwd(name, u):
    f = u.shape[1] // 2

    def fn(rows, bcs, i):
        return [_silu(rows[0]) * rows[1]], []

    return _rowwise(name, fn, u.shape[0], [_ri(u, f, 0), _ri(u, f, 1)], [], [(f, MXU)])[0][0]


def _swiglu_bwd(name, u, dhid):
    f = u.shape[1] // 2

    def fn(rows, bcs, i):
        a, b, dd = rows
        return [jnp.concatenate([dd * b * _dsilu(a), dd * _silu(a)], axis=1)], []

    return _rowwise(name, fn, u.shape[0], [_ri(u, f, 0), _ri(u, f, 1), _ri(dhid)], [], [(2 * f, MXU)])[0][0]


def _glu_fwd(name, u):
    d = u.shape[1] // 2

    def fn(rows, bcs, i):
        return [rows[0] * _sigmoid(rows[1])], []

    return _rowwise(name, fn, u.shape[0], [_ri(u, d, 0), _ri(u, d, 1)], [], [(d, F32)])[0][0]


def _glu_bwd(name, u, dgl):
    d = u.shape[1] // 2

    def fn(rows, bcs, i):
        a, b, dd = rows
        sg = _sigmoid(b)
        du = jnp.concatenate([dd * sg, dd * a * sg * (1.0 - sg)], axis=1)
        return [du], [_colsum(du)]

    (du,), (db,) = _rowwise(name, fn, u.shape[0], [_ri(u, d, 0), _ri(u, d, 1), _ri(dgl)], [], [(2 * d, MXU)], [(1, 2 * d)])
    return du, db


def _ln_silu_fwd(name, v, g, b):
    d = v.shape[1]

    def fn(rows, bcs, i):
        vv = rows[0]
        mu = jnp.mean(vv, axis=-1, keepdims=True)
        xc = vv - mu
        rs = lax.rsqrt(jnp.mean(xc * xc, axis=-1, keepdims=True) + EPS)
        return [_silu(xc * rs * bcs[0] + bcs[1])], []

    return _rowwise(name, fn, v.shape[0], [_ri(v)], [g, b], [(d, MXU)])[0][0]


def _ln_silu_bwd(name, v, ds, g, b):
    d = v.shape[1]

    def fn(rows, bcs, i):
        vv, dd = rows
        mu = jnp.mean(vv, axis=-1, keepdims=True)
        xc = vv - mu
        rs = lax.rsqrt(jnp.mean(xc * xc, axis=-1, keepdims=True) + EPS)
        xh = xc * rs
        dln = dd * _dsilu(xh * bcs[0] + bcs[1])
        dxh = dln * bcs[0]
        dv = rs * (dxh - jnp.mean(dxh, axis=-1, keepdims=True) - xh * jnp.mean(dxh * xh, axis=-1, keepdims=True))
        return [dv], [_colsum(dln * xh), _colsum(dln)]

    (dv,), (dg, db) = _rowwise(name, fn, v.shape[0], [_ri(v), _ri(ds)], [g, b], [(d, F32)], [(1, d)] * 2)
    return dv, dg, db


def _final_loss(name, h, f, target, gate, gf):
    d = h.shape[1]

    def fn(rows, bcs, i):
        hh, ff, tg = rows
        gate_, g_ = bcs
        hn = hh + gate_ * ff
        r = lax.rsqrt(jnp.mean(hn * hn, axis=-1, keepdims=True) + EPS)
        hr = hn * r
        err = hr * g_ - tg
        dout = err * (1.0 / d)
        u = dout * g_
        dh = r * u - hr * (r * r) * jnp.mean(u * hn, axis=-1, keepdims=True)
        sq = jnp.sum(_colsum(err * err), axis=1, keepdims=True)
        return [dh], [jnp.broadcast_to(sq, (1, LANES)), _colsum(dout * hr)]

    (dh,), (sq, dgf) = _rowwise(name, fn, h.shape[0], [_ri(h), _ri(f), _ri(target)], [gate, gf], [(d, F32)], [(1, LANES), (1, d)])
    return dh, sq, dgf


def _shift_rows(x, o, seg_lo, seg_hi, row):
    n = x.shape[0]
    if o == 0:
        return x
    sh = pltpu.roll(x, (-o) % n, 0)
    ok = (row + o >= seg_lo) & (row + o < seg_hi)
    return jnp.where(ok, sh, 0.0)


def _seg_bounds(row, tc, tt):
    ctx = row < tc
    return jnp.where(ctx, 0, tc), jnp.where(ctx, tc, tt)


def _ssd_conv_fwd(name, zx, w, b, di, tc):
    tt, kc, cd = zx.shape[0], w.shape[0], w.shape[1]
    cb = _pick(cd, (LANES,))
    off = di // cb

    def body(x_ref, w_ref, b_ref, o_ref):
        x = x_ref[...].astype(F32)
        row = lax.broadcasted_iota(jnp.int32, (tt, 1), 0)
        lo, hi = _seg_bounds(row, tc, tt)
        acc = jnp.broadcast_to(b_ref[...], x.shape)
        for k in range(kc):
            acc = acc + w_ref[k:k + 1, :] * _shift_rows(x, k - kc // 2, lo, hi, row)
        o_ref[...] = _silu(acc).astype(o_ref.dtype)

    return pl.pallas_call(
        body, name=name, grid=(cd // cb,), out_shape=pltpu.HBM((tt, cd), ACT),
        in_specs=[pl.BlockSpec((tt, cb), lambda j: (0, j + off)), pl.BlockSpec((kc, cb), lambda j: (0, j)),
                  pl.BlockSpec((1, cb), lambda j: (0, j))],
        out_specs=pl.BlockSpec((tt, cb), lambda j: (0, j)),
        compiler_params=pltpu.CompilerParams(dimension_semantics=("parallel",), vmem_limit_bytes=_vmem(4 * tt * cb * 4)),
    )(_hbm(zx), _hbm(w), _hbm(b))


def _ssd_conv_bwd(name, zx, dact2, w, b, dzx, di, tc):
    tt, kc, cd = zx.shape[0], w.shape[0], w.shape[1]
    cb = _pick(cd, (LANES,))
    off = di // cb

    def body(x_ref, d0_ref, d1_ref, w_ref, b_ref, _, dx_ref, dw_ref, db_ref):
        x = x_ref[...].astype(F32)
        row = lax.broadcasted_iota(jnp.int32, (tt, 1), 0)
        lo, hi = _seg_bounds(row, tc, tt)
        pre = jnp.broadcast_to(b_ref[...], x.shape)
        for k in range(kc):
            pre = pre + w_ref[k:k + 1, :] * _shift_rows(x, k - kc // 2, lo, hi, row)
        dpre = (d0_ref[...].astype(F32) + d1_ref[...].astype(F32)) * _dsilu(pre)
        dx = jnp.zeros_like(x)
        for k in range(kc):
            o = k - kc // 2
            dx = dx + w_ref[k:k + 1, :] * _shift_rows(dpre, -o, lo, hi, row)
            dw_ref[k:k + 1, :] = _colsum(dpre * _shift_rows(x, o, lo, hi, row))
        dx_ref[...] = dx.astype(dx_ref.dtype)
        db_ref[...] = _colsum(dpre)

    return pl.pallas_call(
        body, name=name, grid=(cd // cb,),
        out_shape=[pltpu.HBM(dzx.shape, dzx.dtype), pltpu.HBM((kc, cd), F32), pltpu.HBM((1, cd), F32)],
        in_specs=[pl.BlockSpec((tt, cb), lambda j: (0, j + off)), pl.BlockSpec((None, tt, cb), lambda j: (0, 0, j)),
                  pl.BlockSpec((None, tt, cb), lambda j: (1, 0, j)), pl.BlockSpec((kc, cb), lambda j: (0, j)),
                  pl.BlockSpec((1, cb), lambda j: (0, j)), pl.BlockSpec(memory_space=pl.ANY)],
        out_specs=[pl.BlockSpec((tt, cb), lambda j: (0, j + off)), pl.BlockSpec((kc, cb), lambda j: (0, j)),
                   pl.BlockSpec((1, cb), lambda j: (0, j))],
        input_output_aliases={5: 0},
        compiler_params=pltpu.CompilerParams(dimension_semantics=("parallel",), vmem_limit_bytes=_vmem(8 * tt * cb * 4)),
    )(_hbm(zx), _hbm(dact2), _hbm(dact2), _hbm(w), _hbm(b), _hbm(dzx))


def _strided_conv(name, x, w, b, stride):
    t, ch = x.shape
    kk = w.shape[0]
    pad = (kk // 2) * stride
    cb = _pick(ch, (LANES,))
    has_b = b is not None

    def body(*refs):
        x_ref, w_ref = refs[:2]
        o_ref, xp = refs[-2:]
        xp[0:pad, :] = jnp.zeros((pad, cb), F32)
        xp[pad + t:, :] = jnp.zeros((pad, cb), F32)
        xp[pad:pad + t, :] = x_ref[...]
        acc = jnp.broadcast_to(refs[2][...], (t, cb)) if has_b else jnp.zeros((t, cb), F32)
        for k in range(kk):
            acc = acc + w_ref[k:k + 1, :] * xp[k * stride:k * stride + t, :]
        o_ref[...] = acc

    ins, specs = [x, w], [pl.BlockSpec((t, cb), lambda j: (0, j)), pl.BlockSpec((kk, cb), lambda j: (0, j))]
    if has_b:
        ins.append(b)
        specs.append(pl.BlockSpec((1, cb), lambda j: (0, j)))
    return pl.pallas_call(
        body, name=name, grid=(ch // cb,), out_shape=pltpu.HBM((t, ch), F32), in_specs=specs,
        out_specs=pl.BlockSpec((t, cb), lambda j: (0, j)), scratch_shapes=[pltpu.VMEM((t + 2 * pad, cb), F32)],
        compiler_params=pltpu.CompilerParams(dimension_semantics=("parallel",), vmem_limit_bytes=_vmem(6 * t * cb * 4)),
    )(*[_hbm(v) for v in ins])


def _strided_conv_dw(name, x, dv, kk, stride):
    t, ch = x.shape
    pad = (kk // 2) * stride
    cb = _pick(ch, (LANES,))

    def body(x_ref, d_ref, dw_ref, db_ref, xp):
        xp[0:pad, :] = jnp.zeros((pad, cb), F32)
        xp[pad + t:, :] = jnp.zeros((pad, cb), F32)
        xp[pad:pad + t, :] = x_ref[...]
        d = d_ref[...]
        for k in range(kk):
            dw_ref[k:k + 1, :] = _colsum(d * xp[k * stride:k * stride + t, :])
        db_ref[...] = _colsum(d)

    blk = pl.BlockSpec((t, cb), lambda j: (0, j))
    return pl.pallas_call(
        body, name=name, grid=(ch // cb,), out_shape=[pltpu.HBM((kk, ch), F32), pltpu.HBM((1, ch), F32)],
        in_specs=[blk, blk], out_specs=[pl.BlockSpec((kk, cb), lambda j: (0, j)), pl.BlockSpec((1, cb), lambda j: (0, j))],
        scratch_shapes=[pltpu.VMEM((t + 2 * pad, cb), F32)],
        compiler_params=pltpu.CompilerParams(dimension_semantics=("parallel",), vmem_limit_bytes=_vmem(6 * t * cb * 4)),
    )(_hbm(x), _hbm(dv))


def _grid_t(a, n1, n2):
    return a.reshape(n1, n2, a.shape[-1]).swapaxes(0, 1).reshape(n1 * n2, a.shape[-1])


def _chunk_order(d, i, ncc, nc):
    back = jnp.where(i < ncc, ncc - 1 - i, nc - 1 - (i - ncc))
    return jnp.where(d == 0, i, back)


def _ssd_chunk_setup(d, dt_raw, bias, a_log, q, h, di):
    p = di // h
    dt = _softplus(dt_raw + bias)
    a_neg = -jnp.exp(a_log)
    delta = dt * a_neg
    r = lax.broadcasted_iota(jnp.int32, (q, q), 0)
    c = lax.broadcasted_iota(jnp.int32, (q, q), 1)
    sgn = 1 - 2 * d
    mask = (r - c) * sgn >= 0
    mask_t = (c - r) * sgn >= 0
    a = _dot_lx(mask.astype(MXU), delta, NN, parts=3)
    tot = _colsum(delta)
    ea, dte, cd = jnp.exp(a), jnp.exp(tot - a), jnp.exp(tot)
    hh = lax.broadcasted_iota(jnp.int32, (h, di), 0)
    cc = lax.broadcasted_iota(jnp.int32, (h, di), 1)
    e = (cc // p == hh).astype(MXU)
    ex = _dot_rx(jnp.concatenate([dt, ea, dte, jnp.broadcast_to(cd, (8, h))], axis=0), e, NN)
    eye = (lax.broadcasted_iota(jnp.int32, (h, h), 0) == lax.broadcasted_iota(jnp.int32, (h, h), 1)).astype(MXU)
    a_t = _dot_lx(eye, a, NT, parts=3)
    return dict(dt=dt, a_neg=a_neg, a=a, a_t=a_t, mask=mask, mask_t=mask_t, e=e,
                dt_e=ex[0:q], ea_e=ex[q:2 * q], dte_e=ex[2 * q:3 * q], cd_e=ex[3 * q:3 * q + 1])


def _pick_heads(r, q, hpg, p):
    lane = lax.broadcasted_iota(jnp.int32, (q, hpg * p), 1) // p
    out = jnp.zeros((q, hpg * p), F32)
    for j in range(hpg):
        out = out + jnp.where(lane == j, r[j * q:(j + 1) * q], 0.0)
    return out


def _ssd_fwd(name, xbc, dt2, bias2, alog2, di, tc, comm=None):
    tt, cd = xbc.shape
    h = dt2.shape[-1]
    q, n = SSD_CHUNK, SSD_STATE
    gn = (cd - di) // 2
    g = gn // n
    hpg, p = h // g, di // h
    gp = hpg * p
    nc, ncc = tt // q, tc // q
    assert di % gn == 0

    def body(x_ref, b_ref, c_ref, dt_ref, bias_ref, alog_ref, y_ref, hp_ref, ht):
        d, i = pl.program_id(0), pl.program_id(1)

        @pl.when(i == 0)
        def _():
            ht[...] = jnp.zeros_like(ht)

        s = _ssd_chunk_setup(d, dt_ref[...], bias_ref[...], alog_ref[...], q, h, di)
        xd = x_ref[...].astype(F32) * s["dt_e"]
        hp_ref[...] = ht[...].astype(hp_ref.dtype)
        for gi in range(g):
            bg, cg = b_ref[:, gi * n:(gi + 1) * n].astype(MXU), c_ref[:, gi * n:(gi + 1) * n].astype(MXU)
            sl = slice(gi * gp, (gi + 1) * gp)
            sc = _dot(cg, bg, NT)
            ms = []
            for j in range(hpg):
                hd = gi * hpg + j
                seg = s["a"][:, hd:hd + 1] - s["a_t"][hd:hd + 1, :]
                ms.append(sc * jnp.exp(jnp.where(s["mask"], seg, -jnp.inf)))
            xdg = xd[:, sl]
            ydiag = _pick_heads(_dot(jnp.concatenate(ms, axis=0), xdg, NN), q, hpg, p)
            htg = ht[:, sl]
            y_ref[:, sl] = ydiag + _dot(cg, htg, NN) * s["ea_e"][:, sl]
            ht[:, sl] = s["cd_e"][:, sl] * htg + _dot(bg, xdg * s["dte_e"][:, sl], TN)

    def cidx(d, i):
        return _chunk_order(d, i, ncc, nc)

    cm = comm if comm is not None else _Comm()
    res = pl.pallas_call(
        _carry(body, comm, 6, 2, (2, nc)), name=name, grid=(2, nc),
        out_shape=[pltpu.HBM((2, tt, di), F32), pltpu.HBM((2, nc, n, di), ACT)] + cm.out_shape,
        in_specs=[pl.BlockSpec((q, di), lambda d, i: (cidx(d, i), 0)),
                  pl.BlockSpec((q, gn), lambda d, i: (cidx(d, i), di // gn)),
                  pl.BlockSpec((q, gn), lambda d, i: (cidx(d, i), di // gn + 1)),
                  pl.BlockSpec((None, q, h), lambda d, i: (d, cidx(d, i), 0)),
                  pl.BlockSpec((None, 1, h), lambda d, i: (d, 0, 0)),
                  pl.BlockSpec((None, 1, h), lambda d, i: (d, 0, 0))] + cm.specs,
        out_specs=[pl.BlockSpec((None, q, di), lambda d, i: (d, cidx(d, i), 0)),
                   pl.BlockSpec((None, None, n, di), lambda d, i: (d, cidx(d, i), 0, 0))] + cm.specs,
        scratch_shapes=[pltpu.VMEM((n, di), F32)] + (cm.scratch if comm is not None else []),
        compiler_params=pltpu.CompilerParams(dimension_semantics=("arbitrary", "arbitrary"), vmem_limit_bytes=_vmem(16 * q * di * 4)),
    )(*[_hbm(v) for v in (xbc, xbc, xbc, dt2, bias2, alog2)], *cm.operands)
    return res[0], res[1], cm.split(res[2:])


def _ssd_bwd(name, xbc, dt2, bias2, alog2, dy, hp2, dskip_e, di, tc, comm=None):
    tt, cd = xbc.shape
    h = dt2.shape[-1]
    q, n = SSD_CHUNK, SSD_STATE
    gn = (cd - di) // 2
    g = gn // n
    hpg, p = h // g, di // h
    gp = hpg * p
    nc, ncc = tt // q, tc // q

    def body(x_ref, b_ref, c_ref, dt_ref, bias_ref, alog_ref, dy_ref, hp_ref, dsk_ref,
             dxbc_ref, ddt_ref, dalog_ref, dbias_ref, dht, dxd, off):
        d, i = pl.program_id(0), pl.program_id(1)

        @pl.when(i == 0)
        def _():
            dht[...] = jnp.zeros_like(dht)
            dalog_ref[...] = jnp.zeros_like(dalog_ref)
            dbias_ref[...] = jnp.zeros_like(dbias_ref)

        s = _ssd_chunk_setup(d, dt_ref[...], bias_ref[...], alog_ref[...], q, h, di)
        x, dyc = x_ref[...].astype(F32), dy_ref[...]
        xd = x * s["dt_e"]
        dyea = dyc * s["ea_e"]
        xdte = xd * s["dte_e"]
        lane = lax.broadcasted_iota(jnp.int32, (q, gp), 1) // p
        lane_h = lax.broadcasted_iota(jnp.int32, (q, h), 1)
        da_d = jnp.zeros((q, h), F32)
        last_e = []
        for gi in range(g):
            bg, cg = b_ref[:, gi * n:(gi + 1) * n].astype(MXU), c_ref[:, gi * n:(gi + 1) * n].astype(MXU)
            sl = slice(gi * gp, (gi + 1) * gp)
            sc, sct = _dot(cg, bg, NT), _dot(bg, cg, NT)
            dyg, xdg = dyc[:, sl], xd[:, sl]
            htg, dhtg = hp_ref[:, sl].astype(F32), dht[:, sl]
            dystack = jnp.concatenate([jnp.where(lane == j, dyg, 0.0) for j in range(hpg)], axis=0)
            xdstack = jnp.concatenate([jnp.where(lane == j, xdg, 0.0) for j in range(hpg)], axis=0)
            gs = _dot(dystack, xdg, NT)
            gst = _dot(xdstack, dyg, NT)
            ds = jnp.zeros((q, q), F32)
            mts = []
            for j in range(hpg):
                hd = gi * hpg + j
                col, rw = s["a"][:, hd:hd + 1], s["a_t"][hd:hd + 1, :]
                gl = gs[j * q:(j + 1) * q] * jnp.exp(jnp.where(s["mask"], col - rw, -jnp.inf))
                ds = ds + gl
                mt = sct * jnp.exp(jnp.where(s["mask_t"], rw - col, -jnp.inf))
                mts.append(mt)
                da_j = jnp.sum(gl * sc, axis=1, keepdims=True) - jnp.sum(gst[j * q:(j + 1) * q] * mt, axis=1, keepdims=True)
                da_d = da_d + jnp.where(lane_h == hd, da_j, 0.0)
            dxd_diag = _pick_heads(_dot(jnp.concatenate(mts, axis=0), dyg, NN), q, hpg, p)
            z = _dot(bg, dhtg, NN) * s["dte_e"][:, sl]
            yoff = _dot(cg, htg, NN) * s["ea_e"][:, sl]
            off[:, sl] = dyg * yoff - xdg * z
            dxd[:, sl] = dxd_diag + z
            dxbc_ref[:, di + gi * n:di + (gi + 1) * n] = (_dot(ds, cg, TN) + _dot(xdte[:, sl], dhtg, NT)).astype(dxbc_ref.dtype)
            dxbc_ref[:, di + gn + gi * n:di + gn + (gi + 1) * n] = (_dot(ds, bg, NN)
                                                                    + _dot(dyea[:, sl], htg, NT)).astype(dxbc_ref.dtype)
            last_e.append(s["cd_e"][:, sl] * _colsum(dhtg * htg) + _colsum(xdg * z))
            dht[:, sl] = s["cd_e"][:, sl] * dhtg + _dot(cg, dyea[:, sl], TN)
        dxd_all = dxd[...]
        last = jnp.concatenate(last_e, axis=1)
        da = da_d + _dot_rx(off[...], s["e"], NT)
        last_h = _dot_rx(jnp.broadcast_to(last, (8, di)), s["e"], NT)[0:1]
        ddelta = _dot_lx(s["mask_t"].astype(MXU), da, NN, parts=3) + last_h
        ddt = ddelta * s["a_neg"] + _dot_rx(dxd_all * x, s["e"], NT)
        ddt_raw = ddt * _sigmoid(dt_ref[...] + bias_ref[...])
        ddt_ref[...] = ddt_raw
        dalog_ref[...] += _colsum(ddelta * s["dt"]) * s["a_neg"]
        dbias_ref[...] += _colsum(ddt_raw)
        dxbc_ref[:, 0:di] = (dxd_all * s["dt_e"] + jnp.where(d == 0, dyc * dsk_ref[...], 0.0)).astype(dxbc_ref.dtype)

    def cidx(d, i):
        return _chunk_order(d, nc - 1 - i, ncc, nc)

    cm = comm if comm is not None else _Comm()
    res = pl.pallas_call(
        _carry(body, comm, 9, 4, (2, nc)), name=name, grid=(2, nc),
        out_shape=[pltpu.HBM((2, tt, cd), ACT), pltpu.HBM((2, tt, h), F32),
                   pltpu.HBM((2, 1, h), F32), pltpu.HBM((2, 1, h), F32)] + cm.out_shape,
        in_specs=[pl.BlockSpec((q, di), lambda d, i: (cidx(d, i), 0)),
                  pl.BlockSpec((q, gn), lambda d, i: (cidx(d, i), di // gn)),
                  pl.BlockSpec((q, gn), lambda d, i: (cidx(d, i), di // gn + 1)),
                  pl.BlockSpec((None, q, h), lambda d, i: (d, cidx(d, i), 0)),
                  pl.BlockSpec((None, 1, h), lambda d, i: (d, 0, 0)),
                  pl.BlockSpec((None, 1, h), lambda d, i: (d, 0, 0)),
                  pl.BlockSpec((q, di), lambda d, i: (cidx(d, i), 0)),
                  pl.BlockSpec((None, None, n, di), lambda d, i: (d, cidx(d, i), 0, 0)),
                  pl.BlockSpec((1, di), lambda d, i: (0, 0))] + cm.specs,
        out_specs=[pl.BlockSpec((None, q, cd), lambda d, i: (d, cidx(d, i), 0)),
                   pl.BlockSpec((None, q, h), lambda d, i: (d, cidx(d, i), 0)),
                   pl.BlockSpec((None, 1, h), lambda d, i: (d, 0, 0)),
                   pl.BlockSpec((None, 1, h), lambda d, i: (d, 0, 0))] + cm.specs,
        scratch_shapes=[pltpu.VMEM((n, di), F32), pltpu.VMEM((q, di), F32), pltpu.VMEM((q, di), F32)]
        + (cm.scratch if comm is not None else []),
        compiler_params=pltpu.CompilerParams(dimension_semantics=("arbitrary", "arbitrary"), vmem_limit_bytes=_vmem(24 * q * di * 4)),
    )(*[_hbm(v) for v in (xbc, xbc, xbc, dt2, bias2, alog2, dy, hp2, dskip_e)], *cm.operands)
    return res[0], res[1], res[2], res[3], cm.split(res[4:])


def _ssd_gate_fwd(name, y2, xbc, zx, dskip_e, norm_w, di, nct, t):
    def fn(rows, bcs, i):
        yf, yb, xs, z = rows
        zg = (yf + yb + bcs[0] * xs) * _silu(z)
        rn = lax.rsqrt(jnp.mean(zg * zg, axis=-1, keepdims=True) + EPS)
        return [zg * rn * bcs[1]], []

    ins = [_ri(y2, lead=0, ro=nct), _ri(y2, lead=1, ro=nct), _ri(xbc, di, 0, ro=nct), _ri(zx, di, 0, ro=nct)]
    return _rowwise(name, fn, t, ins, [dskip_e, norm_w], [(di, MXU)])[0][0]


def _ssd_gate_bwd(name, dyn, y2, xbc, zx, dskip_e, norm_w, di, nct, tt):
    def fn(rows, bcs, i):
        dn, yf, yb, xs, z = rows
        lat = i >= nct
        ytot = yf + yb + bcs[0] * xs
        sz = _silu(z)
        zg = ytot * sz
        rn = lax.rsqrt(jnp.mean(zg * zg, axis=-1, keepdims=True) + EPS)
        u = dn * bcs[1]
        dzg = rn * u - zg * (rn * rn * rn) * jnp.mean(u * zg, axis=-1, keepdims=True)
        dy = jnp.where(lat, dzg * sz, 0.0)
        dz = jnp.where(lat, dzg * ytot * _dsilu(z), 0.0)
        return [dy, dz], [jnp.where(lat, _colsum(dn * zg * rn), 0.0), jnp.where(lat, _colsum(dy * xs), 0.0)]

    ins = [_ri(dyn, ro=-nct), _ri(y2, lead=0), _ri(y2, lead=1), _ri(xbc, di, 0), _ri(zx, di, 0)]
    (dy, dzx), (dnw, ddsk) = _rowwise(name, fn, tt, ins, [dskip_e, norm_w], [(di, F32), (di, MXU, zx.shape[1], 0)], [(1, di)] * 2)
    return dy, dzx, dnw, ddsk


def _ada_fwd(name, cs, w, b):
    nl, d, c = w.shape
    r = cs.shape[0]

    def body(cs_ref, w_ref, b_ref, o_ref):
        o_ref[...] = _dot(_silu(cs_ref[...]), w_ref[...], NN) + b_ref[...]

    return pl.pallas_call(
        body, name=name, grid=(nl,), out_shape=pltpu.HBM((nl, r, c), F32),
        in_specs=[pl.BlockSpec((r, d), lambda l: (0, 0)), pl.BlockSpec((None, d, c), lambda l: (l, 0, 0)),
                  pl.BlockSpec((None, 1, c), lambda l: (l, 0, 0))],
        out_specs=pl.BlockSpec((None, r, c), lambda l: (l, 0, 0)),
        compiler_params=pltpu.CompilerParams(dimension_semantics=("parallel",), vmem_limit_bytes=_vmem(2 * d * c * 4)),
    )(_hbm(cs), _hbm(w), _hbm(b))


def _ada_bwd(name, cs, w, dmod):
    nl, d, c = w.shape
    r = cs.shape[0]

    def body(cs_ref, w_ref, dm_ref, dw_ref, dsc_ref):
        dm = dm_ref[...]
        dw_ref[...] = _dot(_silu(cs_ref[...]), dm, TN)

        @pl.when(pl.program_id(0) == 0)
        def _():
            dctx = jnp.broadcast_to(_colsum(dm[r // 2:]), (8, c))
            dsc_ref[...] = _dot(dctx, w_ref[...], NT)[0:1]

    return pl.pallas_call(
        body, name=name, grid=(nl,), out_shape=[pltpu.HBM((nl, d, c), F32), pltpu.HBM((1, d), F32)],
        in_specs=[pl.BlockSpec((r, d), lambda l: (0, 0)), pl.BlockSpec((None, d, c), lambda l: (l, 0, 0)),
                  pl.BlockSpec((None, r, c), lambda l: (l, 0, 0))],
        out_specs=[pl.BlockSpec((None, d, c), lambda l: (l, 0, 0)), pl.BlockSpec((1, d), lambda l: (0, 0))],
        compiler_params=pltpu.CompilerParams(dimension_semantics=("arbitrary",), vmem_limit_bytes=_vmem(4 * d * c * 4)),
    )(_hbm(cs), _hbm(w), _hbm(dmod))


def _adam_math(w, g, m, v):
    m = ADAM_B1 * m + (1.0 - ADAM_B1) * g
    v = ADAM_B2 * v + (1.0 - ADAM_B2) * (g * g)
    m_hat = m / (1.0 - ADAM_B1 ** ADAM_STEP)
    v_hat = v / (1.0 - ADAM_B2 ** ADAM_STEP)
    delta = -ADAM_LR * (m_hat / (jnp.sqrt(v_hat) + ADAM_EPS) + ADAM_WD * w)
    return delta, m, v


def _adam(name, slots, w, m, v):
    ns, r, c = slots.shape
    tr = _pick(r, (256, 128, 64, 32, 16, 8))

    def body(s_ref, w_ref, m_ref, v_ref, g_ref, d_ref, mo_ref, vo_ref):
        g = s_ref[0].astype(F32)
        for k in range(1, ns):
            g = g + s_ref[k].astype(F32)
        d, mn, vn = _adam_math(w_ref[...], g, m_ref[...], v_ref[...])
        g_ref[...], d_ref[...], mo_ref[...], vo_ref[...] = g, d, mn, vn

    blk = pl.BlockSpec((tr, c), lambda i: (i, 0))
    return pl.pallas_call(
        body, name=name, grid=(r // tr,), out_shape=[pltpu.HBM((r, c), F32)] * 4,
        in_specs=[pl.BlockSpec((ns, tr, c), lambda i: (0, i, 0)), blk, blk, blk], out_specs=[blk] * 4,
        compiler_params=pltpu.CompilerParams(dimension_semantics=("parallel",), vmem_limit_bytes=_vmem(16 * tr * c * 4)),
    )(_hbm(slots), _hbm(w), _hbm(m), _hbm(v))


def _adam_small(name, slots, ws, ms, vs, scale=None):
    k = len(slots)

    def body(*refs):
        s_refs, w_refs, m_refs, v_refs = refs[:k], refs[k:2 * k], refs[2 * k:3 * k], refs[3 * k:4 * k]
        sc_ref = refs[4 * k] if scale is not None else None
        outs = refs[4 * k + (scale is not None):]
        for a in range(k):
            g = s_refs[a][0]
            for j in range(1, NDEV):
                g = g + s_refs[a][j]
            if scale is not None and a == scale[0]:
                g = g * _dsilu(sc_ref[...])
            d, mn, vn = _adam_math(w_refs[a][...], g, m_refs[a][...], v_refs[a][...])
            outs[a][...], outs[k + a][...], outs[2 * k + a][...], outs[3 * k + a][...] = g, d, mn, vn

    shapes = [pltpu.HBM(w.shape, F32) for w in ws]
    extra = [scale[1]] if scale is not None else []
    ins = [*slots, *ws, *ms, *vs, *extra]

    def whole(shape):
        return pl.BlockSpec(shape, lambda i, nd=len(shape): (0,) * nd)

    res = pl.pallas_call(body, name=name, grid=(1,), out_shape=shapes * 4, in_specs=[whole(v.shape) for v in ins],
                         out_specs=[whole(s.shape) for s in shapes * 4])(*[_hbm(v) for v in ins])
    return res[:k], res[k:2 * k], res[2 * k:3 * k], res[3 * k:]


def _unshard_cols(g):
    g = jnp.moveaxis(g, 0, -2)
    return g.reshape(g.shape[:-2] + (g.shape[-2] * g.shape[-1],))


def _shard_cols(a):
    a = a.reshape(a.shape[:-1] + (NDEV, a.shape[-1] // NDEV))
    return jnp.moveaxis(a, -2, 0)


def _unshard_rows(g):
    g = jnp.moveaxis(g, 0, -3)
    return g.reshape(g.shape[:-3] + (g.shape[-3] * g.shape[-2], g.shape[-1]))


def _shard_rows(a):
    a = a.reshape(a.shape[:-2] + (NDEV, a.shape[-2] // NDEV, a.shape[-1]))
    return jnp.moveaxis(a, -3, 0)


def _flat2(a):
    return a.reshape((-1, a.shape[-1]))


def kernel(x, c, ctx, c_ctx, ada_w, ada_b, norm_mix_g, norm_ffn_g, final_norm_g, ssd_w_in, ssd_conv_w, ssd_conv_b, ssd_dt_bias_f, ssd_dt_bias_b, ssd_a_log_f, ssd_a_log_b, ssd_d_skip, ssd_norm_w, ssd_w_out, conf_w_pw1, conf_b_pw1, conf_dw_w, conf_dw_b, conf_ln_g, conf_ln_b, conf_w_pw2, conf_b_pw2, ffn_w_in, ffn_w_out, loss_target, m_c_ctx, m_ada_w, m_ada_b, m_norm_mix_g, m_norm_ffn_g, m_final_norm_g, m_ssd_w_in, m_ssd_conv_w, m_ssd_conv_b, m_ssd_dt_bias_f, m_ssd_dt_bias_b, m_ssd_a_log_f, m_ssd_a_log_b, m_ssd_d_skip, m_ssd_norm_w, m_ssd_w_out, m_conf_w_pw1, m_conf_b_pw1, m_conf_dw_w, m_conf_dw_b, m_conf_ln_g, m_conf_ln_b, m_conf_w_pw2, m_conf_b_pw2, m_ffn_w_in, m_ffn_w_out, v_c_ctx, v_ada_w, v_ada_b, v_norm_mix_g, v_norm_ffn_g, v_final_norm_g, v_ssd_w_in, v_ssd_conv_w, v_ssd_conv_b, v_ssd_dt_bias_f, v_ssd_dt_bias_b, v_ssd_a_log_f, v_ssd_a_log_b, v_ssd_d_skip, v_ssd_norm_w, v_ssd_w_out, v_conf_w_pw1, v_conf_b_pw1, v_conf_dw_w, v_conf_dw_b, v_conf_ln_g, v_conf_ln_b, v_conf_w_pw2, v_conf_b_pw2, v_ffn_w_in, v_ffn_w_out):
    args = dict(locals())
    names = ['c_ctx', 'ada_w', 'ada_b', 'norm_mix_g', 'norm_ffn_g', 'final_norm_g', 'ssd_w_in', 'ssd_conv_w', 'ssd_conv_b',
             'ssd_dt_bias_f', 'ssd_dt_bias_b', 'ssd_a_log_f', 'ssd_a_log_b', 'ssd_d_skip', 'ssd_norm_w', 'ssd_w_out',
             'conf_w_pw1', 'conf_b_pw1', 'conf_dw_w', 'conf_dw_b', 'conf_ln_g', 'conf_ln_b', 'conf_w_pw2', 'conf_b_pw2',
             'ffn_w_in', 'ffn_w_out']
    me = 4 * lax.axis_index("x") + 2 * lax.axis_index("y") + lax.axis_index("c")
    t, d = x.shape[1], x.shape[2]
    tc = ctx.shape[1]
    tt = tc + t
    nct = tc // ROW_TILE
    assert tc % ROW_TILE == 0 and t % ROW_TILE == 0
    h = ssd_dt_bias_f.shape[-1]
    di = ssd_norm_w.shape[-1]
    cdim = ssd_conv_b.shape[-1]
    kc = ssd_conv_w.shape[1]
    ck = conf_dw_w.shape[1]
    ch = d // 2
    rows_g = t // GRID_W
    nl = ada_w.shape[0]
    cw = ada_w.shape[2]
    x2, ctx2, tgt = x[0], ctx[0], loss_target[0]

    (c_all, w_in_g, convw_g), _ = _exchange("gather_first", [c, ssd_w_in[0].astype(WIRE), ssd_conv_w[0]])
    later = _Comm(gather=[ssd_w_out[0].astype(WIRE), ffn_w_in.astype(WIRE), ffn_w_out.astype(WIRE), conf_w_pw1[0].astype(WIRE),
                          conf_w_pw2[0].astype(WIRE), conf_b_pw1, conf_dw_w[0], conf_dw_b, conf_ln_g, conf_ln_b, conf_b_pw2])
    w_ssd_in = _unshard_cols(w_in_g)
    w_zx = w_ssd_in[:, :di + cdim]
    w_dt = jnp.pad(w_ssd_in[:, di + cdim:], ((0, 0), (0, LANES - 2 * h)))
    conv_w_full = _unshard_cols(convw_g)

    cs_all = jnp.concatenate([c_all[:, 0, :], jnp.broadcast_to(c_ctx[None, :], (NDEV, d))], axis=0)
    ada_b_mine = lax.dynamic_slice_in_dim(ada_b, me * cw, cw, axis=1)[:, None, :]
    mod_part = _ada_fwd("ada_fwd", cs_all, ada_w, ada_b_mine)
    (mod_g,), _ = _exchange("gather_mod", [mod_part])
    mod_all = jnp.moveaxis(mod_g, 0, 2).reshape(nl, 2 * NDEV, NDEV * cw)
    mod_lat = lax.dynamic_slice_in_dim(mod_all, me, 1, axis=1)[:, 0, :]
    mod_ctx = mod_all[0, NDEV, :]

    def six(v):
        return [v[k * d:(k + 1) * d][None, :] for k in range(6)]

    sh1, s1, g1, sh2, s2, g2 = six(mod_lat[0])
    csh1, cs1 = six(mod_ctx)[:2]
    sh1b, s1b, g1b, sh2b, s2b, g2b = six(mod_lat[1])
    nmg, nfg = norm_mix_g, norm_ffn_g

    h_all = jnp.concatenate([ctx2, x2], axis=0)
    s01, sh01 = jnp.concatenate([cs1, s1], axis=0), jnp.concatenate([csh1, sh1], axis=0)
    xn_all = _normmod_fwd("l0_norm", h_all, nmg[0:1], s01, sh01, nct)
    zx = _mm("ssd_in_proj", xn_all, w_zx, "nn", ACT)
    dtr = _mm("ssd_dt_proj", xn_all, w_dt, "nn")
    dt2 = jnp.moveaxis(dtr[:, :2 * h].reshape(tt, 2, h), 1, 0)
    bias2 = jnp.stack([ssd_dt_bias_f, ssd_dt_bias_b])
    alog2 = jnp.stack([ssd_a_log_f, ssd_a_log_b])
    xbc = _ssd_conv_fwd("ssd_conv", zx, conv_w_full, ssd_conv_b, di, tc)
    y2, hp2, (gat, _) = _ssd_fwd("ssd_scan", xbc, dt2, bias2, alog2, di, tc, comm=later)
    (w_out_g, fin_g, fout_g, pw1_g, pw2_g, bpw1_g, dww_g, dwb_g, lng_g, lnb_g, bpw2_g) = gat
    w_ssd_out = _unshard_rows(w_out_g)
    w_pw1, w_pw2 = _unshard_cols(pw1_g), _unshard_rows(pw2_g)
    w_fin, w_fout = _unshard_cols(fin_g), _unshard_rows(fout_g)
    dw_w_full = _unshard_cols(dww_g)
    b_pw1, dw_b, ln_g, ln_b, b_pw2 = (_unshard_cols(a) for a in (bpw1_g, dwb_g, lng_g, lnb_g, bpw2_g))
    dskip_e = jnp.repeat(ssd_d_skip, di // h, axis=1)
    yn = _ssd_gate_fwd("ssd_gate", y2, xbc, zx, dskip_e, ssd_norm_w, di, nct, t)
    mix0 = _mm("ssd_out_proj", yn, w_ssd_out, "nn")
    h1, xf0 = _resnorm_fwd("l0_res_norm", x2, mix0, g1, nfg[0:1], s2, sh2)
    u0 = _mm("ffn0_in", xf0, w_fin[0], "nn", ACT)
    hid0 = _swiglu_fwd("ffn0_act", u0)
    f0 = _mm("ffn0_out", hid0, w_fout[0], "nn")
    h2, xn1 = _resnorm_fwd("l1_norm", h1, f0, g2, nmg[1:2], s1b, sh1b)
    u1 = _mm("conf_pw1", xn1, w_pw1, "nn", ACT, bias=b_pw1)
    gl = _glu_fwd("conf_glu", u1)
    gl_h = _grid_t(gl[:, :ch], rows_g, GRID_W)
    gl_v = gl[:, ch:]
    v_h = _strided_conv("conf_conv_h", gl_h, dw_w_full[:, :ch], dw_b[:, :ch], rows_g)
    v_v = _strided_conv("conf_conv_v", gl_v, dw_w_full[:, ch:], dw_b[:, ch:], GRID_W)
    v = jnp.concatenate([_grid_t(v_h, GRID_W, rows_g), v_v], axis=1)
    sl = _ln_silu_fwd("conf_ln", v, ln_g, ln_b)
    mix1 = _mm("conf_pw2", sl, w_pw2, "nn", bias=b_pw2)
    h3, xf1 = _resnorm_fwd("l1_res_norm", h2, mix1, g1b, nfg[1:2], s2b, sh2b)
    u2 = _mm("ffn1_in", xf1, w_fin[1], "nn", ACT)
    hid1 = _swiglu_fwd("ffn1_act", u2)
    f1 = _mm("ffn1_out", hid1, w_fout[1], "nn")
    dh, sq, d_final_g = _final_loss("final_loss", h3, f1, tgt, g2b, final_norm_g[None, :])
    loss = lax.psum(0.5 * sq[0, 0] / d, AXES)

    zero2 = jnp.zeros((2, d), F32)

    def ffn_bwd(tag, dh, hin, xf, u, hid, f, gate, w_in, w_out, g_norm, s_mod):
        df, dgate, _ = _gate_bwd(tag + "_gate_bwd", dh, f, gate)
        dhid = _mm(tag + "_dhid", df, w_out, "nt", ACT)
        dw_out = _mm(tag + "_dwout", hid, df, "tn", WIRE)
        du = _swiglu_bwd(tag + "_act_bwd", u, dhid)
        dw_in = _mm(tag + "_dwin", xf, du, "tn", WIRE)
        dxf = _mm(tag + "_dx", du, w_in, "nt")
        s_2 = jnp.concatenate([s_mod, s_mod], axis=0)
        dh, dsh, ds, dg = _normmod_bwd(tag + "_norm_bwd", hin, dxf, dh, g_norm, s_2)
        return dh, dgate, dsh[1:2], ds[1:2], dg[1:2], dw_in, dw_out

    dh, d_g2b, d_sh2b, d_s2b, d_nfg1, g_fin1, g_fout1 = ffn_bwd("ffn1", dh, h3, xf1, u2, hid1, f1, g2b, w_fin[1], w_fout[1], nfg[1:2], s2b)
    dmix1, d_g1b, g_bpw2 = _gate_bwd("conf_gate_bwd", dh, mix1, g1b)
    dsl = _mm("conf_dsl", dmix1, w_pw2, "nt")
    g_pw2 = _mm("conf_dwpw2", sl, dmix1, "tn", WIRE)
    dv, g_lng, g_lnb = _ln_silu_bwd("conf_ln_bwd", v, dsl, ln_g, ln_b)
    dv_h, dv_v = _grid_t(dv[:, :ch], rows_g, GRID_W), dv[:, ch:]
    w_flip = dw_w_full[::-1]
    dgl_h = _strided_conv("conf_conv_h_bwd", dv_h, w_flip[:, :ch], None, rows_g)
    dgl_v = _strided_conv("conf_conv_v_bwd", dv_v, w_flip[:, ch:], None, GRID_W)
    g_dww_h, g_dwb_h = _strided_conv_dw("conf_conv_h_dw", gl_h, dv_h, ck, rows_g)
    g_dww_v, g_dwb_v = _strided_conv_dw("conf_conv_v_dw", gl_v, dv_v, ck, GRID_W)
    g_dww, g_dwb = jnp.concatenate([g_dww_h, g_dww_v], axis=1), jnp.concatenate([g_dwb_h, g_dwb_v], axis=1)
    dgl = jnp.concatenate([_grid_t(dgl_h, GRID_W, rows_g), dgl_v], axis=1)
    du1, g_bpw1 = _glu_bwd("conf_glu_bwd", u1, dgl)
    g_pw1 = _mm("conf_dwpw1", xn1, du1, "tn", WIRE)
    dxn1 = _mm("conf_dx", du1, w_pw1, "nt")
    dh, dsh_, ds_, dg_ = _normmod_bwd("l1_norm_bwd", h2, dxn1, dh, nmg[1:2], jnp.concatenate([s1b, s1b], axis=0))
    d_sh1b, d_s1b, d_nmg1 = dsh_[1:2], ds_[1:2], dg_[1:2]
    dh, d_g2, d_sh2, d_s2, d_nfg0, g_fin0, g_fout0 = ffn_bwd("ffn0", dh, h1, xf0, u0, hid0, f0, g2, w_fin[0], w_fout[0], nfg[0:1], s2)
    dmix0, d_g1, _ = _gate_bwd("ssd_gate_res_bwd", dh, mix0, g1)
    dyn = _mm("ssd_dyn", dmix0, w_ssd_out, "nt")
    g_ssd_out = _mm("ssd_dwout", yn, dmix0, "tn", WIRE)
    dy, dzx, g_normw, ddsk_e = _ssd_gate_bwd("ssd_gate_bwd", dyn, y2, xbc, zx, dskip_e, ssd_norm_w, di, nct, tt)
    early = [_shard_rows(g_ssd_out), _shard_cols(g_pw1), _shard_rows(g_pw2),
             _flat3(_shard_cols(jnp.stack([g_fin0, g_fin1]))), _flat3(_shard_rows(jnp.stack([g_fout0, g_fout1]))),
             _shard_cols(g_bpw1), _shard_cols(g_dww), _shard_cols(g_dwb), _shard_cols(g_lng), _shard_cols(g_lnb),
             _shard_cols(g_bpw2)]
    dxbc2, ddt2, g_alog2, g_bias2, (_, early_r) = _ssd_bwd("ssd_scan_bwd", xbc, dt2, bias2, alog2, dy, hp2, dskip_e, di, tc,
                                                           comm=_Comm(scatter=early))
    dzx, g_convw, g_convb = _ssd_conv_bwd("ssd_conv_bwd", zx, dxbc2, conv_w_full, ssd_conv_b, dzx, di, tc)
    ddt_p = jnp.pad(jnp.moveaxis(ddt2, 0, 1).reshape(tt, 2 * h), ((0, 0), (0, LANES - 2 * h))).astype(MXU)
    g_ssd_in = jnp.concatenate([_mm("ssd_dw_zx", xn_all, dzx, "tn", WIRE),
                                _mm("ssd_dw_dt", xn_all, ddt_p, "tn", WIRE)[:, :2 * h]], axis=1)
    dxn, (_, (ssd_in_r, convw_r)) = _mm("ssd_dx_zx", dzx, w_zx, "nt",
                                        comm=_Comm(scatter=[_shard_cols(g_ssd_in), _shard_cols(g_convw)]))
    dxn = _mm("ssd_dx_dt", ddt_p, w_dt, "nt", add=dxn)
    dh_all, dsh_, ds_, dg_ = _normmod_bwd("l0_norm_bwd", h_all, dxn, dh, nmg[0:1], s01, nct)
    grad_x = dh_all[tc:][None]
    d_csh1, d_sh1, d_cs1, d_s1 = dsh_[0:1], dsh_[1:2], ds_[0:1], ds_[1:2]
    d_nmg0 = dg_[0:1] + dg_[1:2]

    z1 = jnp.zeros((1, d), F32)
    dmod = jnp.concatenate([jnp.concatenate([d_sh1, d_s1, d_g1, d_sh2, d_s2, d_g2], axis=1),
                            jnp.concatenate([d_sh1b, d_s1b, d_g1b, d_sh2b, d_s2b, d_g2b], axis=1),
                            jnp.concatenate([d_csh1, d_cs1, z1, z1, z1, z1], axis=1)], axis=0)
    (dmod_g,), _ = _exchange("gather_dmod", [dmod])
    dmod_mine = lax.dynamic_slice_in_dim(dmod_g, me * cw, cw, axis=2)
    dmod16 = jnp.stack([jnp.concatenate([dmod_mine[:, 0], dmod_mine[:, 2]], axis=0),
                        jnp.concatenate([dmod_mine[:, 1], jnp.zeros((NDEV, cw), F32)], axis=0)])
    g_ada_w, dsc_part = _ada_bwd("ada_bwd", cs_all, ada_w, dmod16)
    g_ada_b = dmod[0:2] + jnp.concatenate([dmod[2:3], jnp.zeros((1, 6 * d), F32)], axis=0)

    d_dskip = jnp.sum(ddsk_e.reshape(h, di // h), axis=1)[None, :]
    rep = [dsc_part, g_ada_b, jnp.concatenate([d_nmg0, d_nmg1], axis=0), jnp.concatenate([d_nfg0, d_nfg1], axis=0),
           d_final_g, g_convb, g_bias2[0], g_bias2[1], g_alog2[0], g_alog2[1], d_dskip, g_normw]
    rep_g, _ = _exchange("gather_small_grads", rep)
    big_r = [ssd_in_r] + list(early_r[:5])
    small_r = [convw_r] + list(early_r[5:])

    out = {}

    def put(name, res):
        w = args[name]
        out["grad_" + name], out["delta_" + name], out["new_m_" + name], out["new_v_" + name] = (r.reshape(w.shape) for r in res)

    for name, slots in zip(["ssd_w_in", "ssd_w_out", "conf_w_pw1", "conf_w_pw2", "ffn_w_in", "ffn_w_out"], big_r):
        put(name, _adam("adam_" + name, slots, _flat2(args[name]), _flat2(args["m_" + name]), _flat2(args["v_" + name])))
    put("ada_w", _adam("adam_ada_w", _flat2(g_ada_w)[None], _flat2(ada_w), _flat2(m_ada_w), _flat2(v_ada_w)))
    small_names = ["ssd_conv_w", "conf_b_pw1", "conf_dw_w", "conf_dw_b", "conf_ln_g", "conf_ln_b", "conf_b_pw2",
                   "c_ctx", "ada_b", "norm_mix_g", "norm_ffn_g", "final_norm_g", "ssd_conv_b", "ssd_dt_bias_f", "ssd_dt_bias_b",
                   "ssd_a_log_f", "ssd_a_log_b", "ssd_d_skip", "ssd_norm_w"]
    slots = list(small_r) + list(rep_g)

    def as2(a):
        return a.reshape((1, -1)) if a.ndim == 1 else _flat2(a)

    res = _adam_small("adam_small", slots, [as2(args[n]) for n in small_names], [as2(args["m_" + n]) for n in small_names],
                      [as2(args["v_" + n]) for n in small_names], scale=(small_names.index("c_ctx"), c_ctx[None, :]))
    for k, name in enumerate(small_names):
        put(name, [r[k] for r in res])
    return (loss, grad_x, *[out["grad_" + n] for n in names], *[out["delta_" + n] for n in names],
            *[out["new_m_" + n] for n in names], *[out["new_v_" + n] for n in names])


def _flat3(a):
    return a.reshape((a.shape[0], -1, a.shape[-1]))
```

```python
import functools

import jax
import jax.numpy as jnp
from jax import lax
from jax.experimental import pallas as pl
from jax.experimental.pallas import tpu as pltpu

F32 = jnp.float32
MXU = jnp.bfloat16
WIRE = jnp.bfloat16
ACT = jnp.bfloat16
NDEV = 8
AXES = ("x", "y", "c")
SSD_STATE = 128
SSD_CHUNK = 128
GRID_W = 64
EPS = 1e-6
ROW_TILE = 256
LANES = 128
ADAM_LR, ADAM_B1, ADAM_B2, ADAM_EPS, ADAM_WD, ADAM_STEP = 0.001, 0.9, 0.999, 1e-08, 0.01, 10
VMEM_CAP = 56 * 2 ** 20
MESH_ID = pl.DeviceIdType.MESH


def _pick(dim, cands):
    for c in cands:
        if dim % c == 0:
            return c
    return dim


def _nbytes(shape, dtype):
    n = 1
    for s in shape:
        n *= s
    return n * jnp.dtype(dtype).itemsize


def _vmem(nbytes):
    return int(min(VMEM_CAP, max(24 * 2 ** 20, 2 * nbytes + 8 * 2 ** 20)))


def _sigmoid(x):
    return 1.0 / (1.0 + jnp.exp(-x))


def _silu(x):
    return x * _sigmoid(x)


def _dsilu(x):
    s = _sigmoid(x)
    return s * (1.0 + x * (1.0 - s))


def _softplus(x):
    return jnp.maximum(x, 0.0) + jnp.log(1.0 + jnp.exp(-jnp.abs(x)))


def _dot(a, b, dims):
    return lax.dot_general(a.astype(MXU), b.astype(MXU), (dims, ((), ())), preferred_element_type=F32)


NN, NT, TN = ((1,), (0,)), ((1,), (1,)), ((0,), (0,))


def _split(a, parts):
    out = []
    for _ in range(parts):
        p = a.astype(MXU)
        out.append(p)
        a = a - p.astype(F32)
    return out


def _dot_lx(e, a, dims, parts=2):
    return sum(lax.dot_general(e, p, (dims, ((), ())), preferred_element_type=F32) for p in _split(a, parts))


def _dot_rx(a, e, dims, parts=2):
    return sum(lax.dot_general(p, e, (dims, ((), ())), preferred_element_type=F32) for p in _split(a, parts))


class _Comm:
    def __init__(self, gather=(), scatter=()):
        self.gather, self.scatter = list(gather), list(scatter)
        self.ng, self.n = len(self.gather), len(self.gather) + len(self.scatter)
        self.operands = self.gather + self.scatter
        self.specs = [pl.BlockSpec(memory_space=pl.ANY)] * self.n
        self.out_shape = ([jax.ShapeDtypeStruct((NDEV,) + a.shape, a.dtype) for a in self.gather]
                          + [jax.ShapeDtypeStruct(a.shape, a.dtype) for a in self.scatter])
        self.scratch = [pltpu.SemaphoreType.DMA((self.n, 7)), pltpu.SemaphoreType.DMA((self.n, 7)),
                        pltpu.SemaphoreType.DMA((self.n,))]

    def split(self, res):
        return res[:self.ng], res[self.ng:]

    def _copies(self, ins, outs, sems):
        send, recv, loc = sems
        ng, n = self.ng, self.n
        x, y, c = lax.axis_index("x"), lax.axis_index("y"), lax.axis_index("c")
        me, sib = (x, y, c), (x, y, 1 - c)
        chips = [(1 - x, y), (x, 1 - y), (1 - x, 1 - y)]

        def slot(p):
            return 4 * p[0] + 2 * p[1] + p[2]

        def rcopy(a, k, src, dst, to):
            return functools.partial(pltpu.make_async_remote_copy, src_ref=src, dst_ref=dst, send_sem=send.at[a, k],
                                     recv_sem=recv.at[a, k], device_id=to, device_id_type=MESH_ID)

        local = [functools.partial(pltpu.make_async_copy, ins[a] if a < ng else ins[a].at[slot(me)], outs[a].at[slot(me)],
                                   loc.at[a]) for a in range(n)]
        rel = [(fx, fy, fc) for fx in (0, 1) for fy in (0, 1) for fc in (0, 1)][1:]
        first, landed, passed = [], [], []
        for a in range(ng, n):
            for k, (fx, fy, fc) in enumerate(rel):
                p = (1 - x if fx else x, 1 - y if fy else y, 1 - c if fc else c)
                first.append(rcopy(a, k, ins[a].at[slot(p)], outs[a].at[slot(me)], p))
                blk = outs[a].at[slot(p)]
                landed.append(rcopy(a, k, blk, blk, me))
        for a in range(ng):
            dst = outs[a].at[slot(me)]
            first.append(rcopy(a, 0, ins[a], dst, sib))
            first += [rcopy(a, 1 + j, ins[a], dst, (*ch, c)) for j, ch in enumerate(chips)]
            blk = outs[a].at[slot(sib)]
            landed.append(rcopy(a, 0, blk, blk, me))
            for j, ch in enumerate(chips):
                blk = outs[a].at[slot((*ch, c))]
                passed.append((rcopy(a, 1 + j, blk, blk, me), rcopy(a, 4 + j, blk, blk, sib)))
                blk = outs[a].at[slot((*ch, 1 - c))]
                landed.append(rcopy(a, 4 + j, blk, blk, me))
        return local, first, passed, landed

    def start(self, ins, outs, sems):
        local, first, _, _ = self._copies(ins, outs, sems)
        for make in local + first:
            make().start()

    def finish(self, ins, outs, sems):
        local, first, passed, landed = self._copies(ins, outs, sems)
        onward = []
        for arrived, forward in passed:
            arrived().wait_recv()
            onward.append(forward())
            onward[-1].start()
        for make in landed:
            make().wait_recv()
        for make in first:
            make().wait_send()
        for cp in onward:
            cp.wait_send()
        for make in local:
            make().wait()


def _carry(body, comm, n_in, n_out, grid):
    if comm is None:
        return body
    n = comm.n

    def wrapped(*refs):
        own_in, c_in = refs[:n_in], refs[n_in:n_in + n]
        own_out, c_out = refs[n_in + n:n_in + n + n_out], refs[n_in + n + n_out:n_in + 2 * n + n_out]
        own_scr, sems = refs[n_in + 2 * n + n_out:-3], refs[-3:]
        ids = [pl.program_id(ax) for ax in range(len(grid))]
        first, last = ids[0] == 0, ids[0] == grid[0] - 1
        for ax in range(1, len(grid)):
            first, last = first & (ids[ax] == 0), last & (ids[ax] == grid[ax] - 1)

        @pl.when(first)
        def _():
            comm.start(c_in, c_out, sems)

        body(*own_in, *own_out, *own_scr)

        @pl.when(last)
        def _():
            comm.finish(c_in, c_out, sems)

    return wrapped


def _exchange(name, gather, scatter=()):
    comm = _Comm(gather, scatter)
    n = comm.n

    def body(*refs):
        ins, outs, sems = refs[:n], refs[n:2 * n], refs[2 * n:]
        comm.start(ins, outs, sems)
        comm.finish(ins, outs, sems)

    res = pl.pallas_call(body, name=name, out_shape=comm.out_shape, in_specs=comm.specs, out_specs=comm.specs,
                         scratch_shapes=comm.scratch)(*comm.operands)
    return comm.split(res)


def _hbm(a):
    return pltpu.with_memory_space_constraint(a, pltpu.HBM)


def _divs(dim, mult):
    return [dim] + [dim // parts for parts in range(2, dim // mult + 1) if dim % parts == 0 and (dim // parts) % mult == 0]


MM_VMEM_BUDGET = 40 * 2 ** 20
GRID_STEP_US = 0.35
HBM_BYTES_PER_US = 3.0e6


def _mm_tiles(m, n, k, sizes, mode, has_add):
    sa, sb, so = sizes
    sub = 16
    best = None
    for tk in _divs(k, LANES):
        for tn in _divs(n, LANES):
            for tm in _divs(m, LANES if mode == "tn" else sub):
                nk = k // tk
                out_t = tm * tn
                est = (2 * (tm * tk * sa + tk * tn * sb) + 2 * out_t * so + 2 * (tm * tk + tk * tn) + 4 * out_t
                       + (4 * out_t if nk > 1 else 0) + (8 * out_t if has_add else 0))
                if est > MM_VMEM_BUDGET:
                    continue
                steps = (m // tm) * (n // tn) * nk
                cost = steps * GRID_STEP_US + (tm * tk * sa + tk * tn * sb + out_t * so) / HBM_BYTES_PER_US
                if best is None or cost < best[0]:
                    best = (cost, tm, tn, tk, est)
    assert best is not None, (m, n, k)
    return best[1:]


def _mm(name, a, b, mode, out_dtype=F32, bias=None, add=None, comm=None):
    if mode == "nn":
        (m, k), (k2, n) = a.shape, b.shape
    elif mode == "nt":
        (m, k), (n, k2) = a.shape, b.shape
    else:
        (k, m), (k2, n) = a.shape, b.shape
    assert k == k2, (name, a.shape, b.shape)
    sizes = (a.dtype.itemsize, b.dtype.itemsize, jnp.dtype(out_dtype).itemsize)
    tm, tn, tk, est = _mm_tiles(m, n, k, sizes, mode, add is not None)
    nk = k // tk
    dims = {"nn": NN, "nt": NT, "tn": TN}[mode]
    a_spec = pl.BlockSpec((tk, tm), lambda i, j, kk: (kk, i)) if mode == "tn" else pl.BlockSpec((tm, tk), lambda i, j, kk: (i, kk))
    b_spec = pl.BlockSpec((tn, tk), lambda i, j, kk: (j, kk)) if mode == "nt" else pl.BlockSpec((tk, tn), lambda i, j, kk: (kk, j))
    extra, extra_specs = [], []
    if bias is not None:
        extra.append(bias)
        extra_specs.append(pl.BlockSpec((1, tn), lambda i, j, kk: (0, j)))
    if add is not None:
        extra.append(add)
        extra_specs.append(pl.BlockSpec((tm, tn), lambda i, j, kk: (i, j)))

    def finish(r, extras, o_ref):
        for e in extras:
            r = r + e[...].astype(F32)
        o_ref[...] = r.astype(o_ref.dtype)

    def body_acc(*refs):
        a_ref, b_ref = refs[:2]
        o_ref, acc = refs[-2:]
        kk = pl.program_id(2)

        @pl.when(kk == 0)
        def _():
            acc[...] = jnp.zeros_like(acc)

        acc[...] += _dot(a_ref[...], b_ref[...], dims)

        @pl.when(kk == nk - 1)
        def _():
            finish(acc[...], refs[2:-2], o_ref)

    def body_one(*refs):
        finish(_dot(refs[0][...], refs[1][...], dims), refs[2:-1], refs[-1])

    cm = comm if comm is not None else _Comm()
    grid = (m // tm, n // tn, nk)
    res = pl.pallas_call(
        _carry(body_acc if nk > 1 else body_one, comm, 2 + len(extra), 1, grid), name=name, grid=grid,
        out_shape=[pltpu.HBM((m, n), out_dtype)] + cm.out_shape,
        in_specs=[a_spec, b_spec] + extra_specs + cm.specs,
        out_specs=[pl.BlockSpec((tm, tn), lambda i, j, kk: (i, j))] + cm.specs,
        scratch_shapes=([pltpu.VMEM((tm, tn), F32)] if nk > 1 else []) + (cm.scratch if comm is not None else []),
        compiler_params=pltpu.CompilerParams(
            dimension_semantics=("parallel", "parallel", "arbitrary") if comm is None else ("arbitrary",) * 3,
            vmem_limit_bytes=int(min(VMEM_CAP, est + 12 * 2 ** 20))),
    )(*[_hbm(v) for v in (a, b, *extra)], *cm.operands)
    return res[0] if comm is None else (res[0], cm.split(res[1:]))


def _ri(arr, w=None, cb=0, ro=0, lead=None):
    return (arr, arr.shape[-1] if w is None else w, cb, ro, lead)


def _rowwise(name, fn, nrows, row_ins, bc_ins, outs, accs=()):
    tr = min(ROW_TILE, nrows)
    assert nrows % tr == 0
    in_specs = []
    for (arr, w, cb, ro, lead) in row_ins:
        if lead is None:
            in_specs.append(pl.BlockSpec((tr, w), lambda i, cb=cb, ro=ro: (jnp.maximum(i + ro, 0), cb)))
        else:
            in_specs.append(pl.BlockSpec((None, tr, w), lambda i, cb=cb, ro=ro, lead=lead: (lead, jnp.maximum(i + ro, 0), cb)))
    for arr in bc_ins:
        in_specs.append(pl.BlockSpec(arr.shape, lambda i, nd=arr.ndim: (0,) * nd))
    outs = [o if len(o) == 4 else (o[0], o[1], o[0], 0) for o in outs]
    out_shape = [pltpu.HBM((nrows, total), dt) for _, dt, total, _ in outs] + [pltpu.HBM(s, F32) for s in accs]
    out_specs = ([pl.BlockSpec((tr, c), lambda i, cb=cb: (i, cb)) for c, _, _, cb in outs]
                 + [pl.BlockSpec(s, lambda i: (0, 0)) for s in accs])
    nr, nb, no = len(row_ins), len(bc_ins), len(outs)

    def body(*refs):
        i = pl.program_id(0)
        rows = [r[...].astype(F32) for r in refs[:nr]]
        bcs = [r[...] for r in refs[nr:nr + nb]]
        o, a = fn(rows, bcs, i)
        for ref, val in zip(refs[nr + nb:nr + nb + no], o):
            ref[...] = val.astype(ref.dtype)
        for ref, val in zip(refs[nr + nb + no:], a):
            @pl.when(i == 0)
            def _(ref=ref, val=val):
                ref[...] = val

            @pl.when(i > 0)
            def _(ref=ref, val=val):
                ref[...] += val

    est = sum(tr * w * arr.dtype.itemsize for (arr, w, _, _, _) in row_ins) + sum(tr * o[0] * 4 for o in outs)
    res = pl.pallas_call(
        body, name=name, grid=(nrows // tr,), out_shape=out_shape, in_specs=in_specs, out_specs=out_specs,
        compiler_params=pltpu.CompilerParams(dimension_semantics=("arbitrary",), vmem_limit_bytes=_vmem(3 * est)),
    )(*[_hbm(r[0]) for r in row_ins], *[_hbm(v) for v in bc_ins])
    return res[:no], res[no:]


def _colsum(v):
    return jnp.sum(v, axis=0, keepdims=True)


def _normmod_fwd(name, h, g, s, sh, nct=0):
    d = h.shape[1]

    def fn(rows, bcs, i):
        hh, (g_, s_, sh_) = rows[0], bcs
        s1 = jnp.where(i < nct, s_[0:1], s_[1:2])
        sh1 = jnp.where(i < nct, sh_[0:1], sh_[1:2])
        r = lax.rsqrt(jnp.mean(hh * hh, axis=-1, keepdims=True) + EPS)
        return [hh * r * g_ * (1.0 + s1) + sh1], []

    return _rowwise(name, fn, h.shape[0], [_ri(h)], [g, s, sh], [(d, MXU)])[0][0]


def _normmod_bwd(name, h, dxn, dres, g, s, nct=0):
    d = h.shape[1]

    def fn(rows, bcs, i):
        hh, dx, dr = rows
        g_, s_ = bcs
        ctx = i < nct
        s1 = jnp.where(ctx, s_[0:1], s_[1:2])
        r = lax.rsqrt(jnp.mean(hh * hh, axis=-1, keepdims=True) + EPS)
        hr = hh * r
        dy = dx * (1.0 + s1)
        u = dy * g_
        dh = r * u - hr * (r * r) * jnp.mean(u * hh, axis=-1, keepdims=True)
        dh = dh + jnp.where(ctx, 0.0, dr)

        def seg(v):
            v = _colsum(v)
            return jnp.concatenate([jnp.where(ctx, v, 0.0), jnp.where(ctx, 0.0, v)], axis=0)

        return [dh], [seg(dx), seg(dx * hr * g_), seg(dy * hr)]

    (dh,), (dsh, ds, dg) = _rowwise(name, fn, h.shape[0], [_ri(h), _ri(dxn), _ri(dres, ro=-nct)], [g, s],
                                    [(d, F32)], [(2, d)] * 3)
    return dh, dsh, ds, dg


def _resnorm_fwd(name, h, y, gate, g, s, sh):
    d = h.shape[1]

    def fn(rows, bcs, i):
        hh, yy = rows
        gate_, g_, s_, sh_ = bcs
        hn = hh + gate_ * yy
        r = lax.rsqrt(jnp.mean(hn * hn, axis=-1, keepdims=True) + EPS)
        return [hn, hn * r * g_ * (1.0 + s_) + sh_], []

    return _rowwise(name, fn, h.shape[0], [_ri(h), _ri(y)], [gate, g, s, sh], [(d, F32), (d, MXU)])[0]


def _gate_bwd(name, dh, y, gate):
    d = dh.shape[1]

    def fn(rows, bcs, i):
        dd, yy = rows
        dy = dd * bcs[0]
        return [dy], [_colsum(dd * yy), _colsum(dy)]

    (dy,), (dgate, dbias) = _rowwise(name, fn, dh.shape[0], [_ri(dh), _ri(y)], [gate], [(d, MXU)], [(1, d)] * 2)
    return dy, dgate, dbias


def _swiglu_fwd(name, u):
    f = u.shape[1] // 2

    def fn(rows, bcs, i):
        return [_silu(rows[0]) * rows[1]], []

    return _rowwise(name, fn, u.shape[0], [_ri(u, f, 0), _ri(u, f, 1)], [], [(f, MXU)])[0][0]


def _swiglu_bwd(name, u, dhid):
    f = u.shape[1] // 2

    def fn(rows, bcs, i):
        a, b, dd = rows
        return [jnp.concatenate([dd * b * _dsilu(a), dd * _silu(a)], axis=1)], []

    return _rowwise(name, fn, u.shape[0], [_ri(u, f, 0), _ri(u, f, 1), _ri(dhid)], [], [(2 * f, MXU)])[0][0]


def _glu_fwd(name, u):
    d = u.shape[1] // 2

    def fn(rows, bcs, i):
        return [rows[0] * _sigmoid(rows[1])], []

    return _rowwise(name, fn, u.shape[0], [_ri(u, d, 0), _ri(u, d, 1)], [], [(d, F32)])[0][0]


def _glu_bwd(name, u, dgl):
    d = u.shape[1] // 2

    def fn(rows, bcs, i):
        a, b, dd = rows
        sg = _sigmoid(b)
        du = jnp.concatenate([dd * sg, dd * a * sg * (1.0 - sg)], axis=1)
        return [du], [_colsum(du)]

    (du,), (db,) = _rowwise(name, fn, u.shape[0], [_ri(u, d, 0), _ri(u, d, 1), _ri(dgl)], [], [(2 * d, MXU)], [(1, 2 * d)])
    return du, db


def _ln_silu_fwd(name, v, g, b):
    d = v.shape[1]

    def fn(rows, bcs, i):
        vv = rows[0]
        mu = jnp.mean(vv, axis=-1, keepdims=True)
        xc = vv - mu
        rs = lax.rsqrt(jnp.mean(xc * xc, axis=-1, keepdims=True) + EPS)
        return [_silu(xc * rs * bcs[0] + bcs[1])], []

    return _rowwise(name, fn, v.shape[0], [_ri(v)], [g, b], [(d, MXU)])[0][0]


def _ln_silu_bwd(name, v, ds, g, b):
    d = v.shape[1]

    def fn(rows, bcs, i):
        vv, dd = rows
        mu = jnp.mean(vv, axis=-1, keepdims=True)
        xc = vv - mu
        rs = lax.rsqrt(jnp.mean(xc * xc, axis=-1, keepdims=True) + EPS)
        xh = xc * rs
        dln = dd * _dsilu(xh * bcs[0] + bcs[1])
        dxh = dln * bcs[0]
        dv = rs * (dxh - jnp.mean(dxh, axis=-1, keepdims=True) - xh * jnp.mean(dxh * xh, axis=-1, keepdims=True))
        return [dv], [_colsum(dln * xh), _colsum(dln)]

    (dv,), (dg, db) = _rowwise(name, fn, v.shape[0], [_ri(v), _ri(ds)], [g, b], [(d, F32)], [(1, d)] * 2)
    return dv, dg, db


def _final_loss(name, h, f, target, gate, gf):
    d = h.shape[1]

    def fn(rows, bcs, i):
        hh, ff, tg = rows
        gate_, g_ = bcs
        hn = hh + gate_ * ff
        r = lax.rsqrt(jnp.mean(hn * hn, axis=-1, keepdims=True) + EPS)
        hr = hn * r
        err = hr * g_ - tg
        dout = err * (1.0 / d)
        u = dout * g_
        dh = r * u - hr * (r * r) * jnp.mean(u * hn, axis=-1, keepdims=True)
        sq = jnp.sum(_colsum(err * err), axis=1, keepdims=True)
        return [dh], [jnp.broadcast_to(sq, (1, LANES)), _colsum(dout * hr)]

    (dh,), (sq, dgf) = _rowwise(name, fn, h.shape[0], [_ri(h), _ri(f), _ri(target)], [gate, gf], [(d, F32)], [(1, LANES), (1, d)])
    return dh, sq, dgf


def _shift_rows(x, o, seg_lo, seg_hi, row):
    n = x.shape[0]
    if o == 0:
        return x
    sh = pltpu.roll(x, (-o) % n, 0)
    ok = (row + o >= seg_lo) & (row + o < seg_hi)
    return jnp.where(ok, sh, 0.0)


def _seg_bounds(row, tc, tt):
    ctx = row < tc
    return jnp.where(ctx, 0, tc), jnp.where(ctx, tc, tt)


def _ssd_conv_fwd(name, zx, w, b, di, tc, comm=None):
    tt, kc, cd = zx.shape[0], w.shape[0], w.shape[1]
    cb = _pick(cd, (LANES,))
    off = di // cb

    def body(x_ref, w_ref, b_ref, o_ref):
        x = x_ref[...].astype(F32)
        row = lax.broadcasted_iota(jnp.int32, (tt, 1), 0)
        lo, hi = _seg_bounds(row, tc, tt)
        acc = jnp.broadcast_to(b_ref[...], x.shape)
        for k in range(kc):
            acc = acc + w_ref[k:k + 1, :] * _shift_rows(x, k - kc // 2, lo, hi, row)
        o_ref[...] = _silu(acc).astype(o_ref.dtype)

    cm = comm if comm is not None else _Comm()
    res = pl.pallas_call(
        _carry(body, comm, 3, 1, (cd // cb,)), name=name, grid=(cd // cb,), out_shape=[pltpu.HBM((tt, cd), ACT)] + cm.out_shape,
        in_specs=[pl.BlockSpec((tt, cb), lambda j: (0, j + off)), pl.BlockSpec((kc, cb), lambda j: (0, j)),
                  pl.BlockSpec((1, cb), lambda j: (0, j))] + cm.specs,
        out_specs=[pl.BlockSpec((tt, cb), lambda j: (0, j))] + cm.specs,
        scratch_shapes=cm.scratch if comm is not None else [],
        compiler_params=pltpu.CompilerParams(dimension_semantics=("arbitrary",), vmem_limit_bytes=_vmem(4 * tt * cb * 4)),
    )(_hbm(zx), _hbm(w), _hbm(b), *cm.operands)
    return res[0], cm.split(res[1:])


def _ssd_conv_bwd(name, zx, dact2, w, b, dzx, di, tc, comm=None):
    tt, kc, cd = zx.shape[0], w.shape[0], w.shape[1]
    cb = _pick(cd, (LANES,))
    off = di // cb

    def body(x_ref, d0_ref, d1_ref, w_ref, b_ref, _, dx_ref, dw_ref, db_ref):
        x = x_ref[...].astype(F32)
        row = lax.broadcasted_iota(jnp.int32, (tt, 1), 0)
        lo, hi = _seg_bounds(row, tc, tt)
        pre = jnp.broadcast_to(b_ref[...], x.shape)
        for k in range(kc):
            pre = pre + w_ref[k:k + 1, :] * _shift_rows(x, k - kc // 2, lo, hi, row)
        dpre = (d0_ref[...].astype(F32) + d1_ref[...].astype(F32)) * _dsilu(pre)
        dx = jnp.zeros_like(x)
        for k in range(kc):
            o = k - kc // 2
            dx = dx + w_ref[k:k + 1, :] * _shift_rows(dpre, -o, lo, hi, row)
            dw_ref[k:k + 1, :] = _colsum(dpre * _shift_rows(x, o, lo, hi, row))
        dx_ref[...] = dx.astype(dx_ref.dtype)
        db_ref[...] = _colsum(dpre)

    cm = comm if comm is not None else _Comm()
    res = pl.pallas_call(
        _carry(body, comm, 6, 3, (cd // cb,)), name=name, grid=(cd // cb,),
        out_shape=[pltpu.HBM(dzx.shape, dzx.dtype), pltpu.HBM((kc, cd), F32), pltpu.HBM((1, cd), F32)] + cm.out_shape,
        in_specs=[pl.BlockSpec((tt, cb), lambda j: (0, j + off)), pl.BlockSpec((None, tt, cb), lambda j: (0, 0, j)),
                  pl.BlockSpec((None, tt, cb), lambda j: (1, 0, j)), pl.BlockSpec((kc, cb), lambda j: (0, j)),
                  pl.BlockSpec((1, cb), lambda j: (0, j)), pl.BlockSpec(memory_space=pl.ANY)] + cm.specs,
        out_specs=[pl.BlockSpec((tt, cb), lambda j: (0, j + off)), pl.BlockSpec((kc, cb), lambda j: (0, j)),
                   pl.BlockSpec((1, cb), lambda j: (0, j))] + cm.specs,
        input_output_aliases={5: 0}, scratch_shapes=cm.scratch if comm is not None else [],
        compiler_params=pltpu.CompilerParams(dimension_semantics=("arbitrary",), vmem_limit_bytes=_vmem(8 * tt * cb * 4)),
    )(_hbm(zx), _hbm(dact2), _hbm(dact2), _hbm(w), _hbm(b), _hbm(dzx), *cm.operands)
    return res[0], res[1], res[2], cm.split(res[3:])


def _strided_conv(name, x, w, b, stride):
    t, ch = x.shape
    kk = w.shape[0]
    pad = (kk // 2) * stride
    cb = _pick(ch, (LANES,))
    has_b = b is not None

    def body(*refs):
        x_ref, w_ref = refs[:2]
        o_ref, xp = refs[-2:]
        xp[0:pad, :] = jnp.zeros((pad, cb), F32)
        xp[pad + t:, :] = jnp.zeros((pad, cb), F32)
        xp[pad:pad + t, :] = x_ref[...]
        acc = jnp.broadcast_to(refs[2][...], (t, cb)) if has_b else jnp.zeros((t, cb), F32)
        for k in range(kk):
            acc = acc + w_ref[k:k + 1, :] * xp[k * stride:k * stride + t, :]
        o_ref[...] = acc

    ins, specs = [x, w], [pl.BlockSpec((t, cb), lambda j: (0, j)), pl.BlockSpec((kk, cb), lambda j: (0, j))]
    if has_b:
        ins.append(b)
        specs.append(pl.BlockSpec((1, cb), lambda j: (0, j)))
    return pl.pallas_call(
        body, name=name, grid=(ch // cb,), out_shape=pltpu.HBM((t, ch), F32), in_specs=specs,
        out_specs=pl.BlockSpec((t, cb), lambda j: (0, j)), scratch_shapes=[pltpu.VMEM((t + 2 * pad, cb), F32)],
        compiler_params=pltpu.CompilerParams(dimension_semantics=("parallel",), vmem_limit_bytes=_vmem(6 * t * cb * 4)),
    )(*[_hbm(v) for v in ins])


def _strided_conv_dw(name, x, dv, kk, stride):
    t, ch = x.shape
    pad = (kk // 2) * stride
    cb = _pick(ch, (LANES,))

    def body(x_ref, d_ref, dw_ref, db_ref, xp):
        xp[0:pad, :] = jnp.zeros((pad, cb), F32)
        xp[pad + t:, :] = jnp.zeros((pad, cb), F32)
        xp[pad:pad + t, :] = x_ref[...]
        d = d_ref[...]
        for k in range(kk):
            dw_ref[k:k + 1, :] = _colsum(d * xp[k * stride:k * stride + t, :])
        db_ref[...] = _colsum(d)

    blk = pl.BlockSpec((t, cb), lambda j: (0, j))
    return pl.pallas_call(
        body, name=name, grid=(ch // cb,), out_shape=[pltpu.HBM((kk, ch), F32), pltpu.HBM((1, ch), F32)],
        in_specs=[blk, blk], out_specs=[pl.BlockSpec((kk, cb), lambda j: (0, j)), pl.BlockSpec((1, cb), lambda j: (0, j))],
        scratch_shapes=[pltpu.VMEM((t + 2 * pad, cb), F32)],
        compiler_params=pltpu.CompilerParams(dimension_semantics=("parallel",), vmem_limit_bytes=_vmem(6 * t * cb * 4)),
    )(_hbm(x), _hbm(dv))


def _grid_t(a, n1, n2):
    return a.reshape(n1, n2, a.shape[-1]).swapaxes(0, 1).reshape(n1 * n2, a.shape[-1])


def _chunk_order(d, i, ncc, nc):
    back = jnp.where(i < ncc, ncc - 1 - i, nc - 1 - (i - ncc))
    return jnp.where(d == 0, i, back)


def _ssd_chunk_setup(d, dt_raw, bias, a_log, q, h, di):
    p = di // h
    dt = _softplus(dt_raw + bias)
    a_neg = -jnp.exp(a_log)
    delta = dt * a_neg
    r = lax.broadcasted_iota(jnp.int32, (q, q), 0)
    c = lax.broadcasted_iota(jnp.int32, (q, q), 1)
    sgn = 1 - 2 * d
    mask = (r - c) * sgn >= 0
    mask_t = (c - r) * sgn >= 0
    a = _dot_lx(mask.astype(MXU), delta, NN, parts=3)
    tot = _colsum(delta)
    ea, dte, cd = jnp.exp(a), jnp.exp(tot - a), jnp.exp(tot)
    hh = lax.broadcasted_iota(jnp.int32, (h, di), 0)
    cc = lax.broadcasted_iota(jnp.int32, (h, di), 1)
    e = (cc // p == hh).astype(MXU)
    ex = _dot_rx(jnp.concatenate([dt, ea, dte, jnp.broadcast_to(cd, (8, h))], axis=0), e, NN)
    eye = (lax.broadcasted_iota(jnp.int32, (h, h), 0) == lax.broadcasted_iota(jnp.int32, (h, h), 1)).astype(MXU)
    a_t = _dot_lx(eye, a, NT, parts=3)
    return dict(dt=dt, a_neg=a_neg, a=a, a_t=a_t, mask=mask, mask_t=mask_t, e=e,
                dt_e=ex[0:q], ea_e=ex[q:2 * q], dte_e=ex[2 * q:3 * q], cd_e=ex[3 * q:3 * q + 1])


def _pick_heads(r, q, hpg, p):
    lane = lax.broadcasted_iota(jnp.int32, (q, hpg * p), 1) // p
    out = jnp.zeros((q, hpg * p), F32)
    for j in range(hpg):
        out = out + jnp.where(lane == j, r[j * q:(j + 1) * q], 0.0)
    return out


def _ssd_fwd(name, xbc, dt2, bias2, alog2, di, tc, comm=None):
    tt, cd = xbc.shape
    h = dt2.shape[-1]
    q, n = SSD_CHUNK, SSD_STATE
    gn = (cd - di) // 2
    g = gn // n
    hpg, p = h // g, di // h
    gp = hpg * p
    nc, ncc = tt // q, tc // q
    assert di % gn == 0

    def body(x_ref, b_ref, c_ref, dt_ref, bias_ref, alog_ref, y_ref, hp_ref, ht):
        d, i = pl.program_id(0), pl.program_id(1)

        @pl.when(i == 0)
        def _():
            ht[...] = jnp.zeros_like(ht)

        s = _ssd_chunk_setup(d, dt_ref[...], bias_ref[...], alog_ref[...], q, h, di)
        xd = x_ref[...].astype(F32) * s["dt_e"]
        hp_ref[...] = ht[...].astype(hp_ref.dtype)
        for gi in range(g):
            bg, cg = b_ref[:, gi * n:(gi + 1) * n].astype(MXU), c_ref[:, gi * n:(gi + 1) * n].astype(MXU)
            sl = slice(gi * gp, (gi + 1) * gp)
            sc = _dot(cg, bg, NT)
            ms = []
            for j in range(hpg):
                hd = gi * hpg + j
                seg = s["a"][:, hd:hd + 1] - s["a_t"][hd:hd + 1, :]
                ms.append(sc * jnp.exp(jnp.where(s["mask"], seg, -jnp.inf)))
            xdg = xd[:, sl]
            ydiag = _pick_heads(_dot(jnp.concatenate(ms, axis=0), xdg, NN), q, hpg, p)
            htg = ht[:, sl]
            y_ref[:, sl] = ydiag + _dot(cg, htg, NN) * s["ea_e"][:, sl]
            ht[:, sl] = s["cd_e"][:, sl] * htg + _dot(bg, xdg * s["dte_e"][:, sl], TN)

    def cidx(d, i):
        return _chunk_order(d, i, ncc, nc)

    cm = comm if comm is not None else _Comm()
    res = pl.pallas_call(
        _carry(body, comm, 6, 2, (2, nc)), name=name, grid=(2, nc),
        out_shape=[pltpu.HBM((2, tt, di), F32), pltpu.HBM((2, nc, n, di), ACT)] + cm.out_shape,
        in_specs=[pl.BlockSpec((q, di), lambda d, i: (cidx(d, i), 0)),
                  pl.BlockSpec((q, gn), lambda d, i: (cidx(d, i), di // gn)),
                  pl.BlockSpec((q, gn), lambda d, i: (cidx(d, i), di // gn + 1)),
                  pl.BlockSpec((None, q, h), lambda d, i: (d, cidx(d, i), 0)),
                  pl.BlockSpec((None, 1, h), lambda d, i: (d, 0, 0)),
                  pl.BlockSpec((None, 1, h), lambda d, i: (d, 0, 0))] + cm.specs,
        out_specs=[pl.BlockSpec((None, q, di), lambda d, i: (d, cidx(d, i), 0)),
                   pl.BlockSpec((None, None, n, di), lambda d, i: (d, cidx(d, i), 0, 0))] + cm.specs,
        scratch_shapes=[pltpu.VMEM((n, di), F32)] + (cm.scratch if comm is not None else []),
        compiler_params=pltpu.CompilerParams(dimension_semantics=("arbitrary", "arbitrary"), vmem_limit_bytes=_vmem(16 * q * di * 4)),
    )(*[_hbm(v) for v in (xbc, xbc, xbc, dt2, bias2, alog2)], *cm.operands)
    return res[0], res[1], cm.split(res[2:])


def _ssd_bwd(name, xbc, dt2, bias2, alog2, dy, hp2, dskip_e, di, tc, comm=None):
    tt, cd = xbc.shape
    h = dt2.shape[-1]
    q, n = SSD_CHUNK, SSD_STATE
    gn = (cd - di) // 2
    g = gn // n
    hpg, p = h // g, di // h
    gp = hpg * p
    nc, ncc = tt // q, tc // q

    def body(x_ref, b_ref, c_ref, dt_ref, bias_ref, alog_ref, dy_ref, hp_ref, dsk_ref,
             dxbc_ref, ddt_ref, dalog_ref, dbias_ref, dht, dxd, off):
        d, i = pl.program_id(0), pl.program_id(1)

        @pl.when(i == 0)
        def _():
            dht[...] = jnp.zeros_like(dht)
            dalog_ref[...] = jnp.zeros_like(dalog_ref)
            dbias_ref[...] = jnp.zeros_like(dbias_ref)

        s = _ssd_chunk_setup(d, dt_ref[...], bias_ref[...], alog_ref[...], q, h, di)
        x, dyc = x_ref[...].astype(F32), dy_ref[...]
        xd = x * s["dt_e"]
        dyea = dyc * s["ea_e"]
        xdte = xd * s["dte_e"]
        lane = lax.broadcasted_iota(jnp.int32, (q, gp), 1) // p
        lane_h = lax.broadcasted_iota(jnp.int32, (q, h), 1)
        da_d = jnp.zeros((q, h), F32)
        last_e = []
        for gi in range(g):
            bg, cg = b_ref[:, gi * n:(gi + 1) * n].astype(MXU), c_ref[:, gi * n:(gi + 1) * n].astype(MXU)
            sl = slice(gi * gp, (gi + 1) * gp)
            sc, sct = _dot(cg, bg, NT), _dot(bg, cg, NT)
            dyg, xdg = dyc[:, sl], xd[:, sl]
            htg, dhtg = hp_ref[:, sl].astype(F32), dht[:, sl]
            dystack = jnp.concatenate([jnp.where(lane == j, dyg, 0.0) for j in range(hpg)], axis=0)
            xdstack = jnp.concatenate([jnp.where(lane == j, xdg, 0.0) for j in range(hpg)], axis=0)
            gs = _dot(dystack, xdg, NT)
            gst = _dot(xdstack, dyg, NT)
            ds = jnp.zeros((q, q), F32)
            mts = []
            for j in range(hpg):
                hd = gi * hpg + j
                col, rw = s["a"][:, hd:hd + 1], s["a_t"][hd:hd + 1, :]
                gl = gs[j * q:(j + 1) * q] * jnp.exp(jnp.where(s["mask"], col - rw, -jnp.inf))
                ds = ds + gl
                mt = sct * jnp.exp(jnp.where(s["mask_t"], rw - col, -jnp.inf))
                mts.append(mt)
                da_j = jnp.sum(gl * sc, axis=1, keepdims=True) - jnp.sum(gst[j * q:(j + 1) * q] * mt, axis=1, keepdims=True)
                da_d = da_d + jnp.where(lane_h == hd, da_j, 0.0)
            dxd_diag = _pick_heads(_dot(jnp.concatenate(mts, axis=0), dyg, NN), q, hpg, p)
            z = _dot(bg, dhtg, NN) * s["dte_e"][:, sl]
            yoff = _dot(cg, htg, NN) * s["ea_e"][:, sl]
            off[:, sl] = dyg * yoff - xdg * z
            dxd[:, sl] = dxd_diag + z
            dxbc_ref[:, di + gi * n:di + (gi + 1) * n] = (_dot(ds, cg, TN) + _dot(xdte[:, sl], dhtg, NT)).astype(dxbc_ref.dtype)
            dxbc_ref[:, di + gn + gi * n:di + gn + (gi + 1) * n] = (_dot(ds, bg, NN)
                                                                    + _dot(dyea[:, sl], htg, NT)).astype(dxbc_ref.dtype)
            last_e.append(s["cd_e"][:, sl] * _colsum(dhtg * htg) + _colsum(xdg * z))
            dht[:, sl] = s["cd_e"][:, sl] * dhtg + _dot(cg, dyea[:, sl], TN)
        dxd_all = dxd[...]
        last = jnp.concatenate(last_e, axis=1)
        da = da_d + _dot_rx(off[...], s["e"], NT)
        last_h = _dot_rx(jnp.broadcast_to(last, (8, di)), s["e"], NT)[0:1]
        ddelta = _dot_lx(s["mask_t"].astype(MXU), da, NN, parts=3) + last_h
        ddt = ddelta * s["a_neg"] + _dot_rx(dxd_all * x, s["e"], NT)
        ddt_raw = ddt * _sigmoid(dt_ref[...] + bias_ref[...])
        ddt_ref[...] = ddt_raw
        dalog_ref[...] += _colsum(ddelta * s["dt"]) * s["a_neg"]
        dbias_ref[...] += _colsum(ddt_raw)
        dxbc_ref[:, 0:di] = (dxd_all * s["dt_e"] + jnp.where(d == 0, dyc * dsk_ref[...], 0.0)).astype(dxbc_ref.dtype)

    def cidx(d, i):
        return _chunk_order(d, nc - 1 - i, ncc, nc)

    cm = comm if comm is not None else _Comm()
    res = pl.pallas_call(
        _carry(body, comm, 9, 4, (2, nc)), name=name, grid=(2, nc),
        out_shape=[pltpu.HBM((2, tt, cd), ACT), pltpu.HBM((2, tt, h), F32),
                   pltpu.HBM((2, 1, h), F32), pltpu.HBM((2, 1, h), F32)] + cm.out_shape,
        in_specs=[pl.BlockSpec((q, di), lambda d, i: (cidx(d, i), 0)),
                  pl.BlockSpec((q, gn), lambda d, i: (cidx(d, i), di // gn)),
                  pl.BlockSpec((q, gn), lambda d, i: (cidx(d, i), di // gn + 1)),
                  pl.BlockSpec((None, q, h), lambda d, i: (d, cidx(d, i), 0)),
                  pl.BlockSpec((None, 1, h), lambda d, i: (d, 0, 0)),
                  pl.BlockSpec((None, 1, h), lambda d, i: (d, 0, 0)),
                  pl.BlockSpec((q, di), lambda d, i: (cidx(d, i), 0)),
                  pl.BlockSpec((None, None, n, di), lambda d, i: (d, cidx(d, i), 0, 0)),
                  pl.BlockSpec((1, di), lambda d, i: (0, 0))] + cm.specs,
        out_specs=[pl.BlockSpec((None, q, cd), lambda d, i: (d, cidx(d, i), 0)),
                   pl.BlockSpec((None, q, h), lambda d, i: (d, cidx(d, i), 0)),
                   pl.BlockSpec((None, 1, h), lambda d, i: (d, 0, 0)),
                   pl.BlockSpec((None, 1, h), lambda d, i: (d, 0, 0))] + cm.specs,
        scratch_shapes=[pltpu.VMEM((n, di), F32), pltpu.VMEM((q, di), F32), pltpu.VMEM((q, di), F32)]
        + (cm.scratch if comm is not None else []),
        compiler_params=pltpu.CompilerParams(dimension_semantics=("arbitrary", "arbitrary"), vmem_limit_bytes=_vmem(24 * q * di * 4)),
    )(*[_hbm(v) for v in (xbc, xbc, xbc, dt2, bias2, alog2, dy, hp2, dskip_e)], *cm.operands)
    return res[0], res[1], res[2], res[3], cm.split(res[4:])


def _ssd_gate_fwd(name, y2, xbc, zx, dskip_e, norm_w, di, nct, t):
    def fn(rows, bcs, i):
        yf, yb, xs, z = rows
        zg = (yf + yb + bcs[0] * xs) * _silu(z)
        rn = lax.rsqrt(jnp.mean(zg * zg, axis=-1, keepdims=True) + EPS)
        return [zg * rn * bcs[1]], []

    ins = [_ri(y2, lead=0, ro=nct), _ri(y2, lead=1, ro=nct), _ri(xbc, di, 0, ro=nct), _ri(zx, di, 0, ro=nct)]
    return _rowwise(name, fn, t, ins, [dskip_e, norm_w], [(di, MXU)])[0][0]


def _ssd_gate_bwd(name, dyn, y2, xbc, zx, dskip_e, norm_w, di, nct, tt):
    def fn(rows, bcs, i):
        dn, yf, yb, xs, z = rows
        lat = i >= nct
        ytot = yf + yb + bcs[0] * xs
        sz = _silu(z)
        zg = ytot * sz
        rn = lax.rsqrt(jnp.mean(zg * zg, axis=-1, keepdims=True) + EPS)
        u = dn * bcs[1]
        dzg = rn * u - zg * (rn * rn * rn) * jnp.mean(u * zg, axis=-1, keepdims=True)
        dy = jnp.where(lat, dzg * sz, 0.0)
        dz = jnp.where(lat, dzg * ytot * _dsilu(z), 0.0)
        return [dy, dz], [jnp.where(lat, _colsum(dn * zg * rn), 0.0), jnp.where(lat, _colsum(dy * xs), 0.0)]

    ins = [_ri(dyn, ro=-nct), _ri(y2, lead=0), _ri(y2, lead=1), _ri(xbc, di, 0), _ri(zx, di, 0)]
    (dy, dzx), (dnw, ddsk) = _rowwise(name, fn, tt, ins, [dskip_e, norm_w], [(di, F32), (di, MXU, zx.shape[1], 0)], [(1, di)] * 2)
    return dy, dzx, dnw, ddsk


def _ada_fwd(name, cs, w, b):
    nl, d, c = w.shape
    r = cs.shape[0]

    def body(cs_ref, w_ref, b_ref, o_ref):
        o_ref[...] = _dot(_silu(cs_ref[...]), w_ref[...], NN) + b_ref[...]

    return pl.pallas_call(
        body, name=name, grid=(nl,), out_shape=pltpu.HBM((nl, r, c), F32),
        in_specs=[pl.BlockSpec((r, d), lambda l: (0, 0)), pl.BlockSpec((None, d, c), lambda l: (l, 0, 0)),
                  pl.BlockSpec((None, 1, c), lambda l: (l, 0, 0))],
        out_specs=pl.BlockSpec((None, r, c), lambda l: (l, 0, 0)),
        compiler_params=pltpu.CompilerParams(dimension_semantics=("parallel",), vmem_limit_bytes=_vmem(2 * d * c * 4)),
    )(_hbm(cs), _hbm(w), _hbm(b))


def _ada_bwd(name, cs, w, dmod):
    nl, d, c = w.shape
    r = cs.shape[0]

    def body(cs_ref, w_ref, dm_ref, dw_ref, dsc_ref):
        dm = dm_ref[...]
        dw_ref[...] = _dot(_silu(cs_ref[...]), dm, TN)

        @pl.when(pl.program_id(0) == 0)
        def _():
            dctx = jnp.broadcast_to(_colsum(dm[r // 2:]), (8, c))
            dsc_ref[...] = _dot(dctx, w_ref[...], NT)[0:1]

    return pl.pallas_call(
        body, name=name, grid=(nl,), out_shape=[pltpu.HBM((nl, d, c), F32), pltpu.HBM((1, d), F32)],
        in_specs=[pl.BlockSpec((r, d), lambda l: (0, 0)), pl.BlockSpec((None, d, c), lambda l: (l, 0, 0)),
                  pl.BlockSpec((None, r, c), lambda l: (l, 0, 0))],
        out_specs=[pl.BlockSpec((None, d, c), lambda l: (l, 0, 0)), pl.BlockSpec((1, d), lambda l: (0, 0))],
        compiler_params=pltpu.CompilerParams(dimension_semantics=("arbitrary",), vmem_limit_bytes=_vmem(4 * d * c * 4)),
    )(_hbm(cs), _hbm(w), _hbm(dmod))


def _adam_math(w, g, m, v):
    m = ADAM_B1 * m + (1.0 - ADAM_B1) * g
    v = ADAM_B2 * v + (1.0 - ADAM_B2) * (g * g)
    m_hat = m / (1.0 - ADAM_B1 ** ADAM_STEP)
    v_hat = v / (1.0 - ADAM_B2 ** ADAM_STEP)
    delta = -ADAM_LR * (m_hat / (jnp.sqrt(v_hat) + ADAM_EPS) + ADAM_WD * w)
    return delta, m, v


def _adam(name, slots, w, m, v, comm=None):
    ns, r, c = slots.shape
    tr = _pick(r, (256, 128, 64, 32, 16, 8))

    def body(s_ref, w_ref, m_ref, v_ref, g_ref, d_ref, mo_ref, vo_ref):
        g = s_ref[0].astype(F32)
        for k in range(1, ns):
            g = g + s_ref[k].astype(F32)
        d, mn, vn = _adam_math(w_ref[...], g, m_ref[...], v_ref[...])
        g_ref[...], d_ref[...], mo_ref[...], vo_ref[...] = g, d, mn, vn

    blk = pl.BlockSpec((tr, c), lambda i: (i, 0))
    cm = comm if comm is not None else _Comm()
    res = pl.pallas_call(
        _carry(body, comm, 4, 4, (r // tr,)), name=name, grid=(r // tr,), out_shape=[pltpu.HBM((r, c), F32)] * 4 + cm.out_shape,
        in_specs=[pl.BlockSpec((ns, tr, c), lambda i: (0, i, 0)), blk, blk, blk] + cm.specs, out_specs=[blk] * 4 + cm.specs,
        scratch_shapes=cm.scratch if comm is not None else [],
        compiler_params=pltpu.CompilerParams(dimension_semantics=("arbitrary",), vmem_limit_bytes=_vmem(16 * tr * c * 4)),
    )(_hbm(slots), _hbm(w), _hbm(m), _hbm(v), *cm.operands)
    return res[:4] if comm is None else (res[:4], cm.split(res[4:]))


def _adam_small(name, slots, ws, ms, vs, scale=None):
    k = len(slots)

    def body(*refs):
        s_refs, w_refs, m_refs, v_refs = refs[:k], refs[k:2 * k], refs[2 * k:3 * k], refs[3 * k:4 * k]
        sc_ref = refs[4 * k] if scale is not None else None
        outs = refs[4 * k + (scale is not None):]
        for a in range(k):
            g = s_refs[a][0]
            for j in range(1, NDEV):
                g = g + s_refs[a][j]
            if scale is not None and a == scale[0]:
                g = g * _dsilu(sc_ref[...])
            d, mn, vn = _adam_math(w_refs[a][...], g, m_refs[a][...], v_refs[a][...])
            outs[a][...], outs[k + a][...], outs[2 * k + a][...], outs[3 * k + a][...] = g, d, mn, vn

    shapes = [pltpu.HBM(w.shape, F32) for w in ws]
    extra = [scale[1]] if scale is not None else []
    ins = [*slots, *ws, *ms, *vs, *extra]

    def whole(shape):
        return pl.BlockSpec(shape, lambda i, nd=len(shape): (0,) * nd)

    res = pl.pallas_call(body, name=name, grid=(1,), out_shape=shapes * 4, in_specs=[whole(v.shape) for v in ins],
                         out_specs=[whole(s.shape) for s in shapes * 4])(*[_hbm(v) for v in ins])
    return res[:k], res[k:2 * k], res[2 * k:3 * k], res[3 * k:]


def _unshard_cols(g):
    g = jnp.moveaxis(g, 0, -2)
    return g.reshape(g.shape[:-2] + (g.shape[-2] * g.shape[-1],))


def _shard_cols(a):
    a = a.reshape(a.shape[:-1] + (NDEV, a.shape[-1] // NDEV))
    return jnp.moveaxis(a, -2, 0)


def _unshard_rows(g):
    g = jnp.moveaxis(g, 0, -3)
    return g.reshape(g.shape[:-3] + (g.shape[-3] * g.shape[-2], g.shape[-1]))


def _shard_rows(a):
    a = a.reshape(a.shape[:-2] + (NDEV, a.shape[-2] // NDEV, a.shape[-1]))
    return jnp.moveaxis(a, -3, 0)


def _flat2(a):
    return a.reshape((-1, a.shape[-1]))


def kernel(x, c, ctx, c_ctx, ada_w, ada_b, norm_mix_g, norm_ffn_g, final_norm_g, ssd_w_in, ssd_conv_w, ssd_conv_b, ssd_dt_bias_f, ssd_dt_bias_b, ssd_a_log_f, ssd_a_log_b, ssd_d_skip, ssd_norm_w, ssd_w_out, conf_w_pw1, conf_b_pw1, conf_dw_w, conf_dw_b, conf_ln_g, conf_ln_b, conf_w_pw2, conf_b_pw2, ffn_w_in, ffn_w_out, loss_target, m_c_ctx, m_ada_w, m_ada_b, m_norm_mix_g, m_norm_ffn_g, m_final_norm_g, m_ssd_w_in, m_ssd_conv_w, m_ssd_conv_b, m_ssd_dt_bias_f, m_ssd_dt_bias_b, m_ssd_a_log_f, m_ssd_a_log_b, m_ssd_d_skip, m_ssd_norm_w, m_ssd_w_out, m_conf_w_pw1, m_conf_b_pw1, m_conf_dw_w, m_conf_dw_b, m_conf_ln_g, m_conf_ln_b, m_conf_w_pw2, m_conf_b_pw2, m_ffn_w_in, m_ffn_w_out, v_c_ctx, v_ada_w, v_ada_b, v_norm_mix_g, v_norm_ffn_g, v_final_norm_g, v_ssd_w_in, v_ssd_conv_w, v_ssd_conv_b, v_ssd_dt_bias_f, v_ssd_dt_bias_b, v_ssd_a_log_f, v_ssd_a_log_b, v_ssd_d_skip, v_ssd_norm_w, v_ssd_w_out, v_conf_w_pw1, v_conf_b_pw1, v_conf_dw_w, v_conf_dw_b, v_conf_ln_g, v_conf_ln_b, v_conf_w_pw2, v_conf_b_pw2, v_ffn_w_in, v_ffn_w_out):
    args = dict(locals())
    names = ['c_ctx', 'ada_w', 'ada_b', 'norm_mix_g', 'norm_ffn_g', 'final_norm_g', 'ssd_w_in', 'ssd_conv_w', 'ssd_conv_b',
             'ssd_dt_bias_f', 'ssd_dt_bias_b', 'ssd_a_log_f', 'ssd_a_log_b', 'ssd_d_skip', 'ssd_norm_w', 'ssd_w_out',
             'conf_w_pw1', 'conf_b_pw1', 'conf_dw_w', 'conf_dw_b', 'conf_ln_g', 'conf_ln_b', 'conf_w_pw2', 'conf_b_pw2',
             'ffn_w_in', 'ffn_w_out']
    me = 4 * lax.axis_index("x") + 2 * lax.axis_index("y") + lax.axis_index("c")
    t, d = x.shape[1], x.shape[2]
    tc = ctx.shape[1]
    tt = tc + t
    nct = tc // ROW_TILE
    assert tc % ROW_TILE == 0 and t % ROW_TILE == 0
    h = ssd_dt_bias_f.shape[-1]
    di = ssd_norm_w.shape[-1]
    cdim = ssd_conv_b.shape[-1]
    kc = ssd_conv_w.shape[1]
    ck = conf_dw_w.shape[1]
    ch = d // 2
    rows_g = t // GRID_W
    nl = ada_w.shape[0]
    cw = ada_w.shape[2]
    x2, ctx2, tgt = x[0], ctx[0], loss_target[0]

    (c_all, w_in_g, convw_g), _ = _exchange("gather_first", [c, ssd_w_in[0].astype(WIRE), ssd_conv_w[0]])
    ride_proj = _Comm(gather=[ssd_w_out[0].astype(WIRE), conf_w_pw2[0].astype(WIRE)])
    ride_conv = _Comm(gather=[conf_w_pw1[0].astype(WIRE), conf_b_pw1, conf_dw_w[0], conf_dw_b, conf_ln_g, conf_ln_b, conf_b_pw2])
    ride_scan = _Comm(gather=[ffn_w_in.astype(WIRE), ffn_w_out.astype(WIRE)])
    w_ssd_in = _unshard_cols(w_in_g)
    w_zx = w_ssd_in[:, :di + cdim]
    w_dt = jnp.pad(w_ssd_in[:, di + cdim:], ((0, 0), (0, LANES - 2 * h)))
    conv_w_full = _unshard_cols(convw_g)

    cs_all = jnp.concatenate([c_all[:, 0, :], jnp.broadcast_to(c_ctx[None, :], (NDEV, d))], axis=0)
    ada_b_mine = lax.dynamic_slice_in_dim(ada_b, me * cw, cw, axis=1)[:, None, :]
    mod_part = _ada_fwd("ada_fwd", cs_all, ada_w, ada_b_mine)
    (mod_g,), _ = _exchange("gather_mod", [mod_part])
    mod_all = jnp.moveaxis(mod_g, 0, 2).reshape(nl, 2 * NDEV, NDEV * cw)
    mod_lat = lax.dynamic_slice_in_dim(mod_all, me, 1, axis=1)[:, 0, :]
    mod_ctx = mod_all[0, NDEV, :]

    def six(v):
        return [v[k * d:(k + 1) * d][None, :] for k in range(6)]

    sh1, s1, g1, sh2, s2, g2 = six(mod_lat[0])
    csh1, cs1 = six(mod_ctx)[:2]
    sh1b, s1b, g1b, sh2b, s2b, g2b = six(mod_lat[1])
    nmg, nfg = norm_mix_g, norm_ffn_g

    h_all = jnp.concatenate([ctx2, x2], axis=0)
    s01, sh01 = jnp.concatenate([cs1, s1], axis=0), jnp.concatenate([csh1, sh1], axis=0)
    xn_all = _normmod_fwd("l0_norm", h_all, nmg[0:1], s01, sh01, nct)
    zx, ((w_out_g, pw2_g), _) = _mm("ssd_in_proj", xn_all, w_zx, "nn", ACT, comm=ride_proj)
    dtr = _mm("ssd_dt_proj", xn_all, w_dt, "nn")
    dt2 = jnp.moveaxis(dtr[:, :2 * h].reshape(tt, 2, h), 1, 0)
    bias2 = jnp.stack([ssd_dt_bias_f, ssd_dt_bias_b])
    alog2 = jnp.stack([ssd_a_log_f, ssd_a_log_b])
    xbc, ((pw1_g, bpw1_g, dww_g, dwb_g, lng_g, lnb_g, bpw2_g), _) = _ssd_conv_fwd("ssd_conv", zx, conv_w_full, ssd_conv_b, di, tc,
                                                                                 comm=ride_conv)
    y2, hp2, ((fin_g, fout_g), _) = _ssd_fwd("ssd_scan", xbc, dt2, bias2, alog2, di, tc, comm=ride_scan)
    w_ssd_out = _unshard_rows(w_out_g)
    w_pw1, w_pw2 = _unshard_cols(pw1_g), _unshard_rows(pw2_g)
    w_fin, w_fout = _unshard_cols(fin_g), _unshard_rows(fout_g)
    dw_w_full = _unshard_cols(dww_g)
    b_pw1, dw_b, ln_g, ln_b, b_pw2 = (_unshard_cols(a) for a in (bpw1_g, dwb_g, lng_g, lnb_g, bpw2_g))
    dskip_e = jnp.repeat(ssd_d_skip, di // h, axis=1)
    yn = _ssd_gate_fwd("ssd_gate", y2, xbc, zx, dskip_e, ssd_norm_w, di, nct, t)
    mix0 = _mm("ssd_out_proj", yn, w_ssd_out, "nn")
    h1, xf0 = _resnorm_fwd("l0_res_norm", x2, mix0, g1, nfg[0:1], s2, sh2)
    u0 = _mm("ffn0_in", xf0, w_fin[0], "nn", ACT)
    hid0 = _swiglu_fwd("ffn0_act", u0)
    f0 = _mm("ffn0_out", hid0, w_fout[0], "nn")
    h2, xn1 = _resnorm_fwd("l1_norm", h1, f0, g2, nmg[1:2], s1b, sh1b)
    u1 = _mm("conf_pw1", xn1, w_pw1, "nn", ACT, bias=b_pw1)
    gl = _glu_fwd("conf_glu", u1)
    gl_h = _grid_t(gl[:, :ch], rows_g, GRID_W)
    gl_v = gl[:, ch:]
    v_h = _strided_conv("conf_conv_h", gl_h, dw_w_full[:, :ch], dw_b[:, :ch], rows_g)
    v_v = _strided_conv("conf_conv_v", gl_v, dw_w_full[:, ch:], dw_b[:, ch:], GRID_W)
    v = jnp.concatenate([_grid_t(v_h, GRID_W, rows_g), v_v], axis=1)
    sl = _ln_silu_fwd("conf_ln", v, ln_g, ln_b)
    mix1 = _mm("conf_pw2", sl, w_pw2, "nn", bias=b_pw2)
    h3, xf1 = _resnorm_fwd("l1_res_norm", h2, mix1, g1b, nfg[1:2], s2b, sh2b)
    u2 = _mm("ffn1_in", xf1, w_fin[1], "nn", ACT)
    hid1 = _swiglu_fwd("ffn1_act", u2)
    f1 = _mm("ffn1_out", hid1, w_fout[1], "nn")
    dh, sq, d_final_g = _final_loss("final_loss", h3, f1, tgt, g2b, final_norm_g[None, :])
    loss = lax.psum(0.5 * sq[0, 0] / d, AXES)

    zero2 = jnp.zeros((2, d), F32)

    def ffn_bwd(tag, dh, hin, xf, u, hid, f, gate, w_in, w_out, g_norm, s_mod):
        df, dgate, _ = _gate_bwd(tag + "_gate_bwd", dh, f, gate)
        dhid = _mm(tag + "_dhid", df, w_out, "nt", ACT)
        dw_out = _mm(tag + "_dwout", hid, df, "tn", WIRE)
        du = _swiglu_bwd(tag + "_act_bwd", u, dhid)
        dw_in = _mm(tag + "_dwin", xf, du, "tn", WIRE)
        dxf = _mm(tag + "_dx", du, w_in, "nt")
        s_2 = jnp.concatenate([s_mod, s_mod], axis=0)
        dh, dsh, ds, dg = _normmod_bwd(tag + "_norm_bwd", hin, dxf, dh, g_norm, s_2)
        return dh, dgate, dsh[1:2], ds[1:2], dg[1:2], dw_in, dw_out

    dh, d_g2b, d_sh2b, d_s2b, d_nfg1, g_fin1, g_fout1 = ffn_bwd("ffn1", dh, h3, xf1, u2, hid1, f1, g2b, w_fin[1], w_fout[1], nfg[1:2], s2b)
    dmix1, d_g1b, g_bpw2 = _gate_bwd("conf_gate_bwd", dh, mix1, g1b)
    dsl = _mm("conf_dsl", dmix1, w_pw2, "nt")
    g_pw2 = _mm("conf_dwpw2", sl, dmix1, "tn", WIRE)
    dv, g_lng, g_lnb = _ln_silu_bwd("conf_ln_bwd", v, dsl, ln_g, ln_b)
    dv_h, dv_v = _grid_t(dv[:, :ch], rows_g, GRID_W), dv[:, ch:]
    w_flip = dw_w_full[::-1]
    dgl_h = _strided_conv("conf_conv_h_bwd", dv_h, w_flip[:, :ch], None, rows_g)
    dgl_v = _strided_conv("conf_conv_v_bwd", dv_v, w_flip[:, ch:], None, GRID_W)
    g_dww_h, g_dwb_h = _strided_conv_dw("conf_conv_h_dw", gl_h, dv_h, ck, rows_g)
    g_dww_v, g_dwb_v = _strided_conv_dw("conf_conv_v_dw", gl_v, dv_v, ck, GRID_W)
    g_dww, g_dwb = jnp.concatenate([g_dww_h, g_dww_v], axis=1), jnp.concatenate([g_dwb_h, g_dwb_v], axis=1)
    dgl = jnp.concatenate([_grid_t(dgl_h, GRID_W, rows_g), dgl_v], axis=1)
    du1, g_bpw1 = _glu_bwd("conf_glu_bwd", u1, dgl)
    g_pw1 = _mm("conf_dwpw1", xn1, du1, "tn", WIRE)
    dxn1 = _mm("conf_dx", du1, w_pw1, "nt")
    dh, dsh_, ds_, dg_ = _normmod_bwd("l1_norm_bwd", h2, dxn1, dh, nmg[1:2], jnp.concatenate([s1b, s1b], axis=0))
    d_sh1b, d_s1b, d_nmg1 = dsh_[1:2], ds_[1:2], dg_[1:2]
    dh, d_g2, d_sh2, d_s2, d_nfg0, g_fin0, g_fout0 = ffn_bwd("ffn0", dh, h1, xf0, u0, hid0, f0, g2, w_fin[0], w_fout[0], nfg[0:1], s2)
    dmix0, d_g1, _ = _gate_bwd("ssd_gate_res_bwd", dh, mix0, g1)
    dyn = _mm("ssd_dyn", dmix0, w_ssd_out, "nt")
    g_ssd_out = _mm("ssd_dwout", yn, dmix0, "tn", WIRE)
    dy, dzx, g_normw, ddsk_e = _ssd_gate_bwd("ssd_gate_bwd", dyn, y2, xbc, zx, dskip_e, ssd_norm_w, di, nct, tt)
    ride_scan_bwd = _Comm(scatter=[_flat3(_shard_cols(jnp.stack([g_fin0, g_fin1]))), _flat3(_shard_rows(jnp.stack([g_fout0, g_fout1])))])
    ride_conv_bwd = _Comm(scatter=[_shard_rows(g_ssd_out), _shard_cols(g_pw1), _shard_rows(g_pw2), _shard_cols(g_bpw1),
                                   _shard_cols(g_dww), _shard_cols(g_dwb), _shard_cols(g_lng), _shard_cols(g_lnb), _shard_cols(g_bpw2)])
    dxbc2, ddt2, g_alog2, g_bias2, (_, ffn_r) = _ssd_bwd("ssd_scan_bwd", xbc, dt2, bias2, alog2, dy, hp2, dskip_e, di, tc,
                                                         comm=ride_scan_bwd)
    dzx, g_convw, g_convb, (_, conv_r) = _ssd_conv_bwd("ssd_conv_bwd", zx, dxbc2, conv_w_full, ssd_conv_b, dzx, di, tc,
                                                       comm=ride_conv_bwd)
    ddt_p = jnp.pad(jnp.moveaxis(ddt2, 0, 1).reshape(tt, 2 * h), ((0, 0), (0, LANES - 2 * h))).astype(MXU)
    g_ssd_in = jnp.concatenate([_mm("ssd_dw_zx", xn_all, dzx, "tn", WIRE),
                                _mm("ssd_dw_dt", xn_all, ddt_p, "tn", WIRE)[:, :2 * h]], axis=1)
    dxn, (_, (ssd_in_r, convw_r)) = _mm("ssd_dx_zx", dzx, w_zx, "nt",
                                        comm=_Comm(scatter=[_shard_cols(g_ssd_in), _shard_cols(g_convw)]))
    dxn = _mm("ssd_dx_dt", ddt_p, w_dt, "nt", add=dxn)
    dh_all, dsh_, ds_, dg_ = _normmod_bwd("l0_norm_bwd", h_all, dxn, dh, nmg[0:1], s01, nct)
    grad_x = dh_all[tc:][None]
    d_csh1, d_sh1, d_cs1, d_s1 = dsh_[0:1], dsh_[1:2], ds_[0:1], ds_[1:2]
    d_nmg0 = dg_[0:1] + dg_[1:2]

    z1 = jnp.zeros((1, d), F32)
    dmod = jnp.concatenate([jnp.concatenate([d_sh1, d_s1, d_g1, d_sh2, d_s2, d_g2], axis=1),
                            jnp.concatenate([d_sh1b, d_s1b, d_g1b, d_sh2b, d_s2b, d_g2b], axis=1),
                            jnp.concatenate([d_csh1, d_cs1, z1, z1, z1, z1], axis=1)], axis=0)
    out = {}

    def put(name, res):
        w = args[name]
        out["grad_" + name], out["delta_" + name], out["new_m_" + name], out["new_v_" + name] = (r.reshape(w.shape) for r in res)

    def adam_big(name, slots, comm=None):
        return _adam("adam_" + name, slots, _flat2(args[name]), _flat2(args["m_" + name]), _flat2(args["v_" + name]), comm=comm)

    res, ((dmod_g,), _) = adam_big("ffn_w_in", ffn_r[0], comm=_Comm(gather=[dmod]))
    put("ffn_w_in", res)
    dmod_mine = lax.dynamic_slice_in_dim(dmod_g, me * cw, cw, axis=2)
    dmod16 = jnp.stack([jnp.concatenate([dmod_mine[:, 0], dmod_mine[:, 2]], axis=0),
                        jnp.concatenate([dmod_mine[:, 1], jnp.zeros((NDEV, cw), F32)], axis=0)])
    g_ada_w, dsc_part = _ada_bwd("ada_bwd", cs_all, ada_w, dmod16)
    g_ada_b = dmod[0:2] + jnp.concatenate([dmod[2:3], jnp.zeros((1, 6 * d), F32)], axis=0)

    d_dskip = jnp.sum(ddsk_e.reshape(h, di // h), axis=1)[None, :]
    rep = [dsc_part, g_ada_b, jnp.concatenate([d_nmg0, d_nmg1], axis=0), jnp.concatenate([d_nfg0, d_nfg1], axis=0),
           d_final_g, g_convb, g_bias2[0], g_bias2[1], g_alog2[0], g_alog2[1], d_dskip, g_normw]
    res, (rep_g, _) = _adam("adam_ada_w", _flat2(g_ada_w)[None], _flat2(ada_w), _flat2(m_ada_w), _flat2(v_ada_w),
                            comm=_Comm(gather=rep))
    put("ada_w", res)
    small_r = [convw_r] + list(conv_r[3:])

    for name, slots in zip(["ssd_w_in", "ssd_w_out", "conf_w_pw1", "conf_w_pw2", "ffn_w_out"],
                           [ssd_in_r, conv_r[0], conv_r[1], conv_r[2], ffn_r[1]]):
        put(name, adam_big(name, slots))
    small_names = ["ssd_conv_w", "conf_b_pw1", "conf_dw_w", "conf_dw_b", "conf_ln_g", "conf_ln_b", "conf_b_pw2",
                   "c_ctx", "ada_b", "norm_mix_g", "norm_ffn_g", "final_norm_g", "ssd_conv_b", "ssd_dt_bias_f", "ssd_dt_bias_b",
                   "ssd_a_log_f", "ssd_a_log_b", "ssd_d_skip", "ssd_norm_w"]
    slots = list(small_r) + list(rep_g)

    def as2(a):
        return a.reshape((1, -1)) if a.ndim == 1 else _flat2(a)

    res = _adam_small("adam_small", slots, [as2(args[n]) for n in small_names], [as2(args["m_" + n]) for n in small_names],
                      [as2(args["v_" + n]) for n in small_names], scale=(small_names.index("c_ctx"), c_ctx[None, :]))
    for k, name in enumerate(small_names):
        put(name, [r[k] for r in res])
    return (loss, grad_x, *[out["grad_" + n] for n in names], *[out["delta_" + n] for n in names],
            *[out["new_m_" + n] for n in names], *[out["new_v_" + n] for n in names])


def _flat3(a):
    return a.reshape((a.shape[0], -1, a.shape[-1]))
```

```python
import functools

import jax
import jax.numpy as jnp
from jax import lax
from jax.experimental import pallas as pl
from jax.experimental.pallas import tpu as pltpu

F32 = jnp.float32
MXU = jnp.bfloat16
WIRE = jnp.bfloat16
ACT = jnp.bfloat16
NDEV = 8
AXES = ("x", "y", "c")
SSD_STATE = 128
SSD_CHUNK = 128
GRID_W = 64
EPS = 1e-6
ROW_TILE = 256
LANES = 128
ADAM_LR, ADAM_B1, ADAM_B2, ADAM_EPS, ADAM_WD, ADAM_STEP = 0.001, 0.9, 0.999, 1e-08, 0.01, 10
VMEM_CAP = 56 * 2 ** 20
MESH_ID = pl.DeviceIdType.MESH


def _pick(dim, cands):
    for c in cands:
        if dim % c == 0:
            return c
    return dim


def _nbytes(shape, dtype):
    n = 1
    for s in shape:
        n *= s
    return n * jnp.dtype(dtype).itemsize


def _vmem(nbytes):
    return int(min(VMEM_CAP, max(24 * 2 ** 20, 2 * nbytes + 8 * 2 ** 20)))


def _sigmoid(x):
    return 1.0 / (1.0 + jnp.exp(-x))


def _silu(x):
    return x * _sigmoid(x)


def _dsilu(x):
    s = _sigmoid(x)
    return s * (1.0 + x * (1.0 - s))


def _softplus(x):
    return jnp.maximum(x, 0.0) + jnp.log(1.0 + jnp.exp(-jnp.abs(x)))


def _dot(a, b, dims):
    return lax.dot_general(a.astype(MXU), b.astype(MXU), (dims, ((), ())), preferred_element_type=F32)


NN, NT, TN = ((1,), (0,)), ((1,), (1,)), ((0,), (0,))


def _split(a, parts):
    out = []
    for _ in range(parts):
        p = a.astype(MXU)
        out.append(p)
        a = a - p.astype(F32)
    return out


def _dot_lx(e, a, dims, parts=2):
    return sum(lax.dot_general(e, p, (dims, ((), ())), preferred_element_type=F32) for p in _split(a, parts))


def _dot_rx(a, e, dims, parts=2):
    return sum(lax.dot_general(p, e, (dims, ((), ())), preferred_element_type=F32) for p in _split(a, parts))


class _Comm:
    def __init__(self, gather=(), scatter=()):
        self.gather, self.scatter = list(gather), list(scatter)
        self.ng, self.n = len(self.gather), len(self.gather) + len(self.scatter)
        self.operands = self.gather + self.scatter
        self.specs = [pl.BlockSpec(memory_space=pl.ANY)] * self.n
        self.out_shape = ([jax.ShapeDtypeStruct((NDEV,) + a.shape, a.dtype) for a in self.gather]
                          + [jax.ShapeDtypeStruct(a.shape, a.dtype) for a in self.scatter])
        self.scratch = [pltpu.SemaphoreType.DMA((self.n, 7)), pltpu.SemaphoreType.DMA((self.n, 7)),
                        pltpu.SemaphoreType.DMA((self.n,))]

    def split(self, res):
        return res[:self.ng], res[self.ng:]

    def _copies(self, ins, outs, sems):
        send, recv, loc = sems
        ng, n = self.ng, self.n
        x, y, c = lax.axis_index("x"), lax.axis_index("y"), lax.axis_index("c")
        me, sib = (x, y, c), (x, y, 1 - c)
        chips = [(1 - x, y), (x, 1 - y), (1 - x, 1 - y)]

        def slot(p):
            return 4 * p[0] + 2 * p[1] + p[2]

        def rcopy(a, k, src, dst, to):
            return functools.partial(pltpu.make_async_remote_copy, src_ref=src, dst_ref=dst, send_sem=send.at[a, k],
                                     recv_sem=recv.at[a, k], device_id=to, device_id_type=MESH_ID)

        local = [functools.partial(pltpu.make_async_copy, ins[a] if a < ng else ins[a].at[slot(me)], outs[a].at[slot(me)],
                                   loc.at[a]) for a in range(n)]
        rel = [(fx, fy, fc) for fx in (0, 1) for fy in (0, 1) for fc in (0, 1)][1:]
        first, landed, passed = [], [], []
        for a in range(ng, n):
            for k, (fx, fy, fc) in enumerate(rel):
                p = (1 - x if fx else x, 1 - y if fy else y, 1 - c if fc else c)
                first.append(rcopy(a, k, ins[a].at[slot(p)], outs[a].at[slot(me)], p))
                blk = outs[a].at[slot(p)]
                landed.append(rcopy(a, k, blk, blk, me))
        for a in range(ng):
            dst = outs[a].at[slot(me)]
            first.append(rcopy(a, 0, ins[a], dst, sib))
            first += [rcopy(a, 1 + j, ins[a], dst, (*ch, c)) for j, ch in enumerate(chips)]
            blk = outs[a].at[slot(sib)]
            landed.append(rcopy(a, 0, blk, blk, me))
            for j, ch in enumerate(chips):
                blk = outs[a].at[slot((*ch, c))]
                passed.append((rcopy(a, 1 + j, blk, blk, me), rcopy(a, 4 + j, blk, blk, sib)))
                blk = outs[a].at[slot((*ch, 1 - c))]
                landed.append(rcopy(a, 4 + j, blk, blk, me))
        return local, first, passed, landed

    def start(self, ins, outs, sems):
        local, first, _, _ = self._copies(ins, outs, sems)
        for make in local + first:
            make().start()

    def finish(self, ins, outs, sems):
        local, first, passed, landed = self._copies(ins, outs, sems)
        onward = []
        for arrived, forward in passed:
            arrived().wait_recv()
            onward.append(forward())
            onward[-1].start()
        for make in landed:
            make().wait_recv()
        for make in first:
            make().wait_send()
        for cp in onward:
            cp.wait_send()
        for make in local:
            make().wait()


def _carry(body, comm, n_in, n_out, grid):
    if comm is None:
        return body
    n = comm.n

    def wrapped(*refs):
        own_in, c_in = refs[:n_in], refs[n_in:n_in + n]
        own_out, c_out = refs[n_in + n:n_in + n + n_out], refs[n_in + n + n_out:n_in + 2 * n + n_out]
        own_scr, sems = refs[n_in + 2 * n + n_out:-3], refs[-3:]
        ids = [pl.program_id(ax) for ax in range(len(grid))]
        first, last = ids[0] == 0, ids[0] == grid[0] - 1
        for ax in range(1, len(grid)):
            first, last = first & (ids[ax] == 0), last & (ids[ax] == grid[ax] - 1)

        @pl.when(first)
        def _():
            comm.start(c_in, c_out, sems)

        body(*own_in, *own_out, *own_scr)

        @pl.when(last)
        def _():
            comm.finish(c_in, c_out, sems)

    return wrapped


def _exchange(name, gather, scatter=()):
    comm = _Comm(gather, scatter)
    n = comm.n

    def body(*refs):
        ins, outs, sems = refs[:n], refs[n:2 * n], refs[2 * n:]
        comm.start(ins, outs, sems)
        comm.finish(ins, outs, sems)

    res = pl.pallas_call(body, name=name, out_shape=comm.out_shape, in_specs=comm.specs, out_specs=comm.specs,
                         scratch_shapes=comm.scratch)(*comm.operands)
    return comm.split(res)


def _hbm(a):
    return pltpu.with_memory_space_constraint(a, pltpu.HBM)


def _divs(dim, mult):
    return [dim] + [dim // parts for parts in range(2, dim // mult + 1) if dim % parts == 0 and (dim // parts) % mult == 0]


MM_VMEM_BUDGET = 40 * 2 ** 20
GRID_STEP_US = 0.35
HBM_BYTES_PER_US = 3.0e6


def _mm_tiles(m, n, k, sizes, mode, has_add):
    sa, sb, so = sizes
    sub = 16
    best = None
    for tk in _divs(k, LANES):
        for tn in _divs(n, LANES):
            for tm in _divs(m, LANES if mode == "tn" else sub):
                nk = k // tk
                out_t = tm * tn
                est = (2 * (tm * tk * sa + tk * tn * sb) + 2 * out_t * so + 2 * (tm * tk + tk * tn) + 4 * out_t
                       + (4 * out_t if nk > 1 else 0) + (8 * out_t if has_add else 0))
                if est > MM_VMEM_BUDGET:
                    continue
                steps = (m // tm) * (n // tn) * nk
                cost = steps * GRID_STEP_US + (tm * tk * sa + tk * tn * sb + out_t * so) / HBM_BYTES_PER_US
                if best is None or cost < best[0]:
                    best = (cost, tm, tn, tk, est)
    assert best is not None, (m, n, k)
    return best[1:]


def _mm(name, a, b, mode, out_dtype=F32, bias=None, add=None, comm=None):
    if mode == "nn":
        (m, k), (k2, n) = a.shape, b.shape
    elif mode == "nt":
        (m, k), (n, k2) = a.shape, b.shape
    else:
        (k, m), (k2, n) = a.shape, b.shape
    assert k == k2, (name, a.shape, b.shape)
    sizes = (a.dtype.itemsize, b.dtype.itemsize, jnp.dtype(out_dtype).itemsize)
    tm, tn, tk, est = _mm_tiles(m, n, k, sizes, mode, add is not None)
    nk = k // tk
    dims = {"nn": NN, "nt": NT, "tn": TN}[mode]
    a_spec = pl.BlockSpec((tk, tm), lambda i, j, kk: (kk, i)) if mode == "tn" else pl.BlockSpec((tm, tk), lambda i, j, kk: (i, kk))
    b_spec = pl.BlockSpec((tn, tk), lambda i, j, kk: (j, kk)) if mode == "nt" else pl.BlockSpec((tk, tn), lambda i, j, kk: (kk, j))
    extra, extra_specs = [], []
    if bias is not None:
        extra.append(bias)
        extra_specs.append(pl.BlockSpec((1, tn), lambda i, j, kk: (0, j)))
    if add is not None:
        extra.append(add)
        extra_specs.append(pl.BlockSpec((tm, tn), lambda i, j, kk: (i, j)))

    def finish(r, extras, o_ref):
        for e in extras:
            r = r + e[...].astype(F32)
        o_ref[...] = r.astype(o_ref.dtype)

    def body_acc(*refs):
        a_ref, b_ref = refs[:2]
        o_ref, acc = refs[-2:]
        kk = pl.program_id(2)

        @pl.when(kk == 0)
        def _():
            acc[...] = jnp.zeros_like(acc)

        acc[...] += _dot(a_ref[...], b_ref[...], dims)

        @pl.when(kk == nk - 1)
        def _():
            finish(acc[...], refs[2:-2], o_ref)

    def body_one(*refs):
        finish(_dot(refs[0][...], refs[1][...], dims), refs[2:-1], refs[-1])

    cm = comm if comm is not None else _Comm()
    grid = (m // tm, n // tn, nk)
    res = pl.pallas_call(
        _carry(body_acc if nk > 1 else body_one, comm, 2 + len(extra), 1, grid), name=name, grid=grid,
        out_shape=[pltpu.HBM((m, n), out_dtype)] + cm.out_shape,
        in_specs=[a_spec, b_spec] + extra_specs + cm.specs,
        out_specs=[pl.BlockSpec((tm, tn), lambda i, j, kk: (i, j))] + cm.specs,
        scratch_shapes=([pltpu.VMEM((tm, tn), F32)] if nk > 1 else []) + (cm.scratch if comm is not None else []),
        compiler_params=pltpu.CompilerParams(
            dimension_semantics=("parallel", "parallel", "arbitrary") if comm is None else ("arbitrary",) * 3,
            vmem_limit_bytes=int(min(VMEM_CAP, est + 12 * 2 ** 20))),
    )(*[_hbm(v) for v in (a, b, *extra)], *cm.operands)
    return res[0] if comm is None else (res[0], cm.split(res[1:]))


def _ri(arr, w=None, cb=0, ro=0, lead=None):
    return (arr, arr.shape[-1] if w is None else w, cb, ro, lead)


def _rowwise(name, fn, nrows, row_ins, bc_ins, outs, accs=(), comm=None):
    tr = min(ROW_TILE, nrows)
    assert nrows % tr == 0
    in_specs = []
    for (arr, w, cb, ro, lead) in row_ins:
        if lead is None:
            in_specs.append(pl.BlockSpec((tr, w), lambda i, cb=cb, ro=ro: (jnp.maximum(i + ro, 0), cb)))
        else:
            in_specs.append(pl.BlockSpec((None, tr, w), lambda i, cb=cb, ro=ro, lead=lead: (lead, jnp.maximum(i + ro, 0), cb)))
    for arr in bc_ins:
        in_specs.append(pl.BlockSpec(arr.shape, lambda i, nd=arr.ndim: (0,) * nd))
    outs = [o if len(o) == 4 else (o[0], o[1], o[0], 0) for o in outs]
    out_shape = [pltpu.HBM((nrows, total), dt) for _, dt, total, _ in outs] + [pltpu.HBM(s, F32) for s in accs]
    out_specs = ([pl.BlockSpec((tr, c), lambda i, cb=cb: (i, cb)) for c, _, _, cb in outs]
                 + [pl.BlockSpec(s, lambda i: (0, 0)) for s in accs])
    nr, nb, no = len(row_ins), len(bc_ins), len(outs)

    def body(*refs):
        i = pl.program_id(0)
        rows = [r[...].astype(F32) for r in refs[:nr]]
        bcs = [r[...] for r in refs[nr:nr + nb]]
        o, a = fn(rows, bcs, i)
        for ref, val in zip(refs[nr + nb:nr + nb + no], o):
            ref[...] = val.astype(ref.dtype)
        for ref, val in zip(refs[nr + nb + no:], a):
            @pl.when(i == 0)
            def _(ref=ref, val=val):
                ref[...] = val

            @pl.when(i > 0)
            def _(ref=ref, val=val):
                ref[...] += val

    est = sum(tr * w * arr.dtype.itemsize for (arr, w, _, _, _) in row_ins) + sum(tr * o[0] * 4 for o in outs)
    cm = comm if comm is not None else _Comm()
    nout = no + len(accs)
    res = pl.pallas_call(
        _carry(body, comm, nr + nb, nout, (nrows // tr,)), name=name, grid=(nrows // tr,), out_shape=out_shape + cm.out_shape,
        in_specs=in_specs + cm.specs, out_specs=out_specs + cm.specs, scratch_shapes=cm.scratch if comm is not None else [],
        compiler_params=pltpu.CompilerParams(dimension_semantics=("arbitrary",), vmem_limit_bytes=_vmem(3 * est)),
    )(*[_hbm(r[0]) for r in row_ins], *[_hbm(v) for v in bc_ins], *cm.operands)
    if comm is None:
        return res[:no], res[no:]
    return res[:no], res[no:nout], cm.split(res[nout:])


def _colsum(v):
    return jnp.sum(v, axis=0, keepdims=True)


def _normmod_fwd(name, h, g, s, sh, nct=0, comm=None):
    d = h.shape[1]

    def fn(rows, bcs, i):
        hh, (g_, s_, sh_) = rows[0], bcs
        s1 = jnp.where(i < nct, s_[0:1], s_[1:2])
        sh1 = jnp.where(i < nct, sh_[0:1], sh_[1:2])
        r = lax.rsqrt(jnp.mean(hh * hh, axis=-1, keepdims=True) + EPS)
        return [hh * r * g_ * (1.0 + s1) + sh1], []

    res = _rowwise(name, fn, h.shape[0], [_ri(h)], [g, s, sh], [(d, MXU)], comm=comm)
    return res[0][0] if comm is None else (res[0][0], res[2])


def _normmod_bwd(name, h, dxn, dres, g, s, nct=0):
    d = h.shape[1]

    def fn(rows, bcs, i):
        hh, dx, dr = rows
        g_, s_ = bcs
        ctx = i < nct
        s1 = jnp.where(ctx, s_[0:1], s_[1:2])
        r = lax.rsqrt(jnp.mean(hh * hh, axis=-1, keepdims=True) + EPS)
        hr = hh * r
        dy = dx * (1.0 + s1)
        u = dy * g_
        dh = r * u - hr * (r * r) * jnp.mean(u * hh, axis=-1, keepdims=True)
        dh = dh + jnp.where(ctx, 0.0, dr)

        def seg(v):
            v = _colsum(v)
            return jnp.concatenate([jnp.where(ctx, v, 0.0), jnp.where(ctx, 0.0, v)], axis=0)

        return [dh], [seg(dx), seg(dx * hr * g_), seg(dy * hr)]

    (dh,), (dsh, ds, dg) = _rowwise(name, fn, h.shape[0], [_ri(h), _ri(dxn), _ri(dres, ro=-nct)], [g, s],
                                    [(d, F32)], [(2, d)] * 3)
    return dh, dsh, ds, dg


def _resnorm_fwd(name, h, y, gate, g, s, sh):
    d = h.shape[1]

    def fn(rows, bcs, i):
        hh, yy = rows
        gate_, g_, s_, sh_ = bcs
        hn = hh + gate_ * yy
        r = lax.rsqrt(jnp.mean(hn * hn, axis=-1, keepdims=True) + EPS)
        return [hn, hn * r * g_ * (1.0 + s_) + sh_], []

    return _rowwise(name, fn, h.shape[0], [_ri(h), _ri(y)], [gate, g, s, sh], [(d, F32), (d, MXU)])[0]


def _gate_bwd(name, dh, y, gate):
    d = dh.shape[1]

    def fn(rows, bcs, i):
        dd, yy = rows
        dy = dd * bcs[0]
        return [dy], [_colsum(dd * yy), _colsum(dy)]

    (dy,), (dgate, dbias) = _rowwise(name, fn, dh.shape[0], [_ri(dh), _ri(y)], [gate], [(d, MXU)], [(1, d)] * 2)
    return dy, dgate, dbias


def _swiglu_fwd(name, u):
    f = u.shape[1] // 2

    def fn(rows, bcs, i):
        return [_silu(rows[0]) * rows[1]], []

    return _rowwise(name, fn, u.shape[0], [_ri(u, f, 0), _ri(u, f, 1)], [], [(f, MXU)])[0][0]


def _swiglu_bwd(name, u, dhid):
    f = u.shape[1] // 2

    def fn(rows, bcs, i):
        a, b, dd = rows
        return [jnp.concatenate([dd * b * _dsilu(a), dd * _silu(a)], axis=1)], []

    return _rowwise(name, fn, u.shape[0], [_ri(u, f, 0), _ri(u, f, 1), _ri(dhid)], [], [(2 * f, MXU)])[0][0]


def _glu_fwd(name, u):
    d = u.shape[1] // 2

    def fn(rows, bcs, i):
        return [rows[0] * _sigmoid(rows[1])], []

    return _rowwise(name, fn, u.shape[0], [_ri(u, d, 0), _ri(u, d, 1)], [], [(d, F32)])[0][0]


def _glu_bwd(name, u, dgl_lo, dgl_hi):
    d = u.shape[1] // 2

    def fn(rows, bcs, i):
        a, b = rows[:2]
        dd = jnp.concatenate(rows[2:], axis=1)
        sg = _sigmoid(b)
        du = jnp.concatenate([dd * sg, dd * a * sg * (1.0 - sg)], axis=1)
        return [du], [_colsum(du)]

    (du,), (db,) = _rowwise(name, fn, u.shape[0], [_ri(u, d, 0), _ri(u, d, 1), _ri(dgl_lo), _ri(dgl_hi)], [], [(2 * d, MXU)],
                            [(1, 2 * d)])
    return du, db


def _ln_silu_fwd(name, v_lo, v_hi, g, b):
    d = 2 * v_lo.shape[1]

    def fn(rows, bcs, i):
        vv = jnp.concatenate(rows, axis=1)
        mu = jnp.mean(vv, axis=-1, keepdims=True)
        xc = vv - mu
        rs = lax.rsqrt(jnp.mean(xc * xc, axis=-1, keepdims=True) + EPS)
        return [_silu(xc * rs * bcs[0] + bcs[1])], []

    return _rowwise(name, fn, v_lo.shape[0], [_ri(v_lo), _ri(v_hi)], [g, b], [(d, MXU)])[0][0]


def _ln_silu_bwd(name, v_lo, v_hi, ds, g, b):
    ch = v_lo.shape[1]

    def fn(rows, bcs, i):
        vv, dd = jnp.concatenate(rows[:2], axis=1), rows[2]
        mu = jnp.mean(vv, axis=-1, keepdims=True)
        xc = vv - mu
        rs = lax.rsqrt(jnp.mean(xc * xc, axis=-1, keepdims=True) + EPS)
        xh = xc * rs
        dln = dd * _dsilu(xh * bcs[0] + bcs[1])
        dxh = dln * bcs[0]
        dv = rs * (dxh - jnp.mean(dxh, axis=-1, keepdims=True) - xh * jnp.mean(dxh * xh, axis=-1, keepdims=True))
        return [dv[:, :ch], dv[:, ch:]], [_colsum(dln * xh), _colsum(dln)]

    (dv_lo, dv_hi), (dg, db) = _rowwise(name, fn, v_lo.shape[0], [_ri(v_lo), _ri(v_hi), _ri(ds)], [g, b],
                                        [(ch, F32), (ch, F32)], [(1, 2 * ch)] * 2)
    return dv_lo, dv_hi, dg, db


def _final_loss(name, h, f, target, gate, gf):
    d = h.shape[1]

    def fn(rows, bcs, i):
        hh, ff, tg = rows
        gate_, g_ = bcs
        hn = hh + gate_ * ff
        r = lax.rsqrt(jnp.mean(hn * hn, axis=-1, keepdims=True) + EPS)
        hr = hn * r
        err = hr * g_ - tg
        dout = err * (1.0 / d)
        u = dout * g_
        dh = r * u - hr * (r * r) * jnp.mean(u * hn, axis=-1, keepdims=True)
        sq = jnp.sum(_colsum(err * err), axis=1, keepdims=True)
        return [dh], [jnp.broadcast_to(sq, (1, LANES)), _colsum(dout * hr)]

    (dh,), (sq, dgf) = _rowwise(name, fn, h.shape[0], [_ri(h), _ri(f), _ri(target)], [gate, gf], [(d, F32)], [(1, LANES), (1, d)])
    return dh, sq, dgf


def _shift_rows(x, o, seg_lo, seg_hi, row):
    n = x.shape[0]
    if o == 0:
        return x
    sh = pltpu.roll(x, (-o) % n, 0)
    ok = (row + o >= seg_lo) & (row + o < seg_hi)
    return jnp.where(ok, sh, 0.0)


def _seg_bounds(row, tc, tt):
    ctx = row < tc
    return jnp.where(ctx, 0, tc), jnp.where(ctx, tc, tt)


def _ssd_conv_fwd(name, zx, w, b, di, tc, comm=None):
    tt, kc, cd = zx.shape[0], w.shape[0], w.shape[1]
    cb = _pick(cd, (LANES,))
    off = di // cb

    def body(x_ref, w_ref, b_ref, o_ref):
        x = x_ref[...].astype(F32)
        row = lax.broadcasted_iota(jnp.int32, (tt, 1), 0)
        lo, hi = _seg_bounds(row, tc, tt)
        acc = jnp.broadcast_to(b_ref[...], x.shape)
        for k in range(kc):
            acc = acc + w_ref[k:k + 1, :] * _shift_rows(x, k - kc // 2, lo, hi, row)
        o_ref[...] = _silu(acc).astype(o_ref.dtype)

    cm = comm if comm is not None else _Comm()
    res = pl.pallas_call(
        _carry(body, comm, 3, 1, (cd // cb,)), name=name, grid=(cd // cb,), out_shape=[pltpu.HBM((tt, cd), ACT)] + cm.out_shape,
        in_specs=[pl.BlockSpec((tt, cb), lambda j: (0, j + off)), pl.BlockSpec((kc, cb), lambda j: (0, j)),
                  pl.BlockSpec((1, cb), lambda j: (0, j))] + cm.specs,
        out_specs=[pl.BlockSpec((tt, cb), lambda j: (0, j))] + cm.specs,
        scratch_shapes=cm.scratch if comm is not None else [],
        compiler_params=pltpu.CompilerParams(dimension_semantics=("arbitrary",), vmem_limit_bytes=_vmem(4 * tt * cb * 4)),
    )(_hbm(zx), _hbm(w), _hbm(b), *cm.operands)
    return res[0], cm.split(res[1:])


def _ssd_conv_bwd(name, zx, dact2, w, b, dzx, di, tc, comm=None):
    tt, kc, cd = zx.shape[0], w.shape[0], w.shape[1]
    cb = _pick(cd, (LANES,))
    off = di // cb

    def body(x_ref, d0_ref, d1_ref, w_ref, b_ref, _, dx_ref, dw_ref, db_ref):
        x = x_ref[...].astype(F32)
        row = lax.broadcasted_iota(jnp.int32, (tt, 1), 0)
        lo, hi = _seg_bounds(row, tc, tt)
        pre = jnp.broadcast_to(b_ref[...], x.shape)
        for k in range(kc):
            pre = pre + w_ref[k:k + 1, :] * _shift_rows(x, k - kc // 2, lo, hi, row)
        dpre = (d0_ref[...].astype(F32) + d1_ref[...].astype(F32)) * _dsilu(pre)
        dx = jnp.zeros_like(x)
        for k in range(kc):
            o = k - kc // 2
            dx = dx + w_ref[k:k + 1, :] * _shift_rows(dpre, -o, lo, hi, row)
            dw_ref[k:k + 1, :] = _colsum(dpre * _shift_rows(x, o, lo, hi, row))
        dx_ref[...] = dx.astype(dx_ref.dtype)
        db_ref[...] = _colsum(dpre)

    cm = comm if comm is not None else _Comm()
    res = pl.pallas_call(
        _carry(body, comm, 6, 3, (cd // cb,)), name=name, grid=(cd // cb,),
        out_shape=[pltpu.HBM(dzx.shape, dzx.dtype), pltpu.HBM((kc, cd), F32), pltpu.HBM((1, cd), F32)] + cm.out_shape,
        in_specs=[pl.BlockSpec((tt, cb), lambda j: (0, j + off)), pl.BlockSpec((None, tt, cb), lambda j: (0, 0, j)),
                  pl.BlockSpec((None, tt, cb), lambda j: (1, 0, j)), pl.BlockSpec((kc, cb), lambda j: (0, j)),
                  pl.BlockSpec((1, cb), lambda j: (0, j)), pl.BlockSpec(memory_space=pl.ANY)] + cm.specs,
        out_specs=[pl.BlockSpec((tt, cb), lambda j: (0, j + off)), pl.BlockSpec((kc, cb), lambda j: (0, j)),
                   pl.BlockSpec((1, cb), lambda j: (0, j))] + cm.specs,
        input_output_aliases={5: 0}, scratch_shapes=cm.scratch if comm is not None else [],
        compiler_params=pltpu.CompilerParams(dimension_semantics=("arbitrary",), vmem_limit_bytes=_vmem(8 * tt * cb * 4)),
    )(_hbm(zx), _hbm(dact2), _hbm(dact2), _hbm(w), _hbm(b), _hbm(dzx), *cm.operands)
    return res[0], res[1], res[2], cm.split(res[3:])


def _strided_conv(name, x, w, b, stride, x_col0=0):
    t, ch = x.shape[0], w.shape[1]
    kk = w.shape[0]
    pad = (kk // 2) * stride
    cb = _pick(ch, (LANES,))
    has_b = b is not None

    def body(*refs):
        x_ref, w_ref = refs[:2]
        o_ref, xp = refs[-2:]
        xp[0:pad, :] = jnp.zeros((pad, cb), F32)
        xp[pad + t:, :] = jnp.zeros((pad, cb), F32)
        xp[pad:pad + t, :] = x_ref[...]
        acc = jnp.broadcast_to(refs[2][...], (t, cb)) if has_b else jnp.zeros((t, cb), F32)
        for k in range(kk):
            acc = acc + w_ref[k:k + 1, :] * xp[k * stride:k * stride + t, :]
        o_ref[...] = acc

    xoff = x_col0 // cb
    ins, specs = [x, w], [pl.BlockSpec((t, cb), lambda j: (0, j + xoff)), pl.BlockSpec((kk, cb), lambda j: (0, j))]
    if has_b:
        ins.append(b)
        specs.append(pl.BlockSpec((1, cb), lambda j: (0, j)))
    return pl.pallas_call(
        body, name=name, grid=(ch // cb,), out_shape=pltpu.HBM((t, ch), F32), in_specs=specs,
        out_specs=pl.BlockSpec((t, cb), lambda j: (0, j)), scratch_shapes=[pltpu.VMEM((t + 2 * pad, cb), F32)],
        compiler_params=pltpu.CompilerParams(dimension_semantics=("parallel",), vmem_limit_bytes=_vmem(6 * t * cb * 4)),
    )(*[_hbm(v) for v in ins])


def _strided_conv_dw(name, x, dv, kk, stride, x_col0=0):
    t, ch = dv.shape
    pad = (kk // 2) * stride
    cb = _pick(ch, (LANES,))

    def body(x_ref, d_ref, dw_ref, db_ref, xp):
        xp[0:pad, :] = jnp.zeros((pad, cb), F32)
        xp[pad + t:, :] = jnp.zeros((pad, cb), F32)
        xp[pad:pad + t, :] = x_ref[...]
        d = d_ref[...]
        for k in range(kk):
            dw_ref[k:k + 1, :] = _colsum(d * xp[k * stride:k * stride + t, :])
        db_ref[...] = _colsum(d)

    blk = pl.BlockSpec((t, cb), lambda j: (0, j))
    xoff = x_col0 // cb
    return pl.pallas_call(
        body, name=name, grid=(ch // cb,), out_shape=[pltpu.HBM((kk, ch), F32), pltpu.HBM((1, ch), F32)],
        in_specs=[pl.BlockSpec((t, cb), lambda j: (0, j + xoff)), blk], out_specs=[pl.BlockSpec((kk, cb), lambda j: (0, j)), pl.BlockSpec((1, cb), lambda j: (0, j))],
        scratch_shapes=[pltpu.VMEM((t + 2 * pad, cb), F32)],
        compiler_params=pltpu.CompilerParams(dimension_semantics=("parallel",), vmem_limit_bytes=_vmem(6 * t * cb * 4)),
    )(_hbm(x), _hbm(dv))


def _grid_t(a, n1, n2):
    return a.reshape(n1, n2, a.shape[-1]).swapaxes(0, 1).reshape(n1 * n2, a.shape[-1])


def _chunk_order(d, i, ncc, nc):
    back = jnp.where(i < ncc, ncc - 1 - i, nc - 1 - (i - ncc))
    return jnp.where(d == 0, i, back)


def _ssd_chunk_setup(d, dt_raw, bias, a_log, q, h, di):
    p = di // h
    dt = _softplus(dt_raw + bias)
    a_neg = -jnp.exp(a_log)
    delta = dt * a_neg
    r = lax.broadcasted_iota(jnp.int32, (q, q), 0)
    c = lax.broadcasted_iota(jnp.int32, (q, q), 1)
    sgn = 1 - 2 * d
    mask = (r - c) * sgn >= 0
    mask_t = (c - r) * sgn >= 0
    a = _dot_lx(mask.astype(MXU), delta, NN, parts=3)
    tot = _colsum(delta)
    ea, dte, cd = jnp.exp(a), jnp.exp(tot - a), jnp.exp(tot)
    hh = lax.broadcasted_iota(jnp.int32, (h, di), 0)
    cc = lax.broadcasted_iota(jnp.int32, (h, di), 1)
    e = (cc // p == hh).astype(MXU)
    ex = _dot_rx(jnp.concatenate([dt, ea, dte, jnp.broadcast_to(cd, (8, h))], axis=0), e, NN)
    eye = (lax.broadcasted_iota(jnp.int32, (h, h), 0) == lax.broadcasted_iota(jnp.int32, (h, h), 1)).astype(MXU)
    a_t = _dot_lx(eye, a, NT, parts=3)
    return dict(dt=dt, a_neg=a_neg, a=a, a_t=a_t, mask=mask, mask_t=mask_t, e=e,
                dt_e=ex[0:q], ea_e=ex[q:2 * q], dte_e=ex[2 * q:3 * q], cd_e=ex[3 * q:3 * q + 1])


def _pick_heads(r, q, hpg, p):
    lane = lax.broadcasted_iota(jnp.int32, (q, hpg * p), 1) // p
    out = jnp.zeros((q, hpg * p), F32)
    for j in range(hpg):
        out = out + jnp.where(lane == j, r[j * q:(j + 1) * q], 0.0)
    return out


def _ssd_fwd(name, xbc, dt2, bias2, alog2, di, tc, comm=None):
    tt, cd = xbc.shape
    h = dt2.shape[-1]
    q, n = SSD_CHUNK, SSD_STATE
    gn = (cd - di) // 2
    g = gn // n
    hpg, p = h // g, di // h
    gp = hpg * p
    nc, ncc = tt // q, tc // q
    assert di % gn == 0

    def body(x_ref, b_ref, c_ref, dt_ref, bias_ref, alog_ref, y_ref, hp_ref, ht):
        d, i = pl.program_id(0), pl.program_id(1)

        @pl.when(i == 0)
        def _():
            ht[...] = jnp.zeros_like(ht)

        s = _ssd_chunk_setup(d, dt_ref[...], bias_ref[...], alog_ref[...], q, h, di)
        xd = x_ref[...].astype(F32) * s["dt_e"]
        hp_ref[...] = ht[...].astype(hp_ref.dtype)
        for gi in range(g):
            bg, cg = b_ref[:, gi * n:(gi + 1) * n].astype(MXU), c_ref[:, gi * n:(gi + 1) * n].astype(MXU)
            sl = slice(gi * gp, (gi + 1) * gp)
            sc = _dot(cg, bg, NT)
            ms = []
            for j in range(hpg):
                hd = gi * hpg + j
                seg = s["a"][:, hd:hd + 1] - s["a_t"][hd:hd + 1, :]
                ms.append(sc * jnp.exp(jnp.where(s["mask"], seg, -jnp.inf)))
            xdg = xd[:, sl]
            ydiag = _pick_heads(_dot(jnp.concatenate(ms, axis=0), xdg, NN), q, hpg, p)
            htg = ht[:, sl]
            y_ref[:, sl] = ydiag + _dot(cg, htg, NN) * s["ea_e"][:, sl]
            ht[:, sl] = s["cd_e"][:, sl] * htg + _dot(bg, xdg * s["dte_e"][:, sl], TN)

    def cidx(d, i):
        return _chunk_order(d, i, ncc, nc)

    cm = comm if comm is not None else _Comm()
    res = pl.pallas_call(
        _carry(body, comm, 6, 2, (2, nc)), name=name, grid=(2, nc),
        out_shape=[pltpu.HBM((2, tt, di), F32), pltpu.HBM((2, nc, n, di), ACT)] + cm.out_shape,
        in_specs=[pl.BlockSpec((q, di), lambda d, i: (cidx(d, i), 0)),
                  pl.BlockSpec((q, gn), lambda d, i: (cidx(d, i), di // gn)),
                  pl.BlockSpec((q, gn), lambda d, i: (cidx(d, i), di // gn + 1)),
                  pl.BlockSpec((None, q, h), lambda d, i: (d, cidx(d, i), 0)),
                  pl.BlockSpec((None, 1, h), lambda d, i: (d, 0, 0)),
                  pl.BlockSpec((None, 1, h), lambda d, i: (d, 0, 0))] + cm.specs,
        out_specs=[pl.BlockSpec((None, q, di), lambda d, i: (d, cidx(d, i), 0)),
                   pl.BlockSpec((None, None, n, di), lambda d, i: (d, cidx(d, i), 0, 0))] + cm.specs,
        scratch_shapes=[pltpu.VMEM((n, di), F32)] + (cm.scratch if comm is not None else []),
        compiler_params=pltpu.CompilerParams(dimension_semantics=("arbitrary", "arbitrary"), vmem_limit_bytes=_vmem(16 * q * di * 4)),
    )(*[_hbm(v) for v in (xbc, xbc, xbc, dt2, bias2, alog2)], *cm.operands)
    return res[0], res[1], cm.split(res[2:])


def _ssd_bwd(name, xbc, dt2, bias2, alog2, dy, hp2, dskip_e, di, tc, comm=None):
    tt, cd = xbc.shape
    h = dt2.shape[-1]
    q, n = SSD_CHUNK, SSD_STATE
    gn = (cd - di) // 2
    g = gn // n
    hpg, p = h // g, di // h
    gp = hpg * p
    nc, ncc = tt // q, tc // q

    def body(x_ref, b_ref, c_ref, dt_ref, bias_ref, alog_ref, dy_ref, hp_ref, dsk_ref,
             dxbc_ref, ddt_ref, dalog_ref, dbias_ref, dht, dxd, off):
        d, i = pl.program_id(0), pl.program_id(1)

        @pl.when(i == 0)
        def _():
            dht[...] = jnp.zeros_like(dht)
            dalog_ref[...] = jnp.zeros_like(dalog_ref)
            dbias_ref[...] = jnp.zeros_like(dbias_ref)

        s = _ssd_chunk_setup(d, dt_ref[...], bias_ref[...], alog_ref[...], q, h, di)
        x, dyc = x_ref[...].astype(F32), dy_ref[...]
        xd = x * s["dt_e"]
        dyea = dyc * s["ea_e"]
        xdte = xd * s["dte_e"]
        lane = lax.broadcasted_iota(jnp.int32, (q, gp), 1) // p
        lane_h = lax.broadcasted_iota(jnp.int32, (q, h), 1)
        da_d = jnp.zeros((q, h), F32)
        last_e = []
        for gi in range(g):
            bg, cg = b_ref[:, gi * n:(gi + 1) * n].astype(MXU), c_ref[:, gi * n:(gi + 1) * n].astype(MXU)
            sl = slice(gi * gp, (gi + 1) * gp)
            sc, sct = _dot(cg, bg, NT), _dot(bg, cg, NT)
            dyg, xdg = dyc[:, sl], xd[:, sl]
            htg, dhtg = hp_ref[:, sl].astype(F32), dht[:, sl]
            dystack = jnp.concatenate([jnp.where(lane == j, dyg, 0.0) for j in range(hpg)], axis=0)
            xdstack = jnp.concatenate([jnp.where(lane == j, xdg, 0.0) for j in range(hpg)], axis=0)
            gs = _dot(dystack, xdg, NT)
            gst = _dot(xdstack, dyg, NT)
            ds = jnp.zeros((q, q), F32)
            mts = []
            for j in range(hpg):
                hd = gi * hpg + j
                col, rw = s["a"][:, hd:hd + 1], s["a_t"][hd:hd + 1, :]
                gl = gs[j * q:(j + 1) * q] * jnp.exp(jnp.where(s["mask"], col - rw, -jnp.inf))
                ds = ds + gl
                mt = sct * jnp.exp(jnp.where(s["mask_t"], rw - col, -jnp.inf))
                mts.append(mt)
                da_j = jnp.sum(gl * sc, axis=1, keepdims=True) - jnp.sum(gst[j * q:(j + 1) * q] * mt, axis=1, keepdims=True)
                da_d = da_d + jnp.where(lane_h == hd, da_j, 0.0)
            dxd_diag = _pick_heads(_dot(jnp.concatenate(mts, axis=0), dyg, NN), q, hpg, p)
            z = _dot(bg, dhtg, NN) * s["dte_e"][:, sl]
            yoff = _dot(cg, htg, NN) * s["ea_e"][:, sl]
            off[:, sl] = dyg * yoff - xdg * z
            dxd[:, sl] = dxd_diag + z
            dxbc_ref[:, di + gi * n:di + (gi + 1) * n] = (_dot(ds, cg, TN) + _dot(xdte[:, sl], dhtg, NT)).astype(dxbc_ref.dtype)
            dxbc_ref[:, di + gn + gi * n:di + gn + (gi + 1) * n] = (_dot(ds, bg, NN)
                                                                    + _dot(dyea[:, sl], htg, NT)).astype(dxbc_ref.dtype)
            last_e.append(s["cd_e"][:, sl] * _colsum(dhtg * htg) + _colsum(xdg * z))
            dht[:, sl] = s["cd_e"][:, sl] * dhtg + _dot(cg, dyea[:, sl], TN)
        dxd_all = dxd[...]
        last = jnp.concatenate(last_e, axis=1)
        da = da_d + _dot_rx(off[...], s["e"], NT)
        last_h = _dot_rx(jnp.broadcast_to(last, (8, di)), s["e"], NT)[0:1]
        ddelta = _dot_lx(s["mask_t"].astype(MXU), da, NN, parts=3) + last_h
        ddt = ddelta * s["a_neg"] + _dot_rx(dxd_all * x, s["e"], NT)
        ddt_raw = ddt * _sigmoid(dt_ref[...] + bias_ref[...])
        ddt_ref[...] = ddt_raw
        dalog_ref[...] += _colsum(ddelta * s["dt"]) * s["a_neg"]
        dbias_ref[...] += _colsum(ddt_raw)
        dxbc_ref[:, 0:di] = (dxd_all * s["dt_e"] + jnp.where(d == 0, dyc * dsk_ref[...], 0.0)).astype(dxbc_ref.dtype)

    def cidx(d, i):
        return _chunk_order(d, nc - 1 - i, ncc, nc)

    cm = comm if comm is not None else _Comm()
    res = pl.pallas_call(
        _carry(body, comm, 9, 4, (2, nc)), name=name, grid=(2, nc),
        out_shape=[pltpu.HBM((2, tt, cd), ACT), pltpu.HBM((2, tt, h), F32),
                   pltpu.HBM((2, 1, h), F32), pltpu.HBM((2, 1, h), F32)] + cm.out_shape,
        in_specs=[pl.BlockSpec((q, di), lambda d, i: (cidx(d, i), 0)),
                  pl.BlockSpec((q, gn), lambda d, i: (cidx(d, i), di // gn)),
                  pl.BlockSpec((q, gn), lambda d, i: (cidx(d, i), di // gn + 1)),
                  pl.BlockSpec((None, q, h), lambda d, i: (d, cidx(d, i), 0)),
                  pl.BlockSpec((None, 1, h), lambda d, i: (d, 0, 0)),
                  pl.BlockSpec((None, 1, h), lambda d, i: (d, 0, 0)),
                  pl.BlockSpec((q, di), lambda d, i: (cidx(d, i), 0)),
                  pl.BlockSpec((None, None, n, di), lambda d, i: (d, cidx(d, i), 0, 0)),
                  pl.BlockSpec((1, di), lambda d, i: (0, 0))] + cm.specs,
        out_specs=[pl.BlockSpec((None, q, cd), lambda d, i: (d, cidx(d, i), 0)),
                   pl.BlockSpec((None, q, h), lambda d, i: (d, cidx(d, i), 0)),
                   pl.BlockSpec((None, 1, h), lambda d, i: (d, 0, 0)),
                   pl.BlockSpec((None, 1, h), lambda d, i: (d, 0, 0))] + cm.specs,
        scratch_shapes=[pltpu.VMEM((n, di), F32), pltpu.VMEM((q, di), F32), pltpu.VMEM((q, di), F32)]
        + (cm.scratch if comm is not None else []),
        compiler_params=pltpu.CompilerParams(dimension_semantics=("arbitrary", "arbitrary"), vmem_limit_bytes=_vmem(24 * q * di * 4)),
    )(*[_hbm(v) for v in (xbc, xbc, xbc, dt2, bias2, alog2, dy, hp2, dskip_e)], *cm.operands)
    return res[0], res[1], res[2], res[3], cm.split(res[4:])


def _ssd_gate_fwd(name, y2, xbc, zx, dskip_e, norm_w, di, nct, t):
    def fn(rows, bcs, i):
        yf, yb, xs, z = rows
        zg = (yf + yb + bcs[0] * xs) * _silu(z)
        rn = lax.rsqrt(jnp.mean(zg * zg, axis=-1, keepdims=True) + EPS)
        return [zg * rn * bcs[1]], []

    ins = [_ri(y2, lead=0, ro=nct), _ri(y2, lead=1, ro=nct), _ri(xbc, di, 0, ro=nct), _ri(zx, di, 0, ro=nct)]
    return _rowwise(name, fn, t, ins, [dskip_e, norm_w], [(di, MXU)])[0][0]


def _ssd_gate_bwd(name, dyn, y2, xbc, zx, dskip_e, norm_w, di, nct, tt):
    def fn(rows, bcs, i):
        dn, yf, yb, xs, z = rows
        lat = i >= nct
        ytot = yf + yb + bcs[0] * xs
        sz = _silu(z)
        zg = ytot * sz
        rn = lax.rsqrt(jnp.mean(zg * zg, axis=-1, keepdims=True) + EPS)
        u = dn * bcs[1]
        dzg = rn * u - zg * (rn * rn * rn) * jnp.mean(u * zg, axis=-1, keepdims=True)
        dy = jnp.where(lat, dzg * sz, 0.0)
        dz = jnp.where(lat, dzg * ytot * _dsilu(z), 0.0)
        return [dy, dz], [jnp.where(lat, _colsum(dn * zg * rn), 0.0), jnp.where(lat, _colsum(dy * xs), 0.0)]

    ins = [_ri(dyn, ro=-nct), _ri(y2, lead=0), _ri(y2, lead=1), _ri(xbc, di, 0), _ri(zx, di, 0)]
    (dy, dzx), (dnw, ddsk) = _rowwise(name, fn, tt, ins, [dskip_e, norm_w], [(di, F32), (di, MXU, zx.shape[1], 0)], [(1, di)] * 2)
    return dy, dzx, dnw, ddsk


def _ada_fwd(name, cs, w, b):
    nl, d, c = w.shape
    r = cs.shape[0]

    def body(cs_ref, w_ref, b_ref, o_ref):
        o_ref[...] = _dot(_silu(cs_ref[...]), w_ref[...], NN) + b_ref[...]

    return pl.pallas_call(
        body, name=name, grid=(nl,), out_shape=pltpu.HBM((nl, r, c), F32),
        in_specs=[pl.BlockSpec((r, d), lambda l: (0, 0)), pl.BlockSpec((None, d, c), lambda l: (l, 0, 0)),
                  pl.BlockSpec((None, 1, c), lambda l: (l, 0, 0))],
        out_specs=pl.BlockSpec((None, r, c), lambda l: (l, 0, 0)),
        compiler_params=pltpu.CompilerParams(dimension_semantics=("parallel",), vmem_limit_bytes=_vmem(2 * d * c * 4)),
    )(_hbm(cs), _hbm(w), _hbm(b))


def _ada_bwd(name, cs, w, dmod):
    nl, d, c = w.shape
    r = cs.shape[0]

    def body(cs_ref, w_ref, dm_ref, dw_ref, dsc_ref):
        dm = dm_ref[...]
        dw_ref[...] = _dot(_silu(cs_ref[...]), dm, TN)

        @pl.when(pl.program_id(0) == 0)
        def _():
            dctx = jnp.broadcast_to(_colsum(dm[r // 2:]), (8, c))
            dsc_ref[...] = _dot(dctx, w_ref[...], NT)[0:1]

    return pl.pallas_call(
        body, name=name, grid=(nl,), out_shape=[pltpu.HBM((nl, d, c), F32), pltpu.HBM((1, d), F32)],
        in_specs=[pl.BlockSpec((r, d), lambda l: (0, 0)), pl.BlockSpec((None, d, c), lambda l: (l, 0, 0)),
                  pl.BlockSpec((None, r, c), lambda l: (l, 0, 0))],
        out_specs=[pl.BlockSpec((None, d, c), lambda l: (l, 0, 0)), pl.BlockSpec((1, d), lambda l: (0, 0))],
        compiler_params=pltpu.CompilerParams(dimension_semantics=("arbitrary",), vmem_limit_bytes=_vmem(4 * d * c * 4)),
    )(_hbm(cs), _hbm(w), _hbm(dmod))


def _adam_math(w, g, m, v):
    m = ADAM_B1 * m + (1.0 - ADAM_B1) * g
    v = ADAM_B2 * v + (1.0 - ADAM_B2) * (g * g)
    m_hat = m / (1.0 - ADAM_B1 ** ADAM_STEP)
    v_hat = v / (1.0 - ADAM_B2 ** ADAM_STEP)
    delta = -ADAM_LR * (m_hat / (jnp.sqrt(v_hat) + ADAM_EPS) + ADAM_WD * w)
    return delta, m, v


def _adam(name, slots, w, m, v, comm=None):
    ns, r, c = slots.shape
    tr = _pick(r, (256, 128, 64, 32, 16, 8))

    def body(s_ref, w_ref, m_ref, v_ref, g_ref, d_ref, mo_ref, vo_ref):
        g = s_ref[0].astype(F32)
        for k in range(1, ns):
            g = g + s_ref[k].astype(F32)
        d, mn, vn = _adam_math(w_ref[...], g, m_ref[...], v_ref[...])
        g_ref[...], d_ref[...], mo_ref[...], vo_ref[...] = g, d, mn, vn

    blk = pl.BlockSpec((tr, c), lambda i: (i, 0))
    cm = comm if comm is not None else _Comm()
    res = pl.pallas_call(
        _carry(body, comm, 4, 4, (r // tr,)), name=name, grid=(r // tr,), out_shape=[pltpu.HBM((r, c), F32)] * 4 + cm.out_shape,
        in_specs=[pl.BlockSpec((ns, tr, c), lambda i: (0, i, 0)), blk, blk, blk] + cm.specs, out_specs=[blk] * 4 + cm.specs,
        scratch_shapes=cm.scratch if comm is not None else [],
        compiler_params=pltpu.CompilerParams(dimension_semantics=("arbitrary",), vmem_limit_bytes=_vmem(16 * tr * c * 4)),
    )(_hbm(slots), _hbm(w), _hbm(m), _hbm(v), *cm.operands)
    return res[:4] if comm is None else (res[:4], cm.split(res[4:]))


def _adam_small(name, slots, ws, ms, vs, scale=None):
    k = len(slots)

    def body(*refs):
        s_refs, w_refs, m_refs, v_refs = refs[:k], refs[k:2 * k], refs[2 * k:3 * k], refs[3 * k:4 * k]
        sc_ref = refs[4 * k] if scale is not None else None
        outs = refs[4 * k + (scale is not None):]
        for a in range(k):
            g = s_refs[a][0]
            for j in range(1, NDEV):
                g = g + s_refs[a][j]
            if scale is not None and a == scale[0]:
                g = g * _dsilu(sc_ref[...])
            d, mn, vn = _adam_math(w_refs[a][...], g, m_refs[a][...], v_refs[a][...])
            outs[a][...], outs[k + a][...], outs[2 * k + a][...], outs[3 * k + a][...] = g, d, mn, vn

    shapes = [pltpu.HBM(w.shape, F32) for w in ws]
    extra = [scale[1]] if scale is not None else []
    ins = [*slots, *ws, *ms, *vs, *extra]

    def whole(shape):
        return pl.BlockSpec(shape, lambda i, nd=len(shape): (0,) * nd)

    res = pl.pallas_call(body, name=name, grid=(1,), out_shape=shapes * 4, in_specs=[whole(v.shape) for v in ins],
                         out_specs=[whole(s.shape) for s in shapes * 4])(*[_hbm(v) for v in ins])
    return res[:k], res[k:2 * k], res[2 * k:3 * k], res[3 * k:]


def _unshard_cols(g):
    g = jnp.moveaxis(g, 0, -2)
    return g.reshape(g.shape[:-2] + (g.shape[-2] * g.shape[-1],))


def _shard_cols(a):
    a = a.reshape(a.shape[:-1] + (NDEV, a.shape[-1] // NDEV))
    return jnp.moveaxis(a, -2, 0)


def _unshard_rows(g):
    g = jnp.moveaxis(g, 0, -3)
    return g.reshape(g.shape[:-3] + (g.shape[-3] * g.shape[-2], g.shape[-1]))


def _shard_rows(a):
    a = a.reshape(a.shape[:-2] + (NDEV, a.shape[-2] // NDEV, a.shape[-1]))
    return jnp.moveaxis(a, -3, 0)


def _flat2(a):
    return a.reshape((-1, a.shape[-1]))


def kernel(x, c, ctx, c_ctx, ada_w, ada_b, norm_mix_g, norm_ffn_g, final_norm_g, ssd_w_in, ssd_conv_w, ssd_conv_b, ssd_dt_bias_f, ssd_dt_bias_b, ssd_a_log_f, ssd_a_log_b, ssd_d_skip, ssd_norm_w, ssd_w_out, conf_w_pw1, conf_b_pw1, conf_dw_w, conf_dw_b, conf_ln_g, conf_ln_b, conf_w_pw2, conf_b_pw2, ffn_w_in, ffn_w_out, loss_target, m_c_ctx, m_ada_w, m_ada_b, m_norm_mix_g, m_norm_ffn_g, m_final_norm_g, m_ssd_w_in, m_ssd_conv_w, m_ssd_conv_b, m_ssd_dt_bias_f, m_ssd_dt_bias_b, m_ssd_a_log_f, m_ssd_a_log_b, m_ssd_d_skip, m_ssd_norm_w, m_ssd_w_out, m_conf_w_pw1, m_conf_b_pw1, m_conf_dw_w, m_conf_dw_b, m_conf_ln_g, m_conf_ln_b, m_conf_w_pw2, m_conf_b_pw2, m_ffn_w_in, m_ffn_w_out, v_c_ctx, v_ada_w, v_ada_b, v_norm_mix_g, v_norm_ffn_g, v_final_norm_g, v_ssd_w_in, v_ssd_conv_w, v_ssd_conv_b, v_ssd_dt_bias_f, v_ssd_dt_bias_b, v_ssd_a_log_f, v_ssd_a_log_b, v_ssd_d_skip, v_ssd_norm_w, v_ssd_w_out, v_conf_w_pw1, v_conf_b_pw1, v_conf_dw_w, v_conf_dw_b, v_conf_ln_g, v_conf_ln_b, v_conf_w_pw2, v_conf_b_pw2, v_ffn_w_in, v_ffn_w_out):
    args = dict(locals())
    names = ['c_ctx', 'ada_w', 'ada_b', 'norm_mix_g', 'norm_ffn_g', 'final_norm_g', 'ssd_w_in', 'ssd_conv_w', 'ssd_conv_b',
             'ssd_dt_bias_f', 'ssd_dt_bias_b', 'ssd_a_log_f', 'ssd_a_log_b', 'ssd_d_skip', 'ssd_norm_w', 'ssd_w_out',
             'conf_w_pw1', 'conf_b_pw1', 'conf_dw_w', 'conf_dw_b', 'conf_ln_g', 'conf_ln_b', 'conf_w_pw2', 'conf_b_pw2',
             'ffn_w_in', 'ffn_w_out']
    me = 4 * lax.axis_index("x") + 2 * lax.axis_index("y") + lax.axis_index("c")
    t, d = x.shape[1], x.shape[2]
    tc = ctx.shape[1]
    tt = tc + t
    nct = tc // ROW_TILE
    assert tc % ROW_TILE == 0 and t % ROW_TILE == 0
    h = ssd_dt_bias_f.shape[-1]
    di = ssd_norm_w.shape[-1]
    cdim = ssd_conv_b.shape[-1]
    kc = ssd_conv_w.shape[1]
    ck = conf_dw_w.shape[1]
    ch = d // 2
    rows_g = t // GRID_W
    nl = ada_w.shape[0]
    cw = ada_w.shape[2]
    x2, ctx2, tgt = x[0], ctx[0], loss_target[0]

    (c_all, convw_g), _ = _exchange("gather_first", [c, ssd_conv_w[0]])
    ride_norm = _Comm(gather=[ssd_w_in[0].astype(WIRE)])
    ride_proj = _Comm(gather=[ssd_w_out[0].astype(WIRE), conf_w_pw2[0].astype(WIRE)])
    ride_conv = _Comm(gather=[conf_w_pw1[0].astype(WIRE), conf_b_pw1, conf_dw_w[0], conf_dw_b, conf_ln_g, conf_ln_b, conf_b_pw2])
    ride_scan = _Comm(gather=[ffn_w_in.astype(WIRE), ffn_w_out.astype(WIRE)])
    conv_w_full = _unshard_cols(convw_g)

    cs_all = jnp.concatenate([c_all[:, 0, :], jnp.broadcast_to(c_ctx[None, :], (NDEV, d))], axis=0)
    ada_b_mine = lax.dynamic_slice_in_dim(ada_b, me * cw, cw, axis=1)[:, None, :]
    mod_part = _ada_fwd("ada_fwd", cs_all, ada_w, ada_b_mine)
    (mod_g,), _ = _exchange("gather_mod", [mod_part])
    mod_all = jnp.moveaxis(mod_g, 0, 2).reshape(nl, 2 * NDEV, NDEV * cw)
    mod_lat = lax.dynamic_slice_in_dim(mod_all, me, 1, axis=1)[:, 0, :]
    mod_ctx = mod_all[0, NDEV, :]

    def six(v):
        return [v[k * d:(k + 1) * d][None, :] for k in range(6)]

    sh1, s1, g1, sh2, s2, g2 = six(mod_lat[0])
    csh1, cs1 = six(mod_ctx)[:2]
    sh1b, s1b, g1b, sh2b, s2b, g2b = six(mod_lat[1])
    nmg, nfg = norm_mix_g, norm_ffn_g

    h_all = jnp.concatenate([ctx2, x2], axis=0)
    s01, sh01 = jnp.concatenate([cs1, s1], axis=0), jnp.concatenate([csh1, sh1], axis=0)
    xn_all, ((w_in_g,), _) = _normmod_fwd("l0_norm", h_all, nmg[0:1], s01, sh01, nct, comm=ride_norm)
    w_ssd_in = _unshard_cols(w_in_g)
    w_zx = w_ssd_in[:, :di + cdim]
    w_dt = jnp.pad(w_ssd_in[:, di + cdim:], ((0, 0), (0, LANES - 2 * h)))
    zx, ((w_out_g, pw2_g), _) = _mm("ssd_in_proj", xn_all, w_zx, "nn", ACT, comm=ride_proj)
    dtr = _mm("ssd_dt_proj", xn_all, w_dt, "nn")
    dt2 = jnp.moveaxis(dtr[:, :2 * h].reshape(tt, 2, h), 1, 0)
    bias2 = jnp.stack([ssd_dt_bias_f, ssd_dt_bias_b])
    alog2 = jnp.stack([ssd_a_log_f, ssd_a_log_b])
    xbc, ((pw1_g, bpw1_g, dww_g, dwb_g, lng_g, lnb_g, bpw2_g), _) = _ssd_conv_fwd("ssd_conv", zx, conv_w_full, ssd_conv_b, di, tc,
                                                                                 comm=ride_conv)
    y2, hp2, ((fin_g, fout_g), _) = _ssd_fwd("ssd_scan", xbc, dt2, bias2, alog2, di, tc, comm=ride_scan)
    w_ssd_out = _unshard_rows(w_out_g)
    w_pw1, w_pw2 = _unshard_cols(pw1_g), _unshard_rows(pw2_g)
    w_fin, w_fout = _unshard_cols(fin_g), _unshard_rows(fout_g)
    dw_w_full = _unshard_cols(dww_g)
    b_pw1, dw_b, ln_g, ln_b, b_pw2 = (_unshard_cols(a) for a in (bpw1_g, dwb_g, lng_g, lnb_g, bpw2_g))
    dskip_e = jnp.repeat(ssd_d_skip, di // h, axis=1)
    yn = _ssd_gate_fwd("ssd_gate", y2, xbc, zx, dskip_e, ssd_norm_w, di, nct, t)
    mix0 = _mm("ssd_out_proj", yn, w_ssd_out, "nn")
    h1, xf0 = _resnorm_fwd("l0_res_norm", x2, mix0, g1, nfg[0:1], s2, sh2)
    u0 = _mm("ffn0_in", xf0, w_fin[0], "nn", ACT)
    hid0 = _swiglu_fwd("ffn0_act", u0)
    f0 = _mm("ffn0_out", hid0, w_fout[0], "nn")
    h2, xn1 = _resnorm_fwd("l1_norm", h1, f0, g2, nmg[1:2], s1b, sh1b)
    u1 = _mm("conf_pw1", xn1, w_pw1, "nn", ACT, bias=b_pw1)
    gl = _glu_fwd("conf_glu", u1)
    gl_h = _grid_t(gl[:, :ch], rows_g, GRID_W)
    v_ht = _strided_conv("conf_conv_h", gl_h, dw_w_full[:, :ch], dw_b[:, :ch], rows_g)
    v_v = _strided_conv("conf_conv_v", gl, dw_w_full[:, ch:], dw_b[:, ch:], GRID_W, x_col0=ch)
    v_h = _grid_t(v_ht, GRID_W, rows_g)
    sl = _ln_silu_fwd("conf_ln", v_h, v_v, ln_g, ln_b)
    mix1 = _mm("conf_pw2", sl, w_pw2, "nn", bias=b_pw2)
    h3, xf1 = _resnorm_fwd("l1_res_norm", h2, mix1, g1b, nfg[1:2], s2b, sh2b)
    u2 = _mm("ffn1_in", xf1, w_fin[1], "nn", ACT)
    hid1 = _swiglu_fwd("ffn1_act", u2)
    f1 = _mm("ffn1_out", hid1, w_fout[1], "nn")
    dh, sq, d_final_g = _final_loss("final_loss", h3, f1, tgt, g2b, final_norm_g[None, :])
    loss = lax.psum(0.5 * sq[0, 0] / d, AXES)

    zero2 = jnp.zeros((2, d), F32)

    def ffn_bwd(tag, dh, hin, xf, u, hid, f, gate, w_in, w_out, g_norm, s_mod):
        df, dgate, _ = _gate_bwd(tag + "_gate_bwd", dh, f, gate)
        dhid = _mm(tag + "_dhid", df, w_out, "nt", ACT)
        dw_out = _mm(tag + "_dwout", hid, df, "tn", WIRE)
        du = _swiglu_bwd(tag + "_act_bwd", u, dhid)
        dw_in = _mm(tag + "_dwin", xf, du, "tn", WIRE)
        dxf = _mm(tag + "_dx", du, w_in, "nt")
        s_2 = jnp.concatenate([s_mod, s_mod], axis=0)
        dh, dsh, ds, dg = _normmod_bwd(tag + "_norm_bwd", hin, dxf, dh, g_norm, s_2)
        return dh, dgate, dsh[1:2], ds[1:2], dg[1:2], dw_in, dw_out

    dh, d_g2b, d_sh2b, d_s2b, d_nfg1, g_fin1, g_fout1 = ffn_bwd("ffn1", dh, h3, xf1, u2, hid1, f1, g2b, w_fin[1], w_fout[1], nfg[1:2], s2b)
    dmix1, d_g1b, g_bpw2 = _gate_bwd("conf_gate_bwd", dh, mix1, g1b)
    dsl = _mm("conf_dsl", dmix1, w_pw2, "nt")
    g_pw2 = _mm("conf_dwpw2", sl, dmix1, "tn", WIRE)
    dv_lo, dv_v, g_lng, g_lnb = _ln_silu_bwd("conf_ln_bwd", v_h, v_v, dsl, ln_g, ln_b)
    dv_h = _grid_t(dv_lo, rows_g, GRID_W)
    w_flip = dw_w_full[::-1]
    dgl_h = _strided_conv("conf_conv_h_bwd", dv_h, w_flip[:, :ch], None, rows_g)
    dgl_v = _strided_conv("conf_conv_v_bwd", dv_v, w_flip[:, ch:], None, GRID_W)
    g_dww_h, g_dwb_h = _strided_conv_dw("conf_conv_h_dw", gl_h, dv_h, ck, rows_g)
    g_dww_v, g_dwb_v = _strided_conv_dw("conf_conv_v_dw", gl, dv_v, ck, GRID_W, x_col0=ch)
    g_dww, g_dwb = jnp.concatenate([g_dww_h, g_dww_v], axis=1), jnp.concatenate([g_dwb_h, g_dwb_v], axis=1)
    du1, g_bpw1 = _glu_bwd("conf_glu_bwd", u1, _grid_t(dgl_h, GRID_W, rows_g), dgl_v)
    g_pw1 = _mm("conf_dwpw1", xn1, du1, "tn", WIRE)
    dxn1 = _mm("conf_dx", du1, w_pw1, "nt")
    dh, dsh_, ds_, dg_ = _normmod_bwd("l1_norm_bwd", h2, dxn1, dh, nmg[1:2], jnp.concatenate([s1b, s1b], axis=0))
    d_sh1b, d_s1b, d_nmg1 = dsh_[1:2], ds_[1:2], dg_[1:2]
    dh, d_g2, d_sh2, d_s2, d_nfg0, g_fin0, g_fout0 = ffn_bwd("ffn0", dh, h1, xf0, u0, hid0, f0, g2, w_fin[0], w_fout[0], nfg[0:1], s2)
    dmix0, d_g1, _ = _gate_bwd("ssd_gate_res_bwd", dh, mix0, g1)
    dyn = _mm("ssd_dyn", dmix0, w_ssd_out, "nt")
    g_ssd_out = _mm("ssd_dwout", yn, dmix0, "tn", WIRE)
    dy, dzx, g_normw, ddsk_e = _ssd_gate_bwd("ssd_gate_bwd", dyn, y2, xbc, zx, dskip_e, ssd_norm_w, di, nct, tt)
    ride_scan_bwd = _Comm(scatter=[_flat3(_shard_cols(jnp.stack([g_fin0, g_fin1]))), _flat3(_shard_rows(jnp.stack([g_fout0, g_fout1])))])
    ride_conv_bwd = _Comm(scatter=[_shard_rows(g_ssd_out), _shard_cols(g_pw1), _shard_rows(g_pw2), _shard_cols(g_bpw1),
                                   _shard_cols(g_dww), _shard_cols(g_dwb), _shard_cols(g_lng), _shard_cols(g_lnb), _shard_cols(g_bpw2)])
    dxbc2, ddt2, g_alog2, g_bias2, (_, ffn_r) = _ssd_bwd("ssd_scan_bwd", xbc, dt2, bias2, alog2, dy, hp2, dskip_e, di, tc,
                                                         comm=ride_scan_bwd)
    dzx, g_convw, g_convb, (_, conv_r) = _ssd_conv_bwd("ssd_conv_bwd", zx, dxbc2, conv_w_full, ssd_conv_b, dzx, di, tc,
                                                       comm=ride_conv_bwd)
    ddt_p = jnp.pad(jnp.moveaxis(ddt2, 0, 1).reshape(tt, 2 * h), ((0, 0), (0, LANES - 2 * h))).astype(MXU)
    g_ssd_in = jnp.concatenate([_mm("ssd_dw_zx", xn_all, dzx, "tn", WIRE),
                                _mm("ssd_dw_dt", xn_all, ddt_p, "tn", WIRE)[:, :2 * h]], axis=1)
    dxn, (_, (ssd_in_r, convw_r)) = _mm("ssd_dx_zx", dzx, w_zx, "nt",
                                        comm=_Comm(scatter=[_shard_cols(g_ssd_in), _shard_cols(g_convw)]))
    dxn = _mm("ssd_dx_dt", ddt_p, w_dt, "nt", add=dxn)
    dh_all, dsh_, ds_, dg_ = _normmod_bwd("l0_norm_bwd", h_all, dxn, dh, nmg[0:1], s01, nct)
    grad_x = dh_all[tc:][None]
    d_csh1, d_sh1, d_cs1, d_s1 = dsh_[0:1], dsh_[1:2], ds_[0:1], ds_[1:2]
    d_nmg0 = dg_[0:1] + dg_[1:2]

    z1 = jnp.zeros((1, d), F32)
    dmod = jnp.concatenate([jnp.concatenate([d_sh1, d_s1, d_g1, d_sh2, d_s2, d_g2], axis=1),
                            jnp.concatenate([d_sh1b, d_s1b, d_g1b, d_sh2b, d_s2b, d_g2b], axis=1),
                            jnp.concatenate([d_csh1, d_cs1, z1, z1, z1, z1], axis=1)], axis=0)
    out = {}

    def put(name, res):
        w = args[name]
        out["grad_" + name], out["delta_" + name], out["new_m_" + name], out["new_v_" + name] = (r.reshape(w.shape) for r in res)

    def adam_big(name, slots, comm=None):
        return _adam("adam_" + name, slots, _flat2(args[name]), _flat2(args["m_" + name]), _flat2(args["v_" + name]), comm=comm)

    res, ((dmod_g,), _) = adam_big("ffn_w_in", ffn_r[0], comm=_Comm(gather=[dmod]))
    put("ffn_w_in", res)
    dmod_mine = lax.dynamic_slice_in_dim(dmod_g, me * cw, cw, axis=2)
    dmod16 = jnp.stack([jnp.concatenate([dmod_mine[:, 0], dmod_mine[:, 2]], axis=0),
                        jnp.concatenate([dmod_mine[:, 1], jnp.zeros((NDEV, cw), F32)], axis=0)])
    g_ada_w, dsc_part = _ada_bwd("ada_bwd", cs_all, ada_w, dmod16)
    g_ada_b = dmod[0:2] + jnp.concatenate([dmod[2:3], jnp.zeros((1, 6 * d), F32)], axis=0)

    d_dskip = jnp.sum(ddsk_e.reshape(h, di // h), axis=1)[None, :]
    rep = [dsc_part, g_ada_b, jnp.concatenate([d_nmg0, d_nmg1], axis=0), jnp.concatenate([d_nfg0, d_nfg1], axis=0),
           d_final_g, g_convb, g_bias2[0], g_bias2[1], g_alog2[0], g_alog2[1], d_dskip, g_normw]
    res, (rep_g, _) = _adam("adam_ada_w", _flat2(g_ada_w)[None], _flat2(ada_w), _flat2(m_ada_w), _flat2(v_ada_w),
                            comm=_Comm(gather=rep))
    put("ada_w", res)
    small_r = [convw_r] + list(conv_r[3:])

    for name, slots in zip(["ssd_w_in", "ssd_w_out", "conf_w_pw1", "conf_w_pw2", "ffn_w_out"],
                           [ssd_in_r, conv_r[0], conv_r[1], conv_r[2], ffn_r[1]]):
        put(name, adam_big(name, slots))
    small_names = ["ssd_conv_w", "conf_b_pw1", "conf_dw_w", "conf_dw_b", "conf_ln_g", "conf_ln_b", "conf_b_pw2",
                   "c_ctx", "ada_b", "norm_mix_g", "norm_ffn_g", "final_norm_g", "ssd_conv_b", "ssd_dt_bias_f", "ssd_dt_bias_b",
                   "ssd_a_log_f", "ssd_a_log_b", "ssd_d_skip", "ssd_norm_w"]
    slots = list(small_r) + list(rep_g)

    def as2(a):
        return a.reshape((1, -1)) if a.ndim == 1 else _flat2(a)

    res = _adam_small("adam_small", slots, [as2(args[n]) for n in small_names], [as2(args["m_" + n]) for n in small_names],
                      [as2(args["v_" + n]) for n in small_names], scale=(small_names.index("c_ctx"), c_ctx[None, :]))
    for k, name in enumerate(small_names):
        put(name, [r[k] for r in res])
    return (loss, grad_x, *[out["grad_" + n] for n in names], *[out["delta_" + n] for n in names],
            *[out["new_m_" + n] for n in names], *[out["new_v_" + n] for n in names])


def _flat3(a):
    return a.reshape((a.shape[0], -1, a.shape[-1]))
```

```python
import functools

import jax
import jax.numpy as jnp
from jax import lax
from jax.experimental import pallas as pl
from jax.experimental.pallas import tpu as pltpu

F32 = jnp.float32
MXU = jnp.bfloat16
WIRE = jnp.bfloat16
ACT = jnp.bfloat16
NDEV = 8
AXES = ("x", "y", "c")
SSD_STATE = 128
SSD_CHUNK = 128
GRID_W = 64
EPS = 1e-6
ROW_TILE = 256
LANES = 128
ADAM_LR, ADAM_B1, ADAM_B2, ADAM_EPS, ADAM_WD, ADAM_STEP = 0.001, 0.9, 0.999, 1e-08, 0.01, 10
VMEM_CAP = 56 * 2 ** 20
MESH_ID = pl.DeviceIdType.MESH


def _pick(dim, cands):
    for c in cands:
        if dim % c == 0:
            return c
    return dim


def _nbytes(shape, dtype):
    n = 1
    for s in shape:
        n *= s
    return n * jnp.dtype(dtype).itemsize


def _vmem(nbytes):
    return int(min(VMEM_CAP, max(24 * 2 ** 20, 2 * nbytes + 8 * 2 ** 20)))


def _sigmoid(x):
    return 1.0 / (1.0 + jnp.exp(-x))


def _silu(x):
    return x * _sigmoid(x)


def _dsilu(x):
    s = _sigmoid(x)
    return s * (1.0 + x * (1.0 - s))


def _softplus(x):
    return jnp.maximum(x, 0.0) + jnp.log(1.0 + jnp.exp(-jnp.abs(x)))


def _dot(a, b, dims):
    return lax.dot_general(a.astype(MXU), b.astype(MXU), (dims, ((), ())), preferred_element_type=F32)


NN, NT, TN = ((1,), (0,)), ((1,), (1,)), ((0,), (0,))


def _split(a, parts):
    out = []
    for _ in range(parts):
        p = a.astype(MXU)
        out.append(p)
        a = a - p.astype(F32)
    return out


def _dot_lx(e, a, dims, parts=2):
    return sum(lax.dot_general(e, p, (dims, ((), ())), preferred_element_type=F32) for p in _split(a, parts))


def _dot_rx(a, e, dims, parts=2):
    return sum(lax.dot_general(p, e, (dims, ((), ())), preferred_element_type=F32) for p in _split(a, parts))


class _Comm:
    def __init__(self, gather=(), scatter=()):
        self.gather, self.scatter = list(gather), list(scatter)
        self.ng, self.n = len(self.gather), len(self.gather) + len(self.scatter)
        self.operands = self.gather + self.scatter
        self.specs = [pl.BlockSpec(memory_space=pl.ANY)] * self.n
        self.out_shape = ([jax.ShapeDtypeStruct((NDEV,) + a.shape, a.dtype) for a in self.gather]
                          + [jax.ShapeDtypeStruct(a.shape, a.dtype) for a in self.scatter])
        self.scratch = [pltpu.SemaphoreType.DMA((self.n, 7)), pltpu.SemaphoreType.DMA((self.n, 7)),
                        pltpu.SemaphoreType.DMA((self.n,))]

    def split(self, res):
        return res[:self.ng], res[self.ng:]

    def _copies(self, ins, outs, sems):
        send, recv, loc = sems
        ng, n = self.ng, self.n
        x, y, c = lax.axis_index("x"), lax.axis_index("y"), lax.axis_index("c")
        me, sib = (x, y, c), (x, y, 1 - c)
        chips = [(1 - x, y), (x, 1 - y), (1 - x, 1 - y)]

        def slot(p):
            return 4 * p[0] + 2 * p[1] + p[2]

        def rcopy(a, k, src, dst, to):
            return functools.partial(pltpu.make_async_remote_copy, src_ref=src, dst_ref=dst, send_sem=send.at[a, k],
                                     recv_sem=recv.at[a, k], device_id=to, device_id_type=MESH_ID)

        local = [functools.partial(pltpu.make_async_copy, ins[a] if a < ng else ins[a].at[slot(me)], outs[a].at[slot(me)],
                                   loc.at[a]) for a in range(n)]
        rel = [(fx, fy, fc) for fx in (0, 1) for fy in (0, 1) for fc in (0, 1)][1:]
        first, landed, passed = [], [], []
        for a in range(ng, n):
            for k, (fx, fy, fc) in enumerate(rel):
                p = (1 - x if fx else x, 1 - y if fy else y, 1 - c if fc else c)
                first.append(rcopy(a, k, ins[a].at[slot(p)], outs[a].at[slot(me)], p))
                blk = outs[a].at[slot(p)]
                landed.append(rcopy(a, k, blk, blk, me))
        for a in range(ng):
            dst = outs[a].at[slot(me)]
            first.append(rcopy(a, 0, ins[a], dst, sib))
            first += [rcopy(a, 1 + j, ins[a], dst, (*ch, c)) for j, ch in enumerate(chips)]
            blk = outs[a].at[slot(sib)]
            landed.append(rcopy(a, 0, blk, blk, me))
            for j, ch in enumerate(chips):
                blk = outs[a].at[slot((*ch, c))]
                passed.append((rcopy(a, 1 + j, blk, blk, me), rcopy(a, 4 + j, blk, blk, sib)))
                blk = outs[a].at[slot((*ch, 1 - c))]
                landed.append(rcopy(a, 4 + j, blk, blk, me))
        return local, first, passed, landed

    def start(self, ins, outs, sems):
        local, first, _, _ = self._copies(ins, outs, sems)
        for make in local + first:
            make().start()

    def finish(self, ins, outs, sems):
        local, first, passed, landed = self._copies(ins, outs, sems)
        onward = []
        for arrived, forward in passed:
            arrived().wait_recv()
            onward.append(forward())
            onward[-1].start()
        for make in landed:
            make().wait_recv()
        for make in first:
            make().wait_send()
        for cp in onward:
            cp.wait_send()
        for make in local:
            make().wait()


def _carry(body, comm, n_in, n_out, grid):
    if comm is None:
        return body
    n = comm.n

    def wrapped(*refs):
        own_in, c_in = refs[:n_in], refs[n_in:n_in + n]
        own_out, c_out = refs[n_in + n:n_in + n + n_out], refs[n_in + n + n_out:n_in + 2 * n + n_out]
        own_scr, sems = refs[n_in + 2 * n + n_out:-3], refs[-3:]
        ids = [pl.program_id(ax) for ax in range(len(grid))]
        first, last = ids[0] == 0, ids[0] == grid[0] - 1
        for ax in range(1, len(grid)):
            first, last = first & (ids[ax] == 0), last & (ids[ax] == grid[ax] - 1)

        @pl.when(first)
        def _():
            comm.start(c_in, c_out, sems)

        body(*own_in, *own_out, *own_scr)

        @pl.when(last)
        def _():
            comm.finish(c_in, c_out, sems)

    return wrapped


def _exchange(name, gather, scatter=()):
    comm = _Comm(gather, scatter)
    n = comm.n

    def body(*refs):
        ins, outs, sems = refs[:n], refs[n:2 * n], refs[2 * n:]
        comm.start(ins, outs, sems)
        comm.finish(ins, outs, sems)

    res = pl.pallas_call(body, name=name, out_shape=comm.out_shape, in_specs=comm.specs, out_specs=comm.specs,
                         scratch_shapes=comm.scratch)(*comm.operands)
    return comm.split(res)


def _hbm(a):
    return pltpu.with_memory_space_constraint(a, pltpu.HBM)


def _divs(dim, mult):
    return [dim] + [dim // parts for parts in range(2, dim // mult + 1) if dim % parts == 0 and (dim // parts) % mult == 0]


MM_VMEM_BUDGET = 40 * 2 ** 20
GRID_STEP_US = 0.35
HBM_BYTES_PER_US = 3.0e6


def _mm_tiles(m, n, k, sizes, mode, has_add):
    sa, sb, so = sizes
    sub = 16
    best = None
    for tk in _divs(k, LANES):
        for tn in _divs(n, LANES):
            for tm in _divs(m, LANES if mode == "tn" else sub):
                nk = k // tk
                out_t = tm * tn
                est = (2 * (tm * tk * sa + tk * tn * sb) + 2 * out_t * so + 2 * (tm * tk + tk * tn) + 4 * out_t
                       + (4 * out_t if nk > 1 else 0) + (8 * out_t if has_add else 0))
                if est > MM_VMEM_BUDGET:
                    continue
                steps = (m // tm) * (n // tn) * nk
                cost = steps * GRID_STEP_US + (tm * tk * sa + tk * tn * sb + out_t * so) / HBM_BYTES_PER_US
                if best is None or cost < best[0]:
                    best = (cost, tm, tn, tk, est)
    assert best is not None, (m, n, k)
    return best[1:]


def _mm(name, a, b, mode, out_dtype=F32, bias=None, add=None, comm=None):
    if mode == "nn":
        (m, k), (k2, n) = a.shape, b.shape
    elif mode == "nt":
        (m, k), (n, k2) = a.shape, b.shape
    else:
        (k, m), (k2, n) = a.shape, b.shape
    assert k == k2, (name, a.shape, b.shape)
    sizes = (a.dtype.itemsize, b.dtype.itemsize, jnp.dtype(out_dtype).itemsize)
    tm, tn, tk, est = _mm_tiles(m, n, k, sizes, mode, add is not None)
    nk = k // tk
    dims = {"nn": NN, "nt": NT, "tn": TN}[mode]
    a_spec = pl.BlockSpec((tk, tm), lambda i, j, kk: (kk, i)) if mode == "tn" else pl.BlockSpec((tm, tk), lambda i, j, kk: (i, kk))
    b_spec = pl.BlockSpec((tn, tk), lambda i, j, kk: (j, kk)) if mode == "nt" else pl.BlockSpec((tk, tn), lambda i, j, kk: (kk, j))
    extra, extra_specs = [], []
    if bias is not None:
        extra.append(bias)
        extra_specs.append(pl.BlockSpec((1, tn), lambda i, j, kk: (0, j)))
    if add is not None:
        extra.append(add)
        extra_specs.append(pl.BlockSpec((tm, tn), lambda i, j, kk: (i, j)))

    def finish(r, extras, o_ref):
        for e in extras:
            r = r + e[...].astype(F32)
        o_ref[...] = r.astype(o_ref.dtype)

    def body_acc(*refs):
        a_ref, b_ref = refs[:2]
        o_ref, acc = refs[-2:]
        kk = pl.program_id(2)

        @pl.when(kk == 0)
        def _():
            acc[...] = jnp.zeros_like(acc)

        acc[...] += _dot(a_ref[...], b_ref[...], dims)

        @pl.when(kk == nk - 1)
        def _():
            finish(acc[...], refs[2:-2], o_ref)

    def body_one(*refs):
        finish(_dot(refs[0][...], refs[1][...], dims), refs[2:-1], refs[-1])

    cm = comm if comm is not None else _Comm()
    grid = (m // tm, n // tn, nk)
    res = pl.pallas_call(
        _carry(body_acc if nk > 1 else body_one, comm, 2 + len(extra), 1, grid), name=name, grid=grid,
        out_shape=[pltpu.HBM((m, n), out_dtype)] + cm.out_shape,
        in_specs=[a_spec, b_spec] + extra_specs + cm.specs,
        out_specs=[pl.BlockSpec((tm, tn), lambda i, j, kk: (i, j))] + cm.specs,
        scratch_shapes=([pltpu.VMEM((tm, tn), F32)] if nk > 1 else []) + (cm.scratch if comm is not None else []),
        compiler_params=pltpu.CompilerParams(
            dimension_semantics=("parallel", "parallel", "arbitrary") if comm is None else ("arbitrary",) * 3,
            vmem_limit_bytes=int(min(VMEM_CAP, est + 12 * 2 ** 20))),
    )(*[_hbm(v) for v in (a, b, *extra)], *cm.operands)
    return res[0] if comm is None else (res[0], cm.split(res[1:]))


def _ri(arr, w=None, cb=0, ro=0, lead=None):
    return (arr, arr.shape[-1] if w is None else w, cb, ro, lead)


def _rowwise(name, fn, nrows, row_ins, bc_ins, outs, accs=(), comm=None):
    tr = min(ROW_TILE, nrows)
    assert nrows % tr == 0
    in_specs = []
    for (arr, w, cb, ro, lead) in row_ins:
        if lead is None:
            in_specs.append(pl.BlockSpec((tr, w), lambda i, cb=cb, ro=ro: (jnp.maximum(i + ro, 0), cb)))
        else:
            in_specs.append(pl.BlockSpec((None, tr, w), lambda i, cb=cb, ro=ro, lead=lead: (lead, jnp.maximum(i + ro, 0), cb)))
    for arr in bc_ins:
        in_specs.append(pl.BlockSpec(arr.shape, lambda i, nd=arr.ndim: (0,) * nd))
    outs = [o if len(o) == 4 else (o[0], o[1], o[0], 0) for o in outs]
    out_shape = [pltpu.HBM((nrows, total), dt) for _, dt, total, _ in outs] + [pltpu.HBM(s, F32) for s in accs]
    out_specs = ([pl.BlockSpec((tr, c), lambda i, cb=cb: (i, cb)) for c, _, _, cb in outs]
                 + [pl.BlockSpec(s, lambda i: (0, 0)) for s in accs])
    nr, nb, no = len(row_ins), len(bc_ins), len(outs)

    def body(*refs):
        i = pl.program_id(0)
        rows = [r[...].astype(F32) for r in refs[:nr]]
        bcs = [r[...] for r in refs[nr:nr + nb]]
        o, a = fn(rows, bcs, i)
        for ref, val in zip(refs[nr + nb:nr + nb + no], o):
            ref[...] = val.astype(ref.dtype)
        for ref, val in zip(refs[nr + nb + no:], a):
            @pl.when(i == 0)
            def _(ref=ref, val=val):
                ref[...] = val

            @pl.when(i > 0)
            def _(ref=ref, val=val):
                ref[...] += val

    est = sum(tr * w * arr.dtype.itemsize for (arr, w, _, _, _) in row_ins) + sum(tr * o[0] * 4 for o in outs)
    cm = comm if comm is not None else _Comm()
    nout = no + len(accs)
    res = pl.pallas_call(
        _carry(body, comm, nr + nb, nout, (nrows // tr,)), name=name, grid=(nrows // tr,), out_shape=out_shape + cm.out_shape,
        in_specs=in_specs + cm.specs, out_specs=out_specs + cm.specs, scratch_shapes=cm.scratch if comm is not None else [],
        compiler_params=pltpu.CompilerParams(dimension_semantics=("arbitrary",), vmem_limit_bytes=_vmem(3 * est)),
    )(*[_hbm(r[0]) for r in row_ins], *[_hbm(v) for v in bc_ins], *cm.operands)
    if comm is None:
        return res[:no], res[no:]
    return res[:no], res[no:nout], cm.split(res[nout:])


def _colsum(v):
    return jnp.sum(v, axis=0, keepdims=True)


def _normmod_fwd(name, h, g, s, sh, nct=0, comm=None):
    d = h.shape[1]

    def fn(rows, bcs, i):
        hh, (g_, s_, sh_) = rows[0], bcs
        s1 = jnp.where(i < nct, s_[0:1], s_[1:2])
        sh1 = jnp.where(i < nct, sh_[0:1], sh_[1:2])
        r = lax.rsqrt(jnp.mean(hh * hh, axis=-1, keepdims=True) + EPS)
        return [hh * r * g_ * (1.0 + s1) + sh1], []

    res = _rowwise(name, fn, h.shape[0], [_ri(h)], [g, s, sh], [(d, MXU)], comm=comm)
    return res[0][0] if comm is None else (res[0][0], res[2])


def _normmod_bwd(name, h, dxn, dres, g, s, nct=0):
    d = h.shape[1]

    def fn(rows, bcs, i):
        hh, dx, dr = rows
        g_, s_ = bcs
        ctx = i < nct
        s1 = jnp.where(ctx, s_[0:1], s_[1:2])
        r = lax.rsqrt(jnp.mean(hh * hh, axis=-1, keepdims=True) + EPS)
        hr = hh * r
        dy = dx * (1.0 + s1)
        u = dy * g_
        dh = r * u - hr * (r * r) * jnp.mean(u * hh, axis=-1, keepdims=True)
        dh = dh + jnp.where(ctx, 0.0, dr)

        def seg(v):
            v = _colsum(v)
            return jnp.concatenate([jnp.where(ctx, v, 0.0), jnp.where(ctx, 0.0, v)], axis=0)

        return [dh], [seg(dx), seg(dx * hr * g_), seg(dy * hr)]

    (dh,), (dsh, ds, dg) = _rowwise(name, fn, h.shape[0], [_ri(h), _ri(dxn), _ri(dres, ro=-nct)], [g, s],
                                    [(d, F32)], [(2, d)] * 3)
    return dh, dsh, ds, dg


def _resnorm_fwd(name, h, y, gate, g, s, sh):
    d = h.shape[1]

    def fn(rows, bcs, i):
        hh, yy = rows
        gate_, g_, s_, sh_ = bcs
        hn = hh + gate_ * yy
        r = lax.rsqrt(jnp.mean(hn * hn, axis=-1, keepdims=True) + EPS)
        return [hn, hn * r * g_ * (1.0 + s_) + sh_], []

    return _rowwise(name, fn, h.shape[0], [_ri(h), _ri(y)], [gate, g, s, sh], [(d, F32), (d, MXU)])[0]


def _gate_bwd(name, dh, y, gate):
    d = dh.shape[1]

    def fn(rows, bcs, i):
        dd, yy = rows
        dy = dd * bcs[0]
        return [dy], [_colsum(dd * yy), _colsum(dy)]

    (dy,), (dgate, dbias) = _rowwise(name, fn, dh.shape[0], [_ri(dh), _ri(y)], [gate], [(d, MXU)], [(1, d)] * 2)
    return dy, dgate, dbias


def _swiglu_fwd(name, u):
    f = u.shape[1] // 2

    def fn(rows, bcs, i):
        return [_silu(rows[0]) * rows[1]], []

    return _rowwise(name, fn, u.shape[0], [_ri(u, f, 0), _ri(u, f, 1)], [], [(f, MXU)])[0][0]


def _swiglu_bwd(name, u, dhid):
    f = u.shape[1] // 2

    def fn(rows, bcs, i):
        a, b, dd = rows
        return [jnp.concatenate([dd * b * _dsilu(a), dd * _silu(a)], axis=1)], []

    return _rowwise(name, fn, u.shape[0], [_ri(u, f, 0), _ri(u, f, 1), _ri(dhid)], [], [(2 * f, MXU)])[0][0]


def _glu_fwd(name, u):
    d = u.shape[1] // 2

    def fn(rows, bcs, i):
        return [rows[0] * _sigmoid(rows[1])], []

    return _rowwise(name, fn, u.shape[0], [_ri(u, d, 0), _ri(u, d, 1)], [], [(d, F32)])[0][0]


def _glu_bwd(name, u, dgl_lo, dgl_hi):
    d = u.shape[1] // 2

    def fn(rows, bcs, i):
        a, b = rows[:2]
        dd = jnp.concatenate(rows[2:], axis=1)
        sg = _sigmoid(b)
        du = jnp.concatenate([dd * sg, dd * a * sg * (1.0 - sg)], axis=1)
        return [du], [_colsum(du)]

    (du,), (db,) = _rowwise(name, fn, u.shape[0], [_ri(u, d, 0), _ri(u, d, 1), _ri(dgl_lo), _ri(dgl_hi)], [], [(2 * d, MXU)],
                            [(1, 2 * d)])
    return du, db


def _ln_silu_fwd(name, v_lo, v_hi, g, b):
    d = 2 * v_lo.shape[1]

    def fn(rows, bcs, i):
        vv = jnp.concatenate(rows, axis=1)
        mu = jnp.mean(vv, axis=-1, keepdims=True)
        xc = vv - mu
        rs = lax.rsqrt(jnp.mean(xc * xc, axis=-1, keepdims=True) + EPS)
        return [_silu(xc * rs * bcs[0] + bcs[1])], []

    return _rowwise(name, fn, v_lo.shape[0], [_ri(v_lo), _ri(v_hi)], [g, b], [(d, MXU)])[0][0]


def _ln_silu_bwd(name, v_lo, v_hi, ds, g, b):
    ch = v_lo.shape[1]

    def fn(rows, bcs, i):
        vv, dd = jnp.concatenate(rows[:2], axis=1), rows[2]
        mu = jnp.mean(vv, axis=-1, keepdims=True)
        xc = vv - mu
        rs = lax.rsqrt(jnp.mean(xc * xc, axis=-1, keepdims=True) + EPS)
        xh = xc * rs
        dln = dd * _dsilu(xh * bcs[0] + bcs[1])
        dxh = dln * bcs[0]
        dv = rs * (dxh - jnp.mean(dxh, axis=-1, keepdims=True) - xh * jnp.mean(dxh * xh, axis=-1, keepdims=True))
        return [dv[:, :ch], dv[:, ch:]], [_colsum(dln * xh), _colsum(dln)]

    (dv_lo, dv_hi), (dg, db) = _rowwise(name, fn, v_lo.shape[0], [_ri(v_lo), _ri(v_hi), _ri(ds)], [g, b],
                                        [(ch, F32), (ch, F32)], [(1, 2 * ch)] * 2)
    return dv_lo, dv_hi, dg, db


def _final_loss(name, h, f, target, gate, gf):
    d = h.shape[1]

    def fn(rows, bcs, i):
        hh, ff, tg = rows
        gate_, g_ = bcs
        hn = hh + gate_ * ff
        r = lax.rsqrt(jnp.mean(hn * hn, axis=-1, keepdims=True) + EPS)
        hr = hn * r
        err = hr * g_ - tg
        dout = err * (1.0 / d)
        u = dout * g_
        dh = r * u - hr * (r * r) * jnp.mean(u * hn, axis=-1, keepdims=True)
        sq = jnp.sum(_colsum(err * err), axis=1, keepdims=True)
        return [dh], [jnp.broadcast_to(sq, (1, LANES)), _colsum(dout * hr)]

    (dh,), (sq, dgf) = _rowwise(name, fn, h.shape[0], [_ri(h), _ri(f), _ri(target)], [gate, gf], [(d, F32)], [(1, LANES), (1, d)])
    return dh, sq, dgf


GAP = 8


def _gapped(ref_rows, buf, tc, tt):
    cb = buf.shape[1]
    zero = jnp.zeros((GAP, cb), F32)
    buf[0:GAP, :] = zero
    buf[GAP + tc:2 * GAP + tc, :] = zero
    buf[2 * GAP + tt:, :] = zero
    buf[GAP:GAP + tc, :] = ref_rows[0:tc]
    buf[2 * GAP + tc:2 * GAP + tt, :] = ref_rows[tc:tt]
    return buf[...]


def _ungapped(v, tc, tt):
    return jnp.concatenate([v[GAP:GAP + tc], v[2 * GAP + tc:2 * GAP + tt]], axis=0)


def _shift_rows(x, o):
    return x if o == 0 else pltpu.roll(x, (-o) % x.shape[0], 0)


def _ssd_conv_fwd(name, zx, w, b, di, tc, comm=None):
    tt, kc, cd = zx.shape[0], w.shape[0], w.shape[1]
    cb = _pick(cd, (LANES,))
    off = di // cb
    assert kc // 2 < GAP and tc % GAP == 0 and tt % GAP == 0

    def body(x_ref, w_ref, b_ref, o_ref, xp):
        x = _gapped(x_ref[...].astype(F32), xp, tc, tt)
        acc = jnp.broadcast_to(b_ref[...], x.shape)
        for k in range(kc):
            acc = acc + w_ref[k:k + 1, :] * _shift_rows(x, k - kc // 2)
        o_ref[...] = _ungapped(_silu(acc), tc, tt).astype(o_ref.dtype)

    cm = comm if comm is not None else _Comm()
    res = pl.pallas_call(
        _carry(body, comm, 3, 1, (cd // cb,)), name=name, grid=(cd // cb,), out_shape=[pltpu.HBM((tt, cd), ACT)] + cm.out_shape,
        in_specs=[pl.BlockSpec((tt, cb), lambda j: (0, j + off)), pl.BlockSpec((kc, cb), lambda j: (0, j)),
                  pl.BlockSpec((1, cb), lambda j: (0, j))] + cm.specs,
        out_specs=[pl.BlockSpec((tt, cb), lambda j: (0, j))] + cm.specs,
        scratch_shapes=[pltpu.VMEM((tt + 3 * GAP, cb), F32)] + (cm.scratch if comm is not None else []),
        compiler_params=pltpu.CompilerParams(dimension_semantics=("arbitrary",), vmem_limit_bytes=_vmem(5 * tt * cb * 4)),
    )(_hbm(zx), _hbm(w), _hbm(b), *cm.operands)
    return res[0], cm.split(res[1:])


def _ssd_conv_bwd(name, zx, dact2, w, b, dzx, di, tc, comm=None):
    tt, kc, cd = zx.shape[0], w.shape[0], w.shape[1]
    cb = _pick(cd, (LANES,))
    off = di // cb

    def body(x_ref, d0_ref, d1_ref, w_ref, b_ref, _, dx_ref, dw_ref, db_ref, xp, dp):
        x = _gapped(x_ref[...].astype(F32), xp, tc, tt)
        dact = _gapped(d0_ref[...].astype(F32) + d1_ref[...].astype(F32), dp, tc, tt)
        pre = jnp.broadcast_to(b_ref[...], x.shape)
        for k in range(kc):
            pre = pre + w_ref[k:k + 1, :] * _shift_rows(x, k - kc // 2)
        dpre = dact * _dsilu(pre)
        dx = jnp.zeros_like(x)
        for k in range(kc):
            o = k - kc // 2
            dx = dx + w_ref[k:k + 1, :] * _shift_rows(dpre, -o)
            dw_ref[k:k + 1, :] = _colsum(dpre * _shift_rows(x, o))
        dx_ref[...] = _ungapped(dx, tc, tt).astype(dx_ref.dtype)
        db_ref[...] = _colsum(dpre)

    cm = comm if comm is not None else _Comm()
    res = pl.pallas_call(
        _carry(body, comm, 6, 3, (cd // cb,)), name=name, grid=(cd // cb,),
        out_shape=[pltpu.HBM(dzx.shape, dzx.dtype), pltpu.HBM((kc, cd), F32), pltpu.HBM((1, cd), F32)] + cm.out_shape,
        in_specs=[pl.BlockSpec((tt, cb), lambda j: (0, j + off)), pl.BlockSpec((None, tt, cb), lambda j: (0, 0, j)),
                  pl.BlockSpec((None, tt, cb), lambda j: (1, 0, j)), pl.BlockSpec((kc, cb), lambda j: (0, j)),
                  pl.BlockSpec((1, cb), lambda j: (0, j)), pl.BlockSpec(memory_space=pl.ANY)] + cm.specs,
        out_specs=[pl.BlockSpec((tt, cb), lambda j: (0, j + off)), pl.BlockSpec((kc, cb), lambda j: (0, j)),
                   pl.BlockSpec((1, cb), lambda j: (0, j))] + cm.specs,
        input_output_aliases={5: 0},
        scratch_shapes=[pltpu.VMEM((tt + 3 * GAP, cb), F32)] * 2 + (cm.scratch if comm is not None else []),
        compiler_params=pltpu.CompilerParams(dimension_semantics=("arbitrary",), vmem_limit_bytes=_vmem(10 * tt * cb * 4)),
    )(_hbm(zx), _hbm(dact2), _hbm(dact2), _hbm(w), _hbm(b), _hbm(dzx), *cm.operands)
    return res[0], res[1], res[2], cm.split(res[3:])


def _strided_conv(name, x, w, b, stride, x_col0=0):
    t, ch = x.shape[0], w.shape[1]
    kk = w.shape[0]
    pad = (kk // 2) * stride
    cb = _pick(ch, (LANES,))
    has_b = b is not None

    def body(*refs):
        x_ref, w_ref = refs[:2]
        o_ref, xp = refs[-2:]
        xp[0:pad, :] = jnp.zeros((pad, cb), F32)
        xp[pad + t:, :] = jnp.zeros((pad, cb), F32)
        xp[pad:pad + t, :] = x_ref[...]
        acc = jnp.broadcast_to(refs[2][...], (t, cb)) if has_b else jnp.zeros((t, cb), F32)
        for k in range(kk):
            acc = acc + w_ref[k:k + 1, :] * xp[k * stride:k * stride + t, :]
        o_ref[...] = acc

    xoff = x_col0 // cb
    ins, specs = [x, w], [pl.BlockSpec((t, cb), lambda j: (0, j + xoff)), pl.BlockSpec((kk, cb), lambda j: (0, j))]
    if has_b:
        ins.append(b)
        specs.append(pl.BlockSpec((1, cb), lambda j: (0, j)))
    return pl.pallas_call(
        body, name=name, grid=(ch // cb,), out_shape=pltpu.HBM((t, ch), F32), in_specs=specs,
        out_specs=pl.BlockSpec((t, cb), lambda j: (0, j)), scratch_shapes=[pltpu.VMEM((t + 2 * pad, cb), F32)],
        compiler_params=pltpu.CompilerParams(dimension_semantics=("parallel",), vmem_limit_bytes=_vmem(6 * t * cb * 4)),
    )(*[_hbm(v) for v in ins])


def _strided_conv_dw(name, x, dv, kk, stride, x_col0=0):
    t, ch = dv.shape
    pad = (kk // 2) * stride
    cb = _pick(ch, (LANES,))

    def body(x_ref, d_ref, dw_ref, db_ref, xp):
        xp[0:pad, :] = jnp.zeros((pad, cb), F32)
        xp[pad + t:, :] = jnp.zeros((pad, cb), F32)
        xp[pad:pad + t, :] = x_ref[...]
        d = d_ref[...]
        for k in range(kk):
            dw_ref[k:k + 1, :] = _colsum(d * xp[k * stride:k * stride + t, :])
        db_ref[...] = _colsum(d)

    blk = pl.BlockSpec((t, cb), lambda j: (0, j))
    xoff = x_col0 // cb
    return pl.pallas_call(
        body, name=name, grid=(ch // cb,), out_shape=[pltpu.HBM((kk, ch), F32), pltpu.HBM((1, ch), F32)],
        in_specs=[pl.BlockSpec((t, cb), lambda j: (0, j + xoff)), blk], out_specs=[pl.BlockSpec((kk, cb), lambda j: (0, j)), pl.BlockSpec((1, cb), lambda j: (0, j))],
        scratch_shapes=[pltpu.VMEM((t + 2 * pad, cb), F32)],
        compiler_params=pltpu.CompilerParams(dimension_semantics=("parallel",), vmem_limit_bytes=_vmem(6 * t * cb * 4)),
    )(_hbm(x), _hbm(dv))


def _grid_t(a, n1, n2):
    return a.reshape(n1, n2, a.shape[-1]).swapaxes(0, 1).reshape(n1 * n2, a.shape[-1])


def _chunk_order(d, i, ncc, nc):
    back = jnp.where(i < ncc, ncc - 1 - i, nc - 1 - (i - ncc))
    return jnp.where(d == 0, i, back)


def _ssd_chunk_setup(d, dt_raw, bias, a_log, q, h, di):
    p = di // h
    dt = _softplus(dt_raw + bias)
    a_neg = -jnp.exp(a_log)
    delta = dt * a_neg
    r = lax.broadcasted_iota(jnp.int32, (q, q), 0)
    c = lax.broadcasted_iota(jnp.int32, (q, q), 1)
    sgn = 1 - 2 * d
    mask = (r - c) * sgn >= 0
    mask_t = (c - r) * sgn >= 0
    a = _dot_lx(mask.astype(MXU), delta, NN, parts=3)
    tot = _colsum(delta)
    ea, dte, cd = jnp.exp(a), jnp.exp(tot - a), jnp.exp(tot)
    hh = lax.broadcasted_iota(jnp.int32, (h, di), 0)
    cc = lax.broadcasted_iota(jnp.int32, (h, di), 1)
    e = (cc // p == hh).astype(MXU)
    ex = _dot_rx(jnp.concatenate([dt, ea, dte, jnp.broadcast_to(cd, (8, h))], axis=0), e, NN)
    eye = (lax.broadcasted_iota(jnp.int32, (h, h), 0) == lax.broadcasted_iota(jnp.int32, (h, h), 1)).astype(MXU)
    a_t = _dot_lx(eye, a, NT, parts=3)
    return dict(dt=dt, a_neg=a_neg, a=a, a_t=a_t, mask=mask, mask_t=mask_t, e=e,
                dt_e=ex[0:q], ea_e=ex[q:2 * q], dte_e=ex[2 * q:3 * q], cd_e=ex[3 * q:3 * q + 1])


def _pick_heads(r, q, hpg, p):
    lane = lax.broadcasted_iota(jnp.int32, (q, hpg * p), 1) // p
    out = jnp.zeros((q, hpg * p), F32)
    for j in range(hpg):
        out = out + jnp.where(lane == j, r[j * q:(j + 1) * q], 0.0)
    return out


def _ssd_fwd(name, xbc, dt2, bias2, alog2, di, tc, comm=None):
    tt, cd = xbc.shape
    h = dt2.shape[-1]
    q, n = SSD_CHUNK, SSD_STATE
    gn = (cd - di) // 2
    g = gn // n
    hpg, p = h // g, di // h
    gp = hpg * p
    nc, ncc = tt // q, tc // q
    assert di % gn == 0

    def body(x_ref, b_ref, c_ref, dt_ref, bias_ref, alog_ref, y_ref, hp_ref, ht):
        d, i = pl.program_id(0), pl.program_id(1)

        @pl.when(i == 0)
        def _():
            ht[...] = jnp.zeros_like(ht)

        s = _ssd_chunk_setup(d, dt_ref[...], bias_ref[...], alog_ref[...], q, h, di)
        xd = x_ref[...].astype(F32) * s["dt_e"]
        hp_ref[...] = ht[...].astype(hp_ref.dtype)
        for gi in range(g):
            bg, cg = b_ref[:, gi * n:(gi + 1) * n].astype(MXU), c_ref[:, gi * n:(gi + 1) * n].astype(MXU)
            sl = slice(gi * gp, (gi + 1) * gp)
            sc = _dot(cg, bg, NT)
            ms = []
            for j in range(hpg):
                hd = gi * hpg + j
                seg = s["a"][:, hd:hd + 1] - s["a_t"][hd:hd + 1, :]
                ms.append(sc * jnp.exp(jnp.where(s["mask"], seg, -jnp.inf)))
            xdg = xd[:, sl]
            ydiag = _pick_heads(_dot(jnp.concatenate(ms, axis=0), xdg, NN), q, hpg, p)
            htg = ht[:, sl]
            y_ref[:, sl] = ydiag + _dot(cg, htg, NN) * s["ea_e"][:, sl]
            ht[:, sl] = s["cd_e"][:, sl] * htg + _dot(bg, xdg * s["dte_e"][:, sl], TN)

    def cidx(d, i):
        return _chunk_order(d, i, ncc, nc)

    cm = comm if comm is not None else _Comm()
    res = pl.pallas_call(
        _carry(body, comm, 6, 2, (2, nc)), name=name, grid=(2, nc),
        out_shape=[pltpu.HBM((2, tt, di), F32), pltpu.HBM((2, nc, n, di), ACT)] + cm.out_shape,
        in_specs=[pl.BlockSpec((q, di), lambda d, i: (cidx(d, i), 0)),
                  pl.BlockSpec((q, gn), lambda d, i: (cidx(d, i), di // gn)),
                  pl.BlockSpec((q, gn), lambda d, i: (cidx(d, i), di // gn + 1)),
                  pl.BlockSpec((None, q, h), lambda d, i: (d, cidx(d, i), 0)),
                  pl.BlockSpec((None, 1, h), lambda d, i: (d, 0, 0)),
                  pl.BlockSpec((None, 1, h), lambda d, i: (d, 0, 0))] + cm.specs,
        out_specs=[pl.BlockSpec((None, q, di), lambda d, i: (d, cidx(d, i), 0)),
                   pl.BlockSpec((None, None, n, di), lambda d, i: (d, cidx(d, i), 0, 0))] + cm.specs,
        scratch_shapes=[pltpu.VMEM((n, di), F32)] + (cm.scratch if comm is not None else []),
        compiler_params=pltpu.CompilerParams(dimension_semantics=("arbitrary", "arbitrary"), vmem_limit_bytes=_vmem(16 * q * di * 4)),
    )(*[_hbm(v) for v in (xbc, xbc, xbc, dt2, bias2, alog2)], *cm.operands)
    return res[0], res[1], cm.split(res[2:])


def _ssd_bwd(name, xbc, dt2, bias2, alog2, dy, hp2, dskip_e, di, tc, comm=None):
    tt, cd = xbc.shape
    h = dt2.shape[-1]
    q, n = SSD_CHUNK, SSD_STATE
    gn = (cd - di) // 2
    g = gn // n
    hpg, p = h // g, di // h
    gp = hpg * p
    nc, ncc = tt // q, tc // q

    def body(x_ref, b_ref, c_ref, dt_ref, bias_ref, alog_ref, dy_ref, hp_ref, dsk_ref,
             dxbc_ref, ddt_ref, dalog_ref, dbias_ref, dht, dxd, off):
        d, i = pl.program_id(0), pl.program_id(1)

        @pl.when(i == 0)
        def _():
            dht[...] = jnp.zeros_like(dht)
            dalog_ref[...] = jnp.zeros_like(dalog_ref)
            dbias_ref[...] = jnp.zeros_like(dbias_ref)

        s = _ssd_chunk_setup(d, dt_ref[...], bias_ref[...], alog_ref[...], q, h, di)
        x, dyc = x_ref[...].astype(F32), dy_ref[...]
        xd = x * s["dt_e"]
        dyea = dyc * s["ea_e"]
        xdte = xd * s["dte_e"]
        lane = lax.broadcasted_iota(jnp.int32, (q, gp), 1) // p
        lane_h = lax.broadcasted_iota(jnp.int32, (q, h), 1)
        da_d = jnp.zeros((q, h), F32)
        last_e = []
        for gi in range(g):
            bg, cg = b_ref[:, gi * n:(gi + 1) * n].astype(MXU), c_ref[:, gi * n:(gi + 1) * n].astype(MXU)
            sl = slice(gi * gp, (gi + 1) * gp)
            sc, sct = _dot(cg, bg, NT), _dot(bg, cg, NT)
            dyg, xdg = dyc[:, sl], xd[:, sl]
            htg, dhtg = hp_ref[:, sl].astype(F32), dht[:, sl]
            dystack = jnp.concatenate([jnp.where(lane == j, dyg, 0.0) for j in range(hpg)], axis=0)
            xdstack = jnp.concatenate([jnp.where(lane == j, xdg, 0.0) for j in range(hpg)], axis=0)
            gs = _dot(dystack, xdg, NT)
            gst = _dot(xdstack, dyg, NT)
            ds = jnp.zeros((q, q), F32)
            mts = []
            for j in range(hpg):
                hd = gi * hpg + j
                col, rw = s["a"][:, hd:hd + 1], s["a_t"][hd:hd + 1, :]
                gl = gs[j * q:(j + 1) * q] * jnp.exp(jnp.where(s["mask"], col - rw, -jnp.inf))
                ds = ds + gl
                mt = sct * jnp.exp(jnp.where(s["mask_t"], rw - col, -jnp.inf))
                mts.append(mt)
                da_j = jnp.sum(gl * sc, axis=1, keepdims=True) - jnp.sum(gst[j * q:(j + 1) * q] * mt, axis=1, keepdims=True)
                da_d = da_d + jnp.where(lane_h == hd, da_j, 0.0)
            dxd_diag = _pick_heads(_dot(jnp.concatenate(mts, axis=0), dyg, NN), q, hpg, p)
            z = _dot(bg, dhtg, NN) * s["dte_e"][:, sl]
            yoff = _dot(cg, htg, NN) * s["ea_e"][:, sl]
            off[:, sl] = dyg * yoff - xdg * z
            dxd[:, sl] = dxd_diag + z
            dxbc_ref[:, di + gi * n:di + (gi + 1) * n] = (_dot(ds, cg, TN) + _dot(xdte[:, sl], dhtg, NT)).astype(dxbc_ref.dtype)
            dxbc_ref[:, di + gn + gi * n:di + gn + (gi + 1) * n] = (_dot(ds, bg, NN)
                                                                    + _dot(dyea[:, sl], htg, NT)).astype(dxbc_ref.dtype)
            last_e.append(s["cd_e"][:, sl] * _colsum(dhtg * htg) + _colsum(xdg * z))
            dht[:, sl] = s["cd_e"][:, sl] * dhtg + _dot(cg, dyea[:, sl], TN)
        dxd_all = dxd[...]
        last = jnp.concatenate(last_e, axis=1)
        da = da_d + _dot_rx(off[...], s["e"], NT)
        last_h = _dot_rx(jnp.broadcast_to(last, (8, di)), s["e"], NT)[0:1]
        ddelta = _dot_lx(s["mask_t"].astype(MXU), da, NN, parts=3) + last_h
        ddt = ddelta * s["a_neg"] + _dot_rx(dxd_all * x, s["e"], NT)
        ddt_raw = ddt * _sigmoid(dt_ref[...] + bias_ref[...])
        ddt_ref[...] = ddt_raw
        dalog_ref[...] += _colsum(ddelta * s["dt"]) * s["a_neg"]
        dbias_ref[...] += _colsum(ddt_raw)
        dxbc_ref[:, 0:di] = (dxd_all * s["dt_e"] + jnp.where(d == 0, dyc * dsk_ref[...], 0.0)).astype(dxbc_ref.dtype)

    def cidx(d, i):
        return _chunk_order(d, nc - 1 - i, ncc, nc)

    cm = comm if comm is not None else _Comm()
    res = pl.pallas_call(
        _carry(body, comm, 9, 4, (2, nc)), name=name, grid=(2, nc),
        out_shape=[pltpu.HBM((2, tt, cd), ACT), pltpu.HBM((2, tt, h), F32),
                   pltpu.HBM((2, 1, h), F32), pltpu.HBM((2, 1, h), F32)] + cm.out_shape,
        in_specs=[pl.BlockSpec((q, di), lambda d, i: (cidx(d, i), 0)),
                  pl.BlockSpec((q, gn), lambda d, i: (cidx(d, i), di // gn)),
                  pl.BlockSpec((q, gn), lambda d, i: (cidx(d, i), di // gn + 1)),
                  pl.BlockSpec((None, q, h), lambda d, i: (d, cidx(d, i), 0)),
                  pl.BlockSpec((None, 1, h), lambda d, i: (d, 0, 0)),
                  pl.BlockSpec((None, 1, h), lambda d, i: (d, 0, 0)),
                  pl.BlockSpec((q, di), lambda d, i: (cidx(d, i), 0)),
                  pl.BlockSpec((None, None, n, di), lambda d, i: (d, cidx(d, i), 0, 0)),
                  pl.BlockSpec((1, di), lambda d, i: (0, 0))] + cm.specs,
        out_specs=[pl.BlockSpec((None, q, cd), lambda d, i: (d, cidx(d, i), 0)),
                   pl.BlockSpec((None, q, h), lambda d, i: (d, cidx(d, i), 0)),
                   pl.BlockSpec((None, 1, h), lambda d, i: (d, 0, 0)),
                   pl.BlockSpec((None, 1, h), lambda d, i: (d, 0, 0))] + cm.specs,
        scratch_shapes=[pltpu.VMEM((n, di), F32), pltpu.VMEM((q, di), F32), pltpu.VMEM((q, di), F32)]
        + (cm.scratch if comm is not None else []),
        compiler_params=pltpu.CompilerParams(dimension_semantics=("arbitrary", "arbitrary"), vmem_limit_bytes=_vmem(24 * q * di * 4)),
    )(*[_hbm(v) for v in (xbc, xbc, xbc, dt2, bias2, alog2, dy, hp2, dskip_e)], *cm.operands)
    return res[0], res[1], res[2], res[3], cm.split(res[4:])


def _ssd_gate_fwd(name, y2, xbc, zx, dskip_e, norm_w, di, nct, t):
    def fn(rows, bcs, i):
        yf, yb, xs, z = rows
        zg = (yf + yb + bcs[0] * xs) * _silu(z)
        rn = lax.rsqrt(jnp.mean(zg * zg, axis=-1, keepdims=True) + EPS)
        return [zg * rn * bcs[1]], []

    ins = [_ri(y2, lead=0, ro=nct), _ri(y2, lead=1, ro=nct), _ri(xbc, di, 0, ro=nct), _ri(zx, di, 0, ro=nct)]
    return _rowwise(name, fn, t, ins, [dskip_e, norm_w], [(di, MXU)])[0][0]


def _ssd_gate_bwd(name, dyn, y2, xbc, zx, dskip_e, norm_w, di, nct, tt):
    def fn(rows, bcs, i):
        dn, yf, yb, xs, z = rows
        lat = i >= nct
        ytot = yf + yb + bcs[0] * xs
        sz = _silu(z)
        zg = ytot * sz
        rn = lax.rsqrt(jnp.mean(zg * zg, axis=-1, keepdims=True) + EPS)
        u = dn * bcs[1]
        dzg = rn * u - zg * (rn * rn * rn) * jnp.mean(u * zg, axis=-1, keepdims=True)
        dy = jnp.where(lat, dzg * sz, 0.0)
        dz = jnp.where(lat, dzg * ytot * _dsilu(z), 0.0)
        return [dy, dz], [jnp.where(lat, _colsum(dn * zg * rn), 0.0), jnp.where(lat, _colsum(dy * xs), 0.0)]

    ins = [_ri(dyn, ro=-nct), _ri(y2, lead=0), _ri(y2, lead=1), _ri(xbc, di, 0), _ri(zx, di, 0)]
    (dy, dzx), (dnw, ddsk) = _rowwise(name, fn, tt, ins, [dskip_e, norm_w], [(di, F32), (di, MXU, zx.shape[1], 0)], [(1, di)] * 2)
    return dy, dzx, dnw, ddsk


def _ada_fwd(name, cs, w, b):
    nl, d, c = w.shape
    r = cs.shape[0]

    def body(cs_ref, w_ref, b_ref, o_ref):
        o_ref[...] = _dot(_silu(cs_ref[...]), w_ref[...], NN) + b_ref[...]

    return pl.pallas_call(
        body, name=name, grid=(nl,), out_shape=pltpu.HBM((nl, r, c), F32),
        in_specs=[pl.BlockSpec((r, d), lambda l: (0, 0)), pl.BlockSpec((None, d, c), lambda l: (l, 0, 0)),
                  pl.BlockSpec((None, 1, c), lambda l: (l, 0, 0))],
        out_specs=pl.BlockSpec((None, r, c), lambda l: (l, 0, 0)),
        compiler_params=pltpu.CompilerParams(dimension_semantics=("parallel",), vmem_limit_bytes=_vmem(2 * d * c * 4)),
    )(_hbm(cs), _hbm(w), _hbm(b))


def _ada_bwd(name, cs, w, dmod):
    nl, d, c = w.shape
    r = cs.shape[0]

    def body(cs_ref, w_ref, dm_ref, dw_ref, dsc_ref):
        dm = dm_ref[...]
        dw_ref[...] = _dot(_silu(cs_ref[...]), dm, TN)

        @pl.when(pl.program_id(0) == 0)
        def _():
            dctx = jnp.broadcast_to(_colsum(dm[r // 2:]), (8, c))
            dsc_ref[...] = _dot(dctx, w_ref[...], NT)[0:1]

    return pl.pallas_call(
        body, name=name, grid=(nl,), out_shape=[pltpu.HBM((nl, d, c), F32), pltpu.HBM((1, d), F32)],
        in_specs=[pl.BlockSpec((r, d), lambda l: (0, 0)), pl.BlockSpec((None, d, c), lambda l: (l, 0, 0)),
                  pl.BlockSpec((None, r, c), lambda l: (l, 0, 0))],
        out_specs=[pl.BlockSpec((None, d, c), lambda l: (l, 0, 0)), pl.BlockSpec((1, d), lambda l: (0, 0))],
        compiler_params=pltpu.CompilerParams(dimension_semantics=("arbitrary",), vmem_limit_bytes=_vmem(4 * d * c * 4)),
    )(_hbm(cs), _hbm(w), _hbm(dmod))


def _adam_math(w, g, m, v):
    m = ADAM_B1 * m + (1.0 - ADAM_B1) * g
    v = ADAM_B2 * v + (1.0 - ADAM_B2) * (g * g)
    m_hat = m / (1.0 - ADAM_B1 ** ADAM_STEP)
    v_hat = v / (1.0 - ADAM_B2 ** ADAM_STEP)
    delta = -ADAM_LR * (m_hat / (jnp.sqrt(v_hat) + ADAM_EPS) + ADAM_WD * w)
    return delta, m, v


def _adam(name, slots, w, m, v, comm=None):
    ns, r, c = slots.shape
    tr = _pick(r, (256, 128, 64, 32, 16, 8))

    def body(s_ref, w_ref, m_ref, v_ref, g_ref, d_ref, mo_ref, vo_ref):
        g = s_ref[0].astype(F32)
        for k in range(1, ns):
            g = g + s_ref[k].astype(F32)
        d, mn, vn = _adam_math(w_ref[...], g, m_ref[...], v_ref[...])
        g_ref[...], d_ref[...], mo_ref[...], vo_ref[...] = g, d, mn, vn

    blk = pl.BlockSpec((tr, c), lambda i: (i, 0))
    cm = comm if comm is not None else _Comm()
    res = pl.pallas_call(
        _carry(body, comm, 4, 4, (r // tr,)), name=name, grid=(r // tr,), out_shape=[pltpu.HBM((r, c), F32)] * 4 + cm.out_shape,
        in_specs=[pl.BlockSpec((ns, tr, c), lambda i: (0, i, 0)), blk, blk, blk] + cm.specs, out_specs=[blk] * 4 + cm.specs,
        scratch_shapes=cm.scratch if comm is not None else [],
        compiler_params=pltpu.CompilerParams(dimension_semantics=("arbitrary",), vmem_limit_bytes=_vmem(16 * tr * c * 4)),
    )(_hbm(slots), _hbm(w), _hbm(m), _hbm(v), *cm.operands)
    return res[:4] if comm is None else (res[:4], cm.split(res[4:]))


def _adam_small(name, slots, ws, ms, vs, scale=None):
    k = len(slots)

    def body(*refs):
        s_refs, w_refs, m_refs, v_refs = refs[:k], refs[k:2 * k], refs[2 * k:3 * k], refs[3 * k:4 * k]
        sc_ref = refs[4 * k] if scale is not None else None
        outs = refs[4 * k + (scale is not None):]
        for a in range(k):
            g = s_refs[a][0]
            for j in range(1, NDEV):
                g = g + s_refs[a][j]
            if scale is not None and a == scale[0]:
                g = g * _dsilu(sc_ref[...])
            d, mn, vn = _adam_math(w_refs[a][...], g, m_refs[a][...], v_refs[a][...])
            outs[a][...], outs[k + a][...], outs[2 * k + a][...], outs[3 * k + a][...] = g, d, mn, vn

    shapes = [pltpu.HBM(w.shape, F32) for w in ws]
    extra = [scale[1]] if scale is not None else []
    ins = [*slots, *ws, *ms, *vs, *extra]

    def whole(shape):
        return pl.BlockSpec(shape, lambda i, nd=len(shape): (0,) * nd)

    res = pl.pallas_call(body, name=name, grid=(1,), out_shape=shapes * 4, in_specs=[whole(v.shape) for v in ins],
                         out_specs=[whole(s.shape) for s in shapes * 4])(*[_hbm(v) for v in ins])
    return res[:k], res[k:2 * k], res[2 * k:3 * k], res[3 * k:]


def _unshard_cols(g):
    g = jnp.moveaxis(g, 0, -2)
    return g.reshape(g.shape[:-2] + (g.shape[-2] * g.shape[-1],))


def _shard_cols(a):
    a = a.reshape(a.shape[:-1] + (NDEV, a.shape[-1] // NDEV))
    return jnp.moveaxis(a, -2, 0)


def _unshard_rows(g):
    g = jnp.moveaxis(g, 0, -3)
    return g.reshape(g.shape[:-3] + (g.shape[-3] * g.shape[-2], g.shape[-1]))


def _shard_rows(a):
    a = a.reshape(a.shape[:-2] + (NDEV, a.shape[-2] // NDEV, a.shape[-1]))
    return jnp.moveaxis(a, -3, 0)


def _flat2(a):
    return a.reshape((-1, a.shape[-1]))


def kernel(x, c, ctx, c_ctx, ada_w, ada_b, norm_mix_g, norm_ffn_g, final_norm_g, ssd_w_in, ssd_conv_w, ssd_conv_b, ssd_dt_bias_f, ssd_dt_bias_b, ssd_a_log_f, ssd_a_log_b, ssd_d_skip, ssd_norm_w, ssd_w_out, conf_w_pw1, conf_b_pw1, conf_dw_w, conf_dw_b, conf_ln_g, conf_ln_b, conf_w_pw2, conf_b_pw2, ffn_w_in, ffn_w_out, loss_target, m_c_ctx, m_ada_w, m_ada_b, m_norm_mix_g, m_norm_ffn_g, m_final_norm_g, m_ssd_w_in, m_ssd_conv_w, m_ssd_conv_b, m_ssd_dt_bias_f, m_ssd_dt_bias_b, m_ssd_a_log_f, m_ssd_a_log_b, m_ssd_d_skip, m_ssd_norm_w, m_ssd_w_out, m_conf_w_pw1, m_conf_b_pw1, m_conf_dw_w, m_conf_dw_b, m_conf_ln_g, m_conf_ln_b, m_conf_w_pw2, m_conf_b_pw2, m_ffn_w_in, m_ffn_w_out, v_c_ctx, v_ada_w, v_ada_b, v_norm_mix_g, v_norm_ffn_g, v_final_norm_g, v_ssd_w_in, v_ssd_conv_w, v_ssd_conv_b, v_ssd_dt_bias_f, v_ssd_dt_bias_b, v_ssd_a_log_f, v_ssd_a_log_b, v_ssd_d_skip, v_ssd_norm_w, v_ssd_w_out, v_conf_w_pw1, v_conf_b_pw1, v_conf_dw_w, v_conf_dw_b, v_conf_ln_g, v_conf_ln_b, v_conf_w_pw2, v_conf_b_pw2, v_ffn_w_in, v_ffn_w_out):
    args = dict(locals())
    names = ['c_ctx', 'ada_w', 'ada_b', 'norm_mix_g', 'norm_ffn_g', 'final_norm_g', 'ssd_w_in', 'ssd_conv_w', 'ssd_conv_b',
             'ssd_dt_bias_f', 'ssd_dt_bias_b', 'ssd_a_log_f', 'ssd_a_log_b', 'ssd_d_skip', 'ssd_norm_w', 'ssd_w_out',
             'conf_w_pw1', 'conf_b_pw1', 'conf_dw_w', 'conf_dw_b', 'conf_ln_g', 'conf_ln_b', 'conf_w_pw2', 'conf_b_pw2',
             'ffn_w_in', 'ffn_w_out']
    me = 4 * lax.axis_index("x") + 2 * lax.axis_index("y") + lax.axis_index("c")
    t, d = x.shape[1], x.shape[2]
    tc = ctx.shape[1]
    tt = tc + t
    nct = tc // ROW_TILE
    assert tc % ROW_TILE == 0 and t % ROW_TILE == 0
    h = ssd_dt_bias_f.shape[-1]
    di = ssd_norm_w.shape[-1]
    cdim = ssd_conv_b.shape[-1]
    kc = ssd_conv_w.shape[1]
    ck = conf_dw_w.shape[1]
    ch = d // 2
    rows_g = t // GRID_W
    nl = ada_w.shape[0]
    cw = ada_w.shape[2]
    x2, ctx2, tgt = x[0], ctx[0], loss_target[0]

    (c_all, convw_g), _ = _exchange("gather_first", [c, ssd_conv_w[0]])
    ride_norm = _Comm(gather=[ssd_w_in[0].astype(WIRE)])
    ride_proj = _Comm(gather=[ssd_w_out[0].astype(WIRE), conf_w_pw2[0].astype(WIRE)])
    ride_conv = _Comm(gather=[conf_w_pw1[0].astype(WIRE), conf_b_pw1, conf_dw_w[0], conf_dw_b, conf_ln_g, conf_ln_b, conf_b_pw2])
    ride_scan = _Comm(gather=[ffn_w_in.astype(WIRE), ffn_w_out.astype(WIRE)])
    conv_w_full = _unshard_cols(convw_g)

    cs_all = jnp.concatenate([c_all[:, 0, :], jnp.broadcast_to(c_ctx[None, :], (NDEV, d))], axis=0)
    ada_b_mine = lax.dynamic_slice_in_dim(ada_b, me * cw, cw, axis=1)[:, None, :]
    mod_part = _ada_fwd("ada_fwd", cs_all, ada_w, ada_b_mine)
    (mod_g,), _ = _exchange("gather_mod", [mod_part])
    mod_all = jnp.moveaxis(mod_g, 0, 2).reshape(nl, 2 * NDEV, NDEV * cw)
    mod_lat = lax.dynamic_slice_in_dim(mod_all, me, 1, axis=1)[:, 0, :]
    mod_ctx = mod_all[0, NDEV, :]

    def six(v):
        return [v[k * d:(k + 1) * d][None, :] for k in range(6)]

    sh1, s1, g1, sh2, s2, g2 = six(mod_lat[0])
    csh1, cs1 = six(mod_ctx)[:2]
    sh1b, s1b, g1b, sh2b, s2b, g2b = six(mod_lat[1])
    nmg, nfg = norm_mix_g, norm_ffn_g

    h_all = jnp.concatenate([ctx2, x2], axis=0)
    s01, sh01 = jnp.concatenate([cs1, s1], axis=0), jnp.concatenate([csh1, sh1], axis=0)
    xn_all, ((w_in_g,), _) = _normmod_fwd("l0_norm", h_all, nmg[0:1], s01, sh01, nct, comm=ride_norm)
    w_ssd_in = _unshard_cols(w_in_g)
    w_zx = w_ssd_in[:, :di + cdim]
    w_dt = jnp.pad(w_ssd_in[:, di + cdim:], ((0, 0), (0, LANES - 2 * h)))
    zx, ((w_out_g, pw2_g), _) = _mm("ssd_in_proj", xn_all, w_zx, "nn", ACT, comm=ride_proj)
    dtr = _mm("ssd_dt_proj", xn_all, w_dt, "nn")
    dt2 = jnp.moveaxis(dtr[:, :2 * h].reshape(tt, 2, h), 1, 0)
    bias2 = jnp.stack([ssd_dt_bias_f, ssd_dt_bias_b])
    alog2 = jnp.stack([ssd_a_log_f, ssd_a_log_b])
    xbc, ((pw1_g, bpw1_g, dww_g, dwb_g, lng_g, lnb_g, bpw2_g), _) = _ssd_conv_fwd("ssd_conv", zx, conv_w_full, ssd_conv_b, di, tc,
                                                                                 comm=ride_conv)
    y2, hp2, ((fin_g, fout_g), _) = _ssd_fwd("ssd_scan", xbc, dt2, bias2, alog2, di, tc, comm=ride_scan)
    w_ssd_out = _unshard_rows(w_out_g)
    w_pw1, w_pw2 = _unshard_cols(pw1_g), _unshard_rows(pw2_g)
    w_fin, w_fout = _unshard_cols(fin_g), _unshard_rows(fout_g)
    dw_w_full = _unshard_cols(dww_g)
    b_pw1, dw_b, ln_g, ln_b, b_pw2 = (_unshard_cols(a) for a in (bpw1_g, dwb_g, lng_g, lnb_g, bpw2_g))
    dskip_e = jnp.repeat(ssd_d_skip, di // h, axis=1)
    yn = _ssd_gate_fwd("ssd_gate", y2, xbc, zx, dskip_e, ssd_norm_w, di, nct, t)
    mix0 = _mm("ssd_out_proj", yn, w_ssd_out, "nn")
    h1, xf0 = _resnorm_fwd("l0_res_norm", x2, mix0, g1, nfg[0:1], s2, sh2)
    u0 = _mm("ffn0_in", xf0, w_fin[0], "nn", ACT)
    hid0 = _swiglu_fwd("ffn0_act", u0)
    f0 = _mm("ffn0_out", hid0, w_fout[0], "nn")
    h2, xn1 = _resnorm_fwd("l1_norm", h1, f0, g2, nmg[1:2], s1b, sh1b)
    u1 = _mm("conf_pw1", xn1, w_pw1, "nn", ACT, bias=b_pw1)
    gl = _glu_fwd("conf_glu", u1)
    gl_h = _grid_t(gl[:, :ch], rows_g, GRID_W)
    v_ht = _strided_conv("conf_conv_h", gl_h, dw_w_full[:, :ch], dw_b[:, :ch], rows_g)
    v_v = _strided_conv("conf_conv_v", gl, dw_w_full[:, ch:], dw_b[:, ch:], GRID_W, x_col0=ch)
    v_h = _grid_t(v_ht, GRID_W, rows_g)
    sl = _ln_silu_fwd("conf_ln", v_h, v_v, ln_g, ln_b)
    mix1 = _mm("conf_pw2", sl, w_pw2, "nn", bias=b_pw2)
    h3, xf1 = _resnorm_fwd("l1_res_norm", h2, mix1, g1b, nfg[1:2], s2b, sh2b)
    u2 = _mm("ffn1_in", xf1, w_fin[1], "nn", ACT)
    hid1 = _swiglu_fwd("ffn1_act", u2)
    f1 = _mm("ffn1_out", hid1, w_fout[1], "nn")
    dh, sq, d_final_g = _final_loss("final_loss", h3, f1, tgt, g2b, final_norm_g[None, :])
    loss = lax.psum(0.5 * sq[0, 0] / d, AXES)

    zero2 = jnp.zeros((2, d), F32)

    def ffn_bwd(tag, dh, hin, xf, u, hid, f, gate, w_in, w_out, g_norm, s_mod):
        df, dgate, _ = _gate_bwd(tag + "_gate_bwd", dh, f, gate)
        dhid = _mm(tag + "_dhid", df, w_out, "nt", ACT)
        dw_out = _mm(tag + "_dwout", hid, df, "tn", WIRE)
        du = _swiglu_bwd(tag + "_act_bwd", u, dhid)
        dw_in = _mm(tag + "_dwin", xf, du, "tn", WIRE)
        dxf = _mm(tag + "_dx", du, w_in, "nt")
        s_2 = jnp.concatenate([s_mod, s_mod], axis=0)
        dh, dsh, ds, dg = _normmod_bwd(tag + "_norm_bwd", hin, dxf, dh, g_norm, s_2)
        return dh, dgate, dsh[1:2], ds[1:2], dg[1:2], dw_in, dw_out

    dh, d_g2b, d_sh2b, d_s2b, d_nfg1, g_fin1, g_fout1 = ffn_bwd("ffn1", dh, h3, xf1, u2, hid1, f1, g2b, w_fin[1], w_fout[1], nfg[1:2], s2b)
    dmix1, d_g1b, g_bpw2 = _gate_bwd("conf_gate_bwd", dh, mix1, g1b)
    dsl = _mm("conf_dsl", dmix1, w_pw2, "nt")
    g_pw2 = _mm("conf_dwpw2", sl, dmix1, "tn", WIRE)
    dv_lo, dv_v, g_lng, g_lnb = _ln_silu_bwd("conf_ln_bwd", v_h, v_v, dsl, ln_g, ln_b)
    dv_h = _grid_t(dv_lo, rows_g, GRID_W)
    w_flip = dw_w_full[::-1]
    dgl_h = _strided_conv("conf_conv_h_bwd", dv_h, w_flip[:, :ch], None, rows_g)
    dgl_v = _strided_conv("conf_conv_v_bwd", dv_v, w_flip[:, ch:], None, GRID_W)
    g_dww_h, g_dwb_h = _strided_conv_dw("conf_conv_h_dw", gl_h, dv_h, ck, rows_g)
    g_dww_v, g_dwb_v = _strided_conv_dw("conf_conv_v_dw", gl, dv_v, ck, GRID_W, x_col0=ch)
    g_dww, g_dwb = jnp.concatenate([g_dww_h, g_dww_v], axis=1), jnp.concatenate([g_dwb_h, g_dwb_v], axis=1)
    du1, g_bpw1 = _glu_bwd("conf_glu_bwd", u1, _grid_t(dgl_h, GRID_W, rows_g), dgl_v)
    g_pw1 = _mm("conf_dwpw1", xn1, du1, "tn", WIRE)
    dxn1 = _mm("conf_dx", du1, w_pw1, "nt")
    dh, dsh_, ds_, dg_ = _normmod_bwd("l1_norm_bwd", h2, dxn1, dh, nmg[1:2], jnp.concatenate([s1b, s1b], axis=0))
    d_sh1b, d_s1b, d_nmg1 = dsh_[1:2], ds_[1:2], dg_[1:2]
    dh, d_g2, d_sh2, d_s2, d_nfg0, g_fin0, g_fout0 = ffn_bwd("ffn0", dh, h1, xf0, u0, hid0, f0, g2, w_fin[0], w_fout[0], nfg[0:1], s2)
    dmix0, d_g1, _ = _gate_bwd("ssd_gate_res_bwd", dh, mix0, g1)
    dyn = _mm("ssd_dyn", dmix0, w_ssd_out, "nt")
    g_ssd_out = _mm("ssd_dwout", yn, dmix0, "tn", WIRE)
    dy, dzx, g_normw, ddsk_e = _ssd_gate_bwd("ssd_gate_bwd", dyn, y2, xbc, zx, dskip_e, ssd_norm_w, di, nct, tt)
    ride_scan_bwd = _Comm(scatter=[_flat3(_shard_cols(jnp.stack([g_fin0, g_fin1]))), _flat3(_shard_rows(jnp.stack([g_fout0, g_fout1])))])
    ride_conv_bwd = _Comm(scatter=[_shard_rows(g_ssd_out), _shard_cols(g_pw1), _shard_rows(g_pw2), _shard_cols(g_bpw1),
                                   _shard_cols(g_dww), _shard_cols(g_dwb), _shard_cols(g_lng), _shard_cols(g_lnb), _shard_cols(g_bpw2)])
    dxbc2, ddt2, g_alog2, g_bias2, (_, ffn_r) = _ssd_bwd("ssd_scan_bwd", xbc, dt2, bias2, alog2, dy, hp2, dskip_e, di, tc,
                                                         comm=ride_scan_bwd)
    dzx, g_convw, g_convb, (_, conv_r) = _ssd_conv_bwd("ssd_conv_bwd", zx, dxbc2, conv_w_full, ssd_conv_b, dzx, di, tc,
                                                       comm=ride_conv_bwd)
    ddt_p = jnp.pad(jnp.moveaxis(ddt2, 0, 1).reshape(tt, 2 * h), ((0, 0), (0, LANES - 2 * h))).astype(MXU)
    g_ssd_in = jnp.concatenate([_mm("ssd_dw_zx", xn_all, dzx, "tn", WIRE),
                                _mm("ssd_dw_dt", xn_all, ddt_p, "tn", WIRE)[:, :2 * h]], axis=1)
    dxn, (_, (ssd_in_r, convw_r)) = _mm("ssd_dx_zx", dzx, w_zx, "nt",
                                        comm=_Comm(scatter=[_shard_cols(g_ssd_in), _shard_cols(g_convw)]))
    dxn = _mm("ssd_dx_dt", ddt_p, w_dt, "nt", add=dxn)
    dh_all, dsh_, ds_, dg_ = _normmod_bwd("l0_norm_bwd", h_all, dxn, dh, nmg[0:1], s01, nct)
    grad_x = dh_all[tc:][None]
    d_csh1, d_sh1, d_cs1, d_s1 = dsh_[0:1], dsh_[1:2], ds_[0:1], ds_[1:2]
    d_nmg0 = dg_[0:1] + dg_[1:2]

    z1 = jnp.zeros((1, d), F32)
    dmod = jnp.concatenate([jnp.concatenate([d_sh1, d_s1, d_g1, d_sh2, d_s2, d_g2], axis=1),
                            jnp.concatenate([d_sh1b, d_s1b, d_g1b, d_sh2b, d_s2b, d_g2b], axis=1),
                            jnp.concatenate([d_csh1, d_cs1, z1, z1, z1, z1], axis=1)], axis=0)
    out = {}

    def put(name, res):
        w = args[name]
        out["grad_" + name], out["delta_" + name], out["new_m_" + name], out["new_v_" + name] = (r.reshape(w.shape) for r in res)

    def adam_big(name, slots, comm=None):
        return _adam("adam_" + name, slots, _flat2(args[name]), _flat2(args["m_" + name]), _flat2(args["v_" + name]), comm=comm)

    res, ((dmod_g,), _) = adam_big("ffn_w_in", ffn_r[0], comm=_Comm(gather=[dmod]))
    put("ffn_w_in", res)
    dmod_mine = lax.dynamic_slice_in_dim(dmod_g, me * cw, cw, axis=2)
    dmod16 = jnp.stack([jnp.concatenate([dmod_mine[:, 0], dmod_mine[:, 2]], axis=0),
                        jnp.concatenate([dmod_mine[:, 1], jnp.zeros((NDEV, cw), F32)], axis=0)])
    g_ada_w, dsc_part = _ada_bwd("ada_bwd", cs_all, ada_w, dmod16)
    g_ada_b = dmod[0:2] + jnp.concatenate([dmod[2:3], jnp.zeros((1, 6 * d), F32)], axis=0)

    d_dskip = jnp.sum(ddsk_e.reshape(h, di // h), axis=1)[None, :]
    rep = [dsc_part, g_ada_b, jnp.concatenate([d_nmg0, d_nmg1], axis=0), jnp.concatenate([d_nfg0, d_nfg1], axis=0),
           d_final_g, g_convb, g_bias2[0], g_bias2[1], g_alog2[0], g_alog2[1], d_dskip, g_normw]
    res, (rep_g, _) = _adam("adam_ada_w", _flat2(g_ada_w)[None], _flat2(ada_w), _flat2(m_ada_w), _flat2(v_ada_w),
                            comm=_Comm(gather=rep))
    put("ada_w", res)
    small_r = [convw_r] + list(conv_r[3:])

    for name, slots in zip(["ssd_w_in", "ssd_w_out", "conf_w_pw1", "conf_w_pw2", "ffn_w_out"],
                           [ssd_in_r, conv_r[0], conv_r[1], conv_r[2], ffn_r[1]]):
        put(name, adam_big(name, slots))
    small_names = ["ssd_conv_w", "conf_b_pw1", "conf_dw_w", "conf_dw_b", "conf_ln_g", "conf_ln_b", "conf_b_pw2",
                   "c_ctx", "ada_b", "norm_mix_g", "norm_ffn_g", "final_norm_g", "ssd_conv_b", "ssd_dt_bias_f", "ssd_dt_bias_b",
                   "ssd_a_log_f", "ssd_a_log_b", "ssd_d_skip", "ssd_norm_w"]
    slots = list(small_r) + list(rep_g)

    def as2(a):
        return a.reshape((1, -1)) if a.ndim == 1 else _flat2(a)

    res = _adam_small("adam_small", slots, [as2(args[n]) for n in small_names], [as2(args["m_" + n]) for n in small_names],
                      [as2(args["v_" + n]) for n in small_names], scale=(small_names.index("c_ctx"), c_ctx[None, :]))
    for k, name in enumerate(small_names):
        put(name, [r[k] for r in res])
    return (loss, grad_x, *[out["grad_" + n] for n in names], *[out["delta_" + n] for n in names],
            *[out["new_m_" + n] for n in names], *[out["new_v_" + n] for n in names])


def _flat3(a):
    return a.reshape((a.shape[0], -1, a.shape[-1]))
```

```python
import functools

import jax
import jax.numpy as jnp
from jax import lax
from jax.experimental import pallas as pl
from jax.experimental.pallas import tpu as pltpu

F32 = jnp.float32
MXU = jnp.bfloat16
WIRE = jnp.bfloat16
ACT = jnp.bfloat16
NDEV = 8
AXES = ("x", "y", "c")
SSD_STATE = 128
SSD_CHUNK = 128
GRID_W = 64
EPS = 1e-6
ROW_TILE = 256
LANES = 128
ADAM_LR, ADAM_B1, ADAM_B2, ADAM_EPS, ADAM_WD, ADAM_STEP = 0.001, 0.9, 0.999, 1e-08, 0.01, 10
VMEM_CAP = 56 * 2 ** 20
MESH_ID = pl.DeviceIdType.MESH


def _pick(dim, cands):
    for c in cands:
        if dim % c == 0:
            return c
    return dim


def _nbytes(shape, dtype):
    n = 1
    for s in shape:
        n *= s
    return n * jnp.dtype(dtype).itemsize


def _vmem(nbytes):
    return int(min(VMEM_CAP, max(24 * 2 ** 20, 2 * nbytes + 8 * 2 ** 20)))


def _sigmoid(x):
    return 1.0 / (1.0 + jnp.exp(-x))


def _silu(x):
    return x * _sigmoid(x)


def _dsilu(x):
    s = _sigmoid(x)
    return s * (1.0 + x * (1.0 - s))


def _softplus(x):
    return jnp.maximum(x, 0.0) + jnp.log(1.0 + jnp.exp(-jnp.abs(x)))


def _dot(a, b, dims):
    return lax.dot_general(a.astype(MXU), b.astype(MXU), (dims, ((), ())), preferred_element_type=F32)


NN, NT, TN = ((1,), (0,)), ((1,), (1,)), ((0,), (0,))


def _split(a, parts):
    out = []
    for _ in range(parts):
        p = a.astype(MXU)
        out.append(p)
        a = a - p.astype(F32)
    return out


def _dot_lx(e, a, dims, parts=2):
    return sum(lax.dot_general(e, p, (dims, ((), ())), preferred_element_type=F32) for p in _split(a, parts))


def _dot_rx(a, e, dims, parts=2):
    return sum(lax.dot_general(p, e, (dims, ((), ())), preferred_element_type=F32) for p in _split(a, parts))


class _Comm:
    def __init__(self, gather=(), scatter=()):
        self.gather, self.scatter = list(gather), list(scatter)
        self.ng, self.n = len(self.gather), len(self.gather) + len(self.scatter)
        self.operands = self.gather + self.scatter
        self.specs = [pl.BlockSpec(memory_space=pl.ANY)] * self.n
        self.out_shape = ([jax.ShapeDtypeStruct((NDEV,) + a.shape, a.dtype) for a in self.gather]
                          + [jax.ShapeDtypeStruct(a.shape, a.dtype) for a in self.scatter])
        self.scratch = [pltpu.SemaphoreType.DMA((self.n, 7)), pltpu.SemaphoreType.DMA((self.n, 7)),
                        pltpu.SemaphoreType.DMA((self.n,))]

    def split(self, res):
        return res[:self.ng], res[self.ng:]

    def _copies(self, ins, outs, sems):
        send, recv, loc = sems
        ng, n = self.ng, self.n
        x, y, c = lax.axis_index("x"), lax.axis_index("y"), lax.axis_index("c")
        me, sib = (x, y, c), (x, y, 1 - c)
        chips = [(1 - x, y), (x, 1 - y), (1 - x, 1 - y)]

        def slot(p):
            return 4 * p[0] + 2 * p[1] + p[2]

        def rcopy(a, k, src, dst, to):
            return functools.partial(pltpu.make_async_remote_copy, src_ref=src, dst_ref=dst, send_sem=send.at[a, k],
                                     recv_sem=recv.at[a, k], device_id=to, device_id_type=MESH_ID)

        local = [functools.partial(pltpu.make_async_copy, ins[a] if a < ng else ins[a].at[slot(me)], outs[a].at[slot(me)],
                                   loc.at[a]) for a in range(n)]
        rel = [(fx, fy, fc) for fx in (0, 1) for fy in (0, 1) for fc in (0, 1)][1:]
        first, landed, passed = [], [], []
        for a in range(ng, n):
            for k, (fx, fy, fc) in enumerate(rel):
                p = (1 - x if fx else x, 1 - y if fy else y, 1 - c if fc else c)
                first.append(rcopy(a, k, ins[a].at[slot(p)], outs[a].at[slot(me)], p))
                blk = outs[a].at[slot(p)]
                landed.append(rcopy(a, k, blk, blk, me))
        for a in range(ng):
            dst = outs[a].at[slot(me)]
            first.append(rcopy(a, 0, ins[a], dst, sib))
            first += [rcopy(a, 1 + j, ins[a], dst, (*ch, c)) for j, ch in enumerate(chips)]
            blk = outs[a].at[slot(sib)]
            landed.append(rcopy(a, 0, blk, blk, me))
            for j, ch in enumerate(chips):
                blk = outs[a].at[slot((*ch, c))]
                passed.append((rcopy(a, 1 + j, blk, blk, me), rcopy(a, 4 + j, blk, blk, sib)))
                blk = outs[a].at[slot((*ch, 1 - c))]
                landed.append(rcopy(a, 4 + j, blk, blk, me))
        return local, first, passed, landed

    def start(self, ins, outs, sems):
        local, first, _, _ = self._copies(ins, outs, sems)
        for make in local + first:
            make().start()

    def finish(self, ins, outs, sems):
        local, first, passed, landed = self._copies(ins, outs, sems)
        onward = []
        for arrived, forward in passed:
            arrived().wait_recv()
            onward.append(forward())
            onward[-1].start()
        for make in landed:
            make().wait_recv()
        for make in first:
            make().wait_send()
        for cp in onward:
            cp.wait_send()
        for make in local:
            make().wait()


def _carry(body, comm, n_in, n_out, grid):
    if comm is None:
        return body
    n = comm.n

    def wrapped(*refs):
        own_in, c_in = refs[:n_in], refs[n_in:n_in + n]
        own_out, c_out = refs[n_in + n:n_in + n + n_out], refs[n_in + n + n_out:n_in + 2 * n + n_out]
        own_scr, sems = refs[n_in + 2 * n + n_out:-3], refs[-3:]
        ids = [pl.program_id(ax) for ax in range(len(grid))]
        first, last = ids[0] == 0, ids[0] == grid[0] - 1
        for ax in range(1, len(grid)):
            first, last = first & (ids[ax] == 0), last & (ids[ax] == grid[ax] - 1)

        @pl.when(first)
        def _():
            comm.start(c_in, c_out, sems)

        body(*own_in, *own_out, *own_scr)

        @pl.when(last)
        def _():
            comm.finish(c_in, c_out, sems)

    return wrapped


def _exchange(name, gather, scatter=()):
    comm = _Comm(gather, scatter)
    n = comm.n

    def body(*refs):
        ins, outs, sems = refs[:n], refs[n:2 * n], refs[2 * n:]
        comm.start(ins, outs, sems)
        comm.finish(ins, outs, sems)

    res = pl.pallas_call(body, name=name, out_shape=comm.out_shape, in_specs=comm.specs, out_specs=comm.specs,
                         scratch_shapes=comm.scratch)(*comm.operands)
    return comm.split(res)


def _hbm(a):
    return pltpu.with_memory_space_constraint(a, pltpu.HBM)


def _divs(dim, mult):
    return [dim] + [dim // parts for parts in range(2, dim // mult + 1) if dim % parts == 0 and (dim // parts) % mult == 0]


MM_VMEM_BUDGET = 40 * 2 ** 20
GRID_STEP_US = 0.35
HBM_BYTES_PER_US = 3.0e6


def _mm_tiles(m, n, k, sizes, mode, has_add):
    sa, sb, so = sizes
    sub = 16
    best = None
    for tk in _divs(k, LANES):
        for tn in _divs(n, LANES):
            for tm in _divs(m, LANES if mode == "tn" else sub):
                nk = k // tk
                out_t = tm * tn
                est = (2 * (tm * tk * sa + tk * tn * sb) + 2 * out_t * so + 2 * (tm * tk + tk * tn) + 4 * out_t
                       + (4 * out_t if nk > 1 else 0) + (8 * out_t if has_add else 0))
                if est > MM_VMEM_BUDGET:
                    continue
                steps = (m // tm) * (n // tn) * nk
                cost = steps * GRID_STEP_US + (tm * tk * sa + tk * tn * sb + out_t * so) / HBM_BYTES_PER_US
                if best is None or cost < best[0]:
                    best = (cost, tm, tn, tk, est)
    assert best is not None, (m, n, k)
    return best[1:]


def _mm(name, a, b, mode, out_dtype=F32, bias=None, add=None, comm=None):
    if mode == "nn":
        (m, k), (k2, n) = a.shape, b.shape
    elif mode == "nt":
        (m, k), (n, k2) = a.shape, b.shape
    else:
        (k, m), (k2, n) = a.shape, b.shape
    assert k == k2, (name, a.shape, b.shape)
    sizes = (a.dtype.itemsize, b.dtype.itemsize, jnp.dtype(out_dtype).itemsize)
    tm, tn, tk, est = _mm_tiles(m, n, k, sizes, mode, add is not None)
    nk = k // tk
    dims = {"nn": NN, "nt": NT, "tn": TN}[mode]
    a_spec = pl.BlockSpec((tk, tm), lambda i, j, kk: (kk, i)) if mode == "tn" else pl.BlockSpec((tm, tk), lambda i, j, kk: (i, kk))
    b_spec = pl.BlockSpec((tn, tk), lambda i, j, kk: (j, kk)) if mode == "nt" else pl.BlockSpec((tk, tn), lambda i, j, kk: (kk, j))
    extra, extra_specs = [], []
    if bias is not None:
        extra.append(bias)
        extra_specs.append(pl.BlockSpec((1, tn), lambda i, j, kk: (0, j)))
    if add is not None:
        extra.append(add)
        extra_specs.append(pl.BlockSpec((tm, tn), lambda i, j, kk: (i, j)))

    def finish(r, extras, o_ref):
        for e in extras:
            r = r + e[...].astype(F32)
        o_ref[...] = r.astype(o_ref.dtype)

    def body_acc(*refs):
        a_ref, b_ref = refs[:2]
        o_ref, acc = refs[-2:]
        kk = pl.program_id(2)

        @pl.when(kk == 0)
        def _():
            acc[...] = jnp.zeros_like(acc)

        acc[...] += _dot(a_ref[...], b_ref[...], dims)

        @pl.when(kk == nk - 1)
        def _():
            finish(acc[...], refs[2:-2], o_ref)

    def body_one(*refs):
        finish(_dot(refs[0][...], refs[1][...], dims), refs[2:-1], refs[-1])

    cm = comm if comm is not None else _Comm()
    grid = (m // tm, n // tn, nk)
    res = pl.pallas_call(
        _carry(body_acc if nk > 1 else body_one, comm, 2 + len(extra), 1, grid), name=name, grid=grid,
        out_shape=[pltpu.HBM((m, n), out_dtype)] + cm.out_shape,
        in_specs=[a_spec, b_spec] + extra_specs + cm.specs,
        out_specs=[pl.BlockSpec((tm, tn), lambda i, j, kk: (i, j))] + cm.specs,
        scratch_shapes=([pltpu.VMEM((tm, tn), F32)] if nk > 1 else []) + (cm.scratch if comm is not None else []),
        compiler_params=pltpu.CompilerParams(
            dimension_semantics=("parallel", "parallel", "arbitrary") if comm is None else ("arbitrary",) * 3,
            vmem_limit_bytes=int(min(VMEM_CAP, est + 12 * 2 ** 20))),
    )(*[_hbm(v) for v in (a, b, *extra)], *cm.operands)
    return res[0] if comm is None else (res[0], cm.split(res[1:]))


def _ri(arr, w=None, cb=0, ro=0, lead=None):
    return (arr, arr.shape[-1] if w is None else w, cb, ro, lead)


def _rowwise(name, fn, nrows, row_ins, bc_ins, outs, accs=(), comm=None):
    tr = min(ROW_TILE, nrows)
    assert nrows % tr == 0
    in_specs = []
    for (arr, w, cb, ro, lead) in row_ins:
        if lead is None:
            in_specs.append(pl.BlockSpec((tr, w), lambda i, cb=cb, ro=ro: (jnp.maximum(i + ro, 0), cb)))
        else:
            in_specs.append(pl.BlockSpec((None, tr, w), lambda i, cb=cb, ro=ro, lead=lead: (lead, jnp.maximum(i + ro, 0), cb)))
    for arr in bc_ins:
        in_specs.append(pl.BlockSpec(arr.shape, lambda i, nd=arr.ndim: (0,) * nd))
    outs = [o if len(o) == 4 else (o[0], o[1], o[0], 0) for o in outs]
    out_shape = [pltpu.HBM((nrows, total), dt) for _, dt, total, _ in outs] + [pltpu.HBM(s, F32) for s in accs]
    out_specs = ([pl.BlockSpec((tr, c), lambda i, cb=cb: (i, cb)) for c, _, _, cb in outs]
                 + [pl.BlockSpec(s, lambda i: (0, 0)) for s in accs])
    nr, nb, no = len(row_ins), len(bc_ins), len(outs)

    def body(*refs):
        i = pl.program_id(0)
        rows = [r[...].astype(F32) for r in refs[:nr]]
        bcs = [r[...] for r in refs[nr:nr + nb]]
        o, a = fn(rows, bcs, i)
        for ref, val in zip(refs[nr + nb:nr + nb + no], o):
            ref[...] = val.astype(ref.dtype)
        for ref, val in zip(refs[nr + nb + no:], a):
            @pl.when(i == 0)
            def _(ref=ref, val=val):
                ref[...] = val

            @pl.when(i > 0)
            def _(ref=ref, val=val):
                ref[...] += val

    est = sum(tr * w * arr.dtype.itemsize for (arr, w, _, _, _) in row_ins) + sum(tr * o[0] * 4 for o in outs)
    cm = comm if comm is not None else _Comm()
    nout = no + len(accs)
    res = pl.pallas_call(
        _carry(body, comm, nr + nb, nout, (nrows // tr,)), name=name, grid=(nrows // tr,), out_shape=out_shape + cm.out_shape,
        in_specs=in_specs + cm.specs, out_specs=out_specs + cm.specs, scratch_shapes=cm.scratch if comm is not None else [],
        compiler_params=pltpu.CompilerParams(dimension_semantics=("arbitrary",), vmem_limit_bytes=_vmem(3 * est)),
    )(*[_hbm(r[0]) for r in row_ins], *[_hbm(v) for v in bc_ins], *cm.operands)
    if comm is None:
        return res[:no], res[no:]
    return res[:no], res[no:nout], cm.split(res[nout:])


def _colsum(v):
    return jnp.sum(v, axis=0, keepdims=True)


def _normmod_fwd(name, h, g, s, sh, nct=0, comm=None):
    d = h.shape[1]

    def fn(rows, bcs, i):
        hh, (g_, s_, sh_) = rows[0], bcs
        s1 = jnp.where(i < nct, s_[0:1], s_[1:2])
        sh1 = jnp.where(i < nct, sh_[0:1], sh_[1:2])
        r = lax.rsqrt(jnp.mean(hh * hh, axis=-1, keepdims=True) + EPS)
        return [hh * r * g_ * (1.0 + s1) + sh1], []

    res = _rowwise(name, fn, h.shape[0], [_ri(h)], [g, s, sh], [(d, MXU)], comm=comm)
    return res[0][0] if comm is None else (res[0][0], res[2])


def _normmod_bwd(name, h, dxn, dres, g, s, nct=0):
    d = h.shape[1]

    def fn(rows, bcs, i):
        hh, dx, dr = rows
        g_, s_ = bcs
        ctx = i < nct
        s1 = jnp.where(ctx, s_[0:1], s_[1:2])
        r = lax.rsqrt(jnp.mean(hh * hh, axis=-1, keepdims=True) + EPS)
        hr = hh * r
        dy = dx * (1.0 + s1)
        u = dy * g_
        dh = r * u - hr * (r * r) * jnp.mean(u * hh, axis=-1, keepdims=True)
        dh = dh + jnp.where(ctx, 0.0, dr)

        def seg(v):
            v = _colsum(v)
            return jnp.concatenate([jnp.where(ctx, v, 0.0), jnp.where(ctx, 0.0, v)], axis=0)

        return [dh], [seg(dx), seg(dx * hr * g_), seg(dy * hr)]

    (dh,), (dsh, ds, dg) = _rowwise(name, fn, h.shape[0], [_ri(h), _ri(dxn), _ri(dres, ro=-nct)], [g, s],
                                    [(d, F32)], [(2, d)] * 3)
    return dh, dsh, ds, dg


def _resnorm_fwd(name, h, y, gate, g, s, sh):
    d = h.shape[1]

    def fn(rows, bcs, i):
        hh, yy = rows
        gate_, g_, s_, sh_ = bcs
        hn = hh + gate_ * yy
        r = lax.rsqrt(jnp.mean(hn * hn, axis=-1, keepdims=True) + EPS)
        return [hn, hn * r * g_ * (1.0 + s_) + sh_], []

    return _rowwise(name, fn, h.shape[0], [_ri(h), _ri(y)], [gate, g, s, sh], [(d, F32), (d, MXU)])[0]


def _gate_bwd(name, dh, y, gate):
    d = dh.shape[1]

    def fn(rows, bcs, i):
        dd, yy = rows
        dy = dd * bcs[0]
        return [dy], [_colsum(dd * yy), _colsum(dy)]

    (dy,), (dgate, dbias) = _rowwise(name, fn, dh.shape[0], [_ri(dh), _ri(y)], [gate], [(d, MXU)], [(1, d)] * 2)
    return dy, dgate, dbias


def _swiglu_fwd(name, u):
    f = u.shape[1] // 2

    def fn(rows, bcs, i):
        return [_silu(rows[0]) * rows[1]], []

    return _rowwise(name, fn, u.shape[0], [_ri(u, f, 0), _ri(u, f, 1)], [], [(f, MXU)])[0][0]


def _swiglu_bwd(name, u, dhid):
    f = u.shape[1] // 2

    def fn(rows, bcs, i):
        a, b, dd = rows
        return [jnp.concatenate([dd * b * _dsilu(a), dd * _silu(a)], axis=1)], []

    return _rowwise(name, fn, u.shape[0], [_ri(u, f, 0), _ri(u, f, 1), _ri(dhid)], [], [(2 * f, MXU)])[0][0]


def _glu_fwd(name, u):
    d = u.shape[1] // 2

    def fn(rows, bcs, i):
        return [rows[0] * _sigmoid(rows[1])], []

    return _rowwise(name, fn, u.shape[0], [_ri(u, d, 0), _ri(u, d, 1)], [], [(d, F32)])[0][0]


def _glu_bwd(name, u, dgl_lo, dgl_hi):
    d = u.shape[1] // 2

    def fn(rows, bcs, i):
        a, b = rows[:2]
        dd = jnp.concatenate(rows[2:], axis=1)
        sg = _sigmoid(b)
        du = jnp.concatenate([dd * sg, dd * a * sg * (1.0 - sg)], axis=1)
        return [du], [_colsum(du)]

    (du,), (db,) = _rowwise(name, fn, u.shape[0], [_ri(u, d, 0), _ri(u, d, 1), _ri(dgl_lo), _ri(dgl_hi)], [], [(2 * d, MXU)],
                            [(1, 2 * d)])
    return du, db


def _ln_silu_fwd(name, v_lo, v_hi, g, b):
    d = 2 * v_lo.shape[1]

    def fn(rows, bcs, i):
        vv = jnp.concatenate(rows, axis=1)
        mu = jnp.mean(vv, axis=-1, keepdims=True)
        xc = vv - mu
        rs = lax.rsqrt(jnp.mean(xc * xc, axis=-1, keepdims=True) + EPS)
        return [_silu(xc * rs * bcs[0] + bcs[1])], []

    return _rowwise(name, fn, v_lo.shape[0], [_ri(v_lo), _ri(v_hi)], [g, b], [(d, MXU)])[0][0]


def _ln_silu_bwd(name, v_lo, v_hi, ds, g, b):
    ch = v_lo.shape[1]

    def fn(rows, bcs, i):
        vv, dd = jnp.concatenate(rows[:2], axis=1), rows[2]
        mu = jnp.mean(vv, axis=-1, keepdims=True)
        xc = vv - mu
        rs = lax.rsqrt(jnp.mean(xc * xc, axis=-1, keepdims=True) + EPS)
        xh = xc * rs
        dln = dd * _dsilu(xh * bcs[0] + bcs[1])
        dxh = dln * bcs[0]
        dv = rs * (dxh - jnp.mean(dxh, axis=-1, keepdims=True) - xh * jnp.mean(dxh * xh, axis=-1, keepdims=True))
        return [dv[:, :ch], dv[:, ch:]], [_colsum(dln * xh), _colsum(dln)]

    (dv_lo, dv_hi), (dg, db) = _rowwise(name, fn, v_lo.shape[0], [_ri(v_lo), _ri(v_hi), _ri(ds)], [g, b],
                                        [(ch, F32), (ch, F32)], [(1, 2 * ch)] * 2)
    return dv_lo, dv_hi, dg, db


def _final_loss(name, h, f, target, gate, gf):
    d = h.shape[1]

    def fn(rows, bcs, i):
        hh, ff, tg = rows
        gate_, g_ = bcs
        hn = hh + gate_ * ff
        r = lax.rsqrt(jnp.mean(hn * hn, axis=-1, keepdims=True) + EPS)
        hr = hn * r
        err = hr * g_ - tg
        dout = err * (1.0 / d)
        u = dout * g_
        dh = r * u - hr * (r * r) * jnp.mean(u * hn, axis=-1, keepdims=True)
        sq = jnp.sum(_colsum(err * err), axis=1, keepdims=True)
        return [dh], [jnp.broadcast_to(sq, (1, LANES)), _colsum(dout * hr)]

    (dh,), (sq, dgf) = _rowwise(name, fn, h.shape[0], [_ri(h), _ri(f), _ri(target)], [gate, gf], [(d, F32)], [(1, LANES), (1, d)])
    return dh, sq, dgf


CONV_ROWS = 64
GAP = 8


def _gapped(ref_rows, buf, tc, tt):
    cb = buf.shape[1]
    zero = jnp.zeros((GAP, cb), F32)
    buf[0:GAP, :] = zero
    buf[GAP + tc:2 * GAP + tc, :] = zero
    buf[2 * GAP + tt:, :] = zero
    buf[GAP:GAP + tc, :] = ref_rows[0:tc]
    buf[2 * GAP + tc:2 * GAP + tt, :] = ref_rows[tc:tt]
    return buf[...]


def _ungapped(v, tc, tt):
    return jnp.concatenate([v[GAP:GAP + tc], v[2 * GAP + tc:2 * GAP + tt]], axis=0)


def _shift_rows(x, o):
    return x if o == 0 else pltpu.roll(x, (-o) % x.shape[0], 0)


def _ssd_conv_fwd(name, zx, w, b, di, tc, comm=None):
    tt, kc, cd = zx.shape[0], w.shape[0], w.shape[1]
    cb = _pick(cd, (LANES,))
    off = di // cb
    assert kc // 2 < GAP and tc % GAP == 0 and tt % GAP == 0

    def body(x_ref, w_ref, b_ref, o_ref, pre_ref, xp):
        x = _gapped(x_ref[...].astype(F32), xp, tc, tt)
        acc = jnp.broadcast_to(b_ref[...], x.shape)
        for k in range(kc):
            acc = acc + w_ref[k:k + 1, :] * _shift_rows(x, k - kc // 2)
        acc = _ungapped(acc, tc, tt)
        pre_ref[...] = acc.astype(pre_ref.dtype)
        o_ref[...] = _silu(acc).astype(o_ref.dtype)

    cm = comm if comm is not None else _Comm()
    blk = pl.BlockSpec((tt, cb), lambda j: (0, j))
    res = pl.pallas_call(
        _carry(body, comm, 3, 2, (cd // cb,)), name=name, grid=(cd // cb,),
        out_shape=[pltpu.HBM((tt, cd), ACT), pltpu.HBM((tt, cd), ACT)] + cm.out_shape,
        in_specs=[pl.BlockSpec((tt, cb), lambda j: (0, j + off)), pl.BlockSpec((kc, cb), lambda j: (0, j)),
                  pl.BlockSpec((1, cb), lambda j: (0, j))] + cm.specs,
        out_specs=[blk, blk] + cm.specs,
        scratch_shapes=[pltpu.VMEM((tt + 3 * GAP, cb), F32)] + (cm.scratch if comm is not None else []),
        compiler_params=pltpu.CompilerParams(dimension_semantics=("arbitrary",), vmem_limit_bytes=_vmem(5 * tt * cb * 4)),
    )(_hbm(zx), _hbm(w), _hbm(b), *cm.operands)
    return res[0], res[1], cm.split(res[2:])


def _ssd_conv_bwd(name, zx, dact2, w, pre, dzx, di, tc, comm=None):
    tt, kc, cd = zx.shape[0], w.shape[0], w.shape[1]
    cb = _pick(cd, (LANES,))
    off = di // cb

    def body(x_ref, d0_ref, d1_ref, w_ref, pre_ref, _, dx_ref, dw_ref, db_ref, xp, dp):
        x = _gapped(x_ref[...].astype(F32), xp, tc, tt)
        dpre = (d0_ref[...].astype(F32) + d1_ref[...].astype(F32)) * _dsilu(pre_ref[...].astype(F32))
        db_ref[...] = _colsum(dpre)
        dpre = _gapped(dpre, dp, tc, tt)
        dx = jnp.zeros_like(x)
        for k in range(kc):
            sh = _shift_rows(dpre, -(k - kc // 2))
            dx = dx + w_ref[k:k + 1, :] * sh
            dw_ref[k:k + 1, :] = _colsum(sh * x)
        dx_ref[...] = _ungapped(dx, tc, tt).astype(dx_ref.dtype)

    cm = comm if comm is not None else _Comm()
    res = pl.pallas_call(
        _carry(body, comm, 6, 3, (cd // cb,)), name=name, grid=(cd // cb,),
        out_shape=[pltpu.HBM(dzx.shape, dzx.dtype), pltpu.HBM((kc, cd), F32), pltpu.HBM((1, cd), F32)] + cm.out_shape,
        in_specs=[pl.BlockSpec((tt, cb), lambda j: (0, j + off)), pl.BlockSpec((None, tt, cb), lambda j: (0, 0, j)),
                  pl.BlockSpec((None, tt, cb), lambda j: (1, 0, j)), pl.BlockSpec((kc, cb), lambda j: (0, j)),
                  pl.BlockSpec((tt, cb), lambda j: (0, j)), pl.BlockSpec(memory_space=pl.ANY)] + cm.specs,
        out_specs=[pl.BlockSpec((tt, cb), lambda j: (0, j + off)), pl.BlockSpec((kc, cb), lambda j: (0, j)),
                   pl.BlockSpec((1, cb), lambda j: (0, j))] + cm.specs,
        input_output_aliases={5: 0},
        scratch_shapes=[pltpu.VMEM((tt + 3 * GAP, cb), F32)] * 2 + (cm.scratch if comm is not None else []),
        compiler_params=pltpu.CompilerParams(dimension_semantics=("arbitrary",), vmem_limit_bytes=_vmem(10 * tt * cb * 4)),
    )(_hbm(zx), _hbm(dact2), _hbm(dact2), _hbm(w), _hbm(pre), _hbm(dzx), *cm.operands)
    return res[0], res[1], res[2], cm.split(res[3:])


def _strided_conv(name, x, w, b, stride, x_col0=0):
    t, ch = x.shape[0], w.shape[1]
    kk = w.shape[0]
    pad = (kk // 2) * stride
    cb = _pick(ch, (LANES,))
    has_b = b is not None

    def body(*refs):
        x_ref, w_ref = refs[:2]
        o_ref, xp = refs[-2:]
        xp[0:pad, :] = jnp.zeros((pad, cb), F32)
        xp[pad + t:, :] = jnp.zeros((pad, cb), F32)
        xp[pad:pad + t, :] = x_ref[...]
        rc = _pick(t, (CONV_ROWS,))

        def chunk(ci, carry):
            r0 = pl.multiple_of(ci * rc, rc)
            acc = jnp.broadcast_to(refs[2][...], (rc, cb)) if has_b else jnp.zeros((rc, cb), F32)
            for k in range(kk):
                acc = acc + w_ref[k:k + 1, :] * xp[pl.ds(r0 + k * stride, rc), :]
            o_ref[pl.ds(r0, rc), :] = acc
            return carry

        lax.fori_loop(0, t // rc, chunk, 0)

    xoff = x_col0 // cb
    ins, specs = [x, w], [pl.BlockSpec((t, cb), lambda j: (0, j + xoff)), pl.BlockSpec((kk, cb), lambda j: (0, j))]
    if has_b:
        ins.append(b)
        specs.append(pl.BlockSpec((1, cb), lambda j: (0, j)))
    return pl.pallas_call(
        body, name=name, grid=(ch // cb,), out_shape=pltpu.HBM((t, ch), F32), in_specs=specs,
        out_specs=pl.BlockSpec((t, cb), lambda j: (0, j)), scratch_shapes=[pltpu.VMEM((t + 2 * pad, cb), F32)],
        compiler_params=pltpu.CompilerParams(dimension_semantics=("parallel",), vmem_limit_bytes=_vmem(6 * t * cb * 4)),
    )(*[_hbm(v) for v in ins])


def _strided_conv_dw(name, x, dv, kk, stride, x_col0=0):
    t, ch = dv.shape
    pad = (kk // 2) * stride
    cb = _pick(ch, (LANES,))

    def body(x_ref, d_ref, dw_ref, db_ref, xp):
        xp[0:pad, :] = jnp.zeros((pad, cb), F32)
        xp[pad + t:, :] = jnp.zeros((pad, cb), F32)
        xp[pad:pad + t, :] = x_ref[...]
        rc = _pick(t, (CONV_ROWS,))

        def tap_sum(off):
            def chunk(ci, acc):
                r0 = pl.multiple_of(ci * rc, rc)
                return acc + d_ref[pl.ds(r0, rc), :] * (xp[pl.ds(r0 + off, rc), :] if off is not None else 1.0)

            return _colsum(lax.fori_loop(0, t // rc, chunk, jnp.zeros((rc, cb), F32)))

        for k in range(kk):
            dw_ref[k:k + 1, :] = tap_sum(k * stride)
        db_ref[...] = tap_sum(None)

    blk = pl.BlockSpec((t, cb), lambda j: (0, j))
    xoff = x_col0 // cb
    return pl.pallas_call(
        body, name=name, grid=(ch // cb,), out_shape=[pltpu.HBM((kk, ch), F32), pltpu.HBM((1, ch), F32)],
        in_specs=[pl.BlockSpec((t, cb), lambda j: (0, j + xoff)), blk], out_specs=[pl.BlockSpec((kk, cb), lambda j: (0, j)), pl.BlockSpec((1, cb), lambda j: (0, j))],
        scratch_shapes=[pltpu.VMEM((t + 2 * pad, cb), F32)],
        compiler_params=pltpu.CompilerParams(dimension_semantics=("parallel",), vmem_limit_bytes=_vmem(6 * t * cb * 4)),
    )(_hbm(x), _hbm(dv))


def _grid_t(a, n1, n2):
    return a.reshape(n1, n2, a.shape[-1]).swapaxes(0, 1).reshape(n1 * n2, a.shape[-1])


def _chunk_order(d, i, ncc, nc):
    back = jnp.where(i < ncc, ncc - 1 - i, nc - 1 - (i - ncc))
    return jnp.where(d == 0, i, back)


def _ssd_chunk_setup(d, dt_raw, bias, a_log, q, h, di):
    p = di // h
    dt = _softplus(dt_raw + bias)
    a_neg = -jnp.exp(a_log)
    delta = dt * a_neg
    r = lax.broadcasted_iota(jnp.int32, (q, q), 0)
    c = lax.broadcasted_iota(jnp.int32, (q, q), 1)
    sgn = 1 - 2 * d
    mask = (r - c) * sgn >= 0
    mask_t = (c - r) * sgn >= 0
    a = _dot_lx(mask.astype(MXU), delta, NN, parts=3)
    tot = _colsum(delta)
    ea, dte, cd = jnp.exp(a), jnp.exp(tot - a), jnp.exp(tot)
    hh = lax.broadcasted_iota(jnp.int32, (h, di), 0)
    cc = lax.broadcasted_iota(jnp.int32, (h, di), 1)
    e = (cc // p == hh).astype(MXU)
    ex = _dot_rx(jnp.concatenate([dt, ea, dte, jnp.broadcast_to(cd, (8, h))], axis=0), e, NN)
    eye = (lax.broadcasted_iota(jnp.int32, (h, h), 0) == lax.broadcasted_iota(jnp.int32, (h, h), 1)).astype(MXU)
    a_t = _dot_lx(eye, a, NT, parts=3)
    return dict(dt=dt, a_neg=a_neg, a=a, a_t=a_t, mask=mask, mask_t=mask_t, e=e,
                dt_e=ex[0:q], ea_e=ex[q:2 * q], dte_e=ex[2 * q:3 * q], cd_e=ex[3 * q:3 * q + 1])


def _pick_heads(r, q, hpg, p):
    lane = lax.broadcasted_iota(jnp.int32, (q, hpg * p), 1) // p
    out = jnp.zeros((q, hpg * p), F32)
    for j in range(hpg):
        out = out + jnp.where(lane == j, r[j * q:(j + 1) * q], 0.0)
    return out


def _ssd_fwd(name, xbc, dt2, bias2, alog2, di, tc, comm=None):
    tt, cd = xbc.shape
    h = dt2.shape[-1]
    q, n = SSD_CHUNK, SSD_STATE
    gn = (cd - di) // 2
    g = gn // n
    hpg, p = h // g, di // h
    gp = hpg * p
    nc, ncc = tt // q, tc // q
    assert di % gn == 0

    def body(x_ref, b_ref, c_ref, dt_ref, bias_ref, alog_ref, y_ref, hp_ref, ht):
        d, i = pl.program_id(0), pl.program_id(1)

        @pl.when(i == 0)
        def _():
            ht[...] = jnp.zeros_like(ht)

        s = _ssd_chunk_setup(d, dt_ref[...], bias_ref[...], alog_ref[...], q, h, di)
        xd = x_ref[...].astype(F32) * s["dt_e"]
        hp_ref[...] = ht[...].astype(hp_ref.dtype)
        for gi in range(g):
            bg, cg = b_ref[:, gi * n:(gi + 1) * n].astype(MXU), c_ref[:, gi * n:(gi + 1) * n].astype(MXU)
            sl = slice(gi * gp, (gi + 1) * gp)
            sc = _dot(cg, bg, NT)
            ms = []
            for j in range(hpg):
                hd = gi * hpg + j
                seg = s["a"][:, hd:hd + 1] - s["a_t"][hd:hd + 1, :]
                ms.append(sc * jnp.exp(jnp.where(s["mask"], seg, -jnp.inf)))
            xdg = xd[:, sl]
            ydiag = _pick_heads(_dot(jnp.concatenate(ms, axis=0), xdg, NN), q, hpg, p)
            htg = ht[:, sl]
            y_ref[:, sl] = ydiag + _dot(cg, htg, NN) * s["ea_e"][:, sl]
            ht[:, sl] = s["cd_e"][:, sl] * htg + _dot(bg, xdg * s["dte_e"][:, sl], TN)

    def cidx(d, i):
        return _chunk_order(d, i, ncc, nc)

    cm = comm if comm is not None else _Comm()
    res = pl.pallas_call(
        _carry(body, comm, 6, 2, (2, nc)), name=name, grid=(2, nc),
        out_shape=[pltpu.HBM((2, tt, di), F32), pltpu.HBM((2, nc, n, di), ACT)] + cm.out_shape,
        in_specs=[pl.BlockSpec((q, di), lambda d, i: (cidx(d, i), 0)),
                  pl.BlockSpec((q, gn), lambda d, i: (cidx(d, i), di // gn)),
                  pl.BlockSpec((q, gn), lambda d, i: (cidx(d, i), di // gn + 1)),
                  pl.BlockSpec((None, q, h), lambda d, i: (d, cidx(d, i), 0)),
                  pl.BlockSpec((None, 1, h), lambda d, i: (d, 0, 0)),
                  pl.BlockSpec((None, 1, h), lambda d, i: (d, 0, 0))] + cm.specs,
        out_specs=[pl.BlockSpec((None, q, di), lambda d, i: (d, cidx(d, i), 0)),
                   pl.BlockSpec((None, None, n, di), lambda d, i: (d, cidx(d, i), 0, 0))] + cm.specs,
        scratch_shapes=[pltpu.VMEM((n, di), F32)] + (cm.scratch if comm is not None else []),
        compiler_params=pltpu.CompilerParams(dimension_semantics=("arbitrary", "arbitrary"), vmem_limit_bytes=_vmem(16 * q * di * 4)),
    )(*[_hbm(v) for v in (xbc, xbc, xbc, dt2, bias2, alog2)], *cm.operands)
    return res[0], res[1], cm.split(res[2:])


def _ssd_bwd(name, xbc, dt2, bias2, alog2, dy, hp2, dskip_e, di, tc, comm=None):
    tt, cd = xbc.shape
    h = dt2.shape[-1]
    q, n = SSD_CHUNK, SSD_STATE
    gn = (cd - di) // 2
    g = gn // n
    hpg, p = h // g, di // h
    gp = hpg * p
    nc, ncc = tt // q, tc // q

    def body(x_ref, b_ref, c_ref, dt_ref, bias_ref, alog_ref, dy_ref, hp_ref, dsk_ref,
             dxbc_ref, ddt_ref, dalog_ref, dbias_ref, dht, dxd, off):
        d, i = pl.program_id(0), pl.program_id(1)

        @pl.when(i == 0)
        def _():
            dht[...] = jnp.zeros_like(dht)
            dalog_ref[...] = jnp.zeros_like(dalog_ref)
            dbias_ref[...] = jnp.zeros_like(dbias_ref)

        s = _ssd_chunk_setup(d, dt_ref[...], bias_ref[...], alog_ref[...], q, h, di)
        x, dyc = x_ref[...].astype(F32), dy_ref[...]
        xd = x * s["dt_e"]
        dyea = dyc * s["ea_e"]
        xdte = xd * s["dte_e"]
        lane = lax.broadcasted_iota(jnp.int32, (q, gp), 1) // p
        lane_h = lax.broadcasted_iota(jnp.int32, (q, h), 1)
        da_d = jnp.zeros((q, h), F32)
        last_e = []
        for gi in range(g):
            bg, cg = b_ref[:, gi * n:(gi + 1) * n].astype(MXU), c_ref[:, gi * n:(gi + 1) * n].astype(MXU)
            sl = slice(gi * gp, (gi + 1) * gp)
            sc, sct = _dot(cg, bg, NT), _dot(bg, cg, NT)
            dyg, xdg = dyc[:, sl], xd[:, sl]
            htg, dhtg = hp_ref[:, sl].astype(F32), dht[:, sl]
            dystack = jnp.concatenate([jnp.where(lane == j, dyg, 0.0) for j in range(hpg)], axis=0)
            xdstack = jnp.concatenate([jnp.where(lane == j, xdg, 0.0) for j in range(hpg)], axis=0)
            gs = _dot(dystack, xdg, NT)
            gst = _dot(xdstack, dyg, NT)
            ds = jnp.zeros((q, q), F32)
            mts = []
            for j in range(hpg):
                hd = gi * hpg + j
                col, rw = s["a"][:, hd:hd + 1], s["a_t"][hd:hd + 1, :]
                gl = gs[j * q:(j + 1) * q] * jnp.exp(jnp.where(s["mask"], col - rw, -jnp.inf))
                ds = ds + gl
                mt = sct * jnp.exp(jnp.where(s["mask_t"], rw - col, -jnp.inf))
                mts.append(mt)
                da_j = jnp.sum(gl * sc, axis=1, keepdims=True) - jnp.sum(gst[j * q:(j + 1) * q] * mt, axis=1, keepdims=True)
                da_d = da_d + jnp.where(lane_h == hd, da_j, 0.0)
            dxd_diag = _pick_heads(_dot(jnp.concatenate(mts, axis=0), dyg, NN), q, hpg, p)
            z = _dot(bg, dhtg, NN) * s["dte_e"][:, sl]
            yoff = _dot(cg, htg, NN) * s["ea_e"][:, sl]
            off[:, sl] = dyg * yoff - xdg * z
            dxd[:, sl] = dxd_diag + z
            dxbc_ref[:, di + gi * n:di + (gi + 1) * n] = (_dot(ds, cg, TN) + _dot(xdte[:, sl], dhtg, NT)).astype(dxbc_ref.dtype)
            dxbc_ref[:, di + gn + gi * n:di + gn + (gi + 1) * n] = (_dot(ds, bg, NN)
                                                                    + _dot(dyea[:, sl], htg, NT)).astype(dxbc_ref.dtype)
            last_e.append(s["cd_e"][:, sl] * _colsum(dhtg * htg) + _colsum(xdg * z))
            dht[:, sl] = s["cd_e"][:, sl] * dhtg + _dot(cg, dyea[:, sl], TN)
        dxd_all = dxd[...]
        last = jnp.concatenate(last_e, axis=1)
        da = da_d + _dot_rx(off[...], s["e"], NT)
        last_h = _dot_rx(jnp.broadcast_to(last, (8, di)), s["e"], NT)[0:1]
        ddelta = _dot_lx(s["mask_t"].astype(MXU), da, NN, parts=3) + last_h
        ddt = ddelta * s["a_neg"] + _dot_rx(dxd_all * x, s["e"], NT)
        ddt_raw = ddt * _sigmoid(dt_ref[...] + bias_ref[...])
        ddt_ref[...] = ddt_raw
        dalog_ref[...] += _colsum(ddelta * s["dt"]) * s["a_neg"]
        dbias_ref[...] += _colsum(ddt_raw)
        dxbc_ref[:, 0:di] = (dxd_all * s["dt_e"] + jnp.where(d == 0, dyc * dsk_ref[...], 0.0)).astype(dxbc_ref.dtype)

    def cidx(d, i):
        return _chunk_order(d, nc - 1 - i, ncc, nc)

    cm = comm if comm is not None else _Comm()
    res = pl.pallas_call(
        _carry(body, comm, 9, 4, (2, nc)), name=name, grid=(2, nc),
        out_shape=[pltpu.HBM((2, tt, cd), ACT), pltpu.HBM((2, tt, h), F32),
                   pltpu.HBM((2, 1, h), F32), pltpu.HBM((2, 1, h), F32)] + cm.out_shape,
        in_specs=[pl.BlockSpec((q, di), lambda d, i: (cidx(d, i), 0)),
                  pl.BlockSpec((q, gn), lambda d, i: (cidx(d, i), di // gn)),
                  pl.BlockSpec((q, gn), lambda d, i: (cidx(d, i), di // gn + 1)),
                  pl.BlockSpec((None, q, h), lambda d, i: (d, cidx(d, i), 0)),
                  pl.BlockSpec((None, 1, h), lambda d, i: (d, 0, 0)),
                  pl.BlockSpec((None, 1, h), lambda d, i: (d, 0, 0)),
                  pl.BlockSpec((q, di), lambda d, i: (cidx(d, i), 0)),
                  pl.BlockSpec((None, None, n, di), lambda d, i: (d, cidx(d, i), 0, 0)),
                  pl.BlockSpec((1, di), lambda d, i: (0, 0))] + cm.specs,
        out_specs=[pl.BlockSpec((None, q, cd), lambda d, i: (d, cidx(d, i), 0)),
                   pl.BlockSpec((None, q, h), lambda d, i: (d, cidx(d, i), 0)),
                   pl.BlockSpec((None, 1, h), lambda d, i: (d, 0, 0)),
                   pl.BlockSpec((None, 1, h), lambda d, i: (d, 0, 0))] + cm.specs,
        scratch_shapes=[pltpu.VMEM((n, di), F32), pltpu.VMEM((q, di), F32), pltpu.VMEM((q, di), F32)]
        + (cm.scratch if comm is not None else []),
        compiler_params=pltpu.CompilerParams(dimension_semantics=("arbitrary", "arbitrary"), vmem_limit_bytes=_vmem(24 * q * di * 4)),
    )(*[_hbm(v) for v in (xbc, xbc, xbc, dt2, bias2, alog2, dy, hp2, dskip_e)], *cm.operands)
    return res[0], res[1], res[2], res[3], cm.split(res[4:])


def _ssd_gate_fwd(name, y2, xbc, zx, dskip_e, norm_w, di, nct, t):
    def fn(rows, bcs, i):
        yf, yb, xs, z = rows
        zg = (yf + yb + bcs[0] * xs) * _silu(z)
        rn = lax.rsqrt(jnp.mean(zg * zg, axis=-1, keepdims=True) + EPS)
        return [zg * rn * bcs[1]], []

    ins = [_ri(y2, lead=0, ro=nct), _ri(y2, lead=1, ro=nct), _ri(xbc, di, 0, ro=nct), _ri(zx, di, 0, ro=nct)]
    return _rowwise(name, fn, t, ins, [dskip_e, norm_w], [(di, MXU)])[0][0]


def _ssd_gate_bwd(name, dyn, y2, xbc, zx, dskip_e, norm_w, di, nct, tt):
    def fn(rows, bcs, i):
        dn, yf, yb, xs, z = rows
        lat = i >= nct
        ytot = yf + yb + bcs[0] * xs
        sz = _silu(z)
        zg = ytot * sz
        rn = lax.rsqrt(jnp.mean(zg * zg, axis=-1, keepdims=True) + EPS)
        u = dn * bcs[1]
        dzg = rn * u - zg * (rn * rn * rn) * jnp.mean(u * zg, axis=-1, keepdims=True)
        dy = jnp.where(lat, dzg * sz, 0.0)
        dz = jnp.where(lat, dzg * ytot * _dsilu(z), 0.0)
        return [dy, dz], [jnp.where(lat, _colsum(dn * zg * rn), 0.0), jnp.where(lat, _colsum(dy * xs), 0.0)]

    ins = [_ri(dyn, ro=-nct), _ri(y2, lead=0), _ri(y2, lead=1), _ri(xbc, di, 0), _ri(zx, di, 0)]
    (dy, dzx), (dnw, ddsk) = _rowwise(name, fn, tt, ins, [dskip_e, norm_w], [(di, F32), (di, MXU, zx.shape[1], 0)], [(1, di)] * 2)
    return dy, dzx, dnw, ddsk


def _ada_fwd(name, cs, w, b):
    nl, d, c = w.shape
    r = cs.shape[0]

    def body(cs_ref, w_ref, b_ref, o_ref):
        o_ref[...] = _dot(_silu(cs_ref[...]), w_ref[...], NN) + b_ref[...]

    return pl.pallas_call(
        body, name=name, grid=(nl,), out_shape=pltpu.HBM((nl, r, c), F32),
        in_specs=[pl.BlockSpec((r, d), lambda l: (0, 0)), pl.BlockSpec((None, d, c), lambda l: (l, 0, 0)),
                  pl.BlockSpec((None, 1, c), lambda l: (l, 0, 0))],
        out_specs=pl.BlockSpec((None, r, c), lambda l: (l, 0, 0)),
        compiler_params=pltpu.CompilerParams(dimension_semantics=("parallel",), vmem_limit_bytes=_vmem(2 * d * c * 4)),
    )(_hbm(cs), _hbm(w), _hbm(b))


def _ada_bwd(name, cs, w, dmod):
    nl, d, c = w.shape
    r = cs.shape[0]

    def body(cs_ref, w_ref, dm_ref, dw_ref, dsc_ref):
        dm = dm_ref[...]
        dw_ref[...] = _dot(_silu(cs_ref[...]), dm, TN)

        @pl.when(pl.program_id(0) == 0)
        def _():
            dctx = jnp.broadcast_to(_colsum(dm[r // 2:]), (8, c))
            dsc_ref[...] = _dot(dctx, w_ref[...], NT)[0:1]

    return pl.pallas_call(
        body, name=name, grid=(nl,), out_shape=[pltpu.HBM((nl, d, c), F32), pltpu.HBM((1, d), F32)],
        in_specs=[pl.BlockSpec((r, d), lambda l: (0, 0)), pl.BlockSpec((None, d, c), lambda l: (l, 0, 0)),
                  pl.BlockSpec((None, r, c), lambda l: (l, 0, 0))],
        out_specs=[pl.BlockSpec((None, d, c), lambda l: (l, 0, 0)), pl.BlockSpec((1, d), lambda l: (0, 0))],
        compiler_params=pltpu.CompilerParams(dimension_semantics=("arbitrary",), vmem_limit_bytes=_vmem(4 * d * c * 4)),
    )(_hbm(cs), _hbm(w), _hbm(dmod))


def _adam_math(w, g, m, v):
    m = ADAM_B1 * m + (1.0 - ADAM_B1) * g
    v = ADAM_B2 * v + (1.0 - ADAM_B2) * (g * g)
    m_hat = m / (1.0 - ADAM_B1 ** ADAM_STEP)
    v_hat = v / (1.0 - ADAM_B2 ** ADAM_STEP)
    delta = -ADAM_LR * (m_hat / (jnp.sqrt(v_hat) + ADAM_EPS) + ADAM_WD * w)
    return delta, m, v


def _adam(name, slots, w, m, v, comm=None):
    ns, r, c = slots.shape
    tr = _pick(r, (256, 128, 64, 32, 16, 8))

    def body(s_ref, w_ref, m_ref, v_ref, g_ref, d_ref, mo_ref, vo_ref):
        g = s_ref[0].astype(F32)
        for k in range(1, ns):
            g = g + s_ref[k].astype(F32)
        d, mn, vn = _adam_math(w_ref[...], g, m_ref[...], v_ref[...])
        g_ref[...], d_ref[...], mo_ref[...], vo_ref[...] = g, d, mn, vn

    blk = pl.BlockSpec((tr, c), lambda i: (i, 0))
    cm = comm if comm is not None else _Comm()
    res = pl.pallas_call(
        _carry(body, comm, 4, 4, (r // tr,)), name=name, grid=(r // tr,), out_shape=[pltpu.HBM((r, c), F32)] * 4 + cm.out_shape,
        in_specs=[pl.BlockSpec((ns, tr, c), lambda i: (0, i, 0)), blk, blk, blk] + cm.specs, out_specs=[blk] * 4 + cm.specs,
        scratch_shapes=cm.scratch if comm is not None else [],
        compiler_params=pltpu.CompilerParams(dimension_semantics=("arbitrary",), vmem_limit_bytes=_vmem(16 * tr * c * 4)),
    )(_hbm(slots), _hbm(w), _hbm(m), _hbm(v), *cm.operands)
    return res[:4] if comm is None else (res[:4], cm.split(res[4:]))


def _adam_small(name, slots, ws, ms, vs, scale=None):
    k = len(slots)

    def body(*refs):
        s_refs, w_refs, m_refs, v_refs = refs[:k], refs[k:2 * k], refs[2 * k:3 * k], refs[3 * k:4 * k]
        sc_ref = refs[4 * k] if scale is not None else None
        outs = refs[4 * k + (scale is not None):]
        for a in range(k):
            g = s_refs[a][0]
            for j in range(1, NDEV):
                g = g + s_refs[a][j]
            if scale is not None and a == scale[0]:
                g = g * _dsilu(sc_ref[...])
            d, mn, vn = _adam_math(w_refs[a][...], g, m_refs[a][...], v_refs[a][...])
            outs[a][...], outs[k + a][...], outs[2 * k + a][...], outs[3 * k + a][...] = g, d, mn, vn

    shapes = [pltpu.HBM(w.shape, F32) for w in ws]
    extra = [scale[1]] if scale is not None else []
    ins = [*slots, *ws, *ms, *vs, *extra]

    def whole(shape):
        return pl.BlockSpec(shape, lambda i, nd=len(shape): (0,) * nd)

    res = pl.pallas_call(body, name=name, grid=(1,), out_shape=shapes * 4, in_specs=[whole(v.shape) for v in ins],
                         out_specs=[whole(s.shape) for s in shapes * 4])(*[_hbm(v) for v in ins])
    return res[:k], res[k:2 * k], res[2 * k:3 * k], res[3 * k:]


def _unshard_cols(g):
    g = jnp.moveaxis(g, 0, -2)
    return g.reshape(g.shape[:-2] + (g.shape[-2] * g.shape[-1],))


def _shard_cols(a):
    a = a.reshape(a.shape[:-1] + (NDEV, a.shape[-1] // NDEV))
    return jnp.moveaxis(a, -2, 0)


def _unshard_rows(g):
    g = jnp.moveaxis(g, 0, -3)
    return g.reshape(g.shape[:-3] + (g.shape[-3] * g.shape[-2], g.shape[-1]))


def _shard_rows(a):
    a = a.reshape(a.shape[:-2] + (NDEV, a.shape[-2] // NDEV, a.shape[-1]))
    return jnp.moveaxis(a, -3, 0)


def _flat2(a):
    return a.reshape((-1, a.shape[-1]))


def kernel(x, c, ctx, c_ctx, ada_w, ada_b, norm_mix_g, norm_ffn_g, final_norm_g, ssd_w_in, ssd_conv_w, ssd_conv_b, ssd_dt_bias_f, ssd_dt_bias_b, ssd_a_log_f, ssd_a_log_b, ssd_d_skip, ssd_norm_w, ssd_w_out, conf_w_pw1, conf_b_pw1, conf_dw_w, conf_dw_b, conf_ln_g, conf_ln_b, conf_w_pw2, conf_b_pw2, ffn_w_in, ffn_w_out, loss_target, m_c_ctx, m_ada_w, m_ada_b, m_norm_mix_g, m_norm_ffn_g, m_final_norm_g, m_ssd_w_in, m_ssd_conv_w, m_ssd_conv_b, m_ssd_dt_bias_f, m_ssd_dt_bias_b, m_ssd_a_log_f, m_ssd_a_log_b, m_ssd_d_skip, m_ssd_norm_w, m_ssd_w_out, m_conf_w_pw1, m_conf_b_pw1, m_conf_dw_w, m_conf_dw_b, m_conf_ln_g, m_conf_ln_b, m_conf_w_pw2, m_conf_b_pw2, m_ffn_w_in, m_ffn_w_out, v_c_ctx, v_ada_w, v_ada_b, v_norm_mix_g, v_norm_ffn_g, v_final_norm_g, v_ssd_w_in, v_ssd_conv_w, v_ssd_conv_b, v_ssd_dt_bias_f, v_ssd_dt_bias_b, v_ssd_a_log_f, v_ssd_a_log_b, v_ssd_d_skip, v_ssd_norm_w, v_ssd_w_out, v_conf_w_pw1, v_conf_b_pw1, v_conf_dw_w, v_conf_dw_b, v_conf_ln_g, v_conf_ln_b, v_conf_w_pw2, v_conf_b_pw2, v_ffn_w_in, v_ffn_w_out):
    args = dict(locals())
    names = ['c_ctx', 'ada_w', 'ada_b', 'norm_mix_g', 'norm_ffn_g', 'final_norm_g', 'ssd_w_in', 'ssd_conv_w', 'ssd_conv_b',
             'ssd_dt_bias_f', 'ssd_dt_bias_b', 'ssd_a_log_f', 'ssd_a_log_b', 'ssd_d_skip', 'ssd_norm_w', 'ssd_w_out',
             'conf_w_pw1', 'conf_b_pw1', 'conf_dw_w', 'conf_dw_b', 'conf_ln_g', 'conf_ln_b', 'conf_w_pw2', 'conf_b_pw2',
             'ffn_w_in', 'ffn_w_out']
    me = 4 * lax.axis_index("x") + 2 * lax.axis_index("y") + lax.axis_index("c")
    t, d = x.shape[1], x.shape[2]
    tc = ctx.shape[1]
    tt = tc + t
    nct = tc // ROW_TILE
    assert tc % ROW_TILE == 0 and t % ROW_TILE == 0
    h = ssd_dt_bias_f.shape[-1]
    di = ssd_norm_w.shape[-1]
    cdim = ssd_conv_b.shape[-1]
    kc = ssd_conv_w.shape[1]
    ck = conf_dw_w.shape[1]
    ch = d // 2
    rows_g = t // GRID_W
    nl = ada_w.shape[0]
    cw = ada_w.shape[2]
    x2, ctx2, tgt = x[0], ctx[0], loss_target[0]

    (c_all, convw_g), _ = _exchange("gather_first", [c, ssd_conv_w[0]])
    ride_norm = _Comm(gather=[ssd_w_in[0].astype(WIRE)])
    ride_proj = _Comm(gather=[ssd_w_out[0].astype(WIRE), conf_w_pw2[0].astype(WIRE)])
    ride_conv = _Comm(gather=[conf_w_pw1[0].astype(WIRE), conf_b_pw1, conf_dw_w[0], conf_dw_b, conf_ln_g, conf_ln_b, conf_b_pw2])
    ride_scan = _Comm(gather=[ffn_w_in.astype(WIRE), ffn_w_out.astype(WIRE)])
    conv_w_full = _unshard_cols(convw_g)

    cs_all = jnp.concatenate([c_all[:, 0, :], jnp.broadcast_to(c_ctx[None, :], (NDEV, d))], axis=0)
    ada_b_mine = lax.dynamic_slice_in_dim(ada_b, me * cw, cw, axis=1)[:, None, :]
    mod_part = _ada_fwd("ada_fwd", cs_all, ada_w, ada_b_mine)
    (mod_g,), _ = _exchange("gather_mod", [mod_part])
    mod_all = jnp.moveaxis(mod_g, 0, 2).reshape(nl, 2 * NDEV, NDEV * cw)
    mod_lat = lax.dynamic_slice_in_dim(mod_all, me, 1, axis=1)[:, 0, :]
    mod_ctx = mod_all[0, NDEV, :]

    def six(v):
        return [v[k * d:(k + 1) * d][None, :] for k in range(6)]

    sh1, s1, g1, sh2, s2, g2 = six(mod_lat[0])
    csh1, cs1 = six(mod_ctx)[:2]
    sh1b, s1b, g1b, sh2b, s2b, g2b = six(mod_lat[1])
    nmg, nfg = norm_mix_g, norm_ffn_g

    h_all = jnp.concatenate([ctx2, x2], axis=0)
    s01, sh01 = jnp.concatenate([cs1, s1], axis=0), jnp.concatenate([csh1, sh1], axis=0)
    xn_all, ((w_in_g,), _) = _normmod_fwd("l0_norm", h_all, nmg[0:1], s01, sh01, nct, comm=ride_norm)
    w_ssd_in = _unshard_cols(w_in_g)
    w_zx = w_ssd_in[:, :di + cdim]
    w_dt = jnp.pad(w_ssd_in[:, di + cdim:], ((0, 0), (0, LANES - 2 * h)))
    zx, ((w_out_g, pw2_g), _) = _mm("ssd_in_proj", xn_all, w_zx, "nn", ACT, comm=ride_proj)
    dtr = _mm("ssd_dt_proj", xn_all, w_dt, "nn")
    dt2 = jnp.moveaxis(dtr[:, :2 * h].reshape(tt, 2, h), 1, 0)
    bias2 = jnp.stack([ssd_dt_bias_f, ssd_dt_bias_b])
    alog2 = jnp.stack([ssd_a_log_f, ssd_a_log_b])
    xbc, xbc_pre, ((pw1_g, bpw1_g, dww_g, dwb_g, lng_g, lnb_g, bpw2_g), _) = _ssd_conv_fwd("ssd_conv", zx, conv_w_full, ssd_conv_b, di, tc,
                                                                                 comm=ride_conv)
    y2, hp2, ((fin_g, fout_g), _) = _ssd_fwd("ssd_scan", xbc, dt2, bias2, alog2, di, tc, comm=ride_scan)
    w_ssd_out = _unshard_rows(w_out_g)
    w_pw1, w_pw2 = _unshard_cols(pw1_g), _unshard_rows(pw2_g)
    w_fin, w_fout = _unshard_cols(fin_g), _unshard_rows(fout_g)
    dw_w_full = _unshard_cols(dww_g)
    b_pw1, dw_b, ln_g, ln_b, b_pw2 = (_unshard_cols(a) for a in (bpw1_g, dwb_g, lng_g, lnb_g, bpw2_g))
    dskip_e = jnp.repeat(ssd_d_skip, di // h, axis=1)
    yn = _ssd_gate_fwd("ssd_gate", y2, xbc, zx, dskip_e, ssd_norm_w, di, nct, t)
    mix0 = _mm("ssd_out_proj", yn, w_ssd_out, "nn")
    h1, xf0 = _resnorm_fwd("l0_res_norm", x2, mix0, g1, nfg[0:1], s2, sh2)
    u0 = _mm("ffn0_in", xf0, w_fin[0], "nn", ACT)
    hid0 = _swiglu_fwd("ffn0_act", u0)
    f0 = _mm("ffn0_out", hid0, w_fout[0], "nn")
    h2, xn1 = _resnorm_fwd("l1_norm", h1, f0, g2, nmg[1:2], s1b, sh1b)
    u1 = _mm("conf_pw1", xn1, w_pw1, "nn", ACT, bias=b_pw1)
    gl = _glu_fwd("conf_glu", u1)
    gl_h = _grid_t(gl[:, :ch], rows_g, GRID_W)
    v_ht = _strided_conv("conf_conv_h", gl_h, dw_w_full[:, :ch], dw_b[:, :ch], rows_g)
    v_v = _strided_conv("conf_conv_v", gl, dw_w_full[:, ch:], dw_b[:, ch:], GRID_W, x_col0=ch)
    v_h = _grid_t(v_ht, GRID_W, rows_g)
    sl = _ln_silu_fwd("conf_ln", v_h, v_v, ln_g, ln_b)
    mix1 = _mm("conf_pw2", sl, w_pw2, "nn", bias=b_pw2)
    h3, xf1 = _resnorm_fwd("l1_res_norm", h2, mix1, g1b, nfg[1:2], s2b, sh2b)
    u2 = _mm("ffn1_in", xf1, w_fin[1], "nn", ACT)
    hid1 = _swiglu_fwd("ffn1_act", u2)
    f1 = _mm("ffn1_out", hid1, w_fout[1], "nn")
    dh, sq, d_final_g = _final_loss("final_loss", h3, f1, tgt, g2b, final_norm_g[None, :])
    loss = lax.psum(0.5 * sq[0, 0] / d, AXES)

    zero2 = jnp.zeros((2, d), F32)

    def ffn_bwd(tag, dh, hin, xf, u, hid, f, gate, w_in, w_out, g_norm, s_mod):
        df, dgate, _ = _gate_bwd(tag + "_gate_bwd", dh, f, gate)
        dhid = _mm(tag + "_dhid", df, w_out, "nt", ACT)
        dw_out = _mm(tag + "_dwout", hid, df, "tn", WIRE)
        du = _swiglu_bwd(tag + "_act_bwd", u, dhid)
        dw_in = _mm(tag + "_dwin", xf, du, "tn", WIRE)
        dxf = _mm(tag + "_dx", du, w_in, "nt")
        s_2 = jnp.concatenate([s_mod, s_mod], axis=0)
        dh, dsh, ds, dg = _normmod_bwd(tag + "_norm_bwd", hin, dxf, dh, g_norm, s_2)
        return dh, dgate, dsh[1:2], ds[1:2], dg[1:2], dw_in, dw_out

    dh, d_g2b, d_sh2b, d_s2b, d_nfg1, g_fin1, g_fout1 = ffn_bwd("ffn1", dh, h3, xf1, u2, hid1, f1, g2b, w_fin[1], w_fout[1], nfg[1:2], s2b)
    dmix1, d_g1b, g_bpw2 = _gate_bwd("conf_gate_bwd", dh, mix1, g1b)
    dsl = _mm("conf_dsl", dmix1, w_pw2, "nt")
    g_pw2 = _mm("conf_dwpw2", sl, dmix1, "tn", WIRE)
    dv_lo, dv_v, g_lng, g_lnb = _ln_silu_bwd("conf_ln_bwd", v_h, v_v, dsl, ln_g, ln_b)
    dv_h = _grid_t(dv_lo, rows_g, GRID_W)
    w_flip = dw_w_full[::-1]
    dgl_h = _strided_conv("conf_conv_h_bwd", dv_h, w_flip[:, :ch], None, rows_g)
    dgl_v = _strided_conv("conf_conv_v_bwd", dv_v, w_flip[:, ch:], None, GRID_W)
    g_dww_h, g_dwb_h = _strided_conv_dw("conf_conv_h_dw", gl_h, dv_h, ck, rows_g)
    g_dww_v, g_dwb_v = _strided_conv_dw("conf_conv_v_dw", gl, dv_v, ck, GRID_W, x_col0=ch)
    g_dww, g_dwb = jnp.concatenate([g_dww_h, g_dww_v], axis=1), jnp.concatenate([g_dwb_h, g_dwb_v], axis=1)
    du1, g_bpw1 = _glu_bwd("conf_glu_bwd", u1, _grid_t(dgl_h, GRID_W, rows_g), dgl_v)
    g_pw1 = _mm("conf_dwpw1", xn1, du1, "tn", WIRE)
    dxn1 = _mm("conf_dx", du1, w_pw1, "nt")
    dh, dsh_, ds_, dg_ = _normmod_bwd("l1_norm_bwd", h2, dxn1, dh, nmg[1:2], jnp.concatenate([s1b, s1b], axis=0))
    d_sh1b, d_s1b, d_nmg1 = dsh_[1:2], ds_[1:2], dg_[1:2]
    dh, d_g2, d_sh2, d_s2, d_nfg0, g_fin0, g_fout0 = ffn_bwd("ffn0", dh, h1, xf0, u0, hid0, f0, g2, w_fin[0], w_fout[0], nfg[0:1], s2)
    dmix0, d_g1, _ = _gate_bwd("ssd_gate_res_bwd", dh, mix0, g1)
    dyn = _mm("ssd_dyn", dmix0, w_ssd_out, "nt")
    g_ssd_out = _mm("ssd_dwout", yn, dmix0, "tn", WIRE)
    dy, dzx, g_normw, ddsk_e = _ssd_gate_bwd("ssd_gate_bwd", dyn, y2, xbc, zx, dskip_e, ssd_norm_w, di, nct, tt)
    ride_scan_bwd = _Comm(scatter=[_flat3(_shard_cols(jnp.stack([g_fin0, g_fin1]))), _flat3(_shard_rows(jnp.stack([g_fout0, g_fout1])))])
    ride_conv_bwd = _Comm(scatter=[_shard_rows(g_ssd_out), _shard_cols(g_pw1), _shard_rows(g_pw2), _shard_cols(g_bpw1),
                                   _shard_cols(g_dww), _shard_cols(g_dwb), _shard_cols(g_lng), _shard_cols(g_lnb), _shard_cols(g_bpw2)])
    dxbc2, ddt2, g_alog2, g_bias2, (_, ffn_r) = _ssd_bwd("ssd_scan_bwd", xbc, dt2, bias2, alog2, dy, hp2, dskip_e, di, tc,
                                                         comm=ride_scan_bwd)
    dzx, g_convw, g_convb, (_, conv_r) = _ssd_conv_bwd("ssd_conv_bwd", zx, dxbc2, conv_w_full, xbc_pre, dzx, di, tc,
                                                       comm=ride_conv_bwd)
    ddt_p = jnp.pad(jnp.moveaxis(ddt2, 0, 1).reshape(tt, 2 * h), ((0, 0), (0, LANES - 2 * h))).astype(MXU)
    g_ssd_in = jnp.concatenate([_mm("ssd_dw_zx", xn_all, dzx, "tn", WIRE),
                                _mm("ssd_dw_dt", xn_all, ddt_p, "tn", WIRE)[:, :2 * h]], axis=1)
    dxn, (_, (ssd_in_r, convw_r)) = _mm("ssd_dx_zx", dzx, w_zx, "nt",
                                        comm=_Comm(scatter=[_shard_cols(g_ssd_in), _shard_cols(g_convw)]))
    dxn = _mm("ssd_dx_dt", ddt_p, w_dt, "nt", add=dxn)
    dh_all, dsh_, ds_, dg_ = _normmod_bwd("l0_norm_bwd", h_all, dxn, dh, nmg[0:1], s01, nct)
    grad_x = dh_all[tc:][None]
    d_csh1, d_sh1, d_cs1, d_s1 = dsh_[0:1], dsh_[1:2], ds_[0:1], ds_[1:2]
    d_nmg0 = dg_[0:1] + dg_[1:2]

    z1 = jnp.zeros((1, d), F32)
    dmod = jnp.concatenate([jnp.concatenate([d_sh1, d_s1, d_g1, d_sh2, d_s2, d_g2], axis=1),
                            jnp.concatenate([d_sh1b, d_s1b, d_g1b, d_sh2b, d_s2b, d_g2b], axis=1),
                            jnp.concatenate([d_csh1, d_cs1, z1, z1, z1, z1], axis=1)], axis=0)
    out = {}

    def put(name, res):
        w = args[name]
        out["grad_" + name], out["delta_" + name], out["new_m_" + name], out["new_v_" + name] = (r.reshape(w.shape) for r in res)

    def adam_big(name, slots, comm=None):
        return _adam("adam_" + name, slots, _flat2(args[name]), _flat2(args["m_" + name]), _flat2(args["v_" + name]), comm=comm)

    res, ((dmod_g,), _) = adam_big("ffn_w_in", ffn_r[0], comm=_Comm(gather=[dmod]))
    put("ffn_w_in", res)
    dmod_mine = lax.dynamic_slice_in_dim(dmod_g, me * cw, cw, axis=2)
    dmod16 = jnp.stack([jnp.concatenate([dmod_mine[:, 0], dmod_mine[:, 2]], axis=0),
                        jnp.concatenate([dmod_mine[:, 1], jnp.zeros((NDEV, cw), F32)], axis=0)])
    g_ada_w, dsc_part = _ada_bwd("ada_bwd", cs_all, ada_w, dmod16)
    g_ada_b = dmod[0:2] + jnp.concatenate([dmod[2:3], jnp.zeros((1, 6 * d), F32)], axis=0)

    d_dskip = jnp.sum(ddsk_e.reshape(h, di // h), axis=1)[None, :]
    rep = [dsc_part, g_ada_b, jnp.concatenate([d_nmg0, d_nmg1], axis=0), jnp.concatenate([d_nfg0, d_nfg1], axis=0),
           d_final_g, g_convb, g_bias2[0], g_bias2[1], g_alog2[0], g_alog2[1], d_dskip, g_normw]
    res, (rep_g, _) = _adam("adam_ada_w", _flat2(g_ada_w)[None], _flat2(ada_w), _flat2(m_ada_w), _flat2(v_ada_w),
                            comm=_Comm(gather=rep))
    put("ada_w", res)
    small_r = [convw_r] + list(conv_r[3:])

    for name, slots in zip(["ssd_w_in", "ssd_w_out", "conf_w_pw1", "conf_w_pw2", "ffn_w_out"],
                           [ssd_in_r, conv_r[0], conv_r[1], conv_r[2], ffn_r[1]]):
        put(name, adam_big(name, slots))
    small_names = ["ssd_conv_w", "conf_b_pw1", "conf_dw_w", "conf_dw_b", "conf_ln_g", "conf_ln_b", "conf_b_pw2",
                   "c_ctx", "ada_b", "norm_mix_g", "norm_ffn_g", "final_norm_g", "ssd_conv_b", "ssd_dt_bias_f", "ssd_dt_bias_b",
                   "ssd_a_log_f", "ssd_a_log_b", "ssd_d_skip", "ssd_norm_w"]
    slots = list(small_r) + list(rep_g)

    def as2(a):
        return a.reshape((1, -1)) if a.ndim == 1 else _flat2(a)

    res = _adam_small("adam_small", slots, [as2(args[n]) for n in small_names], [as2(args["m_" + n]) for n in small_names],
                      [as2(args["v_" + n]) for n in small_names], scale=(small_names.index("c_ctx"), c_ctx[None, :]))
    for k, name in enumerate(small_names):
        put(name, [r[k] for r in res])
    return (loss, grad_x, *[out["grad_" + n] for n in names], *[out["delta_" + n] for n in names],
            *[out["new_m_" + n] for n in names], *[out["new_v_" + n] for n in names])


def _flat3(a):
    return a.reshape((a.shape[0], -1, a.shape[-1]))
```

```python
import functools

import jax
import jax.numpy as jnp
from jax import lax
from jax.experimental import pallas as pl
from jax.experimental.pallas import tpu as pltpu

F32 = jnp.float32
MXU = jnp.bfloat16
WIRE = jnp.bfloat16
ACT = jnp.bfloat16
NDEV = 8
AXES = ("x", "y", "c")
SSD_STATE = 128
SSD_CHUNK = 128
GRID_W = 64
EPS = 1e-6
ROW_TILE = 256
LANES = 128
ADAM_LR, ADAM_B1, ADAM_B2, ADAM_EPS, ADAM_WD, ADAM_STEP = 0.001, 0.9, 0.999, 1e-08, 0.01, 10
VMEM_CAP = 56 * 2 ** 20
MESH_ID = pl.DeviceIdType.MESH


def _pick(dim, cands):
    for c in cands:
        if dim % c == 0:
            return c
    return dim


def _nbytes(shape, dtype):
    n = 1
    for s in shape:
        n *= s
    return n * jnp.dtype(dtype).itemsize


def _vmem(nbytes):
    return int(min(VMEM_CAP, max(24 * 2 ** 20, 2 * nbytes + 8 * 2 ** 20)))


def _sigmoid(x):
    return 1.0 / (1.0 + jnp.exp(-x))


def _silu(x):
    return x * _sigmoid(x)


def _dsilu(x):
    s = _sigmoid(x)
    return s * (1.0 + x * (1.0 - s))


def _softplus(x):
    return jnp.maximum(x, 0.0) + jnp.log(1.0 + jnp.exp(-jnp.abs(x)))


def _dot(a, b, dims):
    return lax.dot_general(a.astype(MXU), b.astype(MXU), (dims, ((), ())), preferred_element_type=F32)


NN, NT, TN = ((1,), (0,)), ((1,), (1,)), ((0,), (0,))


def _split(a, parts):
    out = []
    for _ in range(parts):
        p = a.astype(MXU)
        out.append(p)
        a = a - p.astype(F32)
    return out


def _dot_lx(e, a, dims, parts=2):
    return sum(lax.dot_general(e, p, (dims, ((), ())), preferred_element_type=F32) for p in _split(a, parts))


def _dot_rx(a, e, dims, parts=2):
    return sum(lax.dot_general(p, e, (dims, ((), ())), preferred_element_type=F32) for p in _split(a, parts))


class _Comm:
    def __init__(self, gather=(), scatter=()):
        self.gather, self.scatter = list(gather), list(scatter)
        self.ng, self.n = len(self.gather), len(self.gather) + len(self.scatter)
        self.operands = self.gather + self.scatter
        self.specs = [pl.BlockSpec(memory_space=pl.ANY)] * self.n
        self.out_shape = ([jax.ShapeDtypeStruct((NDEV,) + a.shape, a.dtype) for a in self.gather]
                          + [jax.ShapeDtypeStruct(a.shape, a.dtype) for a in self.scatter])
        self.scratch = [pltpu.SemaphoreType.DMA((self.n, 7)), pltpu.SemaphoreType.DMA((self.n, 7)),
                        pltpu.SemaphoreType.DMA((self.n,))]

    def split(self, res):
        return res[:self.ng], res[self.ng:]

    def _copies(self, ins, outs, sems):
        send, recv, loc = sems
        ng, n = self.ng, self.n
        x, y, c = lax.axis_index("x"), lax.axis_index("y"), lax.axis_index("c")
        me, sib = (x, y, c), (x, y, 1 - c)
        chips = [(1 - x, y), (x, 1 - y), (1 - x, 1 - y)]

        def slot(p):
            return 4 * p[0] + 2 * p[1] + p[2]

        def rcopy(a, k, src, dst, to):
            return functools.partial(pltpu.make_async_remote_copy, src_ref=src, dst_ref=dst, send_sem=send.at[a, k],
                                     recv_sem=recv.at[a, k], device_id=to, device_id_type=MESH_ID)

        local = [functools.partial(pltpu.make_async_copy, ins[a] if a < ng else ins[a].at[slot(me)], outs[a].at[slot(me)],
                                   loc.at[a]) for a in range(n)]
        rel = [(fx, fy, fc) for fx in (0, 1) for fy in (0, 1) for fc in (0, 1)][1:]
        first, landed, passed = [], [], []
        for a in range(ng, n):
            for k, (fx, fy, fc) in enumerate(rel):
                p = (1 - x if fx else x, 1 - y if fy else y, 1 - c if fc else c)
                first.append(rcopy(a, k, ins[a].at[slot(p)], outs[a].at[slot(me)], p))
                blk = outs[a].at[slot(p)]
                landed.append(rcopy(a, k, blk, blk, me))
        for a in range(ng):
            dst = outs[a].at[slot(me)]
            first.append(rcopy(a, 0, ins[a], dst, sib))
            first += [rcopy(a, 1 + j, ins[a], dst, (*ch, c)) for j, ch in enumerate(chips)]
            blk = outs[a].at[slot(sib)]
            landed.append(rcopy(a, 0, blk, blk, me))
            for j, ch in enumerate(chips):
                blk = outs[a].at[slot((*ch, c))]
                passed.append((rcopy(a, 1 + j, blk, blk, me), rcopy(a, 4 + j, blk, blk, sib)))
                blk = outs[a].at[slot((*ch, 1 - c))]
                landed.append(rcopy(a, 4 + j, blk, blk, me))
        return local, first, passed, landed

    def start(self, ins, outs, sems):
        local, first, _, _ = self._copies(ins, outs, sems)
        for make in local + first:
            make().start()

    def finish(self, ins, outs, sems):
        local, first, passed, landed = self._copies(ins, outs, sems)
        onward = []
        for arrived, forward in passed:
            arrived().wait_recv()
            onward.append(forward())
            onward[-1].start()
        for make in landed:
            make().wait_recv()
        for make in first:
            make().wait_send()
        for cp in onward:
            cp.wait_send()
        for make in local:
            make().wait()


def _carry(body, comm, n_in, n_out, grid):
    if comm is None:
        return body
    n = comm.n

    def wrapped(*refs):
        own_in, c_in = refs[:n_in], refs[n_in:n_in + n]
        own_out, c_out = refs[n_in + n:n_in + n + n_out], refs[n_in + n + n_out:n_in + 2 * n + n_out]
        own_scr, sems = refs[n_in + 2 * n + n_out:-3], refs[-3:]
        ids = [pl.program_id(ax) for ax in range(len(grid))]
        first, last = ids[0] == 0, ids[0] == grid[0] - 1
        for ax in range(1, len(grid)):
            first, last = first & (ids[ax] == 0), last & (ids[ax] == grid[ax] - 1)

        @pl.when(first)
        def _():
            comm.start(c_in, c_out, sems)

        body(*own_in, *own_out, *own_scr)

        @pl.when(last)
        def _():
            comm.finish(c_in, c_out, sems)

    return wrapped


def _exchange(name, gather, scatter=()):
    comm = _Comm(gather, scatter)
    n = comm.n

    def body(*refs):
        ins, outs, sems = refs[:n], refs[n:2 * n], refs[2 * n:]
        comm.start(ins, outs, sems)
        comm.finish(ins, outs, sems)

    res = pl.pallas_call(body, name=name, out_shape=comm.out_shape, in_specs=comm.specs, out_specs=comm.specs,
                         scratch_shapes=comm.scratch)(*comm.operands)
    return comm.split(res)


def _hbm(a):
    return pltpu.with_memory_space_constraint(a, pltpu.HBM)


def _divs(dim, mult):
    return [dim] + [dim // parts for parts in range(2, dim // mult + 1) if dim % parts == 0 and (dim // parts) % mult == 0]


MM_VMEM_BUDGET = 40 * 2 ** 20
GRID_STEP_US = 0.35
HBM_BYTES_PER_US = 3.0e6


def _mm_tiles(m, n, k, sizes, mode, has_add):
    sa, sb, so = sizes
    sub = 16
    best = None
    for tk in _divs(k, LANES):
        for tn in _divs(n, LANES):
            for tm in _divs(m, LANES if mode == "tn" else sub):
                nk = k // tk
                out_t = tm * tn
                est = (2 * (tm * tk * sa + tk * tn * sb) + 2 * out_t * so + 2 * (tm * tk + tk * tn) + 4 * out_t
                       + (4 * out_t if nk > 1 else 0) + (8 * out_t if has_add else 0))
                if est > MM_VMEM_BUDGET:
                    continue
                steps = (m // tm) * (n // tn) * nk
                cost = steps * GRID_STEP_US + (tm * tk * sa + tk * tn * sb + out_t * so) / HBM_BYTES_PER_US
                if best is None or cost < best[0]:
                    best = (cost, tm, tn, tk, est)
    assert best is not None, (m, n, k)
    return best[1:]


def _mm(name, a, b, mode, out_dtype=F32, bias=None, add=None, comm=None):
    if mode == "nn":
        (m, k), (k2, n) = a.shape, b.shape
    elif mode == "nt":
        (m, k), (n, k2) = a.shape, b.shape
    else:
        (k, m), (k2, n) = a.shape, b.shape
    assert k == k2, (name, a.shape, b.shape)
    sizes = (a.dtype.itemsize, b.dtype.itemsize, jnp.dtype(out_dtype).itemsize)
    tm, tn, tk, est = _mm_tiles(m, n, k, sizes, mode, add is not None)
    nk = k // tk
    dims = {"nn": NN, "nt": NT, "tn": TN}[mode]
    a_spec = pl.BlockSpec((tk, tm), lambda i, j, kk: (kk, i)) if mode == "tn" else pl.BlockSpec((tm, tk), lambda i, j, kk: (i, kk))
    b_spec = pl.BlockSpec((tn, tk), lambda i, j, kk: (j, kk)) if mode == "nt" else pl.BlockSpec((tk, tn), lambda i, j, kk: (kk, j))
    extra, extra_specs = [], []
    if bias is not None:
        extra.append(bias)
        extra_specs.append(pl.BlockSpec((1, tn), lambda i, j, kk: (0, j)))
    if add is not None:
        extra.append(add)
        extra_specs.append(pl.BlockSpec((tm, tn), lambda i, j, kk: (i, j)))

    def finish(r, extras, o_ref):
        for e in extras:
            r = r + e[...].astype(F32)
        o_ref[...] = r.astype(o_ref.dtype)

    def body_acc(*refs):
        a_ref, b_ref = refs[:2]
        o_ref, acc = refs[-2:]
        kk = pl.program_id(2)

        @pl.when(kk == 0)
        def _():
            acc[...] = jnp.zeros_like(acc)

        acc[...] += _dot(a_ref[...], b_ref[...], dims)

        @pl.when(kk == nk - 1)
        def _():
            finish(acc[...], refs[2:-2], o_ref)

    def body_one(*refs):
        finish(_dot(refs[0][...], refs[1][...], dims), refs[2:-1], refs[-1])

    cm = comm if comm is not None else _Comm()
    grid = (m // tm, n // tn, nk)
    res = pl.pallas_call(
        _carry(body_acc if nk > 1 else body_one, comm, 2 + len(extra), 1, grid), name=name, grid=grid,
        out_shape=[pltpu.HBM((m, n), out_dtype)] + cm.out_shape,
        in_specs=[a_spec, b_spec] + extra_specs + cm.specs,
        out_specs=[pl.BlockSpec((tm, tn), lambda i, j, kk: (i, j))] + cm.specs,
        scratch_shapes=([pltpu.VMEM((tm, tn), F32)] if nk > 1 else []) + (cm.scratch if comm is not None else []),
        compiler_params=pltpu.CompilerParams(
            dimension_semantics=("parallel", "parallel", "arbitrary") if comm is None else ("arbitrary",) * 3,
            vmem_limit_bytes=int(min(VMEM_CAP, est + 12 * 2 ** 20))),
    )(*[_hbm(v) for v in (a, b, *extra)], *cm.operands)
    return res[0] if comm is None else (res[0], cm.split(res[1:]))


def _ri(arr, w=None, cb=0, ro=0, lead=None):
    return (arr, arr.shape[-1] if w is None else w, cb, ro, lead)


def _rowwise(name, fn, nrows, row_ins, bc_ins, outs, accs=(), comm=None):
    tr = min(ROW_TILE, nrows)
    assert nrows % tr == 0
    in_specs = []
    for (arr, w, cb, ro, lead) in row_ins:
        if lead is None:
            in_specs.append(pl.BlockSpec((tr, w), lambda i, cb=cb, ro=ro: (jnp.maximum(i + ro, 0), cb)))
        else:
            in_specs.append(pl.BlockSpec((None, tr, w), lambda i, cb=cb, ro=ro, lead=lead: (lead, jnp.maximum(i + ro, 0), cb)))
    for arr in bc_ins:
        in_specs.append(pl.BlockSpec(arr.shape, lambda i, nd=arr.ndim: (0,) * nd))
    outs = [o if len(o) == 4 else (o[0], o[1], o[0], 0) for o in outs]
    out_shape = [pltpu.HBM((nrows, total), dt) for _, dt, total, _ in outs] + [pltpu.HBM(s, F32) for s in accs]
    out_specs = ([pl.BlockSpec((tr, c), lambda i, cb=cb: (i, cb)) for c, _, _, cb in outs]
                 + [pl.BlockSpec(s, lambda i: (0, 0)) for s in accs])
    nr, nb, no = len(row_ins), len(bc_ins), len(outs)

    def body(*refs):
        i = pl.program_id(0)
        rows = [r[...].astype(F32) for r in refs[:nr]]
        bcs = [r[...] for r in refs[nr:nr + nb]]
        o, a = fn(rows, bcs, i)
        for ref, val in zip(refs[nr + nb:nr + nb + no], o):
            ref[...] = val.astype(ref.dtype)
        for ref, val in zip(refs[nr + nb + no:], a):
            @pl.when(i == 0)
            def _(ref=ref, val=val):
                ref[...] = val

            @pl.when(i > 0)
            def _(ref=ref, val=val):
                ref[...] += val

    est = sum(tr * w * arr.dtype.itemsize for (arr, w, _, _, _) in row_ins) + sum(tr * o[0] * 4 for o in outs)
    cm = comm if comm is not None else _Comm()
    nout = no + len(accs)
    res = pl.pallas_call(
        _carry(body, comm, nr + nb, nout, (nrows // tr,)), name=name, grid=(nrows // tr,), out_shape=out_shape + cm.out_shape,
        in_specs=in_specs + cm.specs, out_specs=out_specs + cm.specs, scratch_shapes=cm.scratch if comm is not None else [],
        compiler_params=pltpu.CompilerParams(dimension_semantics=("arbitrary",), vmem_limit_bytes=_vmem(3 * est)),
    )(*[_hbm(r[0]) for r in row_ins], *[_hbm(v) for v in bc_ins], *cm.operands)
    if comm is None:
        return res[:no], res[no:]
    return res[:no], res[no:nout], cm.split(res[nout:])


def _colsum(v):
    return jnp.sum(v, axis=0, keepdims=True)


def _normmod_fwd(name, h, g, s, sh, nct=0, comm=None):
    d = h.shape[1]

    def fn(rows, bcs, i):
        hh, (g_, s_, sh_) = rows[0], bcs
        s1 = jnp.where(i < nct, s_[0:1], s_[1:2])
        sh1 = jnp.where(i < nct, sh_[0:1], sh_[1:2])
        r = lax.rsqrt(jnp.mean(hh * hh, axis=-1, keepdims=True) + EPS)
        return [hh * r * g_ * (1.0 + s1) + sh1], []

    res = _rowwise(name, fn, h.shape[0], [_ri(h)], [g, s, sh], [(d, MXU)], comm=comm)
    return res[0][0] if comm is None else (res[0][0], res[2])


def _normmod_bwd(name, h, dxn, dres, g, s, nct=0):
    d = h.shape[1]

    def fn(rows, bcs, i):
        hh, dx, dr = rows
        g_, s_ = bcs
        ctx = i < nct
        s1 = jnp.where(ctx, s_[0:1], s_[1:2])
        r = lax.rsqrt(jnp.mean(hh * hh, axis=-1, keepdims=True) + EPS)
        hr = hh * r
        dy = dx * (1.0 + s1)
        u = dy * g_
        dh = r * u - hr * (r * r) * jnp.mean(u * hh, axis=-1, keepdims=True)
        dh = dh + jnp.where(ctx, 0.0, dr)

        def seg(v):
            v = _colsum(v)
            return jnp.concatenate([jnp.where(ctx, v, 0.0), jnp.where(ctx, 0.0, v)], axis=0)

        return [dh], [seg(dx), seg(dx * hr * g_), seg(dy * hr)]

    (dh,), (dsh, ds, dg) = _rowwise(name, fn, h.shape[0], [_ri(h), _ri(dxn), _ri(dres, ro=-nct)], [g, s],
                                    [(d, F32)], [(2, d)] * 3)
    return dh, dsh, ds, dg


def _resnorm_fwd(name, h, y, gate, g, s, sh):
    d = h.shape[1]

    def fn(rows, bcs, i):
        hh, yy = rows
        gate_, g_, s_, sh_ = bcs
        hn = hh + gate_ * yy
        r = lax.rsqrt(jnp.mean(hn * hn, axis=-1, keepdims=True) + EPS)
        return [hn, hn * r * g_ * (1.0 + s_) + sh_], []

    return _rowwise(name, fn, h.shape[0], [_ri(h), _ri(y)], [gate, g, s, sh], [(d, F32), (d, MXU)])[0]


def _gate_bwd(name, dh, y, gate):
    d = dh.shape[1]

    def fn(rows, bcs, i):
        dd, yy = rows
        dy = dd * bcs[0]
        return [dy], [_colsum(dd * yy), _colsum(dy)]

    (dy,), (dgate, dbias) = _rowwise(name, fn, dh.shape[0], [_ri(dh), _ri(y)], [gate], [(d, MXU)], [(1, d)] * 2)
    return dy, dgate, dbias


def _swiglu_fwd(name, u):
    f = u.shape[1] // 2

    def fn(rows, bcs, i):
        return [_silu(rows[0]) * rows[1]], []

    return _rowwise(name, fn, u.shape[0], [_ri(u, f, 0), _ri(u, f, 1)], [], [(f, MXU)])[0][0]


def _swiglu_bwd(name, u, dhid):
    f = u.shape[1] // 2

    def fn(rows, bcs, i):
        a, b, dd = rows
        return [jnp.concatenate([dd * b * _dsilu(a), dd * _silu(a)], axis=1)], []

    return _rowwise(name, fn, u.shape[0], [_ri(u, f, 0), _ri(u, f, 1), _ri(dhid)], [], [(2 * f, MXU)])[0][0]


def _glu_fwd(name, u):
    d = u.shape[1] // 2

    def fn(rows, bcs, i):
        return [rows[0] * _sigmoid(rows[1])], []

    return _rowwise(name, fn, u.shape[0], [_ri(u, d, 0), _ri(u, d, 1)], [], [(d, F32)])[0][0]


def _glu_bwd(name, u, dgl_lo, dgl_hi):
    d = u.shape[1] // 2

    def fn(rows, bcs, i):
        a, b = rows[:2]
        dd = jnp.concatenate(rows[2:], axis=1)
        sg = _sigmoid(b)
        du = jnp.concatenate([dd * sg, dd * a * sg * (1.0 - sg)], axis=1)
        return [du], [_colsum(du)]

    (du,), (db,) = _rowwise(name, fn, u.shape[0], [_ri(u, d, 0), _ri(u, d, 1), _ri(dgl_lo), _ri(dgl_hi)], [], [(2 * d, MXU)],
                            [(1, 2 * d)])
    return du, db


def _ln_silu_fwd(name, v_lo, v_hi, g, b):
    d = 2 * v_lo.shape[1]

    def fn(rows, bcs, i):
        vv = jnp.concatenate(rows, axis=1)
        mu = jnp.mean(vv, axis=-1, keepdims=True)
        xc = vv - mu
        rs = lax.rsqrt(jnp.mean(xc * xc, axis=-1, keepdims=True) + EPS)
        return [_silu(xc * rs * bcs[0] + bcs[1])], []

    return _rowwise(name, fn, v_lo.shape[0], [_ri(v_lo), _ri(v_hi)], [g, b], [(d, MXU)])[0][0]


def _ln_silu_bwd(name, v_lo, v_hi, ds, g, b):
    ch = v_lo.shape[1]

    def fn(rows, bcs, i):
        vv, dd = jnp.concatenate(rows[:2], axis=1), rows[2]
        mu = jnp.mean(vv, axis=-1, keepdims=True)
        xc = vv - mu
        rs = lax.rsqrt(jnp.mean(xc * xc, axis=-1, keepdims=True) + EPS)
        xh = xc * rs
        dln = dd * _dsilu(xh * bcs[0] + bcs[1])
        dxh = dln * bcs[0]
        dv = rs * (dxh - jnp.mean(dxh, axis=-1, keepdims=True) - xh * jnp.mean(dxh * xh, axis=-1, keepdims=True))
        return [dv[:, :ch], dv[:, ch:]], [_colsum(dln * xh), _colsum(dln)]

    (dv_lo, dv_hi), (dg, db) = _rowwise(name, fn, v_lo.shape[0], [_ri(v_lo), _ri(v_hi), _ri(ds)], [g, b],
                                        [(ch, F32), (ch, F32)], [(1, 2 * ch)] * 2)
    return dv_lo, dv_hi, dg, db


def _final_loss(name, h, f, target, gate, gf):
    d = h.shape[1]

    def fn(rows, bcs, i):
        hh, ff, tg = rows
        gate_, g_ = bcs
        hn = hh + gate_ * ff
        r = lax.rsqrt(jnp.mean(hn * hn, axis=-1, keepdims=True) + EPS)
        hr = hn * r
        err = hr * g_ - tg
        dout = err * (1.0 / d)
        u = dout * g_
        dh = r * u - hr * (r * r) * jnp.mean(u * hn, axis=-1, keepdims=True)
        sq = jnp.sum(_colsum(err * err), axis=1, keepdims=True)
        return [dh], [jnp.broadcast_to(sq, (1, LANES)), _colsum(dout * hr)]

    (dh,), (sq, dgf) = _rowwise(name, fn, h.shape[0], [_ri(h), _ri(f), _ri(target)], [gate, gf], [(d, F32)], [(1, LANES), (1, d)])
    return dh, sq, dgf


GAP = 8


def _gapped(ref_rows, buf, tc, tt):
    cb = buf.shape[1]
    zero = jnp.zeros((GAP, cb), F32)
    buf[0:GAP, :] = zero
    buf[GAP + tc:2 * GAP + tc, :] = zero
    buf[2 * GAP + tt:, :] = zero
    buf[GAP:GAP + tc, :] = ref_rows[0:tc]
    buf[2 * GAP + tc:2 * GAP + tt, :] = ref_rows[tc:tt]
    return buf[...]


def _ungapped(v, tc, tt):
    return jnp.concatenate([v[GAP:GAP + tc], v[2 * GAP + tc:2 * GAP + tt]], axis=0)


def _shift_rows(x, o):
    return x if o == 0 else pltpu.roll(x, (-o) % x.shape[0], 0)


def _ssd_conv_fwd(name, zx, w, b, di, tc, comm=None):
    tt, kc, cd = zx.shape[0], w.shape[0], w.shape[1]
    cb = _pick(cd, (LANES,))
    off = di // cb
    assert kc // 2 < GAP and tc % GAP == 0 and tt % GAP == 0

    def body(x_ref, w_ref, b_ref, o_ref, pre_ref, xp):
        x = _gapped(x_ref[...].astype(F32), xp, tc, tt)
        acc = jnp.broadcast_to(b_ref[...], x.shape)
        for k in range(kc):
            acc = acc + w_ref[k:k + 1, :] * _shift_rows(x, k - kc // 2)
        acc = _ungapped(acc, tc, tt)
        pre_ref[...] = acc.astype(pre_ref.dtype)
        o_ref[...] = _silu(acc).astype(o_ref.dtype)

    cm = comm if comm is not None else _Comm()
    blk = pl.BlockSpec((tt, cb), lambda j: (0, j))
    res = pl.pallas_call(
        _carry(body, comm, 3, 2, (cd // cb,)), name=name, grid=(cd // cb,),
        out_shape=[pltpu.HBM((tt, cd), ACT), pltpu.HBM((tt, cd), ACT)] + cm.out_shape,
        in_specs=[pl.BlockSpec((tt, cb), lambda j: (0, j + off)), pl.BlockSpec((kc, cb), lambda j: (0, j)),
                  pl.BlockSpec((1, cb), lambda j: (0, j))] + cm.specs,
        out_specs=[blk, blk] + cm.specs,
        scratch_shapes=[pltpu.VMEM((tt + 3 * GAP, cb), F32)] + (cm.scratch if comm is not None else []),
        compiler_params=pltpu.CompilerParams(dimension_semantics=("arbitrary",), vmem_limit_bytes=_vmem(5 * tt * cb * 4)),
    )(_hbm(zx), _hbm(w), _hbm(b), *cm.operands)
    return res[0], res[1], cm.split(res[2:])


def _ssd_conv_bwd(name, zx, dact2, w, pre, dzx, di, tc, comm=None):
    tt, kc, cd = zx.shape[0], w.shape[0], w.shape[1]
    cb = _pick(cd, (LANES,))
    off = di // cb

    def body(x_ref, d0_ref, d1_ref, w_ref, pre_ref, _, dx_ref, dw_ref, db_ref, xp, dp):
        x = _gapped(x_ref[...].astype(F32), xp, tc, tt)
        dpre = (d0_ref[...].astype(F32) + d1_ref[...].astype(F32)) * _dsilu(pre_ref[...].astype(F32))
        db_ref[...] = _colsum(dpre)
        dpre = _gapped(dpre, dp, tc, tt)
        dx = jnp.zeros_like(x)
        for k in range(kc):
            sh = _shift_rows(dpre, -(k - kc // 2))
            dx = dx + w_ref[k:k + 1, :] * sh
            dw_ref[k:k + 1, :] = _colsum(sh * x)
        dx_ref[...] = _ungapped(dx, tc, tt).astype(dx_ref.dtype)

    cm = comm if comm is not None else _Comm()
    res = pl.pallas_call(
        _carry(body, comm, 6, 3, (cd // cb,)), name=name, grid=(cd // cb,),
        out_shape=[pltpu.HBM(dzx.shape, dzx.dtype), pltpu.HBM((kc, cd), F32), pltpu.HBM((1, cd), F32)] + cm.out_shape,
        in_specs=[pl.BlockSpec((tt, cb), lambda j: (0, j + off)), pl.BlockSpec((None, tt, cb), lambda j: (0, 0, j)),
                  pl.BlockSpec((None, tt, cb), lambda j: (1, 0, j)), pl.BlockSpec((kc, cb), lambda j: (0, j)),
                  pl.BlockSpec((tt, cb), lambda j: (0, j)), pl.BlockSpec(memory_space=pl.ANY)] + cm.specs,
        out_specs=[pl.BlockSpec((tt, cb), lambda j: (0, j + off)), pl.BlockSpec((kc, cb), lambda j: (0, j)),
                   pl.BlockSpec((1, cb), lambda j: (0, j))] + cm.specs,
        input_output_aliases={5: 0},
        scratch_shapes=[pltpu.VMEM((tt + 3 * GAP, cb), F32)] * 2 + (cm.scratch if comm is not None else []),
        compiler_params=pltpu.CompilerParams(dimension_semantics=("arbitrary",), vmem_limit_bytes=_vmem(10 * tt * cb * 4)),
    )(_hbm(zx), _hbm(dact2), _hbm(dact2), _hbm(w), _hbm(pre), _hbm(dzx), *cm.operands)
    return res[0], res[1], res[2], cm.split(res[3:])


def _strided_conv(name, x, w, b, stride, x_col0=0):
    t, ch = x.shape[0], w.shape[1]
    kk = w.shape[0]
    pad = (kk // 2) * stride
    cb = _pick(ch, (LANES,))
    has_b = b is not None

    def body(*refs):
        x_ref, w_ref = refs[:2]
        o_ref, xp = refs[-2:]
        xp[0:pad, :] = jnp.zeros((pad, cb), F32)
        xp[pad + t:, :] = jnp.zeros((pad, cb), F32)
        xp[pad:pad + t, :] = x_ref[...]
        acc = jnp.broadcast_to(refs[2][...], (t, cb)) if has_b else jnp.zeros((t, cb), F32)
        for k in range(kk):
            acc = acc + w_ref[k:k + 1, :] * xp[k * stride:k * stride + t, :]
        o_ref[...] = acc

    xoff = x_col0 // cb
    ins, specs = [x, w], [pl.BlockSpec((t, cb), lambda j: (0, j + xoff)), pl.BlockSpec((kk, cb), lambda j: (0, j))]
    if has_b:
        ins.append(b)
        specs.append(pl.BlockSpec((1, cb), lambda j: (0, j)))
    return pl.pallas_call(
        body, name=name, grid=(ch // cb,), out_shape=pltpu.HBM((t, ch), F32), in_specs=specs,
        out_specs=pl.BlockSpec((t, cb), lambda j: (0, j)), scratch_shapes=[pltpu.VMEM((t + 2 * pad, cb), F32)],
        compiler_params=pltpu.CompilerParams(dimension_semantics=("parallel",), vmem_limit_bytes=_vmem(6 * t * cb * 4)),
    )(*[_hbm(v) for v in ins])


def _strided_conv_dw(name, x, dv, kk, stride, x_col0=0):
    t, ch = dv.shape
    pad = (kk // 2) * stride
    cb = _pick(ch, (LANES,))

    def body(x_ref, d_ref, dw_ref, db_ref, xp):
        xp[0:pad, :] = jnp.zeros((pad, cb), F32)
        xp[pad + t:, :] = jnp.zeros((pad, cb), F32)
        xp[pad:pad + t, :] = x_ref[...]
        d = d_ref[...]
        for k in range(kk):
            dw_ref[k:k + 1, :] = _colsum(d * xp[k * stride:k * stride + t, :])
        db_ref[...] = _colsum(d)

    blk = pl.BlockSpec((t, cb), lambda j: (0, j))
    xoff = x_col0 // cb
    return pl.pallas_call(
        body, name=name, grid=(ch // cb,), out_shape=[pltpu.HBM((kk, ch), F32), pltpu.HBM((1, ch), F32)],
        in_specs=[pl.BlockSpec((t, cb), lambda j: (0, j + xoff)), blk], out_specs=[pl.BlockSpec((kk, cb), lambda j: (0, j)), pl.BlockSpec((1, cb), lambda j: (0, j))],
        scratch_shapes=[pltpu.VMEM((t + 2 * pad, cb), F32)],
        compiler_params=pltpu.CompilerParams(dimension_semantics=("parallel",), vmem_limit_bytes=_vmem(6 * t * cb * 4)),
    )(_hbm(x), _hbm(dv))


def _grid_t(a, n1, n2):
    return a.reshape(n1, n2, a.shape[-1]).swapaxes(0, 1).reshape(n1 * n2, a.shape[-1])


def _chunk_order(d, i, ncc, nc):
    back = jnp.where(i < ncc, ncc - 1 - i, nc - 1 - (i - ncc))
    return jnp.where(d == 0, i, back)


def _ssd_chunk_setup(d, dt_raw, bias, a_log, q, h, di):
    p = di // h
    dt = _softplus(dt_raw + bias)
    a_neg = -jnp.exp(a_log)
    delta = dt * a_neg
    r = lax.broadcasted_iota(jnp.int32, (q, q), 0)
    c = lax.broadcasted_iota(jnp.int32, (q, q), 1)
    sgn = 1 - 2 * d
    mask = (r - c) * sgn >= 0
    mask_t = (c - r) * sgn >= 0
    a = _dot_lx(mask.astype(MXU), delta, NN, parts=3)
    tot = _colsum(delta)
    ea, dte, cd = jnp.exp(a), jnp.exp(tot - a), jnp.exp(tot)
    hh = lax.broadcasted_iota(jnp.int32, (h, di), 0)
    cc = lax.broadcasted_iota(jnp.int32, (h, di), 1)
    e = (cc // p == hh).astype(MXU)
    ex = _dot_rx(jnp.concatenate([dt, ea, dte, jnp.broadcast_to(cd, (8, h))], axis=0), e, NN)
    eye = (lax.broadcasted_iota(jnp.int32, (h, h), 0) == lax.broadcasted_iota(jnp.int32, (h, h), 1)).astype(MXU)
    a_t = _dot_lx(eye, a, NT, parts=3)
    return dict(dt=dt, a_neg=a_neg, a=a, a_t=a_t, mask=mask, mask_t=mask_t, e=e,
                dt_e=ex[0:q], ea_e=ex[q:2 * q], dte_e=ex[2 * q:3 * q], cd_e=ex[3 * q:3 * q + 1])


def _pick_heads(r, q, hpg, p):
    lane = lax.broadcasted_iota(jnp.int32, (q, hpg * p), 1) // p
    out = jnp.zeros((q, hpg * p), F32)
    for j in range(hpg):
        out = out + jnp.where(lane == j, r[j * q:(j + 1) * q], 0.0)
    return out


def _ssd_fwd(name, xbc, dt2, bias2, alog2, di, tc, comm=None):
    tt, cd = xbc.shape
    h = dt2.shape[-1]
    q, n = SSD_CHUNK, SSD_STATE
    gn = (cd - di) // 2
    g = gn // n
    hpg, p = h // g, di // h
    gp = hpg * p
    nc, ncc = tt // q, tc // q
    assert di % gn == 0

    def body(x_ref, b_ref, c_ref, dt_ref, bias_ref, alog_ref, y_ref, hp_ref, ht):
        d, i = pl.program_id(0), pl.program_id(1)

        @pl.when(i == 0)
        def _():
            ht[...] = jnp.zeros_like(ht)

        s = _ssd_chunk_setup(d, dt_ref[...], bias_ref[...], alog_ref[...], q, h, di)
        xd = x_ref[...].astype(F32) * s["dt_e"]
        hp_ref[...] = ht[...].astype(hp_ref.dtype)
        for gi in range(g):
            bg, cg = b_ref[:, gi * n:(gi + 1) * n].astype(MXU), c_ref[:, gi * n:(gi + 1) * n].astype(MXU)
            sl = slice(gi * gp, (gi + 1) * gp)
            sc = _dot(cg, bg, NT)
            ms = []
            for j in range(hpg):
                hd = gi * hpg + j
                seg = s["a"][:, hd:hd + 1] - s["a_t"][hd:hd + 1, :]
                ms.append(sc * jnp.exp(jnp.where(s["mask"], seg, -jnp.inf)))
            xdg = xd[:, sl]
            ydiag = _pick_heads(_dot(jnp.concatenate(ms, axis=0), xdg, NN), q, hpg, p)
            htg = ht[:, sl]
            y_ref[:, sl] = ydiag + _dot(cg, htg, NN) * s["ea_e"][:, sl]
            ht[:, sl] = s["cd_e"][:, sl] * htg + _dot(bg, xdg * s["dte_e"][:, sl], TN)

    def cidx(d, i):
        return _chunk_order(d, i, ncc, nc)

    cm = comm if comm is not None else _Comm()
    res = pl.pallas_call(
        _carry(body, comm, 6, 2, (2, nc)), name=name, grid=(2, nc),
        out_shape=[pltpu.HBM((2, tt, di), F32), pltpu.HBM((2, nc, n, di), ACT)] + cm.out_shape,
        in_specs=[pl.BlockSpec((q, di), lambda d, i: (cidx(d, i), 0)),
                  pl.BlockSpec((q, gn), lambda d, i: (cidx(d, i), di // gn)),
                  pl.BlockSpec((q, gn), lambda d, i: (cidx(d, i), di // gn + 1)),
                  pl.BlockSpec((None, q, h), lambda d, i: (d, cidx(d, i), 0)),
                  pl.BlockSpec((None, 1, h), lambda d, i: (d, 0, 0)),
                  pl.BlockSpec((None, 1, h), lambda d, i: (d, 0, 0))] + cm.specs,
        out_specs=[pl.BlockSpec((None, q, di), lambda d, i: (d, cidx(d, i), 0)),
                   pl.BlockSpec((None, None, n, di), lambda d, i: (d, cidx(d, i), 0, 0))] + cm.specs,
        scratch_shapes=[pltpu.VMEM((n, di), F32)] + (cm.scratch if comm is not None else []),
        compiler_params=pltpu.CompilerParams(dimension_semantics=("arbitrary", "arbitrary"), vmem_limit_bytes=_vmem(16 * q * di * 4)),
    )(*[_hbm(v) for v in (xbc, xbc, xbc, dt2, bias2, alog2)], *cm.operands)
    return res[0], res[1], cm.split(res[2:])


def _ssd_bwd(name, xbc, dt2, bias2, alog2, dy, hp2, dskip_e, di, tc, comm=None):
    tt, cd = xbc.shape
    h = dt2.shape[-1]
    q, n = SSD_CHUNK, SSD_STATE
    gn = (cd - di) // 2
    g = gn // n
    hpg, p = h // g, di // h
    gp = hpg * p
    nc, ncc = tt // q, tc // q

    def body(x_ref, b_ref, c_ref, dt_ref, bias_ref, alog_ref, dy_ref, hp_ref, dsk_ref,
             dxbc_ref, ddt_ref, dalog_ref, dbias_ref, dht, dxd, off):
        d, i = pl.program_id(0), pl.program_id(1)

        @pl.when(i == 0)
        def _():
            dht[...] = jnp.zeros_like(dht)
            dalog_ref[...] = jnp.zeros_like(dalog_ref)
            dbias_ref[...] = jnp.zeros_like(dbias_ref)

        s = _ssd_chunk_setup(d, dt_ref[...], bias_ref[...], alog_ref[...], q, h, di)
        x, dyc = x_ref[...].astype(F32), dy_ref[...]
        xd = x * s["dt_e"]
        dyea = dyc * s["ea_e"]
        xdte = xd * s["dte_e"]
        lane = lax.broadcasted_iota(jnp.int32, (q, gp), 1) // p
        lane_h = lax.broadcasted_iota(jnp.int32, (q, h), 1)
        da_d = jnp.zeros((q, h), F32)
        last_e = []
        for gi in range(g):
            bg, cg = b_ref[:, gi * n:(gi + 1) * n].astype(MXU), c_ref[:, gi * n:(gi + 1) * n].astype(MXU)
            sl = slice(gi * gp, (gi + 1) * gp)
            sc, sct = _dot(cg, bg, NT), _dot(bg, cg, NT)
            dyg, xdg = dyc[:, sl], xd[:, sl]
            htg, dhtg = hp_ref[:, sl].astype(F32), dht[:, sl]
            dystack = jnp.concatenate([jnp.where(lane == j, dyg, 0.0) for j in range(hpg)], axis=0)
            xdstack = jnp.concatenate([jnp.where(lane == j, xdg, 0.0) for j in range(hpg)], axis=0)
            gs = _dot(dystack, xdg, NT)
            gst = _dot(xdstack, dyg, NT)
            ds = jnp.zeros((q, q), F32)
            mts = []
            for j in range(hpg):
                hd = gi * hpg + j
                col, rw = s["a"][:, hd:hd + 1], s["a_t"][hd:hd + 1, :]
                gl = gs[j * q:(j + 1) * q] * jnp.exp(jnp.where(s["mask"], col - rw, -jnp.inf))
                ds = ds + gl
                mt = sct * jnp.exp(jnp.where(s["mask_t"], rw - col, -jnp.inf))
                mts.append(mt)
                da_j = jnp.sum(gl * sc, axis=1, keepdims=True) - jnp.sum(gst[j * q:(j + 1) * q] * mt, axis=1, keepdims=True)
                da_d = da_d + jnp.where(lane_h == hd, da_j, 0.0)
            dxd_diag = _pick_heads(_dot(jnp.concatenate(mts, axis=0), dyg, NN), q, hpg, p)
            z = _dot(bg, dhtg, NN) * s["dte_e"][:, sl]
            yoff = _dot(cg, htg, NN) * s["ea_e"][:, sl]
            off[:, sl] = dyg * yoff - xdg * z
            dxd[:, sl] = dxd_diag + z
            dxbc_ref[:, di + gi * n:di + (gi + 1) * n] = (_dot(ds, cg, TN) + _dot(xdte[:, sl], dhtg, NT)).astype(dxbc_ref.dtype)
            dxbc_ref[:, di + gn + gi * n:di + gn + (gi + 1) * n] = (_dot(ds, bg, NN)
                                                                    + _dot(dyea[:, sl], htg, NT)).astype(dxbc_ref.dtype)
            last_e.append(s["cd_e"][:, sl] * _colsum(dhtg * htg) + _colsum(xdg * z))
            dht[:, sl] = s["cd_e"][:, sl] * dhtg + _dot(cg, dyea[:, sl], TN)
        dxd_all = dxd[...]
        last = jnp.concatenate(last_e, axis=1)
        da = da_d + _dot_rx(off[...], s["e"], NT)
        last_h = _dot_rx(jnp.broadcast_to(last, (8, di)), s["e"], NT)[0:1]
        ddelta = _dot_lx(s["mask_t"].astype(MXU), da, NN, parts=3) + last_h
        ddt = ddelta * s["a_neg"] + _dot_rx(dxd_all * x, s["e"], NT)
        ddt_raw = ddt * _sigmoid(dt_ref[...] + bias_ref[...])
        ddt_ref[...] = ddt_raw
        dalog_ref[...] += _colsum(ddelta * s["dt"]) * s["a_neg"]
        dbias_ref[...] += _colsum(ddt_raw)
        dxbc_ref[:, 0:di] = (dxd_all * s["dt_e"] + jnp.where(d == 0, dyc * dsk_ref[...], 0.0)).astype(dxbc_ref.dtype)

    def cidx(d, i):
        return _chunk_order(d, nc - 1 - i, ncc, nc)

    cm = comm if comm is not None else _Comm()
    res = pl.pallas_call(
        _carry(body, comm, 9, 4, (2, nc)), name=name, grid=(2, nc),
        out_shape=[pltpu.HBM((2, tt, cd), ACT), pltpu.HBM((2, tt, h), F32),
                   pltpu.HBM((2, 1, h), F32), pltpu.HBM((2, 1, h), F32)] + cm.out_shape,
        in_specs=[pl.BlockSpec((q, di), lambda d, i: (cidx(d, i), 0)),
                  pl.BlockSpec((q, gn), lambda d, i: (cidx(d, i), di // gn)),
                  pl.BlockSpec((q, gn), lambda d, i: (cidx(d, i), di // gn + 1)),
                  pl.BlockSpec((None, q, h), lambda d, i: (d, cidx(d, i), 0)),
                  pl.BlockSpec((None, 1, h), lambda d, i: (d, 0, 0)),
                  pl.BlockSpec((None, 1, h), lambda d, i: (d, 0, 0)),
                  pl.BlockSpec((q, di), lambda d, i: (cidx(d, i), 0)),
                  pl.BlockSpec((None, None, n, di), lambda d, i: (d, cidx(d, i), 0, 0)),
                  pl.BlockSpec((1, di), lambda d, i: (0, 0))] + cm.specs,
        out_specs=[pl.BlockSpec((None, q, cd), lambda d, i: (d, cidx(d, i), 0)),
                   pl.BlockSpec((None, q, h), lambda d, i: (d, cidx(d, i), 0)),
                   pl.BlockSpec((None, 1, h), lambda d, i: (d, 0, 0)),
                   pl.BlockSpec((None, 1, h), lambda d, i: (d, 0, 0))] + cm.specs,
        scratch_shapes=[pltpu.VMEM((n, di), F32), pltpu.VMEM((q, di), F32), pltpu.VMEM((q, di), F32)]
        + (cm.scratch if comm is not None else []),
        compiler_params=pltpu.CompilerParams(dimension_semantics=("arbitrary", "arbitrary"), vmem_limit_bytes=_vmem(24 * q * di * 4)),
    )(*[_hbm(v) for v in (xbc, xbc, xbc, dt2, bias2, alog2, dy, hp2, dskip_e)], *cm.operands)
    return res[0], res[1], res[2], res[3], cm.split(res[4:])


def _ssd_gate_fwd(name, y2, xbc, zx, dskip_e, norm_w, di, nct, t):
    def fn(rows, bcs, i):
        yf, yb, xs, z = rows
        zg = (yf + yb + bcs[0] * xs) * _silu(z)
        rn = lax.rsqrt(jnp.mean(zg * zg, axis=-1, keepdims=True) + EPS)
        return [zg * rn * bcs[1]], []

    ins = [_ri(y2, lead=0, ro=nct), _ri(y2, lead=1, ro=nct), _ri(xbc, di, 0, ro=nct), _ri(zx, di, 0, ro=nct)]
    return _rowwise(name, fn, t, ins, [dskip_e, norm_w], [(di, MXU)])[0][0]


def _ssd_gate_bwd(name, dyn, y2, xbc, zx, dskip_e, norm_w, di, nct, tt):
    def fn(rows, bcs, i):
        dn, yf, yb, xs, z = rows
        lat = i >= nct
        ytot = yf + yb + bcs[0] * xs
        sz = _silu(z)
        zg = ytot * sz
        rn = lax.rsqrt(jnp.mean(zg * zg, axis=-1, keepdims=True) + EPS)
        u = dn * bcs[1]
        dzg = rn * u - zg * (rn * rn * rn) * jnp.mean(u * zg, axis=-1, keepdims=True)
        dy = jnp.where(lat, dzg * sz, 0.0)
        dz = jnp.where(lat, dzg * ytot * _dsilu(z), 0.0)
        return [dy, dz], [jnp.where(lat, _colsum(dn * zg * rn), 0.0), jnp.where(lat, _colsum(dy * xs), 0.0)]

    ins = [_ri(dyn, ro=-nct), _ri(y2, lead=0), _ri(y2, lead=1), _ri(xbc, di, 0), _ri(zx, di, 0)]
    (dy, dzx), (dnw, ddsk) = _rowwise(name, fn, tt, ins, [dskip_e, norm_w], [(di, F32), (di, MXU, zx.shape[1], 0)], [(1, di)] * 2)
    return dy, dzx, dnw, ddsk


def _ada_fwd(name, cs, w, b):
    nl, d, c = w.shape
    r = cs.shape[0]

    def body(cs_ref, w_ref, b_ref, o_ref):
        o_ref[...] = _dot(_silu(cs_ref[...]), w_ref[...], NN) + b_ref[...]

    return pl.pallas_call(
        body, name=name, grid=(nl,), out_shape=pltpu.HBM((nl, r, c), F32),
        in_specs=[pl.BlockSpec((r, d), lambda l: (0, 0)), pl.BlockSpec((None, d, c), lambda l: (l, 0, 0)),
                  pl.BlockSpec((None, 1, c), lambda l: (l, 0, 0))],
        out_specs=pl.BlockSpec((None, r, c), lambda l: (l, 0, 0)),
        compiler_params=pltpu.CompilerParams(dimension_semantics=("parallel",), vmem_limit_bytes=_vmem(2 * d * c * 4)),
    )(_hbm(cs), _hbm(w), _hbm(b))


def _ada_bwd(name, cs, w, dmod):
    nl, d, c = w.shape
    r = cs.shape[0]

    def body(cs_ref, w_ref, dm_ref, dw_ref, dsc_ref):
        dm = dm_ref[...]
        dw_ref[...] = _dot(_silu(cs_ref[...]), dm, TN)

        @pl.when(pl.program_id(0) == 0)
        def _():
            dctx = jnp.broadcast_to(_colsum(dm[r // 2:]), (8, c))
            dsc_ref[...] = _dot(dctx, w_ref[...], NT)[0:1]

    return pl.pallas_call(
        body, name=name, grid=(nl,), out_shape=[pltpu.HBM((nl, d, c), F32), pltpu.HBM((1, d), F32)],
        in_specs=[pl.BlockSpec((r, d), lambda l: (0, 0)), pl.BlockSpec((None, d, c), lambda l: (l, 0, 0)),
                  pl.BlockSpec((None, r, c), lambda l: (l, 0, 0))],
        out_specs=[pl.BlockSpec((None, d, c), lambda l: (l, 0, 0)), pl.BlockSpec((1, d), lambda l: (0, 0))],
        compiler_params=pltpu.CompilerParams(dimension_semantics=("arbitrary",), vmem_limit_bytes=_vmem(4 * d * c * 4)),
    )(_hbm(cs), _hbm(w), _hbm(dmod))


def _adam_math(w, g, m, v):
    m = ADAM_B1 * m + (1.0 - ADAM_B1) * g
    v = ADAM_B2 * v + (1.0 - ADAM_B2) * (g * g)
    m_hat = m / (1.0 - ADAM_B1 ** ADAM_STEP)
    v_hat = v / (1.0 - ADAM_B2 ** ADAM_STEP)
    delta = -ADAM_LR * (m_hat / (jnp.sqrt(v_hat) + ADAM_EPS) + ADAM_WD * w)
    return delta, m, v


def _adam(name, slots, w, m, v, comm=None):
    segs = list(slots) if isinstance(slots, (list, tuple)) else [slots]
    nseg = len(segs)
    ns, c = segs[0].shape[0], segs[0].shape[2]
    r = sum(s.shape[1] for s in segs)
    tr = _pick(min(s.shape[1] for s in segs), (256, 128, 64, 32, 16, 8))
    starts = [sum(s.shape[1] for s in segs[:k]) // tr for k in range(nseg)]

    def body(*refs):
        s_refs, (w_ref, m_ref, v_ref), (g_ref, d_ref, mo_ref, vo_ref) = refs[:nseg], refs[nseg:nseg + 3], refs[nseg + 3:]
        i = pl.program_id(0)

        def total(s_ref):
            g = s_ref[0].astype(F32)
            for k in range(1, ns):
                g = g + s_ref[k].astype(F32)
            return g

        g = total(s_refs[0])
        for k in range(1, nseg):
            g = jnp.where(i >= starts[k], total(s_refs[k]), g)
        d, mn, vn = _adam_math(w_ref[...], g, m_ref[...], v_ref[...])
        g_ref[...], d_ref[...], mo_ref[...], vo_ref[...] = g, d, mn, vn

    blk = pl.BlockSpec((tr, c), lambda i: (i, 0))
    seg_specs = [pl.BlockSpec((ns, tr, c), lambda i, st=starts[k], nt=segs[k].shape[1] // tr: (0, jnp.clip(i - st, 0, nt - 1), 0))
                 for k in range(nseg)]
    cm = comm if comm is not None else _Comm()
    res = pl.pallas_call(
        _carry(body, comm, nseg + 3, 4, (r // tr,)), name=name, grid=(r // tr,),
        out_shape=[pltpu.HBM((r, c), F32)] * 4 + cm.out_shape,
        in_specs=seg_specs + [blk, blk, blk] + cm.specs, out_specs=[blk] * 4 + cm.specs,
        scratch_shapes=cm.scratch if comm is not None else [],
        compiler_params=pltpu.CompilerParams(dimension_semantics=("arbitrary",), vmem_limit_bytes=_vmem(16 * nseg * tr * c * 4)),
    )(*[_hbm(s) for s in segs], _hbm(w), _hbm(m), _hbm(v), *cm.operands)
    return res[:4] if comm is None else (res[:4], cm.split(res[4:]))


def _adam_small(name, slots, ws, ms, vs, scale=None):
    k = len(slots)

    def body(*refs):
        s_refs, w_refs, m_refs, v_refs = refs[:k], refs[k:2 * k], refs[2 * k:3 * k], refs[3 * k:4 * k]
        sc_ref = refs[4 * k] if scale is not None else None
        outs = refs[4 * k + (scale is not None):]
        for a in range(k):
            g = s_refs[a][0]
            for j in range(1, NDEV):
                g = g + s_refs[a][j]
            if scale is not None and a == scale[0]:
                g = g * _dsilu(sc_ref[...])
            d, mn, vn = _adam_math(w_refs[a][...], g, m_refs[a][...], v_refs[a][...])
            outs[a][...], outs[k + a][...], outs[2 * k + a][...], outs[3 * k + a][...] = g, d, mn, vn

    shapes = [pltpu.HBM(w.shape, F32) for w in ws]
    extra = [scale[1]] if scale is not None else []
    ins = [*slots, *ws, *ms, *vs, *extra]

    def whole(shape):
        return pl.BlockSpec(shape, lambda i, nd=len(shape): (0,) * nd)

    res = pl.pallas_call(body, name=name, grid=(1,), out_shape=shapes * 4, in_specs=[whole(v.shape) for v in ins],
                         out_specs=[whole(s.shape) for s in shapes * 4])(*[_hbm(v) for v in ins])
    return res[:k], res[k:2 * k], res[2 * k:3 * k], res[3 * k:]


def _unshard_cols(g):
    g = jnp.moveaxis(g, 0, -2)
    return g.reshape(g.shape[:-2] + (g.shape[-2] * g.shape[-1],))


def _shard_cols(a):
    a = a.reshape(a.shape[:-1] + (NDEV, a.shape[-1] // NDEV))
    return jnp.moveaxis(a, -2, 0)


def _unshard_rows(g):
    g = jnp.moveaxis(g, 0, -3)
    return g.reshape(g.shape[:-3] + (g.shape[-3] * g.shape[-2], g.shape[-1]))


def _shard_rows(a):
    a = a.reshape(a.shape[:-2] + (NDEV, a.shape[-2] // NDEV, a.shape[-1]))
    return jnp.moveaxis(a, -3, 0)


def _flat2(a):
    return a.reshape((-1, a.shape[-1]))


def kernel(x, c, ctx, c_ctx, ada_w, ada_b, norm_mix_g, norm_ffn_g, final_norm_g, ssd_w_in, ssd_conv_w, ssd_conv_b, ssd_dt_bias_f, ssd_dt_bias_b, ssd_a_log_f, ssd_a_log_b, ssd_d_skip, ssd_norm_w, ssd_w_out, conf_w_pw1, conf_b_pw1, conf_dw_w, conf_dw_b, conf_ln_g, conf_ln_b, conf_w_pw2, conf_b_pw2, ffn_w_in, ffn_w_out, loss_target, m_c_ctx, m_ada_w, m_ada_b, m_norm_mix_g, m_norm_ffn_g, m_final_norm_g, m_ssd_w_in, m_ssd_conv_w, m_ssd_conv_b, m_ssd_dt_bias_f, m_ssd_dt_bias_b, m_ssd_a_log_f, m_ssd_a_log_b, m_ssd_d_skip, m_ssd_norm_w, m_ssd_w_out, m_conf_w_pw1, m_conf_b_pw1, m_conf_dw_w, m_conf_dw_b, m_conf_ln_g, m_conf_ln_b, m_conf_w_pw2, m_conf_b_pw2, m_ffn_w_in, m_ffn_w_out, v_c_ctx, v_ada_w, v_ada_b, v_norm_mix_g, v_norm_ffn_g, v_final_norm_g, v_ssd_w_in, v_ssd_conv_w, v_ssd_conv_b, v_ssd_dt_bias_f, v_ssd_dt_bias_b, v_ssd_a_log_f, v_ssd_a_log_b, v_ssd_d_skip, v_ssd_norm_w, v_ssd_w_out, v_conf_w_pw1, v_conf_b_pw1, v_conf_dw_w, v_conf_dw_b, v_conf_ln_g, v_conf_ln_b, v_conf_w_pw2, v_conf_b_pw2, v_ffn_w_in, v_ffn_w_out):
    args = dict(locals())
    names = ['c_ctx', 'ada_w', 'ada_b', 'norm_mix_g', 'norm_ffn_g', 'final_norm_g', 'ssd_w_in', 'ssd_conv_w', 'ssd_conv_b',
             'ssd_dt_bias_f', 'ssd_dt_bias_b', 'ssd_a_log_f', 'ssd_a_log_b', 'ssd_d_skip', 'ssd_norm_w', 'ssd_w_out',
             'conf_w_pw1', 'conf_b_pw1', 'conf_dw_w', 'conf_dw_b', 'conf_ln_g', 'conf_ln_b', 'conf_w_pw2', 'conf_b_pw2',
             'ffn_w_in', 'ffn_w_out']
    me = 4 * lax.axis_index("x") + 2 * lax.axis_index("y") + lax.axis_index("c")
    t, d = x.shape[1], x.shape[2]
    tc = ctx.shape[1]
    tt = tc + t
    nct = tc // ROW_TILE
    assert tc % ROW_TILE == 0 and t % ROW_TILE == 0
    h = ssd_dt_bias_f.shape[-1]
    di = ssd_norm_w.shape[-1]
    cdim = ssd_conv_b.shape[-1]
    kc = ssd_conv_w.shape[1]
    ck = conf_dw_w.shape[1]
    ch = d // 2
    rows_g = t // GRID_W
    nl = ada_w.shape[0]
    cw = ada_w.shape[2]
    x2, ctx2, tgt = x[0], ctx[0], loss_target[0]

    (c_all, convw_g), _ = _exchange("gather_first", [c, ssd_conv_w[0]])
    ride_norm = _Comm(gather=[ssd_w_in[0].astype(WIRE)])
    ride_proj = _Comm(gather=[ssd_w_out[0].astype(WIRE), conf_w_pw2[0].astype(WIRE)])
    ride_conv = _Comm(gather=[conf_w_pw1[0].astype(WIRE), conf_b_pw1, conf_dw_w[0], conf_dw_b, conf_ln_g, conf_ln_b, conf_b_pw2])
    ride_scan = _Comm(gather=[ffn_w_in[0].astype(WIRE), ffn_w_in[1].astype(WIRE), ffn_w_out[0].astype(WIRE), ffn_w_out[1].astype(WIRE)])
    conv_w_full = _unshard_cols(convw_g)

    cs_all = jnp.concatenate([c_all[:, 0, :], jnp.broadcast_to(c_ctx[None, :], (NDEV, d))], axis=0)
    ada_b_mine = lax.dynamic_slice_in_dim(ada_b, me * cw, cw, axis=1)[:, None, :]
    mod_part = _ada_fwd("ada_fwd", cs_all, ada_w, ada_b_mine)
    (mod_g,), _ = _exchange("gather_mod", [mod_part])
    mod_all = jnp.moveaxis(mod_g, 0, 2).reshape(nl, 2 * NDEV, NDEV * cw)
    mod_lat = lax.dynamic_slice_in_dim(mod_all, me, 1, axis=1)[:, 0, :]
    mod_ctx = mod_all[0, NDEV, :]

    def six(v):
        return [v[k * d:(k + 1) * d][None, :] for k in range(6)]

    sh1, s1, g1, sh2, s2, g2 = six(mod_lat[0])
    csh1, cs1 = six(mod_ctx)[:2]
    sh1b, s1b, g1b, sh2b, s2b, g2b = six(mod_lat[1])
    nmg, nfg = norm_mix_g, norm_ffn_g

    h_all = jnp.concatenate([ctx2, x2], axis=0)
    s01, sh01 = jnp.concatenate([cs1, s1], axis=0), jnp.concatenate([csh1, sh1], axis=0)
    xn_all, ((w_in_g,), _) = _normmod_fwd("l0_norm", h_all, nmg[0:1], s01, sh01, nct, comm=ride_norm)
    w_ssd_in = _unshard_cols(w_in_g)
    w_zx = w_ssd_in[:, :di + cdim]
    w_dt = jnp.pad(w_ssd_in[:, di + cdim:], ((0, 0), (0, LANES - 2 * h)))
    zx, ((w_out_g, pw2_g), _) = _mm("ssd_in_proj", xn_all, w_zx, "nn", ACT, comm=ride_proj)
    dtr = _mm("ssd_dt_proj", xn_all, w_dt, "nn")
    dt2 = jnp.moveaxis(dtr[:, :2 * h].reshape(tt, 2, h), 1, 0)
    bias2 = jnp.stack([ssd_dt_bias_f, ssd_dt_bias_b])
    alog2 = jnp.stack([ssd_a_log_f, ssd_a_log_b])
    xbc, xbc_pre, ((pw1_g, bpw1_g, dww_g, dwb_g, lng_g, lnb_g, bpw2_g), _) = _ssd_conv_fwd("ssd_conv", zx, conv_w_full, ssd_conv_b, di, tc,
                                                                                 comm=ride_conv)
    y2, hp2, (ffn_g, _) = _ssd_fwd("ssd_scan", xbc, dt2, bias2, alog2, di, tc, comm=ride_scan)
    w_ssd_out = _unshard_rows(w_out_g)
    w_pw1, w_pw2 = _unshard_cols(pw1_g), _unshard_rows(pw2_g)
    w_fin = [_unshard_cols(ffn_g[0]), _unshard_cols(ffn_g[1])]
    w_fout = [_unshard_rows(ffn_g[2]), _unshard_rows(ffn_g[3])]
    dw_w_full = _unshard_cols(dww_g)
    b_pw1, dw_b, ln_g, ln_b, b_pw2 = (_unshard_cols(a) for a in (bpw1_g, dwb_g, lng_g, lnb_g, bpw2_g))
    dskip_e = jnp.repeat(ssd_d_skip, di // h, axis=1)
    yn = _ssd_gate_fwd("ssd_gate", y2, xbc, zx, dskip_e, ssd_norm_w, di, nct, t)
    mix0 = _mm("ssd_out_proj", yn, w_ssd_out, "nn")
    h1, xf0 = _resnorm_fwd("l0_res_norm", x2, mix0, g1, nfg[0:1], s2, sh2)
    u0 = _mm("ffn0_in", xf0, w_fin[0], "nn", ACT)
    hid0 = _swiglu_fwd("ffn0_act", u0)
    f0 = _mm("ffn0_out", hid0, w_fout[0], "nn")
    h2, xn1 = _resnorm_fwd("l1_norm", h1, f0, g2, nmg[1:2], s1b, sh1b)
    u1 = _mm("conf_pw1", xn1, w_pw1, "nn", ACT, bias=b_pw1)
    gl = _glu_fwd("conf_glu", u1)
    gl_h = _grid_t(gl[:, :ch], rows_g, GRID_W)
    v_ht = _strided_conv("conf_conv_h", gl_h, dw_w_full[:, :ch], dw_b[:, :ch], rows_g)
    v_v = _strided_conv("conf_conv_v", gl, dw_w_full[:, ch:], dw_b[:, ch:], GRID_W, x_col0=ch)
    v_h = _grid_t(v_ht, GRID_W, rows_g)
    sl = _ln_silu_fwd("conf_ln", v_h, v_v, ln_g, ln_b)
    mix1 = _mm("conf_pw2", sl, w_pw2, "nn", bias=b_pw2)
    h3, xf1 = _resnorm_fwd("l1_res_norm", h2, mix1, g1b, nfg[1:2], s2b, sh2b)
    u2 = _mm("ffn1_in", xf1, w_fin[1], "nn", ACT)
    hid1 = _swiglu_fwd("ffn1_act", u2)
    f1 = _mm("ffn1_out", hid1, w_fout[1], "nn")
    dh, sq, d_final_g = _final_loss("final_loss", h3, f1, tgt, g2b, final_norm_g[None, :])

    zero2 = jnp.zeros((2, d), F32)

    def ffn_bwd(tag, dh, hin, xf, u, hid, f, gate, w_in, w_out, g_norm, s_mod):
        df, dgate, _ = _gate_bwd(tag + "_gate_bwd", dh, f, gate)
        dhid = _mm(tag + "_dhid", df, w_out, "nt", ACT)
        dw_out = _mm(tag + "_dwout", hid, df, "tn", WIRE)
        du = _swiglu_bwd(tag + "_act_bwd", u, dhid)
        dw_in = _mm(tag + "_dwin", xf, du, "tn", WIRE)
        dxf = _mm(tag + "_dx", du, w_in, "nt")
        s_2 = jnp.concatenate([s_mod, s_mod], axis=0)
        dh, dsh, ds, dg = _normmod_bwd(tag + "_norm_bwd", hin, dxf, dh, g_norm, s_2)
        return dh, dgate, dsh[1:2], ds[1:2], dg[1:2], dw_in, dw_out

    dh, d_g2b, d_sh2b, d_s2b, d_nfg1, g_fin1, g_fout1 = ffn_bwd("ffn1", dh, h3, xf1, u2, hid1, f1, g2b, w_fin[1], w_fout[1], nfg[1:2], s2b)
    dmix1, d_g1b, g_bpw2 = _gate_bwd("conf_gate_bwd", dh, mix1, g1b)
    dsl = _mm("conf_dsl", dmix1, w_pw2, "nt")
    g_pw2 = _mm("conf_dwpw2", sl, dmix1, "tn", WIRE)
    dv_lo, dv_v, g_lng, g_lnb = _ln_silu_bwd("conf_ln_bwd", v_h, v_v, dsl, ln_g, ln_b)
    dv_h = _grid_t(dv_lo, rows_g, GRID_W)
    w_flip = dw_w_full[::-1]
    dgl_h = _strided_conv("conf_conv_h_bwd", dv_h, w_flip[:, :ch], None, rows_g)
    dgl_v = _strided_conv("conf_conv_v_bwd", dv_v, w_flip[:, ch:], None, GRID_W)
    g_dww_h, g_dwb_h = _strided_conv_dw("conf_conv_h_dw", gl_h, dv_h, ck, rows_g)
    g_dww_v, g_dwb_v = _strided_conv_dw("conf_conv_v_dw", gl, dv_v, ck, GRID_W, x_col0=ch)
    g_dww, g_dwb = jnp.concatenate([g_dww_h, g_dww_v], axis=1), jnp.concatenate([g_dwb_h, g_dwb_v], axis=1)
    du1, g_bpw1 = _glu_bwd("conf_glu_bwd", u1, _grid_t(dgl_h, GRID_W, rows_g), dgl_v)
    g_pw1 = _mm("conf_dwpw1", xn1, du1, "tn", WIRE)
    dxn1 = _mm("conf_dx", du1, w_pw1, "nt")
    dh, dsh_, ds_, dg_ = _normmod_bwd("l1_norm_bwd", h2, dxn1, dh, nmg[1:2], jnp.concatenate([s1b, s1b], axis=0))
    d_sh1b, d_s1b, d_nmg1 = dsh_[1:2], ds_[1:2], dg_[1:2]
    dh, d_g2, d_sh2, d_s2, d_nfg0, g_fin0, g_fout0 = ffn_bwd("ffn0", dh, h1, xf0, u0, hid0, f0, g2, w_fin[0], w_fout[0], nfg[0:1], s2)
    dmix0, d_g1, _ = _gate_bwd("ssd_gate_res_bwd", dh, mix0, g1)
    dyn = _mm("ssd_dyn", dmix0, w_ssd_out, "nt")
    g_ssd_out = _mm("ssd_dwout", yn, dmix0, "tn", WIRE)
    dy, dzx, g_normw, ddsk_e = _ssd_gate_bwd("ssd_gate_bwd", dyn, y2, xbc, zx, dskip_e, ssd_norm_w, di, nct, tt)
    ride_scan_bwd = _Comm(scatter=[_shard_cols(g_fin0), _shard_cols(g_fin1), _shard_rows(g_fout0), _shard_rows(g_fout1)])
    ride_conv_bwd = _Comm(scatter=[_shard_rows(g_ssd_out), _shard_cols(g_pw1), _shard_rows(g_pw2), _shard_cols(g_bpw1),
                                   _shard_cols(g_dww), _shard_cols(g_dwb), _shard_cols(g_lng), _shard_cols(g_lnb), _shard_cols(g_bpw2)])
    dxbc2, ddt2, g_alog2, g_bias2, (_, ffn_r) = _ssd_bwd("ssd_scan_bwd", xbc, dt2, bias2, alog2, dy, hp2, dskip_e, di, tc,
                                                         comm=ride_scan_bwd)
    dzx, g_convw, g_convb, (_, conv_r) = _ssd_conv_bwd("ssd_conv_bwd", zx, dxbc2, conv_w_full, xbc_pre, dzx, di, tc,
                                                       comm=ride_conv_bwd)
    ddt_p = jnp.pad(jnp.moveaxis(ddt2, 0, 1).reshape(tt, 2 * h), ((0, 0), (0, LANES - 2 * h))).astype(MXU)
    g_ssd_in = jnp.concatenate([_mm("ssd_dw_zx", xn_all, dzx, "tn", WIRE),
                                _mm("ssd_dw_dt", xn_all, ddt_p, "tn", WIRE)[:, :2 * h]], axis=1)
    dxn, (_, (ssd_in_r, convw_r)) = _mm("ssd_dx_zx", dzx, w_zx, "nt",
                                        comm=_Comm(scatter=[_shard_cols(g_ssd_in), _shard_cols(g_convw)]))
    dxn = _mm("ssd_dx_dt", ddt_p, w_dt, "nt", add=dxn)
    dh_all, dsh_, ds_, dg_ = _normmod_bwd("l0_norm_bwd", h_all, dxn, dh, nmg[0:1], s01, nct)
    grad_x = dh_all[tc:][None]
    d_csh1, d_sh1, d_cs1, d_s1 = dsh_[0:1], dsh_[1:2], ds_[0:1], ds_[1:2]
    d_nmg0 = dg_[0:1] + dg_[1:2]

    z1 = jnp.zeros((1, d), F32)
    dmod = jnp.concatenate([jnp.concatenate([d_sh1, d_s1, d_g1, d_sh2, d_s2, d_g2], axis=1),
                            jnp.concatenate([d_sh1b, d_s1b, d_g1b, d_sh2b, d_s2b, d_g2b], axis=1),
                            jnp.concatenate([d_csh1, d_cs1, z1, z1, z1, z1], axis=1)], axis=0)
    out = {}

    def put(name, res):
        w = args[name]
        out["grad_" + name], out["delta_" + name], out["new_m_" + name], out["new_v_" + name] = (r.reshape(w.shape) for r in res)

    def adam_big(name, slots, comm=None):
        return _adam("adam_" + name, slots, _flat2(args[name]), _flat2(args["m_" + name]), _flat2(args["v_" + name]), comm=comm)

    res, ((dmod_g,), _) = adam_big("ffn_w_in", [ffn_r[0], ffn_r[1]], comm=_Comm(gather=[dmod]))
    put("ffn_w_in", res)
    dmod_mine = lax.dynamic_slice_in_dim(dmod_g, me * cw, cw, axis=2)
    dmod16 = jnp.stack([jnp.concatenate([dmod_mine[:, 0], dmod_mine[:, 2]], axis=0),
                        jnp.concatenate([dmod_mine[:, 1], jnp.zeros((NDEV, cw), F32)], axis=0)])
    g_ada_w, dsc_part = _ada_bwd("ada_bwd", cs_all, ada_w, dmod16)
    g_ada_b = dmod[0:2] + jnp.concatenate([dmod[2:3], jnp.zeros((1, 6 * d), F32)], axis=0)

    d_dskip = jnp.sum(ddsk_e.reshape(h, di // h), axis=1)[None, :]
    rep = [dsc_part, g_ada_b, jnp.concatenate([d_nmg0, d_nmg1], axis=0), jnp.concatenate([d_nfg0, d_nfg1], axis=0),
           d_final_g, g_convb, g_bias2[0], g_bias2[1], g_alog2[0], g_alog2[1], d_dskip, g_normw]
    res, (rep_g, _) = _adam("adam_ada_w", _flat2(g_ada_w)[None], _flat2(ada_w), _flat2(m_ada_w), _flat2(v_ada_w),
                            comm=_Comm(gather=rep + [sq]))
    put("ada_w", res)
    rep_g, sq_g = rep_g[:-1], rep_g[-1]
    loss = (0.5 / d) * jnp.sum(sq_g[:, 0, 0])
    small_r = [convw_r] + list(conv_r[3:])

    for name, slots in zip(["ssd_w_in", "ssd_w_out", "conf_w_pw1", "conf_w_pw2", "ffn_w_out"],
                           [ssd_in_r, conv_r[0], conv_r[1], conv_r[2], [ffn_r[2], ffn_r[3]]]):
        put(name, adam_big(name, slots))
    small_names = ["ssd_conv_w", "conf_b_pw1", "conf_dw_w", "conf_dw_b", "conf_ln_g", "conf_ln_b", "conf_b_pw2",
                   "c_ctx", "ada_b", "norm_mix_g", "norm_ffn_g", "final_norm_g", "ssd_conv_b", "ssd_dt_bias_f", "ssd_dt_bias_b",
                   "ssd_a_log_f", "ssd_a_log_b", "ssd_d_skip", "ssd_norm_w"]
    slots = list(small_r) + list(rep_g)

    def as2(a):
        return a.reshape((1, -1)) if a.ndim == 1 else _flat2(a)

    res = _adam_small("adam_small", slots, [as2(args[n]) for n in small_names], [as2(args["m_" + n]) for n in small_names],
                      [as2(args["v_" + n]) for n in small_names], scale=(small_names.index("c_ctx"), c_ctx[None, :]))
    for k, name in enumerate(small_names):
        put(name, [r[k] for r in res])
    return (loss, grad_x, *[out["grad_" + n] for n in names], *[out["delta_" + n] for n in names],
            *[out["new_m_" + n] for n in names], *[out["new_v_" + n] for n in names])


def _flat3(a):
    return a.reshape((a.shape[0], -1, a.shape[-1]))
```

```python
import functools

import jax
import jax.numpy as jnp
from jax import lax
from jax.experimental import pallas as pl
from jax.experimental.pallas import tpu as pltpu

F32 = jnp.float32
MXU = jnp.bfloat16
WIRE = jnp.bfloat16
ACT = jnp.bfloat16
NDEV = 8
AXES = ("x", "y", "c")
SSD_STATE = 128
SSD_CHUNK = 128
GRID_W = 64
EPS = 1e-6
ROW_TILE = 256
LANES = 128
ADAM_LR, ADAM_B1, ADAM_B2, ADAM_EPS, ADAM_WD, ADAM_STEP = 0.001, 0.9, 0.999, 1e-08, 0.01, 10
VMEM_CAP = 56 * 2 ** 20
MESH_ID = pl.DeviceIdType.MESH


def _pick(dim, cands):
    for c in cands:
        if dim % c == 0:
            return c
    return dim


def _nbytes(shape, dtype):
    n = 1
    for s in shape:
        n *= s
    return n * jnp.dtype(dtype).itemsize


def _vmem(nbytes):
    return int(min(VMEM_CAP, max(24 * 2 ** 20, 2 * nbytes + 8 * 2 ** 20)))


def _sigmoid(x):
    return 1.0 / (1.0 + jnp.exp(-x))


def _silu(x):
    return x * _sigmoid(x)


def _dsilu(x):
    s = _sigmoid(x)
    return s * (1.0 + x * (1.0 - s))


def _softplus(x):
    return jnp.maximum(x, 0.0) + jnp.log(1.0 + jnp.exp(-jnp.abs(x)))


def _dot(a, b, dims):
    return lax.dot_general(a.astype(MXU), b.astype(MXU), (dims, ((), ())), preferred_element_type=F32)


NN, NT, TN = ((1,), (0,)), ((1,), (1,)), ((0,), (0,))


def _split(a, parts):
    out = []
    for _ in range(parts):
        p = a.astype(MXU)
        out.append(p)
        a = a - p.astype(F32)
    return out


def _dot_lx(e, a, dims, parts=2):
    return sum(lax.dot_general(e, p, (dims, ((), ())), preferred_element_type=F32) for p in _split(a, parts))


def _dot_rx(a, e, dims, parts=2):
    return sum(lax.dot_general(p, e, (dims, ((), ())), preferred_element_type=F32) for p in _split(a, parts))


class _Comm:
    def __init__(self, gather=(), scatter=()):
        self.gather, self.scatter = list(gather), list(scatter)
        self.ng, self.n = len(self.gather), len(self.gather) + len(self.scatter)
        self.operands = self.gather + self.scatter
        self.specs = [pl.BlockSpec(memory_space=pl.ANY)] * self.n
        self.out_shape = ([jax.ShapeDtypeStruct((NDEV,) + a.shape, a.dtype) for a in self.gather]
                          + [jax.ShapeDtypeStruct(a.shape, a.dtype) for a in self.scatter])
        self.scratch = [pltpu.SemaphoreType.DMA((self.n, 7)), pltpu.SemaphoreType.DMA((self.n, 7)),
                        pltpu.SemaphoreType.DMA((self.n,))]

    def split(self, res):
        return res[:self.ng], res[self.ng:]

    def _copies(self, ins, outs, sems):
        send, recv, loc = sems
        ng, n = self.ng, self.n
        x, y, c = lax.axis_index("x"), lax.axis_index("y"), lax.axis_index("c")
        me, sib = (x, y, c), (x, y, 1 - c)
        chips = [(1 - x, y), (x, 1 - y), (1 - x, 1 - y)]

        def slot(p):
            return 4 * p[0] + 2 * p[1] + p[2]

        def rcopy(a, k, src, dst, to):
            return functools.partial(pltpu.make_async_remote_copy, src_ref=src, dst_ref=dst, send_sem=send.at[a, k],
                                     recv_sem=recv.at[a, k], device_id=to, device_id_type=MESH_ID)

        local = [functools.partial(pltpu.make_async_copy, ins[a] if a < ng else ins[a].at[slot(me)], outs[a].at[slot(me)],
                                   loc.at[a]) for a in range(n)]
        rel = [(fx, fy, fc) for fx in (0, 1) for fy in (0, 1) for fc in (0, 1)][1:]
        first, landed, passed = [], [], []
        for a in range(ng, n):
            for k, (fx, fy, fc) in enumerate(rel):
                p = (1 - x if fx else x, 1 - y if fy else y, 1 - c if fc else c)
                first.append(rcopy(a, k, ins[a].at[slot(p)], outs[a].at[slot(me)], p))
                blk = outs[a].at[slot(p)]
                landed.append(rcopy(a, k, blk, blk, me))
        for a in range(ng):
            dst = outs[a].at[slot(me)]
            first.append(rcopy(a, 0, ins[a], dst, sib))
            first += [rcopy(a, 1 + j, ins[a], dst, (*ch, c)) for j, ch in enumerate(chips)]
            blk = outs[a].at[slot(sib)]
            landed.append(rcopy(a, 0, blk, blk, me))
            for j, ch in enumerate(chips):
                blk = outs[a].at[slot((*ch, c))]
                passed.append((rcopy(a, 1 + j, blk, blk, me), rcopy(a, 4 + j, blk, blk, sib)))
                blk = outs[a].at[slot((*ch, 1 - c))]
                landed.append(rcopy(a, 4 + j, blk, blk, me))
        return local, first, passed, landed

    def start(self, ins, outs, sems):
        local, first, _, _ = self._copies(ins, outs, sems)
        for make in local + first:
            make().start()

    def finish(self, ins, outs, sems):
        local, first, passed, landed = self._copies(ins, outs, sems)
        onward = []
        for arrived, forward in passed:
            arrived().wait_recv()
            onward.append(forward())
            onward[-1].start()
        for make in landed:
            make().wait_recv()
        for make in first:
            make().wait_send()
        for cp in onward:
            cp.wait_send()
        for make in local:
            make().wait()


def _carry(body, comm, n_in, n_out, grid):
    if comm is None:
        return body
    n = comm.n

    def wrapped(*refs):
        own_in, c_in = refs[:n_in], refs[n_in:n_in + n]
        own_out, c_out = refs[n_in + n:n_in + n + n_out], refs[n_in + n + n_out:n_in + 2 * n + n_out]
        own_scr, sems = refs[n_in + 2 * n + n_out:-3], refs[-3:]
        ids = [pl.program_id(ax) for ax in range(len(grid))]
        first, last = ids[0] == 0, ids[0] == grid[0] - 1
        for ax in range(1, len(grid)):
            first, last = first & (ids[ax] == 0), last & (ids[ax] == grid[ax] - 1)

        @pl.when(first)
        def _():
            comm.start(c_in, c_out, sems)

        body(*own_in, *own_out, *own_scr)

        @pl.when(last)
        def _():
            comm.finish(c_in, c_out, sems)

    return wrapped


def _exchange(name, gather, scatter=()):
    comm = _Comm(gather, scatter)
    n = comm.n

    def body(*refs):
        ins, outs, sems = refs[:n], refs[n:2 * n], refs[2 * n:]
        comm.start(ins, outs, sems)
        comm.finish(ins, outs, sems)

    res = pl.pallas_call(body, name=name, out_shape=comm.out_shape, in_specs=comm.specs, out_specs=comm.specs,
                         scratch_shapes=comm.scratch)(*comm.operands)
    return comm.split(res)


def _hbm(a):
    return pltpu.with_memory_space_constraint(a, pltpu.HBM)


def _divs(dim, mult):
    return [dim] + [dim // parts for parts in range(2, dim // mult + 1) if dim % parts == 0 and (dim // parts) % mult == 0]


MM_VMEM_BUDGET = 40 * 2 ** 20
GRID_STEP_US = 0.35
HBM_BYTES_PER_US = 3.0e6


def _mm_tiles(m, n, k, sizes, mode, has_add):
    sa, sb, so = sizes
    sub = 16
    best = None
    for tk in _divs(k, LANES):
        for tn in _divs(n, LANES):
            for tm in _divs(m, LANES if mode == "tn" else sub):
                nk = k // tk
                out_t = tm * tn
                est = (2 * (tm * tk * sa + tk * tn * sb) + 2 * out_t * so + 2 * (tm * tk + tk * tn) + 4 * out_t
                       + (4 * out_t if nk > 1 else 0) + (8 * out_t if has_add else 0))
                if est > MM_VMEM_BUDGET:
                    continue
                steps = (m // tm) * (n // tn) * nk
                cost = steps * GRID_STEP_US + (tm * tk * sa + tk * tn * sb + out_t * so) / HBM_BYTES_PER_US
                if best is None or cost < best[0]:
                    best = (cost, tm, tn, tk, est)
    assert best is not None, (m, n, k)
    return best[1:]


def _mm(name, a, b, mode, out_dtype=F32, bias=None, add=None, comm=None, n_used=None):
    if mode == "nn":
        (m, k), (k2, n) = a.shape, b.shape
        n = n if n_used is None else n_used
    elif mode == "nt":
        (m, k), (n, k2) = a.shape, b.shape
        k2 = min(k, k2)
    else:
        (k, m), (k2, n) = a.shape, b.shape
    assert k == k2, (name, a.shape, b.shape)
    sizes = (a.dtype.itemsize, b.dtype.itemsize, jnp.dtype(out_dtype).itemsize)
    tm, tn, tk, est = _mm_tiles(m, n, k, sizes, mode, add is not None)
    nk = k // tk
    dims = {"nn": NN, "nt": NT, "tn": TN}[mode]
    a_spec = pl.BlockSpec((tk, tm), lambda i, j, kk: (kk, i)) if mode == "tn" else pl.BlockSpec((tm, tk), lambda i, j, kk: (i, kk))
    b_spec = pl.BlockSpec((tn, tk), lambda i, j, kk: (j, kk)) if mode == "nt" else pl.BlockSpec((tk, tn), lambda i, j, kk: (kk, j))
    extra, extra_specs = [], []
    if bias is not None:
        extra.append(bias)
        extra_specs.append(pl.BlockSpec((1, tn), lambda i, j, kk: (0, j)))
    if add is not None:
        extra.append(add)
        extra_specs.append(pl.BlockSpec((tm, tn), lambda i, j, kk: (i, j)))

    def finish(r, extras, o_ref):
        for e in extras:
            r = r + e[...].astype(F32)
        o_ref[...] = r.astype(o_ref.dtype)

    def body_acc(*refs):
        a_ref, b_ref = refs[:2]
        o_ref, acc = refs[-2:]
        kk = pl.program_id(2)

        @pl.when(kk == 0)
        def _():
            acc[...] = jnp.zeros_like(acc)

        acc[...] += _dot(a_ref[...], b_ref[...], dims)

        @pl.when(kk == nk - 1)
        def _():
            finish(acc[...], refs[2:-2], o_ref)

    def body_one(*refs):
        finish(_dot(refs[0][...], refs[1][...], dims), refs[2:-1], refs[-1])

    cm = comm if comm is not None else _Comm()
    grid = (m // tm, n // tn, nk)
    res = pl.pallas_call(
        _carry(body_acc if nk > 1 else body_one, comm, 2 + len(extra), 1, grid), name=name, grid=grid,
        out_shape=[pltpu.HBM((m, n), out_dtype)] + cm.out_shape,
        in_specs=[a_spec, b_spec] + extra_specs + cm.specs,
        out_specs=[pl.BlockSpec((tm, tn), lambda i, j, kk: (i, j))] + cm.specs,
        scratch_shapes=([pltpu.VMEM((tm, tn), F32)] if nk > 1 else []) + (cm.scratch if comm is not None else []),
        compiler_params=pltpu.CompilerParams(
            dimension_semantics=("parallel", "parallel", "arbitrary") if comm is None else ("arbitrary",) * 3,
            vmem_limit_bytes=int(min(VMEM_CAP, est + 12 * 2 ** 20))),
    )(*[_hbm(v) for v in (a, b, *extra)], *cm.operands)
    return res[0] if comm is None else (res[0], cm.split(res[1:]))


def _ri(arr, w=None, cb=0, ro=0, lead=None):
    return (arr, arr.shape[-1] if w is None else w, cb, ro, lead)


def _rowwise(name, fn, nrows, row_ins, bc_ins, outs, accs=(), comm=None):
    tr = min(ROW_TILE, nrows)
    assert nrows % tr == 0
    in_specs = []
    for (arr, w, cb, ro, lead) in row_ins:
        last = arr.shape[-2] // tr - 1
        if lead is None:
            in_specs.append(pl.BlockSpec((tr, w), lambda i, cb=cb, ro=ro, last=last: (jnp.clip(i + ro, 0, last), cb)))
        else:
            in_specs.append(pl.BlockSpec((None, tr, w),
                                         lambda i, cb=cb, ro=ro, lead=lead, last=last: (lead, jnp.clip(i + ro, 0, last), cb)))
    for arr in bc_ins:
        in_specs.append(pl.BlockSpec(arr.shape, lambda i, nd=arr.ndim: (0,) * nd))
    outs = [tuple(o) + (o[0], 0, 0)[len(o) - 2:] for o in outs]
    out_shape = [pltpu.HBM((nrows + ro * tr, total), dt) for _, dt, total, _, ro in outs] + [pltpu.HBM(s, F32) for s in accs]
    out_specs = ([pl.BlockSpec((tr, c), lambda i, cb=cb, ro=ro: (jnp.maximum(i + ro, 0), cb)) for c, _, _, cb, ro in outs]
                 + [pl.BlockSpec(s, lambda i: (0, 0)) for s in accs])
    nr, nb, no = len(row_ins), len(bc_ins), len(outs)

    def body(*refs):
        i = pl.program_id(0)
        rows = [r[...].astype(F32) for r in refs[:nr]]
        bcs = [r[...] for r in refs[nr:nr + nb]]
        o, a = fn(rows, bcs, i)
        for ref, val in zip(refs[nr + nb:nr + nb + no], o):
            ref[...] = val.astype(ref.dtype)
        for ref, val in zip(refs[nr + nb + no:], a):
            @pl.when(i == 0)
            def _(ref=ref, val=val):
                ref[...] = val

            @pl.when(i > 0)
            def _(ref=ref, val=val):
                ref[...] += val

    est = sum(tr * w * arr.dtype.itemsize for (arr, w, _, _, _) in row_ins) + sum(tr * o[0] * 4 for o in outs)
    cm = comm if comm is not None else _Comm()
    nout = no + len(accs)
    res = pl.pallas_call(
        _carry(body, comm, nr + nb, nout, (nrows // tr,)), name=name, grid=(nrows // tr,), out_shape=out_shape + cm.out_shape,
        in_specs=in_specs + cm.specs, out_specs=out_specs + cm.specs, scratch_shapes=cm.scratch if comm is not None else [],
        compiler_params=pltpu.CompilerParams(dimension_semantics=("arbitrary",), vmem_limit_bytes=_vmem(3 * est)),
    )(*[_hbm(r[0]) for r in row_ins], *[_hbm(v) for v in bc_ins], *cm.operands)
    if comm is None:
        return res[:no], res[no:]
    return res[:no], res[no:nout], cm.split(res[nout:])


def _colsum(v):
    return jnp.sum(v, axis=0, keepdims=True)


def _normmod_fwd(name, hc, h, g, s, sh, comm=None):
    d = h.shape[1]
    nct = 0 if hc is None else hc.shape[0] // ROW_TILE
    ins = [_ri(h)] if hc is None else [_ri(hc), _ri(h, ro=-nct)]

    def fn(rows, bcs, i):
        hh = rows[0] if hc is None else jnp.where(i < nct, rows[0], rows[1])
        g_, s_, sh_ = bcs
        s1 = jnp.where(i < nct, s_[0:1], s_[1:2])
        sh1 = jnp.where(i < nct, sh_[0:1], sh_[1:2])
        r = lax.rsqrt(jnp.mean(hh * hh, axis=-1, keepdims=True) + EPS)
        return [hh * r * g_ * (1.0 + s1) + sh1], []

    res = _rowwise(name, fn, h.shape[0] + nct * ROW_TILE, ins, [g, s, sh], [(d, MXU)], comm=comm)
    return res[0][0] if comm is None else (res[0][0], res[2])


def _normmod_bwd(name, hc, h, dxn, dres, g, s):
    d = h.shape[1]
    nct = 0 if hc is None else hc.shape[0] // ROW_TILE
    hins = [_ri(h)] if hc is None else [_ri(hc), _ri(h, ro=-nct)]

    def fn(rows, bcs, i):
        hh = rows[0] if hc is None else jnp.where(i < nct, rows[0], rows[1])
        dx, dr = rows[-2:]
        g_, s_ = bcs
        ctx = i < nct
        s1 = jnp.where(ctx, s_[0:1], s_[1:2])
        r = lax.rsqrt(jnp.mean(hh * hh, axis=-1, keepdims=True) + EPS)
        hr = hh * r
        dy = dx * (1.0 + s1)
        u = dy * g_
        dh = r * u - hr * (r * r) * jnp.mean(u * hh, axis=-1, keepdims=True)
        dh = dh + jnp.where(ctx, 0.0, dr)

        def seg(v):
            v = _colsum(v)
            return jnp.concatenate([jnp.where(ctx, v, 0.0), jnp.where(ctx, 0.0, v)], axis=0)

        return [dh], [seg(dx), seg(dx * hr * g_), seg(dy * hr)]

    (dh,), (dsh, ds, dg) = _rowwise(name, fn, h.shape[0] + nct * ROW_TILE, hins + [_ri(dxn), _ri(dres, ro=-nct)], [g, s],
                                    [(d, F32, d, 0, -nct)], [(2, d)] * 3)
    return dh, dsh, ds, dg


def _resnorm_fwd(name, h, y, gate, g, s, sh):
    d = h.shape[1]

    def fn(rows, bcs, i):
        hh, yy = rows
        gate_, g_, s_, sh_ = bcs
        hn = hh + gate_ * yy
        r = lax.rsqrt(jnp.mean(hn * hn, axis=-1, keepdims=True) + EPS)
        return [hn, hn * r * g_ * (1.0 + s_) + sh_], []

    return _rowwise(name, fn, h.shape[0], [_ri(h), _ri(y)], [gate, g, s, sh], [(d, F32), (d, MXU)])[0]


def _gate_bwd(name, dh, y, gate):
    d = dh.shape[1]

    def fn(rows, bcs, i):
        dd, yy = rows
        dy = dd * bcs[0]
        return [dy], [_colsum(dd * yy), _colsum(dy)]

    (dy,), (dgate, dbias) = _rowwise(name, fn, dh.shape[0], [_ri(dh), _ri(y)], [gate], [(d, MXU)], [(1, d)] * 2)
    return dy, dgate, dbias


def _swiglu_fwd(name, u):
    f = u.shape[1] // 2

    def fn(rows, bcs, i):
        return [_silu(rows[0]) * rows[1]], []

    return _rowwise(name, fn, u.shape[0], [_ri(u, f, 0), _ri(u, f, 1)], [], [(f, MXU)])[0][0]


def _swiglu_bwd(name, u, dhid):
    f = u.shape[1] // 2

    def fn(rows, bcs, i):
        a, b, dd = rows
        return [jnp.concatenate([dd * b * _dsilu(a), dd * _silu(a)], axis=1)], []

    return _rowwise(name, fn, u.shape[0], [_ri(u, f, 0), _ri(u, f, 1), _ri(dhid)], [], [(2 * f, MXU)])[0][0]


def _glu_fwd(name, u):
    d = u.shape[1] // 2

    def fn(rows, bcs, i):
        return [rows[0] * _sigmoid(rows[1])], []

    return _rowwise(name, fn, u.shape[0], [_ri(u, d, 0), _ri(u, d, 1)], [], [(d, F32)])[0][0]


def _glu_bwd(name, u, dgl_lo, dgl_hi):
    d = u.shape[1] // 2

    def fn(rows, bcs, i):
        a, b = rows[:2]
        dd = jnp.concatenate(rows[2:], axis=1)
        sg = _sigmoid(b)
        du = jnp.concatenate([dd * sg, dd * a * sg * (1.0 - sg)], axis=1)
        return [du], [_colsum(du)]

    (du,), (db,) = _rowwise(name, fn, u.shape[0], [_ri(u, d, 0), _ri(u, d, 1), _ri(dgl_lo), _ri(dgl_hi)], [], [(2 * d, MXU)],
                            [(1, 2 * d)])
    return du, db


def _ln_silu_fwd(name, v_lo, v_hi, g, b):
    d = 2 * v_lo.shape[1]

    def fn(rows, bcs, i):
        vv = jnp.concatenate(rows, axis=1)
        mu = jnp.mean(vv, axis=-1, keepdims=True)
        xc = vv - mu
        rs = lax.rsqrt(jnp.mean(xc * xc, axis=-1, keepdims=True) + EPS)
        return [_silu(xc * rs * bcs[0] + bcs[1])], []

    return _rowwise(name, fn, v_lo.shape[0], [_ri(v_lo), _ri(v_hi)], [g, b], [(d, MXU)])[0][0]


def _ln_silu_bwd(name, v_lo, v_hi, ds, g, b):
    ch = v_lo.shape[1]

    def fn(rows, bcs, i):
        vv, dd = jnp.concatenate(rows[:2], axis=1), rows[2]
        mu = jnp.mean(vv, axis=-1, keepdims=True)
        xc = vv - mu
        rs = lax.rsqrt(jnp.mean(xc * xc, axis=-1, keepdims=True) + EPS)
        xh = xc * rs
        dln = dd * _dsilu(xh * bcs[0] + bcs[1])
        dxh = dln * bcs[0]
        dv = rs * (dxh - jnp.mean(dxh, axis=-1, keepdims=True) - xh * jnp.mean(dxh * xh, axis=-1, keepdims=True))
        return [dv[:, :ch], dv[:, ch:]], [_colsum(dln * xh), _colsum(dln)]

    (dv_lo, dv_hi), (dg, db) = _rowwise(name, fn, v_lo.shape[0], [_ri(v_lo), _ri(v_hi), _ri(ds)], [g, b],
                                        [(ch, F32), (ch, F32)], [(1, 2 * ch)] * 2)
    return dv_lo, dv_hi, dg, db


def _final_loss(name, h, f, target, gate, gf):
    d = h.shape[1]

    def fn(rows, bcs, i):
        hh, ff, tg = rows
        gate_, g_ = bcs
        hn = hh + gate_ * ff
        r = lax.rsqrt(jnp.mean(hn * hn, axis=-1, keepdims=True) + EPS)
        hr = hn * r
        err = hr * g_ - tg
        dout = err * (1.0 / d)
        u = dout * g_
        dh = r * u - hr * (r * r) * jnp.mean(u * hn, axis=-1, keepdims=True)
        sq = jnp.sum(_colsum(err * err), axis=1, keepdims=True)
        return [dh], [jnp.broadcast_to(sq, (1, LANES)), _colsum(dout * hr)]

    (dh,), (sq, dgf) = _rowwise(name, fn, h.shape[0], [_ri(h), _ri(f), _ri(target)], [gate, gf], [(d, F32)], [(1, LANES), (1, d)])
    return dh, sq, dgf


GAP = 8


def _gapped(ref_rows, buf, tc, tt):
    cb = buf.shape[1]
    zero = jnp.zeros((GAP, cb), F32)
    buf[0:GAP, :] = zero
    buf[GAP + tc:2 * GAP + tc, :] = zero
    buf[2 * GAP + tt:, :] = zero
    buf[GAP:GAP + tc, :] = ref_rows[0:tc]
    buf[2 * GAP + tc:2 * GAP + tt, :] = ref_rows[tc:tt]
    return buf[...]


def _ungapped(v, tc, tt):
    return jnp.concatenate([v[GAP:GAP + tc], v[2 * GAP + tc:2 * GAP + tt]], axis=0)


def _shift_rows(x, o):
    return x if o == 0 else pltpu.roll(x, (-o) % x.shape[0], 0)


def _ssd_conv_fwd(name, zx, w, b, di, tc, comm=None):
    tt, kc, cd = zx.shape[0], w.shape[0], w.shape[1]
    cb = _pick(cd, (LANES,))
    off = di // cb
    assert kc // 2 < GAP and tc % GAP == 0 and tt % GAP == 0

    def body(x_ref, w_ref, b_ref, o_ref, pre_ref, xp):
        x = _gapped(x_ref[...].astype(F32), xp, tc, tt)
        acc = jnp.broadcast_to(b_ref[...], x.shape)
        for k in range(kc):
            acc = acc + w_ref[k:k + 1, :] * _shift_rows(x, k - kc // 2)
        acc = _ungapped(acc, tc, tt)
        pre_ref[...] = acc.astype(pre_ref.dtype)
        o_ref[...] = _silu(acc).astype(o_ref.dtype)

    cm = comm if comm is not None else _Comm()
    blk = pl.BlockSpec((tt, cb), lambda j: (0, j))
    res = pl.pallas_call(
        _carry(body, comm, 3, 2, (cd // cb,)), name=name, grid=(cd // cb,),
        out_shape=[pltpu.HBM((tt, cd), ACT), pltpu.HBM((tt, cd), ACT)] + cm.out_shape,
        in_specs=[pl.BlockSpec((tt, cb), lambda j: (0, j + off)), pl.BlockSpec((kc, cb), lambda j: (0, j)),
                  pl.BlockSpec((1, cb), lambda j: (0, j))] + cm.specs,
        out_specs=[blk, blk] + cm.specs,
        scratch_shapes=[pltpu.VMEM((tt + 3 * GAP, cb), F32)] + (cm.scratch if comm is not None else []),
        compiler_params=pltpu.CompilerParams(dimension_semantics=("arbitrary",), vmem_limit_bytes=_vmem(5 * tt * cb * 4)),
    )(_hbm(zx), _hbm(w), _hbm(b), *cm.operands)
    return res[0], res[1], cm.split(res[2:])


def _ssd_conv_bwd(name, zx, dact2, w, pre, dzx, di, tc, comm=None):
    tt, kc, cd = zx.shape[0], w.shape[0], w.shape[1]
    cb = _pick(cd, (LANES,))
    off = di // cb

    def body(x_ref, d0_ref, d1_ref, w_ref, pre_ref, _, dx_ref, dw_ref, db_ref, xp, dp):
        x = _gapped(x_ref[...].astype(F32), xp, tc, tt)
        dpre = (d0_ref[...].astype(F32) + d1_ref[...].astype(F32)) * _dsilu(pre_ref[...].astype(F32))
        db_ref[...] = _colsum(dpre)
        dpre = _gapped(dpre, dp, tc, tt)
        dx = jnp.zeros_like(x)
        for k in range(kc):
            sh = _shift_rows(dpre, -(k - kc // 2))
            dx = dx + w_ref[k:k + 1, :] * sh
            dw_ref[k:k + 1, :] = _colsum(sh * x)
        dx_ref[...] = _ungapped(dx, tc, tt).astype(dx_ref.dtype)

    cm = comm if comm is not None else _Comm()
    res = pl.pallas_call(
        _carry(body, comm, 6, 3, (cd // cb,)), name=name, grid=(cd // cb,),
        out_shape=[pltpu.HBM(dzx.shape, dzx.dtype), pltpu.HBM((kc, cd), F32), pltpu.HBM((1, cd), F32)] + cm.out_shape,
        in_specs=[pl.BlockSpec((tt, cb), lambda j: (0, j + off)), pl.BlockSpec((None, tt, cb), lambda j: (0, 0, j)),
                  pl.BlockSpec((None, tt, cb), lambda j: (1, 0, j)), pl.BlockSpec((kc, cb), lambda j: (0, j)),
                  pl.BlockSpec((tt, cb), lambda j: (0, j)), pl.BlockSpec(memory_space=pl.ANY)] + cm.specs,
        out_specs=[pl.BlockSpec((tt, cb), lambda j: (0, j + off)), pl.BlockSpec((kc, cb), lambda j: (0, j)),
                   pl.BlockSpec((1, cb), lambda j: (0, j))] + cm.specs,
        input_output_aliases={5: 0},
        scratch_shapes=[pltpu.VMEM((tt + 3 * GAP, cb), F32)] * 2 + (cm.scratch if comm is not None else []),
        compiler_params=pltpu.CompilerParams(dimension_semantics=("arbitrary",), vmem_limit_bytes=_vmem(10 * tt * cb * 4)),
    )(_hbm(zx), _hbm(dact2), _hbm(dact2), _hbm(w), _hbm(pre), _hbm(dzx), *cm.operands)
    return res[0], res[1], res[2], cm.split(res[3:])


def _strided_conv(name, x, w, b, stride, x_col0=0):
    t, ch = x.shape[0], w.shape[1]
    kk = w.shape[0]
    pad = (kk // 2) * stride
    cb = _pick(ch, (LANES,))
    has_b = b is not None

    def body(*refs):
        x_ref, w_ref = refs[:2]
        o_ref, xp = refs[-2:]
        xp[0:pad, :] = jnp.zeros((pad, cb), F32)
        xp[pad + t:, :] = jnp.zeros((pad, cb), F32)
        xp[pad:pad + t, :] = x_ref[...]
        acc = jnp.broadcast_to(refs[2][...], (t, cb)) if has_b else jnp.zeros((t, cb), F32)
        for k in range(kk):
            acc = acc + w_ref[k:k + 1, :] * xp[k * stride:k * stride + t, :]
        o_ref[...] = acc

    xoff = x_col0 // cb
    ins, specs = [x, w], [pl.BlockSpec((t, cb), lambda j: (0, j + xoff)), pl.BlockSpec((kk, cb), lambda j: (0, j))]
    if has_b:
        ins.append(b)
        specs.append(pl.BlockSpec((1, cb), lambda j: (0, j)))
    return pl.pallas_call(
        body, name=name, grid=(ch // cb,), out_shape=pltpu.HBM((t, ch), F32), in_specs=specs,
        out_specs=pl.BlockSpec((t, cb), lambda j: (0, j)), scratch_shapes=[pltpu.VMEM((t + 2 * pad, cb), F32)],
        compiler_params=pltpu.CompilerParams(dimension_semantics=("parallel",), vmem_limit_bytes=_vmem(6 * t * cb * 4)),
    )(*[_hbm(v) for v in ins])


def _strided_conv_dw(name, x, dv, kk, stride, x_col0=0):
    t, ch = dv.shape
    pad = (kk // 2) * stride
    cb = _pick(ch, (LANES,))

    def body(x_ref, d_ref, dw_ref, db_ref, xp):
        xp[0:pad, :] = jnp.zeros((pad, cb), F32)
        xp[pad + t:, :] = jnp.zeros((pad, cb), F32)
        xp[pad:pad + t, :] = x_ref[...]
        d = d_ref[...]
        for k in range(kk):
            dw_ref[k:k + 1, :] = _colsum(d * xp[k * stride:k * stride + t, :])
        db_ref[...] = _colsum(d)

    blk = pl.BlockSpec((t, cb), lambda j: (0, j))
    xoff = x_col0 // cb
    return pl.pallas_call(
        body, name=name, grid=(ch // cb,), out_shape=[pltpu.HBM((kk, ch), F32), pltpu.HBM((1, ch), F32)],
        in_specs=[pl.BlockSpec((t, cb), lambda j: (0, j + xoff)), blk], out_specs=[pl.BlockSpec((kk, cb), lambda j: (0, j)), pl.BlockSpec((1, cb), lambda j: (0, j))],
        scratch_shapes=[pltpu.VMEM((t + 2 * pad, cb), F32)],
        compiler_params=pltpu.CompilerParams(dimension_semantics=("parallel",), vmem_limit_bytes=_vmem(6 * t * cb * 4)),
    )(_hbm(x), _hbm(dv))


def _grid_t(a, n1, n2):
    return a.reshape(n1, n2, a.shape[-1]).swapaxes(0, 1).reshape(n1 * n2, a.shape[-1])


def _chunk_order(d, i, ncc, nc):
    back = jnp.where(i < ncc, ncc - 1 - i, nc - 1 - (i - ncc))
    return jnp.where(d == 0, i, back)


def _ssd_chunk_setup(d, dt_raw, bias, a_log, q, h, di):
    p = di // h
    dt = _softplus(dt_raw + bias)
    a_neg = -jnp.exp(a_log)
    delta = dt * a_neg
    r = lax.broadcasted_iota(jnp.int32, (q, q), 0)
    c = lax.broadcasted_iota(jnp.int32, (q, q), 1)
    sgn = 1 - 2 * d
    mask = (r - c) * sgn >= 0
    mask_t = (c - r) * sgn >= 0
    a = _dot_lx(mask.astype(MXU), delta, NN, parts=3)
    tot = _colsum(delta)
    ea, dte, cd = jnp.exp(a), jnp.exp(tot - a), jnp.exp(tot)
    hh = lax.broadcasted_iota(jnp.int32, (h, di), 0)
    cc = lax.broadcasted_iota(jnp.int32, (h, di), 1)
    e = (cc // p == hh).astype(MXU)
    ex = _dot_rx(jnp.concatenate([dt, ea, dte, jnp.broadcast_to(cd, (8, h))], axis=0), e, NN)
    eye = (lax.broadcasted_iota(jnp.int32, (h, h), 0) == lax.broadcasted_iota(jnp.int32, (h, h), 1)).astype(MXU)
    a_t = _dot_lx(eye, a, NT, parts=3)
    return dict(dt=dt, a_neg=a_neg, a=a, a_t=a_t, mask=mask, mask_t=mask_t, e=e,
                dt_e=ex[0:q], ea_e=ex[q:2 * q], dte_e=ex[2 * q:3 * q], cd_e=ex[3 * q:3 * q + 1])


def _pick_heads(r, q, hpg, p):
    lane = lax.broadcasted_iota(jnp.int32, (q, hpg * p), 1) // p
    out = jnp.zeros((q, hpg * p), F32)
    for j in range(hpg):
        out = out + jnp.where(lane == j, r[j * q:(j + 1) * q], 0.0)
    return out


def _ssd_fwd(name, xbc, dt2, bias2, alog2, di, tc, comm=None):
    tt, cd = xbc.shape
    h = dt2.shape[-1]
    q, n = SSD_CHUNK, SSD_STATE
    gn = (cd - di) // 2
    g = gn // n
    hpg, p = h // g, di // h
    gp = hpg * p
    nc, ncc = tt // q, tc // q
    assert di % gn == 0

    def body(x_ref, b_ref, c_ref, dt_ref, bias_ref, alog_ref, y_ref, hp_ref, ht):
        d, i = pl.program_id(0), pl.program_id(1)

        @pl.when(i == 0)
        def _():
            ht[...] = jnp.zeros_like(ht)

        s = _ssd_chunk_setup(d, dt_ref[...], bias_ref[...], alog_ref[...], q, h, di)
        xd = x_ref[...].astype(F32) * s["dt_e"]
        hp_ref[...] = ht[...].astype(hp_ref.dtype)
        for gi in range(g):
            bg, cg = b_ref[:, gi * n:(gi + 1) * n].astype(MXU), c_ref[:, gi * n:(gi + 1) * n].astype(MXU)
            sl = slice(gi * gp, (gi + 1) * gp)
            sc = _dot(cg, bg, NT)
            ms = []
            for j in range(hpg):
                hd = gi * hpg + j
                seg = s["a"][:, hd:hd + 1] - s["a_t"][hd:hd + 1, :]
                ms.append(sc * jnp.exp(jnp.where(s["mask"], seg, -jnp.inf)))
            xdg = xd[:, sl]
            ydiag = _pick_heads(_dot(jnp.concatenate(ms, axis=0), xdg, NN), q, hpg, p)
            htg = ht[:, sl]
            y_ref[:, sl] = ydiag + _dot(cg, htg, NN) * s["ea_e"][:, sl]
            ht[:, sl] = s["cd_e"][:, sl] * htg + _dot(bg, xdg * s["dte_e"][:, sl], TN)

    def cidx(d, i):
        return _chunk_order(d, i, ncc, nc)

    cm = comm if comm is not None else _Comm()
    res = pl.pallas_call(
        _carry(body, comm, 6, 2, (2, nc)), name=name, grid=(2, nc),
        out_shape=[pltpu.HBM((2, tt, di), F32), pltpu.HBM((2, nc, n, di), ACT)] + cm.out_shape,
        in_specs=[pl.BlockSpec((q, di), lambda d, i: (cidx(d, i), 0)),
                  pl.BlockSpec((q, gn), lambda d, i: (cidx(d, i), di // gn)),
                  pl.BlockSpec((q, gn), lambda d, i: (cidx(d, i), di // gn + 1)),
                  pl.BlockSpec((None, q, h), lambda d, i: (d, cidx(d, i), 0)),
                  pl.BlockSpec((None, 1, h), lambda d, i: (d, 0, 0)),
                  pl.BlockSpec((None, 1, h), lambda d, i: (d, 0, 0))] + cm.specs,
        out_specs=[pl.BlockSpec((None, q, di), lambda d, i: (d, cidx(d, i), 0)),
                   pl.BlockSpec((None, None, n, di), lambda d, i: (d, cidx(d, i), 0, 0))] + cm.specs,
        scratch_shapes=[pltpu.VMEM((n, di), F32)] + (cm.scratch if comm is not None else []),
        compiler_params=pltpu.CompilerParams(dimension_semantics=("arbitrary", "arbitrary"), vmem_limit_bytes=_vmem(16 * q * di * 4)),
    )(*[_hbm(v) for v in (xbc, xbc, xbc, dt2, bias2, alog2)], *cm.operands)
    return res[0], res[1], cm.split(res[2:])


def _ssd_bwd(name, xbc, dt2, bias2, alog2, dy, hp2, dskip_e, di, tc, comm=None):
    tt, cd = xbc.shape
    h = dt2.shape[-1]
    q, n = SSD_CHUNK, SSD_STATE
    gn = (cd - di) // 2
    g = gn // n
    hpg, p = h // g, di // h
    gp = hpg * p
    nc, ncc = tt // q, tc // q

    def body(x_ref, b_ref, c_ref, dt_ref, bias_ref, alog_ref, dy_ref, hp_ref, dsk_ref,
             dxbc_ref, ddt_ref, dalog_ref, dbias_ref, dht, dxd, off):
        d, i = pl.program_id(0), pl.program_id(1)

        @pl.when(i == 0)
        def _():
            dht[...] = jnp.zeros_like(dht)
            dalog_ref[...] = jnp.zeros_like(dalog_ref)
            dbias_ref[...] = jnp.zeros_like(dbias_ref)

        s = _ssd_chunk_setup(d, dt_ref[...], bias_ref[...], alog_ref[...], q, h, di)
        x, dyc = x_ref[...].astype(F32), dy_ref[...]
        xd = x * s["dt_e"]
        dyea = dyc * s["ea_e"]
        xdte = xd * s["dte_e"]
        lane = lax.broadcasted_iota(jnp.int32, (q, gp), 1) // p
        lane_h = lax.broadcasted_iota(jnp.int32, (q, h), 1)
        da_d = jnp.zeros((q, h), F32)
        last_e = []
        for gi in range(g):
            bg, cg = b_ref[:, gi * n:(gi + 1) * n].astype(MXU), c_ref[:, gi * n:(gi + 1) * n].astype(MXU)
            sl = slice(gi * gp, (gi + 1) * gp)
            sc, sct = _dot(cg, bg, NT), _dot(bg, cg, NT)
            dyg, xdg = dyc[:, sl], xd[:, sl]
            htg, dhtg = hp_ref[:, sl].astype(F32), dht[:, sl]
            dystack = jnp.concatenate([jnp.where(lane == j, dyg, 0.0) for j in range(hpg)], axis=0)
            xdstack = jnp.concatenate([jnp.where(lane == j, xdg, 0.0) for j in range(hpg)], axis=0)
            gs = _dot(dystack, xdg, NT)
            gst = _dot(xdstack, dyg, NT)
            ds = jnp.zeros((q, q), F32)
            mts = []
            for j in range(hpg):
                hd = gi * hpg + j
                col, rw = s["a"][:, hd:hd + 1], s["a_t"][hd:hd + 1, :]
                gl = gs[j * q:(j + 1) * q] * jnp.exp(jnp.where(s["mask"], col - rw, -jnp.inf))
                ds = ds + gl
                mt = sct * jnp.exp(jnp.where(s["mask_t"], rw - col, -jnp.inf))
                mts.append(mt)
                da_j = jnp.sum(gl * sc, axis=1, keepdims=True) - jnp.sum(gst[j * q:(j + 1) * q] * mt, axis=1, keepdims=True)
                da_d = da_d + jnp.where(lane_h == hd, da_j, 0.0)
            dxd_diag = _pick_heads(_dot(jnp.concatenate(mts, axis=0), dyg, NN), q, hpg, p)
            z = _dot(bg, dhtg, NN) * s["dte_e"][:, sl]
            yoff = _dot(cg, htg, NN) * s["ea_e"][:, sl]
            off[:, sl] = dyg * yoff - xdg * z
            dxd[:, sl] = dxd_diag + z
            dxbc_ref[:, di + gi * n:di + (gi + 1) * n] = (_dot(ds, cg, TN) + _dot(xdte[:, sl], dhtg, NT)).astype(dxbc_ref.dtype)
            dxbc_ref[:, di + gn + gi * n:di + gn + (gi + 1) * n] = (_dot(ds, bg, NN)
                                                                    + _dot(dyea[:, sl], htg, NT)).astype(dxbc_ref.dtype)
            last_e.append(s["cd_e"][:, sl] * _colsum(dhtg * htg) + _colsum(xdg * z))
            dht[:, sl] = s["cd_e"][:, sl] * dhtg + _dot(cg, dyea[:, sl], TN)
        dxd_all = dxd[...]
        last = jnp.concatenate(last_e, axis=1)
        da = da_d + _dot_rx(off[...], s["e"], NT)
        last_h = _dot_rx(jnp.broadcast_to(last, (8, di)), s["e"], NT)[0:1]
        ddelta = _dot_lx(s["mask_t"].astype(MXU), da, NN, parts=3) + last_h
        ddt = ddelta * s["a_neg"] + _dot_rx(dxd_all * x, s["e"], NT)
        ddt_raw = ddt * _sigmoid(dt_ref[...] + bias_ref[...])
        ddt_ref[...] = ddt_raw
        dalog_ref[...] += _colsum(ddelta * s["dt"]) * s["a_neg"]
        dbias_ref[...] += _colsum(ddt_raw)
        dxbc_ref[:, 0:di] = (dxd_all * s["dt_e"] + jnp.where(d == 0, dyc * dsk_ref[...], 0.0)).astype(dxbc_ref.dtype)

    def cidx(d, i):
        return _chunk_order(d, nc - 1 - i, ncc, nc)

    cm = comm if comm is not None else _Comm()
    res = pl.pallas_call(
        _carry(body, comm, 9, 4, (2, nc)), name=name, grid=(2, nc),
        out_shape=[pltpu.HBM((2, tt, cd), ACT), pltpu.HBM((2, tt, h), F32),
                   pltpu.HBM((2, 1, h), F32), pltpu.HBM((2, 1, h), F32)] + cm.out_shape,
        in_specs=[pl.BlockSpec((q, di), lambda d, i: (cidx(d, i), 0)),
                  pl.BlockSpec((q, gn), lambda d, i: (cidx(d, i), di // gn)),
                  pl.BlockSpec((q, gn), lambda d, i: (cidx(d, i), di // gn + 1)),
                  pl.BlockSpec((None, q, h), lambda d, i: (d, cidx(d, i), 0)),
                  pl.BlockSpec((None, 1, h), lambda d, i: (d, 0, 0)),
                  pl.BlockSpec((None, 1, h), lambda d, i: (d, 0, 0)),
                  pl.BlockSpec((q, di), lambda d, i: (cidx(d, i), 0)),
                  pl.BlockSpec((None, None, n, di), lambda d, i: (d, cidx(d, i), 0, 0)),
                  pl.BlockSpec((1, di), lambda d, i: (0, 0))] + cm.specs,
        out_specs=[pl.BlockSpec((None, q, cd), lambda d, i: (d, cidx(d, i), 0)),
                   pl.BlockSpec((None, q, h), lambda d, i: (d, cidx(d, i), 0)),
                   pl.BlockSpec((None, 1, h), lambda d, i: (d, 0, 0)),
                   pl.BlockSpec((None, 1, h), lambda d, i: (d, 0, 0))] + cm.specs,
        scratch_shapes=[pltpu.VMEM((n, di), F32), pltpu.VMEM((q, di), F32), pltpu.VMEM((q, di), F32)]
        + (cm.scratch if comm is not None else []),
        compiler_params=pltpu.CompilerParams(dimension_semantics=("arbitrary", "arbitrary"), vmem_limit_bytes=_vmem(24 * q * di * 4)),
    )(*[_hbm(v) for v in (xbc, xbc, xbc, dt2, bias2, alog2, dy, hp2, dskip_e)], *cm.operands)
    return res[0], res[1], res[2], res[3], cm.split(res[4:])


def _ssd_gate_fwd(name, y2, xbc, zx, dskip_e, norm_w, di, nct, t):
    def fn(rows, bcs, i):
        yf, yb, xs, z = rows
        zg = (yf + yb + bcs[0] * xs) * _silu(z)
        rn = lax.rsqrt(jnp.mean(zg * zg, axis=-1, keepdims=True) + EPS)
        return [zg * rn * bcs[1]], []

    ins = [_ri(y2, lead=0, ro=nct), _ri(y2, lead=1, ro=nct), _ri(xbc, di, 0, ro=nct), _ri(zx, di, 0, ro=nct)]
    return _rowwise(name, fn, t, ins, [dskip_e, norm_w], [(di, MXU)])[0][0]


def _ssd_gate_bwd(name, dyn, y2, xbc, zx, dskip_e, norm_w, di, nct, tt):
    def fn(rows, bcs, i):
        dn, yf, yb, xs, z = rows
        lat = i >= nct
        ytot = yf + yb + bcs[0] * xs
        sz = _silu(z)
        zg = ytot * sz
        rn = lax.rsqrt(jnp.mean(zg * zg, axis=-1, keepdims=True) + EPS)
        u = dn * bcs[1]
        dzg = rn * u - zg * (rn * rn * rn) * jnp.mean(u * zg, axis=-1, keepdims=True)
        dy = jnp.where(lat, dzg * sz, 0.0)
        dz = jnp.where(lat, dzg * ytot * _dsilu(z), 0.0)
        return [dy, dz], [jnp.where(lat, _colsum(dn * zg * rn), 0.0), jnp.where(lat, _colsum(dy * xs), 0.0)]

    ins = [_ri(dyn, ro=-nct), _ri(y2, lead=0), _ri(y2, lead=1), _ri(xbc, di, 0), _ri(zx, di, 0)]
    (dy, dzx), (dnw, ddsk) = _rowwise(name, fn, tt, ins, [dskip_e, norm_w], [(di, F32), (di, MXU, zx.shape[1], 0)], [(1, di)] * 2)
    return dy, dzx, dnw, ddsk


def _ada_fwd(name, cs, w, b):
    nl, d, c = w.shape
    r = cs.shape[0]

    def body(cs_ref, w_ref, b_ref, o_ref):
        o_ref[...] = _dot(_silu(cs_ref[...]), w_ref[...], NN) + b_ref[...]

    return pl.pallas_call(
        body, name=name, grid=(nl,), out_shape=pltpu.HBM((nl, r, c), F32),
        in_specs=[pl.BlockSpec((r, d), lambda l: (0, 0)), pl.BlockSpec((None, d, c), lambda l: (l, 0, 0)),
                  pl.BlockSpec((None, 1, c), lambda l: (l, 0, 0))],
        out_specs=pl.BlockSpec((None, r, c), lambda l: (l, 0, 0)),
        compiler_params=pltpu.CompilerParams(dimension_semantics=("parallel",), vmem_limit_bytes=_vmem(2 * d * c * 4)),
    )(_hbm(cs), _hbm(w), _hbm(b))


def _ada_bwd(name, cs, w, dmod):
    nl, d, c = w.shape
    r = cs.shape[0]

    def body(cs_ref, w_ref, dm_ref, dw_ref, dsc_ref):
        dm = dm_ref[...]
        dw_ref[...] = _dot(_silu(cs_ref[...]), dm, TN)

        @pl.when(pl.program_id(0) == 0)
        def _():
            dctx = jnp.broadcast_to(_colsum(dm[r // 2:]), (8, c))
            dsc_ref[...] = _dot(dctx, w_ref[...], NT)[0:1]

    return pl.pallas_call(
        body, name=name, grid=(nl,), out_shape=[pltpu.HBM((nl, d, c), F32), pltpu.HBM((1, d), F32)],
        in_specs=[pl.BlockSpec((r, d), lambda l: (0, 0)), pl.BlockSpec((None, d, c), lambda l: (l, 0, 0)),
                  pl.BlockSpec((None, r, c), lambda l: (l, 0, 0))],
        out_specs=[pl.BlockSpec((None, d, c), lambda l: (l, 0, 0)), pl.BlockSpec((1, d), lambda l: (0, 0))],
        compiler_params=pltpu.CompilerParams(dimension_semantics=("arbitrary",), vmem_limit_bytes=_vmem(4 * d * c * 4)),
    )(_hbm(cs), _hbm(w), _hbm(dmod))


def _adam_math(w, g, m, v):
    m = ADAM_B1 * m + (1.0 - ADAM_B1) * g
    v = ADAM_B2 * v + (1.0 - ADAM_B2) * (g * g)
    m_hat = m / (1.0 - ADAM_B1 ** ADAM_STEP)
    v_hat = v / (1.0 - ADAM_B2 ** ADAM_STEP)
    delta = -ADAM_LR * (m_hat / (jnp.sqrt(v_hat) + ADAM_EPS) + ADAM_WD * w)
    return delta, m, v


def _adam(name, slots, w, m, v, comm=None):
    segs = list(slots) if isinstance(slots, (list, tuple)) else [slots]
    nseg = len(segs)
    ns, c = segs[0].shape[0], segs[0].shape[2]
    r = sum(s.shape[1] for s in segs)
    tr = _pick(min(s.shape[1] for s in segs), (256, 128, 64, 32, 16, 8))
    starts = [sum(s.shape[1] for s in segs[:k]) // tr for k in range(nseg)]

    def body(*refs):
        s_refs, (w_ref, m_ref, v_ref), (g_ref, d_ref, mo_ref, vo_ref) = refs[:nseg], refs[nseg:nseg + 3], refs[nseg + 3:]
        i = pl.program_id(0)

        def total(s_ref):
            g = s_ref[0].astype(F32)
            for k in range(1, ns):
                g = g + s_ref[k].astype(F32)
            return g

        g = total(s_refs[0])
        for k in range(1, nseg):
            g = jnp.where(i >= starts[k], total(s_refs[k]), g)
        d, mn, vn = _adam_math(w_ref[...], g, m_ref[...], v_ref[...])
        g_ref[...], d_ref[...], mo_ref[...], vo_ref[...] = g, d, mn, vn

    blk = pl.BlockSpec((tr, c), lambda i: (i, 0))
    seg_specs = [pl.BlockSpec((ns, tr, c), lambda i, st=starts[k], nt=segs[k].shape[1] // tr: (0, jnp.clip(i - st, 0, nt - 1), 0))
                 for k in range(nseg)]
    cm = comm if comm is not None else _Comm()
    res = pl.pallas_call(
        _carry(body, comm, nseg + 3, 4, (r // tr,)), name=name, grid=(r // tr,),
        out_shape=[pltpu.HBM((r, c), F32)] * 4 + cm.out_shape,
        in_specs=seg_specs + [blk, blk, blk] + cm.specs, out_specs=[blk] * 4 + cm.specs,
        scratch_shapes=cm.scratch if comm is not None else [],
        compiler_params=pltpu.CompilerParams(dimension_semantics=("arbitrary",), vmem_limit_bytes=_vmem(16 * nseg * tr * c * 4)),
    )(*[_hbm(s) for s in segs], _hbm(w), _hbm(m), _hbm(v), *cm.operands)
    return res[:4] if comm is None else (res[:4], cm.split(res[4:]))


def _adam_small(name, slots, ws, ms, vs, scale=None):
    k = len(slots)

    def body(*refs):
        s_refs, w_refs, m_refs, v_refs = refs[:k], refs[k:2 * k], refs[2 * k:3 * k], refs[3 * k:4 * k]
        sc_ref = refs[4 * k] if scale is not None else None
        outs = refs[4 * k + (scale is not None):]
        for a in range(k):
            g = s_refs[a][0]
            for j in range(1, NDEV):
                g = g + s_refs[a][j]
            if scale is not None and a == scale[0]:
                g = g * _dsilu(sc_ref[...])
            d, mn, vn = _adam_math(w_refs[a][...], g, m_refs[a][...], v_refs[a][...])
            outs[a][...], outs[k + a][...], outs[2 * k + a][...], outs[3 * k + a][...] = g, d, mn, vn

    shapes = [pltpu.HBM(w.shape, F32) for w in ws]
    extra = [scale[1]] if scale is not None else []
    ins = [*slots, *ws, *ms, *vs, *extra]

    def whole(shape):
        return pl.BlockSpec(shape, lambda i, nd=len(shape): (0,) * nd)

    res = pl.pallas_call(body, name=name, grid=(1,), out_shape=shapes * 4, in_specs=[whole(v.shape) for v in ins],
                         out_specs=[whole(s.shape) for s in shapes * 4])(*[_hbm(v) for v in ins])
    return res[:k], res[k:2 * k], res[2 * k:3 * k], res[3 * k:]


def _unshard_cols(g):
    g = jnp.moveaxis(g, 0, -2)
    return g.reshape(g.shape[:-2] + (g.shape[-2] * g.shape[-1],))


def _shard_cols(a):
    a = a.reshape(a.shape[:-1] + (NDEV, a.shape[-1] // NDEV))
    return jnp.moveaxis(a, -2, 0)


def _unshard_rows(g):
    g = jnp.moveaxis(g, 0, -3)
    return g.reshape(g.shape[:-3] + (g.shape[-3] * g.shape[-2], g.shape[-1]))


def _shard_rows(a):
    a = a.reshape(a.shape[:-2] + (NDEV, a.shape[-2] // NDEV, a.shape[-1]))
    return jnp.moveaxis(a, -3, 0)


def _flat2(a):
    return a.reshape((-1, a.shape[-1]))


def kernel(x, c, ctx, c_ctx, ada_w, ada_b, norm_mix_g, norm_ffn_g, final_norm_g, ssd_w_in, ssd_conv_w, ssd_conv_b, ssd_dt_bias_f, ssd_dt_bias_b, ssd_a_log_f, ssd_a_log_b, ssd_d_skip, ssd_norm_w, ssd_w_out, conf_w_pw1, conf_b_pw1, conf_dw_w, conf_dw_b, conf_ln_g, conf_ln_b, conf_w_pw2, conf_b_pw2, ffn_w_in, ffn_w_out, loss_target, m_c_ctx, m_ada_w, m_ada_b, m_norm_mix_g, m_norm_ffn_g, m_final_norm_g, m_ssd_w_in, m_ssd_conv_w, m_ssd_conv_b, m_ssd_dt_bias_f, m_ssd_dt_bias_b, m_ssd_a_log_f, m_ssd_a_log_b, m_ssd_d_skip, m_ssd_norm_w, m_ssd_w_out, m_conf_w_pw1, m_conf_b_pw1, m_conf_dw_w, m_conf_dw_b, m_conf_ln_g, m_conf_ln_b, m_conf_w_pw2, m_conf_b_pw2, m_ffn_w_in, m_ffn_w_out, v_c_ctx, v_ada_w, v_ada_b, v_norm_mix_g, v_norm_ffn_g, v_final_norm_g, v_ssd_w_in, v_ssd_conv_w, v_ssd_conv_b, v_ssd_dt_bias_f, v_ssd_dt_bias_b, v_ssd_a_log_f, v_ssd_a_log_b, v_ssd_d_skip, v_ssd_norm_w, v_ssd_w_out, v_conf_w_pw1, v_conf_b_pw1, v_conf_dw_w, v_conf_dw_b, v_conf_ln_g, v_conf_ln_b, v_conf_w_pw2, v_conf_b_pw2, v_ffn_w_in, v_ffn_w_out):
    args = dict(locals())
    names = ['c_ctx', 'ada_w', 'ada_b', 'norm_mix_g', 'norm_ffn_g', 'final_norm_g', 'ssd_w_in', 'ssd_conv_w', 'ssd_conv_b',
             'ssd_dt_bias_f', 'ssd_dt_bias_b', 'ssd_a_log_f', 'ssd_a_log_b', 'ssd_d_skip', 'ssd_norm_w', 'ssd_w_out',
             'conf_w_pw1', 'conf_b_pw1', 'conf_dw_w', 'conf_dw_b', 'conf_ln_g', 'conf_ln_b', 'conf_w_pw2', 'conf_b_pw2',
             'ffn_w_in', 'ffn_w_out']
    me = 4 * lax.axis_index("x") + 2 * lax.axis_index("y") + lax.axis_index("c")
    t, d = x.shape[1], x.shape[2]
    tc = ctx.shape[1]
    tt = tc + t
    nct = tc // ROW_TILE
    assert tc % ROW_TILE == 0 and t % ROW_TILE == 0
    h = ssd_dt_bias_f.shape[-1]
    di = ssd_norm_w.shape[-1]
    cdim = ssd_conv_b.shape[-1]
    kc = ssd_conv_w.shape[1]
    ck = conf_dw_w.shape[1]
    ch = d // 2
    rows_g = t // GRID_W
    nl = ada_w.shape[0]
    cw = ada_w.shape[2]
    x2, ctx2, tgt = x[0], ctx[0], loss_target[0]

    (c_all, convw_g), _ = _exchange("gather_first", [c, ssd_conv_w[0]])
    ride_norm = _Comm(gather=[ssd_w_in[0].astype(WIRE)])
    ride_proj = _Comm(gather=[ssd_w_out[0].astype(WIRE), conf_w_pw2[0].astype(WIRE)])
    ride_conv = _Comm(gather=[conf_w_pw1[0].astype(WIRE), conf_b_pw1, conf_dw_w[0], conf_dw_b, conf_ln_g, conf_ln_b, conf_b_pw2])
    ride_scan = _Comm(gather=[ffn_w_in[0].astype(WIRE), ffn_w_in[1].astype(WIRE), ffn_w_out[0].astype(WIRE), ffn_w_out[1].astype(WIRE)])
    conv_w_full = _unshard_cols(convw_g)

    cs_all = jnp.concatenate([c_all[:, 0, :], jnp.broadcast_to(c_ctx[None, :], (NDEV, d))], axis=0)
    ada_b_mine = lax.dynamic_slice_in_dim(ada_b, me * cw, cw, axis=1)[:, None, :]
    mod_part = _ada_fwd("ada_fwd", cs_all, ada_w, ada_b_mine)
    (mod_g,), _ = _exchange("gather_mod", [mod_part])
    mod_all = jnp.moveaxis(mod_g, 0, 2).reshape(nl, 2 * NDEV, NDEV * cw)
    mod_lat = lax.dynamic_slice_in_dim(mod_all, me, 1, axis=1)[:, 0, :]
    mod_ctx = mod_all[0, NDEV, :]

    def six(v):
        return [v[k * d:(k + 1) * d][None, :] for k in range(6)]

    sh1, s1, g1, sh2, s2, g2 = six(mod_lat[0])
    csh1, cs1 = six(mod_ctx)[:2]
    sh1b, s1b, g1b, sh2b, s2b, g2b = six(mod_lat[1])
    nmg, nfg = norm_mix_g, norm_ffn_g

    s01, sh01 = jnp.concatenate([cs1, s1], axis=0), jnp.concatenate([csh1, sh1], axis=0)
    xn_all, ((w_in_g,), _) = _normmod_fwd("l0_norm", ctx2, x2, nmg[0:1], s01, sh01, comm=ride_norm)
    w_ssd_in = _unshard_cols(w_in_g)
    w_dt = jnp.pad(w_ssd_in[:, di + cdim:], ((0, 0), (0, LANES - 2 * h)))
    zx, ((w_out_g, pw2_g), _) = _mm("ssd_in_proj", xn_all, w_ssd_in, "nn", ACT, comm=ride_proj, n_used=di + cdim)
    dtr = _mm("ssd_dt_proj", xn_all, w_dt, "nn")
    dt2 = jnp.moveaxis(dtr[:, :2 * h].reshape(tt, 2, h), 1, 0)
    bias2 = jnp.stack([ssd_dt_bias_f, ssd_dt_bias_b])
    alog2 = jnp.stack([ssd_a_log_f, ssd_a_log_b])
    xbc, xbc_pre, ((pw1_g, bpw1_g, dww_g, dwb_g, lng_g, lnb_g, bpw2_g), _) = _ssd_conv_fwd("ssd_conv", zx, conv_w_full, ssd_conv_b, di, tc,
                                                                                 comm=ride_conv)
    y2, hp2, (ffn_g, _) = _ssd_fwd("ssd_scan", xbc, dt2, bias2, alog2, di, tc, comm=ride_scan)
    w_ssd_out = _unshard_rows(w_out_g)
    w_pw1, w_pw2 = _unshard_cols(pw1_g), _unshard_rows(pw2_g)
    w_fin = [_unshard_cols(ffn_g[0]), _unshard_cols(ffn_g[1])]
    w_fout = [_unshard_rows(ffn_g[2]), _unshard_rows(ffn_g[3])]
    dw_w_full = _unshard_cols(dww_g)
    b_pw1, dw_b, ln_g, ln_b, b_pw2 = (_unshard_cols(a) for a in (bpw1_g, dwb_g, lng_g, lnb_g, bpw2_g))
    dskip_e = jnp.repeat(ssd_d_skip, di // h, axis=1)
    yn = _ssd_gate_fwd("ssd_gate", y2, xbc, zx, dskip_e, ssd_norm_w, di, nct, t)
    mix0 = _mm("ssd_out_proj", yn, w_ssd_out, "nn")
    h1, xf0 = _resnorm_fwd("l0_res_norm", x2, mix0, g1, nfg[0:1], s2, sh2)
    u0 = _mm("ffn0_in", xf0, w_fin[0], "nn", ACT)
    hid0 = _swiglu_fwd("ffn0_act", u0)
    f0 = _mm("ffn0_out", hid0, w_fout[0], "nn")
    h2, xn1 = _resnorm_fwd("l1_norm", h1, f0, g2, nmg[1:2], s1b, sh1b)
    u1 = _mm("conf_pw1", xn1, w_pw1, "nn", ACT, bias=b_pw1)
    gl = _glu_fwd("conf_glu", u1)
    gl_h = _grid_t(gl[:, :ch], rows_g, GRID_W)
    v_ht = _strided_conv("conf_conv_h", gl_h, dw_w_full[:, :ch], dw_b[:, :ch], rows_g)
    v_v = _strided_conv("conf_conv_v", gl, dw_w_full[:, ch:], dw_b[:, ch:], GRID_W, x_col0=ch)
    v_h = _grid_t(v_ht, GRID_W, rows_g)
    sl = _ln_silu_fwd("conf_ln", v_h, v_v, ln_g, ln_b)
    mix1 = _mm("conf_pw2", sl, w_pw2, "nn", bias=b_pw2)
    h3, xf1 = _resnorm_fwd("l1_res_norm", h2, mix1, g1b, nfg[1:2], s2b, sh2b)
    u2 = _mm("ffn1_in", xf1, w_fin[1], "nn", ACT)
    hid1 = _swiglu_fwd("ffn1_act", u2)
    f1 = _mm("ffn1_out", hid1, w_fout[1], "nn")
    dh, sq, d_final_g = _final_loss("final_loss", h3, f1, tgt, g2b, final_norm_g[None, :])

    zero2 = jnp.zeros((2, d), F32)

    def ffn_bwd(tag, dh, hin, xf, u, hid, f, gate, w_in, w_out, g_norm, s_mod):
        df, dgate, _ = _gate_bwd(tag + "_gate_bwd", dh, f, gate)
        dhid = _mm(tag + "_dhid", df, w_out, "nt", ACT)
        dw_out = _mm(tag + "_dwout", hid, df, "tn", WIRE)
        du = _swiglu_bwd(tag + "_act_bwd", u, dhid)
        dw_in = _mm(tag + "_dwin", xf, du, "tn", WIRE)
        dxf = _mm(tag + "_dx", du, w_in, "nt")
        s_2 = jnp.concatenate([s_mod, s_mod], axis=0)
        dh, dsh, ds, dg = _normmod_bwd(tag + "_norm_bwd", None, hin, dxf, dh, g_norm, s_2)
        return dh, dgate, dsh[1:2], ds[1:2], dg[1:2], dw_in, dw_out

    dh, d_g2b, d_sh2b, d_s2b, d_nfg1, g_fin1, g_fout1 = ffn_bwd("ffn1", dh, h3, xf1, u2, hid1, f1, g2b, w_fin[1], w_fout[1], nfg[1:2], s2b)
    dmix1, d_g1b, g_bpw2 = _gate_bwd("conf_gate_bwd", dh, mix1, g1b)
    dsl = _mm("conf_dsl", dmix1, w_pw2, "nt")
    g_pw2 = _mm("conf_dwpw2", sl, dmix1, "tn", WIRE)
    dv_lo, dv_v, g_lng, g_lnb = _ln_silu_bwd("conf_ln_bwd", v_h, v_v, dsl, ln_g, ln_b)
    dv_h = _grid_t(dv_lo, rows_g, GRID_W)
    w_flip = dw_w_full[::-1]
    dgl_h = _strided_conv("conf_conv_h_bwd", dv_h, w_flip[:, :ch], None, rows_g)
    dgl_v = _strided_conv("conf_conv_v_bwd", dv_v, w_flip[:, ch:], None, GRID_W)
    g_dww_h, g_dwb_h = _strided_conv_dw("conf_conv_h_dw", gl_h, dv_h, ck, rows_g)
    g_dww_v, g_dwb_v = _strided_conv_dw("conf_conv_v_dw", gl, dv_v, ck, GRID_W, x_col0=ch)
    g_dww, g_dwb = jnp.concatenate([g_dww_h, g_dww_v], axis=1), jnp.concatenate([g_dwb_h, g_dwb_v], axis=1)
    du1, g_bpw1 = _glu_bwd("conf_glu_bwd", u1, _grid_t(dgl_h, GRID_W, rows_g), dgl_v)
    g_pw1 = _mm("conf_dwpw1", xn1, du1, "tn", WIRE)
    dxn1 = _mm("conf_dx", du1, w_pw1, "nt")
    dh, dsh_, ds_, dg_ = _normmod_bwd("l1_norm_bwd", None, h2, dxn1, dh, nmg[1:2], jnp.concatenate([s1b, s1b], axis=0))
    d_sh1b, d_s1b, d_nmg1 = dsh_[1:2], ds_[1:2], dg_[1:2]
    dh, d_g2, d_sh2, d_s2, d_nfg0, g_fin0, g_fout0 = ffn_bwd("ffn0", dh, h1, xf0, u0, hid0, f0, g2, w_fin[0], w_fout[0], nfg[0:1], s2)
    dmix0, d_g1, _ = _gate_bwd("ssd_gate_res_bwd", dh, mix0, g1)
    dyn = _mm("ssd_dyn", dmix0, w_ssd_out, "nt")
    g_ssd_out = _mm("ssd_dwout", yn, dmix0, "tn", WIRE)
    dy, dzx, g_normw, ddsk_e = _ssd_gate_bwd("ssd_gate_bwd", dyn, y2, xbc, zx, dskip_e, ssd_norm_w, di, nct, tt)
    ride_scan_bwd = _Comm(scatter=[_shard_cols(g_fin0), _shard_cols(g_fin1), _shard_rows(g_fout0), _shard_rows(g_fout1)])
    ride_conv_bwd = _Comm(scatter=[_shard_rows(g_ssd_out), _shard_cols(g_pw1), _shard_rows(g_pw2), _shard_cols(g_bpw1),
                                   _shard_cols(g_dww), _shard_cols(g_dwb), _shard_cols(g_lng), _shard_cols(g_lnb), _shard_cols(g_bpw2)])
    dxbc2, ddt2, g_alog2, g_bias2, (_, ffn_r) = _ssd_bwd("ssd_scan_bwd", xbc, dt2, bias2, alog2, dy, hp2, dskip_e, di, tc,
                                                         comm=ride_scan_bwd)
    dzx, g_convw, g_convb, (_, conv_r) = _ssd_conv_bwd("ssd_conv_bwd", zx, dxbc2, conv_w_full, xbc_pre, dzx, di, tc,
                                                       comm=ride_conv_bwd)
    ddt_p = jnp.pad(jnp.moveaxis(ddt2, 0, 1).reshape(tt, 2 * h), ((0, 0), (0, LANES - 2 * h))).astype(MXU)
    g_ssd_in = jnp.concatenate([_mm("ssd_dw_zx", xn_all, dzx, "tn", WIRE),
                                _mm("ssd_dw_dt", xn_all, ddt_p, "tn", WIRE)[:, :2 * h]], axis=1)
    dxn, (_, (ssd_in_r, convw_r)) = _mm("ssd_dx_zx", dzx, w_ssd_in, "nt",
                                        comm=_Comm(scatter=[_shard_cols(g_ssd_in), _shard_cols(g_convw)]))
    dxn = _mm("ssd_dx_dt", ddt_p, w_dt, "nt", add=dxn)
    dh0, dsh_, ds_, dg_ = _normmod_bwd("l0_norm_bwd", ctx2, x2, dxn, dh, nmg[0:1], s01)
    grad_x = dh0[None]
    d_csh1, d_sh1, d_cs1, d_s1 = dsh_[0:1], dsh_[1:2], ds_[0:1], ds_[1:2]
    d_nmg0 = dg_[0:1] + dg_[1:2]

    z1 = jnp.zeros((1, d), F32)
    dmod = jnp.concatenate([jnp.concatenate([d_sh1, d_s1, d_g1, d_sh2, d_s2, d_g2], axis=1),
                            jnp.concatenate([d_sh1b, d_s1b, d_g1b, d_sh2b, d_s2b, d_g2b], axis=1),
                            jnp.concatenate([d_csh1, d_cs1, z1, z1, z1, z1], axis=1)], axis=0)
    out = {}

    def put(name, res):
        w = args[name]
        out["grad_" + name], out["delta_" + name], out["new_m_" + name], out["new_v_" + name] = (r.reshape(w.shape) for r in res)

    def adam_big(name, slots, comm=None):
        return _adam("adam_" + name, slots, _flat2(args[name]), _flat2(args["m_" + name]), _flat2(args["v_" + name]), comm=comm)

    res, ((dmod_g,), _) = adam_big("ffn_w_in", [ffn_r[0], ffn_r[1]], comm=_Comm(gather=[dmod]))
    put("ffn_w_in", res)
    dmod_mine = lax.dynamic_slice_in_dim(dmod_g, me * cw, cw, axis=2)
    dmod16 = jnp.stack([jnp.concatenate([dmod_mine[:, 0], dmod_mine[:, 2]], axis=0),
                        jnp.concatenate([dmod_mine[:, 1], jnp.zeros((NDEV, cw), F32)], axis=0)])
    g_ada_w, dsc_part = _ada_bwd("ada_bwd", cs_all, ada_w, dmod16)
    g_ada_b = dmod[0:2] + jnp.concatenate([dmod[2:3], jnp.zeros((1, 6 * d), F32)], axis=0)

    d_dskip = jnp.sum(ddsk_e.reshape(h, di // h), axis=1)[None, :]
    rep = [dsc_part, g_ada_b, jnp.concatenate([d_nmg0, d_nmg1], axis=0), jnp.concatenate([d_nfg0, d_nfg1], axis=0),
           d_final_g, g_convb, g_bias2[0], g_bias2[1], g_alog2[0], g_alog2[1], d_dskip, g_normw]
    res, (rep_g, _) = _adam("adam_ada_w", _flat2(g_ada_w)[None], _flat2(ada_w), _flat2(m_ada_w), _flat2(v_ada_w),
                            comm=_Comm(gather=rep + [sq]))
    put("ada_w", res)
    rep_g, sq_g = rep_g[:-1], rep_g[-1]
    loss = (0.5 / d) * jnp.sum(sq_g[:, 0, 0])
    small_r = [convw_r] + list(conv_r[3:])

    for name, slots in zip(["ssd_w_in", "ssd_w_out", "conf_w_pw1", "conf_w_pw2", "ffn_w_out"],
                           [ssd_in_r, conv_r[0], conv_r[1], conv_r[2], [ffn_r[2], ffn_r[3]]]):
        put(name, adam_big(name, slots))
    small_names = ["ssd_conv_w", "conf_b_pw1", "conf_dw_w", "conf_dw_b", "conf_ln_g", "conf_ln_b", "conf_b_pw2",
                   "c_ctx", "ada_b", "norm_mix_g", "norm_ffn_g", "final_norm_g", "ssd_conv_b", "ssd_dt_bias_f", "ssd_dt_bias_b",
                   "ssd_a_log_f", "ssd_a_log_b", "ssd_d_skip", "ssd_norm_w"]
    slots = list(small_r) + list(rep_g)

    def as2(a):
        return a.reshape((1, -1)) if a.ndim == 1 else _flat2(a)

    res = _adam_small("adam_small", slots, [as2(args[n]) for n in small_names], [as2(args["m_" + n]) for n in small_names],
                      [as2(args["v_" + n]) for n in small_names], scale=(small_names.index("c_ctx"), c_ctx[None, :]))
    for k, name in enumerate(small_names):
        put(name, [r[k] for r in res])
    return (loss, grad_x, *[out["grad_" + n] for n in names], *[out["delta_" + n] for n in names],
            *[out["new_m_" + n] for n in names], *[out["new_v_" + n] for n in names])


def _flat3(a):
    return a.reshape((a.shape[0], -1, a.shape[-1]))
```

```python
import functools

import jax
import jax.numpy as jnp
from jax import lax
from jax.experimental import pallas as pl
from jax.experimental.pallas import tpu as pltpu

F32 = jnp.float32
MXU = jnp.bfloat16
WIRE = jnp.bfloat16
ACT = jnp.bfloat16
NDEV = 8
AXES = ("x", "y", "c")
SSD_STATE = 128
SSD_CHUNK = 128
GRID_W = 64
EPS = 1e-6
ROW_TILE = 256
LANES = 128
ADAM_LR, ADAM_B1, ADAM_B2, ADAM_EPS, ADAM_WD, ADAM_STEP = 0.001, 0.9, 0.999, 1e-08, 0.01, 10
VMEM_CAP = 56 * 2 ** 20
MESH_ID = pl.DeviceIdType.MESH


def _pick(dim, cands):
    for c in cands:
        if dim % c == 0:
            return c
    return dim


def _nbytes(shape, dtype):
    n = 1
    for s in shape:
        n *= s
    return n * jnp.dtype(dtype).itemsize


def _vmem(nbytes):
    return int(min(VMEM_CAP, max(24 * 2 ** 20, 2 * nbytes + 8 * 2 ** 20)))


def _sigmoid(x):
    return 1.0 / (1.0 + jnp.exp(-x))


def _silu(x):
    return x * _sigmoid(x)


def _dsilu(x):
    s = _sigmoid(x)
    return s * (1.0 + x * (1.0 - s))


def _softplus(x):
    return jnp.maximum(x, 0.0) + jnp.log(1.0 + jnp.exp(-jnp.abs(x)))


def _dot(a, b, dims):
    return lax.dot_general(a.astype(MXU), b.astype(MXU), (dims, ((), ())), preferred_element_type=F32)


NN, NT, TN = ((1,), (0,)), ((1,), (1,)), ((0,), (0,))


def _split(a, parts):
    out = []
    for _ in range(parts):
        p = a.astype(MXU)
        out.append(p)
        a = a - p.astype(F32)
    return out


def _dot_lx(e, a, dims, parts=2):
    return sum(lax.dot_general(e, p, (dims, ((), ())), preferred_element_type=F32) for p in _split(a, parts))


def _dot_rx(a, e, dims, parts=2):
    return sum(lax.dot_general(p, e, (dims, ((), ())), preferred_element_type=F32) for p in _split(a, parts))


class _Comm:
    def __init__(self, gather=(), scatter=()):
        self.gather, self.scatter = list(gather), list(scatter)
        self.ng, self.n = len(self.gather), len(self.gather) + len(self.scatter)
        self.operands = self.gather + self.scatter
        self.specs = [pl.BlockSpec(memory_space=pl.ANY)] * self.n
        self.out_shape = ([jax.ShapeDtypeStruct((NDEV,) + a.shape, a.dtype) for a in self.gather]
                          + [jax.ShapeDtypeStruct(a.shape, a.dtype) for a in self.scatter])
        self.scratch = [pltpu.SemaphoreType.DMA((self.n, 7)), pltpu.SemaphoreType.DMA((self.n, 7)),
                        pltpu.SemaphoreType.DMA((self.n,))]

    def split(self, res):
        return res[:self.ng], res[self.ng:]

    def _copies(self, ins, outs, sems):
        send, recv, loc = sems
        ng, n = self.ng, self.n
        x, y, c = lax.axis_index("x"), lax.axis_index("y"), lax.axis_index("c")
        me, sib = (x, y, c), (x, y, 1 - c)
        chips = [(1 - x, y), (x, 1 - y), (1 - x, 1 - y)]

        def slot(p):
            return 4 * p[0] + 2 * p[1] + p[2]

        def rcopy(a, k, src, dst, to):
            return functools.partial(pltpu.make_async_remote_copy, src_ref=src, dst_ref=dst, send_sem=send.at[a, k],
                                     recv_sem=recv.at[a, k], device_id=to, device_id_type=MESH_ID)

        local = [functools.partial(pltpu.make_async_copy, ins[a] if a < ng else ins[a].at[slot(me)], outs[a].at[slot(me)],
                                   loc.at[a]) for a in range(n)]
        rel = [(fx, fy, fc) for fx in (0, 1) for fy in (0, 1) for fc in (0, 1)][1:]
        first, landed, passed = [], [], []
        for a in range(ng, n):
            for k, (fx, fy, fc) in enumerate(rel):
                p = (1 - x if fx else x, 1 - y if fy else y, 1 - c if fc else c)
                first.append(rcopy(a, k, ins[a].at[slot(p)], outs[a].at[slot(me)], p))
                blk = outs[a].at[slot(p)]
                landed.append(rcopy(a, k, blk, blk, me))
        for a in range(ng):
            dst = outs[a].at[slot(me)]
            first.append(rcopy(a, 0, ins[a], dst, sib))
            first += [rcopy(a, 1 + j, ins[a], dst, (*ch, c)) for j, ch in enumerate(chips)]
            blk = outs[a].at[slot(sib)]
            landed.append(rcopy(a, 0, blk, blk, me))
            for j, ch in enumerate(chips):
                blk = outs[a].at[slot((*ch, c))]
                passed.append((rcopy(a, 1 + j, blk, blk, me), rcopy(a, 4 + j, blk, blk, sib)))
                blk = outs[a].at[slot((*ch, 1 - c))]
                landed.append(rcopy(a, 4 + j, blk, blk, me))
        return local, first, passed, landed

    def start(self, ins, outs, sems):
        local, first, _, _ = self._copies(ins, outs, sems)
        for make in local + first:
            make().start()

    def finish(self, ins, outs, sems):
        local, first, passed, landed = self._copies(ins, outs, sems)
        onward = []
        for arrived, forward in passed:
            arrived().wait_recv()
            onward.append(forward())
            onward[-1].start()
        for make in landed:
            make().wait_recv()
        for make in first:
            make().wait_send()
        for cp in onward:
            cp.wait_send()
        for make in local:
            make().wait()


def _carry(body, comm, n_in, n_out, grid):
    if comm is None:
        return body
    n = comm.n

    def wrapped(*refs):
        own_in, c_in = refs[:n_in], refs[n_in:n_in + n]
        own_out, c_out = refs[n_in + n:n_in + n + n_out], refs[n_in + n + n_out:n_in + 2 * n + n_out]
        own_scr, sems = refs[n_in + 2 * n + n_out:-3], refs[-3:]
        ids = [pl.program_id(ax) for ax in range(len(grid))]
        first, last = ids[0] == 0, ids[0] == grid[0] - 1
        for ax in range(1, len(grid)):
            first, last = first & (ids[ax] == 0), last & (ids[ax] == grid[ax] - 1)

        @pl.when(first)
        def _():
            comm.start(c_in, c_out, sems)

        body(*own_in, *own_out, *own_scr)

        @pl.when(last)
        def _():
            comm.finish(c_in, c_out, sems)

    return wrapped


def _exchange(name, gather, scatter=()):
    comm = _Comm(gather, scatter)
    n = comm.n

    def body(*refs):
        ins, outs, sems = refs[:n], refs[n:2 * n], refs[2 * n:]
        comm.start(ins, outs, sems)
        comm.finish(ins, outs, sems)

    res = pl.pallas_call(body, name=name, out_shape=comm.out_shape, in_specs=comm.specs, out_specs=comm.specs,
                         scratch_shapes=comm.scratch)(*comm.operands)
    return comm.split(res)


def _hbm(a):
    return pltpu.with_memory_space_constraint(a, pltpu.HBM)


def _divs(dim, mult):
    return [dim] + [dim // parts for parts in range(2, dim // mult + 1) if dim % parts == 0 and (dim // parts) % mult == 0]


MM_VMEM_BUDGET = 40 * 2 ** 20
GRID_STEP_US = 0.35
HBM_BYTES_PER_US = 3.0e6


def _mm_tiles(m, n, k, sizes, mode, has_add):
    sa, sb, so = sizes
    sub = 16
    best = None
    for tk in _divs(k, LANES):
        for tn in _divs(n, LANES):
            for tm in _divs(m, LANES if mode == "tn" else sub):
                nk = k // tk
                out_t = tm * tn
                est = (2 * (tm * tk * sa + tk * tn * sb) + 2 * out_t * so + 2 * (tm * tk + tk * tn) + 4 * out_t
                       + (4 * out_t if nk > 1 else 0) + (8 * out_t if has_add else 0))
                if est > MM_VMEM_BUDGET:
                    continue
                steps = (m // tm) * (n // tn) * nk
                cost = steps * GRID_STEP_US + (tm * tk * sa + tk * tn * sb + out_t * so) / HBM_BYTES_PER_US
                if best is None or cost < best[0]:
                    best = (cost, tm, tn, tk, est)
    assert best is not None, (m, n, k)
    return best[1:]


def _mm(name, a, b, mode, out_dtype=F32, bias=None, add=None, comm=None, n_used=None):
    if mode == "nn":
        (m, k), (k2, n) = a.shape, b.shape
        n = n if n_used is None else n_used
    elif mode == "nt":
        (m, k), (n, k2) = a.shape, b.shape
        k2 = min(k, k2)
    else:
        (k, m), (k2, n) = a.shape, b.shape
    assert k == k2, (name, a.shape, b.shape)
    sizes = (a.dtype.itemsize, b.dtype.itemsize, jnp.dtype(out_dtype).itemsize)
    tm, tn, tk, est = _mm_tiles(m, n, k, sizes, mode, add is not None)
    nk = k // tk
    dims = {"nn": NN, "nt": NT, "tn": TN}[mode]
    a_spec = pl.BlockSpec((tk, tm), lambda i, j, kk: (kk, i)) if mode == "tn" else pl.BlockSpec((tm, tk), lambda i, j, kk: (i, kk))
    b_spec = pl.BlockSpec((tn, tk), lambda i, j, kk: (j, kk)) if mode == "nt" else pl.BlockSpec((tk, tn), lambda i, j, kk: (kk, j))
    extra, extra_specs = [], []
    if bias is not None:
        extra.append(bias)
        extra_specs.append(pl.BlockSpec((1, tn), lambda i, j, kk: (0, j)))
    if add is not None:
        extra.append(add)
        extra_specs.append(pl.BlockSpec((tm, tn), lambda i, j, kk: (i, j)))

    def finish(r, extras, o_ref):
        for e in extras:
            r = r + e[...].astype(F32)
        o_ref[...] = r.astype(o_ref.dtype)

    def body_acc(*refs):
        a_ref, b_ref = refs[:2]
        o_ref, acc = refs[-2:]
        kk = pl.program_id(2)

        @pl.when(kk == 0)
        def _():
            acc[...] = jnp.zeros_like(acc)

        acc[...] += _dot(a_ref[...], b_ref[...], dims)

        @pl.when(kk == nk - 1)
        def _():
            finish(acc[...], refs[2:-2], o_ref)

    def body_one(*refs):
        finish(_dot(refs[0][...], refs[1][...], dims), refs[2:-1], refs[-1])

    cm = comm if comm is not None else _Comm()
    grid = (m // tm, n // tn, nk)
    res = pl.pallas_call(
        _carry(body_acc if nk > 1 else body_one, comm, 2 + len(extra), 1, grid), name=name, grid=grid,
        out_shape=[pltpu.HBM((m, n), out_dtype)] + cm.out_shape,
        in_specs=[a_spec, b_spec] + extra_specs + cm.specs,
        out_specs=[pl.BlockSpec((tm, tn), lambda i, j, kk: (i, j))] + cm.specs,
        scratch_shapes=([pltpu.VMEM((tm, tn), F32)] if nk > 1 else []) + (cm.scratch if comm is not None else []),
        compiler_params=pltpu.CompilerParams(
            dimension_semantics=("parallel", "parallel", "arbitrary") if comm is None else ("arbitrary",) * 3,
            vmem_limit_bytes=int(min(VMEM_CAP, est + 12 * 2 ** 20))),
    )(*[_hbm(v) for v in (a, b, *extra)], *cm.operands)
    return res[0] if comm is None else (res[0], cm.split(res[1:]))


def _ri(arr, w=None, cb=0, ro=0, lead=None):
    return (arr, arr.shape[-1] if w is None else w, cb, ro, lead)


def _rowwise(name, fn, nrows, row_ins, bc_ins, outs, accs=(), comm=None):
    tr = min(ROW_TILE, nrows)
    assert nrows % tr == 0
    in_specs = []
    for (arr, w, cb, ro, lead) in row_ins:
        last = arr.shape[-2] // tr - 1
        if lead is None:
            in_specs.append(pl.BlockSpec((tr, w), lambda i, cb=cb, ro=ro, last=last: (jnp.clip(i + ro, 0, last), cb)))
        else:
            in_specs.append(pl.BlockSpec((None, tr, w),
                                         lambda i, cb=cb, ro=ro, lead=lead, last=last: (lead, jnp.clip(i + ro, 0, last), cb)))
    for arr in bc_ins:
        in_specs.append(pl.BlockSpec(arr.shape, lambda i, nd=arr.ndim: (0,) * nd))
    outs = [tuple(o) + (o[0], 0, 0)[len(o) - 2:] for o in outs]
    out_shape = [pltpu.HBM((nrows + ro * tr, total), dt) for _, dt, total, _, ro in outs] + [pltpu.HBM(s, F32) for s in accs]
    out_specs = ([pl.BlockSpec((tr, c), lambda i, cb=cb, ro=ro: (jnp.maximum(i + ro, 0), cb)) for c, _, _, cb, ro in outs]
                 + [pl.BlockSpec(s, lambda i: (0, 0)) for s in accs])
    nr, nb, no = len(row_ins), len(bc_ins), len(outs)

    def body(*refs):
        i = pl.program_id(0)
        rows = [r[...].astype(F32) for r in refs[:nr]]
        bcs = [r[...] for r in refs[nr:nr + nb]]
        o, a = fn(rows, bcs, i)
        for ref, val in zip(refs[nr + nb:nr + nb + no], o):
            ref[...] = val.astype(ref.dtype)
        for ref, val in zip(refs[nr + nb + no:], a):
            @pl.when(i == 0)
            def _(ref=ref, val=val):
                ref[...] = val

            @pl.when(i > 0)
            def _(ref=ref, val=val):
                ref[...] += val

    est = sum(tr * w * arr.dtype.itemsize for (arr, w, _, _, _) in row_ins) + sum(tr * o[0] * 4 for o in outs)
    cm = comm if comm is not None else _Comm()
    nout = no + len(accs)
    res = pl.pallas_call(
        _carry(body, comm, nr + nb, nout, (nrows // tr,)), name=name, grid=(nrows // tr,), out_shape=out_shape + cm.out_shape,
        in_specs=in_specs + cm.specs, out_specs=out_specs + cm.specs, scratch_shapes=cm.scratch if comm is not None else [],
        compiler_params=pltpu.CompilerParams(dimension_semantics=("arbitrary",), vmem_limit_bytes=_vmem(3 * est)),
    )(*[_hbm(r[0]) for r in row_ins], *[_hbm(v) for v in bc_ins], *cm.operands)
    if comm is None:
        return res[:no], res[no:]
    return res[:no], res[no:nout], cm.split(res[nout:])


def _colsum(v):
    return jnp.sum(v, axis=0, keepdims=True)


def _normmod_fwd(name, hc, h, g, s, sh, comm=None):
    d = h.shape[1]
    nct = 0 if hc is None else hc.shape[0] // ROW_TILE
    ins = [_ri(h)] if hc is None else [_ri(hc), _ri(h, ro=-nct)]

    def fn(rows, bcs, i):
        hh = rows[0] if hc is None else jnp.where(i < nct, rows[0], rows[1])
        g_, s_, sh_ = bcs
        s1 = jnp.where(i < nct, s_[0:1], s_[1:2])
        sh1 = jnp.where(i < nct, sh_[0:1], sh_[1:2])
        r = lax.rsqrt(jnp.mean(hh * hh, axis=-1, keepdims=True) + EPS)
        return [hh * r * g_ * (1.0 + s1) + sh1], []

    res = _rowwise(name, fn, h.shape[0] + nct * ROW_TILE, ins, [g, s, sh], [(d, MXU)], comm=comm)
    return res[0][0] if comm is None else (res[0][0], res[2])


def _normmod_bwd(name, hc, h, dxn, dres, g, s, comm=None):
    d = h.shape[1]
    nct = 0 if hc is None else hc.shape[0] // ROW_TILE
    hins = [_ri(h)] if hc is None else [_ri(hc), _ri(h, ro=-nct)]

    def fn(rows, bcs, i):
        hh = rows[0] if hc is None else jnp.where(i < nct, rows[0], rows[1])
        dx, dr = rows[-2:]
        g_, s_ = bcs
        ctx = i < nct
        s1 = jnp.where(ctx, s_[0:1], s_[1:2])
        r = lax.rsqrt(jnp.mean(hh * hh, axis=-1, keepdims=True) + EPS)
        hr = hh * r
        dy = dx * (1.0 + s1)
        u = dy * g_
        dh = r * u - hr * (r * r) * jnp.mean(u * hh, axis=-1, keepdims=True)
        dh = dh + jnp.where(ctx, 0.0, dr)

        def seg(v):
            v = _colsum(v)
            return jnp.concatenate([jnp.where(ctx, v, 0.0), jnp.where(ctx, 0.0, v)], axis=0)

        return [dh], [seg(dx), seg(dx * hr * g_), seg(dy * hr)]

    res = _rowwise(name, fn, h.shape[0] + nct * ROW_TILE, hins + [_ri(dxn), _ri(dres, ro=-nct)], [g, s],
                   [(d, F32, d, 0, -nct)], [(2, d)] * 3, comm=comm)
    (dh,), (dsh, ds, dg) = res[:2]
    return (dh, dsh, ds, dg) if comm is None else (dh, dsh, ds, dg, res[2])


def _resnorm_fwd(name, h, y, gate, g, s, sh):
    d = h.shape[1]

    def fn(rows, bcs, i):
        hh, yy = rows
        gate_, g_, s_, sh_ = bcs
        hn = hh + gate_ * yy
        r = lax.rsqrt(jnp.mean(hn * hn, axis=-1, keepdims=True) + EPS)
        return [hn, hn * r * g_ * (1.0 + s_) + sh_], []

    return _rowwise(name, fn, h.shape[0], [_ri(h), _ri(y)], [gate, g, s, sh], [(d, F32), (d, MXU)])[0]


def _gate_bwd(name, dh, y, gate):
    d = dh.shape[1]

    def fn(rows, bcs, i):
        dd, yy = rows
        dy = dd * bcs[0]
        return [dy], [_colsum(dd * yy), _colsum(dy)]

    (dy,), (dgate, dbias) = _rowwise(name, fn, dh.shape[0], [_ri(dh), _ri(y)], [gate], [(d, MXU)], [(1, d)] * 2)
    return dy, dgate, dbias


def _swiglu_fwd(name, u):
    f = u.shape[1] // 2

    def fn(rows, bcs, i):
        return [_silu(rows[0]) * rows[1]], []

    return _rowwise(name, fn, u.shape[0], [_ri(u, f, 0), _ri(u, f, 1)], [], [(f, MXU)])[0][0]


def _swiglu_bwd(name, u, dhid):
    f = u.shape[1] // 2

    def fn(rows, bcs, i):
        a, b, dd = rows
        return [jnp.concatenate([dd * b * _dsilu(a), dd * _silu(a)], axis=1)], []

    return _rowwise(name, fn, u.shape[0], [_ri(u, f, 0), _ri(u, f, 1), _ri(dhid)], [], [(2 * f, MXU)])[0][0]


def _glu_fwd(name, u):
    d = u.shape[1] // 2

    def fn(rows, bcs, i):
        return [rows[0] * _sigmoid(rows[1])], []

    return _rowwise(name, fn, u.shape[0], [_ri(u, d, 0), _ri(u, d, 1)], [], [(d, F32)])[0][0]


def _glu_bwd(name, u, dgl_lo, dgl_hi):
    d = u.shape[1] // 2

    def fn(rows, bcs, i):
        a, b = rows[:2]
        dd = jnp.concatenate(rows[2:], axis=1)
        sg = _sigmoid(b)
        du = jnp.concatenate([dd * sg, dd * a * sg * (1.0 - sg)], axis=1)
        return [du], [_colsum(du)]

    (du,), (db,) = _rowwise(name, fn, u.shape[0], [_ri(u, d, 0), _ri(u, d, 1), _ri(dgl_lo), _ri(dgl_hi)], [], [(2 * d, MXU)],
                            [(1, 2 * d)])
    return du, db


def _ln_silu_fwd(name, v_lo, v_hi, g, b):
    d = 2 * v_lo.shape[1]

    def fn(rows, bcs, i):
        vv = jnp.concatenate(rows, axis=1)
        mu = jnp.mean(vv, axis=-1, keepdims=True)
        xc = vv - mu
        rs = lax.rsqrt(jnp.mean(xc * xc, axis=-1, keepdims=True) + EPS)
        return [_silu(xc * rs * bcs[0] + bcs[1])], []

    return _rowwise(name, fn, v_lo.shape[0], [_ri(v_lo), _ri(v_hi)], [g, b], [(d, MXU)])[0][0]


def _ln_silu_bwd(name, v_lo, v_hi, ds, g, b):
    ch = v_lo.shape[1]

    def fn(rows, bcs, i):
        vv, dd = jnp.concatenate(rows[:2], axis=1), rows[2]
        mu = jnp.mean(vv, axis=-1, keepdims=True)
        xc = vv - mu
        rs = lax.rsqrt(jnp.mean(xc * xc, axis=-1, keepdims=True) + EPS)
        xh = xc * rs
        dln = dd * _dsilu(xh * bcs[0] + bcs[1])
        dxh = dln * bcs[0]
        dv = rs * (dxh - jnp.mean(dxh, axis=-1, keepdims=True) - xh * jnp.mean(dxh * xh, axis=-1, keepdims=True))
        return [dv[:, :ch], dv[:, ch:]], [_colsum(dln * xh), _colsum(dln)]

    (dv_lo, dv_hi), (dg, db) = _rowwise(name, fn, v_lo.shape[0], [_ri(v_lo), _ri(v_hi), _ri(ds)], [g, b],
                                        [(ch, F32), (ch, F32)], [(1, 2 * ch)] * 2)
    return dv_lo, dv_hi, dg, db


def _final_loss(name, h, f, target, gate, gf):
    d = h.shape[1]

    def fn(rows, bcs, i):
        hh, ff, tg = rows
        gate_, g_ = bcs
        hn = hh + gate_ * ff
        r = lax.rsqrt(jnp.mean(hn * hn, axis=-1, keepdims=True) + EPS)
        hr = hn * r
        err = hr * g_ - tg
        dout = err * (1.0 / d)
        u = dout * g_
        dh = r * u - hr * (r * r) * jnp.mean(u * hn, axis=-1, keepdims=True)
        sq = jnp.sum(_colsum(err * err), axis=1, keepdims=True)
        return [dh], [jnp.broadcast_to(sq, (1, LANES)), _colsum(dout * hr)]

    (dh,), (sq, dgf) = _rowwise(name, fn, h.shape[0], [_ri(h), _ri(f), _ri(target)], [gate, gf], [(d, F32)], [(1, LANES), (1, d)])
    return dh, sq, dgf


GAP = 8


def _gapped(ref_rows, buf, tc, tt):
    cb = buf.shape[1]
    zero = jnp.zeros((GAP, cb), F32)
    buf[0:GAP, :] = zero
    buf[GAP + tc:2 * GAP + tc, :] = zero
    buf[2 * GAP + tt:, :] = zero
    buf[GAP:GAP + tc, :] = ref_rows[0:tc]
    buf[2 * GAP + tc:2 * GAP + tt, :] = ref_rows[tc:tt]
    return buf[...]


def _ungapped(v, tc, tt):
    return jnp.concatenate([v[GAP:GAP + tc], v[2 * GAP + tc:2 * GAP + tt]], axis=0)


def _shift_rows(x, o):
    return x if o == 0 else pltpu.roll(x, (-o) % x.shape[0], 0)


def _ssd_conv_fwd(name, zx, w, b, di, tc, comm=None):
    tt, kc, cd = zx.shape[0], w.shape[0], w.shape[1]
    cb = _pick(cd, (LANES,))
    off = di // cb
    assert kc // 2 < GAP and tc % GAP == 0 and tt % GAP == 0

    def body(x_ref, w_ref, b_ref, o_ref, pre_ref, xp):
        x = _gapped(x_ref[...].astype(F32), xp, tc, tt)
        acc = jnp.broadcast_to(b_ref[...], x.shape)
        for k in range(kc):
            acc = acc + w_ref[k:k + 1, :] * _shift_rows(x, k - kc // 2)
        acc = _ungapped(acc, tc, tt)
        pre_ref[...] = acc.astype(pre_ref.dtype)
        o_ref[...] = _silu(acc).astype(o_ref.dtype)

    cm = comm if comm is not None else _Comm()
    blk = pl.BlockSpec((tt, cb), lambda j: (0, j))
    res = pl.pallas_call(
        _carry(body, comm, 3, 2, (cd // cb,)), name=name, grid=(cd // cb,),
        out_shape=[pltpu.HBM((tt, cd), ACT), pltpu.HBM((tt, cd), ACT)] + cm.out_shape,
        in_specs=[pl.BlockSpec((tt, cb), lambda j: (0, j + off)), pl.BlockSpec((kc, cb), lambda j: (0, j)),
                  pl.BlockSpec((1, cb), lambda j: (0, j))] + cm.specs,
        out_specs=[blk, blk] + cm.specs,
        scratch_shapes=[pltpu.VMEM((tt + 3 * GAP, cb), F32)] + (cm.scratch if comm is not None else []),
        compiler_params=pltpu.CompilerParams(dimension_semantics=("arbitrary",), vmem_limit_bytes=_vmem(5 * tt * cb * 4)),
    )(_hbm(zx), _hbm(w), _hbm(b), *cm.operands)
    return res[0], res[1], cm.split(res[2:])


def _ssd_conv_bwd(name, zx, dact2, w, pre, dzx, di, tc, comm=None):
    tt, kc, cd = zx.shape[0], w.shape[0], w.shape[1]
    cb = _pick(cd, (LANES,))
    off = di // cb

    def body(x_ref, d0_ref, d1_ref, w_ref, pre_ref, _, dx_ref, dw_ref, db_ref, xp, dp):
        x = _gapped(x_ref[...].astype(F32), xp, tc, tt)
        dpre = (d0_ref[...].astype(F32) + d1_ref[...].astype(F32)) * _dsilu(pre_ref[...].astype(F32))
        db_ref[...] = _colsum(dpre)
        dpre = _gapped(dpre, dp, tc, tt)
        dx = jnp.zeros_like(x)
        for k in range(kc):
            sh = _shift_rows(dpre, -(k - kc // 2))
            dx = dx + w_ref[k:k + 1, :] * sh
            dw_ref[k:k + 1, :] = _colsum(sh * x)
        dx_ref[...] = _ungapped(dx, tc, tt).astype(dx_ref.dtype)

    cm = comm if comm is not None else _Comm()
    res = pl.pallas_call(
        _carry(body, comm, 6, 3, (cd // cb,)), name=name, grid=(cd // cb,),
        out_shape=[pltpu.HBM(dzx.shape, dzx.dtype), pltpu.HBM((kc, cd), F32), pltpu.HBM((1, cd), F32)] + cm.out_shape,
        in_specs=[pl.BlockSpec((tt, cb), lambda j: (0, j + off)), pl.BlockSpec((None, tt, cb), lambda j: (0, 0, j)),
                  pl.BlockSpec((None, tt, cb), lambda j: (1, 0, j)), pl.BlockSpec((kc, cb), lambda j: (0, j)),
                  pl.BlockSpec((tt, cb), lambda j: (0, j)), pl.BlockSpec(memory_space=pl.ANY)] + cm.specs,
        out_specs=[pl.BlockSpec((tt, cb), lambda j: (0, j + off)), pl.BlockSpec((kc, cb), lambda j: (0, j)),
                   pl.BlockSpec((1, cb), lambda j: (0, j))] + cm.specs,
        input_output_aliases={5: 0},
        scratch_shapes=[pltpu.VMEM((tt + 3 * GAP, cb), F32)] * 2 + (cm.scratch if comm is not None else []),
        compiler_params=pltpu.CompilerParams(dimension_semantics=("arbitrary",), vmem_limit_bytes=_vmem(10 * tt * cb * 4)),
    )(_hbm(zx), _hbm(dact2), _hbm(dact2), _hbm(w), _hbm(pre), _hbm(dzx), *cm.operands)
    return res[0], res[1], res[2], cm.split(res[3:])


def _strided_conv(name, x, w, b, stride, x_col0=0):
    t, ch = x.shape[0], w.shape[1]
    kk = w.shape[0]
    pad = (kk // 2) * stride
    cb = _pick(ch, (LANES,))
    has_b = b is not None

    def body(*refs):
        x_ref, w_ref = refs[:2]
        o_ref, xp = refs[-2:]
        xp[0:pad, :] = jnp.zeros((pad, cb), F32)
        xp[pad + t:, :] = jnp.zeros((pad, cb), F32)
        xp[pad:pad + t, :] = x_ref[...]
        acc = jnp.broadcast_to(refs[2][...], (t, cb)) if has_b else jnp.zeros((t, cb), F32)
        for k in range(kk):
            acc = acc + w_ref[k:k + 1, :] * xp[k * stride:k * stride + t, :]
        o_ref[...] = acc

    xoff = x_col0 // cb
    ins, specs = [x, w], [pl.BlockSpec((t, cb), lambda j: (0, j + xoff)), pl.BlockSpec((kk, cb), lambda j: (0, j))]
    if has_b:
        ins.append(b)
        specs.append(pl.BlockSpec((1, cb), lambda j: (0, j)))
    return pl.pallas_call(
        body, name=name, grid=(ch // cb,), out_shape=pltpu.HBM((t, ch), F32), in_specs=specs,
        out_specs=pl.BlockSpec((t, cb), lambda j: (0, j)), scratch_shapes=[pltpu.VMEM((t + 2 * pad, cb), F32)],
        compiler_params=pltpu.CompilerParams(dimension_semantics=("parallel",), vmem_limit_bytes=_vmem(6 * t * cb * 4)),
    )(*[_hbm(v) for v in ins])


def _strided_conv_dw(name, x, dv, kk, stride, x_col0=0):
    t, ch = dv.shape
    pad = (kk // 2) * stride
    cb = _pick(ch, (LANES,))

    def body(x_ref, d_ref, dw_ref, db_ref, xp):
        xp[0:pad, :] = jnp.zeros((pad, cb), F32)
        xp[pad + t:, :] = jnp.zeros((pad, cb), F32)
        xp[pad:pad + t, :] = x_ref[...]
        d = d_ref[...]
        for k in range(kk):
            dw_ref[k:k + 1, :] = _colsum(d * xp[k * stride:k * stride + t, :])
        db_ref[...] = _colsum(d)

    blk = pl.BlockSpec((t, cb), lambda j: (0, j))
    xoff = x_col0 // cb
    return pl.pallas_call(
        body, name=name, grid=(ch // cb,), out_shape=[pltpu.HBM((kk, ch), F32), pltpu.HBM((1, ch), F32)],
        in_specs=[pl.BlockSpec((t, cb), lambda j: (0, j + xoff)), blk], out_specs=[pl.BlockSpec((kk, cb), lambda j: (0, j)), pl.BlockSpec((1, cb), lambda j: (0, j))],
        scratch_shapes=[pltpu.VMEM((t + 2 * pad, cb), F32)],
        compiler_params=pltpu.CompilerParams(dimension_semantics=("parallel",), vmem_limit_bytes=_vmem(6 * t * cb * 4)),
    )(_hbm(x), _hbm(dv))


def _grid_t(a, n1, n2):
    return a.reshape(n1, n2, a.shape[-1]).swapaxes(0, 1).reshape(n1 * n2, a.shape[-1])


def _chunk_order(d, i, ncc, nc):
    back = jnp.where(i < ncc, ncc - 1 - i, nc - 1 - (i - ncc))
    return jnp.where(d == 0, i, back)


def _ssd_chunk_setup(d, dt_raw, bias, a_log, q, h, di):
    p = di // h
    dt = _softplus(dt_raw + bias)
    a_neg = -jnp.exp(a_log)
    delta = dt * a_neg
    r = lax.broadcasted_iota(jnp.int32, (q, q), 0)
    c = lax.broadcasted_iota(jnp.int32, (q, q), 1)
    sgn = 1 - 2 * d
    mask = (r - c) * sgn >= 0
    mask_t = (c - r) * sgn >= 0
    a = _dot_lx(mask.astype(MXU), delta, NN, parts=3)
    tot = _colsum(delta)
    ea, dte, cd = jnp.exp(a), jnp.exp(tot - a), jnp.exp(tot)
    hh = lax.broadcasted_iota(jnp.int32, (h, di), 0)
    cc = lax.broadcasted_iota(jnp.int32, (h, di), 1)
    e = (cc // p == hh).astype(MXU)
    ex = _dot_rx(jnp.concatenate([dt, ea, dte, jnp.broadcast_to(cd, (8, h))], axis=0), e, NN)
    eye = (lax.broadcasted_iota(jnp.int32, (h, h), 0) == lax.broadcasted_iota(jnp.int32, (h, h), 1)).astype(MXU)
    a_t = _dot_lx(eye, a, NT, parts=3)
    return dict(dt=dt, a_neg=a_neg, a=a, a_t=a_t, mask=mask, mask_t=mask_t, e=e,
                dt_e=ex[0:q], ea_e=ex[q:2 * q], dte_e=ex[2 * q:3 * q], cd_e=ex[3 * q:3 * q + 1])


def _pick_heads(r, q, hpg, p):
    lane = lax.broadcasted_iota(jnp.int32, (q, hpg * p), 1) // p
    out = jnp.zeros((q, hpg * p), F32)
    for j in range(hpg):
        out = out + jnp.where(lane == j, r[j * q:(j + 1) * q], 0.0)
    return out


def _ssd_fwd(name, xbc, dt2, bias2, alog2, di, tc, comm=None):
    tt, cd = xbc.shape
    h = dt2.shape[-1]
    q, n = SSD_CHUNK, SSD_STATE
    gn = (cd - di) // 2
    g = gn // n
    hpg, p = h // g, di // h
    gp = hpg * p
    nc, ncc = tt // q, tc // q
    assert di % gn == 0

    def body(x_ref, b_ref, c_ref, dt_ref, bias_ref, alog_ref, y_ref, hp_ref, ht):
        d, i = pl.program_id(0), pl.program_id(1)

        @pl.when(i == 0)
        def _():
            ht[...] = jnp.zeros_like(ht)

        s = _ssd_chunk_setup(d, dt_ref[...], bias_ref[...], alog_ref[...], q, h, di)
        xd = x_ref[...].astype(F32) * s["dt_e"]
        hp_ref[...] = ht[...].astype(hp_ref.dtype)
        for gi in range(g):
            bg, cg = b_ref[:, gi * n:(gi + 1) * n].astype(MXU), c_ref[:, gi * n:(gi + 1) * n].astype(MXU)
            sl = slice(gi * gp, (gi + 1) * gp)
            sc = _dot(cg, bg, NT)
            ms = []
            for j in range(hpg):
                hd = gi * hpg + j
                seg = s["a"][:, hd:hd + 1] - s["a_t"][hd:hd + 1, :]
                ms.append(sc * jnp.exp(jnp.where(s["mask"], seg, -jnp.inf)))
            xdg = xd[:, sl]
            ydiag = _pick_heads(_dot(jnp.concatenate(ms, axis=0), xdg, NN), q, hpg, p)
            htg = ht[:, sl]
            y_ref[:, sl] = ydiag + _dot(cg, htg, NN) * s["ea_e"][:, sl]
            ht[:, sl] = s["cd_e"][:, sl] * htg + _dot(bg, xdg * s["dte_e"][:, sl], TN)

    def cidx(d, i):
        return _chunk_order(d, i, ncc, nc)

    cm = comm if comm is not None else _Comm()
    res = pl.pallas_call(
        _carry(body, comm, 6, 2, (2, nc)), name=name, grid=(2, nc),
        out_shape=[pltpu.HBM((2, tt, di), F32), pltpu.HBM((2, nc, n, di), ACT)] + cm.out_shape,
        in_specs=[pl.BlockSpec((q, di), lambda d, i: (cidx(d, i), 0)),
                  pl.BlockSpec((q, gn), lambda d, i: (cidx(d, i), di // gn)),
                  pl.BlockSpec((q, gn), lambda d, i: (cidx(d, i), di // gn + 1)),
                  pl.BlockSpec((None, q, h), lambda d, i: (d, cidx(d, i), 0)),
                  pl.BlockSpec((None, 1, h), lambda d, i: (d, 0, 0)),
                  pl.BlockSpec((None, 1, h), lambda d, i: (d, 0, 0))] + cm.specs,
        out_specs=[pl.BlockSpec((None, q, di), lambda d, i: (d, cidx(d, i), 0)),
                   pl.BlockSpec((None, None, n, di), lambda d, i: (d, cidx(d, i), 0, 0))] + cm.specs,
        scratch_shapes=[pltpu.VMEM((n, di), F32)] + (cm.scratch if comm is not None else []),
        compiler_params=pltpu.CompilerParams(dimension_semantics=("arbitrary", "arbitrary"), vmem_limit_bytes=_vmem(16 * q * di * 4)),
    )(*[_hbm(v) for v in (xbc, xbc, xbc, dt2, bias2, alog2)], *cm.operands)
    return res[0], res[1], cm.split(res[2:])


def _ssd_bwd(name, xbc, dt2, bias2, alog2, dy, hp2, dskip_e, di, tc, comm=None):
    tt, cd = xbc.shape
    h = dt2.shape[-1]
    q, n = SSD_CHUNK, SSD_STATE
    gn = (cd - di) // 2
    g = gn // n
    hpg, p = h // g, di // h
    gp = hpg * p
    nc, ncc = tt // q, tc // q

    def body(x_ref, b_ref, c_ref, dt_ref, bias_ref, alog_ref, dy_ref, hp_ref, dsk_ref,
             dxbc_ref, ddt_ref, dalog_ref, dbias_ref, dht, dxd, off):
        d, i = pl.program_id(0), pl.program_id(1)

        @pl.when(i == 0)
        def _():
            dht[...] = jnp.zeros_like(dht)
            dalog_ref[...] = jnp.zeros_like(dalog_ref)
            dbias_ref[...] = jnp.zeros_like(dbias_ref)

        s = _ssd_chunk_setup(d, dt_ref[...], bias_ref[...], alog_ref[...], q, h, di)
        x, dyc = x_ref[...].astype(F32), dy_ref[...]
        xd = x * s["dt_e"]
        dyea = dyc * s["ea_e"]
        xdte = xd * s["dte_e"]
        lane = lax.broadcasted_iota(jnp.int32, (q, gp), 1) // p
        lane_h = lax.broadcasted_iota(jnp.int32, (q, h), 1)
        da_d = jnp.zeros((q, h), F32)
        last_e = []
        for gi in range(g):
            bg, cg = b_ref[:, gi * n:(gi + 1) * n].astype(MXU), c_ref[:, gi * n:(gi + 1) * n].astype(MXU)
            sl = slice(gi * gp, (gi + 1) * gp)
            sc, sct = _dot(cg, bg, NT), _dot(bg, cg, NT)
            dyg, xdg = dyc[:, sl], xd[:, sl]
            htg, dhtg = hp_ref[:, sl].astype(F32), dht[:, sl]
            dystack = jnp.concatenate([jnp.where(lane == j, dyg, 0.0) for j in range(hpg)], axis=0)
            xdstack = jnp.concatenate([jnp.where(lane == j, xdg, 0.0) for j in range(hpg)], axis=0)
            gs = _dot(dystack, xdg, NT)
            gst = _dot(xdstack, dyg, NT)
            ds = jnp.zeros((q, q), F32)
            mts = []
            for j in range(hpg):
                hd = gi * hpg + j
                col, rw = s["a"][:, hd:hd + 1], s["a_t"][hd:hd + 1, :]
                gl = gs[j * q:(j + 1) * q] * jnp.exp(jnp.where(s["mask"], col - rw, -jnp.inf))
                ds = ds + gl
                mt = sct * jnp.exp(jnp.where(s["mask_t"], rw - col, -jnp.inf))
                mts.append(mt)
                da_j = jnp.sum(gl * sc, axis=1, keepdims=True) - jnp.sum(gst[j * q:(j + 1) * q] * mt, axis=1, keepdims=True)
                da_d = da_d + jnp.where(lane_h == hd, da_j, 0.0)
            dxd_diag = _pick_heads(_dot(jnp.concatenate(mts, axis=0), dyg, NN), q, hpg, p)
            z = _dot(bg, dhtg, NN) * s["dte_e"][:, sl]
            yoff = _dot(cg, htg, NN) * s["ea_e"][:, sl]
            off[:, sl] = dyg * yoff - xdg * z
            dxd[:, sl] = dxd_diag + z
            dxbc_ref[:, di + gi * n:di + (gi + 1) * n] = (_dot(ds, cg, TN) + _dot(xdte[:, sl], dhtg, NT)).astype(dxbc_ref.dtype)
            dxbc_ref[:, di + gn + gi * n:di + gn + (gi + 1) * n] = (_dot(ds, bg, NN)
                                                                    + _dot(dyea[:, sl], htg, NT)).astype(dxbc_ref.dtype)
            last_e.append(s["cd_e"][:, sl] * _colsum(dhtg * htg) + _colsum(xdg * z))
            dht[:, sl] = s["cd_e"][:, sl] * dhtg + _dot(cg, dyea[:, sl], TN)
        dxd_all = dxd[...]
        last = jnp.concatenate(last_e, axis=1)
        da = da_d + _dot_rx(off[...], s["e"], NT)
        last_h = _dot_rx(jnp.broadcast_to(last, (8, di)), s["e"], NT)[0:1]
        ddelta = _dot_lx(s["mask_t"].astype(MXU), da, NN, parts=3) + last_h
        ddt = ddelta * s["a_neg"] + _dot_rx(dxd_all * x, s["e"], NT)
        ddt_raw = ddt * _sigmoid(dt_ref[...] + bias_ref[...])
        ddt_ref[...] = ddt_raw
        dalog_ref[...] += _colsum(ddelta * s["dt"]) * s["a_neg"]
        dbias_ref[...] += _colsum(ddt_raw)
        dxbc_ref[:, 0:di] = (dxd_all * s["dt_e"] + jnp.where(d == 0, dyc * dsk_ref[...], 0.0)).astype(dxbc_ref.dtype)

    def cidx(d, i):
        return _chunk_order(d, nc - 1 - i, ncc, nc)

    cm = comm if comm is not None else _Comm()
    res = pl.pallas_call(
        _carry(body, comm, 9, 4, (2, nc)), name=name, grid=(2, nc),
        out_shape=[pltpu.HBM((2, tt, cd), ACT), pltpu.HBM((2, tt, h), F32),
                   pltpu.HBM((2, 1, h), F32), pltpu.HBM((2, 1, h), F32)] + cm.out_shape,
        in_specs=[pl.BlockSpec((q, di), lambda d, i: (cidx(d, i), 0)),
                  pl.BlockSpec((q, gn), lambda d, i: (cidx(d, i), di // gn)),
                  pl.BlockSpec((q, gn), lambda d, i: (cidx(d, i), di // gn + 1)),
                  pl.BlockSpec((None, q, h), lambda d, i: (d, cidx(d, i), 0)),
                  pl.BlockSpec((None, 1, h), lambda d, i: (d, 0, 0)),
                  pl.BlockSpec((None, 1, h), lambda d, i: (d, 0, 0)),
                  pl.BlockSpec((q, di), lambda d, i: (cidx(d, i), 0)),
                  pl.BlockSpec((None, None, n, di), lambda d, i: (d, cidx(d, i), 0, 0)),
                  pl.BlockSpec((1, di), lambda d, i: (0, 0))] + cm.specs,
        out_specs=[pl.BlockSpec((None, q, cd), lambda d, i: (d, cidx(d, i), 0)),
                   pl.BlockSpec((None, q, h), lambda d, i: (d, cidx(d, i), 0)),
                   pl.BlockSpec((None, 1, h), lambda d, i: (d, 0, 0)),
                   pl.BlockSpec((None, 1, h), lambda d, i: (d, 0, 0))] + cm.specs,
        scratch_shapes=[pltpu.VMEM((n, di), F32), pltpu.VMEM((q, di), F32), pltpu.VMEM((q, di), F32)]
        + (cm.scratch if comm is not None else []),
        compiler_params=pltpu.CompilerParams(dimension_semantics=("arbitrary", "arbitrary"), vmem_limit_bytes=_vmem(24 * q * di * 4)),
    )(*[_hbm(v) for v in (xbc, xbc, xbc, dt2, bias2, alog2, dy, hp2, dskip_e)], *cm.operands)
    return res[0], res[1], res[2], res[3], cm.split(res[4:])


def _ssd_gate_fwd(name, y2, xbc, zx, dskip_e, norm_w, di, nct, t):
    def fn(rows, bcs, i):
        yf, yb, xs, z = rows
        zg = (yf + yb + bcs[0] * xs) * _silu(z)
        rn = lax.rsqrt(jnp.mean(zg * zg, axis=-1, keepdims=True) + EPS)
        return [zg * rn * bcs[1]], []

    ins = [_ri(y2, lead=0, ro=nct), _ri(y2, lead=1, ro=nct), _ri(xbc, di, 0, ro=nct), _ri(zx, di, 0, ro=nct)]
    return _rowwise(name, fn, t, ins, [dskip_e, norm_w], [(di, MXU)])[0][0]


def _ssd_gate_bwd(name, dyn, y2, xbc, zx, dskip_e, norm_w, di, nct, tt):
    def fn(rows, bcs, i):
        dn, yf, yb, xs, z = rows
        lat = i >= nct
        ytot = yf + yb + bcs[0] * xs
        sz = _silu(z)
        zg = ytot * sz
        rn = lax.rsqrt(jnp.mean(zg * zg, axis=-1, keepdims=True) + EPS)
        u = dn * bcs[1]
        dzg = rn * u - zg * (rn * rn * rn) * jnp.mean(u * zg, axis=-1, keepdims=True)
        dy = jnp.where(lat, dzg * sz, 0.0)
        dz = jnp.where(lat, dzg * ytot * _dsilu(z), 0.0)
        return [dy, dz], [jnp.where(lat, _colsum(dn * zg * rn), 0.0), jnp.where(lat, _colsum(dy * xs), 0.0)]

    ins = [_ri(dyn, ro=-nct), _ri(y2, lead=0), _ri(y2, lead=1), _ri(xbc, di, 0), _ri(zx, di, 0)]
    (dy, dzx), (dnw, ddsk) = _rowwise(name, fn, tt, ins, [dskip_e, norm_w], [(di, F32), (di, MXU, zx.shape[1], 0)], [(1, di)] * 2)
    return dy, dzx, dnw, ddsk


def _ada_fwd(name, cs, w, b):
    nl, d, c = w.shape
    r = cs.shape[0]

    def body(cs_ref, w_ref, b_ref, o_ref):
        o_ref[...] = _dot(_silu(cs_ref[...]), w_ref[...], NN) + b_ref[...]

    return pl.pallas_call(
        body, name=name, grid=(nl,), out_shape=pltpu.HBM((nl, r, c), F32),
        in_specs=[pl.BlockSpec((r, d), lambda l: (0, 0)), pl.BlockSpec((None, d, c), lambda l: (l, 0, 0)),
                  pl.BlockSpec((None, 1, c), lambda l: (l, 0, 0))],
        out_specs=pl.BlockSpec((None, r, c), lambda l: (l, 0, 0)),
        compiler_params=pltpu.CompilerParams(dimension_semantics=("parallel",), vmem_limit_bytes=_vmem(2 * d * c * 4)),
    )(_hbm(cs), _hbm(w), _hbm(b))


def _ada_bwd(name, cs, w, dmod):
    nl, d, c = w.shape
    r = cs.shape[0]

    def body(cs_ref, w_ref, dm_ref, dw_ref, dsc_ref):
        dm = dm_ref[...]
        dw_ref[...] = _dot(_silu(cs_ref[...]), dm, TN)

        @pl.when(pl.program_id(0) == 0)
        def _():
            dctx = jnp.broadcast_to(_colsum(dm[r // 2:]), (8, c))
            dsc_ref[...] = _dot(dctx, w_ref[...], NT)[0:1]

    return pl.pallas_call(
        body, name=name, grid=(nl,), out_shape=[pltpu.HBM((nl, d, c), F32), pltpu.HBM((1, d), F32)],
        in_specs=[pl.BlockSpec((r, d), lambda l: (0, 0)), pl.BlockSpec((None, d, c), lambda l: (l, 0, 0)),
                  pl.BlockSpec((None, r, c), lambda l: (l, 0, 0))],
        out_specs=[pl.BlockSpec((None, d, c), lambda l: (l, 0, 0)), pl.BlockSpec((1, d), lambda l: (0, 0))],
        compiler_params=pltpu.CompilerParams(dimension_semantics=("arbitrary",), vmem_limit_bytes=_vmem(4 * d * c * 4)),
    )(_hbm(cs), _hbm(w), _hbm(dmod))


def _adam_math(w, g, m, v):
    m = ADAM_B1 * m + (1.0 - ADAM_B1) * g
    v = ADAM_B2 * v + (1.0 - ADAM_B2) * (g * g)
    m_hat = m / (1.0 - ADAM_B1 ** ADAM_STEP)
    v_hat = v / (1.0 - ADAM_B2 ** ADAM_STEP)
    delta = -ADAM_LR * (m_hat / (jnp.sqrt(v_hat) + ADAM_EPS) + ADAM_WD * w)
    return delta, m, v


def _adam(name, slots, w, m, v, comm=None):
    segs = list(slots) if isinstance(slots, (list, tuple)) else [slots]
    nseg = len(segs)
    ns, c = segs[0].shape[0], segs[0].shape[2]
    r = sum(s.shape[1] for s in segs)
    tr = _pick(min(s.shape[1] for s in segs), (256, 128, 64, 32, 16, 8))
    starts = [sum(s.shape[1] for s in segs[:k]) // tr for k in range(nseg)]

    def body(*refs):
        s_refs, (w_ref, m_ref, v_ref), (g_ref, d_ref, mo_ref, vo_ref) = refs[:nseg], refs[nseg:nseg + 3], refs[nseg + 3:]
        i = pl.program_id(0)

        def total(s_ref):
            g = s_ref[0].astype(F32)
            for k in range(1, ns):
                g = g + s_ref[k].astype(F32)
            return g

        g = total(s_refs[0])
        for k in range(1, nseg):
            g = jnp.where(i >= starts[k], total(s_refs[k]), g)
        d, mn, vn = _adam_math(w_ref[...], g, m_ref[...], v_ref[...])
        g_ref[...], d_ref[...], mo_ref[...], vo_ref[...] = g, d, mn, vn

    blk = pl.BlockSpec((tr, c), lambda i: (i, 0))
    seg_specs = [pl.BlockSpec((ns, tr, c), lambda i, st=starts[k], nt=segs[k].shape[1] // tr: (0, jnp.clip(i - st, 0, nt - 1), 0))
                 for k in range(nseg)]
    cm = comm if comm is not None else _Comm()
    res = pl.pallas_call(
        _carry(body, comm, nseg + 3, 4, (r // tr,)), name=name, grid=(r // tr,),
        out_shape=[pltpu.HBM((r, c), F32)] * 4 + cm.out_shape,
        in_specs=seg_specs + [blk, blk, blk] + cm.specs, out_specs=[blk] * 4 + cm.specs,
        scratch_shapes=cm.scratch if comm is not None else [],
        compiler_params=pltpu.CompilerParams(dimension_semantics=("arbitrary",), vmem_limit_bytes=_vmem(16 * nseg * tr * c * 4)),
    )(*[_hbm(s) for s in segs], _hbm(w), _hbm(m), _hbm(v), *cm.operands)
    return res[:4] if comm is None else (res[:4], cm.split(res[4:]))


def _adam_small(name, slots, ws, ms, vs, scale=None):
    k = len(slots)

    def body(*refs):
        s_refs, w_refs, m_refs, v_refs = refs[:k], refs[k:2 * k], refs[2 * k:3 * k], refs[3 * k:4 * k]
        sc_ref = refs[4 * k] if scale is not None else None
        outs = refs[4 * k + (scale is not None):]
        for a in range(k):
            g = s_refs[a][0]
            for j in range(1, NDEV):
                g = g + s_refs[a][j]
            if scale is not None and a == scale[0]:
                g = g * _dsilu(sc_ref[...])
            d, mn, vn = _adam_math(w_refs[a][...], g, m_refs[a][...], v_refs[a][...])
            outs[a][...], outs[k + a][...], outs[2 * k + a][...], outs[3 * k + a][...] = g, d, mn, vn

    shapes = [pltpu.HBM(w.shape, F32) for w in ws]
    extra = [scale[1]] if scale is not None else []
    ins = [*slots, *ws, *ms, *vs, *extra]

    def whole(shape):
        return pl.BlockSpec(shape, lambda i, nd=len(shape): (0,) * nd)

    res = pl.pallas_call(body, name=name, grid=(1,), out_shape=shapes * 4, in_specs=[whole(v.shape) for v in ins],
                         out_specs=[whole(s.shape) for s in shapes * 4])(*[_hbm(v) for v in ins])
    return res[:k], res[k:2 * k], res[2 * k:3 * k], res[3 * k:]


def _unshard_cols(g):
    g = jnp.moveaxis(g, 0, -2)
    return g.reshape(g.shape[:-2] + (g.shape[-2] * g.shape[-1],))


def _shard_cols(a):
    a = a.reshape(a.shape[:-1] + (NDEV, a.shape[-1] // NDEV))
    return jnp.moveaxis(a, -2, 0)


def _unshard_rows(g):
    g = jnp.moveaxis(g, 0, -3)
    return g.reshape(g.shape[:-3] + (g.shape[-3] * g.shape[-2], g.shape[-1]))


def _shard_rows(a):
    a = a.reshape(a.shape[:-2] + (NDEV, a.shape[-2] // NDEV, a.shape[-1]))
    return jnp.moveaxis(a, -3, 0)


def _flat2(a):
    return a.reshape((-1, a.shape[-1]))


def kernel(x, c, ctx, c_ctx, ada_w, ada_b, norm_mix_g, norm_ffn_g, final_norm_g, ssd_w_in, ssd_conv_w, ssd_conv_b, ssd_dt_bias_f, ssd_dt_bias_b, ssd_a_log_f, ssd_a_log_b, ssd_d_skip, ssd_norm_w, ssd_w_out, conf_w_pw1, conf_b_pw1, conf_dw_w, conf_dw_b, conf_ln_g, conf_ln_b, conf_w_pw2, conf_b_pw2, ffn_w_in, ffn_w_out, loss_target, m_c_ctx, m_ada_w, m_ada_b, m_norm_mix_g, m_norm_ffn_g, m_final_norm_g, m_ssd_w_in, m_ssd_conv_w, m_ssd_conv_b, m_ssd_dt_bias_f, m_ssd_dt_bias_b, m_ssd_a_log_f, m_ssd_a_log_b, m_ssd_d_skip, m_ssd_norm_w, m_ssd_w_out, m_conf_w_pw1, m_conf_b_pw1, m_conf_dw_w, m_conf_dw_b, m_conf_ln_g, m_conf_ln_b, m_conf_w_pw2, m_conf_b_pw2, m_ffn_w_in, m_ffn_w_out, v_c_ctx, v_ada_w, v_ada_b, v_norm_mix_g, v_norm_ffn_g, v_final_norm_g, v_ssd_w_in, v_ssd_conv_w, v_ssd_conv_b, v_ssd_dt_bias_f, v_ssd_dt_bias_b, v_ssd_a_log_f, v_ssd_a_log_b, v_ssd_d_skip, v_ssd_norm_w, v_ssd_w_out, v_conf_w_pw1, v_conf_b_pw1, v_conf_dw_w, v_conf_dw_b, v_conf_ln_g, v_conf_ln_b, v_conf_w_pw2, v_conf_b_pw2, v_ffn_w_in, v_ffn_w_out):
    args = dict(locals())
    names = ['c_ctx', 'ada_w', 'ada_b', 'norm_mix_g', 'norm_ffn_g', 'final_norm_g', 'ssd_w_in', 'ssd_conv_w', 'ssd_conv_b',
             'ssd_dt_bias_f', 'ssd_dt_bias_b', 'ssd_a_log_f', 'ssd_a_log_b', 'ssd_d_skip', 'ssd_norm_w', 'ssd_w_out',
             'conf_w_pw1', 'conf_b_pw1', 'conf_dw_w', 'conf_dw_b', 'conf_ln_g', 'conf_ln_b', 'conf_w_pw2', 'conf_b_pw2',
             'ffn_w_in', 'ffn_w_out']
    me = 4 * lax.axis_index("x") + 2 * lax.axis_index("y") + lax.axis_index("c")
    t, d = x.shape[1], x.shape[2]
    tc = ctx.shape[1]
    tt = tc + t
    nct = tc // ROW_TILE
    assert tc % ROW_TILE == 0 and t % ROW_TILE == 0
    h = ssd_dt_bias_f.shape[-1]
    di = ssd_norm_w.shape[-1]
    cdim = ssd_conv_b.shape[-1]
    kc = ssd_conv_w.shape[1]
    ck = conf_dw_w.shape[1]
    ch = d // 2
    rows_g = t // GRID_W
    nl = ada_w.shape[0]
    cw = ada_w.shape[2]
    x2, ctx2, tgt = x[0], ctx[0], loss_target[0]

    (c_all, convw_g), _ = _exchange("gather_first", [c, ssd_conv_w[0]])
    w_in_wire = ssd_w_in[0].astype(WIRE)
    ride_norm = _Comm(gather=[w_in_wire[d // 2:]])
    ride_proj = _Comm(gather=[ssd_w_out[0].astype(WIRE), conf_w_pw2[0].astype(WIRE)])
    ride_conv = _Comm(gather=[conf_w_pw1[0].astype(WIRE), conf_b_pw1, conf_dw_w[0], conf_dw_b, conf_ln_g, conf_ln_b, conf_b_pw2])
    ride_scan = _Comm(gather=[ffn_w_in[0].astype(WIRE), ffn_w_in[1].astype(WIRE), ffn_w_out[0].astype(WIRE), ffn_w_out[1].astype(WIRE)])
    conv_w_full = _unshard_cols(convw_g)

    cs_all = jnp.concatenate([c_all[:, 0, :], jnp.broadcast_to(c_ctx[None, :], (NDEV, d))], axis=0)
    ada_b_mine = lax.dynamic_slice_in_dim(ada_b, me * cw, cw, axis=1)[:, None, :]
    mod_part = _ada_fwd("ada_fwd", cs_all, ada_w, ada_b_mine)
    (mod_g, w_in_g0), _ = _exchange("gather_mod", [mod_part, w_in_wire[:d // 2]])
    mod_all = jnp.moveaxis(mod_g, 0, 2).reshape(nl, 2 * NDEV, NDEV * cw)
    mod_lat = lax.dynamic_slice_in_dim(mod_all, me, 1, axis=1)[:, 0, :]
    mod_ctx = mod_all[0, NDEV, :]

    def six(v):
        return [v[k * d:(k + 1) * d][None, :] for k in range(6)]

    sh1, s1, g1, sh2, s2, g2 = six(mod_lat[0])
    csh1, cs1 = six(mod_ctx)[:2]
    sh1b, s1b, g1b, sh2b, s2b, g2b = six(mod_lat[1])
    nmg, nfg = norm_mix_g, norm_ffn_g

    s01, sh01 = jnp.concatenate([cs1, s1], axis=0), jnp.concatenate([csh1, sh1], axis=0)
    xn_all, ((w_in_g1,), _) = _normmod_fwd("l0_norm", ctx2, x2, nmg[0:1], s01, sh01, comm=ride_norm)
    w_ssd_in = jnp.concatenate([_unshard_cols(w_in_g0), _unshard_cols(w_in_g1)], axis=0)
    w_dt = jnp.pad(w_ssd_in[:, di + cdim:], ((0, 0), (0, LANES - 2 * h)))
    zx, ((w_out_g, pw2_g), _) = _mm("ssd_in_proj", xn_all, w_ssd_in, "nn", ACT, comm=ride_proj, n_used=di + cdim)
    dtr = _mm("ssd_dt_proj", xn_all, w_dt, "nn")
    dt2 = jnp.moveaxis(dtr[:, :2 * h].reshape(tt, 2, h), 1, 0)
    bias2 = jnp.stack([ssd_dt_bias_f, ssd_dt_bias_b])
    alog2 = jnp.stack([ssd_a_log_f, ssd_a_log_b])
    xbc, xbc_pre, ((pw1_g, bpw1_g, dww_g, dwb_g, lng_g, lnb_g, bpw2_g), _) = _ssd_conv_fwd("ssd_conv", zx, conv_w_full, ssd_conv_b, di, tc,
                                                                                 comm=ride_conv)
    y2, hp2, (ffn_g, _) = _ssd_fwd("ssd_scan", xbc, dt2, bias2, alog2, di, tc, comm=ride_scan)
    w_ssd_out = _unshard_rows(w_out_g)
    w_pw1, w_pw2 = _unshard_cols(pw1_g), _unshard_rows(pw2_g)
    w_fin = [_unshard_cols(ffn_g[0]), _unshard_cols(ffn_g[1])]
    w_fout = [_unshard_rows(ffn_g[2]), _unshard_rows(ffn_g[3])]
    dw_w_full = _unshard_cols(dww_g)
    b_pw1, dw_b, ln_g, ln_b, b_pw2 = (_unshard_cols(a) for a in (bpw1_g, dwb_g, lng_g, lnb_g, bpw2_g))
    dskip_e = jnp.repeat(ssd_d_skip, di // h, axis=1)
    yn = _ssd_gate_fwd("ssd_gate", y2, xbc, zx, dskip_e, ssd_norm_w, di, nct, t)
    mix0 = _mm("ssd_out_proj", yn, w_ssd_out, "nn")
    h1, xf0 = _resnorm_fwd("l0_res_norm", x2, mix0, g1, nfg[0:1], s2, sh2)
    u0 = _mm("ffn0_in", xf0, w_fin[0], "nn", ACT)
    hid0 = _swiglu_fwd("ffn0_act", u0)
    f0 = _mm("ffn0_out", hid0, w_fout[0], "nn")
    h2, xn1 = _resnorm_fwd("l1_norm", h1, f0, g2, nmg[1:2], s1b, sh1b)
    u1 = _mm("conf_pw1", xn1, w_pw1, "nn", ACT, bias=b_pw1)
    gl = _glu_fwd("conf_glu", u1)
    gl_h = _grid_t(gl[:, :ch], rows_g, GRID_W)
    v_ht = _strided_conv("conf_conv_h", gl_h, dw_w_full[:, :ch], dw_b[:, :ch], rows_g)
    v_v = _strided_conv("conf_conv_v", gl, dw_w_full[:, ch:], dw_b[:, ch:], GRID_W, x_col0=ch)
    v_h = _grid_t(v_ht, GRID_W, rows_g)
    sl = _ln_silu_fwd("conf_ln", v_h, v_v, ln_g, ln_b)
    mix1 = _mm("conf_pw2", sl, w_pw2, "nn", bias=b_pw2)
    h3, xf1 = _resnorm_fwd("l1_res_norm", h2, mix1, g1b, nfg[1:2], s2b, sh2b)
    u2 = _mm("ffn1_in", xf1, w_fin[1], "nn", ACT)
    hid1 = _swiglu_fwd("ffn1_act", u2)
    f1 = _mm("ffn1_out", hid1, w_fout[1], "nn")
    dh, sq, d_final_g = _final_loss("final_loss", h3, f1, tgt, g2b, final_norm_g[None, :])

    zero2 = jnp.zeros((2, d), F32)

    def ffn_bwd(tag, dh, hin, xf, u, hid, f, gate, w_in, w_out, g_norm, s_mod):
        df, dgate, _ = _gate_bwd(tag + "_gate_bwd", dh, f, gate)
        dhid = _mm(tag + "_dhid", df, w_out, "nt", ACT)
        dw_out = _mm(tag + "_dwout", hid, df, "tn", WIRE)
        du = _swiglu_bwd(tag + "_act_bwd", u, dhid)
        dw_in = _mm(tag + "_dwin", xf, du, "tn", WIRE)
        dxf = _mm(tag + "_dx", du, w_in, "nt")
        s_2 = jnp.concatenate([s_mod, s_mod], axis=0)
        dh, dsh, ds, dg = _normmod_bwd(tag + "_norm_bwd", None, hin, dxf, dh, g_norm, s_2)
        return dh, dgate, dsh[1:2], ds[1:2], dg[1:2], dw_in, dw_out

    dh, d_g2b, d_sh2b, d_s2b, d_nfg1, g_fin1, g_fout1 = ffn_bwd("ffn1", dh, h3, xf1, u2, hid1, f1, g2b, w_fin[1], w_fout[1], nfg[1:2], s2b)
    dmix1, d_g1b, g_bpw2 = _gate_bwd("conf_gate_bwd", dh, mix1, g1b)
    dsl = _mm("conf_dsl", dmix1, w_pw2, "nt")
    g_pw2 = _mm("conf_dwpw2", sl, dmix1, "tn", WIRE)
    dv_lo, dv_v, g_lng, g_lnb = _ln_silu_bwd("conf_ln_bwd", v_h, v_v, dsl, ln_g, ln_b)
    dv_h = _grid_t(dv_lo, rows_g, GRID_W)
    w_flip = dw_w_full[::-1]
    dgl_h = _strided_conv("conf_conv_h_bwd", dv_h, w_flip[:, :ch], None, rows_g)
    dgl_v = _strided_conv("conf_conv_v_bwd", dv_v, w_flip[:, ch:], None, GRID_W)
    g_dww_h, g_dwb_h = _strided_conv_dw("conf_conv_h_dw", gl_h, dv_h, ck, rows_g)
    g_dww_v, g_dwb_v = _strided_conv_dw("conf_conv_v_dw", gl, dv_v, ck, GRID_W, x_col0=ch)
    g_dww, g_dwb = jnp.concatenate([g_dww_h, g_dww_v], axis=1), jnp.concatenate([g_dwb_h, g_dwb_v], axis=1)
    du1, g_bpw1 = _glu_bwd("conf_glu_bwd", u1, _grid_t(dgl_h, GRID_W, rows_g), dgl_v)
    g_pw1 = _mm("conf_dwpw1", xn1, du1, "tn", WIRE)
    dxn1 = _mm("conf_dx", du1, w_pw1, "nt")
    dh, dsh_, ds_, dg_ = _normmod_bwd("l1_norm_bwd", None, h2, dxn1, dh, nmg[1:2], jnp.concatenate([s1b, s1b], axis=0))
    d_sh1b, d_s1b, d_nmg1 = dsh_[1:2], ds_[1:2], dg_[1:2]
    dh, d_g2, d_sh2, d_s2, d_nfg0, g_fin0, g_fout0 = ffn_bwd("ffn0", dh, h1, xf0, u0, hid0, f0, g2, w_fin[0], w_fout[0], nfg[0:1], s2)
    dmix0, d_g1, _ = _gate_bwd("ssd_gate_res_bwd", dh, mix0, g1)
    dyn = _mm("ssd_dyn", dmix0, w_ssd_out, "nt")
    g_ssd_out = _mm("ssd_dwout", yn, dmix0, "tn", WIRE)
    dy, dzx, g_normw, ddsk_e = _ssd_gate_bwd("ssd_gate_bwd", dyn, y2, xbc, zx, dskip_e, ssd_norm_w, di, nct, tt)
    ride_scan_bwd = _Comm(scatter=[_shard_cols(g_fin0), _shard_cols(g_fin1), _shard_rows(g_fout0), _shard_rows(g_fout1)])
    ride_conv_bwd = _Comm(scatter=[_shard_rows(g_ssd_out), _shard_cols(g_pw1), _shard_rows(g_pw2), _shard_cols(g_bpw1),
                                   _shard_cols(g_dww), _shard_cols(g_dwb), _shard_cols(g_lng), _shard_cols(g_lnb), _shard_cols(g_bpw2)])
    dxbc2, ddt2, g_alog2, g_bias2, (_, ffn_r) = _ssd_bwd("ssd_scan_bwd", xbc, dt2, bias2, alog2, dy, hp2, dskip_e, di, tc,
                                                         comm=ride_scan_bwd)
    dzx, g_convw, g_convb, (_, conv_r) = _ssd_conv_bwd("ssd_conv_bwd", zx, dxbc2, conv_w_full, xbc_pre, dzx, di, tc,
                                                       comm=ride_conv_bwd)
    ddt_p = jnp.pad(jnp.moveaxis(ddt2, 0, 1).reshape(tt, 2 * h), ((0, 0), (0, LANES - 2 * h))).astype(MXU)
    g_ssd_in = jnp.concatenate([_mm("ssd_dw_zx", xn_all, dzx, "tn", WIRE),
                                _mm("ssd_dw_dt", xn_all, ddt_p, "tn", WIRE)[:, :2 * h]], axis=1)
    dxn, (_, (ssd_in_r0, convw_r)) = _mm("ssd_dx_zx", dzx, w_ssd_in, "nt",
                                         comm=_Comm(scatter=[_shard_cols(g_ssd_in[:d // 2]), _shard_cols(g_convw)]))
    dxn = _mm("ssd_dx_dt", ddt_p, w_dt, "nt", add=dxn)
    dh0, dsh_, ds_, dg_, (_, (ssd_in_r1,)) = _normmod_bwd("l0_norm_bwd", ctx2, x2, dxn, dh, nmg[0:1], s01,
                                                          comm=_Comm(scatter=[_shard_cols(g_ssd_in[d // 2:])]))
    grad_x = dh0[None]
    d_csh1, d_sh1, d_cs1, d_s1 = dsh_[0:1], dsh_[1:2], ds_[0:1], ds_[1:2]
    d_nmg0 = dg_[0:1] + dg_[1:2]

    z1 = jnp.zeros((1, d), F32)
    dmod = jnp.concatenate([jnp.concatenate([d_sh1, d_s1, d_g1, d_sh2, d_s2, d_g2], axis=1),
                            jnp.concatenate([d_sh1b, d_s1b, d_g1b, d_sh2b, d_s2b, d_g2b], axis=1),
                            jnp.concatenate([d_csh1, d_cs1, z1, z1, z1, z1], axis=1)], axis=0)
    out = {}

    def put(name, res):
        w = args[name]
        out["grad_" + name], out["delta_" + name], out["new_m_" + name], out["new_v_" + name] = (r.reshape(w.shape) for r in res)

    def adam_big(name, slots, comm=None):
        return _adam("adam_" + name, slots, _flat2(args[name]), _flat2(args["m_" + name]), _flat2(args["v_" + name]), comm=comm)

    res, ((dmod_g,), _) = adam_big("ffn_w_in", [ffn_r[0], ffn_r[1]], comm=_Comm(gather=[dmod]))
    put("ffn_w_in", res)
    dmod_mine = lax.dynamic_slice_in_dim(dmod_g, me * cw, cw, axis=2)
    dmod16 = jnp.stack([jnp.concatenate([dmod_mine[:, 0], dmod_mine[:, 2]], axis=0),
                        jnp.concatenate([dmod_mine[:, 1], jnp.zeros((NDEV, cw), F32)], axis=0)])
    g_ada_w, dsc_part = _ada_bwd("ada_bwd", cs_all, ada_w, dmod16)
    g_ada_b = dmod[0:2] + jnp.concatenate([dmod[2:3], jnp.zeros((1, 6 * d), F32)], axis=0)

    d_dskip = jnp.sum(ddsk_e.reshape(h, di // h), axis=1)[None, :]
    rep = [dsc_part, g_ada_b, jnp.concatenate([d_nmg0, d_nmg1], axis=0), jnp.concatenate([d_nfg0, d_nfg1], axis=0),
           d_final_g, g_convb, g_bias2[0], g_bias2[1], g_alog2[0], g_alog2[1], d_dskip, g_normw]
    res, (rep_g, _) = _adam("adam_ada_w", _flat2(g_ada_w)[None], _flat2(ada_w), _flat2(m_ada_w), _flat2(v_ada_w),
                            comm=_Comm(gather=rep + [sq]))
    put("ada_w", res)
    rep_g, sq_g = rep_g[:-1], rep_g[-1]
    loss = (0.5 / d) * jnp.sum(sq_g[:, 0, 0])
    small_r = [convw_r] + list(conv_r[3:])

    for name, slots in zip(["ssd_w_in", "ssd_w_out", "conf_w_pw1", "conf_w_pw2", "ffn_w_out"],
                           [[ssd_in_r0, ssd_in_r1], conv_r[0], conv_r[1], conv_r[2], [ffn_r[2], ffn_r[3]]]):
        put(name, adam_big(name, slots))
    small_names = ["ssd_conv_w", "conf_b_pw1", "conf_dw_w", "conf_dw_b", "conf_ln_g", "conf_ln_b", "conf_b_pw2",
                   "c_ctx", "ada_b", "norm_mix_g", "norm_ffn_g", "final_norm_g", "ssd_conv_b", "ssd_dt_bias_f", "ssd_dt_bias_b",
                   "ssd_a_log_f", "ssd_a_log_b", "ssd_d_skip", "ssd_norm_w"]
    slots = list(small_r) + list(rep_g)

    def as2(a):
        return a.reshape((1, -1)) if a.ndim == 1 else _flat2(a)

    res = _adam_small("adam_small", slots, [as2(args[n]) for n in small_names], [as2(args["m_" + n]) for n in small_names],
                      [as2(args["v_" + n]) for n in small_names], scale=(small_names.index("c_ctx"), c_ctx[None, :]))
    for k, name in enumerate(small_names):
        put(name, [r[k] for r in res])
    return (loss, grad_x, *[out["grad_" + n] for n in names], *[out["delta_" + n] for n in names],
            *[out["new_m_" + n] for n in names], *[out["new_v_" + n] for n in names])


def _flat3(a):
    return a.reshape((a.shape[0], -1, a.shape[-1]))
```

```python
import functools

import jax
import jax.numpy as jnp
from jax import lax
from jax.experimental import pallas as pl
from jax.experimental.pallas import tpu as pltpu

F32 = jnp.float32
MXU = jnp.bfloat16
WIRE = jnp.bfloat16
ACT = jnp.bfloat16
NDEV = 8
AXES = ("x", "y", "c")
SSD_STATE = 128
SSD_CHUNK = 128
GRID_W = 64
EPS = 1e-6
ROW_TILE = 256
LANES = 128
ADAM_LR, ADAM_B1, ADAM_B2, ADAM_EPS, ADAM_WD, ADAM_STEP = 0.001, 0.9, 0.999, 1e-08, 0.01, 10
VMEM_CAP = 56 * 2 ** 20
MESH_ID = pl.DeviceIdType.MESH


def _pick(dim, cands):
    for c in cands:
        if dim % c == 0:
            return c
    return dim


def _nbytes(shape, dtype):
    n = 1
    for s in shape:
        n *= s
    return n * jnp.dtype(dtype).itemsize


def _vmem(nbytes):
    return int(min(VMEM_CAP, max(24 * 2 ** 20, 2 * nbytes + 8 * 2 ** 20)))


def _sigmoid(x):
    return 1.0 / (1.0 + jnp.exp(-x))


def _silu(x):
    return x * _sigmoid(x)


def _dsilu(x):
    s = _sigmoid(x)
    return s * (1.0 + x * (1.0 - s))


def _softplus(x):
    return jnp.maximum(x, 0.0) + jnp.log(1.0 + jnp.exp(-jnp.abs(x)))


def _dot(a, b, dims):
    return lax.dot_general(a.astype(MXU), b.astype(MXU), (dims, ((), ())), preferred_element_type=F32)


NN, NT, TN = ((1,), (0,)), ((1,), (1,)), ((0,), (0,))


def _split(a, parts):
    out = []
    for _ in range(parts):
        p = a.astype(MXU)
        out.append(p)
        a = a - p.astype(F32)
    return out


def _dot_lx(e, a, dims, parts=2):
    return sum(lax.dot_general(e, p, (dims, ((), ())), preferred_element_type=F32) for p in _split(a, parts))


def _dot_rx(a, e, dims, parts=2):
    return sum(lax.dot_general(p, e, (dims, ((), ())), preferred_element_type=F32) for p in _split(a, parts))


class _Comm:
    def __init__(self, gather=(), scatter=()):
        self.gather, self.scatter = list(gather), list(scatter)
        self.ng, self.n = len(self.gather), len(self.gather) + len(self.scatter)
        self.operands = self.gather + self.scatter
        self.specs = [pl.BlockSpec(memory_space=pl.ANY)] * self.n
        self.out_shape = ([jax.ShapeDtypeStruct((NDEV,) + a.shape, a.dtype) for a in self.gather]
                          + [jax.ShapeDtypeStruct(a.shape, a.dtype) for a in self.scatter])
        self.scratch = [pltpu.SemaphoreType.DMA((self.n, 7)), pltpu.SemaphoreType.DMA((self.n, 7)),
                        pltpu.SemaphoreType.DMA((self.n,))]

    def split(self, res):
        return res[:self.ng], res[self.ng:]

    def _copies(self, ins, outs, sems):
        send, recv, loc = sems
        ng, n = self.ng, self.n
        x, y, c = lax.axis_index("x"), lax.axis_index("y"), lax.axis_index("c")
        me, sib = (x, y, c), (x, y, 1 - c)
        chips = [(1 - x, y), (x, 1 - y), (1 - x, 1 - y)]

        def slot(p):
            return 4 * p[0] + 2 * p[1] + p[2]

        def rcopy(a, k, src, dst, to):
            return functools.partial(pltpu.make_async_remote_copy, src_ref=src, dst_ref=dst, send_sem=send.at[a, k],
                                     recv_sem=recv.at[a, k], device_id=to, device_id_type=MESH_ID)

        local = [functools.partial(pltpu.make_async_copy, ins[a] if a < ng else ins[a].at[slot(me)], outs[a].at[slot(me)],
                                   loc.at[a]) for a in range(n)]
        rel = [(fx, fy, fc) for fx in (0, 1) for fy in (0, 1) for fc in (0, 1)][1:]
        first, landed, passed = [], [], []
        for a in range(ng, n):
            for k, (fx, fy, fc) in enumerate(rel):
                p = (1 - x if fx else x, 1 - y if fy else y, 1 - c if fc else c)
                first.append(rcopy(a, k, ins[a].at[slot(p)], outs[a].at[slot(me)], p))
                blk = outs[a].at[slot(p)]
                landed.append(rcopy(a, k, blk, blk, me))
        for a in range(ng):
            dst = outs[a].at[slot(me)]
            first.append(rcopy(a, 0, ins[a], dst, sib))
            first += [rcopy(a, 1 + j, ins[a], dst, (*ch, c)) for j, ch in enumerate(chips)]
            blk = outs[a].at[slot(sib)]
            landed.append(rcopy(a, 0, blk, blk, me))
            for j, ch in enumerate(chips):
                blk = outs[a].at[slot((*ch, c))]
                passed.append((rcopy(a, 1 + j, blk, blk, me), rcopy(a, 4 + j, blk, blk, sib)))
                blk = outs[a].at[slot((*ch, 1 - c))]
                landed.append(rcopy(a, 4 + j, blk, blk, me))
        return local, first, passed, landed

    def start(self, ins, outs, sems):
        local, first, _, _ = self._copies(ins, outs, sems)
        for make in local + first:
            make().start()

    def finish(self, ins, outs, sems):
        local, first, passed, landed = self._copies(ins, outs, sems)
        onward = []
        for arrived, forward in passed:
            arrived().wait_recv()
            onward.append(forward())
            onward[-1].start()
        for make in landed:
            make().wait_recv()
        for make in first:
            make().wait_send()
        for cp in onward:
            cp.wait_send()
        for make in local:
            make().wait()


def _carry(body, comm, n_in, n_out, grid):
    if comm is None:
        return body
    n = comm.n

    def wrapped(*refs):
        own_in, c_in = refs[:n_in], refs[n_in:n_in + n]
        own_out, c_out = refs[n_in + n:n_in + n + n_out], refs[n_in + n + n_out:n_in + 2 * n + n_out]
        own_scr, sems = refs[n_in + 2 * n + n_out:-3], refs[-3:]
        ids = [pl.program_id(ax) for ax in range(len(grid))]
        first, last = ids[0] == 0, ids[0] == grid[0] - 1
        for ax in range(1, len(grid)):
            first, last = first & (ids[ax] == 0), last & (ids[ax] == grid[ax] - 1)

        @pl.when(first)
        def _():
            comm.start(c_in, c_out, sems)

        body(*own_in, *own_out, *own_scr)

        @pl.when(last)
        def _():
            comm.finish(c_in, c_out, sems)

    return wrapped


def _exchange(name, gather, scatter=()):
    comm = _Comm(gather, scatter)
    n = comm.n

    def body(*refs):
        ins, outs, sems = refs[:n], refs[n:2 * n], refs[2 * n:]
        comm.start(ins, outs, sems)
        comm.finish(ins, outs, sems)

    res = pl.pallas_call(body, name=name, out_shape=comm.out_shape, in_specs=comm.specs, out_specs=comm.specs,
                         scratch_shapes=comm.scratch)(*comm.operands)
    return comm.split(res)


def _hbm(a):
    return pltpu.with_memory_space_constraint(a, pltpu.HBM)


def _divs(dim, mult):
    return [dim] + [dim // parts for parts in range(2, dim // mult + 1) if dim % parts == 0 and (dim // parts) % mult == 0]


MM_VMEM_BUDGET = 40 * 2 ** 20
GRID_STEP_US = 0.35
HBM_BYTES_PER_US = 3.0e6


def _mm_tiles(m, n, k, sizes, mode, has_add):
    sa, sb, so = sizes
    sub = 16
    best = None
    for tk in _divs(k, LANES):
        for tn in _divs(n, LANES):
            for tm in _divs(m, LANES if mode == "tn" else sub):
                nk = k // tk
                out_t = tm * tn
                est = (2 * (tm * tk * sa + tk * tn * sb) + 2 * out_t * so + 2 * (tm * tk + tk * tn) + 4 * out_t
                       + (4 * out_t if nk > 1 else 0) + (8 * out_t if has_add else 0))
                if est > MM_VMEM_BUDGET:
                    continue
                steps = (m // tm) * (n // tn) * nk
                cost = steps * GRID_STEP_US + (tm * tk * sa + tk * tn * sb + out_t * so) / HBM_BYTES_PER_US
                if best is None or cost < best[0]:
                    best = (cost, tm, tn, tk, est)
    assert best is not None, (m, n, k)
    return best[1:]


def _mm(name, a, b, mode, out_dtype=F32, bias=None, add=None, comm=None, n_used=None):
    if mode == "nn":
        (m, k), (k2, n) = a.shape, b.shape
        n = n if n_used is None else n_used
    elif mode == "nt":
        (m, k), (n, k2) = a.shape, b.shape
        k2 = min(k, k2)
    else:
        (k, m), (k2, n) = a.shape, b.shape
    assert k == k2, (name, a.shape, b.shape)
    sizes = (a.dtype.itemsize, b.dtype.itemsize, jnp.dtype(out_dtype).itemsize)
    tm, tn, tk, est = _mm_tiles(m, n, k, sizes, mode, add is not None)
    nk = k // tk
    dims = {"nn": NN, "nt": NT, "tn": TN}[mode]
    a_spec = pl.BlockSpec((tk, tm), lambda i, j, kk: (kk, i)) if mode == "tn" else pl.BlockSpec((tm, tk), lambda i, j, kk: (i, kk))
    b_spec = pl.BlockSpec((tn, tk), lambda i, j, kk: (j, kk)) if mode == "nt" else pl.BlockSpec((tk, tn), lambda i, j, kk: (kk, j))
    extra, extra_specs = [], []
    if bias is not None:
        extra.append(bias)
        extra_specs.append(pl.BlockSpec((1, tn), lambda i, j, kk: (0, j)))
    if add is not None:
        extra.append(add)
        extra_specs.append(pl.BlockSpec((tm, tn), lambda i, j, kk: (i, j)))

    def finish(r, extras, o_ref):
        for e in extras:
            r = r + e[...].astype(F32)
        o_ref[...] = r.astype(o_ref.dtype)

    def body_acc(*refs):
        a_ref, b_ref = refs[:2]
        o_ref, acc = refs[-2:]
        kk = pl.program_id(2)

        @pl.when(kk == 0)
        def _():
            acc[...] = jnp.zeros_like(acc)

        acc[...] += _dot(a_ref[...], b_ref[...], dims)

        @pl.when(kk == nk - 1)
        def _():
            finish(acc[...], refs[2:-2], o_ref)

    def body_one(*refs):
        finish(_dot(refs[0][...], refs[1][...], dims), refs[2:-1], refs[-1])

    cm = comm if comm is not None else _Comm()
    grid = (m // tm, n // tn, nk)
    res = pl.pallas_call(
        _carry(body_acc if nk > 1 else body_one, comm, 2 + len(extra), 1, grid), name=name, grid=grid,
        out_shape=[pltpu.HBM((m, n), out_dtype)] + cm.out_shape,
        in_specs=[a_spec, b_spec] + extra_specs + cm.specs,
        out_specs=[pl.BlockSpec((tm, tn), lambda i, j, kk: (i, j))] + cm.specs,
        scratch_shapes=([pltpu.VMEM((tm, tn), F32)] if nk > 1 else []) + (cm.scratch if comm is not None else []),
        compiler_params=pltpu.CompilerParams(
            dimension_semantics=("parallel", "parallel", "arbitrary") if comm is None else ("arbitrary",) * 3,
            vmem_limit_bytes=int(min(VMEM_CAP, est + 12 * 2 ** 20))),
    )(*[_hbm(v) for v in (a, b, *extra)], *cm.operands)
    return res[0] if comm is None else (res[0], cm.split(res[1:]))


def _ri(arr, w=None, cb=0, ro=0, lead=None):
    return (arr, arr.shape[-1] if w is None else w, cb, ro, lead)


def _rowwise(name, fn, nrows, row_ins, bc_ins, outs, accs=(), comm=None):
    tr = min(ROW_TILE, nrows)
    assert nrows % tr == 0
    in_specs = []
    for (arr, w, cb, ro, lead) in row_ins:
        last = arr.shape[-2] // tr - 1
        if lead is None:
            in_specs.append(pl.BlockSpec((tr, w), lambda i, cb=cb, ro=ro, last=last: (jnp.clip(i + ro, 0, last), cb)))
        else:
            in_specs.append(pl.BlockSpec((None, tr, w),
                                         lambda i, cb=cb, ro=ro, lead=lead, last=last: (lead, jnp.clip(i + ro, 0, last), cb)))
    for arr in bc_ins:
        in_specs.append(pl.BlockSpec(arr.shape, lambda i, nd=arr.ndim: (0,) * nd))
    outs = [tuple(o) + (o[0], 0, 0)[len(o) - 2:] for o in outs]
    out_shape = [pltpu.HBM((nrows + ro * tr, total), dt) for _, dt, total, _, ro in outs] + [pltpu.HBM(s, F32) for s in accs]
    out_specs = ([pl.BlockSpec((tr, c), lambda i, cb=cb, ro=ro: (jnp.maximum(i + ro, 0), cb)) for c, _, _, cb, ro in outs]
                 + [pl.BlockSpec(s, lambda i: (0, 0)) for s in accs])
    nr, nb, no = len(row_ins), len(bc_ins), len(outs)

    def body(*refs):
        i = pl.program_id(0)
        rows = [r[...].astype(F32) for r in refs[:nr]]
        bcs = [r[...] for r in refs[nr:nr + nb]]
        o, a = fn(rows, bcs, i)
        for ref, val in zip(refs[nr + nb:nr + nb + no], o):
            ref[...] = val.astype(ref.dtype)
        for ref, val in zip(refs[nr + nb + no:], a):
            @pl.when(i == 0)
            def _(ref=ref, val=val):
                ref[...] = val

            @pl.when(i > 0)
            def _(ref=ref, val=val):
                ref[...] += val

    est = sum(tr * w * arr.dtype.itemsize for (arr, w, _, _, _) in row_ins) + sum(tr * o[0] * 4 for o in outs)
    cm = comm if comm is not None else _Comm()
    nout = no + len(accs)
    res = pl.pallas_call(
        _carry(body, comm, nr + nb, nout, (nrows // tr,)), name=name, grid=(nrows // tr,), out_shape=out_shape + cm.out_shape,
        in_specs=in_specs + cm.specs, out_specs=out_specs + cm.specs, scratch_shapes=cm.scratch if comm is not None else [],
        compiler_params=pltpu.CompilerParams(dimension_semantics=("arbitrary",), vmem_limit_bytes=_vmem(3 * est)),
    )(*[_hbm(r[0]) for r in row_ins], *[_hbm(v) for v in bc_ins], *cm.operands)
    if comm is None:
        return res[:no], res[no:]
    return res[:no], res[no:nout], cm.split(res[nout:])


def _colsum(v):
    return jnp.sum(v, axis=0, keepdims=True)


def _normmod_fwd(name, hc, h, g, s, sh, comm=None):
    d = h.shape[1]
    nct = 0 if hc is None else hc.shape[0] // ROW_TILE
    ins = [_ri(h)] if hc is None else [_ri(hc), _ri(h, ro=-nct)]

    def fn(rows, bcs, i):
        hh = rows[0] if hc is None else jnp.where(i < nct, rows[0], rows[1])
        g_, s_, sh_ = bcs
        s1 = jnp.where(i < nct, s_[0:1], s_[1:2])
        sh1 = jnp.where(i < nct, sh_[0:1], sh_[1:2])
        r = lax.rsqrt(jnp.mean(hh * hh, axis=-1, keepdims=True) + EPS)
        return [hh * r * g_ * (1.0 + s1) + sh1], []

    res = _rowwise(name, fn, h.shape[0] + nct * ROW_TILE, ins, [g, s, sh], [(d, MXU)], comm=comm)
    return res[0][0] if comm is None else (res[0][0], res[2])


def _normmod_bwd(name, hc, h, dxn, dres, g, s, branch=None):
    d = h.shape[1]
    nct = 0 if hc is None else hc.shape[0] // ROW_TILE
    hins = [_ri(h)] if hc is None else [_ri(hc), _ri(h, ro=-nct)]
    nh = len(hins)

    def fn(rows, bcs, i):
        hh = rows[0] if hc is None else jnp.where(i < nct, rows[0], rows[1])
        dx, dr = rows[nh], rows[nh + 1]
        g_, s_ = bcs[:2]
        ctx = i < nct
        s1 = jnp.where(ctx, s_[0:1], s_[1:2])
        r = lax.rsqrt(jnp.mean(hh * hh, axis=-1, keepdims=True) + EPS)
        hr = hh * r
        dy = dx * (1.0 + s1)
        u = dy * g_
        dh = r * u - hr * (r * r) * jnp.mean(u * hh, axis=-1, keepdims=True)
        dh = dh + jnp.where(ctx, 0.0, dr)

        def seg(v):
            v = _colsum(v)
            return jnp.concatenate([jnp.where(ctx, v, 0.0), jnp.where(ctx, 0.0, v)], axis=0)

        outs, accs = [dh], [seg(dx), seg(dx * hr * g_), seg(dy * hr)]
        if branch is not None:
            dyb = dh * bcs[2]
            outs.append(dyb)
            accs += [_colsum(dh * rows[nh + 2]), _colsum(dyb)]
        return outs, accs

    if branch is None:
        (dh,), (dsh, ds, dg) = _rowwise(name, fn, h.shape[0] + nct * ROW_TILE, hins + [_ri(dxn), _ri(dres, ro=-nct)], [g, s],
                                        [(d, F32, d, 0, -nct)], [(2, d)] * 3)
        return dh, dsh, ds, dg
    (dh, dyb), (dsh, ds, dg, dgate, dbias) = _rowwise(name, fn, h.shape[0], hins + [_ri(dxn), _ri(dres), _ri(branch[0])],
                                                      [g, s, branch[1]], [(d, F32), (d, MXU)], [(2, d)] * 3 + [(1, d)] * 2)
    return dh, dsh, ds, dg, (dyb, dgate, dbias)


def _resnorm_fwd(name, h, y, gate, g, s, sh):
    d = h.shape[1]

    def fn(rows, bcs, i):
        hh, yy = rows
        gate_, g_, s_, sh_ = bcs
        hn = hh + gate_ * yy
        r = lax.rsqrt(jnp.mean(hn * hn, axis=-1, keepdims=True) + EPS)
        return [hn, hn * r * g_ * (1.0 + s_) + sh_], []

    return _rowwise(name, fn, h.shape[0], [_ri(h), _ri(y)], [gate, g, s, sh], [(d, F32), (d, MXU)])[0]


def _gate_bwd(name, dh, y, gate):
    d = dh.shape[1]

    def fn(rows, bcs, i):
        dd, yy = rows
        dy = dd * bcs[0]
        return [dy], [_colsum(dd * yy), _colsum(dy)]

    (dy,), (dgate, dbias) = _rowwise(name, fn, dh.shape[0], [_ri(dh), _ri(y)], [gate], [(d, MXU)], [(1, d)] * 2)
    return dy, dgate, dbias


def _swiglu_fwd(name, u):
    f = u.shape[1] // 2

    def fn(rows, bcs, i):
        return [_silu(rows[0]) * rows[1]], []

    return _rowwise(name, fn, u.shape[0], [_ri(u, f, 0), _ri(u, f, 1)], [], [(f, MXU)])[0][0]


def _swiglu_bwd(name, u, dhid):
    f = u.shape[1] // 2

    def fn(rows, bcs, i):
        a, b, dd = rows
        return [jnp.concatenate([dd * b * _dsilu(a), dd * _silu(a)], axis=1)], []

    return _rowwise(name, fn, u.shape[0], [_ri(u, f, 0), _ri(u, f, 1), _ri(dhid)], [], [(2 * f, MXU)])[0][0]


def _glu_fwd(name, u):
    d = u.shape[1] // 2

    def fn(rows, bcs, i):
        return [rows[0] * _sigmoid(rows[1])], []

    return _rowwise(name, fn, u.shape[0], [_ri(u, d, 0), _ri(u, d, 1)], [], [(d, F32)])[0][0]


def _glu_bwd(name, u, dgl_lo, dgl_hi):
    d = u.shape[1] // 2

    def fn(rows, bcs, i):
        a, b = rows[:2]
        dd = jnp.concatenate(rows[2:], axis=1)
        sg = _sigmoid(b)
        du = jnp.concatenate([dd * sg, dd * a * sg * (1.0 - sg)], axis=1)
        return [du], [_colsum(du)]

    (du,), (db,) = _rowwise(name, fn, u.shape[0], [_ri(u, d, 0), _ri(u, d, 1), _ri(dgl_lo), _ri(dgl_hi)], [], [(2 * d, MXU)],
                            [(1, 2 * d)])
    return du, db


def _ln_silu_fwd(name, v_lo, v_hi, g, b):
    d = 2 * v_lo.shape[1]

    def fn(rows, bcs, i):
        vv = jnp.concatenate(rows, axis=1)
        mu = jnp.mean(vv, axis=-1, keepdims=True)
        xc = vv - mu
        rs = lax.rsqrt(jnp.mean(xc * xc, axis=-1, keepdims=True) + EPS)
        return [_silu(xc * rs * bcs[0] + bcs[1])], []

    return _rowwise(name, fn, v_lo.shape[0], [_ri(v_lo), _ri(v_hi)], [g, b], [(d, MXU)])[0][0]


def _ln_silu_bwd(name, v_lo, v_hi, ds, g, b):
    ch = v_lo.shape[1]

    def fn(rows, bcs, i):
        vv, dd = jnp.concatenate(rows[:2], axis=1), rows[2]
        mu = jnp.mean(vv, axis=-1, keepdims=True)
        xc = vv - mu
        rs = lax.rsqrt(jnp.mean(xc * xc, axis=-1, keepdims=True) + EPS)
        xh = xc * rs
        dln = dd * _dsilu(xh * bcs[0] + bcs[1])
        dxh = dln * bcs[0]
        dv = rs * (dxh - jnp.mean(dxh, axis=-1, keepdims=True) - xh * jnp.mean(dxh * xh, axis=-1, keepdims=True))
        return [dv[:, :ch], dv[:, ch:]], [_colsum(dln * xh), _colsum(dln)]

    (dv_lo, dv_hi), (dg, db) = _rowwise(name, fn, v_lo.shape[0], [_ri(v_lo), _ri(v_hi), _ri(ds)], [g, b],
                                        [(ch, F32), (ch, F32)], [(1, 2 * ch)] * 2)
    return dv_lo, dv_hi, dg, db


def _final_loss(name, h, f, target, gate, gf):
    d = h.shape[1]

    def fn(rows, bcs, i):
        hh, ff, tg = rows
        gate_, g_ = bcs
        hn = hh + gate_ * ff
        r = lax.rsqrt(jnp.mean(hn * hn, axis=-1, keepdims=True) + EPS)
        hr = hn * r
        err = hr * g_ - tg
        dout = err * (1.0 / d)
        u = dout * g_
        dh = r * u - hr * (r * r) * jnp.mean(u * hn, axis=-1, keepdims=True)
        sq = jnp.sum(_colsum(err * err), axis=1, keepdims=True)
        return [dh], [jnp.broadcast_to(sq, (1, LANES)), _colsum(dout * hr)]

    (dh,), (sq, dgf) = _rowwise(name, fn, h.shape[0], [_ri(h), _ri(f), _ri(target)], [gate, gf], [(d, F32)], [(1, LANES), (1, d)])
    return dh, sq, dgf


GAP = 8


def _gapped(ref_rows, buf, tc, tt):
    cb = buf.shape[1]
    zero = jnp.zeros((GAP, cb), F32)
    buf[0:GAP, :] = zero
    buf[GAP + tc:2 * GAP + tc, :] = zero
    buf[2 * GAP + tt:, :] = zero
    buf[GAP:GAP + tc, :] = ref_rows[0:tc]
    buf[2 * GAP + tc:2 * GAP + tt, :] = ref_rows[tc:tt]
    return buf[...]


def _ungapped(v, tc, tt):
    return jnp.concatenate([v[GAP:GAP + tc], v[2 * GAP + tc:2 * GAP + tt]], axis=0)


def _shift_rows(x, o):
    return x if o == 0 else pltpu.roll(x, (-o) % x.shape[0], 0)


def _ssd_conv_fwd(name, zx, w, b, di, tc, comm=None):
    tt, kc, cd = zx.shape[0], w.shape[0], w.shape[1]
    cb = _pick(cd, (LANES,))
    off = di // cb
    assert kc // 2 < GAP and tc % GAP == 0 and tt % GAP == 0

    def body(x_ref, w_ref, b_ref, o_ref, pre_ref, xp):
        x = _gapped(x_ref[...].astype(F32), xp, tc, tt)
        acc = jnp.broadcast_to(b_ref[...], x.shape)
        for k in range(kc):
            acc = acc + w_ref[k:k + 1, :] * _shift_rows(x, k - kc // 2)
        acc = _ungapped(acc, tc, tt)
        pre_ref[...] = acc.astype(pre_ref.dtype)
        o_ref[...] = _silu(acc).astype(o_ref.dtype)

    cm = comm if comm is not None else _Comm()
    blk = pl.BlockSpec((tt, cb), lambda j: (0, j))
    res = pl.pallas_call(
        _carry(body, comm, 3, 2, (cd // cb,)), name=name, grid=(cd // cb,),
        out_shape=[pltpu.HBM((tt, cd), ACT), pltpu.HBM((tt, cd), ACT)] + cm.out_shape,
        in_specs=[pl.BlockSpec((tt, cb), lambda j: (0, j + off)), pl.BlockSpec((kc, cb), lambda j: (0, j)),
                  pl.BlockSpec((1, cb), lambda j: (0, j))] + cm.specs,
        out_specs=[blk, blk] + cm.specs,
        scratch_shapes=[pltpu.VMEM((tt + 3 * GAP, cb), F32)] + (cm.scratch if comm is not None else []),
        compiler_params=pltpu.CompilerParams(dimension_semantics=("arbitrary",), vmem_limit_bytes=_vmem(5 * tt * cb * 4)),
    )(_hbm(zx), _hbm(w), _hbm(b), *cm.operands)
    return res[0], res[1], cm.split(res[2:])


def _ssd_conv_bwd(name, zx, dact2, w, pre, dzx, di, tc, comm=None):
    tt, kc, cd = zx.shape[0], w.shape[0], w.shape[1]
    cb = _pick(cd, (LANES,))
    off = di // cb

    def body(x_ref, d0_ref, d1_ref, w_ref, pre_ref, _, dx_ref, dw_ref, db_ref, xp, dp):
        x = _gapped(x_ref[...].astype(F32), xp, tc, tt)
        dpre = (d0_ref[...].astype(F32) + d1_ref[...].astype(F32)) * _dsilu(pre_ref[...].astype(F32))
        db_ref[...] = _colsum(dpre)
        dpre = _gapped(dpre, dp, tc, tt)
        dx = jnp.zeros_like(x)
        for k in range(kc):
            sh = _shift_rows(dpre, -(k - kc // 2))
            dx = dx + w_ref[k:k + 1, :] * sh
            dw_ref[k:k + 1, :] = _colsum(sh * x)
        dx_ref[...] = _ungapped(dx, tc, tt).astype(dx_ref.dtype)

    cm = comm if comm is not None else _Comm()
    res = pl.pallas_call(
        _carry(body, comm, 6, 3, (cd // cb,)), name=name, grid=(cd // cb,),
        out_shape=[pltpu.HBM(dzx.shape, dzx.dtype), pltpu.HBM((kc, cd), F32), pltpu.HBM((1, cd), F32)] + cm.out_shape,
        in_specs=[pl.BlockSpec((tt, cb), lambda j: (0, j + off)), pl.BlockSpec((None, tt, cb), lambda j: (0, 0, j)),
                  pl.BlockSpec((None, tt, cb), lambda j: (1, 0, j)), pl.BlockSpec((kc, cb), lambda j: (0, j)),
                  pl.BlockSpec((tt, cb), lambda j: (0, j)), pl.BlockSpec(memory_space=pl.ANY)] + cm.specs,
        out_specs=[pl.BlockSpec((tt, cb), lambda j: (0, j + off)), pl.BlockSpec((kc, cb), lambda j: (0, j)),
                   pl.BlockSpec((1, cb), lambda j: (0, j))] + cm.specs,
        input_output_aliases={5: 0},
        scratch_shapes=[pltpu.VMEM((tt + 3 * GAP, cb), F32)] * 2 + (cm.scratch if comm is not None else []),
        compiler_params=pltpu.CompilerParams(dimension_semantics=("arbitrary",), vmem_limit_bytes=_vmem(10 * tt * cb * 4)),
    )(_hbm(zx), _hbm(dact2), _hbm(dact2), _hbm(w), _hbm(pre), _hbm(dzx), *cm.operands)
    return res[0], res[1], res[2], cm.split(res[3:])


def _strided_conv(name, x, w, b, stride, x_col0=0):
    t, ch = x.shape[0], w.shape[1]
    kk = w.shape[0]
    pad = (kk // 2) * stride
    cb = _pick(ch, (LANES,))
    has_b = b is not None

    def body(*refs):
        x_ref, w_ref = refs[:2]
        o_ref, xp = refs[-2:]
        xp[0:pad, :] = jnp.zeros((pad, cb), F32)
        xp[pad + t:, :] = jnp.zeros((pad, cb), F32)
        xp[pad:pad + t, :] = x_ref[...]
        acc = jnp.broadcast_to(refs[2][...], (t, cb)) if has_b else jnp.zeros((t, cb), F32)
        for k in range(kk):
            acc = acc + w_ref[k:k + 1, :] * xp[k * stride:k * stride + t, :]
        o_ref[...] = acc

    xoff = x_col0 // cb
    ins, specs = [x, w], [pl.BlockSpec((t, cb), lambda j: (0, j + xoff)), pl.BlockSpec((kk, cb), lambda j: (0, j))]
    if has_b:
        ins.append(b)
        specs.append(pl.BlockSpec((1, cb), lambda j: (0, j)))
    return pl.pallas_call(
        body, name=name, grid=(ch // cb,), out_shape=pltpu.HBM((t, ch), F32), in_specs=specs,
        out_specs=pl.BlockSpec((t, cb), lambda j: (0, j)), scratch_shapes=[pltpu.VMEM((t + 2 * pad, cb), F32)],
        compiler_params=pltpu.CompilerParams(dimension_semantics=("parallel",), vmem_limit_bytes=_vmem(6 * t * cb * 4)),
    )(*[_hbm(v) for v in ins])


def _strided_conv_dw(name, x, dv, kk, stride, x_col0=0):
    t, ch = dv.shape
    pad = (kk // 2) * stride
    cb = _pick(ch, (LANES,))

    def body(x_ref, d_ref, dw_ref, db_ref, xp):
        xp[0:pad, :] = jnp.zeros((pad, cb), F32)
        xp[pad + t:, :] = jnp.zeros((pad, cb), F32)
        xp[pad:pad + t, :] = x_ref[...]
        d = d_ref[...]
        for k in range(kk):
            dw_ref[k:k + 1, :] = _colsum(d * xp[k * stride:k * stride + t, :])
        db_ref[...] = _colsum(d)

    blk = pl.BlockSpec((t, cb), lambda j: (0, j))
    xoff = x_col0 // cb
    return pl.pallas_call(
        body, name=name, grid=(ch // cb,), out_shape=[pltpu.HBM((kk, ch), F32), pltpu.HBM((1, ch), F32)],
        in_specs=[pl.BlockSpec((t, cb), lambda j: (0, j + xoff)), blk], out_specs=[pl.BlockSpec((kk, cb), lambda j: (0, j)), pl.BlockSpec((1, cb), lambda j: (0, j))],
        scratch_shapes=[pltpu.VMEM((t + 2 * pad, cb), F32)],
        compiler_params=pltpu.CompilerParams(dimension_semantics=("parallel",), vmem_limit_bytes=_vmem(6 * t * cb * 4)),
    )(_hbm(x), _hbm(dv))


def _grid_t(a, n1, n2):
    return a.reshape(n1, n2, a.shape[-1]).swapaxes(0, 1).reshape(n1 * n2, a.shape[-1])


def _chunk_order(d, i, ncc, nc):
    back = jnp.where(i < ncc, ncc - 1 - i, nc - 1 - (i - ncc))
    return jnp.where(d == 0, i, back)


def _ssd_chunk_setup(d, dt_raw, bias, a_log, q, h, di):
    p = di // h
    dt = _softplus(dt_raw + bias)
    a_neg = -jnp.exp(a_log)
    delta = dt * a_neg
    r = lax.broadcasted_iota(jnp.int32, (q, q), 0)
    c = lax.broadcasted_iota(jnp.int32, (q, q), 1)
    sgn = 1 - 2 * d
    mask = (r - c) * sgn >= 0
    mask_t = (c - r) * sgn >= 0
    a = _dot_lx(mask.astype(MXU), delta, NN, parts=3)
    tot = _colsum(delta)
    ea, dte, cd = jnp.exp(a), jnp.exp(tot - a), jnp.exp(tot)
    hh = lax.broadcasted_iota(jnp.int32, (h, di), 0)
    cc = lax.broadcasted_iota(jnp.int32, (h, di), 1)
    e = (cc // p == hh).astype(MXU)
    ex = _dot_rx(jnp.concatenate([dt, ea, dte, jnp.broadcast_to(cd, (8, h))], axis=0), e, NN)
    eye = (lax.broadcasted_iota(jnp.int32, (h, h), 0) == lax.broadcasted_iota(jnp.int32, (h, h), 1)).astype(MXU)
    a_t = _dot_lx(eye, a, NT, parts=3)
    return dict(dt=dt, a_neg=a_neg, a=a, a_t=a_t, mask=mask, mask_t=mask_t, e=e,
                dt_e=ex[0:q], ea_e=ex[q:2 * q], dte_e=ex[2 * q:3 * q], cd_e=ex[3 * q:3 * q + 1])


def _pick_heads(r, q, hpg, p):
    lane = lax.broadcasted_iota(jnp.int32, (q, hpg * p), 1) // p
    out = jnp.zeros((q, hpg * p), F32)
    for j in range(hpg):
        out = out + jnp.where(lane == j, r[j * q:(j + 1) * q], 0.0)
    return out


def _ssd_fwd(name, xbc, dt2, bias2, alog2, di, tc, comm=None):
    tt, cd = xbc.shape
    h = dt2.shape[-1]
    q, n = SSD_CHUNK, SSD_STATE
    gn = (cd - di) // 2
    g = gn // n
    hpg, p = h // g, di // h
    gp = hpg * p
    nc, ncc = tt // q, tc // q
    assert di % gn == 0

    def body(x_ref, b_ref, c_ref, dt_ref, bias_ref, alog_ref, y_ref, hp_ref, ht):
        d, i = pl.program_id(0), pl.program_id(1)

        @pl.when(i == 0)
        def _():
            ht[...] = jnp.zeros_like(ht)

        s = _ssd_chunk_setup(d, dt_ref[...], bias_ref[...], alog_ref[...], q, h, di)
        xd = x_ref[...].astype(F32) * s["dt_e"]
        hp_ref[...] = ht[...].astype(hp_ref.dtype)
        for gi in range(g):
            bg, cg = b_ref[:, gi * n:(gi + 1) * n].astype(MXU), c_ref[:, gi * n:(gi + 1) * n].astype(MXU)
            sl = slice(gi * gp, (gi + 1) * gp)
            sc = _dot(cg, bg, NT)
            ms = []
            for j in range(hpg):
                hd = gi * hpg + j
                seg = s["a"][:, hd:hd + 1] - s["a_t"][hd:hd + 1, :]
                ms.append(sc * jnp.exp(jnp.where(s["mask"], seg, -jnp.inf)))
            xdg = xd[:, sl]
            ydiag = _pick_heads(_dot(jnp.concatenate(ms, axis=0), xdg, NN), q, hpg, p)
            htg = ht[:, sl]
            y_ref[:, sl] = ydiag + _dot(cg, htg, NN) * s["ea_e"][:, sl]
            ht[:, sl] = s["cd_e"][:, sl] * htg + _dot(bg, xdg * s["dte_e"][:, sl], TN)

    def cidx(d, i):
        return _chunk_order(d, i, ncc, nc)

    cm = comm if comm is not None else _Comm()
    res = pl.pallas_call(
        _carry(body, comm, 6, 2, (2, nc)), name=name, grid=(2, nc),
        out_shape=[pltpu.HBM((2, tt, di), F32), pltpu.HBM((2, nc, n, di), ACT)] + cm.out_shape,
        in_specs=[pl.BlockSpec((q, di), lambda d, i: (cidx(d, i), 0)),
                  pl.BlockSpec((q, gn), lambda d, i: (cidx(d, i), di // gn)),
                  pl.BlockSpec((q, gn), lambda d, i: (cidx(d, i), di // gn + 1)),
                  pl.BlockSpec((None, q, h), lambda d, i: (d, cidx(d, i), 0)),
                  pl.BlockSpec((None, 1, h), lambda d, i: (d, 0, 0)),
                  pl.BlockSpec((None, 1, h), lambda d, i: (d, 0, 0))] + cm.specs,
        out_specs=[pl.BlockSpec((None, q, di), lambda d, i: (d, cidx(d, i), 0)),
                   pl.BlockSpec((None, None, n, di), lambda d, i: (d, cidx(d, i), 0, 0))] + cm.specs,
        scratch_shapes=[pltpu.VMEM((n, di), F32)] + (cm.scratch if comm is not None else []),
        compiler_params=pltpu.CompilerParams(dimension_semantics=("arbitrary", "arbitrary"), vmem_limit_bytes=_vmem(16 * q * di * 4)),
    )(*[_hbm(v) for v in (xbc, xbc, xbc, dt2, bias2, alog2)], *cm.operands)
    return res[0], res[1], cm.split(res[2:])


def _ssd_bwd(name, xbc, dt2, bias2, alog2, dy, hp2, dskip_e, di, tc, comm=None):
    tt, cd = xbc.shape
    h = dt2.shape[-1]
    q, n = SSD_CHUNK, SSD_STATE
    gn = (cd - di) // 2
    g = gn // n
    hpg, p = h // g, di // h
    gp = hpg * p
    nc, ncc = tt // q, tc // q

    def body(x_ref, b_ref, c_ref, dt_ref, bias_ref, alog_ref, dy_ref, hp_ref, dsk_ref,
             dxbc_ref, ddt_ref, dalog_ref, dbias_ref, dht, dxd, off):
        d, i = pl.program_id(0), pl.program_id(1)

        @pl.when(i == 0)
        def _():
            dht[...] = jnp.zeros_like(dht)
            dalog_ref[...] = jnp.zeros_like(dalog_ref)
            dbias_ref[...] = jnp.zeros_like(dbias_ref)

        s = _ssd_chunk_setup(d, dt_ref[...], bias_ref[...], alog_ref[...], q, h, di)
        x, dyc = x_ref[...].astype(F32), dy_ref[...]
        xd = x * s["dt_e"]
        dyea = dyc * s["ea_e"]
        xdte = xd * s["dte_e"]
        lane = lax.broadcasted_iota(jnp.int32, (q, gp), 1) // p
        lane_h = lax.broadcasted_iota(jnp.int32, (q, h), 1)
        da_d = jnp.zeros((q, h), F32)
        last_e = []
        for gi in range(g):
            bg, cg = b_ref[:, gi * n:(gi + 1) * n].astype(MXU), c_ref[:, gi * n:(gi + 1) * n].astype(MXU)
            sl = slice(gi * gp, (gi + 1) * gp)
            sc, sct = _dot(cg, bg, NT), _dot(bg, cg, NT)
            dyg, xdg = dyc[:, sl], xd[:, sl]
            htg, dhtg = hp_ref[:, sl].astype(F32), dht[:, sl]
            dystack = jnp.concatenate([jnp.where(lane == j, dyg, 0.0) for j in range(hpg)], axis=0)
            xdstack = jnp.concatenate([jnp.where(lane == j, xdg, 0.0) for j in range(hpg)], axis=0)
            gs = _dot(dystack, xdg, NT)
            gst = _dot(xdstack, dyg, NT)
            ds = jnp.zeros((q, q), F32)
            mts = []
            for j in range(hpg):
                hd = gi * hpg + j
                col, rw = s["a"][:, hd:hd + 1], s["a_t"][hd:hd + 1, :]
                gl = gs[j * q:(j + 1) * q] * jnp.exp(jnp.where(s["mask"], col - rw, -jnp.inf))
                ds = ds + gl
                mt = sct * jnp.exp(jnp.where(s["mask_t"], rw - col, -jnp.inf))
                mts.append(mt)
                da_j = jnp.sum(gl * sc, axis=1, keepdims=True) - jnp.sum(gst[j * q:(j + 1) * q] * mt, axis=1, keepdims=True)
                da_d = da_d + jnp.where(lane_h == hd, da_j, 0.0)
            dxd_diag = _pick_heads(_dot(jnp.concatenate(mts, axis=0), dyg, NN), q, hpg, p)
            z = _dot(bg, dhtg, NN) * s["dte_e"][:, sl]
            yoff = _dot(cg, htg, NN) * s["ea_e"][:, sl]
            off[:, sl] = dyg * yoff - xdg * z
            dxd[:, sl] = dxd_diag + z
            dxbc_ref[:, di + gi * n:di + (gi + 1) * n] = (_dot(ds, cg, TN) + _dot(xdte[:, sl], dhtg, NT)).astype(dxbc_ref.dtype)
            dxbc_ref[:, di + gn + gi * n:di + gn + (gi + 1) * n] = (_dot(ds, bg, NN)
                                                                    + _dot(dyea[:, sl], htg, NT)).astype(dxbc_ref.dtype)
            last_e.append(s["cd_e"][:, sl] * _colsum(dhtg * htg) + _colsum(xdg * z))
            dht[:, sl] = s["cd_e"][:, sl] * dhtg + _dot(cg, dyea[:, sl], TN)
        dxd_all = dxd[...]
        last = jnp.concatenate(last_e, axis=1)
        da = da_d + _dot_rx(off[...], s["e"], NT)
        last_h = _dot_rx(jnp.broadcast_to(last, (8, di)), s["e"], NT)[0:1]
        ddelta = _dot_lx(s["mask_t"].astype(MXU), da, NN, parts=3) + last_h
        ddt = ddelta * s["a_neg"] + _dot_rx(dxd_all * x, s["e"], NT)
        ddt_raw = ddt * _sigmoid(dt_ref[...] + bias_ref[...])
        ddt_ref[...] = ddt_raw
        dalog_ref[...] += _colsum(ddelta * s["dt"]) * s["a_neg"]
        dbias_ref[...] += _colsum(ddt_raw)
        dxbc_ref[:, 0:di] = (dxd_all * s["dt_e"] + jnp.where(d == 0, dyc * dsk_ref[...], 0.0)).astype(dxbc_ref.dtype)

    def cidx(d, i):
        return _chunk_order(d, nc - 1 - i, ncc, nc)

    cm = comm if comm is not None else _Comm()
    res = pl.pallas_call(
        _carry(body, comm, 9, 4, (2, nc)), name=name, grid=(2, nc),
        out_shape=[pltpu.HBM((2, tt, cd), ACT), pltpu.HBM((2, tt, h), F32),
                   pltpu.HBM((2, 1, h), F32), pltpu.HBM((2, 1, h), F32)] + cm.out_shape,
        in_specs=[pl.BlockSpec((q, di), lambda d, i: (cidx(d, i), 0)),
                  pl.BlockSpec((q, gn), lambda d, i: (cidx(d, i), di // gn)),
                  pl.BlockSpec((q, gn), lambda d, i: (cidx(d, i), di // gn + 1)),
                  pl.BlockSpec((None, q, h), lambda d, i: (d, cidx(d, i), 0)),
                  pl.BlockSpec((None, 1, h), lambda d, i: (d, 0, 0)),
                  pl.BlockSpec((None, 1, h), lambda d, i: (d, 0, 0)),
                  pl.BlockSpec((q, di), lambda d, i: (cidx(d, i), 0)),
                  pl.BlockSpec((None, None, n, di), lambda d, i: (d, cidx(d, i), 0, 0)),
                  pl.BlockSpec((1, di), lambda d, i: (0, 0))] + cm.specs,
        out_specs=[pl.BlockSpec((None, q, cd), lambda d, i: (d, cidx(d, i), 0)),
                   pl.BlockSpec((None, q, h), lambda d, i: (d, cidx(d, i), 0)),
                   pl.BlockSpec((None, 1, h), lambda d, i: (d, 0, 0)),
                   pl.BlockSpec((None, 1, h), lambda d, i: (d, 0, 0))] + cm.specs,
        scratch_shapes=[pltpu.VMEM((n, di), F32), pltpu.VMEM((q, di), F32), pltpu.VMEM((q, di), F32)]
        + (cm.scratch if comm is not None else []),
        compiler_params=pltpu.CompilerParams(dimension_semantics=("arbitrary", "arbitrary"), vmem_limit_bytes=_vmem(24 * q * di * 4)),
    )(*[_hbm(v) for v in (xbc, xbc, xbc, dt2, bias2, alog2, dy, hp2, dskip_e)], *cm.operands)
    return res[0], res[1], res[2], res[3], cm.split(res[4:])


def _ssd_gate_fwd(name, y2, xbc, zx, dskip_e, norm_w, di, nct, t):
    def fn(rows, bcs, i):
        yf, yb, xs, z = rows
        zg = (yf + yb + bcs[0] * xs) * _silu(z)
        rn = lax.rsqrt(jnp.mean(zg * zg, axis=-1, keepdims=True) + EPS)
        return [zg * rn * bcs[1]], []

    ins = [_ri(y2, lead=0, ro=nct), _ri(y2, lead=1, ro=nct), _ri(xbc, di, 0, ro=nct), _ri(zx, di, 0, ro=nct)]
    return _rowwise(name, fn, t, ins, [dskip_e, norm_w], [(di, MXU)])[0][0]


def _ssd_gate_bwd(name, dyn, y2, xbc, zx, dskip_e, norm_w, di, nct, tt):
    def fn(rows, bcs, i):
        dn, yf, yb, xs, z = rows
        lat = i >= nct
        ytot = yf + yb + bcs[0] * xs
        sz = _silu(z)
        zg = ytot * sz
        rn = lax.rsqrt(jnp.mean(zg * zg, axis=-1, keepdims=True) + EPS)
        u = dn * bcs[1]
        dzg = rn * u - zg * (rn * rn * rn) * jnp.mean(u * zg, axis=-1, keepdims=True)
        dy = jnp.where(lat, dzg * sz, 0.0)
        dz = jnp.where(lat, dzg * ytot * _dsilu(z), 0.0)
        return [dy, dz], [jnp.where(lat, _colsum(dn * zg * rn), 0.0), jnp.where(lat, _colsum(dy * xs), 0.0)]

    ins = [_ri(dyn, ro=-nct), _ri(y2, lead=0), _ri(y2, lead=1), _ri(xbc, di, 0), _ri(zx, di, 0)]
    (dy, dzx), (dnw, ddsk) = _rowwise(name, fn, tt, ins, [dskip_e, norm_w], [(di, F32), (di, MXU, zx.shape[1], 0)], [(1, di)] * 2)
    return dy, dzx, dnw, ddsk


def _ada_fwd(name, cs, w, b):
    nl, d, c = w.shape
    r = cs.shape[0]

    def body(cs_ref, w_ref, b_ref, o_ref):
        o_ref[...] = _dot(_silu(cs_ref[...]), w_ref[...], NN) + b_ref[...]

    return pl.pallas_call(
        body, name=name, grid=(nl,), out_shape=pltpu.HBM((nl, r, c), F32),
        in_specs=[pl.BlockSpec((r, d), lambda l: (0, 0)), pl.BlockSpec((None, d, c), lambda l: (l, 0, 0)),
                  pl.BlockSpec((None, 1, c), lambda l: (l, 0, 0))],
        out_specs=pl.BlockSpec((None, r, c), lambda l: (l, 0, 0)),
        compiler_params=pltpu.CompilerParams(dimension_semantics=("parallel",), vmem_limit_bytes=_vmem(2 * d * c * 4)),
    )(_hbm(cs), _hbm(w), _hbm(b))


def _ada_bwd(name, cs, w, dmod):
    nl, d, c = w.shape
    r = cs.shape[0]

    def body(cs_ref, w_ref, dm_ref, dw_ref, dsc_ref):
        dm = dm_ref[...]
        dw_ref[...] = _dot(_silu(cs_ref[...]), dm, TN)

        @pl.when(pl.program_id(0) == 0)
        def _():
            dctx = jnp.broadcast_to(_colsum(dm[r // 2:]), (8, c))
            dsc_ref[...] = _dot(dctx, w_ref[...], NT)[0:1]

    return pl.pallas_call(
        body, name=name, grid=(nl,), out_shape=[pltpu.HBM((nl, d, c), F32), pltpu.HBM((1, d), F32)],
        in_specs=[pl.BlockSpec((r, d), lambda l: (0, 0)), pl.BlockSpec((None, d, c), lambda l: (l, 0, 0)),
                  pl.BlockSpec((None, r, c), lambda l: (l, 0, 0))],
        out_specs=[pl.BlockSpec((None, d, c), lambda l: (l, 0, 0)), pl.BlockSpec((1, d), lambda l: (0, 0))],
        compiler_params=pltpu.CompilerParams(dimension_semantics=("arbitrary",), vmem_limit_bytes=_vmem(4 * d * c * 4)),
    )(_hbm(cs), _hbm(w), _hbm(dmod))


def _adam_math(w, g, m, v):
    m = ADAM_B1 * m + (1.0 - ADAM_B1) * g
    v = ADAM_B2 * v + (1.0 - ADAM_B2) * (g * g)
    m_hat = m / (1.0 - ADAM_B1 ** ADAM_STEP)
    v_hat = v / (1.0 - ADAM_B2 ** ADAM_STEP)
    delta = -ADAM_LR * (m_hat / (jnp.sqrt(v_hat) + ADAM_EPS) + ADAM_WD * w)
    return delta, m, v


def _adam(name, slots, w, m, v, comm=None):
    segs = list(slots) if isinstance(slots, (list, tuple)) else [slots]
    nseg = len(segs)
    ns, c = segs[0].shape[0], segs[0].shape[2]
    r = sum(s.shape[1] for s in segs)
    tr = _pick(min(s.shape[1] for s in segs), (256, 128, 64, 32, 16, 8))
    starts = [sum(s.shape[1] for s in segs[:k]) // tr for k in range(nseg)]

    def body(*refs):
        s_refs, (w_ref, m_ref, v_ref), (g_ref, d_ref, mo_ref, vo_ref) = refs[:nseg], refs[nseg:nseg + 3], refs[nseg + 3:]
        i = pl.program_id(0)

        def total(s_ref):
            g = s_ref[0].astype(F32)
            for k in range(1, ns):
                g = g + s_ref[k].astype(F32)
            return g

        g = total(s_refs[0])
        for k in range(1, nseg):
            g = jnp.where(i >= starts[k], total(s_refs[k]), g)
        d, mn, vn = _adam_math(w_ref[...], g, m_ref[...], v_ref[...])
        g_ref[...], d_ref[...], mo_ref[...], vo_ref[...] = g, d, mn, vn

    blk = pl.BlockSpec((tr, c), lambda i: (i, 0))
    seg_specs = [pl.BlockSpec((ns, tr, c), lambda i, st=starts[k], nt=segs[k].shape[1] // tr: (0, jnp.clip(i - st, 0, nt - 1), 0))
                 for k in range(nseg)]
    cm = comm if comm is not None else _Comm()
    res = pl.pallas_call(
        _carry(body, comm, nseg + 3, 4, (r // tr,)), name=name, grid=(r // tr,),
        out_shape=[pltpu.HBM((r, c), F32)] * 4 + cm.out_shape,
        in_specs=seg_specs + [blk, blk, blk] + cm.specs, out_specs=[blk] * 4 + cm.specs,
        scratch_shapes=cm.scratch if comm is not None else [],
        compiler_params=pltpu.CompilerParams(dimension_semantics=("arbitrary",), vmem_limit_bytes=_vmem(16 * nseg * tr * c * 4)),
    )(*[_hbm(s) for s in segs], _hbm(w), _hbm(m), _hbm(v), *cm.operands)
    return res[:4] if comm is None else (res[:4], cm.split(res[4:]))


def _adam_small(name, slots, ws, ms, vs, scale=None):
    k = len(slots)

    def body(*refs):
        s_refs, w_refs, m_refs, v_refs = refs[:k], refs[k:2 * k], refs[2 * k:3 * k], refs[3 * k:4 * k]
        sc_ref = refs[4 * k] if scale is not None else None
        outs = refs[4 * k + (scale is not None):]
        for a in range(k):
            g = s_refs[a][0]
            for j in range(1, NDEV):
                g = g + s_refs[a][j]
            if scale is not None and a == scale[0]:
                g = g * _dsilu(sc_ref[...])
            d, mn, vn = _adam_math(w_refs[a][...], g, m_refs[a][...], v_refs[a][...])
            outs[a][...], outs[k + a][...], outs[2 * k + a][...], outs[3 * k + a][...] = g, d, mn, vn

    shapes = [pltpu.HBM(w.shape, F32) for w in ws]
    extra = [scale[1]] if scale is not None else []
    ins = [*slots, *ws, *ms, *vs, *extra]

    def whole(shape):
        return pl.BlockSpec(shape, lambda i, nd=len(shape): (0,) * nd)

    res = pl.pallas_call(body, name=name, grid=(1,), out_shape=shapes * 4, in_specs=[whole(v.shape) for v in ins],
                         out_specs=[whole(s.shape) for s in shapes * 4])(*[_hbm(v) for v in ins])
    return res[:k], res[k:2 * k], res[2 * k:3 * k], res[3 * k:]


def _unshard_cols(g):
    g = jnp.moveaxis(g, 0, -2)
    return g.reshape(g.shape[:-2] + (g.shape[-2] * g.shape[-1],))


def _shard_cols(a):
    a = a.reshape(a.shape[:-1] + (NDEV, a.shape[-1] // NDEV))
    return jnp.moveaxis(a, -2, 0)


def _unshard_rows(g):
    g = jnp.moveaxis(g, 0, -3)
    return g.reshape(g.shape[:-3] + (g.shape[-3] * g.shape[-2], g.shape[-1]))


def _shard_rows(a):
    a = a.reshape(a.shape[:-2] + (NDEV, a.shape[-2] // NDEV, a.shape[-1]))
    return jnp.moveaxis(a, -3, 0)


def _flat2(a):
    return a.reshape((-1, a.shape[-1]))


def kernel(x, c, ctx, c_ctx, ada_w, ada_b, norm_mix_g, norm_ffn_g, final_norm_g, ssd_w_in, ssd_conv_w, ssd_conv_b, ssd_dt_bias_f, ssd_dt_bias_b, ssd_a_log_f, ssd_a_log_b, ssd_d_skip, ssd_norm_w, ssd_w_out, conf_w_pw1, conf_b_pw1, conf_dw_w, conf_dw_b, conf_ln_g, conf_ln_b, conf_w_pw2, conf_b_pw2, ffn_w_in, ffn_w_out, loss_target, m_c_ctx, m_ada_w, m_ada_b, m_norm_mix_g, m_norm_ffn_g, m_final_norm_g, m_ssd_w_in, m_ssd_conv_w, m_ssd_conv_b, m_ssd_dt_bias_f, m_ssd_dt_bias_b, m_ssd_a_log_f, m_ssd_a_log_b, m_ssd_d_skip, m_ssd_norm_w, m_ssd_w_out, m_conf_w_pw1, m_conf_b_pw1, m_conf_dw_w, m_conf_dw_b, m_conf_ln_g, m_conf_ln_b, m_conf_w_pw2, m_conf_b_pw2, m_ffn_w_in, m_ffn_w_out, v_c_ctx, v_ada_w, v_ada_b, v_norm_mix_g, v_norm_ffn_g, v_final_norm_g, v_ssd_w_in, v_ssd_conv_w, v_ssd_conv_b, v_ssd_dt_bias_f, v_ssd_dt_bias_b, v_ssd_a_log_f, v_ssd_a_log_b, v_ssd_d_skip, v_ssd_norm_w, v_ssd_w_out, v_conf_w_pw1, v_conf_b_pw1, v_conf_dw_w, v_conf_dw_b, v_conf_ln_g, v_conf_ln_b, v_conf_w_pw2, v_conf_b_pw2, v_ffn_w_in, v_ffn_w_out):
    args = dict(locals())
    names = ['c_ctx', 'ada_w', 'ada_b', 'norm_mix_g', 'norm_ffn_g', 'final_norm_g', 'ssd_w_in', 'ssd_conv_w', 'ssd_conv_b',
             'ssd_dt_bias_f', 'ssd_dt_bias_b', 'ssd_a_log_f', 'ssd_a_log_b', 'ssd_d_skip', 'ssd_norm_w', 'ssd_w_out',
             'conf_w_pw1', 'conf_b_pw1', 'conf_dw_w', 'conf_dw_b', 'conf_ln_g', 'conf_ln_b', 'conf_w_pw2', 'conf_b_pw2',
             'ffn_w_in', 'ffn_w_out']
    me = 4 * lax.axis_index("x") + 2 * lax.axis_index("y") + lax.axis_index("c")
    t, d = x.shape[1], x.shape[2]
    tc = ctx.shape[1]
    tt = tc + t
    nct = tc // ROW_TILE
    assert tc % ROW_TILE == 0 and t % ROW_TILE == 0
    h = ssd_dt_bias_f.shape[-1]
    di = ssd_norm_w.shape[-1]
    cdim = ssd_conv_b.shape[-1]
    kc = ssd_conv_w.shape[1]
    ck = conf_dw_w.shape[1]
    ch = d // 2
    rows_g = t // GRID_W
    nl = ada_w.shape[0]
    cw = ada_w.shape[2]
    x2, ctx2, tgt = x[0], ctx[0], loss_target[0]

    (c_all, convw_g), _ = _exchange("gather_first", [c, ssd_conv_w[0]])
    ride_norm = _Comm(gather=[ssd_w_in[0].astype(WIRE)])
    ride_proj = _Comm(gather=[ssd_w_out[0].astype(WIRE), conf_w_pw2[0].astype(WIRE)])
    ride_conv = _Comm(gather=[conf_w_pw1[0].astype(WIRE), conf_b_pw1, conf_dw_w[0], conf_dw_b, conf_ln_g, conf_ln_b, conf_b_pw2])
    ride_scan = _Comm(gather=[ffn_w_in[0].astype(WIRE), ffn_w_in[1].astype(WIRE), ffn_w_out[0].astype(WIRE), ffn_w_out[1].astype(WIRE)])
    conv_w_full = _unshard_cols(convw_g)

    cs_all = jnp.concatenate([c_all[:, 0, :], jnp.broadcast_to(c_ctx[None, :], (NDEV, d))], axis=0)
    ada_b_mine = lax.dynamic_slice_in_dim(ada_b, me * cw, cw, axis=1)[:, None, :]
    mod_part = _ada_fwd("ada_fwd", cs_all, ada_w, ada_b_mine)
    (mod_g,), _ = _exchange("gather_mod", [mod_part])
    mod_all = jnp.moveaxis(mod_g, 0, 2).reshape(nl, 2 * NDEV, NDEV * cw)
    mod_lat = lax.dynamic_slice_in_dim(mod_all, me, 1, axis=1)[:, 0, :]
    mod_ctx = mod_all[0, NDEV, :]

    def six(v):
        return [v[k * d:(k + 1) * d][None, :] for k in range(6)]

    sh1, s1, g1, sh2, s2, g2 = six(mod_lat[0])
    csh1, cs1 = six(mod_ctx)[:2]
    sh1b, s1b, g1b, sh2b, s2b, g2b = six(mod_lat[1])
    nmg, nfg = norm_mix_g, norm_ffn_g

    s01, sh01 = jnp.concatenate([cs1, s1], axis=0), jnp.concatenate([csh1, sh1], axis=0)
    xn_all, ((w_in_g,), _) = _normmod_fwd("l0_norm", ctx2, x2, nmg[0:1], s01, sh01, comm=ride_norm)
    w_ssd_in = _unshard_cols(w_in_g)
    w_dt = jnp.pad(w_ssd_in[:, di + cdim:], ((0, 0), (0, LANES - 2 * h)))
    zx, ((w_out_g, pw2_g), _) = _mm("ssd_in_proj", xn_all, w_ssd_in, "nn", ACT, comm=ride_proj, n_used=di + cdim)
    dtr = _mm("ssd_dt_proj", xn_all, w_dt, "nn")
    dt2 = jnp.moveaxis(dtr[:, :2 * h].reshape(tt, 2, h), 1, 0)
    bias2 = jnp.stack([ssd_dt_bias_f, ssd_dt_bias_b])
    alog2 = jnp.stack([ssd_a_log_f, ssd_a_log_b])
    xbc, xbc_pre, ((pw1_g, bpw1_g, dww_g, dwb_g, lng_g, lnb_g, bpw2_g), _) = _ssd_conv_fwd("ssd_conv", zx, conv_w_full, ssd_conv_b, di, tc,
                                                                                 comm=ride_conv)
    y2, hp2, (ffn_g, _) = _ssd_fwd("ssd_scan", xbc, dt2, bias2, alog2, di, tc, comm=ride_scan)
    w_ssd_out = _unshard_rows(w_out_g)
    w_pw1, w_pw2 = _unshard_cols(pw1_g), _unshard_rows(pw2_g)
    w_fin = [_unshard_cols(ffn_g[0]), _unshard_cols(ffn_g[1])]
    w_fout = [_unshard_rows(ffn_g[2]), _unshard_rows(ffn_g[3])]
    dw_w_full = _unshard_cols(dww_g)
    b_pw1, dw_b, ln_g, ln_b, b_pw2 = (_unshard_cols(a) for a in (bpw1_g, dwb_g, lng_g, lnb_g, bpw2_g))
    dskip_e = jnp.repeat(ssd_d_skip, di // h, axis=1)
    yn = _ssd_gate_fwd("ssd_gate", y2, xbc, zx, dskip_e, ssd_norm_w, di, nct, t)
    mix0 = _mm("ssd_out_proj", yn, w_ssd_out, "nn")
    h1, xf0 = _resnorm_fwd("l0_res_norm", x2, mix0, g1, nfg[0:1], s2, sh2)
    u0 = _mm("ffn0_in", xf0, w_fin[0], "nn", ACT)
    hid0 = _swiglu_fwd("ffn0_act", u0)
    f0 = _mm("ffn0_out", hid0, w_fout[0], "nn")
    h2, xn1 = _resnorm_fwd("l1_norm", h1, f0, g2, nmg[1:2], s1b, sh1b)
    u1 = _mm("conf_pw1", xn1, w_pw1, "nn", ACT, bias=b_pw1)
    gl = _glu_fwd("conf_glu", u1)
    gl_h = _grid_t(gl[:, :ch], rows_g, GRID_W)
    v_ht = _strided_conv("conf_conv_h", gl_h, dw_w_full[:, :ch], dw_b[:, :ch], rows_g)
    v_v = _strided_conv("conf_conv_v", gl, dw_w_full[:, ch:], dw_b[:, ch:], GRID_W, x_col0=ch)
    v_h = _grid_t(v_ht, GRID_W, rows_g)
    sl = _ln_silu_fwd("conf_ln", v_h, v_v, ln_g, ln_b)
    mix1 = _mm("conf_pw2", sl, w_pw2, "nn", bias=b_pw2)
    h3, xf1 = _resnorm_fwd("l1_res_norm", h2, mix1, g1b, nfg[1:2], s2b, sh2b)
    u2 = _mm("ffn1_in", xf1, w_fin[1], "nn", ACT)
    hid1 = _swiglu_fwd("ffn1_act", u2)
    f1 = _mm("ffn1_out", hid1, w_fout[1], "nn")
    dh, sq, d_final_g = _final_loss("final_loss", h3, f1, tgt, g2b, final_norm_g[None, :])

    zero2 = jnp.zeros((2, d), F32)

    def ffn_bwd(tag, gated, dh, hin, xf, u, hid, w_in, w_out, g_norm, s_mod, branch):
        df, dgate = gated
        dhid = _mm(tag + "_dhid", df, w_out, "nt", ACT)
        dw_out = _mm(tag + "_dwout", hid, df, "tn", WIRE)
        du = _swiglu_bwd(tag + "_act_bwd", u, dhid)
        dw_in = _mm(tag + "_dwin", xf, du, "tn", WIRE)
        dxf = _mm(tag + "_dx", du, w_in, "nt")
        s_2 = jnp.concatenate([s_mod, s_mod], axis=0)
        dh, dsh, ds, dg, nxt = _normmod_bwd(tag + "_norm_bwd", None, hin, dxf, dh, g_norm, s_2, branch=branch)
        return dh, dgate, dsh[1:2], ds[1:2], dg[1:2], dw_in, dw_out, nxt

    df1, d_g2b_, _ = _gate_bwd("ffn1_gate_bwd", dh, f1, g2b)
    dh, d_g2b, d_sh2b, d_s2b, d_nfg1, g_fin1, g_fout1, (dmix1, d_g1b, g_bpw2) = ffn_bwd(
        "ffn1", (df1, d_g2b_), dh, h3, xf1, u2, hid1, w_fin[1], w_fout[1], nfg[1:2], s2b, (mix1, g1b))
    dsl = _mm("conf_dsl", dmix1, w_pw2, "nt")
    g_pw2 = _mm("conf_dwpw2", sl, dmix1, "tn", WIRE)
    dv_lo, dv_v, g_lng, g_lnb = _ln_silu_bwd("conf_ln_bwd", v_h, v_v, dsl, ln_g, ln_b)
    dv_h = _grid_t(dv_lo, rows_g, GRID_W)
    w_flip = dw_w_full[::-1]
    dgl_h = _strided_conv("conf_conv_h_bwd", dv_h, w_flip[:, :ch], None, rows_g)
    dgl_v = _strided_conv("conf_conv_v_bwd", dv_v, w_flip[:, ch:], None, GRID_W)
    g_dww_h, g_dwb_h = _strided_conv_dw("conf_conv_h_dw", gl_h, dv_h, ck, rows_g)
    g_dww_v, g_dwb_v = _strided_conv_dw("conf_conv_v_dw", gl, dv_v, ck, GRID_W, x_col0=ch)
    g_dww, g_dwb = jnp.concatenate([g_dww_h, g_dww_v], axis=1), jnp.concatenate([g_dwb_h, g_dwb_v], axis=1)
    du1, g_bpw1 = _glu_bwd("conf_glu_bwd", u1, _grid_t(dgl_h, GRID_W, rows_g), dgl_v)
    g_pw1 = _mm("conf_dwpw1", xn1, du1, "tn", WIRE)
    dxn1 = _mm("conf_dx", du1, w_pw1, "nt")
    dh, dsh_, ds_, dg_, (df0, d_g2_, _) = _normmod_bwd("l1_norm_bwd", None, h2, dxn1, dh, nmg[1:2],
                                                       jnp.concatenate([s1b, s1b], axis=0), branch=(f0, g2))
    d_sh1b, d_s1b, d_nmg1 = dsh_[1:2], ds_[1:2], dg_[1:2]
    dh, d_g2, d_sh2, d_s2, d_nfg0, g_fin0, g_fout0, (dmix0, d_g1, _) = ffn_bwd(
        "ffn0", (df0, d_g2_), dh, h1, xf0, u0, hid0, w_fin[0], w_fout[0], nfg[0:1], s2, (mix0, g1))
    dyn = _mm("ssd_dyn", dmix0, w_ssd_out, "nt")
    g_ssd_out = _mm("ssd_dwout", yn, dmix0, "tn", WIRE)
    dy, dzx, g_normw, ddsk_e = _ssd_gate_bwd("ssd_gate_bwd", dyn, y2, xbc, zx, dskip_e, ssd_norm_w, di, nct, tt)
    ride_scan_bwd = _Comm(scatter=[_shard_cols(g_fin0), _shard_cols(g_fin1), _shard_rows(g_fout0), _shard_rows(g_fout1)])
    ride_conv_bwd = _Comm(scatter=[_shard_rows(g_ssd_out), _shard_cols(g_pw1), _shard_rows(g_pw2), _shard_cols(g_bpw1),
                                   _shard_cols(g_dww), _shard_cols(g_dwb), _shard_cols(g_lng), _shard_cols(g_lnb), _shard_cols(g_bpw2)])
    dxbc2, ddt2, g_alog2, g_bias2, (_, ffn_r) = _ssd_bwd("ssd_scan_bwd", xbc, dt2, bias2, alog2, dy, hp2, dskip_e, di, tc,
                                                         comm=ride_scan_bwd)
    dzx, g_convw, g_convb, (_, conv_r) = _ssd_conv_bwd("ssd_conv_bwd", zx, dxbc2, conv_w_full, xbc_pre, dzx, di, tc,
                                                       comm=ride_conv_bwd)
    ddt_p = jnp.pad(jnp.moveaxis(ddt2, 0, 1).reshape(tt, 2 * h), ((0, 0), (0, LANES - 2 * h))).astype(MXU)
    g_ssd_in = jnp.concatenate([_mm("ssd_dw_zx", xn_all, dzx, "tn", WIRE),
                                _mm("ssd_dw_dt", xn_all, ddt_p, "tn", WIRE)[:, :2 * h]], axis=1)
    dxn, (_, (ssd_in_r, convw_r)) = _mm("ssd_dx_zx", dzx, w_ssd_in, "nt",
                                        comm=_Comm(scatter=[_shard_cols(g_ssd_in), _shard_cols(g_convw)]))
    dxn = _mm("ssd_dx_dt", ddt_p, w_dt, "nt", add=dxn)
    dh0, dsh_, ds_, dg_ = _normmod_bwd("l0_norm_bwd", ctx2, x2, dxn, dh, nmg[0:1], s01)
    grad_x = dh0[None]
    d_csh1, d_sh1, d_cs1, d_s1 = dsh_[0:1], dsh_[1:2], ds_[0:1], ds_[1:2]
    d_nmg0 = dg_[0:1] + dg_[1:2]

    z1 = jnp.zeros((1, d), F32)
    dmod = jnp.concatenate([jnp.concatenate([d_sh1, d_s1, d_g1, d_sh2, d_s2, d_g2], axis=1),
                            jnp.concatenate([d_sh1b, d_s1b, d_g1b, d_sh2b, d_s2b, d_g2b], axis=1),
                            jnp.concatenate([d_csh1, d_cs1, z1, z1, z1, z1], axis=1)], axis=0)
    out = {}

    def put(name, res):
        w = args[name]
        out["grad_" + name], out["delta_" + name], out["new_m_" + name], out["new_v_" + name] = (r.reshape(w.shape) for r in res)

    def adam_big(name, slots, comm=None):
        return _adam("adam_" + name, slots, _flat2(args[name]), _flat2(args["m_" + name]), _flat2(args["v_" + name]), comm=comm)

    res, ((dmod_g,), _) = adam_big("ffn_w_in", [ffn_r[0], ffn_r[1]], comm=_Comm(gather=[dmod]))
    put("ffn_w_in", res)
    dmod_mine = lax.dynamic_slice_in_dim(dmod_g, me * cw, cw, axis=2)
    dmod16 = jnp.stack([jnp.concatenate([dmod_mine[:, 0], dmod_mine[:, 2]], axis=0),
                        jnp.concatenate([dmod_mine[:, 1], jnp.zeros((NDEV, cw), F32)], axis=0)])
    g_ada_w, dsc_part = _ada_bwd("ada_bwd", cs_all, ada_w, dmod16)
    g_ada_b = dmod[0:2] + jnp.concatenate([dmod[2:3], jnp.zeros((1, 6 * d), F32)], axis=0)

    d_dskip = jnp.sum(ddsk_e.reshape(h, di // h), axis=1)[None, :]
    rep = [dsc_part, g_ada_b, jnp.concatenate([d_nmg0, d_nmg1], axis=0), jnp.concatenate([d_nfg0, d_nfg1], axis=0),
           d_final_g, g_convb, g_bias2[0], g_bias2[1], g_alog2[0], g_alog2[1], d_dskip, g_normw]
    res, (rep_g, _) = _adam("adam_ada_w", _flat2(g_ada_w)[None], _flat2(ada_w), _flat2(m_ada_w), _flat2(v_ada_w),
                            comm=_Comm(gather=rep + [sq]))
    put("ada_w", res)
    rep_g, sq_g = rep_g[:-1], rep_g[-1]
    loss = (0.5 / d) * jnp.sum(sq_g[:, 0, 0])
    small_r = [convw_r] + list(conv_r[3:])

    for name, slots in zip(["ssd_w_in", "ssd_w_out", "conf_w_pw1", "conf_w_pw2", "ffn_w_out"],
                           [ssd_in_r, conv_r[0], conv_r[1], conv_r[2], [ffn_r[2], ffn_r[3]]]):
        put(name, adam_big(name, slots))
    small_names = ["ssd_conv_w", "conf_b_pw1", "conf_dw_w", "conf_dw_b", "conf_ln_g", "conf_ln_b", "conf_b_pw2",
                   "c_ctx", "ada_b", "norm_mix_g", "norm_ffn_g", "final_norm_g", "ssd_conv_b", "ssd_dt_bias_f", "ssd_dt_bias_b",
                   "ssd_a_log_f", "ssd_a_log_b", "ssd_d_skip", "ssd_norm_w"]
    slots = list(small_r) + list(rep_g)

    def as2(a):
        return a.reshape((1, -1)) if a.ndim == 1 else _flat2(a)

    res = _adam_small("adam_small", slots, [as2(args[n]) for n in small_names], [as2(args["m_" + n]) for n in small_names],
                      [as2(args["v_" + n]) for n in small_names], scale=(small_names.index("c_ctx"), c_ctx[None, :]))
    for k, name in enumerate(small_names):
        put(name, [r[k] for r in res])
    return (loss, grad_x, *[out["grad_" + n] for n in names], *[out["delta_" + n] for n in names],
            *[out["new_m_" + n] for n in names], *[out["new_v_" + n] for n in names])


def _flat3(a):
    return a.reshape((a.shape[0], -1, a.shape[-1]))
```

```python
import functools

import jax
import jax.numpy as jnp
from jax import lax
from jax.experimental import pallas as pl
from jax.experimental.pallas import tpu as pltpu

F32 = jnp.float32
MXU = jnp.bfloat16
WIRE = jnp.bfloat16
ACT = jnp.bfloat16
NDEV = 8
AXES = ("x", "y", "c")
SSD_STATE = 128
SSD_CHUNK = 128
GRID_W = 64
EPS = 1e-6
ROW_TILE = 256
LANES = 128
ADAM_LR, ADAM_B1, ADAM_B2, ADAM_EPS, ADAM_WD, ADAM_STEP = 0.001, 0.9, 0.999, 1e-08, 0.01, 10
VMEM_CAP = 56 * 2 ** 20
MESH_ID = pl.DeviceIdType.MESH


def _pick(dim, cands):
    for c in cands:
        if dim % c == 0:
            return c
    return dim


def _nbytes(shape, dtype):
    n = 1
    for s in shape:
        n *= s
    return n * jnp.dtype(dtype).itemsize


def _vmem(nbytes):
    return int(min(VMEM_CAP, max(24 * 2 ** 20, 2 * nbytes + 8 * 2 ** 20)))


def _sigmoid(x):
    return 1.0 / (1.0 + jnp.exp(-x))


def _silu(x):
    return x * _sigmoid(x)


def _dsilu(x):
    s = _sigmoid(x)
    return s * (1.0 + x * (1.0 - s))


def _softplus(x):
    return jnp.maximum(x, 0.0) + jnp.log(1.0 + jnp.exp(-jnp.abs(x)))


def _dot(a, b, dims):
    return lax.dot_general(a.astype(MXU), b.astype(MXU), (dims, ((), ())), preferred_element_type=F32)


NN, NT, TN = ((1,), (0,)), ((1,), (1,)), ((0,), (0,))


def _split(a, parts):
    out = []
    for _ in range(parts):
        p = a.astype(MXU)
        out.append(p)
        a = a - p.astype(F32)
    return out


def _dot_lx(e, a, dims, parts=2):
    return sum(lax.dot_general(e, p, (dims, ((), ())), preferred_element_type=F32) for p in _split(a, parts))


def _dot_rx(a, e, dims, parts=2):
    return sum(lax.dot_general(p, e, (dims, ((), ())), preferred_element_type=F32) for p in _split(a, parts))


class _Comm:
    def __init__(self, gather=(), scatter=()):
        self.gather, self.scatter = list(gather), list(scatter)
        self.ng, self.n = len(self.gather), len(self.gather) + len(self.scatter)
        self.operands = self.gather + self.scatter
        self.specs = [pl.BlockSpec(memory_space=pl.ANY)] * self.n
        self.out_shape = ([jax.ShapeDtypeStruct((NDEV,) + a.shape, a.dtype) for a in self.gather]
                          + [jax.ShapeDtypeStruct(a.shape, a.dtype) for a in self.scatter])
        self.scratch = [pltpu.SemaphoreType.DMA((self.n, 7)), pltpu.SemaphoreType.DMA((self.n, 7)),
                        pltpu.SemaphoreType.DMA((self.n,))]

    def split(self, res):
        return res[:self.ng], res[self.ng:]

    def _copies(self, ins, outs, sems):
        send, recv, loc = sems
        ng, n = self.ng, self.n
        x, y, c = lax.axis_index("x"), lax.axis_index("y"), lax.axis_index("c")
        me, sib = (x, y, c), (x, y, 1 - c)
        chips = [(1 - x, y), (x, 1 - y), (1 - x, 1 - y)]

        def slot(p):
            return 4 * p[0] + 2 * p[1] + p[2]

        def rcopy(a, k, src, dst, to):
            return functools.partial(pltpu.make_async_remote_copy, src_ref=src, dst_ref=dst, send_sem=send.at[a, k],
                                     recv_sem=recv.at[a, k], device_id=to, device_id_type=MESH_ID)

        local = [functools.partial(pltpu.make_async_copy, ins[a] if a < ng else ins[a].at[slot(me)], outs[a].at[slot(me)],
                                   loc.at[a]) for a in range(n)]
        rel = [(fx, fy, fc) for fx in (0, 1) for fy in (0, 1) for fc in (0, 1)][1:]
        first, landed, passed = [], [], []
        for a in range(ng, n):
            for k, (fx, fy, fc) in enumerate(rel):
                p = (1 - x if fx else x, 1 - y if fy else y, 1 - c if fc else c)
                first.append(rcopy(a, k, ins[a].at[slot(p)], outs[a].at[slot(me)], p))
                blk = outs[a].at[slot(p)]
                landed.append(rcopy(a, k, blk, blk, me))
        for a in range(ng):
            dst = outs[a].at[slot(me)]
            first.append(rcopy(a, 0, ins[a], dst, sib))
            first += [rcopy(a, 1 + j, ins[a], dst, (*ch, c)) for j, ch in enumerate(chips)]
            blk = outs[a].at[slot(sib)]
            landed.append(rcopy(a, 0, blk, blk, me))
            for j, ch in enumerate(chips):
                blk = outs[a].at[slot((*ch, c))]
                passed.append((rcopy(a, 1 + j, blk, blk, me), rcopy(a, 4 + j, blk, blk, sib)))
                blk = outs[a].at[slot((*ch, 1 - c))]
                landed.append(rcopy(a, 4 + j, blk, blk, me))
        return local, first, passed, landed

    def start(self, ins, outs, sems):
        local, first, _, _ = self._copies(ins, outs, sems)
        for make in local + first:
            make().start()

    def finish(self, ins, outs, sems):
        local, first, passed, landed = self._copies(ins, outs, sems)
        onward = []
        for arrived, forward in passed:
            arrived().wait_recv()
            onward.append(forward())
            onward[-1].start()
        for make in landed:
            make().wait_recv()
        for make in first:
            make().wait_send()
        for cp in onward:
            cp.wait_send()
        for make in local:
            make().wait()


def _carry(body, comm, n_in, n_out, grid):
    if comm is None:
        return body
    n = comm.n

    def wrapped(*refs):
        own_in, c_in = refs[:n_in], refs[n_in:n_in + n]
        own_out, c_out = refs[n_in + n:n_in + n + n_out], refs[n_in + n + n_out:n_in + 2 * n + n_out]
        own_scr, sems = refs[n_in + 2 * n + n_out:-3], refs[-3:]
        ids = [pl.program_id(ax) for ax in range(len(grid))]
        first, last = ids[0] == 0, ids[0] == grid[0] - 1
        for ax in range(1, len(grid)):
            first, last = first & (ids[ax] == 0), last & (ids[ax] == grid[ax] - 1)

        @pl.when(first)
        def _():
            comm.start(c_in, c_out, sems)

        body(*own_in, *own_out, *own_scr)

        @pl.when(last)
        def _():
            comm.finish(c_in, c_out, sems)

    return wrapped


def _exchange(name, gather, scatter=()):
    comm = _Comm(gather, scatter)
    n = comm.n

    def body(*refs):
        ins, outs, sems = refs[:n], refs[n:2 * n], refs[2 * n:]
        comm.start(ins, outs, sems)
        comm.finish(ins, outs, sems)

    res = pl.pallas_call(body, name=name, out_shape=comm.out_shape, in_specs=comm.specs, out_specs=comm.specs,
                         scratch_shapes=comm.scratch)(*comm.operands)
    return comm.split(res)


def _hbm(a):
    return pltpu.with_memory_space_constraint(a, pltpu.HBM)


def _divs(dim, mult):
    return [dim] + [dim // parts for parts in range(2, dim // mult + 1) if dim % parts == 0 and (dim // parts) % mult == 0]


MM_VMEM_BUDGET = 40 * 2 ** 20
GRID_STEP_US = 0.35
HBM_BYTES_PER_US = 3.0e6


def _mm_tiles(m, n, k, sizes, mode, has_add):
    sa, sb, so = sizes
    sub = 16
    best = None
    for tk in _divs(k, LANES):
        for tn in _divs(n, LANES):
            for tm in _divs(m, LANES if mode == "tn" else sub):
                nk = k // tk
                out_t = tm * tn
                est = (2 * (tm * tk * sa + tk * tn * sb) + 2 * out_t * so + 2 * (tm * tk + tk * tn) + 4 * out_t
                       + (4 * out_t if nk > 1 else 0) + (8 * out_t if has_add else 0))
                if est > MM_VMEM_BUDGET:
                    continue
                steps = (m // tm) * (n // tn) * nk
                cost = steps * GRID_STEP_US + (tm * tk * sa + tk * tn * sb + out_t * so) / HBM_BYTES_PER_US
                if best is None or cost < best[0]:
                    best = (cost, tm, tn, tk, est)
    assert best is not None, (m, n, k)
    return best[1:]


def _mm(name, a, b, mode, out_dtype=F32, bias=None, add=None, comm=None, n_used=None):
    if mode == "nn":
        (m, k), (k2, n) = a.shape, b.shape
        n = n if n_used is None else n_used
    elif mode == "nt":
        (m, k), (n, k2) = a.shape, b.shape
        k2 = min(k, k2)
    else:
        (k, m), (k2, n) = a.shape, b.shape
    assert k == k2, (name, a.shape, b.shape)
    sizes = (a.dtype.itemsize, b.dtype.itemsize, jnp.dtype(out_dtype).itemsize)
    tm, tn, tk, est = _mm_tiles(m, n, k, sizes, mode, add is not None)
    nk = k // tk
    dims = {"nn": NN, "nt": NT, "tn": TN}[mode]
    a_spec = pl.BlockSpec((tk, tm), lambda i, j, kk: (kk, i)) if mode == "tn" else pl.BlockSpec((tm, tk), lambda i, j, kk: (i, kk))
    b_spec = pl.BlockSpec((tn, tk), lambda i, j, kk: (j, kk)) if mode == "nt" else pl.BlockSpec((tk, tn), lambda i, j, kk: (kk, j))
    extra, extra_specs = [], []
    if bias is not None:
        extra.append(bias)
        extra_specs.append(pl.BlockSpec((1, tn), lambda i, j, kk: (0, j)))
    if add is not None:
        extra.append(add)
        extra_specs.append(pl.BlockSpec((tm, tn), lambda i, j, kk: (i, j)))

    def finish(r, extras, o_ref):
        for e in extras:
            r = r + e[...].astype(F32)
        o_ref[...] = r.astype(o_ref.dtype)

    def body_acc(*refs):
        a_ref, b_ref = refs[:2]
        o_ref, acc = refs[-2:]
        kk = pl.program_id(2)

        @pl.when(kk == 0)
        def _():
            acc[...] = jnp.zeros_like(acc)

        acc[...] += _dot(a_ref[...], b_ref[...], dims)

        @pl.when(kk == nk - 1)
        def _():
            finish(acc[...], refs[2:-2], o_ref)

    def body_one(*refs):
        finish(_dot(refs[0][...], refs[1][...], dims), refs[2:-1], refs[-1])

    cm = comm if comm is not None else _Comm()
    grid = (m // tm, n // tn, nk)
    res = pl.pallas_call(
        _carry(body_acc if nk > 1 else body_one, comm, 2 + len(extra), 1, grid), name=name, grid=grid,
        out_shape=[pltpu.HBM((m, n), out_dtype)] + cm.out_shape,
        in_specs=[a_spec, b_spec] + extra_specs + cm.specs,
        out_specs=[pl.BlockSpec((tm, tn), lambda i, j, kk: (i, j))] + cm.specs,
        scratch_shapes=([pltpu.VMEM((tm, tn), F32)] if nk > 1 else []) + (cm.scratch if comm is not None else []),
        compiler_params=pltpu.CompilerParams(
            dimension_semantics=("parallel", "parallel", "arbitrary") if comm is None else ("arbitrary",) * 3,
            vmem_limit_bytes=int(min(VMEM_CAP, est + 12 * 2 ** 20))),
    )(*[_hbm(v) for v in (a, b, *extra)], *cm.operands)
    return res[0] if comm is None else (res[0], cm.split(res[1:]))


def _ri(arr, w=None, cb=0, ro=0, lead=None):
    return (arr, arr.shape[-1] if w is None else w, cb, ro, lead)


def _rowwise(name, fn, nrows, row_ins, bc_ins, outs, accs=(), comm=None):
    tr = min(ROW_TILE, nrows)
    assert nrows % tr == 0
    in_specs = []
    for (arr, w, cb, ro, lead) in row_ins:
        last = arr.shape[-2] // tr - 1
        if lead is None:
            in_specs.append(pl.BlockSpec((tr, w), lambda i, cb=cb, ro=ro, last=last: (jnp.clip(i + ro, 0, last), cb)))
        else:
            in_specs.append(pl.BlockSpec((None, tr, w),
                                         lambda i, cb=cb, ro=ro, lead=lead, last=last: (lead, jnp.clip(i + ro, 0, last), cb)))
    for arr in bc_ins:
        in_specs.append(pl.BlockSpec(arr.shape, lambda i, nd=arr.ndim: (0,) * nd))
    outs = [tuple(o) + (o[0], 0, 0)[len(o) - 2:] for o in outs]
    out_shape = [pltpu.HBM((nrows + ro * tr, total), dt) for _, dt, total, _, ro in outs] + [pltpu.HBM(s, F32) for s in accs]
    out_specs = ([pl.BlockSpec((tr, c), lambda i, cb=cb, ro=ro: (jnp.maximum(i + ro, 0), cb)) for c, _, _, cb, ro in outs]
                 + [pl.BlockSpec(s, lambda i: (0, 0)) for s in accs])
    nr, nb, no = len(row_ins), len(bc_ins), len(outs)

    def body(*refs):
        i = pl.program_id(0)
        rows = [r[...].astype(F32) for r in refs[:nr]]
        bcs = [r[...] for r in refs[nr:nr + nb]]
        o, a = fn(rows, bcs, i)
        for ref, val in zip(refs[nr + nb:nr + nb + no], o):
            ref[...] = val.astype(ref.dtype)
        for ref, val in zip(refs[nr + nb + no:], a):
            @pl.when(i == 0)
            def _(ref=ref, val=val):
                ref[...] = val

            @pl.when(i > 0)
            def _(ref=ref, val=val):
                ref[...] += val

    est = sum(tr * w * arr.dtype.itemsize for (arr, w, _, _, _) in row_ins) + sum(tr * o[0] * 4 for o in outs)
    cm = comm if comm is not None else _Comm()
    nout = no + len(accs)
    res = pl.pallas_call(
        _carry(body, comm, nr + nb, nout, (nrows // tr,)), name=name, grid=(nrows // tr,), out_shape=out_shape + cm.out_shape,
        in_specs=in_specs + cm.specs, out_specs=out_specs + cm.specs, scratch_shapes=cm.scratch if comm is not None else [],
        compiler_params=pltpu.CompilerParams(dimension_semantics=("arbitrary",), vmem_limit_bytes=_vmem(3 * est)),
    )(*[_hbm(r[0]) for r in row_ins], *[_hbm(v) for v in bc_ins], *cm.operands)
    if comm is None:
        return res[:no], res[no:]
    return res[:no], res[no:nout], cm.split(res[nout:])


def _colsum(v):
    return jnp.sum(v, axis=0, keepdims=True)


def _normmod_fwd(name, hc, h, g, s, sh, comm=None):
    d = h.shape[1]
    nct = 0 if hc is None else hc.shape[0] // ROW_TILE
    ins = [_ri(h)] if hc is None else [_ri(hc), _ri(h, ro=-nct)]

    def fn(rows, bcs, i):
        hh = rows[0] if hc is None else jnp.where(i < nct, rows[0], rows[1])
        g_, s_, sh_ = bcs
        s1 = jnp.where(i < nct, s_[0:1], s_[1:2])
        sh1 = jnp.where(i < nct, sh_[0:1], sh_[1:2])
        r = lax.rsqrt(jnp.mean(hh * hh, axis=-1, keepdims=True) + EPS)
        return [hh * r * g_ * (1.0 + s1) + sh1], []

    res = _rowwise(name, fn, h.shape[0] + nct * ROW_TILE, ins, [g, s, sh], [(d, MXU)], comm=comm)
    return res[0][0] if comm is None else (res[0][0], res[2])


def _normmod_bwd(name, hc, h, dxn, dres, g, s, branch=None):
    d = h.shape[1]
    nct = 0 if hc is None else hc.shape[0] // ROW_TILE
    hins = [_ri(h)] if hc is None else [_ri(hc), _ri(h, ro=-nct)]
    nh = len(hins)

    def fn(rows, bcs, i):
        hh = rows[0] if hc is None else jnp.where(i < nct, rows[0], rows[1])
        dx, dr = rows[nh], rows[nh + 1]
        g_, s_ = bcs[:2]
        ctx = i < nct
        s1 = jnp.where(ctx, s_[0:1], s_[1:2])
        r = lax.rsqrt(jnp.mean(hh * hh, axis=-1, keepdims=True) + EPS)
        hr = hh * r
        dy = dx * (1.0 + s1)
        u = dy * g_
        dh = r * u - hr * (r * r) * jnp.mean(u * hh, axis=-1, keepdims=True)
        dh = dh + jnp.where(ctx, 0.0, dr)

        def seg(v):
            v = _colsum(v)
            return jnp.concatenate([jnp.where(ctx, v, 0.0), jnp.where(ctx, 0.0, v)], axis=0)

        outs, accs = [dh], [seg(dx), seg(dx * hr * g_), seg(dy * hr)]
        if branch is not None:
            dyb = dh * bcs[2]
            outs.append(dyb)
            accs += [_colsum(dh * rows[nh + 2]), _colsum(dyb)]
        return outs, accs

    if branch is None:
        (dh,), (dsh, ds, dg) = _rowwise(name, fn, h.shape[0] + nct * ROW_TILE, hins + [_ri(dxn), _ri(dres, ro=-nct)], [g, s],
                                        [(d, F32, d, 0, -nct)], [(2, d)] * 3)
        return dh, dsh, ds, dg
    (dh, dyb), (dsh, ds, dg, dgate, dbias) = _rowwise(name, fn, h.shape[0], hins + [_ri(dxn), _ri(dres), _ri(branch[0])],
                                                      [g, s, branch[1]], [(d, F32), (d, MXU)], [(2, d)] * 3 + [(1, d)] * 2)
    return dh, dsh, ds, dg, (dyb, dgate, dbias)


def _resnorm_fwd(name, h, y, gate, g, s, sh):
    d = h.shape[1]

    def fn(rows, bcs, i):
        hh, yy = rows
        gate_, g_, s_, sh_ = bcs
        hn = hh + gate_ * yy
        r = lax.rsqrt(jnp.mean(hn * hn, axis=-1, keepdims=True) + EPS)
        return [hn, hn * r * g_ * (1.0 + s_) + sh_], []

    return _rowwise(name, fn, h.shape[0], [_ri(h), _ri(y)], [gate, g, s, sh], [(d, F32), (d, MXU)])[0]


def _gate_bwd(name, dh, y, gate):
    d = dh.shape[1]

    def fn(rows, bcs, i):
        dd, yy = rows
        dy = dd * bcs[0]
        return [dy], [_colsum(dd * yy), _colsum(dy)]

    (dy,), (dgate, dbias) = _rowwise(name, fn, dh.shape[0], [_ri(dh), _ri(y)], [gate], [(d, MXU)], [(1, d)] * 2)
    return dy, dgate, dbias


def _swiglu_fwd(name, u):
    f = u.shape[1] // 2

    def fn(rows, bcs, i):
        return [_silu(rows[0]) * rows[1]], []

    return _rowwise(name, fn, u.shape[0], [_ri(u, f, 0), _ri(u, f, 1)], [], [(f, MXU)])[0][0]


def _swiglu_bwd(name, u, dhid):
    f = u.shape[1] // 2

    def fn(rows, bcs, i):
        a, b, dd = rows
        return [jnp.concatenate([dd * b * _dsilu(a), dd * _silu(a)], axis=1)], []

    return _rowwise(name, fn, u.shape[0], [_ri(u, f, 0), _ri(u, f, 1), _ri(dhid)], [], [(2 * f, MXU)])[0][0]


def _glu_fwd(name, u):
    d = u.shape[1] // 2

    def fn(rows, bcs, i):
        return [rows[0] * _sigmoid(rows[1])], []

    return _rowwise(name, fn, u.shape[0], [_ri(u, d, 0), _ri(u, d, 1)], [], [(d, F32)])[0][0]


def _glu_bwd(name, u, dgl_lo, dgl_hi):
    d = u.shape[1] // 2

    def fn(rows, bcs, i):
        a, b = rows[:2]
        dd = jnp.concatenate(rows[2:], axis=1)
        sg = _sigmoid(b)
        du = jnp.concatenate([dd * sg, dd * a * sg * (1.0 - sg)], axis=1)
        return [du], [_colsum(du)]

    (du,), (db,) = _rowwise(name, fn, u.shape[0], [_ri(u, d, 0), _ri(u, d, 1), _ri(dgl_lo), _ri(dgl_hi)], [], [(2 * d, MXU)],
                            [(1, 2 * d)])
    return du, db


def _ln_silu_fwd(name, v_lo, v_hi, g, b):
    d = 2 * v_lo.shape[1]

    def fn(rows, bcs, i):
        vv = jnp.concatenate(rows, axis=1)
        mu = jnp.mean(vv, axis=-1, keepdims=True)
        xc = vv - mu
        rs = lax.rsqrt(jnp.mean(xc * xc, axis=-1, keepdims=True) + EPS)
        return [_silu(xc * rs * bcs[0] + bcs[1])], []

    return _rowwise(name, fn, v_lo.shape[0], [_ri(v_lo), _ri(v_hi)], [g, b], [(d, MXU)])[0][0]


def _ln_silu_bwd(name, v_lo, v_hi, ds, g, b):
    ch = v_lo.shape[1]

    def fn(rows, bcs, i):
        vv, dd = jnp.concatenate(rows[:2], axis=1), rows[2]
        mu = jnp.mean(vv, axis=-1, keepdims=True)
        xc = vv - mu
        rs = lax.rsqrt(jnp.mean(xc * xc, axis=-1, keepdims=True) + EPS)
        xh = xc * rs
        dln = dd * _dsilu(xh * bcs[0] + bcs[1])
        dxh = dln * bcs[0]
        dv = rs * (dxh - jnp.mean(dxh, axis=-1, keepdims=True) - xh * jnp.mean(dxh * xh, axis=-1, keepdims=True))
        return [dv[:, :ch], dv[:, ch:]], [_colsum(dln * xh), _colsum(dln)]

    (dv_lo, dv_hi), (dg, db) = _rowwise(name, fn, v_lo.shape[0], [_ri(v_lo), _ri(v_hi), _ri(ds)], [g, b],
                                        [(ch, F32), (ch, F32)], [(1, 2 * ch)] * 2)
    return dv_lo, dv_hi, dg, db


def _final_loss(name, h, f, target, gate, gf):
    d = h.shape[1]

    def fn(rows, bcs, i):
        hh, ff, tg = rows
        gate_, g_ = bcs
        hn = hh + gate_ * ff
        r = lax.rsqrt(jnp.mean(hn * hn, axis=-1, keepdims=True) + EPS)
        hr = hn * r
        err = hr * g_ - tg
        dout = err * (1.0 / d)
        u = dout * g_
        dh = r * u - hr * (r * r) * jnp.mean(u * hn, axis=-1, keepdims=True)
        sq = jnp.sum(_colsum(err * err), axis=1, keepdims=True)
        return [dh, dh * gate_], [jnp.broadcast_to(sq, (1, LANES)), _colsum(dout * hr), _colsum(dh * ff)]

    (dh, df), (sq, dgf, dgate) = _rowwise(name, fn, h.shape[0], [_ri(h), _ri(f), _ri(target)], [gate, gf], [(d, F32), (d, MXU)],
                                          [(1, LANES), (1, d), (1, d)])
    return dh, sq, dgf, df, dgate


GAP = 8


def _gapped(ref_rows, buf, tc, tt):
    cb = buf.shape[1]
    zero = jnp.zeros((GAP, cb), F32)
    buf[0:GAP, :] = zero
    buf[GAP + tc:2 * GAP + tc, :] = zero
    buf[2 * GAP + tt:, :] = zero
    buf[GAP:GAP + tc, :] = ref_rows[0:tc]
    buf[2 * GAP + tc:2 * GAP + tt, :] = ref_rows[tc:tt]
    return buf[...]


def _ungapped(v, tc, tt):
    return jnp.concatenate([v[GAP:GAP + tc], v[2 * GAP + tc:2 * GAP + tt]], axis=0)


def _shift_rows(x, o):
    return x if o == 0 else pltpu.roll(x, (-o) % x.shape[0], 0)


def _ssd_conv_fwd(name, zx, w, b, di, tc, comm=None):
    tt, kc, cd = zx.shape[0], w.shape[0], w.shape[1]
    cb = _pick(cd, (LANES,))
    off = di // cb
    assert kc // 2 < GAP and tc % GAP == 0 and tt % GAP == 0

    def body(x_ref, w_ref, b_ref, o_ref, pre_ref, xp):
        x = _gapped(x_ref[...].astype(F32), xp, tc, tt)
        acc = jnp.broadcast_to(b_ref[...], x.shape)
        for k in range(kc):
            acc = acc + w_ref[k:k + 1, :] * _shift_rows(x, k - kc // 2)
        acc = _ungapped(acc, tc, tt)
        pre_ref[...] = acc.astype(pre_ref.dtype)
        o_ref[...] = _silu(acc).astype(o_ref.dtype)

    cm = comm if comm is not None else _Comm()
    blk = pl.BlockSpec((tt, cb), lambda j: (0, j))
    res = pl.pallas_call(
        _carry(body, comm, 3, 2, (cd // cb,)), name=name, grid=(cd // cb,),
        out_shape=[pltpu.HBM((tt, cd), ACT), pltpu.HBM((tt, cd), ACT)] + cm.out_shape,
        in_specs=[pl.BlockSpec((tt, cb), lambda j: (0, j + off)), pl.BlockSpec((kc, cb), lambda j: (0, j)),
                  pl.BlockSpec((1, cb), lambda j: (0, j))] + cm.specs,
        out_specs=[blk, blk] + cm.specs,
        scratch_shapes=[pltpu.VMEM((tt + 3 * GAP, cb), F32)] + (cm.scratch if comm is not None else []),
        compiler_params=pltpu.CompilerParams(dimension_semantics=("arbitrary",), vmem_limit_bytes=_vmem(5 * tt * cb * 4)),
    )(_hbm(zx), _hbm(w), _hbm(b), *cm.operands)
    return res[0], res[1], cm.split(res[2:])


def _ssd_conv_bwd(name, zx, dact2, w, pre, dzx, di, tc, comm=None):
    tt, kc, cd = zx.shape[0], w.shape[0], w.shape[1]
    cb = _pick(cd, (LANES,))
    off = di // cb

    def body(x_ref, d0_ref, d1_ref, w_ref, pre_ref, _, dx_ref, dw_ref, db_ref, xp, dp):
        x = _gapped(x_ref[...].astype(F32), xp, tc, tt)
        dpre = (d0_ref[...].astype(F32) + d1_ref[...].astype(F32)) * _dsilu(pre_ref[...].astype(F32))
        db_ref[...] = _colsum(dpre)
        dpre = _gapped(dpre, dp, tc, tt)
        dx = jnp.zeros_like(x)
        for k in range(kc):
            sh = _shift_rows(dpre, -(k - kc // 2))
            dx = dx + w_ref[k:k + 1, :] * sh
            dw_ref[k:k + 1, :] = _colsum(sh * x)
        dx_ref[...] = _ungapped(dx, tc, tt).astype(dx_ref.dtype)

    cm = comm if comm is not None else _Comm()
    res = pl.pallas_call(
        _carry(body, comm, 6, 3, (cd // cb,)), name=name, grid=(cd // cb,),
        out_shape=[pltpu.HBM(dzx.shape, dzx.dtype), pltpu.HBM((kc, cd), F32), pltpu.HBM((1, cd), F32)] + cm.out_shape,
        in_specs=[pl.BlockSpec((tt, cb), lambda j: (0, j + off)), pl.BlockSpec((None, tt, cb), lambda j: (0, 0, j)),
                  pl.BlockSpec((None, tt, cb), lambda j: (1, 0, j)), pl.BlockSpec((kc, cb), lambda j: (0, j)),
                  pl.BlockSpec((tt, cb), lambda j: (0, j)), pl.BlockSpec(memory_space=pl.ANY)] + cm.specs,
        out_specs=[pl.BlockSpec((tt, cb), lambda j: (0, j + off)), pl.BlockSpec((kc, cb), lambda j: (0, j)),
                   pl.BlockSpec((1, cb), lambda j: (0, j))] + cm.specs,
        input_output_aliases={5: 0},
        scratch_shapes=[pltpu.VMEM((tt + 3 * GAP, cb), F32)] * 2 + (cm.scratch if comm is not None else []),
        compiler_params=pltpu.CompilerParams(dimension_semantics=("arbitrary",), vmem_limit_bytes=_vmem(10 * tt * cb * 4)),
    )(_hbm(zx), _hbm(dact2), _hbm(dact2), _hbm(w), _hbm(pre), _hbm(dzx), *cm.operands)
    return res[0], res[1], res[2], cm.split(res[3:])


def _strided_conv(name, x, w, b, stride, x_col0=0):
    t, ch = x.shape[0], w.shape[1]
    kk = w.shape[0]
    pad = (kk // 2) * stride
    cb = _pick(ch, (LANES,))
    has_b = b is not None

    def body(*refs):
        x_ref, w_ref = refs[:2]
        o_ref, xp = refs[-2:]
        xp[0:pad, :] = jnp.zeros((pad, cb), F32)
        xp[pad + t:, :] = jnp.zeros((pad, cb), F32)
        xp[pad:pad + t, :] = x_ref[...]
        acc = jnp.broadcast_to(refs[2][...], (t, cb)) if has_b else jnp.zeros((t, cb), F32)
        for k in range(kk):
            acc = acc + w_ref[k:k + 1, :] * xp[k * stride:k * stride + t, :]
        o_ref[...] = acc

    xoff = x_col0 // cb
    ins, specs = [x, w], [pl.BlockSpec((t, cb), lambda j: (0, j + xoff)), pl.BlockSpec((kk, cb), lambda j: (0, j))]
    if has_b:
        ins.append(b)
        specs.append(pl.BlockSpec((1, cb), lambda j: (0, j)))
    return pl.pallas_call(
        body, name=name, grid=(ch // cb,), out_shape=pltpu.HBM((t, ch), F32), in_specs=specs,
        out_specs=pl.BlockSpec((t, cb), lambda j: (0, j)), scratch_shapes=[pltpu.VMEM((t + 2 * pad, cb), F32)],
        compiler_params=pltpu.CompilerParams(dimension_semantics=("parallel",), vmem_limit_bytes=_vmem(6 * t * cb * 4)),
    )(*[_hbm(v) for v in ins])


def _strided_conv_dw(name, x, dv, kk, stride, x_col0=0):
    t, ch = dv.shape
    pad = (kk // 2) * stride
    cb = _pick(ch, (LANES,))

    def body(x_ref, d_ref, dw_ref, db_ref, xp):
        xp[0:pad, :] = jnp.zeros((pad, cb), F32)
        xp[pad + t:, :] = jnp.zeros((pad, cb), F32)
        xp[pad:pad + t, :] = x_ref[...]
        d = d_ref[...]
        for k in range(kk):
            dw_ref[k:k + 1, :] = _colsum(d * xp[k * stride:k * stride + t, :])
        db_ref[...] = _colsum(d)

    blk = pl.BlockSpec((t, cb), lambda j: (0, j))
    xoff = x_col0 // cb
    return pl.pallas_call(
        body, name=name, grid=(ch // cb,), out_shape=[pltpu.HBM((kk, ch), F32), pltpu.HBM((1, ch), F32)],
        in_specs=[pl.BlockSpec((t, cb), lambda j: (0, j + xoff)), blk], out_specs=[pl.BlockSpec((kk, cb), lambda j: (0, j)), pl.BlockSpec((1, cb), lambda j: (0, j))],
        scratch_shapes=[pltpu.VMEM((t + 2 * pad, cb), F32)],
        compiler_params=pltpu.CompilerParams(dimension_semantics=("parallel",), vmem_limit_bytes=_vmem(6 * t * cb * 4)),
    )(_hbm(x), _hbm(dv))


def _grid_t(a, n1, n2):
    return a.reshape(n1, n2, a.shape[-1]).swapaxes(0, 1).reshape(n1 * n2, a.shape[-1])


def _chunk_order(d, i, ncc, nc):
    back = jnp.where(i < ncc, ncc - 1 - i, nc - 1 - (i - ncc))
    return jnp.where(d == 0, i, back)


def _ssd_chunk_setup(d, dt_raw, bias, a_log, q, h, di):
    p = di // h
    dt = _softplus(dt_raw + bias)
    a_neg = -jnp.exp(a_log)
    delta = dt * a_neg
    r = lax.broadcasted_iota(jnp.int32, (q, q), 0)
    c = lax.broadcasted_iota(jnp.int32, (q, q), 1)
    sgn = 1 - 2 * d
    mask = (r - c) * sgn >= 0
    mask_t = (c - r) * sgn >= 0
    a = _dot_lx(mask.astype(MXU), delta, NN, parts=3)
    tot = _colsum(delta)
    ea, dte, cd = jnp.exp(a), jnp.exp(tot - a), jnp.exp(tot)
    hh = lax.broadcasted_iota(jnp.int32, (h, di), 0)
    cc = lax.broadcasted_iota(jnp.int32, (h, di), 1)
    e = (cc // p == hh).astype(MXU)
    ex = _dot_rx(jnp.concatenate([dt, ea, dte, jnp.broadcast_to(cd, (8, h))], axis=0), e, NN)
    eye = (lax.broadcasted_iota(jnp.int32, (h, h), 0) == lax.broadcasted_iota(jnp.int32, (h, h), 1)).astype(MXU)
    a_t = _dot_lx(eye, a, NT, parts=3)
    return dict(dt=dt, a_neg=a_neg, a=a, a_t=a_t, mask=mask, mask_t=mask_t, e=e,
                dt_e=ex[0:q], ea_e=ex[q:2 * q], dte_e=ex[2 * q:3 * q], cd_e=ex[3 * q:3 * q + 1])


def _pick_heads(r, q, hpg, p):
    lane = lax.broadcasted_iota(jnp.int32, (q, hpg * p), 1) // p
    out = jnp.zeros((q, hpg * p), F32)
    for j in range(hpg):
        out = out + jnp.where(lane == j, r[j * q:(j + 1) * q], 0.0)
    return out


def _ssd_fwd(name, xbc, dt2, bias2, alog2, di, tc, comm=None):
    tt, cd = xbc.shape
    h = dt2.shape[-1]
    q, n = SSD_CHUNK, SSD_STATE
    gn = (cd - di) // 2
    g = gn // n
    hpg, p = h // g, di // h
    gp = hpg * p
    nc, ncc = tt // q, tc // q
    assert di % gn == 0

    def body(x_ref, b_ref, c_ref, dt_ref, bias_ref, alog_ref, y_ref, hp_ref, ht):
        d, i = pl.program_id(0), pl.program_id(1)

        @pl.when(i == 0)
        def _():
            ht[...] = jnp.zeros_like(ht)

        s = _ssd_chunk_setup(d, dt_ref[...], bias_ref[...], alog_ref[...], q, h, di)
        xd = x_ref[...].astype(F32) * s["dt_e"]
        hp_ref[...] = ht[...].astype(hp_ref.dtype)
        for gi in range(g):
            bg, cg = b_ref[:, gi * n:(gi + 1) * n].astype(MXU), c_ref[:, gi * n:(gi + 1) * n].astype(MXU)
            sl = slice(gi * gp, (gi + 1) * gp)
            sc = _dot(cg, bg, NT)
            ms = []
            for j in range(hpg):
                hd = gi * hpg + j
                seg = s["a"][:, hd:hd + 1] - s["a_t"][hd:hd + 1, :]
                ms.append(sc * jnp.exp(jnp.where(s["mask"], seg, -jnp.inf)))
            xdg = xd[:, sl]
            ydiag = _pick_heads(_dot(jnp.concatenate(ms, axis=0), xdg, NN), q, hpg, p)
            htg = ht[:, sl]
            y_ref[:, sl] = ydiag + _dot(cg, htg, NN) * s["ea_e"][:, sl]
            ht[:, sl] = s["cd_e"][:, sl] * htg + _dot(bg, xdg * s["dte_e"][:, sl], TN)

    def cidx(d, i):
        return _chunk_order(d, i, ncc, nc)

    cm = comm if comm is not None else _Comm()
    res = pl.pallas_call(
        _carry(body, comm, 6, 2, (2, nc)), name=name, grid=(2, nc),
        out_shape=[pltpu.HBM((2, tt, di), F32), pltpu.HBM((2, nc, n, di), ACT)] + cm.out_shape,
        in_specs=[pl.BlockSpec((q, di), lambda d, i: (cidx(d, i), 0)),
                  pl.BlockSpec((q, gn), lambda d, i: (cidx(d, i), di // gn)),
                  pl.BlockSpec((q, gn), lambda d, i: (cidx(d, i), di // gn + 1)),
                  pl.BlockSpec((None, q, h), lambda d, i: (d, cidx(d, i), 0)),
                  pl.BlockSpec((None, 1, h), lambda d, i: (d, 0, 0)),
                  pl.BlockSpec((None, 1, h), lambda d, i: (d, 0, 0))] + cm.specs,
        out_specs=[pl.BlockSpec((None, q, di), lambda d, i: (d, cidx(d, i), 0)),
                   pl.BlockSpec((None, None, n, di), lambda d, i: (d, cidx(d, i), 0, 0))] + cm.specs,
        scratch_shapes=[pltpu.VMEM((n, di), F32)] + (cm.scratch if comm is not None else []),
        compiler_params=pltpu.CompilerParams(dimension_semantics=("arbitrary", "arbitrary"), vmem_limit_bytes=_vmem(16 * q * di * 4)),
    )(*[_hbm(v) for v in (xbc, xbc, xbc, dt2, bias2, alog2)], *cm.operands)
    return res[0], res[1], cm.split(res[2:])


def _ssd_bwd(name, xbc, dt2, bias2, alog2, dy, hp2, dskip_e, di, tc, comm=None):
    tt, cd = xbc.shape
    h = dt2.shape[-1]
    q, n = SSD_CHUNK, SSD_STATE
    gn = (cd - di) // 2
    g = gn // n
    hpg, p = h // g, di // h
    gp = hpg * p
    nc, ncc = tt // q, tc // q

    def body(x_ref, b_ref, c_ref, dt_ref, bias_ref, alog_ref, dy_ref, hp_ref, dsk_ref,
             dxbc_ref, ddt_ref, dalog_ref, dbias_ref, dht, dxd, off):
        d, i = pl.program_id(0), pl.program_id(1)

        @pl.when(i == 0)
        def _():
            dht[...] = jnp.zeros_like(dht)
            dalog_ref[...] = jnp.zeros_like(dalog_ref)
            dbias_ref[...] = jnp.zeros_like(dbias_ref)

        s = _ssd_chunk_setup(d, dt_ref[...], bias_ref[...], alog_ref[...], q, h, di)
        x, dyc = x_ref[...].astype(F32), dy_ref[...]
        xd = x * s["dt_e"]
        dyea = dyc * s["ea_e"]
        xdte = xd * s["dte_e"]
        lane = lax.broadcasted_iota(jnp.int32, (q, gp), 1) // p
        lane_h = lax.broadcasted_iota(jnp.int32, (q, h), 1)
        da_d = jnp.zeros((q, h), F32)
        last_e = []
        for gi in range(g):
            bg, cg = b_ref[:, gi * n:(gi + 1) * n].astype(MXU), c_ref[:, gi * n:(gi + 1) * n].astype(MXU)
            sl = slice(gi * gp, (gi + 1) * gp)
            sc, sct = _dot(cg, bg, NT), _dot(bg, cg, NT)
            dyg, xdg = dyc[:, sl], xd[:, sl]
            htg, dhtg = hp_ref[:, sl].astype(F32), dht[:, sl]
            dystack = jnp.concatenate([jnp.where(lane == j, dyg, 0.0) for j in range(hpg)], axis=0)
            xdstack = jnp.concatenate([jnp.where(lane == j, xdg, 0.0) for j in range(hpg)], axis=0)
            gs = _dot(dystack, xdg, NT)
            gst = _dot(xdstack, dyg, NT)
            ds = jnp.zeros((q, q), F32)
            mts = []
            for j in range(hpg):
                hd = gi * hpg + j
                col, rw = s["a"][:, hd:hd + 1], s["a_t"][hd:hd + 1, :]
                gl = gs[j * q:(j + 1) * q] * jnp.exp(jnp.where(s["mask"], col - rw, -jnp.inf))
                ds = ds + gl
                mt = sct * jnp.exp(jnp.where(s["mask_t"], rw - col, -jnp.inf))
                mts.append(mt)
                da_j = jnp.sum(gl * sc, axis=1, keepdims=True) - jnp.sum(gst[j * q:(j + 1) * q] * mt, axis=1, keepdims=True)
                da_d = da_d + jnp.where(lane_h == hd, da_j, 0.0)
            dxd_diag = _pick_heads(_dot(jnp.concatenate(mts, axis=0), dyg, NN), q, hpg, p)
            z = _dot(bg, dhtg, NN) * s["dte_e"][:, sl]
            yoff = _dot(cg, htg, NN) * s["ea_e"][:, sl]
            off[:, sl] = dyg * yoff - xdg * z
            dxd[:, sl] = dxd_diag + z
            dxbc_ref[:, di + gi * n:di + (gi + 1) * n] = (_dot(ds, cg, TN) + _dot(xdte[:, sl], dhtg, NT)).astype(dxbc_ref.dtype)
            dxbc_ref[:, di + gn + gi * n:di + gn + (gi + 1) * n] = (_dot(ds, bg, NN)
                                                                    + _dot(dyea[:, sl], htg, NT)).astype(dxbc_ref.dtype)
            last_e.append(s["cd_e"][:, sl] * _colsum(dhtg * htg) + _colsum(xdg * z))
            dht[:, sl] = s["cd_e"][:, sl] * dhtg + _dot(cg, dyea[:, sl], TN)
        dxd_all = dxd[...]
        last = jnp.concatenate(last_e, axis=1)
        da = da_d + _dot_rx(off[...], s["e"], NT)
        last_h = _dot_rx(jnp.broadcast_to(last, (8, di)), s["e"], NT)[0:1]
        ddelta = _dot_lx(s["mask_t"].astype(MXU), da, NN, parts=3) + last_h
        ddt = ddelta * s["a_neg"] + _dot_rx(dxd_all * x, s["e"], NT)
        ddt_raw = ddt * _sigmoid(dt_ref[...] + bias_ref[...])
        ddt_ref[...] = ddt_raw
        dalog_ref[...] += _colsum(ddelta * s["dt"]) * s["a_neg"]
        dbias_ref[...] += _colsum(ddt_raw)
        dxbc_ref[:, 0:di] = (dxd_all * s["dt_e"] + jnp.where(d == 0, dyc * dsk_ref[...], 0.0)).astype(dxbc_ref.dtype)

    def cidx(d, i):
        return _chunk_order(d, nc - 1 - i, ncc, nc)

    cm = comm if comm is not None else _Comm()
    res = pl.pallas_call(
        _carry(body, comm, 9, 4, (2, nc)), name=name, grid=(2, nc),
        out_shape=[pltpu.HBM((2, tt, cd), ACT), pltpu.HBM((2, tt, h), F32),
                   pltpu.HBM((2, 1, h), F32), pltpu.HBM((2, 1, h), F32)] + cm.out_shape,
        in_specs=[pl.BlockSpec((q, di), lambda d, i: (cidx(d, i), 0)),
                  pl.BlockSpec((q, gn), lambda d, i: (cidx(d, i), di // gn)),
                  pl.BlockSpec((q, gn), lambda d, i: (cidx(d, i), di // gn + 1)),
                  pl.BlockSpec((None, q, h), lambda d, i: (d, cidx(d, i), 0)),
                  pl.BlockSpec((None, 1, h), lambda d, i: (d, 0, 0)),
                  pl.BlockSpec((None, 1, h), lambda d, i: (d, 0, 0)),
                  pl.BlockSpec((q, di), lambda d, i: (cidx(d, i), 0)),
                  pl.BlockSpec((None, None, n, di), lambda d, i: (d, cidx(d, i), 0, 0)),
                  pl.BlockSpec((1, di), lambda d, i: (0, 0))] + cm.specs,
        out_specs=[pl.BlockSpec((None, q, cd), lambda d, i: (d, cidx(d, i), 0)),
                   pl.BlockSpec((None, q, h), lambda d, i: (d, cidx(d, i), 0)),
                   pl.BlockSpec((None, 1, h), lambda d, i: (d, 0, 0)),
                   pl.BlockSpec((None, 1, h), lambda d, i: (d, 0, 0))] + cm.specs,
        scratch_shapes=[pltpu.VMEM((n, di), F32), pltpu.VMEM((q, di), F32), pltpu.VMEM((q, di), F32)]
        + (cm.scratch if comm is not None else []),
        compiler_params=pltpu.CompilerParams(dimension_semantics=("arbitrary", "arbitrary"), vmem_limit_bytes=_vmem(24 * q * di * 4)),
    )(*[_hbm(v) for v in (xbc, xbc, xbc, dt2, bias2, alog2, dy, hp2, dskip_e)], *cm.operands)
    return res[0], res[1], res[2], res[3], cm.split(res[4:])


def _ssd_gate_fwd(name, y2, xbc, zx, dskip_e, norm_w, di, nct, t):
    def fn(rows, bcs, i):
        yf, yb, xs, z = rows
        zg = (yf + yb + bcs[0] * xs) * _silu(z)
        rn = lax.rsqrt(jnp.mean(zg * zg, axis=-1, keepdims=True) + EPS)
        return [zg * rn * bcs[1]], []

    ins = [_ri(y2, lead=0, ro=nct), _ri(y2, lead=1, ro=nct), _ri(xbc, di, 0, ro=nct), _ri(zx, di, 0, ro=nct)]
    return _rowwise(name, fn, t, ins, [dskip_e, norm_w], [(di, MXU)])[0][0]


def _ssd_gate_bwd(name, dyn, y2, xbc, zx, dskip_e, norm_w, di, nct, tt):
    def fn(rows, bcs, i):
        dn, yf, yb, xs, z = rows
        lat = i >= nct
        ytot = yf + yb + bcs[0] * xs
        sz = _silu(z)
        zg = ytot * sz
        rn = lax.rsqrt(jnp.mean(zg * zg, axis=-1, keepdims=True) + EPS)
        u = dn * bcs[1]
        dzg = rn * u - zg * (rn * rn * rn) * jnp.mean(u * zg, axis=-1, keepdims=True)
        dy = jnp.where(lat, dzg * sz, 0.0)
        dz = jnp.where(lat, dzg * ytot * _dsilu(z), 0.0)
        return [dy, dz], [jnp.where(lat, _colsum(dn * zg * rn), 0.0), jnp.where(lat, _colsum(dy * xs), 0.0)]

    ins = [_ri(dyn, ro=-nct), _ri(y2, lead=0), _ri(y2, lead=1), _ri(xbc, di, 0), _ri(zx, di, 0)]
    (dy, dzx), (dnw, ddsk) = _rowwise(name, fn, tt, ins, [dskip_e, norm_w], [(di, F32), (di, MXU, zx.shape[1], 0)], [(1, di)] * 2)
    return dy, dzx, dnw, ddsk


def _ada_fwd(name, cs, w, b):
    nl, d, c = w.shape
    r = cs.shape[0]

    def body(cs_ref, w_ref, b_ref, o_ref):
        o_ref[...] = _dot(_silu(cs_ref[...]), w_ref[...], NN) + b_ref[...]

    return pl.pallas_call(
        body, name=name, grid=(nl,), out_shape=pltpu.HBM((nl, r, c), F32),
        in_specs=[pl.BlockSpec((r, d), lambda l: (0, 0)), pl.BlockSpec((None, d, c), lambda l: (l, 0, 0)),
                  pl.BlockSpec((None, 1, c), lambda l: (l, 0, 0))],
        out_specs=pl.BlockSpec((None, r, c), lambda l: (l, 0, 0)),
        compiler_params=pltpu.CompilerParams(dimension_semantics=("parallel",), vmem_limit_bytes=_vmem(2 * d * c * 4)),
    )(_hbm(cs), _hbm(w), _hbm(b))


def _ada_bwd(name, cs, w, dmod):
    nl, d, c = w.shape
    r = cs.shape[0]

    def body(cs_ref, w_ref, dm_ref, dw_ref, dsc_ref):
        dm = dm_ref[...]
        dw_ref[...] = _dot(_silu(cs_ref[...]), dm, TN)

        @pl.when(pl.program_id(0) == 0)
        def _():
            dctx = jnp.broadcast_to(_colsum(dm[r // 2:]), (8, c))
            dsc_ref[...] = _dot(dctx, w_ref[...], NT)[0:1]

    return pl.pallas_call(
        body, name=name, grid=(nl,), out_shape=[pltpu.HBM((nl, d, c), F32), pltpu.HBM((1, d), F32)],
        in_specs=[pl.BlockSpec((r, d), lambda l: (0, 0)), pl.BlockSpec((None, d, c), lambda l: (l, 0, 0)),
                  pl.BlockSpec((None, r, c), lambda l: (l, 0, 0))],
        out_specs=[pl.BlockSpec((None, d, c), lambda l: (l, 0, 0)), pl.BlockSpec((1, d), lambda l: (0, 0))],
        compiler_params=pltpu.CompilerParams(dimension_semantics=("arbitrary",), vmem_limit_bytes=_vmem(4 * d * c * 4)),
    )(_hbm(cs), _hbm(w), _hbm(dmod))


def _adam_math(w, g, m, v):
    m = ADAM_B1 * m + (1.0 - ADAM_B1) * g
    v = ADAM_B2 * v + (1.0 - ADAM_B2) * (g * g)
    m_hat = m / (1.0 - ADAM_B1 ** ADAM_STEP)
    v_hat = v / (1.0 - ADAM_B2 ** ADAM_STEP)
    delta = -ADAM_LR * (m_hat / (jnp.sqrt(v_hat) + ADAM_EPS) + ADAM_WD * w)
    return delta, m, v


def _adam(name, slots, w, m, v, comm=None):
    segs = list(slots) if isinstance(slots, (list, tuple)) else [slots]
    nseg = len(segs)
    ns, c = segs[0].shape[0], segs[0].shape[2]
    r = sum(s.shape[1] for s in segs)
    tr = _pick(min(s.shape[1] for s in segs), (256, 128, 64, 32, 16, 8))
    starts = [sum(s.shape[1] for s in segs[:k]) // tr for k in range(nseg)]

    def body(*refs):
        s_refs, (w_ref, m_ref, v_ref), (g_ref, d_ref, mo_ref, vo_ref) = refs[:nseg], refs[nseg:nseg + 3], refs[nseg + 3:]
        i = pl.program_id(0)

        def total(s_ref):
            g = s_ref[0].astype(F32)
            for k in range(1, ns):
                g = g + s_ref[k].astype(F32)
            return g

        g = total(s_refs[0])
        for k in range(1, nseg):
            g = jnp.where(i >= starts[k], total(s_refs[k]), g)
        d, mn, vn = _adam_math(w_ref[...], g, m_ref[...], v_ref[...])
        g_ref[...], d_ref[...], mo_ref[...], vo_ref[...] = g, d, mn, vn

    blk = pl.BlockSpec((tr, c), lambda i: (i, 0))
    seg_specs = [pl.BlockSpec((ns, tr, c), lambda i, st=starts[k], nt=segs[k].shape[1] // tr: (0, jnp.clip(i - st, 0, nt - 1), 0))
                 for k in range(nseg)]
    cm = comm if comm is not None else _Comm()
    res = pl.pallas_call(
        _carry(body, comm, nseg + 3, 4, (r // tr,)), name=name, grid=(r // tr,),
        out_shape=[pltpu.HBM((r, c), F32)] * 4 + cm.out_shape,
        in_specs=seg_specs + [blk, blk, blk] + cm.specs, out_specs=[blk] * 4 + cm.specs,
        scratch_shapes=cm.scratch if comm is not None else [],
        compiler_params=pltpu.CompilerParams(dimension_semantics=("arbitrary",), vmem_limit_bytes=_vmem(16 * nseg * tr * c * 4)),
    )(*[_hbm(s) for s in segs], _hbm(w), _hbm(m), _hbm(v), *cm.operands)
    return res[:4] if comm is None else (res[:4], cm.split(res[4:]))


def _adam_small(name, slots, ws, ms, vs, scale=None):
    k = len(slots)

    def body(*refs):
        s_refs, w_refs, m_refs, v_refs = refs[:k], refs[k:2 * k], refs[2 * k:3 * k], refs[3 * k:4 * k]
        sc_ref = refs[4 * k] if scale is not None else None
        outs = refs[4 * k + (scale is not None):]
        for a in range(k):
            g = s_refs[a][0]
            for j in range(1, NDEV):
                g = g + s_refs[a][j]
            if scale is not None and a == scale[0]:
                g = g * _dsilu(sc_ref[...])
            d, mn, vn = _adam_math(w_refs[a][...], g, m_refs[a][...], v_refs[a][...])
            outs[a][...], outs[k + a][...], outs[2 * k + a][...], outs[3 * k + a][...] = g, d, mn, vn

    shapes = [pltpu.HBM(w.shape, F32) for w in ws]
    extra = [scale[1]] if scale is not None else []
    ins = [*slots, *ws, *ms, *vs, *extra]

    def whole(shape):
        return pl.BlockSpec(shape, lambda i, nd=len(shape): (0,) * nd)

    res = pl.pallas_call(body, name=name, grid=(1,), out_shape=shapes * 4, in_specs=[whole(v.shape) for v in ins],
                         out_specs=[whole(s.shape) for s in shapes * 4])(*[_hbm(v) for v in ins])
    return res[:k], res[k:2 * k], res[2 * k:3 * k], res[3 * k:]


def _unshard_cols(g):
    g = jnp.moveaxis(g, 0, -2)
    return g.reshape(g.shape[:-2] + (g.shape[-2] * g.shape[-1],))


def _shard_cols(a):
    a = a.reshape(a.shape[:-1] + (NDEV, a.shape[-1] // NDEV))
    return jnp.moveaxis(a, -2, 0)


def _unshard_rows(g):
    g = jnp.moveaxis(g, 0, -3)
    return g.reshape(g.shape[:-3] + (g.shape[-3] * g.shape[-2], g.shape[-1]))


def _shard_rows(a):
    a = a.reshape(a.shape[:-2] + (NDEV, a.shape[-2] // NDEV, a.shape[-1]))
    return jnp.moveaxis(a, -3, 0)


def _flat2(a):
    return a.reshape((-1, a.shape[-1]))


def kernel(x, c, ctx, c_ctx, ada_w, ada_b, norm_mix_g, norm_ffn_g, final_norm_g, ssd_w_in, ssd_conv_w, ssd_conv_b, ssd_dt_bias_f, ssd_dt_bias_b, ssd_a_log_f, ssd_a_log_b, ssd_d_skip, ssd_norm_w, ssd_w_out, conf_w_pw1, conf_b_pw1, conf_dw_w, conf_dw_b, conf_ln_g, conf_ln_b, conf_w_pw2, conf_b_pw2, ffn_w_in, ffn_w_out, loss_target, m_c_ctx, m_ada_w, m_ada_b, m_norm_mix_g, m_norm_ffn_g, m_final_norm_g, m_ssd_w_in, m_ssd_conv_w, m_ssd_conv_b, m_ssd_dt_bias_f, m_ssd_dt_bias_b, m_ssd_a_log_f, m_ssd_a_log_b, m_ssd_d_skip, m_ssd_norm_w, m_ssd_w_out, m_conf_w_pw1, m_conf_b_pw1, m_conf_dw_w, m_conf_dw_b, m_conf_ln_g, m_conf_ln_b, m_conf_w_pw2, m_conf_b_pw2, m_ffn_w_in, m_ffn_w_out, v_c_ctx, v_ada_w, v_ada_b, v_norm_mix_g, v_norm_ffn_g, v_final_norm_g, v_ssd_w_in, v_ssd_conv_w, v_ssd_conv_b, v_ssd_dt_bias_f, v_ssd_dt_bias_b, v_ssd_a_log_f, v_ssd_a_log_b, v_ssd_d_skip, v_ssd_norm_w, v_ssd_w_out, v_conf_w_pw1, v_conf_b_pw1, v_conf_dw_w, v_conf_dw_b, v_conf_ln_g, v_conf_ln_b, v_conf_w_pw2, v_conf_b_pw2, v_ffn_w_in, v_ffn_w_out):
    args = dict(locals())
    names = ['c_ctx', 'ada_w', 'ada_b', 'norm_mix_g', 'norm_ffn_g', 'final_norm_g', 'ssd_w_in', 'ssd_conv_w', 'ssd_conv_b',
             'ssd_dt_bias_f', 'ssd_dt_bias_b', 'ssd_a_log_f', 'ssd_a_log_b', 'ssd_d_skip', 'ssd_norm_w', 'ssd_w_out',
             'conf_w_pw1', 'conf_b_pw1', 'conf_dw_w', 'conf_dw_b', 'conf_ln_g', 'conf_ln_b', 'conf_w_pw2', 'conf_b_pw2',
             'ffn_w_in', 'ffn_w_out']
    me = 4 * lax.axis_index("x") + 2 * lax.axis_index("y") + lax.axis_index("c")
    t, d = x.shape[1], x.shape[2]
    tc = ctx.shape[1]
    tt = tc + t
    nct = tc // ROW_TILE
    assert tc % ROW_TILE == 0 and t % ROW_TILE == 0
    h = ssd_dt_bias_f.shape[-1]
    di = ssd_norm_w.shape[-1]
    cdim = ssd_conv_b.shape[-1]
    kc = ssd_conv_w.shape[1]
    ck = conf_dw_w.shape[1]
    ch = d // 2
    rows_g = t // GRID_W
    nl = ada_w.shape[0]
    cw = ada_w.shape[2]
    x2, ctx2, tgt = x[0], ctx[0], loss_target[0]

    (c_all, convw_g), _ = _exchange("gather_first", [c, ssd_conv_w[0]])
    ride_norm = _Comm(gather=[ssd_w_in[0].astype(WIRE)])
    ride_proj = _Comm(gather=[ssd_w_out[0].astype(WIRE), conf_w_pw2[0].astype(WIRE)])
    ride_conv = _Comm(gather=[conf_w_pw1[0].astype(WIRE), conf_b_pw1, conf_dw_w[0], conf_dw_b, conf_ln_g, conf_ln_b, conf_b_pw2])
    ride_scan = _Comm(gather=[ffn_w_in[0].astype(WIRE), ffn_w_in[1].astype(WIRE), ffn_w_out[0].astype(WIRE), ffn_w_out[1].astype(WIRE)])
    conv_w_full = _unshard_cols(convw_g)

    cs_all = jnp.concatenate([c_all[:, 0, :], jnp.broadcast_to(c_ctx[None, :], (NDEV, d))], axis=0)
    ada_b_mine = lax.dynamic_slice_in_dim(ada_b, me * cw, cw, axis=1)[:, None, :]
    mod_part = _ada_fwd("ada_fwd", cs_all, ada_w, ada_b_mine)
    (mod_g,), _ = _exchange("gather_mod", [mod_part])
    mod_all = jnp.moveaxis(mod_g, 0, 2).reshape(nl, 2 * NDEV, NDEV * cw)
    mod_lat = lax.dynamic_slice_in_dim(mod_all, me, 1, axis=1)[:, 0, :]
    mod_ctx = mod_all[0, NDEV, :]

    def six(v):
        return [v[k * d:(k + 1) * d][None, :] for k in range(6)]

    sh1, s1, g1, sh2, s2, g2 = six(mod_lat[0])
    csh1, cs1 = six(mod_ctx)[:2]
    sh1b, s1b, g1b, sh2b, s2b, g2b = six(mod_lat[1])
    nmg, nfg = norm_mix_g, norm_ffn_g

    s01, sh01 = jnp.concatenate([cs1, s1], axis=0), jnp.concatenate([csh1, sh1], axis=0)
    xn_all, ((w_in_g,), _) = _normmod_fwd("l0_norm", ctx2, x2, nmg[0:1], s01, sh01, comm=ride_norm)
    w_ssd_in = _unshard_cols(w_in_g)
    w_dt = jnp.pad(w_ssd_in[:, di + cdim:], ((0, 0), (0, LANES - 2 * h)))
    zx, ((w_out_g, pw2_g), _) = _mm("ssd_in_proj", xn_all, w_ssd_in, "nn", ACT, comm=ride_proj, n_used=di + cdim)
    dtr = _mm("ssd_dt_proj", xn_all, w_dt, "nn")
    dt2 = jnp.moveaxis(dtr[:, :2 * h].reshape(tt, 2, h), 1, 0)
    bias2 = jnp.stack([ssd_dt_bias_f, ssd_dt_bias_b])
    alog2 = jnp.stack([ssd_a_log_f, ssd_a_log_b])
    xbc, xbc_pre, ((pw1_g, bpw1_g, dww_g, dwb_g, lng_g, lnb_g, bpw2_g), _) = _ssd_conv_fwd("ssd_conv", zx, conv_w_full, ssd_conv_b, di, tc,
                                                                                 comm=ride_conv)
    y2, hp2, (ffn_g, _) = _ssd_fwd("ssd_scan", xbc, dt2, bias2, alog2, di, tc, comm=ride_scan)
    w_ssd_out = _unshard_rows(w_out_g)
    w_pw1, w_pw2 = _unshard_cols(pw1_g), _unshard_rows(pw2_g)
    w_fin = [_unshard_cols(ffn_g[0]), _unshard_cols(ffn_g[1])]
    w_fout = [_unshard_rows(ffn_g[2]), _unshard_rows(ffn_g[3])]
    dw_w_full = _unshard_cols(dww_g)
    b_pw1, dw_b, ln_g, ln_b, b_pw2 = (_unshard_cols(a) for a in (bpw1_g, dwb_g, lng_g, lnb_g, bpw2_g))
    dskip_e = jnp.repeat(ssd_d_skip, di // h, axis=1)
    yn = _ssd_gate_fwd("ssd_gate", y2, xbc, zx, dskip_e, ssd_norm_w, di, nct, t)
    mix0 = _mm("ssd_out_proj", yn, w_ssd_out, "nn")
    h1, xf0 = _resnorm_fwd("l0_res_norm", x2, mix0, g1, nfg[0:1], s2, sh2)
    u0 = _mm("ffn0_in", xf0, w_fin[0], "nn", ACT)
    hid0 = _swiglu_fwd("ffn0_act", u0)
    f0 = _mm("ffn0_out", hid0, w_fout[0], "nn")
    h2, xn1 = _resnorm_fwd("l1_norm", h1, f0, g2, nmg[1:2], s1b, sh1b)
    u1 = _mm("conf_pw1", xn1, w_pw1, "nn", ACT, bias=b_pw1)
    gl = _glu_fwd("conf_glu", u1)
    gl_h = _grid_t(gl[:, :ch], rows_g, GRID_W)
    v_ht = _strided_conv("conf_conv_h", gl_h, dw_w_full[:, :ch], dw_b[:, :ch], rows_g)
    v_v = _strided_conv("conf_conv_v", gl, dw_w_full[:, ch:], dw_b[:, ch:], GRID_W, x_col0=ch)
    v_h = _grid_t(v_ht, GRID_W, rows_g)
    sl = _ln_silu_fwd("conf_ln", v_h, v_v, ln_g, ln_b)
    mix1 = _mm("conf_pw2", sl, w_pw2, "nn", bias=b_pw2)
    h3, xf1 = _resnorm_fwd("l1_res_norm", h2, mix1, g1b, nfg[1:2], s2b, sh2b)
    u2 = _mm("ffn1_in", xf1, w_fin[1], "nn", ACT)
    hid1 = _swiglu_fwd("ffn1_act", u2)
    f1 = _mm("ffn1_out", hid1, w_fout[1], "nn")
    dh, sq, d_final_g, df1, d_g2b_ = _final_loss("final_loss", h3, f1, tgt, g2b, final_norm_g[None, :])

    zero2 = jnp.zeros((2, d), F32)

    def ffn_bwd(tag, gated, dh, hin, xf, u, hid, w_in, w_out, g_norm, s_mod, branch):
        df, dgate = gated
        dhid = _mm(tag + "_dhid", df, w_out, "nt", ACT)
        dw_out = _mm(tag + "_dwout", hid, df, "tn", WIRE)
        du = _swiglu_bwd(tag + "_act_bwd", u, dhid)
        dw_in = _mm(tag + "_dwin", xf, du, "tn", WIRE)
        dxf = _mm(tag + "_dx", du, w_in, "nt")
        s_2 = jnp.concatenate([s_mod, s_mod], axis=0)
        dh, dsh, ds, dg, nxt = _normmod_bwd(tag + "_norm_bwd", None, hin, dxf, dh, g_norm, s_2, branch=branch)
        return dh, dgate, dsh[1:2], ds[1:2], dg[1:2], dw_in, dw_out, nxt

    dh, d_g2b, d_sh2b, d_s2b, d_nfg1, g_fin1, g_fout1, (dmix1, d_g1b, g_bpw2) = ffn_bwd(
        "ffn1", (df1, d_g2b_), dh, h3, xf1, u2, hid1, w_fin[1], w_fout[1], nfg[1:2], s2b, (mix1, g1b))
    dsl = _mm("conf_dsl", dmix1, w_pw2, "nt")
    g_pw2 = _mm("conf_dwpw2", sl, dmix1, "tn", WIRE)
    dv_lo, dv_v, g_lng, g_lnb = _ln_silu_bwd("conf_ln_bwd", v_h, v_v, dsl, ln_g, ln_b)
    dv_h = _grid_t(dv_lo, rows_g, GRID_W)
    w_flip = dw_w_full[::-1]
    dgl_h = _strided_conv("conf_conv_h_bwd", dv_h, w_flip[:, :ch], None, rows_g)
    dgl_v = _strided_conv("conf_conv_v_bwd", dv_v, w_flip[:, ch:], None, GRID_W)
    g_dww_h, g_dwb_h = _strided_conv_dw("conf_conv_h_dw", gl_h, dv_h, ck, rows_g)
    g_dww_v, g_dwb_v = _strided_conv_dw("conf_conv_v_dw", gl, dv_v, ck, GRID_W, x_col0=ch)
    g_dww, g_dwb = jnp.concatenate([g_dww_h, g_dww_v], axis=1), jnp.concatenate([g_dwb_h, g_dwb_v], axis=1)
    du1, g_bpw1 = _glu_bwd("conf_glu_bwd", u1, _grid_t(dgl_h, GRID_W, rows_g), dgl_v)
    g_pw1 = _mm("conf_dwpw1", xn1, du1, "tn", WIRE)
    dxn1 = _mm("conf_dx", du1, w_pw1, "nt")
    dh, dsh_, ds_, dg_, (df0, d_g2_, _) = _normmod_bwd("l1_norm_bwd", None, h2, dxn1, dh, nmg[1:2],
                                                       jnp.concatenate([s1b, s1b], axis=0), branch=(f0, g2))
    d_sh1b, d_s1b, d_nmg1 = dsh_[1:2], ds_[1:2], dg_[1:2]
    dh, d_g2, d_sh2, d_s2, d_nfg0, g_fin0, g_fout0, (dmix0, d_g1, _) = ffn_bwd(
        "ffn0", (df0, d_g2_), dh, h1, xf0, u0, hid0, w_fin[0], w_fout[0], nfg[0:1], s2, (mix0, g1))
    dyn = _mm("ssd_dyn", dmix0, w_ssd_out, "nt")
    g_ssd_out = _mm("ssd_dwout", yn, dmix0, "tn", WIRE)
    dy, dzx, g_normw, ddsk_e = _ssd_gate_bwd("ssd_gate_bwd", dyn, y2, xbc, zx, dskip_e, ssd_norm_w, di, nct, tt)
    ride_scan_bwd = _Comm(scatter=[_shard_cols(g_fin0), _shard_cols(g_fin1), _shard_rows(g_fout0), _shard_rows(g_fout1)])
    ride_conv_bwd = _Comm(scatter=[_shard_rows(g_ssd_out), _shard_cols(g_pw1), _shard_rows(g_pw2), _shard_cols(g_bpw1),
                                   _shard_cols(g_dww), _shard_cols(g_dwb), _shard_cols(g_lng), _shard_cols(g_lnb), _shard_cols(g_bpw2)])
    dxbc2, ddt2, g_alog2, g_bias2, (_, ffn_r) = _ssd_bwd("ssd_scan_bwd", xbc, dt2, bias2, alog2, dy, hp2, dskip_e, di, tc,
                                                         comm=ride_scan_bwd)
    dzx, g_convw, g_convb, (_, conv_r) = _ssd_conv_bwd("ssd_conv_bwd", zx, dxbc2, conv_w_full, xbc_pre, dzx, di, tc,
                                                       comm=ride_conv_bwd)
    ddt_p = jnp.pad(jnp.moveaxis(ddt2, 0, 1).reshape(tt, 2 * h), ((0, 0), (0, LANES - 2 * h))).astype(MXU)
    g_ssd_in = jnp.concatenate([_mm("ssd_dw_zx", xn_all, dzx, "tn", WIRE),
                                _mm("ssd_dw_dt", xn_all, ddt_p, "tn", WIRE)[:, :2 * h]], axis=1)
    dxn, (_, (ssd_in_r, convw_r)) = _mm("ssd_dx_zx", dzx, w_ssd_in, "nt",
                                        comm=_Comm(scatter=[_shard_cols(g_ssd_in), _shard_cols(g_convw)]))
    dxn = _mm("ssd_dx_dt", ddt_p, w_dt, "nt", add=dxn)
    dh0, dsh_, ds_, dg_ = _normmod_bwd("l0_norm_bwd", ctx2, x2, dxn, dh, nmg[0:1], s01)
    grad_x = dh0[None]
    d_csh1, d_sh1, d_cs1, d_s1 = dsh_[0:1], dsh_[1:2], ds_[0:1], ds_[1:2]
    d_nmg0 = dg_[0:1] + dg_[1:2]

    z1 = jnp.zeros((1, d), F32)
    dmod = jnp.concatenate([jnp.concatenate([d_sh1, d_s1, d_g1, d_sh2, d_s2, d_g2], axis=1),
                            jnp.concatenate([d_sh1b, d_s1b, d_g1b, d_sh2b, d_s2b, d_g2b], axis=1),
                            jnp.concatenate([d_csh1, d_cs1, z1, z1, z1, z1], axis=1)], axis=0)
    out = {}

    def put(name, res):
        w = args[name]
        out["grad_" + name], out["delta_" + name], out["new_m_" + name], out["new_v_" + name] = (r.reshape(w.shape) for r in res)

    def adam_big(name, slots, comm=None):
        return _adam("adam_" + name, slots, _flat2(args[name]), _flat2(args["m_" + name]), _flat2(args["v_" + name]), comm=comm)

    res, ((dmod_g,), _) = adam_big("ffn_w_in", [ffn_r[0], ffn_r[1]], comm=_Comm(gather=[dmod]))
    put("ffn_w_in", res)
    dmod_mine = lax.dynamic_slice_in_dim(dmod_g, me * cw, cw, axis=2)
    dmod16 = jnp.stack([jnp.concatenate([dmod_mine[:, 0], dmod_mine[:, 2]], axis=0),
                        jnp.concatenate([dmod_mine[:, 1], jnp.zeros((NDEV, cw), F32)], axis=0)])
    g_ada_w, dsc_part = _ada_bwd("ada_bwd", cs_all, ada_w, dmod16)
    g_ada_b = dmod[0:2] + jnp.concatenate([dmod[2:3], jnp.zeros((1, 6 * d), F32)], axis=0)

    d_dskip = jnp.sum(ddsk_e.reshape(h, di // h), axis=1)[None, :]
    rep = [dsc_part, g_ada_b, jnp.concatenate([d_nmg0, d_nmg1], axis=0), jnp.concatenate([d_nfg0, d_nfg1], axis=0),
           d_final_g, g_convb, g_bias2[0], g_bias2[1], g_alog2[0], g_alog2[1], d_dskip, g_normw]
    res, (rep_g, _) = _adam("adam_ada_w", _flat2(g_ada_w)[None], _flat2(ada_w), _flat2(m_ada_w), _flat2(v_ada_w),
                            comm=_Comm(gather=rep + [sq]))
    put("ada_w", res)
    rep_g, sq_g = rep_g[:-1], rep_g[-1]
    loss = (0.5 / d) * jnp.sum(sq_g[:, 0, 0])
    small_r = [convw_r] + list(conv_r[3:])

    for name, slots in zip(["ssd_w_in", "ssd_w_out", "conf_w_pw1", "conf_w_pw2", "ffn_w_out"],
                           [ssd_in_r, conv_r[0], conv_r[1], conv_r[2], [ffn_r[2], ffn_r[3]]]):
        put(name, adam_big(name, slots))
    small_names = ["ssd_conv_w", "conf_b_pw1", "conf_dw_w", "conf_dw_b", "conf_ln_g", "conf_ln_b", "conf_b_pw2",
                   "c_ctx", "ada_b", "norm_mix_g", "norm_ffn_g", "final_norm_g", "ssd_conv_b", "ssd_dt_bias_f", "ssd_dt_bias_b",
                   "ssd_a_log_f", "ssd_a_log_b", "ssd_d_skip", "ssd_norm_w"]
    slots = list(small_r) + list(rep_g)

    def as2(a):
        return a.reshape((1, -1)) if a.ndim == 1 else _flat2(a)

    res = _adam_small("adam_small", slots, [as2(args[n]) for n in small_names], [as2(args["m_" + n]) for n in small_names],
                      [as2(args["v_" + n]) for n in small_names], scale=(small_names.index("c_ctx"), c_ctx[None, :]))
    for k, name in enumerate(small_names):
        put(name, [r[k] for r in res])
    return (loss, grad_x, *[out["grad_" + n] for n in names], *[out["delta_" + n] for n in names],
            *[out["new_m_" + n] for n in names], *[out["new_v_" + n] for n in names])


def _flat3(a):
    return a.reshape((a.shape[0], -1, a.shape[-1]))
```
